```python
import math
import jax, jax.numpy as jnp
from jax import lax
import numpy as np

D_MODEL = 2048
BATCH = 8
SEQ = 4096
DEPTH = 1

GRID_W = 64
WIN_H = 8
WIN_W = 16
ATTN_WIDTH = D_MODEL // 2
SSM_WIDTH = D_MODEL - ATTN_WIDTH
MIX_WIDTH = ATTN_WIDTH + SSM_WIDTH
HEAD_DIM = 64
N_HEADS = ATTN_WIDTH // HEAD_DIM
SSM_GROUP_CH = 16
SSM_GROUPS = SSM_WIDTH // SSM_GROUP_CH
SSM_STATE = 64
N_DIRS = 2
D_FF = ((8 * D_MODEL + 3 * 256 - 1) // (3 * 256)) * 256
IN_WIDTH = 3 * ATTN_WIDTH + SSM_WIDTH
RMS_EPS = 1e-6
NEG_INF = -1e30

kernel_name = "hymba_natten_s5_encoder_block"


def rmsnorm(x, g):
    xf = x.astype(jnp.float32)
    y = xf * lax.rsqrt(jnp.mean(xf * xf, axis=-1, keepdims=True) + RMS_EPS)
    return (y * g.astype(jnp.float32)).astype(x.dtype)


def neighbourhood_attention(q, k, v, q_gain, k_gain, rpb):
    b, s, h, dh = q.shape
    rows = s // GRID_W
    kh = min(WIN_H, rows)
    q = rmsnorm(q, q_gain)
    k = rmsnorm(k, k_gain)
    qg = q.reshape(b, rows, GRID_W, h, dh)
    kg = k.reshape(b, rows, GRID_W, h, dh)
    vg = v.reshape(b, rows, GRID_W, h, dh)
    r_idx = jnp.arange(rows)
    row_start = jnp.clip(r_idx - kh // 2, 0, rows - kh)
    key_rows = row_start[:, None] + jnp.arange(kh)[None, :]
    k_blk = kg[:, key_rows]
    v_blk = vg[:, key_rows]
    c_idx = jnp.arange(GRID_W)
    col_start = jnp.clip(c_idx - WIN_W // 2, 0, GRID_W - WIN_W)
    col_in = (c_idx[None, :] >= col_start[:, None]) & (c_idx[None, :] < col_start[:, None] + WIN_W)
    dr_idx = key_rows - r_idx[:, None] + (WIN_H - 1)
    dc_idx = jnp.clip(c_idx[None, :] - c_idx[:, None], -(WIN_W - 1), WIN_W - 1) + (WIN_W - 1)
    bias = rpb[:, dr_idx[:, None, :, None], dc_idx[None, :, None, :]].astype(jnp.float32)
    scale = 1.0 / math.sqrt(dh)
    scores = jnp.einsum('brqhd,brikhd->bhrqik', qg, k_blk).astype(jnp.float32) * scale
    scores = jnp.where(col_in[None, None, None, :, None, :], scores + bias[None], NEG_INF)
    p = jax.nn.softmax(scores.reshape(b, h, rows, GRID_W, kh * GRID_W), axis=-1)
    p = p.reshape(b, h, rows, GRID_W, kh, GRID_W).astype(v.dtype)
    out = jnp.einsum('bhrqik,brikhd->brqhd', p, v_blk)
    return out.reshape(b, s, h * dh)


def _linear_recurrence(e1, e2):
    a1, b1 = e1
    a2, b2 = e2
    return a2 * a1, a2 * b1 + b2


def s5_bidirectional(u, a_re, a_im, b_re, b_im, c_re, c_im, log_step, d_skip, w_glu, b_glu):
    b, s, _ = u.shape
    ug = u.astype(jnp.float32).reshape(b, s, SSM_GROUPS, SSM_GROUP_CH)
    uc = lax.complex(ug, jnp.zeros_like(ug))
    y = jnp.zeros_like(ug)
    for d in range(N_DIRS):
        lam = lax.complex(jnp.minimum(a_re[d].astype(jnp.float32), -1e-4), a_im[d].astype(jnp.float32))
        dt = jnp.exp(log_step[d].astype(jnp.float32))[:, None]
        lam_bar = jnp.exp(lam * dt)
        b_mat = lax.complex(b_re[d].astype(jnp.float32), b_im[d].astype(jnp.float32))
        b_bar = ((lam_bar - 1.0) / lam)[:, :, None] * b_mat
        c_mat = lax.complex(c_re[d].astype(jnp.float32), c_im[d].astype(jnp.float32))
        bu = jnp.einsum('bsgc,gpc->bsgp', uc, b_bar)
        a_seq = jnp.broadcast_to(lam_bar[None, None], (1, s, SSM_GROUPS, SSM_STATE))
        _, states = lax.associative_scan(_linear_recurrence, (a_seq, bu), reverse=(d == 1), axis=1)
        y = y + jnp.real(jnp.einsum('bsgp,gcp->bsgc', states, c_mat))
    y = y.reshape(b, s, SSM_WIDTH) + d_skip.astype(jnp.float32) * ug.reshape(b, s, SSM_WIDTH)
    y = jax.nn.gelu(y.astype(u.dtype))
    return y * jax.nn.sigmoid(y @ w_glu + b_glu)


def _fwd_setup_inputs(seed: int = 0) -> dict:
    key = jax.random.key(seed)
    ks = jax.random.split(key, 24)
    f32 = jnp.float32
    L, G, P, C = DEPTH, SSM_GROUPS, SSM_STATE, SSM_GROUP_CH
    nrm = lambda k, shp, sc: jax.random.normal(k, shp, f32) * sc
    a_im_base = jnp.pi * jnp.arange(P, dtype=f32)
    return {
        "x": jax.random.normal(ks[0], (BATCH, SEQ, D_MODEL), f32),
        "g_mix": 1.0 + nrm(ks[1], (L, D_MODEL), 0.05),
        "w_in": nrm(ks[2], (L, D_MODEL, IN_WIDTH), D_MODEL ** -0.5),
        "q_gain": 1.0 + nrm(ks[3], (L, HEAD_DIM), 0.05),
        "k_gain": 1.0 + nrm(ks[4], (L, HEAD_DIM), 0.05),
        "rpb": nrm(ks[5], (L, N_HEADS, 2 * WIN_H - 1, 2 * WIN_W - 1), 0.5),
        "ssm_a_re": -0.5 + nrm(ks[6], (L, N_DIRS, G, P), 0.01),
        "ssm_a_im": a_im_base + nrm(ks[7], (L, N_DIRS, G, P), 0.01),
        "ssm_b_re": nrm(ks[8], (L, N_DIRS, G, P, C), (2 * C) ** -0.5),
        "ssm_b_im": nrm(ks[9], (L, N_DIRS, G, P, C), (2 * C) ** -0.5),
        "ssm_c_re": nrm(ks[10], (L, N_DIRS, G, C, P), (2 * P) ** -0.5),
        "ssm_c_im": nrm(ks[11], (L, N_DIRS, G, C, P), (2 * P) ** -0.5),
        "ssm_log_step": jax.random.uniform(ks[12], (L, N_DIRS, G), f32, math.log(1e-3), math.log(1e-1)),
        "ssm_d": nrm(ks[13], (L, SSM_WIDTH), 1.0),
        "w_glu": nrm(ks[14], (L, SSM_WIDTH, SSM_WIDTH), SSM_WIDTH ** -0.5),
        "b_glu": nrm(ks[15], (L, SSM_WIDTH), 0.02),
        "g_out_attn": 1.0 + nrm(ks[16], (L, ATTN_WIDTH), 0.05),
        "g_out_ssm": 1.0 + nrm(ks[17], (L, SSM_WIDTH), 0.05),
        "w_out": nrm(ks[18], (L, MIX_WIDTH, D_MODEL), MIX_WIDTH ** -0.5),
        "g_ffn": 1.0 + nrm(ks[19], (L, D_MODEL), 0.05),
        "w_ffn_gate": nrm(ks[20], (L, D_MODEL, D_FF), D_MODEL ** -0.5),
        "w_ffn_up": nrm(ks[21], (L, D_MODEL, D_FF), D_MODEL ** -0.5),
        "w_ffn_down": nrm(ks[22], (L, D_FF, D_MODEL), D_FF ** -0.5),
    }


def _fwd_reference(x, g_mix, w_in, q_gain, k_gain, rpb, ssm_a_re, ssm_a_im, ssm_b_re, ssm_b_im,
              ssm_c_re, ssm_c_im, ssm_log_step, ssm_d, w_glu, b_glu, g_out_attn, g_out_ssm,
              w_out, g_ffn, w_ffn_gate, w_ffn_up, w_ffn_down):
    b, s, _ = x.shape
    for l in range(DEPTH):
        h = rmsnorm(x, g_mix[l])
        z = h @ w_in[l]
        q, k, v, u = jnp.split(z, [ATTN_WIDTH, 2 * ATTN_WIDTH, 3 * ATTN_WIDTH], axis=-1)
        q = q.reshape(b, s, N_HEADS, HEAD_DIM)
        k = k.reshape(b, s, N_HEADS, HEAD_DIM)
        v = v.reshape(b, s, N_HEADS, HEAD_DIM)
        ya = neighbourhood_attention(q, k, v, q_gain[l], k_gain[l], rpb[l])
        ys = s5_bidirectional(u, ssm_a_re[l], ssm_a_im[l], ssm_b_re[l], ssm_b_im[l],
                              ssm_c_re[l], ssm_c_im[l], ssm_log_step[l], ssm_d[l],
                              w_glu[l], b_glu[l])
        y = jnp.concatenate([rmsnorm(ya, g_out_attn[l]), rmsnorm(ys, g_out_ssm[l])], axis=-1)
        x = x + y @ w_out[l]
        h = rmsnorm(x, g_ffn[l])
        x = x + (jax.nn.silu(h @ w_ffn_gate[l]) * (h @ w_ffn_up[l])) @ w_ffn_down[l]
    return x


import jax as _jax
import jax.numpy as _jnp

TWIN_FORMAT = 'train_step'
FWD_PARAMS = ['x', 'g_mix', 'w_in', 'q_gain', 'k_gain', 'rpb', 'ssm_a_re', 'ssm_a_im', 'ssm_b_re', 'ssm_b_im', 'ssm_c_re', 'ssm_c_im', 'ssm_log_step', 'ssm_d', 'w_glu', 'b_glu', 'g_out_attn', 'g_out_ssm', 'w_out', 'g_ffn', 'w_ffn_gate', 'w_ffn_up', 'w_ffn_down']
TWIN_WEIGHTS = ['g_mix', 'w_in', 'q_gain', 'k_gain', 'rpb', 'ssm_a_re', 'ssm_a_im', 'ssm_b_re', 'ssm_b_im', 'ssm_c_re', 'ssm_c_im', 'ssm_log_step', 'ssm_d', 'w_glu', 'b_glu', 'g_out_attn', 'g_out_ssm', 'w_out', 'g_ffn', 'w_ffn_gate', 'w_ffn_up', 'w_ffn_down']
TWIN_DIFF_INPUT = 'x'
TWIN_INPUTS = ['x', 'g_mix', 'w_in', 'q_gain', 'k_gain', 'rpb', 'ssm_a_re', 'ssm_a_im', 'ssm_b_re', 'ssm_b_im', 'ssm_c_re', 'ssm_c_im', 'ssm_log_step', 'ssm_d', 'w_glu', 'b_glu', 'g_out_attn', 'g_out_ssm', 'w_out', 'g_ffn', 'w_ffn_gate', 'w_ffn_up', 'w_ffn_down', 'loss_target', 'm_g_mix', 'm_w_in', 'm_q_gain', 'm_k_gain', 'm_rpb', 'm_ssm_a_re', 'm_ssm_a_im', 'm_ssm_b_re', 'm_ssm_b_im', 'm_ssm_c_re', 'm_ssm_c_im', 'm_ssm_log_step', 'm_ssm_d', 'm_w_glu', 'm_b_glu', 'm_g_out_attn', 'm_g_out_ssm', 'm_w_out', 'm_g_ffn', 'm_w_ffn_gate', 'm_w_ffn_up', 'm_w_ffn_down', 'v_g_mix', 'v_w_in', 'v_q_gain', 'v_k_gain', 'v_rpb', 'v_ssm_a_re', 'v_ssm_a_im', 'v_ssm_b_re', 'v_ssm_b_im', 'v_ssm_c_re', 'v_ssm_c_im', 'v_ssm_log_step', 'v_ssm_d', 'v_w_glu', 'v_b_glu', 'v_g_out_attn', 'v_g_out_ssm', 'v_w_out', 'v_g_ffn', 'v_w_ffn_gate', 'v_w_ffn_up', 'v_w_ffn_down']
TWIN_OUTPUTS = ['loss', 'grad_x', 'grad_g_mix', 'grad_w_in', 'grad_q_gain', 'grad_k_gain', 'grad_rpb', 'grad_ssm_a_re', 'grad_ssm_a_im', 'grad_ssm_b_re', 'grad_ssm_b_im', 'grad_ssm_c_re', 'grad_ssm_c_im', 'grad_ssm_log_step', 'grad_ssm_d', 'grad_w_glu', 'grad_b_glu', 'grad_g_out_attn', 'grad_g_out_ssm', 'grad_w_out', 'grad_g_ffn', 'grad_w_ffn_gate', 'grad_w_ffn_up', 'grad_w_ffn_down', 'delta_g_mix', 'delta_w_in', 'delta_q_gain', 'delta_k_gain', 'delta_rpb', 'delta_ssm_a_re', 'delta_ssm_a_im', 'delta_ssm_b_re', 'delta_ssm_b_im', 'delta_ssm_c_re', 'delta_ssm_c_im', 'delta_ssm_log_step', 'delta_ssm_d', 'delta_w_glu', 'delta_b_glu', 'delta_g_out_attn', 'delta_g_out_ssm', 'delta_w_out', 'delta_g_ffn', 'delta_w_ffn_gate', 'delta_w_ffn_up', 'delta_w_ffn_down', 'new_m_g_mix', 'new_m_w_in', 'new_m_q_gain', 'new_m_k_gain', 'new_m_rpb', 'new_m_ssm_a_re', 'new_m_ssm_a_im', 'new_m_ssm_b_re', 'new_m_ssm_b_im', 'new_m_ssm_c_re', 'new_m_ssm_c_im', 'new_m_ssm_log_step', 'new_m_ssm_d', 'new_m_w_glu', 'new_m_b_glu', 'new_m_g_out_attn', 'new_m_g_out_ssm', 'new_m_w_out', 'new_m_g_ffn', 'new_m_w_ffn_gate', 'new_m_w_ffn_up', 'new_m_w_ffn_down', 'new_v_g_mix', 'new_v_w_in', 'new_v_q_gain', 'new_v_k_gain', 'new_v_rpb', 'new_v_ssm_a_re', 'new_v_ssm_a_im', 'new_v_ssm_b_re', 'new_v_ssm_b_im', 'new_v_ssm_c_re', 'new_v_ssm_c_im', 'new_v_ssm_log_step', 'new_v_ssm_d', 'new_v_w_glu', 'new_v_b_glu', 'new_v_g_out_attn', 'new_v_g_out_ssm', 'new_v_w_out', 'new_v_g_ffn', 'new_v_w_ffn_gate', 'new_v_w_ffn_up', 'new_v_w_ffn_down']
TWIN_LEAF_KINDS = {'loss': 'loss', 'grad_x': 'grad_x', 'grad_g_mix': 'grad_w', 'grad_w_in': 'grad_w', 'grad_q_gain': 'grad_w', 'grad_k_gain': 'grad_w', 'grad_rpb': 'grad_w', 'grad_ssm_a_re': 'grad_w', 'grad_ssm_a_im': 'grad_w', 'grad_ssm_b_re': 'grad_w', 'grad_ssm_b_im': 'grad_w', 'grad_ssm_c_re': 'grad_w', 'grad_ssm_c_im': 'grad_w', 'grad_ssm_log_step': 'grad_w', 'grad_ssm_d': 'grad_w', 'grad_w_glu': 'grad_w', 'grad_b_glu': 'grad_w', 'grad_g_out_attn': 'grad_w', 'grad_g_out_ssm': 'grad_w', 'grad_w_out': 'grad_w', 'grad_g_ffn': 'grad_w', 'grad_w_ffn_gate': 'grad_w', 'grad_w_ffn_up': 'grad_w', 'grad_w_ffn_down': 'grad_w', 'delta_g_mix': 'delta_w', 'delta_w_in': 'delta_w', 'delta_q_gain': 'delta_w', 'delta_k_gain': 'delta_w', 'delta_rpb': 'delta_w', 'delta_ssm_a_re': 'delta_w', 'delta_ssm_a_im': 'delta_w', 'delta_ssm_b_re': 'delta_w', 'delta_ssm_b_im': 'delta_w', 'delta_ssm_c_re': 'delta_w', 'delta_ssm_c_im': 'delta_w', 'delta_ssm_log_step': 'delta_w', 'delta_ssm_d': 'delta_w', 'delta_w_glu': 'delta_w', 'delta_b_glu': 'delta_w', 'delta_g_out_attn': 'delta_w', 'delta_g_out_ssm': 'delta_w', 'delta_w_out': 'delta_w', 'delta_g_ffn': 'delta_w', 'delta_w_ffn_gate': 'delta_w', 'delta_w_ffn_up': 'delta_w', 'delta_w_ffn_down': 'delta_w', 'new_m_g_mix': 'new_m', 'new_m_w_in': 'new_m', 'new_m_q_gain': 'new_m', 'new_m_k_gain': 'new_m', 'new_m_rpb': 'new_m', 'new_m_ssm_a_re': 'new_m', 'new_m_ssm_a_im': 'new_m', 'new_m_ssm_b_re': 'new_m', 'new_m_ssm_b_im': 'new_m', 'new_m_ssm_c_re': 'new_m', 'new_m_ssm_c_im': 'new_m', 'new_m_ssm_log_step': 'new_m', 'new_m_ssm_d': 'new_m', 'new_m_w_glu': 'new_m', 'new_m_b_glu': 'new_m', 'new_m_g_out_attn': 'new_m', 'new_m_g_out_ssm': 'new_m', 'new_m_w_out': 'new_m', 'new_m_g_ffn': 'new_m', 'new_m_w_ffn_gate': 'new_m', 'new_m_w_ffn_up': 'new_m', 'new_m_w_ffn_down': 'new_m', 'new_v_g_mix': 'new_v', 'new_v_w_in': 'new_v', 'new_v_q_gain': 'new_v', 'new_v_k_gain': 'new_v', 'new_v_rpb': 'new_v', 'new_v_ssm_a_re': 'new_v', 'new_v_ssm_a_im': 'new_v', 'new_v_ssm_b_re': 'new_v', 'new_v_ssm_b_im': 'new_v', 'new_v_ssm_c_re': 'new_v', 'new_v_ssm_c_im': 'new_v', 'new_v_ssm_log_step': 'new_v', 'new_v_ssm_d': 'new_v', 'new_v_w_glu': 'new_v', 'new_v_b_glu': 'new_v', 'new_v_g_out_attn': 'new_v', 'new_v_g_out_ssm': 'new_v', 'new_v_w_out': 'new_v', 'new_v_g_ffn': 'new_v', 'new_v_w_ffn_gate': 'new_v', 'new_v_w_ffn_up': 'new_v', 'new_v_w_ffn_down': 'new_v'}


def _forward(args):
    return _fwd_reference(*[args[k] for k in FWD_PARAMS])


def _output_shape():
    def fwd():
        inp = _fwd_setup_inputs(0)
        return _fwd_reference(*[inp[k] for k in FWD_PARAMS])
    out = _jax.eval_shape(fwd)
    return out.shape, out.dtype

N_MICROBATCH = 1
ADAM_LR = 0.001
ADAM_B1 = 0.9
ADAM_B2 = 0.999
ADAM_EPS = 1e-08
ADAM_WD = 0.01
ADAM_STEP = 10
PER_EXAMPLE_BATCH_AXIS = {'x': 0, 'loss_target': 0}
SHARED_INPUTS = []
_WEIGHT_DTYPES = {'g_mix': _jnp.float32, 'w_in': _jnp.float32, 'q_gain': _jnp.float32, 'k_gain': _jnp.float32, 'rpb': _jnp.float32, 'ssm_a_re': _jnp.float32, 'ssm_a_im': _jnp.float32, 'ssm_b_re': _jnp.float32, 'ssm_b_im': _jnp.float32, 'ssm_c_re': _jnp.float32, 'ssm_c_im': _jnp.float32, 'ssm_log_step': _jnp.float32, 'ssm_d': _jnp.float32, 'w_glu': _jnp.float32, 'b_glu': _jnp.float32, 'g_out_attn': _jnp.float32, 'g_out_ssm': _jnp.float32, 'w_out': _jnp.float32, 'g_ffn': _jnp.float32, 'w_ffn_gate': _jnp.float32, 'w_ffn_up': _jnp.float32, 'w_ffn_down': _jnp.float32}
MOMENT_SCALE = {'g_mix': 4.542692e-01, 'w_in': 2.994272e-01, 'q_gain': 1.151681e+00, 'k_gain': 1.100683e+00, 'rpb': 7.970129e-02, 'ssm_a_re': 1.101914e-02, 'ssm_a_im': 1.049628e-02, 'ssm_b_re': 8.433018e-03, 'ssm_b_im': 8.469043e-03, 'ssm_c_re': 1.662058e-02, 'ssm_c_im': 1.676673e-02, 'ssm_log_step': 6.943684e+00, 'ssm_d': 3.199548e+00, 'w_glu': 4.907292e-01, 'b_glu': 1.484854e+00, 'g_out_attn': 1.598433e+01, 'g_out_ssm': 2.894271e+01, 'w_out': 1.919806e+00, 'g_ffn': 1.256931e+01, 'w_ffn_gate': 3.842876e-01, 'w_ffn_up': 2.173015e-01, 'w_ffn_down': 3.034643e-01}


def _to_microbatches(a, axis):
    t = _jnp.moveaxis(a, axis, 0)
    t = t.reshape((N_MICROBATCH, t.shape[0] // N_MICROBATCH) + t.shape[1:])
    return _jnp.moveaxis(t, 1, axis + 1)


def setup_inputs(seed: int = 0) -> dict:
    inp = _fwd_setup_inputs(seed)
    key = _jax.random.fold_in(_jax.random.key(seed), 7919)
    shape, _ = _output_shape()
    out = dict(inp)
    out["loss_target"] = _jax.random.normal(_jax.random.fold_in(key, 0), shape, _jnp.float32)
    for i, name in enumerate(TWIN_WEIGHTS):
        w = inp[name].astype(_jnp.float32)
        if MOMENT_SCALE is None:
            s = _jnp.sqrt(_jnp.mean(_jnp.square(w)) + 1e-30)
        else:
            s = MOMENT_SCALE[name]
        km, kv = _jax.random.split(_jax.random.fold_in(key, i + 1))
        out[name] = w
        out["m_" + name] = s * _jax.random.normal(km, w.shape, _jnp.float32)
        out["v_" + name] = (s * s) * _jax.random.uniform(kv, w.shape, _jnp.float32, 0.5, 1.5)
    if N_MICROBATCH > 1:
        for name, axis in PER_EXAMPLE_BATCH_AXIS.items():
            out[name] = _to_microbatches(out[name], axis)
    return {'x': out['x'], 'g_mix': out['g_mix'], 'w_in': out['w_in'], 'q_gain': out['q_gain'], 'k_gain': out['k_gain'], 'rpb': out['rpb'], 'ssm_a_re': out['ssm_a_re'], 'ssm_a_im': out['ssm_a_im'], 'ssm_b_re': out['ssm_b_re'], 'ssm_b_im': out['ssm_b_im'], 'ssm_c_re': out['ssm_c_re'], 'ssm_c_im': out['ssm_c_im'], 'ssm_log_step': out['ssm_log_step'], 'ssm_d': out['ssm_d'], 'w_glu': out['w_glu'], 'b_glu': out['b_glu'], 'g_out_attn': out['g_out_attn'], 'g_out_ssm': out['g_out_ssm'], 'w_out': out['w_out'], 'g_ffn': out['g_ffn'], 'w_ffn_gate': out['w_ffn_gate'], 'w_ffn_up': out['w_ffn_up'], 'w_ffn_down': out['w_ffn_down'], 'loss_target': out['loss_target'], 'm_g_mix': out['m_g_mix'], 'm_w_in': out['m_w_in'], 'm_q_gain': out['m_q_gain'], 'm_k_gain': out['m_k_gain'], 'm_rpb': out['m_rpb'], 'm_ssm_a_re': out['m_ssm_a_re'], 'm_ssm_a_im': out['m_ssm_a_im'], 'm_ssm_b_re': out['m_ssm_b_re'], 'm_ssm_b_im': out['m_ssm_b_im'], 'm_ssm_c_re': out['m_ssm_c_re'], 'm_ssm_c_im': out['m_ssm_c_im'], 'm_ssm_log_step': out['m_ssm_log_step'], 'm_ssm_d': out['m_ssm_d'], 'm_w_glu': out['m_w_glu'], 'm_b_glu': out['m_b_glu'], 'm_g_out_attn': out['m_g_out_attn'], 'm_g_out_ssm': out['m_g_out_ssm'], 'm_w_out': out['m_w_out'], 'm_g_ffn': out['m_g_ffn'], 'm_w_ffn_gate': out['m_w_ffn_gate'], 'm_w_ffn_up': out['m_w_ffn_up'], 'm_w_ffn_down': out['m_w_ffn_down'], 'v_g_mix': out['v_g_mix'], 'v_w_in': out['v_w_in'], 'v_q_gain': out['v_q_gain'], 'v_k_gain': out['v_k_gain'], 'v_rpb': out['v_rpb'], 'v_ssm_a_re': out['v_ssm_a_re'], 'v_ssm_a_im': out['v_ssm_a_im'], 'v_ssm_b_re': out['v_ssm_b_re'], 'v_ssm_b_im': out['v_ssm_b_im'], 'v_ssm_c_re': out['v_ssm_c_re'], 'v_ssm_c_im': out['v_ssm_c_im'], 'v_ssm_log_step': out['v_ssm_log_step'], 'v_ssm_d': out['v_ssm_d'], 'v_w_glu': out['v_w_glu'], 'v_b_glu': out['v_b_glu'], 'v_g_out_attn': out['v_g_out_attn'], 'v_g_out_ssm': out['v_g_out_ssm'], 'v_w_out': out['v_w_out'], 'v_g_ffn': out['v_g_ffn'], 'v_w_ffn_gate': out['v_w_ffn_gate'], 'v_w_ffn_up': out['v_w_ffn_up'], 'v_w_ffn_down': out['v_w_ffn_down']}


def _loss(weights, diff, rest, loss_target):
    with _jax.named_scope("forward"):
        args = {**rest, TWIN_DIFF_INPUT: diff, **{k: w.astype(_WEIGHT_DTYPES[k]) for k, w in weights.items()}}
        y = _forward(args)
    with _jax.named_scope("loss_head"):
        err = _jnp.square(y.astype(_jnp.float32) - loss_target)
        return 0.5 * _jnp.sum(_jnp.mean(err, axis=-1)) if err.ndim else 0.5 * err


def _adamw(w, g, m, v):
    m = ADAM_B1 * m + (1.0 - ADAM_B1) * g
    v = ADAM_B2 * v + (1.0 - ADAM_B2) * _jnp.square(g)
    m_hat = m / (1.0 - ADAM_B1 ** ADAM_STEP)
    v_hat = v / (1.0 - ADAM_B2 ** ADAM_STEP)
    delta = -ADAM_LR * (m_hat / (_jnp.sqrt(v_hat) + ADAM_EPS) + ADAM_WD * w)
    return delta, m, v


def reference(x, g_mix, w_in, q_gain, k_gain, rpb, ssm_a_re, ssm_a_im, ssm_b_re, ssm_b_im, ssm_c_re, ssm_c_im, ssm_log_step, ssm_d, w_glu, b_glu, g_out_attn, g_out_ssm, w_out, g_ffn, w_ffn_gate, w_ffn_up, w_ffn_down, loss_target, m_g_mix, m_w_in, m_q_gain, m_k_gain, m_rpb, m_ssm_a_re, m_ssm_a_im, m_ssm_b_re, m_ssm_b_im, m_ssm_c_re, m_ssm_c_im, m_ssm_log_step, m_ssm_d, m_w_glu, m_b_glu, m_g_out_attn, m_g_out_ssm, m_w_out, m_g_ffn, m_w_ffn_gate, m_w_ffn_up, m_w_ffn_down, v_g_mix, v_w_in, v_q_gain, v_k_gain, v_rpb, v_ssm_a_re, v_ssm_a_im, v_ssm_b_re, v_ssm_b_im, v_ssm_c_re, v_ssm_c_im, v_ssm_log_step, v_ssm_d, v_w_glu, v_b_glu, v_g_out_attn, v_g_out_ssm, v_w_out, v_g_ffn, v_w_ffn_gate, v_w_ffn_up, v_w_ffn_down):
    given = dict(x=x, g_mix=g_mix, w_in=w_in, q_gain=q_gain, k_gain=k_gain, rpb=rpb, ssm_a_re=ssm_a_re, ssm_a_im=ssm_a_im, ssm_b_re=ssm_b_re, ssm_b_im=ssm_b_im, ssm_c_re=ssm_c_re, ssm_c_im=ssm_c_im, ssm_log_step=ssm_log_step, ssm_d=ssm_d, w_glu=w_glu, b_glu=b_glu, g_out_attn=g_out_attn, g_out_ssm=g_out_ssm, w_out=w_out, g_ffn=g_ffn, w_ffn_gate=w_ffn_gate, w_ffn_up=w_ffn_up, w_ffn_down=w_ffn_down, loss_target=loss_target, m_g_mix=m_g_mix, m_w_in=m_w_in, m_q_gain=m_q_gain, m_k_gain=m_k_gain, m_rpb=m_rpb, m_ssm_a_re=m_ssm_a_re, m_ssm_a_im=m_ssm_a_im, m_ssm_b_re=m_ssm_b_re, m_ssm_b_im=m_ssm_b_im, m_ssm_c_re=m_ssm_c_re, m_ssm_c_im=m_ssm_c_im, m_ssm_log_step=m_ssm_log_step, m_ssm_d=m_ssm_d, m_w_glu=m_w_glu, m_b_glu=m_b_glu, m_g_out_attn=m_g_out_attn, m_g_out_ssm=m_g_out_ssm, m_w_out=m_w_out, m_g_ffn=m_g_ffn, m_w_ffn_gate=m_w_ffn_gate, m_w_ffn_up=m_w_ffn_up, m_w_ffn_down=m_w_ffn_down, v_g_mix=v_g_mix, v_w_in=v_w_in, v_q_gain=v_q_gain, v_k_gain=v_k_gain, v_rpb=v_rpb, v_ssm_a_re=v_ssm_a_re, v_ssm_a_im=v_ssm_a_im, v_ssm_b_re=v_ssm_b_re, v_ssm_b_im=v_ssm_b_im, v_ssm_c_re=v_ssm_c_re, v_ssm_c_im=v_ssm_c_im, v_ssm_log_step=v_ssm_log_step, v_ssm_d=v_ssm_d, v_w_glu=v_w_glu, v_b_glu=v_b_glu, v_g_out_attn=v_g_out_attn, v_g_out_ssm=v_g_out_ssm, v_w_out=v_w_out, v_g_ffn=v_g_ffn, v_w_ffn_gate=v_w_ffn_gate, v_w_ffn_up=v_w_ffn_up, v_w_ffn_down=v_w_ffn_down)
    weights = {n: given[n] for n in TWIN_WEIGHTS}
    shared = {n: given[n] for n in SHARED_INPUTS}
    per_example = {n: given[n] for n in ['x']}
    grad_fn = _jax.value_and_grad(_loss, argnums=(0, 1))

    def one_microbatch(ex, loss_target):
        ex = dict(ex)
        diff = ex.pop(TWIN_DIFF_INPUT)
        return grad_fn(weights, diff, {**shared, **ex}, loss_target)

    if N_MICROBATCH == 1:
        loss, (grad_w, grad_x) = one_microbatch(per_example, given["loss_target"])
    else:
        def body(carry, xs):
            loss_sum, grad_sum = carry
            l_k, (gw_k, gx_k) = one_microbatch(xs[0], xs[1])
            with _jax.named_scope("update"):
                return (loss_sum + l_k, _jax.tree.map(_jnp.add, grad_sum, gw_k)), gx_k

        init = (_jnp.zeros((), _jnp.float32), _jax.tree.map(_jnp.zeros_like, weights))
        (loss, grad_w), grad_x = _jax.lax.scan(body, init, (per_example, given["loss_target"]))
    with _jax.named_scope("update"):
        delta_w, new_m, new_v = {}, {}, {}
        for n in TWIN_WEIGHTS:
            delta_w[n], new_m[n], new_v[n] = _adamw(weights[n], grad_w[n], given["m_" + n], given["v_" + n])
    return (loss, grad_x, *[grad_w[n] for n in TWIN_WEIGHTS], *[delta_w[n] for n in TWIN_WEIGHTS],
            *[new_m[n] for n in TWIN_WEIGHTS], *[new_v[n] for n in TWIN_WEIGHTS])
```

```python
import functools
import math

import jax
import jax.numpy as jnp
from jax import lax
from jax.experimental import pallas as pl
from jax.experimental.pallas import tpu as pltpu

F32 = jnp.float32
BF16 = jnp.bfloat16

N_DEV = 8
GRID_W = 64
WIN_H = 8
WIN_W = 16
HEAD_DIM = 64
HEADS_PER_GROUP = 4
GROUP_LANES = HEADS_PER_GROUP * HEAD_DIM
SSM_C = 16
SSM_P = 64
GROUPS_PER_TILE = 8
U_TILE = GROUPS_PER_TILE * SSM_C
ST_TILE = GROUPS_PER_TILE * SSM_P
SUBLANES = 8
RMS_EPS = 1e-6
NEG_INF = -1e30
A_RE_MAX = -1e-4
ADAM_LR, ADAM_B1, ADAM_B2, ADAM_EPS, ADAM_WD, ADAM_STEP = 0.001, 0.9, 0.999, 1e-08, 0.01, 10
VMEM_LIMIT_V7X = 56 * 1024 * 1024
MESH = pl.DeviceIdType.MESH

_NN = (((1,), (0,)), ((), ()))
_NT = (((1,), (1,)), ((), ()))
_TN = (((0,), (0,)), ((), ()))
_DIMS = {"nn": _NN, "nt": _NT, "tn": _TN}


def _params(**kw):
    return pltpu.CompilerParams(vmem_limit_bytes=VMEM_LIMIT_V7X, **kw)


def _dot(a, b, dims=_NN):
    return lax.dot_general(a, b, dims, preferred_element_type=F32)


def _mm(a, b, *, name, grid, a_spec, b_spec, o_spec, o_shape, dims, k_axis=None, res=None, out_dtype=F32):
    dn = _DIMS[dims]
    nk = 1 if k_axis is None else grid[k_axis]
    acc_shape = tuple(d for d in o_spec.block_shape if d is not None)

    def body(*refs):
        if res is not None:
            a_ref, b_ref, r_ref, o_ref, acc = refs
        else:
            a_ref, b_ref, o_ref, acc = refs
            r_ref = None
        p = _dot(a_ref[...].astype(BF16), b_ref[...].astype(BF16), dn)

        def finish(v):
            if r_ref is not None:
                v = v + r_ref[...].astype(F32)
            o_ref[...] = v.astype(out_dtype)

        if nk == 1:
            finish(p)
        else:
            k = pl.program_id(k_axis)

            @pl.when(k == 0)
            def _():
                acc[...] = p

            @pl.when(k > 0)
            def _():
                acc[...] += p

            @pl.when(k == nk - 1)
            def _():
                finish(acc[...])

    ins = [a, b] + ([res] if res is not None else [])
    in_specs = [a_spec, b_spec] + ([o_spec] if res is not None else [])
    return pl.pallas_call(
        body, name=name, grid=grid, in_specs=in_specs, out_specs=o_spec,
        out_shape=jax.ShapeDtypeStruct(o_shape, out_dtype),
        scratch_shapes=[pltpu.VMEM(acc_shape if nk > 1 else (SUBLANES, 128), F32)],
        compiler_params=_params(),
    )(*ins)


def _tile(n, want):
    if n <= want:
        return n
    t = want
    while n % t:
        t //= 2
    return t


def _mm_plain(a, b, dims, *, name, res=None, out_dtype=F32, tm=512, tn=512, tk=512):
    if dims == "nn":
        (m, k), n = a.shape, b.shape[1]
    elif dims == "nt":
        (m, k), n = a.shape, b.shape[0]
    else:
        (k, m), n = a.shape, b.shape[1]
    tm, tn = _tile(m, tm), _tile(n, tn)
    if dims == "tn":
        tk = _tile(k, tk)
        grid = (m // tm, n // tn, k // tk)
        a_spec = pl.BlockSpec((tk, tm), lambda i, j, kk: (kk, i))
        b_spec = pl.BlockSpec((tk, tn), lambda i, j, kk: (kk, j))
        o_spec = pl.BlockSpec((tm, tn), lambda i, j, kk: (i, j))
        return _mm(a, b, name=name, grid=grid, a_spec=a_spec, b_spec=b_spec, o_spec=o_spec, o_shape=(m, n),
                   dims=dims, k_axis=2, res=res, out_dtype=out_dtype)
    grid = (n // tn, m // tm)
    a_spec = pl.BlockSpec((tm, k), lambda j, i: (i, 0))
    if dims == "nn":
        b_spec = pl.BlockSpec((k, tn), lambda j, i: (0, j))
    else:
        b_spec = pl.BlockSpec((tn, k), lambda j, i: (j, 0))
    o_spec = pl.BlockSpec((tm, tn), lambda j, i: (i, j))
    return _mm(a, b, name=name, grid=grid, a_spec=a_spec, b_spec=b_spec, o_spec=o_spec, o_shape=(m, n), dims=dims,
               res=res, out_dtype=out_dtype)


def _rowwise(fn, tiled, bcast, outs, accs=(), *, name, tm=256):
    m = tiled[0].shape[0]
    tm = _tile(m, tm)
    n_t, n_b, n_o, n_a = len(tiled), len(bcast), len(outs), len(accs)

    def body(*refs):
        ins = [r[...] for r in refs[: n_t + n_b]]
        o_refs = refs[n_t + n_b: n_t + n_b + n_o]
        a_refs = refs[n_t + n_b + n_o:]
        res = fn(*ins)
        if not isinstance(res, (tuple, list)):
            res = (res,)
        for r, v in zip(o_refs, res[:n_o]):
            r[...] = v.astype(r.dtype)
        first = pl.program_id(0) == 0
        for r, v in zip(a_refs, res[n_o:]):
            s = jnp.sum(v, axis=0, keepdims=True)

            @pl.when(first)
            def _():
                r[...] = s

            @pl.when(jnp.logical_not(first))
            def _():
                r[...] += s

    in_specs = [pl.BlockSpec((tm, t.shape[1]), lambda i: (i, 0)) for t in tiled]
    in_specs += [pl.BlockSpec(b.shape, lambda i, nd=b.ndim: (0,) * nd) for b in bcast]
    out_specs = [pl.BlockSpec((tm, n), lambda i: (i, 0)) for n, _ in outs]
    out_specs += [pl.BlockSpec((1, n), lambda i: (0, 0)) for n in accs]
    out_shape = [jax.ShapeDtypeStruct((m, n), dt) for n, dt in outs]
    out_shape += [jax.ShapeDtypeStruct((1, n), F32) for n in accs]
    res = pl.pallas_call(body, name=name, grid=(m // tm,), in_specs=in_specs, out_specs=out_specs,
                         out_shape=out_shape, compiler_params=_params())(*tiled, *bcast)
    return res


def _rstd(x):
    return lax.rsqrt(jnp.mean(x * x, axis=-1, keepdims=True) + RMS_EPS)


def _rms_bwd(dh, x, g):
    xh = x * _rstd(x)
    dxh = dh * g
    dx = _rstd(x) * (dxh - xh * jnp.mean(dxh * xh, axis=-1, keepdims=True))
    return dx, dh * xh


def _sigmoid(x):
    return 1.0 / (1.0 + jnp.exp(-x))


_GELU_K = math.sqrt(2.0 / math.pi)
_GELU_C = 0.044715


def _gelu(x):
    return 0.5 * x * (1.0 + jnp.tanh(_GELU_K * (x + _GELU_C * x * x * x)))


def _gelu_grad(x):
    th = jnp.tanh(_GELU_K * (x + _GELU_C * x * x * x))
    return 0.5 * (1.0 + th) + 0.5 * x * (1.0 - th * th) * _GELU_K * (1.0 + 3.0 * _GELU_C * x * x)


def _head_masks():
    lane_head = lax.broadcasted_iota(jnp.int32, (1, GROUP_LANES), 1) // HEAD_DIM
    return [(lane_head == h).astype(F32) for h in range(HEADS_PER_GROUP)]


def _head_block_diag():
    r = lax.broadcasted_iota(jnp.int32, (GROUP_LANES, GROUP_LANES), 0) // HEAD_DIM
    c = lax.broadcasted_iota(jnp.int32, (GROUP_LANES, GROUP_LANES), 1) // HEAD_DIM
    return (r == c).astype(F32)


def _head_mean(x, bd):
    return lax.dot_general(x, bd, _NN, precision=lax.Precision.HIGHEST, preferred_element_type=F32) * (1.0 / HEAD_DIM)


def _stack_heads(x, masks):
    return jnp.concatenate([x * m for m in masks], axis=0)


def _unstack_heads(xs, masks):
    out = xs[0:GRID_W] * masks[0]
    for h in range(1, HEADS_PER_GROUP):
        out = out + xs[h * GRID_W:(h + 1) * GRID_W] * masks[h]
    return out


def _row_start(r, rows):
    return jnp.clip(r - WIN_H // 2, 0, rows - WIN_H)


def _attn_common_specs(seq, n_hg, rows):
    win_keys = WIN_H * GRID_W
    q_spec = pl.BlockSpec((GRID_W, GROUP_LANES), lambda g, r: (r, g))
    k_spec = pl.BlockSpec((seq, GROUP_LANES), lambda g, r: (0, n_hg + g))
    v_spec = pl.BlockSpec((seq, GROUP_LANES), lambda g, r: (0, 2 * n_hg + g))
    gain_spec = pl.BlockSpec((1, GROUP_LANES), lambda g, r: (0, 0))

    def variant(r):
        return _row_start(r, rows) - r + (WIN_H - 1)

    bias_spec = pl.BlockSpec((None, None, HEADS_PER_GROUP, GRID_W, win_keys), lambda g, r: (g, variant(r), 0, 0, 0))
    return q_spec, k_spec, v_spec, gain_spec, bias_spec, variant


def _attn_prepare_kv(k_ref, v_ref, kg, kn_scr, vb_scr, bd, seq):
    chunk = _tile(seq, 512)

    def step(c, carry):
        rows = pl.ds(pl.multiple_of(c * chunk, chunk), chunk)
        k = k_ref[rows, :]
        kn_scr[rows, :] = (k * lax.rsqrt(_head_mean(k * k, bd) + RMS_EPS) * kg).astype(BF16)
        vb_scr[rows, :] = v_ref[rows, :].astype(BF16)
        return carry

    lax.fori_loop(0, seq // chunk, step, 0)


def _attn_probs(qn, kw, bias, masks):
    qs = _stack_heads(qn, masks).astype(BF16)
    s = _dot(qs, kw, _NT) * (1.0 / math.sqrt(HEAD_DIM)) + bias
    m = jnp.max(s, axis=-1, keepdims=True)
    p = jnp.exp(s - m)
    return qs, p / jnp.sum(p, axis=-1, keepdims=True)


def _attn_fwd(z, qg4, kg4, btab):
    seq = z.shape[0]
    a_width = btab.shape[0] * GROUP_LANES
    n_hg, rows, win_keys = btab.shape[0], seq // GRID_W, WIN_H * GRID_W
    q_spec, k_spec, v_spec, gain_spec, bias_spec, _ = _attn_common_specs(seq, n_hg, rows)

    def body(q_ref, k_ref, v_ref, qg_ref, kg_ref, b_ref, o_ref, kn_scr, vb_scr):
        r = pl.program_id(1)
        bd, masks = _head_block_diag(), _head_masks()

        @pl.when(r == 0)
        def _():
            _attn_prepare_kv(k_ref, v_ref, kg_ref[...], kn_scr, vb_scr, bd, seq)

        win = pl.ds(pl.multiple_of(_row_start(r, rows) * GRID_W, GRID_W), win_keys)
        q = q_ref[...]
        qn = q * lax.rsqrt(_head_mean(q * q, bd) + RMS_EPS) * qg_ref[...]
        bias = b_ref[...].reshape(HEADS_PER_GROUP * GRID_W, win_keys)
        _, p = _attn_probs(qn, kn_scr[win, :], bias, masks)
        o_ref[...] = _unstack_heads(_dot(p.astype(BF16), vb_scr[win, :]), masks)

    return pl.pallas_call(
        body, name="attn_fwd", grid=(n_hg, rows),
        in_specs=[q_spec, k_spec, v_spec, gain_spec, gain_spec, bias_spec],
        out_specs=pl.BlockSpec((GRID_W, GROUP_LANES), lambda g, r: (r, g)),
        out_shape=jax.ShapeDtypeStruct((seq, a_width), F32),
        scratch_shapes=[pltpu.VMEM((seq, GROUP_LANES), BF16), pltpu.VMEM((seq, GROUP_LANES), BF16)],
        compiler_params=_params(),
    )(z, z, z, qg4, kg4, btab)


def _attn_bwd(z, d_out, qg4, kg4, btab):
    seq = z.shape[0]
    n_hg, rows, win_keys = btab.shape[0], seq // GRID_W, WIN_H * GRID_W
    a_width = n_hg * GROUP_LANES
    q_spec, k_spec, v_spec, gain_spec, bias_spec, variant = _attn_common_specs(seq, n_hg, rows)
    scale = 1.0 / math.sqrt(HEAD_DIM)

    def body(q_ref, k_ref, v_ref, do_ref, qg_ref, kg_ref, b_ref,
             dq_ref, dk_ref, dv_ref, db_ref, dqg_ref, dkg_ref, kn_scr, vb_scr):
        r = pl.program_id(1)
        bd, masks = _head_block_diag(), _head_masks()

        @pl.when(r == 0)
        def _():
            _attn_prepare_kv(k_ref, v_ref, kg_ref[...], kn_scr, vb_scr, bd, seq)
            dk_ref[...] = jnp.zeros_like(dk_ref)
            dv_ref[...] = jnp.zeros_like(dv_ref)
            db_ref[...] = jnp.zeros_like(db_ref)
            dqg_ref[...] = jnp.zeros_like(dqg_ref)

        win = pl.ds(pl.multiple_of(_row_start(r, rows) * GRID_W, GRID_W), win_keys)
        q, qg = q_ref[...], qg_ref[...]
        rq = lax.rsqrt(_head_mean(q * q, bd) + RMS_EPS)
        qh = q * rq
        kw, vw = kn_scr[win, :], vb_scr[win, :]
        bias = b_ref[...].reshape(HEADS_PER_GROUP * GRID_W, win_keys)
        qs, p = _attn_probs(qh * qg, kw, bias, masks)
        dos = _stack_heads(do_ref[...], masks).astype(BF16)
        dp = _dot(dos, vw, _NT)
        ds = p * (dp - jnp.sum(p * dp, axis=-1, keepdims=True))
        db_ref[variant(r)] += ds.reshape(HEADS_PER_GROUP, GRID_W, win_keys)
        dsb = ds.astype(BF16)
        dqn = _unstack_heads(_dot(dsb, kw), masks) * scale
        dk_ref[win, :] += _dot(dsb, qs, _TN) * scale
        dv_ref[win, :] += _dot(p.astype(BF16), dos, _TN)
        dqg_ref[...] += jnp.sum(dqn * qh, axis=0, keepdims=True)
        dqh = dqn * qg
        dq_ref[...] = rq * (dqh - qh * _head_mean(dqh * qh, bd))

        @pl.when(r == rows - 1)
        def _():
            chunk = _tile(seq, 512)
            kg = kg_ref[...]

            def step(c, dkg):
                rws = pl.ds(pl.multiple_of(c * chunk, chunk), chunk)
                k = k_ref[rws, :]
                rk = lax.rsqrt(_head_mean(k * k, bd) + RMS_EPS)
                kh = k * rk
                dkn = dk_ref[rws, :]
                dkh = dkn * kg
                dk_ref[rws, :] = rk * (dkh - kh * _head_mean(dkh * kh, bd))
                return dkg + jnp.sum(dkn * kh, axis=0, keepdims=True)

            dkg_ref[...] = lax.fori_loop(0, seq // chunk, step, jnp.zeros((1, GROUP_LANES), F32))

    col_spec = pl.BlockSpec((seq, GROUP_LANES), lambda g, r: (0, g))
    gsum_spec = pl.BlockSpec((None, 1, GROUP_LANES), lambda g, r: (g, 0, 0))
    return pl.pallas_call(
        body, name="attn_bwd", grid=(n_hg, rows),
        in_specs=[q_spec, k_spec, v_spec, pl.BlockSpec((GRID_W, GROUP_LANES), lambda g, r: (r, g)),
                  gain_spec, gain_spec, bias_spec],
        out_specs=[pl.BlockSpec((GRID_W, GROUP_LANES), lambda g, r: (r, g)), col_spec, col_spec,
                   pl.BlockSpec((None, WIN_H, HEADS_PER_GROUP, GRID_W, win_keys), lambda g, r: (g, 0, 0, 0, 0)),
                   gsum_spec, gsum_spec],
        out_shape=[jax.ShapeDtypeStruct((seq, a_width), F32)] * 3
        + [jax.ShapeDtypeStruct(btab.shape, F32)]
        + [jax.ShapeDtypeStruct((n_hg, 1, GROUP_LANES), F32)] * 2,
        scratch_shapes=[pltpu.VMEM((seq, GROUP_LANES), BF16), pltpu.VMEM((seq, GROUP_LANES), BF16)],
        compiler_params=_params(),
    )(z, z, z, d_out, qg4, kg4, btab)


def _bias_index():
    c = jnp.arange(GRID_W)
    col_start = jnp.clip(c - WIN_W // 2, 0, GRID_W - WIN_W)
    col_in = (c[None, :] >= col_start[:, None]) & (c[None, :] < col_start[:, None] + WIN_W)
    dc = jnp.clip(c[None, :] - c[:, None], -(WIN_W - 1), WIN_W - 1) + (WIN_W - 1)
    dr = jnp.arange(WIN_H)[:, None] + jnp.arange(WIN_H)[None, :]
    return col_in, dc, dr


def _bias_table(rpb):
    col_in, dc, dr = _bias_index()
    n_hg = rpb.shape[0] // HEADS_PER_GROUP
    tab = rpb[:, dr[:, :, None, None], dc[None, None, :, :]]
    tab = jnp.where(col_in[None, None, None], tab, NEG_INF)
    tab = tab.reshape(n_hg, HEADS_PER_GROUP, WIN_H, WIN_H, GRID_W, GRID_W).transpose(0, 2, 1, 4, 3, 5)
    return tab.reshape(n_hg, WIN_H, HEADS_PER_GROUP, GRID_W, WIN_H * GRID_W)


def _bias_grad(dtab, n_h):
    col_in, dc, _ = _bias_index()
    onehot = (dc.reshape(-1, 1) == jnp.arange(128)[None, :]) & col_in.reshape(-1, 1)
    n_hg = n_h // HEADS_PER_GROUP
    d = dtab.reshape(n_hg, WIN_H, HEADS_PER_GROUP, GRID_W, WIN_H, GRID_W).transpose(0, 2, 1, 4, 3, 5)
    d = d.reshape(n_h * WIN_H * WIN_H, GRID_W * GRID_W)
    diag = _mm_plain(d, onehot.astype(BF16), "nn", name="rpb_diag_sum", tm=256, tn=128)
    diag = diag.reshape(n_h, WIN_H, WIN_H, 128)[..., : 2 * WIN_W - 1]
    out = jnp.zeros((n_h, 2 * WIN_H - 1, 2 * WIN_W - 1), F32)
    for v in range(WIN_H):
        out = out.at[:, v:v + WIN_H].add(diag[:, v])
    return out


def _cmul(ar, ai, br, bi):
    return ar * br - ai * bi, ar * bi + ai * br


def _s5_discretize(a_re, a_im, dt, b_re, b_im):
    c = b_re.shape[1]

    def fn(are, aim, dt_, bre, bim):
        lr, li = jnp.minimum(are, A_RE_MAX), aim
        mag = jnp.exp(lr * dt_)
        l1r, l1i = mag * jnp.cos(li * dt_), mag * jnp.sin(li * dt_)
        den = lr * lr + li * li
        nr, ni = l1r - 1.0, l1i
        cr, ci = (nr * lr + ni * li) / den, (ni * lr - nr * li) / den
        bbr, bbi = _cmul(cr, ci, bre, bim)
        shape = (are.shape[0], SUBLANES)
        lane = lax.broadcasted_iota(jnp.int32, shape, 1)
        pr, pi = l1r, l1i
        acc_r, acc_i = jnp.zeros(shape, F32), jnp.zeros(shape, F32)
        for k in range(SUBLANES):
            acc_r = jnp.where(lane == k, pr, acc_r)
            acc_i = jnp.where(lane == k, pi, acc_i)
            pr, pi = _cmul(pr, pi, l1r, l1i)
        return acc_r, acc_i, cr, ci, bbr, bbi

    return _rowwise(fn, [a_re, a_im, dt, b_re, b_im], [],
                    [(SUBLANES, F32), (SUBLANES, F32), (1, F32), (1, F32), (c, F32), (c, F32)],
                    name="s5_discretize", tm=1024)


def _s5_param_grads(a_re, a_im, dt, b_re, b_im, l1r, l1i, cr, ci, bbr, bbi, r_re, r_im, gb_re, gb_im):
    c = b_re.shape[1]

    def fn(are, aim, dt_, bre, bim, l1r_, l1i_, cr_, ci_, bbr_, bbi_, rr, ri, gbr, gbi):
        lr, li = jnp.minimum(are, A_RE_MAX), aim
        den = lr * lr + li * li
        dbr, dbi = _cmul(cr_, -ci_, gbr, gbi)
        gcr, gci = _cmul(bre, -bim, gbr, gbi)
        gcr, gci = jnp.sum(gcr, axis=1, keepdims=True), jnp.sum(gci, axis=1, keepdims=True)
        qr, qi = _cmul(bbr_, -bbi_, gbr, gbi)
        qr = rr - jnp.sum(qr, axis=1, keepdims=True)
        qi = ri - jnp.sum(qi, axis=1, keepdims=True)
        tr, ti = _cmul(gcr, gci, lr / den, li / den)
        ur, ui = _cmul(l1r_, -l1i_, tr, ti)
        gwr, gwi = qr + ur, qi + ui
        vr, vi = _cmul(cr_, -ci_, lr / den, li / den)
        vr, vi = _cmul(gcr, gci, vr, vi)
        glr, gli = dt_ * gwr - vr, dt_ * gwi - vi
        return jnp.where(are < A_RE_MAX, glr, 0.0), gli, (gwr * lr + gwi * li) * dt_, dbr, dbi

    return _rowwise(fn, [a_re, a_im, dt, b_re, b_im, l1r, l1i, cr, ci, bbr, bbi, r_re, r_im, gb_re, gb_im], [],
                    [(1, F32), (1, F32), (1, F32), (c, F32), (c, F32)], name="s5_param_grads", tm=1024)


def _s5_scan(v, win_re, win_im, tabs, wo_re, wo_im, *, reverse, name, t_chunk=256):
    seq, width = v.shape
    n_tiles, n_state = width // U_TILE, width * (SSM_P // SSM_C)
    t_chunk = _tile(seq, t_chunk)
    n_chunks, n_blk = seq // t_chunk, t_chunk // SUBLANES
    last_row = 0 if reverse else SUBLANES - 1

    def chunk_of(j):
        return (n_chunks - 1 - j) if reverse else j

    def body(v_ref, wir_ref, wii_ref, tab_ref, wor_ref, woi_ref, sr_ref, si_ref, y_ref, carry):
        @pl.when(pl.program_id(0) == 0)
        def _():
            carry[...] = jnp.zeros_like(carry)

        for jt in range(n_tiles):
            ls = slice(jt * ST_TILE, (jt + 1) * ST_TILE)
            us = slice(jt * U_TILE, (jt + 1) * U_TILE)
            vj = v_ref[:, us].astype(BF16)
            sr_ref[:, ls] = _dot(vj, wir_ref[jt])
            si_ref[:, ls] = _dot(vj, wii_ref[jt])
            consts = [tab_ref[k, :, ls] for k in range(8)]

            def blk(b, c, ls=ls, consts=consts):
                cr, ci = c
                bb = (n_blk - 1 - b) if reverse else b
                rows = pl.ds(pl.multiple_of(bb * SUBLANES, SUBLANES), SUBLANES)
                xr, xi = sr_ref[rows, ls], si_ref[rows, ls]
                for s, k in enumerate((1, 2, 4)):
                    sh = (SUBLANES - k) if reverse else k
                    tr, ti = pltpu.roll(xr, sh, 0), pltpu.roll(xi, sh, 0)
                    lr, li = consts[2 * s], consts[2 * s + 1]
                    xr, xi = xr + lr * tr - li * ti, xi + lr * ti + li * tr
                lr, li = consts[6], consts[7]
                xr, xi = xr + lr * cr - li * ci, xi + lr * ci + li * cr
                sr_ref[rows, ls], si_ref[rows, ls] = xr, xi
                shape = (SUBLANES, ST_TILE)
                return (jnp.broadcast_to(xr[last_row:last_row + 1], shape),
                        jnp.broadcast_to(xi[last_row:last_row + 1], shape))

            cr, ci = lax.fori_loop(0, n_blk, blk, (carry[0, :, ls], carry[1, :, ls]))
            carry[0, :, ls], carry[1, :, ls] = cr, ci
            y_ref[:, us] = (_dot(sr_ref[:, ls].astype(BF16), wor_ref[jt])
                            + _dot(si_ref[:, ls].astype(BF16), woi_ref[jt]))

    whole = lambda a: pl.BlockSpec(a.shape, lambda j, nd=a.ndim: (0,) * nd)
    st_spec = pl.BlockSpec((t_chunk, n_state), lambda j: (chunk_of(j), 0))
    v_spec = pl.BlockSpec((t_chunk, width), lambda j: (chunk_of(j), 0))
    return pl.pallas_call(
        body, name=name, grid=(n_chunks,),
        in_specs=[v_spec, whole(win_re), whole(win_im), whole(tabs), whole(wo_re), whole(wo_im)],
        out_specs=[st_spec, st_spec, v_spec],
        out_shape=[jax.ShapeDtypeStruct((seq, n_state), F32)] * 2 + [jax.ShapeDtypeStruct((seq, width), F32)],
        scratch_shapes=[pltpu.VMEM((2, SUBLANES, n_state), F32)],
        compiler_params=_params(),
    )(v, win_re, win_im, tabs, wo_re, wo_im)


def _s5_reduce(x_re, x_im, a_re, a_im, u, dy, *, name, t_chunk=512):
    seq, n_state = x_re.shape
    width = u.shape[1]
    n_tiles = width // U_TILE
    t_chunk = _tile(seq, t_chunk)

    def body(xr_ref, xi_ref, ar_ref, ai_ref, u_ref, dy_ref, rr_ref, ri_ref, gbr_ref, gbi_ref, gcr_ref, gci_ref):
        xr, xi, ar, ai = xr_ref[...], xi_ref[...], ar_ref[...], ai_ref[...]
        ub, dyb = u_ref[...].astype(BF16), dy_ref[...].astype(BF16)
        parts = (jnp.sum(ar * xr + ai * xi, axis=0, keepdims=True), jnp.sum(ai * xr - ar * xi, axis=0, keepdims=True),
                 _dot(ar.astype(BF16), ub, _TN), _dot(ai.astype(BF16), ub, _TN),
                 _dot(xr.astype(BF16), dyb, _TN), _dot(xi.astype(BF16), dyb, _TN))
        first = pl.program_id(1) == 0
        for ref, val in zip((rr_ref, ri_ref, gbr_ref, gbi_ref, gcr_ref, gci_ref), parts):
            @pl.when(first)
            def _():
                ref[...] = val

            @pl.when(jnp.logical_not(first))
            def _():
                ref[...] += val

    st_spec = pl.BlockSpec((t_chunk, ST_TILE), lambda j, t: (t, j))
    u_spec = pl.BlockSpec((t_chunk, U_TILE), lambda j, t: (t, j))
    r_spec = pl.BlockSpec((1, ST_TILE), lambda j, t: (0, j))
    g_spec = pl.BlockSpec((None, ST_TILE, U_TILE), lambda j, t: (j, 0, 0))
    return pl.pallas_call(
        body, name=name, grid=(n_tiles, seq // t_chunk),
        in_specs=[st_spec] * 4 + [u_spec] * 2,
        out_specs=[r_spec, r_spec] + [g_spec] * 4,
        out_shape=[jax.ShapeDtypeStruct((1, n_state), F32)] * 2
        + [jax.ShapeDtypeStruct((n_tiles, ST_TILE, U_TILE), F32)] * 4,
        compiler_params=_params(),
    )(x_re, x_im, a_re, a_im, u, dy)


def _block_diag_in(m):
    g, c, p = m.shape
    m5 = m.reshape(g // GROUPS_PER_TILE, GROUPS_PER_TILE, c, p)
    eye = jnp.eye(GROUPS_PER_TILE, dtype=m.dtype)
    out = m5[:, :, :, None, :] * eye[None, :, None, :, None]
    return out.reshape(g // GROUPS_PER_TILE, GROUPS_PER_TILE * c, GROUPS_PER_TILE * p)


def _block_diag_take(m, c, p):
    t = m.shape[0]
    m5 = m.reshape(t, GROUPS_PER_TILE, p, GROUPS_PER_TILE, c)
    idx = jnp.arange(GROUPS_PER_TILE)
    return m5[:, idx, :, idx, :].transpose(1, 0, 2, 3).reshape(t * GROUPS_PER_TILE, p, c)


def _scan_tables(pw_re, pw_im, reverse):
    row = jnp.arange(SUBLANES)[:, None]
    tabs = []
    for k in (1, 2, 4):
        keep = (row <= SUBLANES - 1 - k) if reverse else (row >= k)
        tabs += [jnp.where(keep, pw_re[k - 1][None, :], 0.0), jnp.where(keep, pw_im[k - 1][None, :], 0.0)]
    order = jnp.arange(SUBLANES)[::-1] if reverse else jnp.arange(SUBLANES)
    tabs += [pw_re[order], pw_im[order]]
    return jnp.stack(tabs)


def _local_step(x, target, p, w):
    seq, d_model = x.shape
    a_width = p["g_out_attn"].shape[-1]
    s_width = p["g_out_ssm"].shape[-1]
    n_heads = a_width // HEAD_DIM
    n_hg = n_heads // HEADS_PER_GROUP
    n_groups = s_width // SSM_C
    n_sh, _, in_sh = w["w_in"].shape
    f_sh = w["w_gate"].shape[2]
    tm = _tile(seq, 512)
    n_i = seq // tm

    h1 = _rowwise(lambda xv, g: x_norm(xv, g), [x], [p["g_mix"]], [(d_model, BF16)], name="rms_mix")[0]
    z = _mm(h1, w["w_in"], name="in_proj", grid=(n_sh, n_i),
            a_spec=pl.BlockSpec((tm, d_model), lambda j, i: (i, 0)),
            b_spec=pl.BlockSpec((None, d_model, in_sh), lambda j, i: (j, 0, 0)),
            o_spec=pl.BlockSpec((tm, in_sh), lambda j, i: (i, j)), o_shape=(seq, n_sh * in_sh), dims="nn")
    qg4 = jnp.tile(p["q_gain"], (1, HEADS_PER_GROUP))
    kg4 = jnp.tile(p["k_gain"], (1, HEADS_PER_GROUP))
    btab = _bias_table(p["rpb"])
    ya = _attn_fwd(z, qg4, kg4, btab)
    u = z[:, 3 * a_width:]

    n_col = 2 * n_groups * SSM_P
    col = lambda a: a.reshape(n_col, 1)
    a_re_c, a_im_c = col(p["ssm_a_re"]), col(p["ssm_a_im"])
    dt_c = col(jnp.broadcast_to(jnp.exp(p["ssm_log_step"])[:, :, None], (2, n_groups, SSM_P)))
    b_re_c, b_im_c = p["ssm_b_re"].reshape(n_col, SSM_C), p["ssm_b_im"].reshape(n_col, SSM_C)
    pw_re, pw_im, cf_re, cf_im, bb_re, bb_im = _s5_discretize(a_re_c, a_im_c, dt_c, b_re_c, b_im_c)
    n_state = n_groups * SSM_P
    pw_re = pw_re.reshape(2, n_state, SUBLANES).transpose(0, 2, 1)
    pw_im = pw_im.reshape(2, n_state, SUBLANES).transpose(0, 2, 1)
    bb_re4, bb_im4 = bb_re.reshape(2, n_groups, SSM_P, SSM_C), bb_im.reshape(2, n_groups, SSM_P, SSM_C)
    c_re, c_im = p["ssm_c_re"], p["ssm_c_im"]
    fwd, bwd_in = [], []
    for d in range(2):
        rev = d == 1
        win_re = _block_diag_in(bb_re4[d].transpose(0, 2, 1)).astype(BF16)
        win_im = _block_diag_in(bb_im4[d].transpose(0, 2, 1)).astype(BF16)
        wo_re = _block_diag_in(c_re[d].transpose(0, 2, 1)).astype(BF16)
        wo_im = _block_diag_in(-c_im[d].transpose(0, 2, 1)).astype(BF16)
        tabs = _scan_tables(pw_re[d], pw_im[d], rev)
        xs_re, xs_im, y_d = _s5_scan(u, win_re, win_im, tabs, wo_re, wo_im, reverse=rev, name=f"s5_fwd_{d}")
        fwd.append((xs_re, xs_im, y_d))
        bwd_in.append((_block_diag_in(c_re[d]).astype(BF16), _block_diag_in(-c_im[d]).astype(BF16),
                       _scan_tables(pw_re[d], -pw_im[d], not rev),
                       _block_diag_in(bb_re4[d]).astype(BF16), _block_diag_in(bb_im4[d]).astype(BF16)))

    ypre, yg = _rowwise(lambda y0, y1, uv, dsk: s5_mid(y0, y1, uv, dsk), [fwd[0][2], fwd[1][2], u], [p["ssm_d"]],
                        [(s_width, F32), (s_width, F32)], name="s5_skip_gelu")
    t_glu = _mm_plain(yg, w["w_glu"], "nn", name="glu_proj", tn=s_width)
    y_cat = _rowwise(mix_out_fwd, [ya, yg, t_glu], [p["b_glu"], p["g_out_attn"], p["g_out_ssm"]],
                     [(a_width + s_width, BF16)], name="mix_out")[0]
    x1 = _mm_plain(y_cat, w["w_out"], "nn", name="out_proj", res=x)

    h2 = _rowwise(lambda xv, g: x_norm(xv, g), [x1], [p["g_ffn"]], [(d_model, BF16)], name="rms_ffn")[0]
    ffn_up = functools.partial(
        _mm, grid=(n_sh, n_i), a_spec=pl.BlockSpec((tm, d_model), lambda j, i: (i, 0)),
        b_spec=pl.BlockSpec((None, d_model, f_sh), lambda j, i: (j, 0, 0)),
        o_spec=pl.BlockSpec((None, tm, f_sh), lambda j, i: (j, i, 0)), o_shape=(n_sh, seq, f_sh), dims="nn")
    gate = ffn_up(h2, w["w_gate"], name="ffn_gate")
    up = ffn_up(h2, w["w_up"], name="ffn_up")
    flat = lambda a: a.reshape(n_sh * seq, f_sh)
    act = _rowwise(lambda gv, uv: gv * _sigmoid(gv) * uv, [flat(gate), flat(up)], [], [(f_sh, BF16)],
                   name="swiglu", tm=512)[0].reshape(n_sh, seq, f_sh)
    tn_d = _tile(d_model, 512)
    x2 = _mm(act, w["w_down"], name="ffn_down", grid=(d_model // tn_d, n_i, n_sh),
             a_spec=pl.BlockSpec((None, tm, f_sh), lambda n, i, j: (j, i, 0)),
             b_spec=pl.BlockSpec((None, f_sh, tn_d), lambda n, i, j: (j, 0, n)),
             o_spec=pl.BlockSpec((tm, tn_d), lambda n, i, j: (i, n)), o_shape=(seq, d_model), dims="nn", k_axis=2,
             res=x1)

    dx2, sq = _rowwise(lambda xv, tv: ((xv - tv) * (1.0 / d_model), (xv - tv) * (xv - tv)), [x2, target], [],
                       [(d_model, F32)], [d_model], name="loss_head")
    loss = 0.5 * jnp.sum(sq) / d_model

    d_act = _mm(dx2, w["w_down"], name="ffn_down_dx", grid=(n_sh, n_i),
                a_spec=pl.BlockSpec((tm, d_model), lambda j, i: (i, 0)),
                b_spec=pl.BlockSpec((None, f_sh, d_model), lambda j, i: (j, 0, 0)),
                o_spec=pl.BlockSpec((None, tm, f_sh), lambda j, i: (j, i, 0)), o_shape=(n_sh, seq, f_sh), dims="nt")
    g_w_down = _mm(act, dx2, name="ffn_down_dw", grid=(n_sh, d_model // tn_d, n_i),
                   a_spec=pl.BlockSpec((None, tm, f_sh), lambda j, n, k: (j, k, 0)),
                   b_spec=pl.BlockSpec((tm, tn_d), lambda j, n, k: (k, n)),
                   o_spec=pl.BlockSpec((None, f_sh, tn_d), lambda j, n, k: (j, 0, n)),
                   o_shape=(n_sh, f_sh, d_model), dims="tn", k_axis=2)
    d_gate, d_up = _rowwise(swiglu_bwd, [flat(d_act), flat(gate), flat(up)], [], [(f_sh, BF16), (f_sh, BF16)],
                            name="swiglu_bwd", tm=512)
    d_gate, d_up = d_gate.reshape(n_sh, seq, f_sh), d_up.reshape(n_sh, seq, f_sh)
    ffn_dx = functools.partial(
        _mm, grid=(n_i, n_sh), a_spec=pl.BlockSpec((None, tm, f_sh), lambda i, j: (j, i, 0)),
        b_spec=pl.BlockSpec((None, d_model, f_sh), lambda i, j: (j, 0, 0)),
        o_spec=pl.BlockSpec((tm, d_model), lambda i, j: (i, 0)), o_shape=(seq, d_model), dims="nt", k_axis=1)
    d_h2 = ffn_dx(d_gate, w["w_gate"], name="ffn_gate_dx")
    d_h2 = ffn_dx(d_up, w["w_up"], name="ffn_up_dx", res=d_h2)
    ffn_dw = functools.partial(
        _mm, grid=(n_sh, n_i), a_spec=pl.BlockSpec((tm, d_model), lambda j, k: (k, 0)),
        b_spec=pl.BlockSpec((None, tm, f_sh), lambda j, k: (j, k, 0)),
        o_spec=pl.BlockSpec((None, d_model, f_sh), lambda j, k: (j, 0, 0)), o_shape=(n_sh, d_model, f_sh), dims="tn",
        k_axis=1)
    g_w_gate = ffn_dw(h2, d_gate, name="ffn_gate_dw")
    g_w_up = ffn_dw(h2, d_up, name="ffn_up_dw")
    dx1, g_g_ffn = _rowwise(residual_rms_bwd, [dx2, d_h2, x1], [p["g_ffn"]], [(d_model, F32)], [d_model],
                            name="rms_ffn_bwd")

    d_ycat = _mm_plain(dx1, w["w_out"], "nt", name="out_proj_dx")
    g_w_out = _mm_plain(y_cat, dx1, "tn", name="out_proj_dw")
    (d_ya, d_yg_direct, d_t, g_goa, g_gos, g_b_glu) = _rowwise(
        functools.partial(mix_out_bwd, a_width=a_width), [d_ycat, ya, yg, t_glu],
        [p["b_glu"], p["g_out_attn"], p["g_out_ssm"]],
        [(a_width, F32), (s_width, F32), (s_width, BF16)], [a_width, s_width, s_width], name="mix_out_bwd")
    d_yg = _mm_plain(d_t, w["w_glu"], "nt", name="glu_proj_dx", res=d_yg_direct, tn=s_width)
    g_w_glu = _mm_plain(yg, d_t, "tn", name="glu_proj_dw")
    d_ypre, du_skip, g_ssm_d = _rowwise(gelu_skip_bwd, [d_yg, ypre, u], [p["ssm_d"]],
                                        [(s_width, F32), (s_width, F32)], [s_width], name="s5_skip_gelu_bwd")

    du_dirs, r_parts, gb_parts, gc_parts = [], [], [], []
    for d in range(2):
        win_re, win_im, tabs, wo_re, wo_im = bwd_in[d]
        as_re, as_im, du_d = _s5_scan(d_ypre, win_re, win_im, tabs, wo_re, wo_im, reverse=(d == 0),
                                      name=f"s5_bwd_{d}")
        du_dirs.append(du_d)
        r_re, r_im, gbt_re, gbt_im, gct_re, gct_im = _s5_reduce(fwd[d][0], fwd[d][1], as_re, as_im, u, d_ypre,
                                                                name=f"s5_reduce_{d}")
        r_parts.append((r_re.reshape(n_state, 1), r_im.reshape(n_state, 1)))
        gb_parts.append((_block_diag_take(gbt_re, SSM_C, SSM_P), _block_diag_take(gbt_im, SSM_C, SSM_P)))
        gc_parts.append((_block_diag_take(gct_re, SSM_C, SSM_P), _block_diag_take(gct_im, SSM_C, SSM_P)))
    cat = lambda i, parts: jnp.concatenate([parts[0][i], parts[1][i]], axis=0)
    gbb_re, gbb_im = cat(0, gb_parts).reshape(n_col, SSM_C), cat(1, gb_parts).reshape(n_col, SSM_C)
    g_a_re, g_a_im, g_ls, g_b_re, g_b_im = _s5_param_grads(
        a_re_c, a_im_c, dt_c, b_re_c, b_im_c, pw_re[:, 0].reshape(n_col, 1), pw_im[:, 0].reshape(n_col, 1),
        cf_re, cf_im, bb_re, bb_im, cat(0, r_parts), cat(1, r_parts), gbb_re, gbb_im)
    g_c_re = cat(0, gc_parts).reshape(2, n_groups, SSM_P, SSM_C).transpose(0, 1, 3, 2)
    g_c_im = -cat(1, gc_parts).reshape(2, n_groups, SSM_P, SSM_C).transpose(0, 1, 3, 2)

    d_q, d_k, d_v, d_btab, g_qg, g_kg = _attn_bwd(z, d_ya, qg4, kg4, btab)
    d_u = _rowwise(lambda a, b, c: a + b + c, [du_dirs[0], du_dirs[1], du_skip], [], [(s_width, F32)],
                   name="s5_du_sum")[0]
    d_z = jnp.concatenate([d_q, d_k, d_v, d_u], axis=1)
    d_h1 = _mm(d_z, w["w_in"], name="in_proj_dx", grid=(n_i, n_sh),
               a_spec=pl.BlockSpec((tm, in_sh), lambda i, j: (i, j)),
               b_spec=pl.BlockSpec((None, d_model, in_sh), lambda i, j: (j, 0, 0)),
               o_spec=pl.BlockSpec((tm, d_model), lambda i, j: (i, 0)), o_shape=(seq, d_model), dims="nt", k_axis=1)
    g_w_in = _mm(h1, d_z, name="in_proj_dw", grid=(n_sh, n_i),
                 a_spec=pl.BlockSpec((tm, d_model), lambda j, k: (k, 0)),
                 b_spec=pl.BlockSpec((tm, in_sh), lambda j, k: (k, j)),
                 o_spec=pl.BlockSpec((None, d_model, in_sh), lambda j, k: (j, 0, 0)),
                 o_shape=(n_sh, d_model, in_sh), dims="tn", k_axis=1)
    grad_x, g_g_mix = _rowwise(residual_rms_bwd, [dx1, d_h1, x], [p["g_mix"]], [(d_model, F32)], [d_model],
                               name="rms_mix_bwd")

    fold_heads = lambda g: g.reshape(n_heads, HEAD_DIM).sum(axis=0, keepdims=True)
    small = {
        "g_mix": g_g_mix, "q_gain": fold_heads(g_qg), "k_gain": fold_heads(g_kg),
        "rpb": _bias_grad(d_btab, n_heads),
        "ssm_a_re": g_a_re.reshape(2, n_groups, SSM_P), "ssm_a_im": g_a_im.reshape(2, n_groups, SSM_P),
        "ssm_b_re": g_b_re.reshape(2, n_groups, SSM_P, SSM_C), "ssm_b_im": g_b_im.reshape(2, n_groups, SSM_P, SSM_C),
        "ssm_c_re": g_c_re, "ssm_c_im": g_c_im,
        "ssm_log_step": g_ls.reshape(2, n_groups, SSM_P).sum(axis=-1),
        "ssm_d": g_ssm_d, "b_glu": g_b_glu, "g_out_attn": g_goa, "g_out_ssm": g_gos, "g_ffn": g_g_ffn,
    }
    big = {"w_in": g_w_in, "w_glu": g_w_glu, "w_out": g_w_out, "w_ffn_gate": g_w_gate, "w_ffn_up": g_w_up,
           "w_ffn_down": g_w_down}
    return loss, grad_x, small, big


def x_norm(xv, g):
    return xv * _rstd(xv) * g


def s5_mid(y0, y1, uv, d_skip):
    ypre = y0 + y1 + d_skip * uv
    return ypre, _gelu(ypre)


def mix_out_fwd(ya, yg, t, b_glu, g_oa, g_os):
    ys = yg * _sigmoid(t + b_glu)
    return jnp.concatenate([ya * _rstd(ya) * g_oa, ys * _rstd(ys) * g_os], axis=1)


def mix_out_bwd(d_y, ya, yg, t, b_glu, g_oa, g_os, *, a_width):
    sg = _sigmoid(t + b_glu)
    ys = yg * sg
    d_ya, c_goa = _rms_bwd(d_y[:, :a_width], ya, g_oa)
    d_ys, c_gos = _rms_bwd(d_y[:, a_width:], ys, g_os)
    d_t = d_ys * yg * sg * (1.0 - sg)
    return d_ya, d_ys * sg, d_t, c_goa, c_gos, d_t


def gelu_skip_bwd(d_yg, ypre, uv, d_skip):
    d_ypre = d_yg * _gelu_grad(ypre)
    return d_ypre, d_ypre * d_skip, d_ypre * uv


def swiglu_bwd(d_act, gv, uv):
    sg = _sigmoid(gv)
    return d_act * uv * (sg * (1.0 + gv * (1.0 - sg))), d_act * gv * sg


def residual_rms_bwd(d_res, d_h, xv, g):
    dx, c_g = _rms_bwd(d_h, xv, g)
    return d_res + dx, c_g


_ANY = pl.BlockSpec(memory_space=pl.ANY)


def _mesh_place():
    return lax.axis_index("x"), lax.axis_index("y"), lax.axis_index("c")


def _chips(x, y):
    return [(x, y), (1 - x, y), (x, 1 - y), (1 - x, 1 - y)]


def _slab(px, py, pc):
    return 4 * px + 2 * py + pc


def _all_gather(arrs, *, name):
    n = len(arrs)

    def body(*refs):
        in_refs, out_refs = refs[:n], refs[n:2 * n]
        send_sems, recv_sems, local_sems = refs[2 * n:]
        x, y, c = _mesh_place()
        me, sibling = (x, y, c), (x, y, 1 - c)
        others = _chips(x, y)[1:]

        def copy(w, k, block, to, src=None):
            dst = out_refs[w].at[_slab(*block)]
            return pltpu.make_async_remote_copy(
                src_ref=dst if src is None else src, dst_ref=dst, send_sem=send_sems.at[7 * w + k],
                recv_sem=recv_sems.at[7 * w + k], device_id=to, device_id_type=MESH)

        mine = [pltpu.make_async_copy(in_refs[w], out_refs[w].at[_slab(*me)], local_sems.at[w]) for w in range(n)]
        first = []
        for w in range(n):
            mine[w].start()
            first.append(copy(w, 0, me, sibling, src=in_refs[w]))
            first += [copy(w, 1 + j, me, (*chip, c), src=in_refs[w]) for j, chip in enumerate(others)]
        for cp in first:
            cp.start()
        passed = []
        for j, chip in enumerate(others):
            for w in range(n):
                copy(w, 1 + j, (*chip, c), me).wait_recv()
                fwd = copy(w, 4 + j, (*chip, c), sibling)
                fwd.start()
                passed.append(fwd)
        for w in range(n):
            copy(w, 0, sibling, me).wait_recv()
        for j, chip in enumerate(others):
            for w in range(n):
                copy(w, 4 + j, (*chip, 1 - c), me).wait_recv()
        for cp in first + passed:
            cp.wait_send()
        for cp in mine:
            cp.wait()

    return pl.pallas_call(
        body, name=name, in_specs=[_ANY] * n, out_specs=[_ANY] * n,
        out_shape=[jax.ShapeDtypeStruct((N_DEV,) + a.shape, a.dtype) for a in arrs],
        scratch_shapes=[pltpu.SemaphoreType.DMA((7 * n,)), pltpu.SemaphoreType.DMA((7 * n,)),
                        pltpu.SemaphoreType.DMA((n,))],
        compiler_params=pltpu.CompilerParams(has_side_effects=True),
    )(*arrs)


def _swap(arrs, n_out, plan, *, name):
    n = len(arrs)

    def body(*refs):
        in_refs, out_refs = refs[:n], refs[n:2 * n]
        send_sems, recv_sems = refs[2 * n:]
        copies = []
        for k in range(n_out):
            for w, (src, dst, to) in enumerate(plan(in_refs, out_refs, k)):
                copies.append(pltpu.make_async_remote_copy(
                    src_ref=src, dst_ref=dst, send_sem=send_sems.at[n_out * w + k],
                    recv_sem=recv_sems.at[n_out * w + k], device_id=to, device_id_type=MESH))
        for cp in copies:
            cp.start()
        for cp in copies:
            cp.wait_recv()
        for cp in copies:
            cp.wait_send()

    return pl.pallas_call(
        body, name=name, in_specs=[_ANY] * n, out_specs=[_ANY] * n,
        out_shape=[jax.ShapeDtypeStruct((n_out,) + a.shape[1:], a.dtype) for a in arrs],
        scratch_shapes=[pltpu.SemaphoreType.DMA((n_out * n,)), pltpu.SemaphoreType.DMA((n_out * n,))],
        compiler_params=pltpu.CompilerParams(has_side_effects=True),
    )(*arrs)


def _sibling_exchange(grads):
    def plan(in_refs, out_refs, k):
        x, y, c = _mesh_place()
        px, py = _chips(x, y)[k]
        return [(g.at[_slab(px, py, 1 - c)], o.at[k], (x, y, 1 - c)) for g, o in zip(in_refs, out_refs)]

    return _swap(grads, 4, plan, name="reduce_sibling")


def _chip_exchange(partials):
    def plan(in_refs, out_refs, k):
        x, y, c = _mesh_place()
        px, py = _chips(x, y)[k + 1]
        return [(g.at[k], o.at[k], (px, py, c)) for g, o in zip(in_refs, out_refs)]

    return _swap(partials, 3, plan, name="reduce_chips")


def _adamw(w, m, v, parts, *, name, tr=256):
    rows, cols = w.shape
    tr = _tile(rows, tr)
    n_p = len(parts)

    def body(*refs):
        w_ref, m_ref, v_ref = refs[:3]
        p_refs = refs[3:3 + n_p]
        g_ref, d_ref, nm_ref, nv_ref = refs[3 + n_p:]
        g = None
        for (_, lead), r in zip(parts, p_refs):
            for piece in ([r[...]] if lead is None else [r[i] for i in range(lead)]):
                g = piece if g is None else g + piece
        new_m = ADAM_B1 * m_ref[...] + (1.0 - ADAM_B1) * g
        new_v = ADAM_B2 * v_ref[...] + (1.0 - ADAM_B2) * (g * g)
        m_hat = new_m / (1.0 - ADAM_B1 ** ADAM_STEP)
        v_hat = new_v / (1.0 - ADAM_B2 ** ADAM_STEP)
        g_ref[...] = g
        d_ref[...] = -ADAM_LR * (m_hat / (jnp.sqrt(v_hat) + ADAM_EPS) + ADAM_WD * w_ref[...])
        nm_ref[...] = new_m
        nv_ref[...] = new_v

    tile = pl.BlockSpec((tr, cols), lambda i: (i, 0))
    p_specs = [tile if lead is None else pl.BlockSpec((lead, tr, cols), lambda i: (0, i, 0)) for _, lead in parts]
    return pl.pallas_call(
        body, name=name, grid=(rows // tr,), in_specs=[tile] * 3 + p_specs, out_specs=[tile] * 4,
        out_shape=[jax.ShapeDtypeStruct((rows, cols), F32)] * 4, compiler_params=_params(),
    )(w, m, v, *[a for a, _ in parts])


_PACK_TILE = SUBLANES * 128


def _pack(arrs):
    flat = []
    for a in arrs:
        f = a.reshape(-1)
        flat.append(jnp.pad(f, (0, (-f.shape[0]) % _PACK_TILE)))
    return jnp.concatenate(flat).reshape(-1, 128)


def _unpack(buf, shapes):
    out, at = [], 0
    flat = buf.reshape(-1)
    for s in shapes:
        n = math.prod(s)
        out.append(flat[at:at + n].reshape(s))
        at += n + (-n) % _PACK_TILE
    return out


BIG = ("w_in", "w_glu", "w_out", "w_ffn_gate", "w_ffn_up", "w_ffn_down")
WEIGHTS = ("g_mix", "w_in", "q_gain", "k_gain", "rpb", "ssm_a_re", "ssm_a_im", "ssm_b_re", "ssm_b_im", "ssm_c_re",
           "ssm_c_im", "ssm_log_step", "ssm_d", "w_glu", "b_glu", "g_out_attn", "g_out_ssm", "w_out", "g_ffn",
           "w_ffn_gate", "w_ffn_up", "w_ffn_down")
SMALL = tuple(n for n in WEIGHTS if n not in BIG)
VECTORS = ("g_mix", "q_gain", "k_gain", "ssm_d", "b_glu", "g_out_attn", "g_out_ssm", "g_ffn")


def kernel(x, g_mix, w_in, q_gain, k_gain, rpb, ssm_a_re, ssm_a_im, ssm_b_re, ssm_b_im, ssm_c_re, ssm_c_im, ssm_log_step, ssm_d, w_glu, b_glu, g_out_attn, g_out_ssm, w_out, g_ffn, w_ffn_gate, w_ffn_up, w_ffn_down, loss_target, m_g_mix, m_w_in, m_q_gain, m_k_gain, m_rpb, m_ssm_a_re, m_ssm_a_im, m_ssm_b_re, m_ssm_b_im, m_ssm_c_re, m_ssm_c_im, m_ssm_log_step, m_ssm_d, m_w_glu, m_b_glu, m_g_out_attn, m_g_out_ssm, m_w_out, m_g_ffn, m_w_ffn_gate, m_w_ffn_up, m_w_ffn_down, v_g_mix, v_w_in, v_q_gain, v_k_gain, v_rpb, v_ssm_a_re, v_ssm_a_im, v_ssm_b_re, v_ssm_b_im, v_ssm_c_re, v_ssm_c_im, v_ssm_log_step, v_ssm_d, v_w_glu, v_b_glu, v_g_out_attn, v_g_out_ssm, v_w_out, v_g_ffn, v_w_ffn_gate, v_w_ffn_up, v_w_ffn_down):
    wts = dict(g_mix=g_mix, w_in=w_in, q_gain=q_gain, k_gain=k_gain, rpb=rpb, ssm_a_re=ssm_a_re, ssm_a_im=ssm_a_im,
               ssm_b_re=ssm_b_re, ssm_b_im=ssm_b_im, ssm_c_re=ssm_c_re, ssm_c_im=ssm_c_im, ssm_log_step=ssm_log_step,
               ssm_d=ssm_d, w_glu=w_glu, b_glu=b_glu, g_out_attn=g_out_attn, g_out_ssm=g_out_ssm, w_out=w_out,
               g_ffn=g_ffn, w_ffn_gate=w_ffn_gate, w_ffn_up=w_ffn_up, w_ffn_down=w_ffn_down)
    mom = dict(g_mix=m_g_mix, w_in=m_w_in, q_gain=m_q_gain, k_gain=m_k_gain, rpb=m_rpb, ssm_a_re=m_ssm_a_re,
               ssm_a_im=m_ssm_a_im, ssm_b_re=m_ssm_b_re, ssm_b_im=m_ssm_b_im, ssm_c_re=m_ssm_c_re,
               ssm_c_im=m_ssm_c_im, ssm_log_step=m_ssm_log_step, ssm_d=m_ssm_d, w_glu=m_w_glu, b_glu=m_b_glu,
               g_out_attn=m_g_out_attn, g_out_ssm=m_g_out_ssm, w_out=m_w_out, g_ffn=m_g_ffn,
               w_ffn_gate=m_w_ffn_gate, w_ffn_up=m_w_ffn_up, w_ffn_down=m_w_ffn_down)
    var = dict(g_mix=v_g_mix, w_in=v_w_in, q_gain=v_q_gain, k_gain=v_k_gain, rpb=v_rpb, ssm_a_re=v_ssm_a_re,
               ssm_a_im=v_ssm_a_im, ssm_b_re=v_ssm_b_re, ssm_b_im=v_ssm_b_im, ssm_c_re=v_ssm_c_re,
               ssm_c_im=v_ssm_c_im, ssm_log_step=v_ssm_log_step, ssm_d=v_ssm_d, w_glu=v_w_glu, b_glu=v_b_glu,
               g_out_attn=v_g_out_attn, g_out_ssm=v_g_out_ssm, w_out=v_w_out, g_ffn=v_g_ffn,
               w_ffn_gate=v_w_ffn_gate, w_ffn_up=v_w_ffn_up, w_ffn_down=v_w_ffn_down)
    ix, iy, ic = _mesh_place()
    me = _slab(ix, iy, ic)
    d_model = x.shape[-1]

    shard = {n: wts[n][0] for n in BIG}
    full = dict(zip(BIG, _all_gather([shard[n].astype(BF16) for n in BIG], name="gather_weights")))
    w = {"w_in": full["w_in"], "w_glu": full["w_glu"].reshape(-1, shard["w_glu"].shape[-1]),
         "w_out": full["w_out"].reshape(-1, d_model), "w_gate": full["w_ffn_gate"], "w_up": full["w_ffn_up"],
         "w_down": full["w_ffn_down"]}
    p = {n: (wts[n][0].reshape(1, -1) if n in VECTORS else wts[n][0]) for n in SMALL}

    loss, grad_x, g_small, g_big = _local_step(x[0], loss_target[0], p, w)
    loss = lax.psum(loss, ("x", "y", "c"))

    slabs = [g_big[n].reshape((N_DEV,) + shard[n].shape) for n in BIG]
    slabs = [s.reshape(N_DEV, -1, s.shape[-1]) for s in slabs]
    from_sibling = _sibling_exchange(slabs)
    chips = _chips(ix, iy)
    take = lambda a, i: lax.dynamic_index_in_dim(a, i, 0, keepdims=False)
    partials = []
    for i, n in enumerate(BIG):
        own = jnp.stack([take(slabs[i], _slab(px, py, ic)) for px, py in chips[1:]])
        rows, cols = own.shape[1:]
        partials.append(_rowwise(lambda a, b: a + b, [own.reshape(3 * rows, cols),
                                                      from_sibling[i][1:].reshape(3 * rows, cols)], [],
                                 [(cols, F32)], name=f"reduce_add_{n}")[0].reshape(3, rows, cols))
    from_chips = _chip_exchange(partials)
    out = {}
    for i, n in enumerate(BIG):
        rows, cols = slabs[i].shape[1:]
        res = _adamw(shard[n].reshape(rows, cols), mom[n][0].reshape(rows, cols), var[n][0].reshape(rows, cols),
                     [(take(slabs[i], me), None), (from_sibling[i], 1), (from_chips[i], 3)], name=f"adamw_{n}")
        out[n] = [r.reshape(wts[n].shape) for r in res]

    order = list(SMALL)
    shapes = [wts[n].shape for n in order]
    packed = _pack([g_small[n] for n in order])
    gathered = _all_gather([packed], name="gather_small_grads")[0]
    res = _adamw(_pack([wts[n] for n in order]), _pack([mom[n] for n in order]), _pack([var[n] for n in order]),
                 [(gathered, N_DEV)], name="adamw_small")
    for kind, buf in enumerate(res):
        for n, a in zip(order, _unpack(buf, shapes)):
            out.setdefault(n, [None] * 4)[kind] = a

    return (loss, grad_x[None], *[out[n][0] for n in WEIGHTS], *[out[n][1] for n in WEIGHTS],
            *[out[n][2] for n in WEIGHTS], *[out[n][3] for n in WEIGHTS])
```

```python
import functools
import math

import jax
import jax.numpy as jnp
from jax import lax
from jax.experimental import pallas as pl
from jax.experimental.pallas import tpu as pltpu

F32 = jnp.float32
BF16 = jnp.bfloat16

N_DEV = 8
GRID_W = 64
WIN_H = 8
WIN_W = 16
HEAD_DIM = 64
HEADS_PER_GROUP = 4
GROUP_LANES = HEADS_PER_GROUP * HEAD_DIM
SSM_C = 16
SSM_P = 64
GROUPS_PER_TILE = 8
U_TILE = GROUPS_PER_TILE * SSM_C
ST_TILE = GROUPS_PER_TILE * SSM_P
SUBLANES = 8
RMS_EPS = 1e-6
NEG_INF = -1e30
A_RE_MAX = -1e-4
ADAM_LR, ADAM_B1, ADAM_B2, ADAM_EPS, ADAM_WD, ADAM_STEP = 0.001, 0.9, 0.999, 1e-08, 0.01, 10
VMEM_LIMIT_V7X = 56 * 1024 * 1024
MESH = pl.DeviceIdType.MESH

_NN = (((1,), (0,)), ((), ()))
_NT = (((1,), (1,)), ((), ()))
_TN = (((0,), (0,)), ((), ()))
_DIMS = {"nn": _NN, "nt": _NT, "tn": _TN}


def _params(**kw):
    return pltpu.CompilerParams(vmem_limit_bytes=VMEM_LIMIT_V7X, **kw)


def _dot(a, b, dims=_NN):
    return lax.dot_general(a, b, dims, preferred_element_type=F32)


def _mm(a, b, *, name, grid, a_spec, b_spec, o_spec, o_shape, dims, k_axis=None, res=None, out_dtype=F32,
        exact=False):
    dn = _DIMS[dims]
    nk = 1 if k_axis is None else grid[k_axis]
    acc_shape = tuple(d for d in o_spec.block_shape if d is not None)

    def body(*refs):
        if res is not None:
            a_ref, b_ref, r_ref, o_ref, acc = refs
        else:
            a_ref, b_ref, o_ref, acc = refs
            r_ref = None
        if exact:
            p = lax.dot_general(a_ref[...], b_ref[...], dn, precision=lax.Precision.HIGHEST,
                                preferred_element_type=F32)
        else:
            p = _dot(a_ref[...].astype(BF16), b_ref[...].astype(BF16), dn)

        def finish(v):
            if r_ref is not None:
                v = v + r_ref[...].astype(F32)
            o_ref[...] = v.astype(out_dtype)

        if nk == 1:
            finish(p)
        else:
            k = pl.program_id(k_axis)

            @pl.when(k == 0)
            def _():
                acc[...] = p

            @pl.when(k > 0)
            def _():
                acc[...] += p

            @pl.when(k == nk - 1)
            def _():
                finish(acc[...])

    ins = [a, b] + ([res] if res is not None else [])
    in_specs = [a_spec, b_spec] + ([o_spec] if res is not None else [])
    return pl.pallas_call(
        body, name=name, grid=grid, in_specs=in_specs, out_specs=o_spec,
        out_shape=jax.ShapeDtypeStruct(o_shape, out_dtype),
        scratch_shapes=[pltpu.VMEM(acc_shape if nk > 1 else (SUBLANES, 128), F32)],
        compiler_params=_params(),
    )(*ins)


def _tile(n, want):
    if n <= want:
        return n
    t = want
    while n % t:
        t //= 2
    return t


def _mm_plain(a, b, dims, *, name, res=None, out_dtype=F32, tm=512, tn=512, tk=512, exact=False):
    if dims == "nn":
        (m, k), n = a.shape, b.shape[1]
    elif dims == "nt":
        (m, k), n = a.shape, b.shape[0]
    else:
        (k, m), n = a.shape, b.shape[1]
    tm, tn = _tile(m, tm), _tile(n, tn)
    if dims == "tn":
        tk = _tile(k, tk)
        grid = (m // tm, n // tn, k // tk)
        a_spec = pl.BlockSpec((tk, tm), lambda i, j, kk: (kk, i))
        b_spec = pl.BlockSpec((tk, tn), lambda i, j, kk: (kk, j))
        o_spec = pl.BlockSpec((tm, tn), lambda i, j, kk: (i, j))
        return _mm(a, b, name=name, grid=grid, a_spec=a_spec, b_spec=b_spec, o_spec=o_spec, o_shape=(m, n),
                   dims=dims, k_axis=2, res=res, out_dtype=out_dtype)
    grid = (n // tn, m // tm)
    a_spec = pl.BlockSpec((tm, k), lambda j, i: (i, 0))
    if dims == "nn":
        b_spec = pl.BlockSpec((k, tn), lambda j, i: (0, j))
    else:
        b_spec = pl.BlockSpec((tn, k), lambda j, i: (j, 0))
    o_spec = pl.BlockSpec((tm, tn), lambda j, i: (i, j))
    return _mm(a, b, name=name, grid=grid, a_spec=a_spec, b_spec=b_spec, o_spec=o_spec, o_shape=(m, n), dims=dims,
               res=res, out_dtype=out_dtype, exact=exact)


def _rowwise(fn, tiled, bcast, outs, accs=(), *, name, tm=256):
    m = tiled[0].shape[0]
    tm = _tile(m, tm)
    n_t, n_b, n_o, n_a = len(tiled), len(bcast), len(outs), len(accs)

    def body(*refs):
        ins = [r[...] for r in refs[: n_t + n_b]]
        o_refs = refs[n_t + n_b: n_t + n_b + n_o]
        a_refs = refs[n_t + n_b + n_o:]
        res = fn(*ins)
        if not isinstance(res, (tuple, list)):
            res = (res,)
        for r, v in zip(o_refs, res[:n_o]):
            r[...] = v.astype(r.dtype)
        first = pl.program_id(0) == 0
        for r, v in zip(a_refs, res[n_o:]):
            s = jnp.sum(v, axis=0, keepdims=True)

            @pl.when(first)
            def _():
                r[...] = s

            @pl.when(jnp.logical_not(first))
            def _():
                r[...] += s

    in_specs = [pl.BlockSpec((tm, t.shape[1]), lambda i: (i, 0)) for t in tiled]
    in_specs += [pl.BlockSpec(b.shape, lambda i, nd=b.ndim: (0,) * nd) for b in bcast]
    out_specs = [pl.BlockSpec((tm, n), lambda i: (i, 0)) for n, _ in outs]
    out_specs += [pl.BlockSpec((1, n), lambda i: (0, 0)) for n in accs]
    out_shape = [jax.ShapeDtypeStruct((m, n), dt) for n, dt in outs]
    out_shape += [jax.ShapeDtypeStruct((1, n), F32) for n in accs]
    res = pl.pallas_call(body, name=name, grid=(m // tm,), in_specs=in_specs, out_specs=out_specs,
                         out_shape=out_shape, compiler_params=_params())(*tiled, *bcast)
    return res


def _rstd(x):
    return lax.rsqrt(jnp.mean(x * x, axis=-1, keepdims=True) + RMS_EPS)


def _rms_bwd(dh, x, g):
    xh = x * _rstd(x)
    dxh = dh * g
    dx = _rstd(x) * (dxh - xh * jnp.mean(dxh * xh, axis=-1, keepdims=True))
    return dx, dh * xh


def _sigmoid(x):
    return 1.0 / (1.0 + jnp.exp(-x))


_GELU_K = math.sqrt(2.0 / math.pi)
_GELU_C = 0.044715


def _gelu(x):
    return 0.5 * x * (1.0 + jnp.tanh(_GELU_K * (x + _GELU_C * x * x * x)))


def _gelu_grad(x):
    th = jnp.tanh(_GELU_K * (x + _GELU_C * x * x * x))
    return 0.5 * (1.0 + th) + 0.5 * x * (1.0 - th * th) * _GELU_K * (1.0 + 3.0 * _GELU_C * x * x)


def _head_masks():
    lane_head = lax.broadcasted_iota(jnp.int32, (1, GROUP_LANES), 1) // HEAD_DIM
    return [(lane_head == h).astype(F32) for h in range(HEADS_PER_GROUP)]


def _head_block_diag():
    r = lax.broadcasted_iota(jnp.int32, (GROUP_LANES, GROUP_LANES), 0) // HEAD_DIM
    c = lax.broadcasted_iota(jnp.int32, (GROUP_LANES, GROUP_LANES), 1) // HEAD_DIM
    return (r == c).astype(F32)


def _head_mean(x, bd):
    return lax.dot_general(x, bd, _NN, precision=lax.Precision.HIGHEST, preferred_element_type=F32) * (1.0 / HEAD_DIM)


def _stack_heads(x, masks):
    return jnp.concatenate([x * m for m in masks], axis=0)


def _unstack_heads(xs, masks):
    out = xs[0:GRID_W] * masks[0]
    for h in range(1, HEADS_PER_GROUP):
        out = out + xs[h * GRID_W:(h + 1) * GRID_W] * masks[h]
    return out


def _row_start(r, rows):
    return jnp.clip(r - WIN_H // 2, 0, rows - WIN_H)


def _attn_common_specs(seq, n_hg, rows):
    win_keys = WIN_H * GRID_W
    q_spec = pl.BlockSpec((GRID_W, GROUP_LANES), lambda g, r: (r, g))
    k_spec = pl.BlockSpec((seq, GROUP_LANES), lambda g, r: (0, n_hg + g))
    v_spec = pl.BlockSpec((seq, GROUP_LANES), lambda g, r: (0, 2 * n_hg + g))
    gain_spec = pl.BlockSpec((1, GROUP_LANES), lambda g, r: (0, 0))

    def variant(r):
        return _row_start(r, rows) - r + (WIN_H - 1)

    bias_spec = pl.BlockSpec((None, None, HEADS_PER_GROUP, GRID_W, win_keys), lambda g, r: (g, variant(r), 0, 0, 0))
    return q_spec, k_spec, v_spec, gain_spec, bias_spec, variant


def _attn_prepare_kv(k_ref, v_ref, kg, kn_scr, vb_scr, bd, seq):
    chunk = _tile(seq, 512)

    def step(c, carry):
        rows = pl.ds(pl.multiple_of(c * chunk, chunk), chunk)
        k = k_ref[rows, :]
        kn_scr[rows, :] = (k * lax.rsqrt(_head_mean(k * k, bd) + RMS_EPS) * kg).astype(BF16)
        vb_scr[rows, :] = v_ref[rows, :].astype(BF16)
        return carry

    lax.fori_loop(0, seq // chunk, step, 0)


def _attn_probs(qn, kw, bias, masks):
    qs = _stack_heads(qn, masks).astype(BF16)
    s = _dot(qs, kw, _NT) * (1.0 / math.sqrt(HEAD_DIM)) + bias
    m = jnp.max(s, axis=-1, keepdims=True)
    p = jnp.exp(s - m)
    return qs, p / jnp.sum(p, axis=-1, keepdims=True)


def _attn_fwd(z, qg4, kg4, btab):
    seq = z.shape[0]
    a_width = btab.shape[0] * GROUP_LANES
    n_hg, rows, win_keys = btab.shape[0], seq // GRID_W, WIN_H * GRID_W
    q_spec, k_spec, v_spec, gain_spec, bias_spec, _ = _attn_common_specs(seq, n_hg, rows)

    def body(q_ref, k_ref, v_ref, qg_ref, kg_ref, b_ref, o_ref, kn_scr, vb_scr):
        r = pl.program_id(1)
        bd, masks = _head_block_diag(), _head_masks()

        @pl.when(r == 0)
        def _():
            _attn_prepare_kv(k_ref, v_ref, kg_ref[...], kn_scr, vb_scr, bd, seq)

        win = pl.ds(pl.multiple_of(_row_start(r, rows) * GRID_W, GRID_W), win_keys)
        q = q_ref[...]
        qn = q * lax.rsqrt(_head_mean(q * q, bd) + RMS_EPS) * qg_ref[...]
        bias = b_ref[...].reshape(HEADS_PER_GROUP * GRID_W, win_keys)
        _, p = _attn_probs(qn, kn_scr[win, :], bias, masks)
        o_ref[...] = _unstack_heads(_dot(p.astype(BF16), vb_scr[win, :]), masks)

    return pl.pallas_call(
        body, name="attn_fwd", grid=(n_hg, rows),
        in_specs=[q_spec, k_spec, v_spec, gain_spec, gain_spec, bias_spec],
        out_specs=pl.BlockSpec((GRID_W, GROUP_LANES), lambda g, r: (r, g)),
        out_shape=jax.ShapeDtypeStruct((seq, a_width), F32),
        scratch_shapes=[pltpu.VMEM((seq, GROUP_LANES), BF16), pltpu.VMEM((seq, GROUP_LANES), BF16)],
        compiler_params=_params(),
    )(z, z, z, qg4, kg4, btab)


def _attn_bwd(z, d_out, qg4, kg4, btab):
    seq = z.shape[0]
    n_hg, rows, win_keys = btab.shape[0], seq // GRID_W, WIN_H * GRID_W
    a_width = n_hg * GROUP_LANES
    q_spec, k_spec, v_spec, gain_spec, bias_spec, variant = _attn_common_specs(seq, n_hg, rows)
    scale = 1.0 / math.sqrt(HEAD_DIM)

    def body(q_ref, k_ref, v_ref, do_ref, qg_ref, kg_ref, b_ref,
             dq_ref, dk_ref, dv_ref, db_ref, dqg_ref, dkg_ref, kn_scr, vb_scr):
        r = pl.program_id(1)
        bd, masks = _head_block_diag(), _head_masks()

        @pl.when(r == 0)
        def _():
            _attn_prepare_kv(k_ref, v_ref, kg_ref[...], kn_scr, vb_scr, bd, seq)
            dk_ref[...] = jnp.zeros_like(dk_ref)
            dv_ref[...] = jnp.zeros_like(dv_ref)
            db_ref[...] = jnp.zeros_like(db_ref)
            dqg_ref[...] = jnp.zeros_like(dqg_ref)

        win = pl.ds(pl.multiple_of(_row_start(r, rows) * GRID_W, GRID_W), win_keys)
        q, qg = q_ref[...], qg_ref[...]
        rq = lax.rsqrt(_head_mean(q * q, bd) + RMS_EPS)
        qh = q * rq
        kw, vw = kn_scr[win, :], vb_scr[win, :]
        bias = b_ref[...].reshape(HEADS_PER_GROUP * GRID_W, win_keys)
        qs, p = _attn_probs(qh * qg, kw, bias, masks)
        dos = _stack_heads(do_ref[...], masks).astype(BF16)
        dp = _dot(dos, vw, _NT)
        ds = p * (dp - jnp.sum(p * dp, axis=-1, keepdims=True))
        db_ref[variant(r)] += ds.reshape(HEADS_PER_GROUP, GRID_W, win_keys)
        dsb = ds.astype(BF16)
        dqn = _unstack_heads(_dot(dsb, kw), masks) * scale
        dk_ref[win, :] += _dot(dsb, qs, _TN) * scale
        dv_ref[win, :] += _dot(p.astype(BF16), dos, _TN)
        dqg_ref[...] += jnp.sum(dqn * qh, axis=0, keepdims=True)
        dqh = dqn * qg
        dq_ref[...] = rq * (dqh - qh * _head_mean(dqh * qh, bd))

        @pl.when(r == rows - 1)
        def _():
            chunk = _tile(seq, 512)
            kg = kg_ref[...]

            def step(c, dkg):
                rws = pl.ds(pl.multiple_of(c * chunk, chunk), chunk)
                k = k_ref[rws, :]
                rk = lax.rsqrt(_head_mean(k * k, bd) + RMS_EPS)
                kh = k * rk
                dkn = dk_ref[rws, :]
                dkh = dkn * kg
                dk_ref[rws, :] = rk * (dkh - kh * _head_mean(dkh * kh, bd))
                return dkg + jnp.sum(dkn * kh, axis=0, keepdims=True)

            dkg_ref[...] = lax.fori_loop(0, seq // chunk, step, jnp.zeros((1, GROUP_LANES), F32))

    col_spec = pl.BlockSpec((seq, GROUP_LANES), lambda g, r: (0, g))
    gsum_spec = pl.BlockSpec((None, 1, GROUP_LANES), lambda g, r: (g, 0, 0))
    return pl.pallas_call(
        body, name="attn_bwd", grid=(n_hg, rows),
        in_specs=[q_spec, k_spec, v_spec, pl.BlockSpec((GRID_W, GROUP_LANES), lambda g, r: (r, g)),
                  gain_spec, gain_spec, bias_spec],
        out_specs=[pl.BlockSpec((GRID_W, GROUP_LANES), lambda g, r: (r, g)), col_spec, col_spec,
                   pl.BlockSpec((None, WIN_H, HEADS_PER_GROUP, GRID_W, win_keys), lambda g, r: (g, 0, 0, 0, 0)),
                   gsum_spec, gsum_spec],
        out_shape=[jax.ShapeDtypeStruct((seq, a_width), F32)] * 3
        + [jax.ShapeDtypeStruct(btab.shape, F32)]
        + [jax.ShapeDtypeStruct((n_hg, 1, GROUP_LANES), F32)] * 2,
        scratch_shapes=[pltpu.VMEM((seq, GROUP_LANES), BF16), pltpu.VMEM((seq, GROUP_LANES), BF16)],
        compiler_params=_params(),
    )(z, z, z, d_out, qg4, kg4, btab)


def _bias_index():
    c = jnp.arange(GRID_W)
    col_start = jnp.clip(c - WIN_W // 2, 0, GRID_W - WIN_W)
    col_in = (c[None, :] >= col_start[:, None]) & (c[None, :] < col_start[:, None] + WIN_W)
    dc = jnp.clip(c[None, :] - c[:, None], -(WIN_W - 1), WIN_W - 1) + (WIN_W - 1)
    dr = jnp.arange(WIN_H)[:, None] + jnp.arange(WIN_H)[None, :]
    return col_in, dc, dr


def _bias_table(rpb):
    col_in, dc, _ = _bias_index()
    n_h = rpb.shape[0]
    n_hg = n_h // HEADS_PER_GROUP
    spread = ((jnp.arange(128)[:, None] == dc.reshape(1, -1)) & col_in.reshape(1, -1)).astype(F32)
    rows = jnp.stack([rpb[:, v:v + WIN_H] for v in range(WIN_H)], axis=1)
    rows = jnp.pad(rows, ((0, 0), (0, 0), (0, 0), (0, 128 - rows.shape[-1]))).reshape(n_h * WIN_H * WIN_H, 128)
    tab = _mm_plain(rows, spread, "nn", name="rpb_spread", tm=256, tn=2048, exact=True)
    tab = jnp.where(col_in.reshape(1, -1), tab, NEG_INF)
    tab = tab.reshape(n_hg, HEADS_PER_GROUP, WIN_H, WIN_H, GRID_W, GRID_W).transpose(0, 2, 1, 4, 3, 5)
    return tab.reshape(n_hg, WIN_H, HEADS_PER_GROUP, GRID_W, WIN_H * GRID_W)


def _bias_grad(dtab, n_h):
    col_in, dc, _ = _bias_index()
    onehot = (dc.reshape(-1, 1) == jnp.arange(128)[None, :]) & col_in.reshape(-1, 1)
    n_hg = n_h // HEADS_PER_GROUP
    d = dtab.reshape(n_hg, WIN_H, HEADS_PER_GROUP, GRID_W, WIN_H, GRID_W).transpose(0, 2, 1, 4, 3, 5)
    d = d.reshape(n_h * WIN_H * WIN_H, GRID_W * GRID_W)
    diag = _mm_plain(d, onehot.astype(BF16), "nn", name="rpb_diag_sum", tm=256, tn=128)
    diag = diag.reshape(n_h, WIN_H, WIN_H, 128)[..., : 2 * WIN_W - 1]
    out = jnp.zeros((n_h, 2 * WIN_H - 1, 2 * WIN_W - 1), F32)
    for v in range(WIN_H):
        out = out.at[:, v:v + WIN_H].add(diag[:, v])
    return out


def _cmul(ar, ai, br, bi):
    return ar * br - ai * bi, ar * bi + ai * br


def _s5_discretize(a_re, a_im, dt, b_re, b_im):
    c = b_re.shape[1]

    def fn(are, aim, dt_, bre, bim):
        lr, li = jnp.minimum(are, A_RE_MAX), aim
        mag = jnp.exp(lr * dt_)
        l1r, l1i = mag * jnp.cos(li * dt_), mag * jnp.sin(li * dt_)
        den = lr * lr + li * li
        nr, ni = l1r - 1.0, l1i
        cr, ci = (nr * lr + ni * li) / den, (ni * lr - nr * li) / den
        bbr, bbi = _cmul(cr, ci, bre, bim)
        shape = (are.shape[0], SUBLANES)
        lane = lax.broadcasted_iota(jnp.int32, shape, 1)
        pr, pi = l1r, l1i
        acc_r, acc_i = jnp.zeros(shape, F32), jnp.zeros(shape, F32)
        for k in range(SUBLANES):
            acc_r = jnp.where(lane == k, pr, acc_r)
            acc_i = jnp.where(lane == k, pi, acc_i)
            pr, pi = _cmul(pr, pi, l1r, l1i)
        return acc_r, acc_i, cr, ci, bbr, bbi

    return _rowwise(fn, [a_re, a_im, dt, b_re, b_im], [],
                    [(SUBLANES, F32), (SUBLANES, F32), (1, F32), (1, F32), (c, F32), (c, F32)],
                    name="s5_discretize", tm=1024)


def _s5_param_grads(a_re, a_im, dt, b_re, b_im, l1r, l1i, cr, ci, bbr, bbi, r_re, r_im, gb_re, gb_im):
    c = b_re.shape[1]

    def fn(are, aim, dt_, bre, bim, l1r_, l1i_, cr_, ci_, bbr_, bbi_, rr, ri, gbr, gbi):
        lr, li = jnp.minimum(are, A_RE_MAX), aim
        den = lr * lr + li * li
        dbr, dbi = _cmul(cr_, -ci_, gbr, gbi)
        gcr, gci = _cmul(bre, -bim, gbr, gbi)
        gcr, gci = jnp.sum(gcr, axis=1, keepdims=True), jnp.sum(gci, axis=1, keepdims=True)
        qr, qi = _cmul(bbr_, -bbi_, gbr, gbi)
        qr = rr - jnp.sum(qr, axis=1, keepdims=True)
        qi = ri - jnp.sum(qi, axis=1, keepdims=True)
        tr, ti = _cmul(gcr, gci, lr / den, li / den)
        ur, ui = _cmul(l1r_, -l1i_, tr, ti)
        gwr, gwi = qr + ur, qi + ui
        vr, vi = _cmul(cr_, -ci_, lr / den, li / den)
        vr, vi = _cmul(gcr, gci, vr, vi)
        glr, gli = dt_ * gwr - vr, dt_ * gwi - vi
        return jnp.where(are < A_RE_MAX, glr, 0.0), gli, (gwr * lr + gwi * li) * dt_, dbr, dbi

    return _rowwise(fn, [a_re, a_im, dt, b_re, b_im, l1r, l1i, cr, ci, bbr, bbi, r_re, r_im, gb_re, gb_im], [],
                    [(1, F32), (1, F32), (1, F32), (c, F32), (c, F32)], name="s5_param_grads", tm=1024)


def _s5_scan(v, win_re, win_im, tabs, wo_re, wo_im, *, reverse, name, t_chunk=256):
    seq, width = v.shape
    n_tiles, n_state = width // U_TILE, width * (SSM_P // SSM_C)
    t_chunk = _tile(seq, t_chunk)
    n_chunks, n_blk = seq // t_chunk, t_chunk // SUBLANES
    last_row = 0 if reverse else SUBLANES - 1

    def chunk_of(j):
        return (n_chunks - 1 - j) if reverse else j

    def body(v_ref, wir_ref, wii_ref, tab_ref, wor_ref, woi_ref, sr_ref, si_ref, y_ref, carry):
        @pl.when(pl.program_id(0) == 0)
        def _():
            carry[...] = jnp.zeros_like(carry)

        for jt in range(n_tiles):
            ls = slice(jt * ST_TILE, (jt + 1) * ST_TILE)
            us = slice(jt * U_TILE, (jt + 1) * U_TILE)
            vj = v_ref[:, us].astype(BF16)
            sr_ref[:, ls] = _dot(vj, wir_ref[jt])
            si_ref[:, ls] = _dot(vj, wii_ref[jt])
            consts = [tab_ref[k, :, ls] for k in range(8)]

            def blk(b, c, ls=ls, consts=consts):
                cr, ci = c
                bb = (n_blk - 1 - b) if reverse else b
                rows = pl.ds(pl.multiple_of(bb * SUBLANES, SUBLANES), SUBLANES)
                xr, xi = sr_ref[rows, ls], si_ref[rows, ls]
                for s, k in enumerate((1, 2, 4)):
                    sh = (SUBLANES - k) if reverse else k
                    tr, ti = pltpu.roll(xr, sh, 0), pltpu.roll(xi, sh, 0)
                    lr, li = consts[2 * s], consts[2 * s + 1]
                    xr, xi = xr + lr * tr - li * ti, xi + lr * ti + li * tr
                lr, li = consts[6], consts[7]
                xr, xi = xr + lr * cr - li * ci, xi + lr * ci + li * cr
                sr_ref[rows, ls], si_ref[rows, ls] = xr, xi
                shape = (SUBLANES, ST_TILE)
                return (jnp.broadcast_to(xr[last_row:last_row + 1], shape),
                        jnp.broadcast_to(xi[last_row:last_row + 1], shape))

            cr, ci = lax.fori_loop(0, n_blk, blk, (carry[0, :, ls], carry[1, :, ls]), unroll=4)
            carry[0, :, ls], carry[1, :, ls] = cr, ci
            y_ref[:, us] = (_dot(sr_ref[:, ls].astype(BF16), wor_ref[jt])
                            + _dot(si_ref[:, ls].astype(BF16), woi_ref[jt]))

    whole = lambda a: pl.BlockSpec(a.shape, lambda j, nd=a.ndim: (0,) * nd)
    st_spec = pl.BlockSpec((t_chunk, n_state), lambda j: (chunk_of(j), 0))
    v_spec = pl.BlockSpec((t_chunk, width), lambda j: (chunk_of(j), 0))
    return pl.pallas_call(
        body, name=name, grid=(n_chunks,),
        in_specs=[v_spec, whole(win_re), whole(win_im), whole(tabs), whole(wo_re), whole(wo_im)],
        out_specs=[st_spec, st_spec, v_spec],
        out_shape=[jax.ShapeDtypeStruct((seq, n_state), F32)] * 2 + [jax.ShapeDtypeStruct((seq, width), F32)],
        scratch_shapes=[pltpu.VMEM((2, SUBLANES, n_state), F32)],
        compiler_params=_params(),
    )(v, win_re, win_im, tabs, wo_re, wo_im)


def _s5_reduce(x_re, x_im, a_re, a_im, u, dy, *, name, t_chunk=512):
    seq, n_state = x_re.shape
    width = u.shape[1]
    n_tiles = width // U_TILE
    t_chunk = _tile(seq, t_chunk)

    def body(xr_ref, xi_ref, ar_ref, ai_ref, u_ref, dy_ref, rr_ref, ri_ref, gbr_ref, gbi_ref, gcr_ref, gci_ref):
        xr, xi, ar, ai = xr_ref[...], xi_ref[...], ar_ref[...], ai_ref[...]
        ub, dyb = u_ref[...].astype(BF16), dy_ref[...].astype(BF16)
        parts = (jnp.sum(ar * xr + ai * xi, axis=0, keepdims=True), jnp.sum(ai * xr - ar * xi, axis=0, keepdims=True),
                 _dot(ar.astype(BF16), ub, _TN), _dot(ai.astype(BF16), ub, _TN),
                 _dot(xr.astype(BF16), dyb, _TN), _dot(xi.astype(BF16), dyb, _TN))
        first = pl.program_id(1) == 0
        for ref, val in zip((rr_ref, ri_ref, gbr_ref, gbi_ref, gcr_ref, gci_ref), parts):
            @pl.when(first)
            def _():
                ref[...] = val

            @pl.when(jnp.logical_not(first))
            def _():
                ref[...] += val

    st_spec = pl.BlockSpec((t_chunk, ST_TILE), lambda j, t: (t, j))
    u_spec = pl.BlockSpec((t_chunk, U_TILE), lambda j, t: (t, j))
    r_spec = pl.BlockSpec((1, ST_TILE), lambda j, t: (0, j))
    g_spec = pl.BlockSpec((None, ST_TILE, U_TILE), lambda j, t: (j, 0, 0))
    return pl.pallas_call(
        body, name=name, grid=(n_tiles, seq // t_chunk),
        in_specs=[st_spec] * 4 + [u_spec] * 2,
        out_specs=[r_spec, r_spec] + [g_spec] * 4,
        out_shape=[jax.ShapeDtypeStruct((1, n_state), F32)] * 2
        + [jax.ShapeDtypeStruct((n_tiles, ST_TILE, U_TILE), F32)] * 4,
        compiler_params=_params(),
    )(x_re, x_im, a_re, a_im, u, dy)


def _block_diag_in(ms):
    m = jnp.stack(ms)
    n, g, c, p = m.shape
    m5 = m.reshape(n, g // GROUPS_PER_TILE, GROUPS_PER_TILE, c, p)
    eye = jnp.eye(GROUPS_PER_TILE, dtype=m.dtype)
    out = m5[:, :, :, :, None, :] * eye[None, None, :, None, :, None]
    return out.astype(BF16).reshape(n, g // GROUPS_PER_TILE, GROUPS_PER_TILE * c, GROUPS_PER_TILE * p)


def _block_diag_take(m, c, p):
    t = m.shape[0]
    m5 = m.reshape(t, GROUPS_PER_TILE, p, GROUPS_PER_TILE, c)
    idx = jnp.arange(GROUPS_PER_TILE)
    return m5[:, idx, :, idx, :].transpose(1, 0, 2, 3).reshape(t * GROUPS_PER_TILE, p, c)


def _scan_tables(pw_re, pw_im, reverse):
    row = jnp.arange(SUBLANES)[:, None]
    tabs = []
    for k in (1, 2, 4):
        keep = (row <= SUBLANES - 1 - k) if reverse else (row >= k)
        tabs += [jnp.where(keep, pw_re[k - 1][None, :], 0.0), jnp.where(keep, pw_im[k - 1][None, :], 0.0)]
    order = jnp.arange(SUBLANES)[::-1] if reverse else jnp.arange(SUBLANES)
    tabs += [pw_re[order], pw_im[order]]
    return jnp.stack(tabs)


def _local_step(x, target, p, w):
    seq, d_model = x.shape
    a_width = p["g_out_attn"].shape[-1]
    s_width = p["g_out_ssm"].shape[-1]
    n_heads = a_width // HEAD_DIM
    n_hg = n_heads // HEADS_PER_GROUP
    n_groups = s_width // SSM_C
    n_sh, _, in_sh = w["w_in"].shape
    f_sh = w["w_gate"].shape[2]
    tm = _tile(seq, 512)
    n_i = seq // tm

    h1 = _rowwise(lambda xv, g: x_norm(xv, g), [x], [p["g_mix"]], [(d_model, BF16)], name="rms_mix")[0]
    z = _mm(h1, w["w_in"], name="in_proj", grid=(n_sh, n_i),
            a_spec=pl.BlockSpec((tm, d_model), lambda j, i: (i, 0)),
            b_spec=pl.BlockSpec((None, d_model, in_sh), lambda j, i: (j, 0, 0)),
            o_spec=pl.BlockSpec((tm, in_sh), lambda j, i: (i, j)), o_shape=(seq, n_sh * in_sh), dims="nn")
    qg4 = jnp.tile(p["q_gain"], (1, HEADS_PER_GROUP))
    kg4 = jnp.tile(p["k_gain"], (1, HEADS_PER_GROUP))
    btab = _bias_table(p["rpb"])
    ya = _attn_fwd(z, qg4, kg4, btab)
    u = z[:, 3 * a_width:]

    n_col = 2 * n_groups * SSM_P
    col = lambda a: a.reshape(n_col, 1)
    a_re_c, a_im_c = col(p["ssm_a_re"]), col(p["ssm_a_im"])
    dt_c = col(jnp.broadcast_to(jnp.exp(p["ssm_log_step"])[:, :, None], (2, n_groups, SSM_P)))
    b_re_c, b_im_c = p["ssm_b_re"].reshape(n_col, SSM_C), p["ssm_b_im"].reshape(n_col, SSM_C)
    pw_re, pw_im, cf_re, cf_im, bb_re, bb_im = _s5_discretize(a_re_c, a_im_c, dt_c, b_re_c, b_im_c)
    n_state = n_groups * SSM_P
    pw_re = pw_re.reshape(2, n_state, SUBLANES).transpose(0, 2, 1)
    pw_im = pw_im.reshape(2, n_state, SUBLANES).transpose(0, 2, 1)
    bb_re4, bb_im4 = bb_re.reshape(2, n_groups, SSM_P, SSM_C), bb_im.reshape(2, n_groups, SSM_P, SSM_C)
    c_re, c_im = p["ssm_c_re"], p["ssm_c_im"]
    t21 = lambda a: a.transpose(0, 2, 1)
    maps_in = _block_diag_in([m for d in range(2) for m in (t21(bb_re4[d]), t21(bb_im4[d]), c_re[d], -c_im[d])])
    maps_out = _block_diag_in([m for d in range(2) for m in (t21(c_re[d]), -t21(c_im[d]), bb_re4[d], bb_im4[d])])
    fwd, bwd_in = [], []
    for d in range(2):
        rev = d == 1
        tabs = _scan_tables(pw_re[d], pw_im[d], rev)
        xs_re, xs_im, y_d = _s5_scan(u, maps_in[4 * d], maps_in[4 * d + 1], tabs, maps_out[4 * d],
                                     maps_out[4 * d + 1], reverse=rev, name=f"s5_fwd_{d}")
        fwd.append((xs_re, xs_im, y_d))
        bwd_in.append((maps_in[4 * d + 2], maps_in[4 * d + 3], _scan_tables(pw_re[d], -pw_im[d], not rev),
                       maps_out[4 * d + 2], maps_out[4 * d + 3]))

    ypre, yg = _rowwise(lambda y0, y1, uv, dsk: s5_mid(y0, y1, uv, dsk), [fwd[0][2], fwd[1][2], u], [p["ssm_d"]],
                        [(s_width, F32), (s_width, F32)], name="s5_skip_gelu")
    t_glu = _mm_plain(yg, w["w_glu"], "nn", name="glu_proj", tn=s_width)
    y_cat = _rowwise(mix_out_fwd, [ya, yg, t_glu], [p["b_glu"], p["g_out_attn"], p["g_out_ssm"]],
                     [(a_width + s_width, BF16)], name="mix_out")[0]
    x1 = _mm_plain(y_cat, w["w_out"], "nn", name="out_proj", res=x)

    h2 = _rowwise(lambda xv, g: x_norm(xv, g), [x1], [p["g_ffn"]], [(d_model, BF16)], name="rms_ffn")[0]
    ffn_up = functools.partial(
        _mm, grid=(n_sh, n_i), a_spec=pl.BlockSpec((tm, d_model), lambda j, i: (i, 0)),
        b_spec=pl.BlockSpec((None, d_model, f_sh), lambda j, i: (j, 0, 0)),
        o_spec=pl.BlockSpec((None, tm, f_sh), lambda j, i: (j, i, 0)), o_shape=(n_sh, seq, f_sh), dims="nn")
    gate = ffn_up(h2, w["w_gate"], name="ffn_gate")
    up = ffn_up(h2, w["w_up"], name="ffn_up")
    flat = lambda a: a.reshape(n_sh * seq, f_sh)
    act = _rowwise(lambda gv, uv: gv * _sigmoid(gv) * uv, [flat(gate), flat(up)], [], [(f_sh, BF16)],
                   name="swiglu", tm=512)[0].reshape(n_sh, seq, f_sh)
    tn_d = _tile(d_model, 2048)
    x2 = _mm(act, w["w_down"], name="ffn_down", grid=(d_model // tn_d, n_i, n_sh),
             a_spec=pl.BlockSpec((None, tm, f_sh), lambda n, i, j: (j, i, 0)),
             b_spec=pl.BlockSpec((None, f_sh, tn_d), lambda n, i, j: (j, 0, n)),
             o_spec=pl.BlockSpec((tm, tn_d), lambda n, i, j: (i, n)), o_shape=(seq, d_model), dims="nn", k_axis=2,
             res=x1)

    dx2, sq = _rowwise(lambda xv, tv: ((xv - tv) * (1.0 / d_model), (xv - tv) * (xv - tv)), [x2, target], [],
                       [(d_model, F32)], [d_model], name="loss_head")
    loss = 0.5 * jnp.sum(sq) / d_model

    d_act = _mm(dx2, w["w_down"], name="ffn_down_dx", grid=(n_sh, n_i),
                a_spec=pl.BlockSpec((tm, d_model), lambda j, i: (i, 0)),
                b_spec=pl.BlockSpec((None, f_sh, d_model), lambda j, i: (j, 0, 0)),
                o_spec=pl.BlockSpec((None, tm, f_sh), lambda j, i: (j, i, 0)), o_shape=(n_sh, seq, f_sh), dims="nt")
    g_w_down = _mm(act, dx2, name="ffn_down_dw", grid=(n_sh, d_model // tn_d, n_i),
                   a_spec=pl.BlockSpec((None, tm, f_sh), lambda j, n, k: (j, k, 0)),
                   b_spec=pl.BlockSpec((tm, tn_d), lambda j, n, k: (k, n)),
                   o_spec=pl.BlockSpec((None, f_sh, tn_d), lambda j, n, k: (j, 0, n)),
                   o_shape=(n_sh, f_sh, d_model), dims="tn", k_axis=2)
    d_gate, d_up = _rowwise(swiglu_bwd, [flat(d_act), flat(gate), flat(up)], [], [(f_sh, BF16), (f_sh, BF16)],
                            name="swiglu_bwd", tm=512)
    d_gate, d_up = d_gate.reshape(n_sh, seq, f_sh), d_up.reshape(n_sh, seq, f_sh)
    ffn_dx = functools.partial(
        _mm, grid=(n_i, n_sh), a_spec=pl.BlockSpec((None, tm, f_sh), lambda i, j: (j, i, 0)),
        b_spec=pl.BlockSpec((None, d_model, f_sh), lambda i, j: (j, 0, 0)),
        o_spec=pl.BlockSpec((tm, d_model), lambda i, j: (i, 0)), o_shape=(seq, d_model), dims="nt", k_axis=1)
    d_h2 = ffn_dx(d_gate, w["w_gate"], name="ffn_gate_dx")
    d_h2 = ffn_dx(d_up, w["w_up"], name="ffn_up_dx", res=d_h2)
    ffn_dw = functools.partial(
        _mm, grid=(n_sh, n_i), a_spec=pl.BlockSpec((tm, d_model), lambda j, k: (k, 0)),
        b_spec=pl.BlockSpec((None, tm, f_sh), lambda j, k: (j, k, 0)),
        o_spec=pl.BlockSpec((None, d_model, f_sh), lambda j, k: (j, 0, 0)), o_shape=(n_sh, d_model, f_sh), dims="tn",
        k_axis=1)
    g_w_gate = ffn_dw(h2, d_gate, name="ffn_gate_dw")
    g_w_up = ffn_dw(h2, d_up, name="ffn_up_dw")
    dx1, g_g_ffn = _rowwise(residual_rms_bwd, [dx2, d_h2, x1], [p["g_ffn"]], [(d_model, F32)], [d_model],
                            name="rms_ffn_bwd")

    d_ycat = _mm_plain(dx1, w["w_out"], "nt", name="out_proj_dx")
    g_w_out = _mm_plain(y_cat, dx1, "tn", name="out_proj_dw", tm=1024, tn=2048)
    (d_ya, d_yg_direct, d_t, g_goa, g_gos, g_b_glu) = _rowwise(
        functools.partial(mix_out_bwd, a_width=a_width), [d_ycat, ya, yg, t_glu],
        [p["b_glu"], p["g_out_attn"], p["g_out_ssm"]],
        [(a_width, F32), (s_width, F32), (s_width, BF16)], [a_width, s_width, s_width], name="mix_out_bwd")
    d_yg = _mm_plain(d_t, w["w_glu"], "nt", name="glu_proj_dx", res=d_yg_direct, tn=s_width)
    g_w_glu = _mm_plain(yg, d_t, "tn", name="glu_proj_dw", tm=1024, tn=1024)
    d_ypre, du_skip, g_ssm_d = _rowwise(gelu_skip_bwd, [d_yg, ypre, u], [p["ssm_d"]],
                                        [(s_width, F32), (s_width, F32)], [s_width], name="s5_skip_gelu_bwd")

    du_dirs, r_parts, gb_parts, gc_parts = [], [], [], []
    for d in range(2):
        win_re, win_im, tabs, wo_re, wo_im = bwd_in[d]
        as_re, as_im, du_d = _s5_scan(d_ypre, win_re, win_im, tabs, wo_re, wo_im, reverse=(d == 0),
                                      name=f"s5_bwd_{d}")
        du_dirs.append(du_d)
        r_re, r_im, gbt_re, gbt_im, gct_re, gct_im = _s5_reduce(fwd[d][0], fwd[d][1], as_re, as_im, u, d_ypre,
                                                                name=f"s5_reduce_{d}")
        r_parts.append((r_re.reshape(n_state, 1), r_im.reshape(n_state, 1)))
        gb_parts.append((_block_diag_take(gbt_re, SSM_C, SSM_P), _block_diag_take(gbt_im, SSM_C, SSM_P)))
        gc_parts.append((_block_diag_take(gct_re, SSM_C, SSM_P), _block_diag_take(gct_im, SSM_C, SSM_P)))
    cat = lambda i, parts: jnp.concatenate([parts[0][i], parts[1][i]], axis=0)
    gbb_re, gbb_im = cat(0, gb_parts).reshape(n_col, SSM_C), cat(1, gb_parts).reshape(n_col, SSM_C)
    g_a_re, g_a_im, g_ls, g_b_re, g_b_im = _s5_param_grads(
        a_re_c, a_im_c, dt_c, b_re_c, b_im_c, pw_re[:, 0].reshape(n_col, 1), pw_im[:, 0].reshape(n_col, 1),
        cf_re, cf_im, bb_re, bb_im, cat(0, r_parts), cat(1, r_parts), gbb_re, gbb_im)
    g_c_re = cat(0, gc_parts).reshape(2, n_groups, SSM_P, SSM_C).transpose(0, 1, 3, 2)
    g_c_im = -cat(1, gc_parts).reshape(2, n_groups, SSM_P, SSM_C).transpose(0, 1, 3, 2)

    d_q, d_k, d_v, d_btab, g_qg, g_kg = _attn_bwd(z, d_ya, qg4, kg4, btab)
    d_u = _rowwise(lambda a, b, c: a + b + c, [du_dirs[0], du_dirs[1], du_skip], [], [(s_width, F32)],
                   name="s5_du_sum")[0]
    d_z = jnp.concatenate([d_q, d_k, d_v, d_u], axis=1)
    d_h1 = _mm(d_z, w["w_in"], name="in_proj_dx", grid=(n_i, n_sh),
               a_spec=pl.BlockSpec((tm, in_sh), lambda i, j: (i, j)),
               b_spec=pl.BlockSpec((None, d_model, in_sh), lambda i, j: (j, 0, 0)),
               o_spec=pl.BlockSpec((tm, d_model), lambda i, j: (i, 0)), o_shape=(seq, d_model), dims="nt", k_axis=1)
    g_w_in = _mm(h1, d_z, name="in_proj_dw", grid=(n_sh, n_i),
                 a_spec=pl.BlockSpec((tm, d_model), lambda j, k: (k, 0)),
                 b_spec=pl.BlockSpec((tm, in_sh), lambda j, k: (k, j)),
                 o_spec=pl.BlockSpec((None, d_model, in_sh), lambda j, k: (j, 0, 0)),
                 o_shape=(n_sh, d_model, in_sh), dims="tn", k_axis=1)
    grad_x, g_g_mix = _rowwise(residual_rms_bwd, [dx1, d_h1, x], [p["g_mix"]], [(d_model, F32)], [d_model],
                               name="rms_mix_bwd")

    fold_heads = lambda g: g.reshape(n_heads, HEAD_DIM).sum(axis=0, keepdims=True)
    small = {
        "g_mix": g_g_mix, "q_gain": fold_heads(g_qg), "k_gain": fold_heads(g_kg),
        "rpb": _bias_grad(d_btab, n_heads),
        "ssm_a_re": g_a_re.reshape(2, n_groups, SSM_P), "ssm_a_im": g_a_im.reshape(2, n_groups, SSM_P),
        "ssm_b_re": g_b_re.reshape(2, n_groups, SSM_P, SSM_C), "ssm_b_im": g_b_im.reshape(2, n_groups, SSM_P, SSM_C),
        "ssm_c_re": g_c_re, "ssm_c_im": g_c_im,
        "ssm_log_step": g_ls.reshape(2, n_groups, SSM_P).sum(axis=-1),
        "ssm_d": g_ssm_d, "b_glu": g_b_glu, "g_out_attn": g_goa, "g_out_ssm": g_gos, "g_ffn": g_g_ffn,
    }
    big = {"w_in": g_w_in, "w_glu": g_w_glu, "w_out": g_w_out, "w_ffn_gate": g_w_gate, "w_ffn_up": g_w_up,
           "w_ffn_down": g_w_down}
    return loss, grad_x, small, big


def x_norm(xv, g):
    return xv * _rstd(xv) * g


def s5_mid(y0, y1, uv, d_skip):
    ypre = y0 + y1 + d_skip * uv
    return ypre, _gelu(ypre)


def mix_out_fwd(ya, yg, t, b_glu, g_oa, g_os):
    ys = yg * _sigmoid(t + b_glu)
    return jnp.concatenate([ya * _rstd(ya) * g_oa, ys * _rstd(ys) * g_os], axis=1)


def mix_out_bwd(d_y, ya, yg, t, b_glu, g_oa, g_os, *, a_width):
    sg = _sigmoid(t + b_glu)
    ys = yg * sg
    d_ya, c_goa = _rms_bwd(d_y[:, :a_width], ya, g_oa)
    d_ys, c_gos = _rms_bwd(d_y[:, a_width:], ys, g_os)
    d_t = d_ys * yg * sg * (1.0 - sg)
    return d_ya, d_ys * sg, d_t, c_goa, c_gos, d_t


def gelu_skip_bwd(d_yg, ypre, uv, d_skip):
    d_ypre = d_yg * _gelu_grad(ypre)
    return d_ypre, d_ypre * d_skip, d_ypre * uv


def swiglu_bwd(d_act, gv, uv):
    sg = _sigmoid(gv)
    return d_act * uv * (sg * (1.0 + gv * (1.0 - sg))), d_act * gv * sg


def residual_rms_bwd(d_res, d_h, xv, g):
    dx, c_g = _rms_bwd(d_h, xv, g)
    return d_res + dx, c_g


_ANY = pl.BlockSpec(memory_space=pl.ANY)


def _mesh_place():
    return lax.axis_index("x"), lax.axis_index("y"), lax.axis_index("c")


def _chips(x, y):
    return [(x, y), (1 - x, y), (x, 1 - y), (1 - x, 1 - y)]


def _slab(px, py, pc):
    return 4 * px + 2 * py + pc


def _all_gather(arrs, *, name):
    n = len(arrs)

    def body(*refs):
        in_refs, out_refs = refs[:n], refs[n:2 * n]
        send_sems, recv_sems, local_sems = refs[2 * n:]
        x, y, c = _mesh_place()
        me, sibling = (x, y, c), (x, y, 1 - c)
        others = _chips(x, y)[1:]

        def copy(w, k, block, to, src=None):
            dst = out_refs[w].at[_slab(*block)]
            return pltpu.make_async_remote_copy(
                src_ref=dst if src is None else src, dst_ref=dst, send_sem=send_sems.at[7 * w + k],
                recv_sem=recv_sems.at[7 * w + k], device_id=to, device_id_type=MESH)

        mine = [pltpu.make_async_copy(in_refs[w], out_refs[w].at[_slab(*me)], local_sems.at[w]) for w in range(n)]
        first = []
        for w in range(n):
            mine[w].start()
            first.append(copy(w, 0, me, sibling, src=in_refs[w]))
            first += [copy(w, 1 + j, me, (*chip, c), src=in_refs[w]) for j, chip in enumerate(others)]
        for cp in first:
            cp.start()
        passed = []
        for j, chip in enumerate(others):
            for w in range(n):
                copy(w, 1 + j, (*chip, c), me).wait_recv()
                fwd = copy(w, 4 + j, (*chip, c), sibling)
                fwd.start()
                passed.append(fwd)
        for w in range(n):
            copy(w, 0, sibling, me).wait_recv()
        for j, chip in enumerate(others):
            for w in range(n):
                copy(w, 4 + j, (*chip, 1 - c), me).wait_recv()
        for cp in first + passed:
            cp.wait_send()
        for cp in mine:
            cp.wait()

    return pl.pallas_call(
        body, name=name, in_specs=[_ANY] * n, out_specs=[_ANY] * n,
        out_shape=[jax.ShapeDtypeStruct((N_DEV,) + a.shape, a.dtype) for a in arrs],
        scratch_shapes=[pltpu.SemaphoreType.DMA((7 * n,)), pltpu.SemaphoreType.DMA((7 * n,)),
                        pltpu.SemaphoreType.DMA((n,))],
        compiler_params=pltpu.CompilerParams(has_side_effects=True),
    )(*arrs)


def _swap(arrs, n_out, plan, *, name):
    n = len(arrs)

    def body(*refs):
        in_refs, out_refs = refs[:n], refs[n:2 * n]
        send_sems, recv_sems = refs[2 * n:]
        copies = []
        for k in range(n_out):
            for w, (src, dst, to) in enumerate(plan(in_refs, out_refs, k)):
                copies.append(pltpu.make_async_remote_copy(
                    src_ref=src, dst_ref=dst, send_sem=send_sems.at[n_out * w + k],
                    recv_sem=recv_sems.at[n_out * w + k], device_id=to, device_id_type=MESH))
        for cp in copies:
            cp.start()
        for cp in copies:
            cp.wait_recv()
        for cp in copies:
            cp.wait_send()

    return pl.pallas_call(
        body, name=name, in_specs=[_ANY] * n, out_specs=[_ANY] * n,
        out_shape=[jax.ShapeDtypeStruct((n_out,) + a.shape[1:], a.dtype) for a in arrs],
        scratch_shapes=[pltpu.SemaphoreType.DMA((n_out * n,)), pltpu.SemaphoreType.DMA((n_out * n,))],
        compiler_params=pltpu.CompilerParams(has_side_effects=True),
    )(*arrs)


def _sibling_exchange(grads):
    def plan(in_refs, out_refs, k):
        x, y, c = _mesh_place()
        px, py = _chips(x, y)[k]
        return [(g.at[_slab(px, py, 1 - c)], o.at[k], (x, y, 1 - c)) for g, o in zip(in_refs, out_refs)]

    return _swap(grads, 4, plan, name="reduce_sibling")


def _chip_exchange(partials):
    def plan(in_refs, out_refs, k):
        x, y, c = _mesh_place()
        px, py = _chips(x, y)[k + 1]
        return [(g.at[k], o.at[k], (px, py, c)) for g, o in zip(in_refs, out_refs)]

    return _swap(partials, 3, plan, name="reduce_chips")


def _adamw(w, m, v, parts, *, name, tr=256):
    rows, cols = w.shape
    tr = _tile(rows, tr)
    n_p = len(parts)

    def body(*refs):
        w_ref, m_ref, v_ref = refs[:3]
        p_refs = refs[3:3 + n_p]
        g_ref, d_ref, nm_ref, nv_ref = refs[3 + n_p:]
        g = None
        for (_, lead), r in zip(parts, p_refs):
            for piece in ([r[...]] if lead is None else [r[i] for i in range(lead)]):
                g = piece.astype(F32) if g is None else g + piece.astype(F32)
        new_m = ADAM_B1 * m_ref[...] + (1.0 - ADAM_B1) * g
        new_v = ADAM_B2 * v_ref[...] + (1.0 - ADAM_B2) * (g * g)
        m_hat = new_m / (1.0 - ADAM_B1 ** ADAM_STEP)
        v_hat = new_v / (1.0 - ADAM_B2 ** ADAM_STEP)
        g_ref[...] = g
        d_ref[...] = -ADAM_LR * (m_hat / (jnp.sqrt(v_hat) + ADAM_EPS) + ADAM_WD * w_ref[...])
        nm_ref[...] = new_m
        nv_ref[...] = new_v

    tile = pl.BlockSpec((tr, cols), lambda i: (i, 0))
    p_specs = [tile if lead is None else pl.BlockSpec((lead, tr, cols), lambda i: (0, i, 0)) for _, lead in parts]
    return pl.pallas_call(
        body, name=name, grid=(rows // tr,), in_specs=[tile] * 3 + p_specs, out_specs=[tile] * 4,
        out_shape=[jax.ShapeDtypeStruct((rows, cols), F32)] * 4, compiler_params=_params(),
    )(w, m, v, *[a for a, _ in parts])


_PACK_TILE = SUBLANES * 128
_PACK_ROWS = 512


def _pack(arrs):
    flat = []
    for a in arrs:
        f = a.reshape(-1)
        flat.append(jnp.pad(f, (0, (-f.shape[0]) % _PACK_TILE)))
    total = sum(f.shape[0] for f in flat)
    flat.append(jnp.zeros(((-total) % (_PACK_ROWS * 128),), F32))
    return jnp.concatenate(flat).reshape(-1, 128)


def _unpack(buf, shapes):
    out, at = [], 0
    flat = buf.reshape(-1)
    for s in shapes:
        n = math.prod(s)
        out.append(flat[at:at + n].reshape(s))
        at += n + (-n) % _PACK_TILE
    return out


BIG = ("w_in", "w_glu", "w_out", "w_ffn_gate", "w_ffn_up", "w_ffn_down")
WEIGHTS = ("g_mix", "w_in", "q_gain", "k_gain", "rpb", "ssm_a_re", "ssm_a_im", "ssm_b_re", "ssm_b_im", "ssm_c_re",
           "ssm_c_im", "ssm_log_step", "ssm_d", "w_glu", "b_glu", "g_out_attn", "g_out_ssm", "w_out", "g_ffn",
           "w_ffn_gate", "w_ffn_up", "w_ffn_down")
SMALL = tuple(n for n in WEIGHTS if n not in BIG)
VECTORS = ("g_mix", "q_gain", "k_gain", "ssm_d", "b_glu", "g_out_attn", "g_out_ssm", "g_ffn")


def kernel(x, g_mix, w_in, q_gain, k_gain, rpb, ssm_a_re, ssm_a_im, ssm_b_re, ssm_b_im, ssm_c_re, ssm_c_im, ssm_log_step, ssm_d, w_glu, b_glu, g_out_attn, g_out_ssm, w_out, g_ffn, w_ffn_gate, w_ffn_up, w_ffn_down, loss_target, m_g_mix, m_w_in, m_q_gain, m_k_gain, m_rpb, m_ssm_a_re, m_ssm_a_im, m_ssm_b_re, m_ssm_b_im, m_ssm_c_re, m_ssm_c_im, m_ssm_log_step, m_ssm_d, m_w_glu, m_b_glu, m_g_out_attn, m_g_out_ssm, m_w_out, m_g_ffn, m_w_ffn_gate, m_w_ffn_up, m_w_ffn_down, v_g_mix, v_w_in, v_q_gain, v_k_gain, v_rpb, v_ssm_a_re, v_ssm_a_im, v_ssm_b_re, v_ssm_b_im, v_ssm_c_re, v_ssm_c_im, v_ssm_log_step, v_ssm_d, v_w_glu, v_b_glu, v_g_out_attn, v_g_out_ssm, v_w_out, v_g_ffn, v_w_ffn_gate, v_w_ffn_up, v_w_ffn_down):
    wts = dict(g_mix=g_mix, w_in=w_in, q_gain=q_gain, k_gain=k_gain, rpb=rpb, ssm_a_re=ssm_a_re, ssm_a_im=ssm_a_im,
               ssm_b_re=ssm_b_re, ssm_b_im=ssm_b_im, ssm_c_re=ssm_c_re, ssm_c_im=ssm_c_im, ssm_log_step=ssm_log_step,
               ssm_d=ssm_d, w_glu=w_glu, b_glu=b_glu, g_out_attn=g_out_attn, g_out_ssm=g_out_ssm, w_out=w_out,
               g_ffn=g_ffn, w_ffn_gate=w_ffn_gate, w_ffn_up=w_ffn_up, w_ffn_down=w_ffn_down)
    mom = dict(g_mix=m_g_mix, w_in=m_w_in, q_gain=m_q_gain, k_gain=m_k_gain, rpb=m_rpb, ssm_a_re=m_ssm_a_re,
               ssm_a_im=m_ssm_a_im, ssm_b_re=m_ssm_b_re, ssm_b_im=m_ssm_b_im, ssm_c_re=m_ssm_c_re,
               ssm_c_im=m_ssm_c_im, ssm_log_step=m_ssm_log_step, ssm_d=m_ssm_d, w_glu=m_w_glu, b_glu=m_b_glu,
               g_out_attn=m_g_out_attn, g_out_ssm=m_g_out_ssm, w_out=m_w_out, g_ffn=m_g_ffn,
               w_ffn_gate=m_w_ffn_gate, w_ffn_up=m_w_ffn_up, w_ffn_down=m_w_ffn_down)
    var = dict(g_mix=v_g_mix, w_in=v_w_in, q_gain=v_q_gain, k_gain=v_k_gain, rpb=v_rpb, ssm_a_re=v_ssm_a_re,
               ssm_a_im=v_ssm_a_im, ssm_b_re=v_ssm_b_re, ssm_b_im=v_ssm_b_im, ssm_c_re=v_ssm_c_re,
               ssm_c_im=v_ssm_c_im, ssm_log_step=v_ssm_log_step, ssm_d=v_ssm_d, w_glu=v_w_glu, b_glu=v_b_glu,
               g_out_attn=v_g_out_attn, g_out_ssm=v_g_out_ssm, w_out=v_w_out, g_ffn=v_g_ffn,
               w_ffn_gate=v_w_ffn_gate, w_ffn_up=v_w_ffn_up, w_ffn_down=v_w_ffn_down)
    ix, iy, ic = _mesh_place()
    me = _slab(ix, iy, ic)
    d_model = x.shape[-1]

    shard = {n: wts[n][0] for n in BIG}
    full = dict(zip(BIG, _all_gather([shard[n].astype(BF16) for n in BIG], name="gather_weights")))
    w = {"w_in": full["w_in"], "w_glu": full["w_glu"].reshape(-1, shard["w_glu"].shape[-1]),
         "w_out": full["w_out"].reshape(-1, d_model), "w_gate": full["w_ffn_gate"], "w_up": full["w_ffn_up"],
         "w_down": full["w_ffn_down"]}
    p = {n: (wts[n][0].reshape(1, -1) if n in VECTORS else wts[n][0]) for n in SMALL}

    loss, grad_x, g_small, g_big = _local_step(x[0], loss_target[0], p, w)
    loss = lax.psum(loss, ("x", "y", "c"))

    slabs = [g_big[n].reshape((N_DEV,) + shard[n].shape) for n in BIG]
    slabs = [s.reshape(N_DEV, -1, s.shape[-1]) for s in slabs]
    from_sibling = _sibling_exchange(slabs)
    chips = _chips(ix, iy)
    take = lambda a, i: lax.dynamic_index_in_dim(a, i, 0, keepdims=False)
    partials = []
    for i, n in enumerate(BIG):
        own = jnp.stack([take(slabs[i], _slab(px, py, ic)) for px, py in chips[1:]])
        rows, cols = own.shape[1:]
        partials.append(_rowwise(lambda a, b: a + b, [own.reshape(3 * rows, cols),
                                                      from_sibling[i][1:].reshape(3 * rows, cols)], [],
                                 [(cols, BF16)], name=f"reduce_add_{n}", tm=512)[0].reshape(3, rows, cols))
    from_chips = _chip_exchange(partials)
    out = {}
    for i, n in enumerate(BIG):
        rows, cols = slabs[i].shape[1:]
        res = _adamw(shard[n].reshape(rows, cols), mom[n][0].reshape(rows, cols), var[n][0].reshape(rows, cols),
                     [(take(slabs[i], me), None), (from_sibling[i], 1), (from_chips[i], 3)], name=f"adamw_{n}")
        out[n] = [r.reshape(wts[n].shape) for r in res]

    order = list(SMALL)
    shapes = [wts[n].shape for n in order]
    packed = _pack([g_small[n] for n in order])
    gathered = _all_gather([packed], name="gather_small_grads")[0]
    res = _adamw(_pack([wts[n] for n in order]), _pack([mom[n] for n in order]), _pack([var[n] for n in order]),
                 [(gathered, N_DEV)], name="adamw_small")
    for kind, buf in enumerate(res):
        for n, a in zip(order, _unpack(buf, shapes)):
            out.setdefault(n, [None] * 4)[kind] = a

    return (loss, grad_x[None], *[out[n][0] for n in WEIGHTS], *[out[n][1] for n in WEIGHTS],
            *[out[n][2] for n in WEIGHTS], *[out[n][3] for n in WEIGHTS])
```

```python
import functools
import math

import jax
import jax.numpy as jnp
from jax import lax
from jax.experimental import pallas as pl
from jax.experimental.pallas import tpu as pltpu

F32 = jnp.float32
BF16 = jnp.bfloat16

N_DEV = 8
GRID_W = 64
WIN_H = 8
WIN_W = 16
HEAD_DIM = 64
HEADS_PER_GROUP = 4
GROUP_LANES = HEADS_PER_GROUP * HEAD_DIM
SSM_C = 16
SSM_P = 64
GROUPS_PER_TILE = 8
U_TILE = GROUPS_PER_TILE * SSM_C
ST_TILE = GROUPS_PER_TILE * SSM_P
SUBLANES = 8
RMS_EPS = 1e-6
NEG_INF = -1e30
A_RE_MAX = -1e-4
ADAM_LR, ADAM_B1, ADAM_B2, ADAM_EPS, ADAM_WD, ADAM_STEP = 0.001, 0.9, 0.999, 1e-08, 0.01, 10
VMEM_LIMIT_V7X = 56 * 1024 * 1024
MESH = pl.DeviceIdType.MESH

_NN = (((1,), (0,)), ((), ()))
_NT = (((1,), (1,)), ((), ()))
_TN = (((0,), (0,)), ((), ()))
_DIMS = {"nn": _NN, "nt": _NT, "tn": _TN}


def _params(**kw):
    return pltpu.CompilerParams(vmem_limit_bytes=VMEM_LIMIT_V7X, **kw)


def _dot(a, b, dims=_NN):
    return lax.dot_general(a, b, dims, preferred_element_type=F32)


def _mm(a, b, *, name, grid, a_spec, b_spec, o_spec, o_shape, dims, k_axis=None, res=None, out_dtype=F32,
        exact=False, second=None):
    dn = _DIMS[dims]
    nk = 1 if k_axis is None else grid[k_axis]
    acc_shape = tuple(d for d in o_spec.block_shape if d is not None)
    n_in = 2 + (2 if second is not None else 0)

    def body(*refs):
        a_ref, b_ref = refs[:2]
        r_ref = refs[n_in] if res is not None else None
        o_ref, acc = refs[-2:]
        if exact:
            p = lax.dot_general(a_ref[...], b_ref[...], dn, precision=lax.Precision.HIGHEST,
                                preferred_element_type=F32)
        else:
            p = _dot(a_ref[...].astype(BF16), b_ref[...].astype(BF16), dn)
        if second is not None:
            p = p + _dot(refs[2][...].astype(BF16), refs[3][...].astype(BF16), dn)

        def finish(v):
            if r_ref is not None:
                v = v + r_ref[...].astype(F32)
            o_ref[...] = v.astype(out_dtype)

        if nk == 1:
            finish(p)
        else:
            k = pl.program_id(k_axis)

            @pl.when(k == 0)
            def _():
                acc[...] = p

            @pl.when(k > 0)
            def _():
                acc[...] += p

            @pl.when(k == nk - 1)
            def _():
                finish(acc[...])

    ins = [a, b] + (list(second) if second is not None else []) + ([res] if res is not None else [])
    in_specs = [a_spec, b_spec] * (n_in // 2) + ([o_spec] if res is not None else [])
    return pl.pallas_call(
        body, name=name, grid=grid, in_specs=in_specs, out_specs=o_spec,
        out_shape=jax.ShapeDtypeStruct(o_shape, out_dtype),
        scratch_shapes=[pltpu.VMEM(acc_shape if nk > 1 else (SUBLANES, 128), F32)],
        compiler_params=_params(),
    )(*ins)


def _tile(n, want):
    if n <= want:
        return n
    t = want
    while n % t:
        t //= 2
    return t


def _mm_plain(a, b, dims, *, name, res=None, out_dtype=F32, tm=512, tn=512, tk=512, exact=False):
    if dims == "nn":
        (m, k), n = a.shape, b.shape[1]
    elif dims == "nt":
        (m, k), n = a.shape, b.shape[0]
    else:
        (k, m), n = a.shape, b.shape[1]
    tm, tn = _tile(m, tm), _tile(n, tn)
    if dims == "tn":
        tk = _tile(k, tk)
        grid = (m // tm, n // tn, k // tk)
        a_spec = pl.BlockSpec((tk, tm), lambda i, j, kk: (kk, i))
        b_spec = pl.BlockSpec((tk, tn), lambda i, j, kk: (kk, j))
        o_spec = pl.BlockSpec((tm, tn), lambda i, j, kk: (i, j))
        return _mm(a, b, name=name, grid=grid, a_spec=a_spec, b_spec=b_spec, o_spec=o_spec, o_shape=(m, n),
                   dims=dims, k_axis=2, res=res, out_dtype=out_dtype)
    grid = (n // tn, m // tm)
    a_spec = pl.BlockSpec((tm, k), lambda j, i: (i, 0))
    if dims == "nn":
        b_spec = pl.BlockSpec((k, tn), lambda j, i: (0, j))
    else:
        b_spec = pl.BlockSpec((tn, k), lambda j, i: (j, 0))
    o_spec = pl.BlockSpec((tm, tn), lambda j, i: (i, j))
    return _mm(a, b, name=name, grid=grid, a_spec=a_spec, b_spec=b_spec, o_spec=o_spec, o_shape=(m, n), dims=dims,
               res=res, out_dtype=out_dtype, exact=exact)


def _rowwise(fn, tiled, bcast, outs, accs=(), *, name, tm=256):
    m = tiled[0].shape[0]
    tm = _tile(m, tm)
    n_t, n_b, n_o, n_a = len(tiled), len(bcast), len(outs), len(accs)

    def body(*refs):
        ins = [r[...] for r in refs[: n_t + n_b]]
        o_refs = refs[n_t + n_b: n_t + n_b + n_o]
        a_refs = refs[n_t + n_b + n_o:]
        res = fn(*ins)
        if not isinstance(res, (tuple, list)):
            res = (res,)
        for r, v in zip(o_refs, res[:n_o]):
            r[...] = v.astype(r.dtype)
        first = pl.program_id(0) == 0
        for r, v in zip(a_refs, res[n_o:]):
            s = jnp.sum(v, axis=0, keepdims=True)

            @pl.when(first)
            def _():
                r[...] = s

            @pl.when(jnp.logical_not(first))
            def _():
                r[...] += s

    in_specs = [pl.BlockSpec((tm, t.shape[1]), lambda i: (i, 0)) for t in tiled]
    in_specs += [pl.BlockSpec(b.shape, lambda i, nd=b.ndim: (0,) * nd) for b in bcast]
    out_specs = [pl.BlockSpec((tm, n), lambda i: (i, 0)) for n, _ in outs]
    out_specs += [pl.BlockSpec((1, n), lambda i: (0, 0)) for n in accs]
    out_shape = [jax.ShapeDtypeStruct((m, n), dt) for n, dt in outs]
    out_shape += [jax.ShapeDtypeStruct((1, n), F32) for n in accs]
    res = pl.pallas_call(body, name=name, grid=(m // tm,), in_specs=in_specs, out_specs=out_specs,
                         out_shape=out_shape, compiler_params=_params())(*tiled, *bcast)
    return res


def _rstd(x):
    return lax.rsqrt(jnp.mean(x * x, axis=-1, keepdims=True) + RMS_EPS)


def _rms_bwd(dh, x, g):
    xh = x * _rstd(x)
    dxh = dh * g
    dx = _rstd(x) * (dxh - xh * jnp.mean(dxh * xh, axis=-1, keepdims=True))
    return dx, dh * xh


def _sigmoid(x):
    return 1.0 / (1.0 + jnp.exp(-x))


_GELU_K = math.sqrt(2.0 / math.pi)
_GELU_C = 0.044715


def _gelu(x):
    return 0.5 * x * (1.0 + jnp.tanh(_GELU_K * (x + _GELU_C * x * x * x)))


def _gelu_grad(x):
    th = jnp.tanh(_GELU_K * (x + _GELU_C * x * x * x))
    return 0.5 * (1.0 + th) + 0.5 * x * (1.0 - th * th) * _GELU_K * (1.0 + 3.0 * _GELU_C * x * x)


def _head_masks():
    lane_head = lax.broadcasted_iota(jnp.int32, (1, GROUP_LANES), 1) // HEAD_DIM
    return [(lane_head == h).astype(F32) for h in range(HEADS_PER_GROUP)]


def _head_block_diag():
    r = lax.broadcasted_iota(jnp.int32, (GROUP_LANES, GROUP_LANES), 0) // HEAD_DIM
    c = lax.broadcasted_iota(jnp.int32, (GROUP_LANES, GROUP_LANES), 1) // HEAD_DIM
    return (r == c).astype(BF16)


def _head_mean(x, bd):
    hi = x.astype(BF16)
    lo = (x - hi.astype(F32)).astype(BF16)
    return (_dot(hi, bd) + _dot(lo, bd)) * (1.0 / HEAD_DIM)


def _stack_heads(x, masks):
    return jnp.concatenate([x * m for m in masks], axis=0)


def _unstack_heads(xs, masks):
    out = xs[0:GRID_W] * masks[0]
    for h in range(1, HEADS_PER_GROUP):
        out = out + xs[h * GRID_W:(h + 1) * GRID_W] * masks[h]
    return out


def _row_start(r, rows):
    return jnp.clip(r - WIN_H // 2, 0, rows - WIN_H)


def _attn_common_specs(seq, n_hg, rows):
    win_keys = WIN_H * GRID_W
    q_spec = pl.BlockSpec((GRID_W, GROUP_LANES), lambda g, r: (r, g))
    k_spec = pl.BlockSpec((seq, GROUP_LANES), lambda g, r: (0, n_hg + g))
    v_spec = pl.BlockSpec((seq, GROUP_LANES), lambda g, r: (0, 2 * n_hg + g))
    gain_spec = pl.BlockSpec((1, GROUP_LANES), lambda g, r: (0, 0))

    def variant(r):
        return _row_start(r, rows) - r + (WIN_H - 1)

    bias_spec = pl.BlockSpec((None, None, HEADS_PER_GROUP, GRID_W, win_keys), lambda g, r: (g, variant(r), 0, 0, 0))
    return q_spec, k_spec, v_spec, gain_spec, bias_spec, variant


def _attn_prepare_kv(k_ref, v_ref, kg, kn_scr, vb_scr, bd, seq):
    chunk = _tile(seq, 512)

    def step(c, carry):
        rows = pl.ds(pl.multiple_of(c * chunk, chunk), chunk)
        k = k_ref[rows, :]
        kn_scr[rows, :] = (k * lax.rsqrt(_head_mean(k * k, bd) + RMS_EPS) * kg).astype(BF16)
        vb_scr[rows, :] = v_ref[rows, :].astype(BF16)
        return carry

    lax.fori_loop(0, seq // chunk, step, 0)


def _attn_probs(qn, kw, bias, masks):
    qs = _stack_heads(qn, masks).astype(BF16)
    s = _dot(qs, kw, _NT) * (1.0 / math.sqrt(HEAD_DIM)) + bias
    m = jnp.max(s, axis=-1, keepdims=True)
    p = jnp.exp(s - m)
    return qs, p / jnp.sum(p, axis=-1, keepdims=True)


def _attn_fwd(z, qg4, kg4, btab):
    seq = z.shape[0]
    a_width = btab.shape[0] * GROUP_LANES
    n_hg, rows, win_keys = btab.shape[0], seq // GRID_W, WIN_H * GRID_W
    q_spec, k_spec, v_spec, gain_spec, bias_spec, _ = _attn_common_specs(seq, n_hg, rows)

    def body(q_ref, k_ref, v_ref, qg_ref, kg_ref, b_ref, o_ref, kn_scr, vb_scr):
        r = pl.program_id(1)
        bd, masks = _head_block_diag(), _head_masks()

        @pl.when(r == 0)
        def _():
            _attn_prepare_kv(k_ref, v_ref, kg_ref[...], kn_scr, vb_scr, bd, seq)

        win = pl.ds(pl.multiple_of(_row_start(r, rows) * GRID_W, GRID_W), win_keys)
        q = q_ref[...]
        qn = q * lax.rsqrt(_head_mean(q * q, bd) + RMS_EPS) * qg_ref[...]
        bias = b_ref[...].reshape(HEADS_PER_GROUP * GRID_W, win_keys)
        _, p = _attn_probs(qn, kn_scr[win, :], bias, masks)
        o_ref[...] = _unstack_heads(_dot(p.astype(BF16), vb_scr[win, :]), masks)

    return pl.pallas_call(
        body, name="attn_fwd", grid=(n_hg, rows),
        in_specs=[q_spec, k_spec, v_spec, gain_spec, gain_spec, bias_spec],
        out_specs=pl.BlockSpec((GRID_W, GROUP_LANES), lambda g, r: (r, g)),
        out_shape=jax.ShapeDtypeStruct((seq, a_width), F32),
        scratch_shapes=[pltpu.VMEM((seq, GROUP_LANES), BF16), pltpu.VMEM((seq, GROUP_LANES), BF16)],
        compiler_params=_params(),
    )(z, z, z, qg4, kg4, btab)


def _attn_bwd(z, d_out, qg4, kg4, btab):
    seq = z.shape[0]
    n_hg, rows, win_keys = btab.shape[0], seq // GRID_W, WIN_H * GRID_W
    a_width = n_hg * GROUP_LANES
    q_spec, k_spec, v_spec, gain_spec, bias_spec, variant = _attn_common_specs(seq, n_hg, rows)
    scale = 1.0 / math.sqrt(HEAD_DIM)

    def body(q_ref, k_ref, v_ref, do_ref, qg_ref, kg_ref, b_ref,
             dq_ref, dk_out, dv_out, db_ref, dqg_ref, dkg_ref, kn_scr, vb_scr, dk_ref, dv_ref):
        r = pl.program_id(1)
        bd, masks = _head_block_diag(), _head_masks()

        @pl.when(r == 0)
        def _():
            _attn_prepare_kv(k_ref, v_ref, kg_ref[...], kn_scr, vb_scr, bd, seq)
            dk_ref[...] = jnp.zeros_like(dk_ref)
            dv_ref[...] = jnp.zeros_like(dv_ref)
            db_ref[...] = jnp.zeros_like(db_ref)
            dqg_ref[...] = jnp.zeros_like(dqg_ref)

        win = pl.ds(pl.multiple_of(_row_start(r, rows) * GRID_W, GRID_W), win_keys)
        q, qg = q_ref[...], qg_ref[...]
        rq = lax.rsqrt(_head_mean(q * q, bd) + RMS_EPS)
        qh = q * rq
        kw, vw = kn_scr[win, :], vb_scr[win, :]
        bias = b_ref[...].reshape(HEADS_PER_GROUP * GRID_W, win_keys)
        qs, p = _attn_probs(qh * qg, kw, bias, masks)
        dos = _stack_heads(do_ref[...], masks).astype(BF16)
        dp = _dot(dos, vw, _NT)
        ds = p * (dp - jnp.sum(p * dp, axis=-1, keepdims=True))
        db_ref[variant(r)] += ds.reshape(HEADS_PER_GROUP, GRID_W, win_keys)
        dsb = ds.astype(BF16)
        dqn = _unstack_heads(_dot(dsb, kw), masks) * scale
        dk_ref[win, :] += _dot(dsb, qs, _TN) * scale
        dv_ref[win, :] += _dot(p.astype(BF16), dos, _TN)
        dqg_ref[...] += jnp.sum(dqn * qh, axis=0, keepdims=True)
        dqh = dqn * qg
        dq_ref[...] = (rq * (dqh - qh * _head_mean(dqh * qh, bd))).astype(BF16)

        @pl.when(r == rows - 1)
        def _():
            chunk = _tile(seq, 512)
            kg = kg_ref[...]

            def step(c, dkg):
                rws = pl.ds(pl.multiple_of(c * chunk, chunk), chunk)
                k = k_ref[rws, :]
                rk = lax.rsqrt(_head_mean(k * k, bd) + RMS_EPS)
                kh = k * rk
                dkn = dk_ref[rws, :]
                dkh = dkn * kg
                dk_out[rws, :] = (rk * (dkh - kh * _head_mean(dkh * kh, bd))).astype(BF16)
                dv_out[rws, :] = dv_ref[rws, :].astype(BF16)
                return dkg + jnp.sum(dkn * kh, axis=0, keepdims=True)

            dkg_ref[...] = lax.fori_loop(0, seq // chunk, step, jnp.zeros((1, GROUP_LANES), F32))

    col_spec = pl.BlockSpec((seq, GROUP_LANES), lambda g, r: (0, g))
    gsum_spec = pl.BlockSpec((None, 1, GROUP_LANES), lambda g, r: (g, 0, 0))
    return pl.pallas_call(
        body, name="attn_bwd", grid=(n_hg, rows),
        in_specs=[q_spec, k_spec, v_spec, pl.BlockSpec((GRID_W, GROUP_LANES), lambda g, r: (r, g)),
                  gain_spec, gain_spec, bias_spec],
        out_specs=[pl.BlockSpec((GRID_W, GROUP_LANES), lambda g, r: (r, g)), col_spec, col_spec,
                   pl.BlockSpec((None, WIN_H, HEADS_PER_GROUP, GRID_W, win_keys), lambda g, r: (g, 0, 0, 0, 0)),
                   gsum_spec, gsum_spec],
        out_shape=[jax.ShapeDtypeStruct((seq, a_width), BF16)] * 3
        + [jax.ShapeDtypeStruct(btab.shape, F32)]
        + [jax.ShapeDtypeStruct((n_hg, 1, GROUP_LANES), F32)] * 2,
        scratch_shapes=[pltpu.VMEM((seq, GROUP_LANES), BF16), pltpu.VMEM((seq, GROUP_LANES), BF16),
                        pltpu.VMEM((seq, GROUP_LANES), F32), pltpu.VMEM((seq, GROUP_LANES), F32)],
        compiler_params=_params(),
    )(z, z, z, d_out, qg4, kg4, btab)


def _bias_index():
    c = jnp.arange(GRID_W)
    col_start = jnp.clip(c - WIN_W // 2, 0, GRID_W - WIN_W)
    col_in = (c[None, :] >= col_start[:, None]) & (c[None, :] < col_start[:, None] + WIN_W)
    dc = jnp.clip(c[None, :] - c[:, None], -(WIN_W - 1), WIN_W - 1) + (WIN_W - 1)
    dr = jnp.arange(WIN_H)[:, None] + jnp.arange(WIN_H)[None, :]
    return col_in, dc, dr


def _bias_table(rpb):
    col_in, dc, _ = _bias_index()
    n_h = rpb.shape[0]
    n_hg = n_h // HEADS_PER_GROUP
    spread = ((jnp.arange(128)[:, None] == dc.reshape(1, -1)) & col_in.reshape(1, -1)).astype(F32)
    rows = jnp.stack([rpb[:, v:v + WIN_H] for v in range(WIN_H)], axis=1)
    rows = jnp.pad(rows, ((0, 0), (0, 0), (0, 0), (0, 128 - rows.shape[-1]))).reshape(n_h * WIN_H * WIN_H, 128)
    tab = _mm_plain(rows, spread, "nn", name="rpb_spread", tm=256, tn=2048, exact=True)
    tab = jnp.where(col_in.reshape(1, -1), tab, NEG_INF)
    tab = tab.reshape(n_hg, HEADS_PER_GROUP, WIN_H, WIN_H, GRID_W, GRID_W).transpose(0, 2, 1, 4, 3, 5)
    return tab.reshape(n_hg, WIN_H, HEADS_PER_GROUP, GRID_W, WIN_H * GRID_W)


def _bias_grad(dtab, n_h):
    col_in, dc, _ = _bias_index()
    onehot = (dc.reshape(-1, 1) == jnp.arange(128)[None, :]) & col_in.reshape(-1, 1)
    n_hg = n_h // HEADS_PER_GROUP
    d = dtab.reshape(n_hg, WIN_H, HEADS_PER_GROUP, GRID_W, WIN_H, GRID_W).transpose(0, 2, 1, 4, 3, 5)
    d = d.reshape(n_h * WIN_H * WIN_H, GRID_W * GRID_W)
    diag = _mm_plain(d, onehot.astype(BF16), "nn", name="rpb_diag_sum", tm=256, tn=128)
    diag = diag.reshape(n_h, WIN_H, WIN_H, 128)[..., : 2 * WIN_W - 1]
    out = jnp.zeros((n_h, 2 * WIN_H - 1, 2 * WIN_W - 1), F32)
    for v in range(WIN_H):
        out = out.at[:, v:v + WIN_H].add(diag[:, v])
    return out


def _cmul(ar, ai, br, bi):
    return ar * br - ai * bi, ar * bi + ai * br


def _s5_discretize(a_re, a_im, dt, b_re, b_im):
    c = b_re.shape[1]

    def fn(are, aim, dt_, bre, bim):
        lr, li = jnp.minimum(are, A_RE_MAX), aim
        mag = jnp.exp(lr * dt_)
        l1r, l1i = mag * jnp.cos(li * dt_), mag * jnp.sin(li * dt_)
        den = lr * lr + li * li
        nr, ni = l1r - 1.0, l1i
        cr, ci = (nr * lr + ni * li) / den, (ni * lr - nr * li) / den
        bbr, bbi = _cmul(cr, ci, bre, bim)
        shape = (are.shape[0], SUBLANES)
        lane = lax.broadcasted_iota(jnp.int32, shape, 1)
        pr, pi = l1r, l1i
        acc_r, acc_i = jnp.zeros(shape, F32), jnp.zeros(shape, F32)
        for k in range(SUBLANES):
            acc_r = jnp.where(lane == k, pr, acc_r)
            acc_i = jnp.where(lane == k, pi, acc_i)
            pr, pi = _cmul(pr, pi, l1r, l1i)
        return acc_r, acc_i, cr, ci, bbr, bbi

    return _rowwise(fn, [a_re, a_im, dt, b_re, b_im], [],
                    [(SUBLANES, F32), (SUBLANES, F32), (1, F32), (1, F32), (c, F32), (c, F32)],
                    name="s5_discretize", tm=1024)


def _s5_param_grads(a_re, a_im, dt, b_re, b_im, l1r, l1i, cr, ci, bbr, bbi, r_re, r_im, gb_re, gb_im):
    c = b_re.shape[1]

    def fn(are, aim, dt_, bre, bim, l1r_, l1i_, cr_, ci_, bbr_, bbi_, rr, ri, gbr, gbi):
        lr, li = jnp.minimum(are, A_RE_MAX), aim
        den = lr * lr + li * li
        dbr, dbi = _cmul(cr_, -ci_, gbr, gbi)
        gcr, gci = _cmul(bre, -bim, gbr, gbi)
        gcr, gci = jnp.sum(gcr, axis=1, keepdims=True), jnp.sum(gci, axis=1, keepdims=True)
        qr, qi = _cmul(bbr_, -bbi_, gbr, gbi)
        qr = rr - jnp.sum(qr, axis=1, keepdims=True)
        qi = ri - jnp.sum(qi, axis=1, keepdims=True)
        tr, ti = _cmul(gcr, gci, lr / den, li / den)
        ur, ui = _cmul(l1r_, -l1i_, tr, ti)
        gwr, gwi = qr + ur, qi + ui
        vr, vi = _cmul(cr_, -ci_, lr / den, li / den)
        vr, vi = _cmul(gcr, gci, vr, vi)
        glr, gli = dt_ * gwr - vr, dt_ * gwi - vi
        return jnp.where(are < A_RE_MAX, glr, 0.0), gli, (gwr * lr + gwi * li) * dt_, dbr, dbi

    return _rowwise(fn, [a_re, a_im, dt, b_re, b_im, l1r, l1i, cr, ci, bbr, bbi, r_re, r_im, gb_re, gb_im], [],
                    [(1, F32), (1, F32), (1, F32), (c, F32), (c, F32)], name="s5_param_grads", tm=1024)


def _s5_scan(v, win_re, win_im, tabs, wo_re, wo_im, *, reverse, name, t_chunk=256):
    seq, width = v.shape
    n_tiles, n_state = width // U_TILE, width * (SSM_P // SSM_C)
    t_chunk = _tile(seq, t_chunk)
    n_chunks, n_blk = seq // t_chunk, t_chunk // SUBLANES
    last_row = 0 if reverse else SUBLANES - 1

    def chunk_of(j):
        return (n_chunks - 1 - j) if reverse else j

    def body(v_ref, wir_ref, wii_ref, tab_ref, wor_ref, woi_ref, sr_ref, si_ref, y_ref, carry):
        @pl.when(pl.program_id(0) == 0)
        def _():
            carry[...] = jnp.zeros_like(carry)

        for jt in range(n_tiles):
            ls = slice(jt * ST_TILE, (jt + 1) * ST_TILE)
            us = slice(jt * U_TILE, (jt + 1) * U_TILE)
            vj = v_ref[:, us].astype(BF16)
            sr_ref[:, ls] = _dot(vj, wir_ref[jt])
            si_ref[:, ls] = _dot(vj, wii_ref[jt])
            consts = [tab_ref[k, :, ls] for k in range(8)]

            def blk(b, c, ls=ls, consts=consts):
                cr, ci = c
                bb = (n_blk - 1 - b) if reverse else b
                rows = pl.ds(pl.multiple_of(bb * SUBLANES, SUBLANES), SUBLANES)
                xr, xi = sr_ref[rows, ls], si_ref[rows, ls]
                for s, k in enumerate((1, 2, 4)):
                    sh = (SUBLANES - k) if reverse else k
                    tr, ti = pltpu.roll(xr, sh, 0), pltpu.roll(xi, sh, 0)
                    lr, li = consts[2 * s], consts[2 * s + 1]
                    xr, xi = xr + lr * tr - li * ti, xi + lr * ti + li * tr
                lr, li = consts[6], consts[7]
                xr, xi = xr + lr * cr - li * ci, xi + lr * ci + li * cr
                sr_ref[rows, ls], si_ref[rows, ls] = xr, xi
                shape = (SUBLANES, ST_TILE)
                return (jnp.broadcast_to(xr[last_row:last_row + 1], shape),
                        jnp.broadcast_to(xi[last_row:last_row + 1], shape))

            cr, ci = lax.fori_loop(0, n_blk, blk, (carry[0, :, ls], carry[1, :, ls]), unroll=4)
            carry[0, :, ls], carry[1, :, ls] = cr, ci
            y_ref[:, us] = (_dot(sr_ref[:, ls].astype(BF16), wor_ref[jt])
                            + _dot(si_ref[:, ls].astype(BF16), woi_ref[jt]))

    whole = lambda a: pl.BlockSpec(a.shape, lambda j, nd=a.ndim: (0,) * nd)
    st_spec = pl.BlockSpec((t_chunk, n_state), lambda j: (chunk_of(j), 0))
    v_spec = pl.BlockSpec((t_chunk, width), lambda j: (chunk_of(j), 0))
    return pl.pallas_call(
        body, name=name, grid=(n_chunks,),
        in_specs=[v_spec, whole(win_re), whole(win_im), whole(tabs), whole(wo_re), whole(wo_im)],
        out_specs=[st_spec, st_spec, v_spec],
        out_shape=[jax.ShapeDtypeStruct((seq, n_state), F32)] * 2 + [jax.ShapeDtypeStruct((seq, width), F32)],
        scratch_shapes=[pltpu.VMEM((2, SUBLANES, n_state), F32)],
        compiler_params=_params(),
    )(v, win_re, win_im, tabs, wo_re, wo_im)


def _s5_reduce(x_re, x_im, a_re, a_im, u, dy, *, name, t_chunk=512):
    seq, n_state = x_re.shape
    width = u.shape[1]
    n_tiles = width // U_TILE
    t_chunk = _tile(seq, t_chunk)

    def body(xr_ref, xi_ref, ar_ref, ai_ref, u_ref, dy_ref, rr_ref, ri_ref, gbr_ref, gbi_ref, gcr_ref, gci_ref):
        xr, xi, ar, ai = xr_ref[...], xi_ref[...], ar_ref[...], ai_ref[...]
        ub, dyb = u_ref[...].astype(BF16), dy_ref[...].astype(BF16)
        parts = (jnp.sum(ar * xr + ai * xi, axis=0, keepdims=True), jnp.sum(ai * xr - ar * xi, axis=0, keepdims=True),
                 _dot(ar.astype(BF16), ub, _TN), _dot(ai.astype(BF16), ub, _TN),
                 _dot(xr.astype(BF16), dyb, _TN), _dot(xi.astype(BF16), dyb, _TN))
        first = pl.program_id(1) == 0
        for ref, val in zip((rr_ref, ri_ref, gbr_ref, gbi_ref, gcr_ref, gci_ref), parts):
            @pl.when(first)
            def _():
                ref[...] = val

            @pl.when(jnp.logical_not(first))
            def _():
                ref[...] += val

    st_spec = pl.BlockSpec((t_chunk, ST_TILE), lambda j, t: (t, j))
    u_spec = pl.BlockSpec((t_chunk, U_TILE), lambda j, t: (t, j))
    r_spec = pl.BlockSpec((1, ST_TILE), lambda j, t: (0, j))
    g_spec = pl.BlockSpec((None, ST_TILE, U_TILE), lambda j, t: (j, 0, 0))
    return pl.pallas_call(
        body, name=name, grid=(n_tiles, seq // t_chunk),
        in_specs=[st_spec] * 4 + [u_spec] * 2,
        out_specs=[r_spec, r_spec] + [g_spec] * 4,
        out_shape=[jax.ShapeDtypeStruct((1, n_state), F32)] * 2
        + [jax.ShapeDtypeStruct((n_tiles, ST_TILE, U_TILE), F32)] * 4,
        compiler_params=_params(),
    )(x_re, x_im, a_re, a_im, u, dy)


def _block_diag_in(ms):
    m = jnp.stack(ms)
    n, g, c, p = m.shape
    m5 = m.reshape(n, g // GROUPS_PER_TILE, GROUPS_PER_TILE, c, p)
    eye = jnp.eye(GROUPS_PER_TILE, dtype=m.dtype)
    out = m5[:, :, :, :, None, :] * eye[None, None, :, None, :, None]
    return out.astype(BF16).reshape(n, g // GROUPS_PER_TILE, GROUPS_PER_TILE * c, GROUPS_PER_TILE * p)


def _block_diag_take(m, c, p):
    t = m.shape[0]
    m5 = m.reshape(t, GROUPS_PER_TILE, p, GROUPS_PER_TILE, c)
    idx = jnp.arange(GROUPS_PER_TILE)
    return m5[:, idx, :, idx, :].transpose(1, 0, 2, 3).reshape(t * GROUPS_PER_TILE, p, c)


def _scan_tables(pw_re, pw_im, reverse):
    row = jnp.arange(SUBLANES)[:, None]
    tabs = []
    for k in (1, 2, 4):
        keep = (row <= SUBLANES - 1 - k) if reverse else (row >= k)
        tabs += [jnp.where(keep, pw_re[k - 1][None, :], 0.0), jnp.where(keep, pw_im[k - 1][None, :], 0.0)]
    order = jnp.arange(SUBLANES)[::-1] if reverse else jnp.arange(SUBLANES)
    tabs += [pw_re[order], pw_im[order]]
    return jnp.stack(tabs)


def _local_step(x, target, p, w):
    seq, d_model = x.shape
    a_width = p["g_out_attn"].shape[-1]
    s_width = p["g_out_ssm"].shape[-1]
    n_heads = a_width // HEAD_DIM
    n_hg = n_heads // HEADS_PER_GROUP
    n_groups = s_width // SSM_C
    n_sh, _, in_sh = w["w_in"].shape
    f_sh = w["w_gate"].shape[2]
    t2, t1 = _tile(seq, 2048), _tile(seq, 1024)
    n2, n1 = seq // t2, seq // t1

    h1 = _rowwise(lambda xv, g: x_norm(xv, g), [x], [p["g_mix"]], [(d_model, BF16)], name="rms_mix")[0]
    z = _mm(h1, w["w_in"], name="in_proj", grid=(n2, n_sh),
            a_spec=pl.BlockSpec((t2, d_model), lambda i, j: (i, 0)),
            b_spec=pl.BlockSpec((None, d_model, in_sh), lambda i, j: (j, 0, 0)),
            o_spec=pl.BlockSpec((t2, in_sh), lambda i, j: (i, j)), o_shape=(seq, n_sh * in_sh), dims="nn")
    qg4 = jnp.tile(p["q_gain"], (1, HEADS_PER_GROUP))
    kg4 = jnp.tile(p["k_gain"], (1, HEADS_PER_GROUP))
    btab = _bias_table(p["rpb"])
    ya = _attn_fwd(z, qg4, kg4, btab)
    u = z[:, 3 * a_width:]

    n_col = 2 * n_groups * SSM_P
    col = lambda a: a.reshape(n_col, 1)
    a_re_c, a_im_c = col(p["ssm_a_re"]), col(p["ssm_a_im"])
    dt_c = col(jnp.broadcast_to(jnp.exp(p["ssm_log_step"])[:, :, None], (2, n_groups, SSM_P)))
    b_re_c, b_im_c = p["ssm_b_re"].reshape(n_col, SSM_C), p["ssm_b_im"].reshape(n_col, SSM_C)
    pw_re, pw_im, cf_re, cf_im, bb_re, bb_im = _s5_discretize(a_re_c, a_im_c, dt_c, b_re_c, b_im_c)
    n_state = n_groups * SSM_P
    pw_re = pw_re.reshape(2, n_state, SUBLANES).transpose(0, 2, 1)
    pw_im = pw_im.reshape(2, n_state, SUBLANES).transpose(0, 2, 1)
    bb_re4, bb_im4 = bb_re.reshape(2, n_groups, SSM_P, SSM_C), bb_im.reshape(2, n_groups, SSM_P, SSM_C)
    c_re, c_im = p["ssm_c_re"], p["ssm_c_im"]
    t21 = lambda a: a.transpose(0, 2, 1)
    maps_in = _block_diag_in([m for d in range(2) for m in (t21(bb_re4[d]), t21(bb_im4[d]), c_re[d], -c_im[d])])
    maps_out = _block_diag_in([m for d in range(2) for m in (t21(c_re[d]), -t21(c_im[d]), bb_re4[d], bb_im4[d])])
    fwd, bwd_in = [], []
    for d in range(2):
        rev = d == 1
        tabs = _scan_tables(pw_re[d], pw_im[d], rev)
        xs_re, xs_im, y_d = _s5_scan(u, maps_in[4 * d], maps_in[4 * d + 1], tabs, maps_out[4 * d],
                                     maps_out[4 * d + 1], reverse=rev, name=f"s5_fwd_{d}")
        fwd.append((xs_re, xs_im, y_d))
        bwd_in.append((maps_in[4 * d + 2], maps_in[4 * d + 3], _scan_tables(pw_re[d], -pw_im[d], not rev),
                       maps_out[4 * d + 2], maps_out[4 * d + 3]))

    ypre, yg = _rowwise(lambda y0, y1, uv, dsk: s5_mid(y0, y1, uv, dsk), [fwd[0][2], fwd[1][2], u], [p["ssm_d"]],
                        [(s_width, F32), (s_width, F32)], name="s5_skip_gelu")
    t_glu = _mm_plain(yg, w["w_glu"], "nn", name="glu_proj", tn=s_width)
    y_cat = _rowwise(mix_out_fwd, [ya, yg, t_glu], [p["b_glu"], p["g_out_attn"], p["g_out_ssm"]],
                     [(a_width + s_width, BF16)], name="mix_out")[0]
    x1 = _mm_plain(y_cat, w["w_out"], "nn", name="out_proj", res=x, tn=2048)

    h2 = _rowwise(lambda xv, g: x_norm(xv, g), [x1], [p["g_ffn"]], [(d_model, BF16)], name="rms_ffn")[0]
    ffn_up = functools.partial(
        _mm, grid=(n2, n_sh), a_spec=pl.BlockSpec((t2, d_model), lambda i, j: (i, 0)),
        b_spec=pl.BlockSpec((None, d_model, f_sh), lambda i, j: (j, 0, 0)),
        o_spec=pl.BlockSpec((None, t2, f_sh), lambda i, j: (j, i, 0)), o_shape=(n_sh, seq, f_sh), dims="nn",
        out_dtype=BF16)
    gate = ffn_up(h2, w["w_gate"], name="ffn_gate")
    up = ffn_up(h2, w["w_up"], name="ffn_up")
    flat = lambda a: a.reshape(n_sh * seq, f_sh)
    act = _rowwise(swiglu_fwd, [flat(gate), flat(up)], [], [(f_sh, BF16)], name="swiglu",
                   tm=1024)[0].reshape(n_sh, seq, f_sh)
    ffn_out = _mm(act, w["w_down"], name="ffn_down", grid=(n1, n_sh),
                  a_spec=pl.BlockSpec((None, t1, f_sh), lambda i, j: (j, i, 0)),
                  b_spec=pl.BlockSpec((None, f_sh, d_model), lambda i, j: (j, 0, 0)),
                  o_spec=pl.BlockSpec((t1, d_model), lambda i, j: (i, 0)), o_shape=(seq, d_model), dims="nn",
                  k_axis=1)

    dx2, dx2_b, sq = _rowwise(functools.partial(loss_head, inv_d=1.0 / d_model), [ffn_out, x1, target], [],
                              [(d_model, F32), (d_model, BF16)], [d_model], name="loss_head")
    loss = 0.5 * jnp.sum(sq) / d_model

    d_act = _mm(dx2_b, w["w_down"], name="ffn_down_dx", grid=(n2, n_sh),
                a_spec=pl.BlockSpec((t2, d_model), lambda i, j: (i, 0)),
                b_spec=pl.BlockSpec((None, f_sh, d_model), lambda i, j: (j, 0, 0)),
                o_spec=pl.BlockSpec((None, t2, f_sh), lambda i, j: (j, i, 0)), o_shape=(n_sh, seq, f_sh), dims="nt",
                out_dtype=BF16)
    g_w_down = _mm(act, dx2_b, name="ffn_down_dw", grid=(n_sh, n1),
                   a_spec=pl.BlockSpec((None, t1, f_sh), lambda j, k: (j, k, 0)),
                   b_spec=pl.BlockSpec((t1, d_model), lambda j, k: (k, 0)),
                   o_spec=pl.BlockSpec((None, f_sh, d_model), lambda j, k: (j, 0, 0)),
                   o_shape=(n_sh, f_sh, d_model), dims="tn", k_axis=1)
    d_gate, d_up = _rowwise(swiglu_bwd, [flat(d_act), flat(gate), flat(up)], [], [(f_sh, BF16), (f_sh, BF16)],
                            name="swiglu_bwd", tm=1024)
    d_gate, d_up = d_gate.reshape(n_sh, seq, f_sh), d_up.reshape(n_sh, seq, f_sh)
    d_h2 = _mm(d_gate, w["w_gate"], second=(d_up, w["w_up"]), name="ffn_up_gate_dx", grid=(n1, n_sh),
               a_spec=pl.BlockSpec((None, t1, f_sh), lambda i, j: (j, i, 0)),
               b_spec=pl.BlockSpec((None, d_model, f_sh), lambda i, j: (j, 0, 0)),
               o_spec=pl.BlockSpec((t1, d_model), lambda i, j: (i, 0)), o_shape=(seq, d_model), dims="nt", k_axis=1)
    ffn_dw = functools.partial(
        _mm, grid=(n_sh, n1), a_spec=pl.BlockSpec((t1, d_model), lambda j, k: (k, 0)),
        b_spec=pl.BlockSpec((None, t1, f_sh), lambda j, k: (j, k, 0)),
        o_spec=pl.BlockSpec((None, d_model, f_sh), lambda j, k: (j, 0, 0)), o_shape=(n_sh, d_model, f_sh), dims="tn",
        k_axis=1)
    g_w_gate = ffn_dw(h2, d_gate, name="ffn_gate_dw")
    g_w_up = ffn_dw(h2, d_up, name="ffn_up_dw")
    dx1, g_g_ffn = _rowwise(residual_rms_bwd, [dx2, d_h2, x1], [p["g_ffn"]], [(d_model, F32)], [d_model],
                            name="rms_ffn_bwd")

    d_ycat = _mm_plain(dx1, w["w_out"], "nt", name="out_proj_dx", tn=2048)
    g_w_out = _mm_plain(y_cat, dx1, "tn", name="out_proj_dw", tm=1024, tn=2048)
    (d_ya, d_yg_direct, d_t, g_goa, g_gos, g_b_glu) = _rowwise(
        functools.partial(mix_out_bwd, a_width=a_width), [d_ycat, ya, yg, t_glu],
        [p["b_glu"], p["g_out_attn"], p["g_out_ssm"]],
        [(a_width, F32), (s_width, F32), (s_width, BF16)], [a_width, s_width, s_width], name="mix_out_bwd")
    d_yg = _mm_plain(d_t, w["w_glu"], "nt", name="glu_proj_dx", res=d_yg_direct, tn=s_width)
    g_w_glu = _mm_plain(yg, d_t, "tn", name="glu_proj_dw", tm=1024, tn=1024)
    d_ypre, du_skip, g_ssm_d = _rowwise(gelu_skip_bwd, [d_yg, ypre, u], [p["ssm_d"]],
                                        [(s_width, F32), (s_width, F32)], [s_width], name="s5_skip_gelu_bwd")

    du_dirs, r_parts, gb_parts, gc_parts = [], [], [], []
    for d in range(2):
        win_re, win_im, tabs, wo_re, wo_im = bwd_in[d]
        as_re, as_im, du_d = _s5_scan(d_ypre, win_re, win_im, tabs, wo_re, wo_im, reverse=(d == 0),
                                      name=f"s5_bwd_{d}")
        du_dirs.append(du_d)
        r_re, r_im, gbt_re, gbt_im, gct_re, gct_im = _s5_reduce(fwd[d][0], fwd[d][1], as_re, as_im, u, d_ypre,
                                                                name=f"s5_reduce_{d}")
        r_parts.append((r_re.reshape(n_state, 1), r_im.reshape(n_state, 1)))
        gb_parts.append((_block_diag_take(gbt_re, SSM_C, SSM_P), _block_diag_take(gbt_im, SSM_C, SSM_P)))
        gc_parts.append((_block_diag_take(gct_re, SSM_C, SSM_P), _block_diag_take(gct_im, SSM_C, SSM_P)))
    cat = lambda i, parts: jnp.concatenate([parts[0][i], parts[1][i]], axis=0)
    gbb_re, gbb_im = cat(0, gb_parts).reshape(n_col, SSM_C), cat(1, gb_parts).reshape(n_col, SSM_C)
    g_a_re, g_a_im, g_ls, g_b_re, g_b_im = _s5_param_grads(
        a_re_c, a_im_c, dt_c, b_re_c, b_im_c, pw_re[:, 0].reshape(n_col, 1), pw_im[:, 0].reshape(n_col, 1),
        cf_re, cf_im, bb_re, bb_im, cat(0, r_parts), cat(1, r_parts), gbb_re, gbb_im)
    g_c_re = cat(0, gc_parts).reshape(2, n_groups, SSM_P, SSM_C).transpose(0, 1, 3, 2)
    g_c_im = -cat(1, gc_parts).reshape(2, n_groups, SSM_P, SSM_C).transpose(0, 1, 3, 2)

    d_q, d_k, d_v, d_btab, g_qg, g_kg = _attn_bwd(z, d_ya, qg4, kg4, btab)
    d_u = _rowwise(lambda a, b, c: a + b + c, [du_dirs[0], du_dirs[1], du_skip], [], [(s_width, BF16)],
                   name="s5_du_sum")[0]
    d_z = jnp.concatenate([d_q, d_k, d_v, d_u], axis=1)
    d_h1 = _mm(d_z, w["w_in"], name="in_proj_dx", grid=(n1, n_sh),
               a_spec=pl.BlockSpec((t1, in_sh), lambda i, j: (i, j)),
               b_spec=pl.BlockSpec((None, d_model, in_sh), lambda i, j: (j, 0, 0)),
               o_spec=pl.BlockSpec((t1, d_model), lambda i, j: (i, 0)), o_shape=(seq, d_model), dims="nt", k_axis=1)
    g_w_in = _mm(h1, d_z, name="in_proj_dw", grid=(n_sh, n1),
                 a_spec=pl.BlockSpec((t1, d_model), lambda j, k: (k, 0)),
                 b_spec=pl.BlockSpec((t1, in_sh), lambda j, k: (k, j)),
                 o_spec=pl.BlockSpec((None, d_model, in_sh), lambda j, k: (j, 0, 0)),
                 o_shape=(n_sh, d_model, in_sh), dims="tn", k_axis=1)
    grad_x, g_g_mix = _rowwise(residual_rms_bwd, [dx1, d_h1, x], [p["g_mix"]], [(d_model, F32)], [d_model],
                               name="rms_mix_bwd")

    fold_heads = lambda g: g.reshape(n_heads, HEAD_DIM).sum(axis=0, keepdims=True)
    small = {
        "g_mix": g_g_mix, "q_gain": fold_heads(g_qg), "k_gain": fold_heads(g_kg),
        "rpb": _bias_grad(d_btab, n_heads),
        "ssm_a_re": g_a_re.reshape(2, n_groups, SSM_P), "ssm_a_im": g_a_im.reshape(2, n_groups, SSM_P),
        "ssm_b_re": g_b_re.reshape(2, n_groups, SSM_P, SSM_C), "ssm_b_im": g_b_im.reshape(2, n_groups, SSM_P, SSM_C),
        "ssm_c_re": g_c_re, "ssm_c_im": g_c_im,
        "ssm_log_step": g_ls.reshape(2, n_groups, SSM_P).sum(axis=-1),
        "ssm_d": g_ssm_d, "b_glu": g_b_glu, "g_out_attn": g_goa, "g_out_ssm": g_gos, "g_ffn": g_g_ffn,
    }
    big = {"w_in": g_w_in, "w_glu": g_w_glu, "w_out": g_w_out, "w_ffn_gate": g_w_gate, "w_ffn_up": g_w_up,
           "w_ffn_down": g_w_down}
    return loss, grad_x, small, big


def x_norm(xv, g):
    return xv * _rstd(xv) * g


def s5_mid(y0, y1, uv, d_skip):
    ypre = y0 + y1 + d_skip * uv
    return ypre, _gelu(ypre)


def mix_out_fwd(ya, yg, t, b_glu, g_oa, g_os):
    ys = yg * _sigmoid(t + b_glu)
    return jnp.concatenate([ya * _rstd(ya) * g_oa, ys * _rstd(ys) * g_os], axis=1)


def mix_out_bwd(d_y, ya, yg, t, b_glu, g_oa, g_os, *, a_width):
    sg = _sigmoid(t + b_glu)
    ys = yg * sg
    d_ya, c_goa = _rms_bwd(d_y[:, :a_width], ya, g_oa)
    d_ys, c_gos = _rms_bwd(d_y[:, a_width:], ys, g_os)
    d_t = d_ys * yg * sg * (1.0 - sg)
    return d_ya, d_ys * sg, d_t, c_goa, c_gos, d_t


def gelu_skip_bwd(d_yg, ypre, uv, d_skip):
    d_ypre = d_yg * _gelu_grad(ypre)
    return d_ypre, d_ypre * d_skip, d_ypre * uv


def swiglu_fwd(gv, uv):
    gv, uv = gv.astype(F32), uv.astype(F32)
    return gv * _sigmoid(gv) * uv


def swiglu_bwd(d_act, gv, uv):
    d_act, gv, uv = d_act.astype(F32), gv.astype(F32), uv.astype(F32)
    sg = _sigmoid(gv)
    return d_act * uv * (sg * (1.0 + gv * (1.0 - sg))), d_act * gv * sg


def loss_head(ffn_out, x1, target, *, inv_d):
    diff = ffn_out + x1 - target
    return diff * inv_d, diff * inv_d, diff * diff


def residual_rms_bwd(d_res, d_h, xv, g):
    dx, c_g = _rms_bwd(d_h, xv, g)
    return d_res + dx, c_g


_ANY = pl.BlockSpec(memory_space=pl.ANY)


def _mesh_place():
    return lax.axis_index("x"), lax.axis_index("y"), lax.axis_index("c")


def _chips(x, y):
    return [(x, y), (1 - x, y), (x, 1 - y), (1 - x, 1 - y)]


def _slab(px, py, pc):
    return 4 * px + 2 * py + pc


def _all_gather(arrs, *, name):
    n = len(arrs)

    def body(*refs):
        in_refs, out_refs = refs[:n], refs[n:2 * n]
        send_sems, recv_sems, local_sems = refs[2 * n:]
        x, y, c = _mesh_place()
        me, sibling = (x, y, c), (x, y, 1 - c)
        others = _chips(x, y)[1:]

        def copy(w, k, block, to, src=None):
            dst = out_refs[w].at[_slab(*block)]
            return pltpu.make_async_remote_copy(
                src_ref=dst if src is None else src, dst_ref=dst, send_sem=send_sems.at[7 * w + k],
                recv_sem=recv_sems.at[7 * w + k], device_id=to, device_id_type=MESH)

        mine = [pltpu.make_async_copy(in_refs[w], out_refs[w].at[_slab(*me)], local_sems.at[w]) for w in range(n)]
        first = []
        for w in range(n):
            mine[w].start()
            first.append(copy(w, 0, me, sibling, src=in_refs[w]))
            first += [copy(w, 1 + j, me, (*chip, c), src=in_refs[w]) for j, chip in enumerate(others)]
        for cp in first:
            cp.start()
        passed = []
        for j, chip in enumerate(others):
            for w in range(n):
                copy(w, 1 + j, (*chip, c), me).wait_recv()
                fwd = copy(w, 4 + j, (*chip, c), sibling)
                fwd.start()
                passed.append(fwd)
        for w in range(n):
            copy(w, 0, sibling, me).wait_recv()
        for j, chip in enumerate(others):
            for w in range(n):
                copy(w, 4 + j, (*chip, 1 - c), me).wait_recv()
        for cp in first + passed:
            cp.wait_send()
        for cp in mine:
            cp.wait()

    return pl.pallas_call(
        body, name=name, in_specs=[_ANY] * n, out_specs=[_ANY] * n,
        out_shape=[jax.ShapeDtypeStruct((N_DEV,) + a.shape, a.dtype) for a in arrs],
        scratch_shapes=[pltpu.SemaphoreType.DMA((7 * n,)), pltpu.SemaphoreType.DMA((7 * n,)),
                        pltpu.SemaphoreType.DMA((n,))],
        compiler_params=pltpu.CompilerParams(has_side_effects=True),
    )(*arrs)


def _swap(arrs, n_out, plan, *, name):
    n = len(arrs)

    def body(*refs):
        in_refs, out_refs = refs[:n], refs[n:2 * n]
        send_sems, recv_sems = refs[2 * n:]
        copies = []
        for k in range(n_out):
            for w, (src, dst, to) in enumerate(plan(in_refs, out_refs, k)):
                copies.append(pltpu.make_async_remote_copy(
                    src_ref=src, dst_ref=dst, send_sem=send_sems.at[n_out * w + k],
                    recv_sem=recv_sems.at[n_out * w + k], device_id=to, device_id_type=MESH))
        for cp in copies:
            cp.start()
        for cp in copies:
            cp.wait_recv()
        for cp in copies:
            cp.wait_send()

    return pl.pallas_call(
        body, name=name, in_specs=[_ANY] * n, out_specs=[_ANY] * n,
        out_shape=[jax.ShapeDtypeStruct((n_out,) + a.shape[1:], a.dtype) for a in arrs],
        scratch_shapes=[pltpu.SemaphoreType.DMA((n_out * n,)), pltpu.SemaphoreType.DMA((n_out * n,))],
        compiler_params=pltpu.CompilerParams(has_side_effects=True),
    )(*arrs)


def _sibling_exchange(grads):
    def plan(in_refs, out_refs, k):
        x, y, c = _mesh_place()
        px, py = _chips(x, y)[k]
        return [(g.at[_slab(px, py, 1 - c)], o.at[k], (x, y, 1 - c)) for g, o in zip(in_refs, out_refs)]

    return _swap(grads, 4, plan, name="reduce_sibling")


def _chip_exchange(partials):
    def plan(in_refs, out_refs, k):
        x, y, c = _mesh_place()
        px, py = _chips(x, y)[k + 1]
        return [(g.at[k], o.at[k], (px, py, c)) for g, o in zip(in_refs, out_refs)]

    return _swap(partials, 3, plan, name="reduce_chips")


def _adamw(w, m, v, parts, *, name, tr=256):
    rows, cols = w.shape
    tr = _tile(rows, tr)
    n_p = len(parts)

    def body(*refs):
        w_ref, m_ref, v_ref = refs[:3]
        p_refs = refs[3:3 + n_p]
        g_ref, d_ref, nm_ref, nv_ref = refs[3 + n_p:]
        g = None
        for (_, lead), r in zip(parts, p_refs):
            for piece in ([r[...]] if lead is None else [r[i] for i in range(lead)]):
                g = piece.astype(F32) if g is None else g + piece.astype(F32)
        new_m = ADAM_B1 * m_ref[...] + (1.0 - ADAM_B1) * g
        new_v = ADAM_B2 * v_ref[...] + (1.0 - ADAM_B2) * (g * g)
        m_hat = new_m / (1.0 - ADAM_B1 ** ADAM_STEP)
        v_hat = new_v / (1.0 - ADAM_B2 ** ADAM_STEP)
        g_ref[...] = g
        d_ref[...] = -ADAM_LR * (m_hat / (jnp.sqrt(v_hat) + ADAM_EPS) + ADAM_WD * w_ref[...])
        nm_ref[...] = new_m
        nv_ref[...] = new_v

    tile = pl.BlockSpec((tr, cols), lambda i: (i, 0))
    p_specs = [tile if lead is None else pl.BlockSpec((lead, tr, cols), lambda i: (0, i, 0)) for _, lead in parts]
    return pl.pallas_call(
        body, name=name, grid=(rows // tr,), in_specs=[tile] * 3 + p_specs, out_specs=[tile] * 4,
        out_shape=[jax.ShapeDtypeStruct((rows, cols), F32)] * 4, compiler_params=_params(),
    )(w, m, v, *[a for a, _ in parts])


_PACK_TILE = SUBLANES * 128
_PACK_ROWS = 512


def _pack(arrs):
    flat = []
    for a in arrs:
        f = a.reshape(-1)
        flat.append(jnp.pad(f, (0, (-f.shape[0]) % _PACK_TILE)))
    total = sum(f.shape[0] for f in flat)
    flat.append(jnp.zeros(((-total) % (_PACK_ROWS * 128),), F32))
    return jnp.concatenate(flat).reshape(-1, 128)


def _unpack(buf, shapes):
    out, at = [], 0
    flat = buf.reshape(-1)
    for s in shapes:
        n = math.prod(s)
        out.append(flat[at:at + n].reshape(s))
        at += n + (-n) % _PACK_TILE
    return out


BIG = ("w_in", "w_glu", "w_out", "w_ffn_gate", "w_ffn_up", "w_ffn_down")
WEIGHTS = ("g_mix", "w_in", "q_gain", "k_gain", "rpb", "ssm_a_re", "ssm_a_im", "ssm_b_re", "ssm_b_im", "ssm_c_re",
           "ssm_c_im", "ssm_log_step", "ssm_d", "w_glu", "b_glu", "g_out_attn", "g_out_ssm", "w_out", "g_ffn",
           "w_ffn_gate", "w_ffn_up", "w_ffn_down")
SMALL = tuple(n for n in WEIGHTS if n not in BIG)
VECTORS = ("g_mix", "q_gain", "k_gain", "ssm_d", "b_glu", "g_out_attn", "g_out_ssm", "g_ffn")


def kernel(x, g_mix, w_in, q_gain, k_gain, rpb, ssm_a_re, ssm_a_im, ssm_b_re, ssm_b_im, ssm_c_re, ssm_c_im, ssm_log_step, ssm_d, w_glu, b_glu, g_out_attn, g_out_ssm, w_out, g_ffn, w_ffn_gate, w_ffn_up, w_ffn_down, loss_target, m_g_mix, m_w_in, m_q_gain, m_k_gain, m_rpb, m_ssm_a_re, m_ssm_a_im, m_ssm_b_re, m_ssm_b_im, m_ssm_c_re, m_ssm_c_im, m_ssm_log_step, m_ssm_d, m_w_glu, m_b_glu, m_g_out_attn, m_g_out_ssm, m_w_out, m_g_ffn, m_w_ffn_gate, m_w_ffn_up, m_w_ffn_down, v_g_mix, v_w_in, v_q_gain, v_k_gain, v_rpb, v_ssm_a_re, v_ssm_a_im, v_ssm_b_re, v_ssm_b_im, v_ssm_c_re, v_ssm_c_im, v_ssm_log_step, v_ssm_d, v_w_glu, v_b_glu, v_g_out_attn, v_g_out_ssm, v_w_out, v_g_ffn, v_w_ffn_gate, v_w_ffn_up, v_w_ffn_down):
    wts = dict(g_mix=g_mix, w_in=w_in, q_gain=q_gain, k_gain=k_gain, rpb=rpb, ssm_a_re=ssm_a_re, ssm_a_im=ssm_a_im,
               ssm_b_re=ssm_b_re, ssm_b_im=ssm_b_im, ssm_c_re=ssm_c_re, ssm_c_im=ssm_c_im, ssm_log_step=ssm_log_step,
               ssm_d=ssm_d, w_glu=w_glu, b_glu=b_glu, g_out_attn=g_out_attn, g_out_ssm=g_out_ssm, w_out=w_out,
               g_ffn=g_ffn, w_ffn_gate=w_ffn_gate, w_ffn_up=w_ffn_up, w_ffn_down=w_ffn_down)
    mom = dict(g_mix=m_g_mix, w_in=m_w_in, q_gain=m_q_gain, k_gain=m_k_gain, rpb=m_rpb, ssm_a_re=m_ssm_a_re,
               ssm_a_im=m_ssm_a_im, ssm_b_re=m_ssm_b_re, ssm_b_im=m_ssm_b_im, ssm_c_re=m_ssm_c_re,
               ssm_c_im=m_ssm_c_im, ssm_log_step=m_ssm_log_step, ssm_d=m_ssm_d, w_glu=m_w_glu, b_glu=m_b_glu,
               g_out_attn=m_g_out_attn, g_out_ssm=m_g_out_ssm, w_out=m_w_out, g_ffn=m_g_ffn,
               w_ffn_gate=m_w_ffn_gate, w_ffn_up=m_w_ffn_up, w_ffn_down=m_w_ffn_down)
    var = dict(g_mix=v_g_mix, w_in=v_w_in, q_gain=v_q_gain, k_gain=v_k_gain, rpb=v_rpb, ssm_a_re=v_ssm_a_re,
               ssm_a_im=v_ssm_a_im, ssm_b_re=v_ssm_b_re, ssm_b_im=v_ssm_b_im, ssm_c_re=v_ssm_c_re,
               ssm_c_im=v_ssm_c_im, ssm_log_step=v_ssm_log_step, ssm_d=v_ssm_d, w_glu=v_w_glu, b_glu=v_b_glu,
               g_out_attn=v_g_out_attn, g_out_ssm=v_g_out_ssm, w_out=v_w_out, g_ffn=v_g_ffn,
               w_ffn_gate=v_w_ffn_gate, w_ffn_up=v_w_ffn_up, w_ffn_down=v_w_ffn_down)
    ix, iy, ic = _mesh_place()
    me = _slab(ix, iy, ic)
    d_model = x.shape[-1]

    shard = {n: wts[n][0] for n in BIG}
    full = dict(zip(BIG, _all_gather([shard[n].astype(BF16) for n in BIG], name="gather_weights")))
    w = {"w_in": full["w_in"], "w_glu": full["w_glu"].reshape(-1, shard["w_glu"].shape[-1]),
         "w_out": full["w_out"].reshape(-1, d_model), "w_gate": full["w_ffn_gate"], "w_up": full["w_ffn_up"],
         "w_down": full["w_ffn_down"]}
    p = {n: (wts[n][0].reshape(1, -1) if n in VECTORS else wts[n][0]) for n in SMALL}

    loss, grad_x, g_small, g_big = _local_step(x[0], loss_target[0], p, w)
    loss = lax.psum(loss, ("x", "y", "c"))

    slabs = [g_big[n].reshape((N_DEV,) + shard[n].shape) for n in BIG]
    slabs = [s.reshape(N_DEV, -1, s.shape[-1]) for s in slabs]
    from_sibling = _sibling_exchange(slabs)
    chips = _chips(ix, iy)
    take = lambda a, i: lax.dynamic_index_in_dim(a, i, 0, keepdims=False)
    partials = []
    for i, n in enumerate(BIG):
        own = jnp.stack([take(slabs[i], _slab(px, py, ic)) for px, py in chips[1:]])
        rows, cols = own.shape[1:]
        partials.append(_rowwise(lambda a, b: a + b, [own.reshape(3 * rows, cols),
                                                      from_sibling[i][1:].reshape(3 * rows, cols)], [],
                                 [(cols, BF16)], name=f"reduce_add_{n}", tm=512)[0].reshape(3, rows, cols))
    from_chips = _chip_exchange(partials)
    out = {}
    for i, n in enumerate(BIG):
        rows, cols = slabs[i].shape[1:]
        res = _adamw(shard[n].reshape(rows, cols), mom[n][0].reshape(rows, cols), var[n][0].reshape(rows, cols),
                     [(take(slabs[i], me), None), (from_sibling[i], 1), (from_chips[i], 3)], name=f"adamw_{n}")
        out[n] = [r.reshape(wts[n].shape) for r in res]

    order = list(SMALL)
    shapes = [wts[n].shape for n in order]
    packed = _pack([g_small[n] for n in order])
    gathered = _all_gather([packed], name="gather_small_grads")[0]
    res = _adamw(_pack([wts[n] for n in order]), _pack([mom[n] for n in order]), _pack([var[n] for n in order]),
                 [(gathered, N_DEV)], name="adamw_small")
    for kind, buf in enumerate(res):
        for n, a in zip(order, _unpack(buf, shapes)):
            out.setdefault(n, [None] * 4)[kind] = a

    return (loss, grad_x[None], *[out[n][0] for n in WEIGHTS], *[out[n][1] for n in WEIGHTS],
            *[out[n][2] for n in WEIGHTS], *[out[n][3] for n in WEIGHTS])
```

```python
import functools
import math

import jax
import jax.numpy as jnp
from jax import lax
from jax.experimental import pallas as pl
from jax.experimental.pallas import tpu as pltpu

F32 = jnp.float32
BF16 = jnp.bfloat16

N_DEV = 8
GRID_W = 64
WIN_H = 8
WIN_W = 16
HEAD_DIM = 64
HEADS_PER_GROUP = 4
GROUP_LANES = HEADS_PER_GROUP * HEAD_DIM
SSM_C = 16
SSM_P = 64
GROUPS_PER_TILE = 8
U_TILE = GROUPS_PER_TILE * SSM_C
ST_TILE = GROUPS_PER_TILE * SSM_P
SUBLANES = 8
RMS_EPS = 1e-6
NEG_INF = -1e30
A_RE_MAX = -1e-4
ADAM_LR, ADAM_B1, ADAM_B2, ADAM_EPS, ADAM_WD, ADAM_STEP = 0.001, 0.9, 0.999, 1e-08, 0.01, 10
VMEM_LIMIT_V7X = 56 * 1024 * 1024
MESH = pl.DeviceIdType.MESH

_NN = (((1,), (0,)), ((), ()))
_NT = (((1,), (1,)), ((), ()))
_TN = (((0,), (0,)), ((), ()))
_DIMS = {"nn": _NN, "nt": _NT, "tn": _TN}


def _params(**kw):
    return pltpu.CompilerParams(vmem_limit_bytes=VMEM_LIMIT_V7X, **kw)


def _dot(a, b, dims=_NN):
    return lax.dot_general(a, b, dims, preferred_element_type=F32)


def _mm(a, b, *, name, grid, a_spec, b_spec, o_spec, o_shape, dims, k_axis=None, res=None, out_dtype=F32,
        exact=False, second=None):
    dn = _DIMS[dims]
    nk = 1 if k_axis is None else grid[k_axis]
    acc_shape = tuple(d for d in o_spec.block_shape if d is not None)
    n_in = 2 + (2 if second is not None else 0)

    def body(*refs):
        a_ref, b_ref = refs[:2]
        r_ref = refs[n_in] if res is not None else None
        o_ref, acc = refs[-2:]
        if exact:
            p = lax.dot_general(a_ref[...], b_ref[...], dn, precision=lax.Precision.HIGHEST,
                                preferred_element_type=F32)
        else:
            p = _dot(a_ref[...].astype(BF16), b_ref[...].astype(BF16), dn)
        if second is not None:
            p = p + _dot(refs[2][...].astype(BF16), refs[3][...].astype(BF16), dn)

        def finish(v):
            if r_ref is not None:
                v = v + r_ref[...].astype(F32)
            o_ref[...] = v.astype(out_dtype)

        if nk == 1:
            finish(p)
        else:
            k = pl.program_id(k_axis)

            @pl.when(k == 0)
            def _():
                acc[...] = p

            @pl.when(k > 0)
            def _():
                acc[...] += p

            @pl.when(k == nk - 1)
            def _():
                finish(acc[...])

    ins = [a, b] + (list(second) if second is not None else []) + ([res] if res is not None else [])
    in_specs = [a_spec, b_spec] * (n_in // 2) + ([o_spec] if res is not None else [])
    return pl.pallas_call(
        body, name=name, grid=grid, in_specs=in_specs, out_specs=o_spec,
        out_shape=jax.ShapeDtypeStruct(o_shape, out_dtype),
        scratch_shapes=[pltpu.VMEM(acc_shape if nk > 1 else (SUBLANES, 128), F32)],
        compiler_params=_params(),
    )(*ins)


def _tile(n, want):
    if n <= want:
        return n
    t = want
    while n % t:
        t //= 2
    return t


def _mm_plain(a, b, dims, *, name, res=None, out_dtype=F32, tm=512, tn=512, tk=512, exact=False):
    if dims == "nn":
        (m, k), n = a.shape, b.shape[1]
    elif dims == "nt":
        (m, k), n = a.shape, b.shape[0]
    else:
        (k, m), n = a.shape, b.shape[1]
    tm, tn = _tile(m, tm), _tile(n, tn)
    if dims == "tn":
        tk = _tile(k, tk)
        grid = (m // tm, n // tn, k // tk)
        a_spec = pl.BlockSpec((tk, tm), lambda i, j, kk: (kk, i))
        b_spec = pl.BlockSpec((tk, tn), lambda i, j, kk: (kk, j))
        o_spec = pl.BlockSpec((tm, tn), lambda i, j, kk: (i, j))
        return _mm(a, b, name=name, grid=grid, a_spec=a_spec, b_spec=b_spec, o_spec=o_spec, o_shape=(m, n),
                   dims=dims, k_axis=2, res=res, out_dtype=out_dtype)
    grid = (n // tn, m // tm)
    a_spec = pl.BlockSpec((tm, k), lambda j, i: (i, 0))
    if dims == "nn":
        b_spec = pl.BlockSpec((k, tn), lambda j, i: (0, j))
    else:
        b_spec = pl.BlockSpec((tn, k), lambda j, i: (j, 0))
    o_spec = pl.BlockSpec((tm, tn), lambda j, i: (i, j))
    return _mm(a, b, name=name, grid=grid, a_spec=a_spec, b_spec=b_spec, o_spec=o_spec, o_shape=(m, n), dims=dims,
               res=res, out_dtype=out_dtype, exact=exact)


def _rowwise(fn, tiled, bcast, outs, accs=(), *, name, tm=256):
    m = tiled[0].shape[0]
    tm = _tile(m, tm)
    n_t, n_b, n_o, n_a = len(tiled), len(bcast), len(outs), len(accs)

    def body(*refs):
        ins = [r[...] for r in refs[: n_t + n_b]]
        o_refs = refs[n_t + n_b: n_t + n_b + n_o]
        a_refs = refs[n_t + n_b + n_o:]
        res = fn(*ins)
        if not isinstance(res, (tuple, list)):
            res = (res,)
        for r, v in zip(o_refs, res[:n_o]):
            r[...] = v.astype(r.dtype)
        first = pl.program_id(0) == 0
        for r, v in zip(a_refs, res[n_o:]):
            s = jnp.sum(v, axis=0, keepdims=True)

            @pl.when(first)
            def _():
                r[...] = s

            @pl.when(jnp.logical_not(first))
            def _():
                r[...] += s

    in_specs = [pl.BlockSpec((tm, t.shape[1]), lambda i: (i, 0)) for t in tiled]
    in_specs += [pl.BlockSpec(b.shape, lambda i, nd=b.ndim: (0,) * nd) for b in bcast]
    out_specs = [pl.BlockSpec((tm, n), lambda i: (i, 0)) for n, _ in outs]
    out_specs += [pl.BlockSpec((1, n), lambda i: (0, 0)) for n in accs]
    out_shape = [jax.ShapeDtypeStruct((m, n), dt) for n, dt in outs]
    out_shape += [jax.ShapeDtypeStruct((1, n), F32) for n in accs]
    res = pl.pallas_call(body, name=name, grid=(m // tm,), in_specs=in_specs, out_specs=out_specs,
                         out_shape=out_shape, compiler_params=_params())(*tiled, *bcast)
    return res


def _rstd(x):
    return lax.rsqrt(jnp.mean(x * x, axis=-1, keepdims=True) + RMS_EPS)


def _rms_bwd(dh, x, g):
    xh = x * _rstd(x)
    dxh = dh * g
    dx = _rstd(x) * (dxh - xh * jnp.mean(dxh * xh, axis=-1, keepdims=True))
    return dx, dh * xh


def _sigmoid(x):
    return 1.0 / (1.0 + jnp.exp(-x))


_GELU_K = math.sqrt(2.0 / math.pi)
_GELU_C = 0.044715


def _gelu(x):
    return 0.5 * x * (1.0 + jnp.tanh(_GELU_K * (x + _GELU_C * x * x * x)))


def _gelu_grad(x):
    th = jnp.tanh(_GELU_K * (x + _GELU_C * x * x * x))
    return 0.5 * (1.0 + th) + 0.5 * x * (1.0 - th * th) * _GELU_K * (1.0 + 3.0 * _GELU_C * x * x)


class _Exchange:
    def __init__(self, arrays, outs, n_sems, sends, recvs=None, local=None, aliases=None):
        self.arrays, self.outs, self.n_sems = list(arrays), list(outs), n_sems
        self.sends, self.local, self.aliases = sends, local, aliases or {}
        self.recvs = recvs or (lambda i, o: [(k, dst) for k, _, dst, _ in sends(i, o)])

    def descriptors(self, in_refs, out_refs, send_sems, recv_sems, local_sems):
        me = _mesh_place()
        remote = lambda k, src, dst, to: pltpu.make_async_remote_copy(
            src_ref=src, dst_ref=dst, send_sem=send_sems.at[k], recv_sem=recv_sems.at[k], device_id=to,
            device_id_type=MESH)
        out = [remote(*s) for s in self.sends(in_refs, out_refs)]
        arrive = [remote(k, dst, dst, me) for k, dst in self.recvs(in_refs, out_refs)]
        own = [pltpu.make_async_copy(src, dst, local_sems.at[i])
               for i, (src, dst) in enumerate(self.local(in_refs, out_refs) if self.local else [])]
        return out, arrive, own

    def start(self, *refs):
        out, _, own = self.descriptors(*refs)
        for cp in own + out:
            cp.start()

    def finish(self, *refs):
        out, arrive, own = self.descriptors(*refs)
        for cp in arrive:
            cp.wait_recv()
        for cp in out:
            cp.wait_send()
        for cp in own:
            cp.wait()


def _call(body, *, name, grid, in_specs, out_specs, out_shape, scratch_shapes, args, ride=None, first=None, last=None):
    if ride is None:
        res = pl.pallas_call(body, name=name, grid=grid, in_specs=in_specs, out_specs=out_specs, out_shape=out_shape,
                             scratch_shapes=scratch_shapes, compiler_params=_params())(*args)
        return list(res), []
    n_in, n_out, n_scr = len(in_specs), len(out_specs), len(scratch_shapes)
    r_in, r_out = len(ride.arrays), len(ride.outs)

    def wrapped(*refs):
        ins, refs = refs[:n_in], refs[n_in:]
        x_in, refs = refs[:r_in], refs[r_in:]
        outs, refs = refs[:n_out], refs[n_out:]
        x_out, refs = refs[:r_out], refs[r_out:]
        scr, sems = refs[:n_scr], refs[n_scr:]

        @pl.when(first())
        def _():
            ride.start(x_in, x_out, *sems)

        body(*ins, *outs, *scr)

        @pl.when(last())
        def _():
            ride.finish(x_in, x_out, *sems)

    n_local = max(1, len(ride.arrays))
    res = pl.pallas_call(
        wrapped, name=name, grid=grid, in_specs=list(in_specs) + [_ANY] * r_in,
        out_specs=list(out_specs) + [_ANY] * r_out, out_shape=list(out_shape) + ride.outs,
        scratch_shapes=list(scratch_shapes) + [pltpu.SemaphoreType.DMA((ride.n_sems,)),
                                               pltpu.SemaphoreType.DMA((ride.n_sems,)),
                                               pltpu.SemaphoreType.DMA((n_local,))],
        input_output_aliases={n_in + i: n_out + o for i, o in ride.aliases.items()},
        compiler_params=_params(has_side_effects=True),
    )(*args, *ride.arrays)
    return list(res[:n_out]), list(res[n_out:])


def _gather_first(shards):
    def sends(i, o):
        x, y, c = _mesh_place()
        peers = [(x, y, 1 - c)] + [(px, py, c) for px, py in _chips(x, y)[1:]]
        return [(4 * w + k, i[w], o[w].at[_slab(x, y, c)], to) for w in range(len(i)) for k, to in enumerate(peers)]

    def recvs(i, o):
        x, y, c = _mesh_place()
        peers = [(x, y, 1 - c)] + [(px, py, c) for px, py in _chips(x, y)[1:]]
        return [(4 * w + k, o[w].at[_slab(*peer)]) for w in range(len(i)) for k, peer in enumerate(peers)]

    def local(i, o):
        return [(i[w], o[w].at[_slab(*_mesh_place())]) for w in range(len(i))]

    outs = [jax.ShapeDtypeStruct((N_DEV,) + a.shape, a.dtype) for a in shards]
    return _Exchange(shards, outs, 4 * len(shards), sends, recvs, local)


def _gather_second(gathered):
    def sends(i, o):
        x, y, c = _mesh_place()
        return [(3 * w + j, o[w].at[_slab(px, py, c)], o[w].at[_slab(px, py, c)], (x, y, 1 - c))
                for w in range(len(o)) for j, (px, py) in enumerate(_chips(x, y)[1:])]

    def recvs(i, o):
        x, y, c = _mesh_place()
        return [(3 * w + j, o[w].at[_slab(px, py, 1 - c)])
                for w in range(len(o)) for j, (px, py) in enumerate(_chips(x, y)[1:])]

    outs = [jax.ShapeDtypeStruct(a.shape, a.dtype) for a in gathered]
    return _Exchange(gathered, outs, 3 * len(gathered), sends, recvs, aliases={w: w for w in range(len(gathered))})


def _reduce_sibling(slabs):
    def sends(i, o):
        x, y, c = _mesh_place()
        return [(4 * w + k, i[w].at[_slab(px, py, 1 - c)], o[w].at[k], (x, y, 1 - c))
                for w in range(len(i)) for k, (px, py) in enumerate(_chips(x, y))]

    outs = [jax.ShapeDtypeStruct((4,) + a.shape[1:], a.dtype) for a in slabs]
    return _Exchange(slabs, outs, 4 * len(slabs), sends)


def _reduce_chips(partials):
    def sends(i, o):
        x, y, c = _mesh_place()
        return [(3 * w + k, i[w].at[k], o[w].at[k], (px, py, c))
                for w in range(len(i)) for k, (px, py) in enumerate(_chips(x, y)[1:])]

    outs = [jax.ShapeDtypeStruct(a.shape, a.dtype) for a in partials]
    return _Exchange(partials, outs, 3 * len(partials), sends)


def _head_masks():
    lane_head = lax.broadcasted_iota(jnp.int32, (1, GROUP_LANES), 1) // HEAD_DIM
    return [(lane_head == h).astype(F32) for h in range(HEADS_PER_GROUP)]


def _head_block_diag():
    r = lax.broadcasted_iota(jnp.int32, (GROUP_LANES, GROUP_LANES), 0) // HEAD_DIM
    c = lax.broadcasted_iota(jnp.int32, (GROUP_LANES, GROUP_LANES), 1) // HEAD_DIM
    return (r == c).astype(BF16)


def _head_mean(x, bd):
    hi = x.astype(BF16)
    lo = (x - hi.astype(F32)).astype(BF16)
    return (_dot(hi, bd) + _dot(lo, bd)) * (1.0 / HEAD_DIM)


def _stack_heads(x, masks):
    return jnp.concatenate([x * m for m in masks], axis=0)


def _unstack_heads(xs, masks):
    out = xs[0:GRID_W] * masks[0]
    for h in range(1, HEADS_PER_GROUP):
        out = out + xs[h * GRID_W:(h + 1) * GRID_W] * masks[h]
    return out


def _row_start(r, rows):
    return jnp.clip(r - WIN_H // 2, 0, rows - WIN_H)


def _attn_common_specs(seq, n_hg, rows):
    win_keys = WIN_H * GRID_W
    q_spec = pl.BlockSpec((GRID_W, GROUP_LANES), lambda g, r: (r, g))
    k_spec = pl.BlockSpec((seq, GROUP_LANES), lambda g, r: (0, n_hg + g))
    v_spec = pl.BlockSpec((seq, GROUP_LANES), lambda g, r: (0, 2 * n_hg + g))
    gain_spec = pl.BlockSpec((1, GROUP_LANES), lambda g, r: (0, 0))

    def variant(r):
        return _row_start(r, rows) - r + (WIN_H - 1)

    bias_spec = pl.BlockSpec((None, None, HEADS_PER_GROUP, GRID_W, win_keys), lambda g, r: (g, variant(r), 0, 0, 0))
    return q_spec, k_spec, v_spec, gain_spec, bias_spec, variant


def _attn_prepare_kv(k_ref, v_ref, kg, kn_scr, vb_scr, bd, seq):
    chunk = _tile(seq, 512)

    def step(c, carry):
        rows = pl.ds(pl.multiple_of(c * chunk, chunk), chunk)
        k = k_ref[rows, :]
        kn_scr[rows, :] = (k * lax.rsqrt(_head_mean(k * k, bd) + RMS_EPS) * kg).astype(BF16)
        vb_scr[rows, :] = v_ref[rows, :].astype(BF16)
        return carry

    lax.fori_loop(0, seq // chunk, step, 0)


def _attn_probs(qn, kw, bias, masks):
    qs = _stack_heads(qn, masks).astype(BF16)
    s = _dot(qs, kw, _NT) * (1.0 / math.sqrt(HEAD_DIM)) + bias
    m = jnp.max(s, axis=-1, keepdims=True)
    p = jnp.exp(s - m)
    return qs, p / jnp.sum(p, axis=-1, keepdims=True)


def _grid_ends(grid):
    first = lambda: functools.reduce(jnp.logical_and, [pl.program_id(a) == 0 for a in range(len(grid))])
    last = lambda: functools.reduce(jnp.logical_and, [pl.program_id(a) == n - 1 for a, n in enumerate(grid)])
    return first, last


def _attn_fwd(z, qg4, kg4, btab, ride=None):
    seq = z.shape[0]
    a_width = btab.shape[0] * GROUP_LANES
    n_hg, rows, win_keys = btab.shape[0], seq // GRID_W, WIN_H * GRID_W
    q_spec, k_spec, v_spec, gain_spec, bias_spec, _ = _attn_common_specs(seq, n_hg, rows)

    def body(q_ref, k_ref, v_ref, qg_ref, kg_ref, b_ref, o_ref, kn_scr, vb_scr):
        r = pl.program_id(1)
        bd, masks = _head_block_diag(), _head_masks()

        @pl.when(r == 0)
        def _():
            _attn_prepare_kv(k_ref, v_ref, kg_ref[...], kn_scr, vb_scr, bd, seq)

        win = pl.ds(pl.multiple_of(_row_start(r, rows) * GRID_W, GRID_W), win_keys)
        q = q_ref[...]
        qn = q * lax.rsqrt(_head_mean(q * q, bd) + RMS_EPS) * qg_ref[...]
        bias = b_ref[...].reshape(HEADS_PER_GROUP * GRID_W, win_keys)
        _, p = _attn_probs(qn, kn_scr[win, :], bias, masks)
        o_ref[...] = _unstack_heads(_dot(p.astype(BF16), vb_scr[win, :]), masks)

    first, last = _grid_ends((n_hg, rows))
    (ya,), rode = _call(
        body, name="attn_fwd", grid=(n_hg, rows),
        in_specs=[q_spec, k_spec, v_spec, gain_spec, gain_spec, bias_spec],
        out_specs=[pl.BlockSpec((GRID_W, GROUP_LANES), lambda g, r: (r, g))],
        out_shape=[jax.ShapeDtypeStruct((seq, a_width), F32)],
        scratch_shapes=[pltpu.VMEM((seq, GROUP_LANES), BF16), pltpu.VMEM((seq, GROUP_LANES), BF16)],
        args=(z, z, z, qg4, kg4, btab), ride=ride, first=first, last=last)
    return ya, rode


def _attn_bwd(z, d_out, qg4, kg4, btab, ride=None):
    seq = z.shape[0]
    n_hg, rows, win_keys = btab.shape[0], seq // GRID_W, WIN_H * GRID_W
    a_width = n_hg * GROUP_LANES
    q_spec, k_spec, v_spec, gain_spec, bias_spec, variant = _attn_common_specs(seq, n_hg, rows)
    scale = 1.0 / math.sqrt(HEAD_DIM)

    def body(q_ref, k_ref, v_ref, do_ref, qg_ref, kg_ref, b_ref,
             dq_ref, dk_out, dv_out, db_ref, dqg_ref, dkg_ref, kn_scr, vb_scr, dk_ref, dv_ref):
        r = pl.program_id(1)
        bd, masks = _head_block_diag(), _head_masks()

        @pl.when(r == 0)
        def _():
            _attn_prepare_kv(k_ref, v_ref, kg_ref[...], kn_scr, vb_scr, bd, seq)
            dk_ref[...] = jnp.zeros_like(dk_ref)
            dv_ref[...] = jnp.zeros_like(dv_ref)
            db_ref[...] = jnp.zeros_like(db_ref)
            dqg_ref[...] = jnp.zeros_like(dqg_ref)

        win = pl.ds(pl.multiple_of(_row_start(r, rows) * GRID_W, GRID_W), win_keys)
        q, qg = q_ref[...], qg_ref[...]
        rq = lax.rsqrt(_head_mean(q * q, bd) + RMS_EPS)
        qh = q * rq
        kw, vw = kn_scr[win, :], vb_scr[win, :]
        bias = b_ref[...].reshape(HEADS_PER_GROUP * GRID_W, win_keys)
        qs, p = _attn_probs(qh * qg, kw, bias, masks)
        dos = _stack_heads(do_ref[...], masks).astype(BF16)
        dp = _dot(dos, vw, _NT)
        ds = p * (dp - jnp.sum(p * dp, axis=-1, keepdims=True))
        db_ref[variant(r)] += ds.reshape(HEADS_PER_GROUP, GRID_W, win_keys)
        dsb = ds.astype(BF16)
        dqn = _unstack_heads(_dot(dsb, kw), masks) * scale
        dk_ref[win, :] += _dot(dsb, qs, _TN) * scale
        dv_ref[win, :] += _dot(p.astype(BF16), dos, _TN)
        dqg_ref[...] += jnp.sum(dqn * qh, axis=0, keepdims=True)
        dqh = dqn * qg
        dq_ref[...] = (rq * (dqh - qh * _head_mean(dqh * qh, bd))).astype(BF16)

        @pl.when(r == rows - 1)
        def _():
            chunk = _tile(seq, 512)
            kg = kg_ref[...]

            def step(c, dkg):
                rws = pl.ds(pl.multiple_of(c * chunk, chunk), chunk)
                k = k_ref[rws, :]
                rk = lax.rsqrt(_head_mean(k * k, bd) + RMS_EPS)
                kh = k * rk
                dkn = dk_ref[rws, :]
                dkh = dkn * kg
                dk_out[rws, :] = (rk * (dkh - kh * _head_mean(dkh * kh, bd))).astype(BF16)
                dv_out[rws, :] = dv_ref[rws, :].astype(BF16)
                return dkg + jnp.sum(dkn * kh, axis=0, keepdims=True)

            dkg_ref[...] = lax.fori_loop(0, seq // chunk, step, jnp.zeros((1, GROUP_LANES), F32))

    col_spec = pl.BlockSpec((seq, GROUP_LANES), lambda g, r: (0, g))
    gsum_spec = pl.BlockSpec((None, 1, GROUP_LANES), lambda g, r: (g, 0, 0))
    first, last = _grid_ends((n_hg, rows))
    return _call(
        body, name="attn_bwd", grid=(n_hg, rows),
        in_specs=[q_spec, k_spec, v_spec, pl.BlockSpec((GRID_W, GROUP_LANES), lambda g, r: (r, g)),
                  gain_spec, gain_spec, bias_spec],
        out_specs=[pl.BlockSpec((GRID_W, GROUP_LANES), lambda g, r: (r, g)), col_spec, col_spec,
                   pl.BlockSpec((None, WIN_H, HEADS_PER_GROUP, GRID_W, win_keys), lambda g, r: (g, 0, 0, 0, 0)),
                   gsum_spec, gsum_spec],
        out_shape=[jax.ShapeDtypeStruct((seq, a_width), BF16)] * 3
        + [jax.ShapeDtypeStruct(btab.shape, F32)]
        + [jax.ShapeDtypeStruct((n_hg, 1, GROUP_LANES), F32)] * 2,
        scratch_shapes=[pltpu.VMEM((seq, GROUP_LANES), BF16), pltpu.VMEM((seq, GROUP_LANES), BF16),
                        pltpu.VMEM((seq, GROUP_LANES), F32), pltpu.VMEM((seq, GROUP_LANES), F32)],
        args=(z, z, z, d_out, qg4, kg4, btab), ride=ride, first=first, last=last)


def _bias_index():
    c = jnp.arange(GRID_W)
    col_start = jnp.clip(c - WIN_W // 2, 0, GRID_W - WIN_W)
    col_in = (c[None, :] >= col_start[:, None]) & (c[None, :] < col_start[:, None] + WIN_W)
    dc = jnp.clip(c[None, :] - c[:, None], -(WIN_W - 1), WIN_W - 1) + (WIN_W - 1)
    dr = jnp.arange(WIN_H)[:, None] + jnp.arange(WIN_H)[None, :]
    return col_in, dc, dr


def _bias_table(rpb):
    col_in, dc, _ = _bias_index()
    n_h = rpb.shape[0]
    n_hg = n_h // HEADS_PER_GROUP
    spread = ((jnp.arange(128)[:, None] == dc.reshape(1, -1)) & col_in.reshape(1, -1)).astype(F32)
    rows = jnp.stack([rpb[:, v:v + WIN_H] for v in range(WIN_H)], axis=1)
    rows = jnp.pad(rows, ((0, 0), (0, 0), (0, 0), (0, 128 - rows.shape[-1]))).reshape(n_h * WIN_H * WIN_H, 128)
    tab = _mm_plain(rows, spread, "nn", name="rpb_spread", tm=256, tn=2048, exact=True)
    tab = jnp.where(col_in.reshape(1, -1), tab, NEG_INF)
    tab = tab.reshape(n_hg, HEADS_PER_GROUP, WIN_H, WIN_H, GRID_W, GRID_W).transpose(0, 2, 1, 4, 3, 5)
    return tab.reshape(n_hg, WIN_H, HEADS_PER_GROUP, GRID_W, WIN_H * GRID_W)


def _bias_grad(dtab, n_h):
    col_in, dc, _ = _bias_index()
    onehot = (dc.reshape(-1, 1) == jnp.arange(128)[None, :]) & col_in.reshape(-1, 1)
    n_hg = n_h // HEADS_PER_GROUP
    d = dtab.reshape(n_hg, WIN_H, HEADS_PER_GROUP, GRID_W, WIN_H, GRID_W).transpose(0, 2, 1, 4, 3, 5)
    d = d.reshape(n_h * WIN_H * WIN_H, GRID_W * GRID_W)
    diag = _mm_plain(d, onehot.astype(BF16), "nn", name="rpb_diag_sum", tm=256, tn=128)
    diag = diag.reshape(n_h, WIN_H, WIN_H, 128)[..., : 2 * WIN_W - 1]
    out = jnp.zeros((n_h, 2 * WIN_H - 1, 2 * WIN_W - 1), F32)
    for v in range(WIN_H):
        out = out.at[:, v:v + WIN_H].add(diag[:, v])
    return out


def _cmul(ar, ai, br, bi):
    return ar * br - ai * bi, ar * bi + ai * br


def _s5_discretize(a_re, a_im, dt, b_re, b_im):
    c = b_re.shape[1]

    def fn(are, aim, dt_, bre, bim):
        lr, li = jnp.minimum(are, A_RE_MAX), aim
        mag = jnp.exp(lr * dt_)
        l1r, l1i = mag * jnp.cos(li * dt_), mag * jnp.sin(li * dt_)
        den = lr * lr + li * li
        nr, ni = l1r - 1.0, l1i
        cr, ci = (nr * lr + ni * li) / den, (ni * lr - nr * li) / den
        bbr, bbi = _cmul(cr, ci, bre, bim)
        shape = (are.shape[0], SUBLANES)
        lane = lax.broadcasted_iota(jnp.int32, shape, 1)
        pr, pi = l1r, l1i
        acc_r, acc_i = jnp.zeros(shape, F32), jnp.zeros(shape, F32)
        for k in range(SUBLANES):
            acc_r = jnp.where(lane == k, pr, acc_r)
            acc_i = jnp.where(lane == k, pi, acc_i)
            pr, pi = _cmul(pr, pi, l1r, l1i)
        return acc_r, acc_i, cr, ci, bbr, bbi

    return _rowwise(fn, [a_re, a_im, dt, b_re, b_im], [],
                    [(SUBLANES, F32), (SUBLANES, F32), (1, F32), (1, F32), (c, F32), (c, F32)],
                    name="s5_discretize", tm=1024)


def _s5_param_grads(a_re, a_im, dt, b_re, b_im, l1r, l1i, cr, ci, bbr, bbi, r_re, r_im, gb_re, gb_im):
    c = b_re.shape[1]

    def fn(are, aim, dt_, bre, bim, l1r_, l1i_, cr_, ci_, bbr_, bbi_, rr, ri, gbr, gbi):
        lr, li = jnp.minimum(are, A_RE_MAX), aim
        den = lr * lr + li * li
        dbr, dbi = _cmul(cr_, -ci_, gbr, gbi)
        gcr, gci = _cmul(bre, -bim, gbr, gbi)
        gcr, gci = jnp.sum(gcr, axis=1, keepdims=True), jnp.sum(gci, axis=1, keepdims=True)
        qr, qi = _cmul(bbr_, -bbi_, gbr, gbi)
        qr = rr - jnp.sum(qr, axis=1, keepdims=True)
        qi = ri - jnp.sum(qi, axis=1, keepdims=True)
        tr, ti = _cmul(gcr, gci, lr / den, li / den)
        ur, ui = _cmul(l1r_, -l1i_, tr, ti)
        gwr, gwi = qr + ur, qi + ui
        vr, vi = _cmul(cr_, -ci_, lr / den, li / den)
        vr, vi = _cmul(gcr, gci, vr, vi)
        glr, gli = dt_ * gwr - vr, dt_ * gwi - vi
        return jnp.where(are < A_RE_MAX, glr, 0.0), gli, (gwr * lr + gwi * li) * dt_, dbr, dbi

    return _rowwise(fn, [a_re, a_im, dt, b_re, b_im, l1r, l1i, cr, ci, bbr, bbi, r_re, r_im, gb_re, gb_im], [],
                    [(1, F32), (1, F32), (1, F32), (c, F32), (c, F32)], name="s5_param_grads", tm=1024)


def _s5_scan(v, win_re, win_im, tabs, wo_re, wo_im, *, reverse, name, t_chunk=256, ride=None):
    seq, width = v.shape
    n_tiles, n_state = width // U_TILE, width * (SSM_P // SSM_C)
    t_chunk = _tile(seq, t_chunk)
    n_chunks, n_blk = seq // t_chunk, t_chunk // SUBLANES
    last_row = 0 if reverse else SUBLANES - 1

    def chunk_of(j):
        return (n_chunks - 1 - j) if reverse else j

    def body(v_ref, wir_ref, wii_ref, tab_ref, wor_ref, woi_ref, sr_ref, si_ref, y_ref, carry):
        @pl.when(pl.program_id(0) == 0)
        def _():
            carry[...] = jnp.zeros_like(carry)

        for jt in range(n_tiles):
            ls = slice(jt * ST_TILE, (jt + 1) * ST_TILE)
            us = slice(jt * U_TILE, (jt + 1) * U_TILE)
            vj = v_ref[:, us].astype(BF16)
            sr_ref[:, ls] = _dot(vj, wir_ref[jt])
            si_ref[:, ls] = _dot(vj, wii_ref[jt])
            consts = [tab_ref[k, :, ls] for k in range(8)]

            def blk(b, c, ls=ls, consts=consts):
                cr, ci = c
                bb = (n_blk - 1 - b) if reverse else b
                rows = pl.ds(pl.multiple_of(bb * SUBLANES, SUBLANES), SUBLANES)
                xr, xi = sr_ref[rows, ls], si_ref[rows, ls]
                for s, k in enumerate((1, 2, 4)):
                    sh = (SUBLANES - k) if reverse else k
                    tr, ti = pltpu.roll(xr, sh, 0), pltpu.roll(xi, sh, 0)
                    lr, li = consts[2 * s], consts[2 * s + 1]
                    xr, xi = xr + lr * tr - li * ti, xi + lr * ti + li * tr
                lr, li = consts[6], consts[7]
                xr, xi = xr + lr * cr - li * ci, xi + lr * ci + li * cr
                sr_ref[rows, ls], si_ref[rows, ls] = xr, xi
                shape = (SUBLANES, ST_TILE)
                return (jnp.broadcast_to(xr[last_row:last_row + 1], shape),
                        jnp.broadcast_to(xi[last_row:last_row + 1], shape))

            cr, ci = lax.fori_loop(0, n_blk, blk, (carry[0, :, ls], carry[1, :, ls]), unroll=4)
            carry[0, :, ls], carry[1, :, ls] = cr, ci
            y_ref[:, us] = (_dot(sr_ref[:, ls].astype(BF16), wor_ref[jt])
                            + _dot(si_ref[:, ls].astype(BF16), woi_ref[jt]))

    whole = lambda a: pl.BlockSpec(a.shape, lambda j, nd=a.ndim: (0,) * nd)
    st_spec = pl.BlockSpec((t_chunk, n_state), lambda j: (chunk_of(j), 0))
    v_spec = pl.BlockSpec((t_chunk, width), lambda j: (chunk_of(j), 0))
    first, last = _grid_ends((n_chunks,))
    return _call(
        body, name=name, grid=(n_chunks,),
        in_specs=[v_spec, whole(win_re), whole(win_im), whole(tabs), whole(wo_re), whole(wo_im)],
        out_specs=[st_spec, st_spec, v_spec],
        out_shape=[jax.ShapeDtypeStruct((seq, n_state), F32)] * 2 + [jax.ShapeDtypeStruct((seq, width), F32)],
        scratch_shapes=[pltpu.VMEM((2, SUBLANES, n_state), F32)],
        args=(v, win_re, win_im, tabs, wo_re, wo_im), ride=ride, first=first, last=last)


def _s5_reduce(x_re, x_im, a_re, a_im, u, dy, *, name, t_chunk=512, ride=None):
    seq, n_state = x_re.shape
    width = u.shape[1]
    n_tiles = width // U_TILE
    t_chunk = _tile(seq, t_chunk)

    def body(xr_ref, xi_ref, ar_ref, ai_ref, u_ref, dy_ref, rr_ref, ri_ref, gbr_ref, gbi_ref, gcr_ref, gci_ref):
        xr, xi, ar, ai = xr_ref[...], xi_ref[...], ar_ref[...], ai_ref[...]
        ub, dyb = u_ref[...].astype(BF16), dy_ref[...].astype(BF16)
        parts = (jnp.sum(ar * xr + ai * xi, axis=0, keepdims=True), jnp.sum(ai * xr - ar * xi, axis=0, keepdims=True),
                 _dot(ar.astype(BF16), ub, _TN), _dot(ai.astype(BF16), ub, _TN),
                 _dot(xr.astype(BF16), dyb, _TN), _dot(xi.astype(BF16), dyb, _TN))
        first = pl.program_id(1) == 0
        for ref, val in zip((rr_ref, ri_ref, gbr_ref, gbi_ref, gcr_ref, gci_ref), parts):
            @pl.when(first)
            def _():
                ref[...] = val

            @pl.when(jnp.logical_not(first))
            def _():
                ref[...] += val

    st_spec = pl.BlockSpec((t_chunk, ST_TILE), lambda j, t: (t, j))
    u_spec = pl.BlockSpec((t_chunk, U_TILE), lambda j, t: (t, j))
    r_spec = pl.BlockSpec((1, ST_TILE), lambda j, t: (0, j))
    g_spec = pl.BlockSpec((None, ST_TILE, U_TILE), lambda j, t: (j, 0, 0))
    first, last = _grid_ends((n_tiles, seq // t_chunk))
    return _call(
        body, name=name, grid=(n_tiles, seq // t_chunk),
        in_specs=[st_spec] * 4 + [u_spec] * 2,
        out_specs=[r_spec, r_spec] + [g_spec] * 4,
        out_shape=[jax.ShapeDtypeStruct((1, n_state), F32)] * 2
        + [jax.ShapeDtypeStruct((n_tiles, ST_TILE, U_TILE), F32)] * 4,
        scratch_shapes=[], args=(x_re, x_im, a_re, a_im, u, dy), ride=ride, first=first, last=last)


def _block_diag_in(ms):
    m = jnp.stack(ms)
    n, g, c, p = m.shape
    m5 = m.reshape(n, g // GROUPS_PER_TILE, GROUPS_PER_TILE, c, p)
    eye = jnp.eye(GROUPS_PER_TILE, dtype=m.dtype)
    out = m5[:, :, :, :, None, :] * eye[None, None, :, None, :, None]
    return out.astype(BF16).reshape(n, g // GROUPS_PER_TILE, GROUPS_PER_TILE * c, GROUPS_PER_TILE * p)


def _block_diag_take(m, c, p):
    t = m.shape[0]
    m5 = m.reshape(t, GROUPS_PER_TILE, p, GROUPS_PER_TILE, c)
    idx = jnp.arange(GROUPS_PER_TILE)
    return m5[:, idx, :, idx, :].transpose(1, 0, 2, 3).reshape(t * GROUPS_PER_TILE, p, c)


def _scan_tables(pw_re, pw_im, reverse):
    row = jnp.arange(SUBLANES)[:, None]
    tabs = []
    for k in (1, 2, 4):
        keep = (row <= SUBLANES - 1 - k) if reverse else (row >= k)
        tabs += [jnp.where(keep, pw_re[k - 1][None, :], 0.0), jnp.where(keep, pw_im[k - 1][None, :], 0.0)]
    order = jnp.arange(SUBLANES)[::-1] if reverse else jnp.arange(SUBLANES)
    tabs += [pw_re[order], pw_im[order]]
    return jnp.stack(tabs)


def _partial_sums(slabs, from_sibling, names):
    x, y, c = _mesh_place()
    out = []
    for s, f, n in zip(slabs, from_sibling, names):
        own = jnp.stack([lax.dynamic_index_in_dim(s, _slab(px, py, c), 0, keepdims=False)
                         for px, py in _chips(x, y)[1:]])
        rows, cols = own.shape[1:]
        out.append(_rowwise(lambda a, b: a + b, [own.reshape(3 * rows, cols), f[1:].reshape(3 * rows, cols)], [],
                            [(cols, BF16)], name=f"reduce_add_{n}", tm=512)[0].reshape(3, rows, cols))
    return out


def _local_step(x, target, p, w_in, shards):
    seq, d_model = x.shape
    a_width = p["g_out_attn"].shape[-1]
    s_width = p["g_out_ssm"].shape[-1]
    n_heads = a_width // HEAD_DIM
    n_hg = n_heads // HEADS_PER_GROUP
    n_groups = s_width // SSM_C
    n_sh, _, in_sh = w_in.shape
    f_sh = shards["w_ffn_gate"].shape[-1]
    w = {"w_in": w_in}
    slab3 = lambda g, n: g.reshape(N_DEV, -1, shards[n].shape[-1])
    t2, t1 = _tile(seq, 2048), _tile(seq, 1024)
    n2, n1 = seq // t2, seq // t1

    h1 = _rowwise(lambda xv, g: x_norm(xv, g), [x], [p["g_mix"]], [(d_model, BF16)], name="rms_mix")[0]
    z = _mm(h1, w["w_in"], name="in_proj", grid=(n2, n_sh),
            a_spec=pl.BlockSpec((t2, d_model), lambda i, j: (i, 0)),
            b_spec=pl.BlockSpec((None, d_model, in_sh), lambda i, j: (j, 0, 0)),
            o_spec=pl.BlockSpec((t2, in_sh), lambda i, j: (i, j)), o_shape=(seq, n_sh * in_sh), dims="nn")
    qg4 = jnp.tile(p["q_gain"], (1, HEADS_PER_GROUP))
    kg4 = jnp.tile(p["k_gain"], (1, HEADS_PER_GROUP))
    btab = _bias_table(p["rpb"])
    ya, got_a = _attn_fwd(z, qg4, kg4, btab, ride=_gather_first([shards["w_ffn_gate"], shards["w_ffn_up"]]))
    u = z[:, 3 * a_width:]

    n_col = 2 * n_groups * SSM_P
    col = lambda a: a.reshape(n_col, 1)
    a_re_c, a_im_c = col(p["ssm_a_re"]), col(p["ssm_a_im"])
    dt_c = col(jnp.broadcast_to(jnp.exp(p["ssm_log_step"])[:, :, None], (2, n_groups, SSM_P)))
    b_re_c, b_im_c = p["ssm_b_re"].reshape(n_col, SSM_C), p["ssm_b_im"].reshape(n_col, SSM_C)
    pw_re, pw_im, cf_re, cf_im, bb_re, bb_im = _s5_discretize(a_re_c, a_im_c, dt_c, b_re_c, b_im_c)
    n_state = n_groups * SSM_P
    pw_re = pw_re.reshape(2, n_state, SUBLANES).transpose(0, 2, 1)
    pw_im = pw_im.reshape(2, n_state, SUBLANES).transpose(0, 2, 1)
    bb_re4, bb_im4 = bb_re.reshape(2, n_groups, SSM_P, SSM_C), bb_im.reshape(2, n_groups, SSM_P, SSM_C)
    c_re, c_im = p["ssm_c_re"], p["ssm_c_im"]
    t21 = lambda a: a.transpose(0, 2, 1)
    maps_in = _block_diag_in([m for d in range(2) for m in (t21(bb_re4[d]), t21(bb_im4[d]), c_re[d], -c_im[d])])
    maps_out = _block_diag_in([m for d in range(2) for m in (t21(c_re[d]), -t21(c_im[d]), bb_re4[d], bb_im4[d])])
    fwd, bwd_in = [], []
    got_b = None
    for d in range(2):
        rev = d == 1
        tabs = _scan_tables(pw_re[d], pw_im[d], rev)
        if d == 0:
            ride = _gather_first([shards["w_glu"], shards["w_out"], shards["w_ffn_down"]])
        else:
            ride = _gather_second(got_a + got_b)
        (xs_re, xs_im, y_d), got = _s5_scan(u, maps_in[4 * d], maps_in[4 * d + 1], tabs, maps_out[4 * d],
                                            maps_out[4 * d + 1], reverse=rev, name=f"s5_fwd_{d}", ride=ride)
        if d == 0:
            got_b = got
        fwd.append((xs_re, xs_im, y_d))
        bwd_in.append((maps_in[4 * d + 2], maps_in[4 * d + 3], _scan_tables(pw_re[d], -pw_im[d], not rev),
                       maps_out[4 * d + 2], maps_out[4 * d + 3]))
    w["w_gate"], w["w_up"], w_glu_full, w_out_full, w["w_down"] = got
    w["w_glu"] = w_glu_full.reshape(-1, s_width)
    w["w_out"] = w_out_full.reshape(-1, d_model)

    ypre, yg = _rowwise(lambda y0, y1, uv, dsk: s5_mid(y0, y1, uv, dsk), [fwd[0][2], fwd[1][2], u], [p["ssm_d"]],
                        [(s_width, F32), (s_width, F32)], name="s5_skip_gelu")
    t_glu = _mm_plain(yg, w["w_glu"], "nn", name="glu_proj", tn=s_width)
    y_cat = _rowwise(mix_out_fwd, [ya, yg, t_glu], [p["b_glu"], p["g_out_attn"], p["g_out_ssm"]],
                     [(a_width + s_width, BF16)], name="mix_out")[0]
    x1 = _mm_plain(y_cat, w["w_out"], "nn", name="out_proj", res=x, tn=2048)

    h2 = _rowwise(lambda xv, g: x_norm(xv, g), [x1], [p["g_ffn"]], [(d_model, BF16)], name="rms_ffn")[0]
    ffn_up = functools.partial(
        _mm, grid=(n2, n_sh), a_spec=pl.BlockSpec((t2, d_model), lambda i, j: (i, 0)),
        b_spec=pl.BlockSpec((None, d_model, f_sh), lambda i, j: (j, 0, 0)),
        o_spec=pl.BlockSpec((None, t2, f_sh), lambda i, j: (j, i, 0)), o_shape=(n_sh, seq, f_sh), dims="nn",
        out_dtype=BF16)
    gate = ffn_up(h2, w["w_gate"], name="ffn_gate")
    up = ffn_up(h2, w["w_up"], name="ffn_up")
    flat = lambda a: a.reshape(n_sh * seq, f_sh)
    act = _rowwise(swiglu_fwd, [flat(gate), flat(up)], [], [(f_sh, BF16)], name="swiglu",
                   tm=1024)[0].reshape(n_sh, seq, f_sh)
    ffn_out = _mm(act, w["w_down"], name="ffn_down", grid=(n1, n_sh),
                  a_spec=pl.BlockSpec((None, t1, f_sh), lambda i, j: (j, i, 0)),
                  b_spec=pl.BlockSpec((None, f_sh, d_model), lambda i, j: (j, 0, 0)),
                  o_spec=pl.BlockSpec((t1, d_model), lambda i, j: (i, 0)), o_shape=(seq, d_model), dims="nn",
                  k_axis=1)

    dx2, dx2_b, sq = _rowwise(functools.partial(loss_head, inv_d=1.0 / d_model), [ffn_out, x1, target], [],
                              [(d_model, F32), (d_model, BF16)], [d_model], name="loss_head")
    loss = 0.5 * jnp.sum(sq) / d_model

    d_act = _mm(dx2_b, w["w_down"], name="ffn_down_dx", grid=(n2, n_sh),
                a_spec=pl.BlockSpec((t2, d_model), lambda i, j: (i, 0)),
                b_spec=pl.BlockSpec((None, f_sh, d_model), lambda i, j: (j, 0, 0)),
                o_spec=pl.BlockSpec((None, t2, f_sh), lambda i, j: (j, i, 0)), o_shape=(n_sh, seq, f_sh), dims="nt",
                out_dtype=BF16)
    g_w_down = _mm(act, dx2_b, name="ffn_down_dw", grid=(n_sh, n1),
                   a_spec=pl.BlockSpec((None, t1, f_sh), lambda j, k: (j, k, 0)),
                   b_spec=pl.BlockSpec((t1, d_model), lambda j, k: (k, 0)),
                   o_spec=pl.BlockSpec((None, f_sh, d_model), lambda j, k: (j, 0, 0)),
                   o_shape=(n_sh, f_sh, d_model), dims="tn", k_axis=1)
    d_gate, d_up = _rowwise(swiglu_bwd, [flat(d_act), flat(gate), flat(up)], [], [(f_sh, BF16), (f_sh, BF16)],
                            name="swiglu_bwd", tm=1024)
    d_gate, d_up = d_gate.reshape(n_sh, seq, f_sh), d_up.reshape(n_sh, seq, f_sh)
    d_h2 = _mm(d_gate, w["w_gate"], second=(d_up, w["w_up"]), name="ffn_up_gate_dx", grid=(n1, n_sh),
               a_spec=pl.BlockSpec((None, t1, f_sh), lambda i, j: (j, i, 0)),
               b_spec=pl.BlockSpec((None, d_model, f_sh), lambda i, j: (j, 0, 0)),
               o_spec=pl.BlockSpec((t1, d_model), lambda i, j: (i, 0)), o_shape=(seq, d_model), dims="nt", k_axis=1)
    ffn_dw = functools.partial(
        _mm, grid=(n_sh, n1), a_spec=pl.BlockSpec((t1, d_model), lambda j, k: (k, 0)),
        b_spec=pl.BlockSpec((None, t1, f_sh), lambda j, k: (j, k, 0)),
        o_spec=pl.BlockSpec((None, d_model, f_sh), lambda j, k: (j, 0, 0)), o_shape=(n_sh, d_model, f_sh), dims="tn",
        k_axis=1)
    g_w_gate = ffn_dw(h2, d_gate, name="ffn_gate_dw")
    g_w_up = ffn_dw(h2, d_up, name="ffn_up_dw")
    dx1, g_g_ffn = _rowwise(residual_rms_bwd, [dx2, d_h2, x1], [p["g_ffn"]], [(d_model, F32)], [d_model],
                            name="rms_ffn_bwd")

    d_ycat = _mm_plain(dx1, w["w_out"], "nt", name="out_proj_dx", tn=2048)
    g_w_out = _mm_plain(y_cat, dx1, "tn", name="out_proj_dw", tm=1024, tn=2048)
    (d_ya, d_yg_direct, d_t, g_goa, g_gos, g_b_glu) = _rowwise(
        functools.partial(mix_out_bwd, a_width=a_width), [d_ycat, ya, yg, t_glu],
        [p["b_glu"], p["g_out_attn"], p["g_out_ssm"]],
        [(a_width, F32), (s_width, F32), (s_width, BF16)], [a_width, s_width, s_width], name="mix_out_bwd")
    d_yg = _mm_plain(d_t, w["w_glu"], "nt", name="glu_proj_dx", res=d_yg_direct, tn=s_width)
    g_w_glu = _mm_plain(yg, d_t, "tn", name="glu_proj_dw", tm=1024, tn=1024)
    d_ypre, du_skip, g_ssm_d = _rowwise(gelu_skip_bwd, [d_yg, ypre, u], [p["ssm_d"]],
                                        [(s_width, F32), (s_width, F32)], [s_width], name="s5_skip_gelu_bwd")

    ffn_names, mix_names = ("w_ffn_gate", "w_ffn_up", "w_ffn_down"), ("w_glu", "w_out")
    ffn_slabs = [slab3(g, n) for g, n in zip((g_w_gate, g_w_up, g_w_down), ffn_names)]
    mix_slabs = [slab3(g, n) for g, n in zip((g_w_glu, g_w_out), mix_names)]
    du_dirs, adj, r_parts, gb_parts, gc_parts = [], [], [], [], []
    sib, part = {}, {}
    for d, (names, slabs) in enumerate(((ffn_names, ffn_slabs), (mix_names, mix_slabs))):
        win_re, win_im, tabs, wo_re, wo_im = bwd_in[d]
        (as_re, as_im, du_d), got = _s5_scan(d_ypre, win_re, win_im, tabs, wo_re, wo_im, reverse=(d == 0),
                                             name=f"s5_bwd_{d}", ride=_reduce_sibling(slabs))
        du_dirs.append(du_d)
        adj.append((as_re, as_im))
        sib[names] = got
        part[names] = _partial_sums(slabs, got, names)
    for d in range(2):
        (r_re, r_im, gbt_re, gbt_im, gct_re, gct_im), got = _s5_reduce(
            fwd[d][0], fwd[d][1], adj[d][0], adj[d][1], u, d_ypre, name=f"s5_reduce_{d}",
            ride=_reduce_chips(part[mix_names]) if d == 0 else None)
        if d == 0:
            mix_chips = got
        r_parts.append((r_re.reshape(n_state, 1), r_im.reshape(n_state, 1)))
        gb_parts.append((_block_diag_take(gbt_re, SSM_C, SSM_P), _block_diag_take(gbt_im, SSM_C, SSM_P)))
        gc_parts.append((_block_diag_take(gct_re, SSM_C, SSM_P), _block_diag_take(gct_im, SSM_C, SSM_P)))
    cat = lambda i, parts: jnp.concatenate([parts[0][i], parts[1][i]], axis=0)
    gbb_re, gbb_im = cat(0, gb_parts).reshape(n_col, SSM_C), cat(1, gb_parts).reshape(n_col, SSM_C)
    g_a_re, g_a_im, g_ls, g_b_re, g_b_im = _s5_param_grads(
        a_re_c, a_im_c, dt_c, b_re_c, b_im_c, pw_re[:, 0].reshape(n_col, 1), pw_im[:, 0].reshape(n_col, 1),
        cf_re, cf_im, bb_re, bb_im, cat(0, r_parts), cat(1, r_parts), gbb_re, gbb_im)
    g_c_re = cat(0, gc_parts).reshape(2, n_groups, SSM_P, SSM_C).transpose(0, 1, 3, 2)
    g_c_im = -cat(1, gc_parts).reshape(2, n_groups, SSM_P, SSM_C).transpose(0, 1, 3, 2)

    (d_q, d_k, d_v, d_btab, g_qg, g_kg), ffn_chips = _attn_bwd(z, d_ya, qg4, kg4, btab,
                                                                ride=_reduce_chips(part[ffn_names]))
    d_u = _rowwise(lambda a, b, c: a + b + c, [du_dirs[0], du_dirs[1], du_skip], [], [(s_width, BF16)],
                   name="s5_du_sum")[0]
    d_z = jnp.concatenate([d_q, d_k, d_v, d_u], axis=1)
    d_h1 = _mm(d_z, w["w_in"], name="in_proj_dx", grid=(n1, n_sh),
               a_spec=pl.BlockSpec((t1, in_sh), lambda i, j: (i, j)),
               b_spec=pl.BlockSpec((None, d_model, in_sh), lambda i, j: (j, 0, 0)),
               o_spec=pl.BlockSpec((t1, d_model), lambda i, j: (i, 0)), o_shape=(seq, d_model), dims="nt", k_axis=1)
    g_w_in = _mm(h1, d_z, name="in_proj_dw", grid=(n_sh, n1),
                 a_spec=pl.BlockSpec((t1, d_model), lambda j, k: (k, 0)),
                 b_spec=pl.BlockSpec((t1, in_sh), lambda j, k: (k, j)),
                 o_spec=pl.BlockSpec((None, d_model, in_sh), lambda j, k: (j, 0, 0)),
                 o_shape=(n_sh, d_model, in_sh), dims="tn", k_axis=1)
    grad_x, g_g_mix = _rowwise(residual_rms_bwd, [dx1, d_h1, x], [p["g_mix"]], [(d_model, F32)], [d_model],
                               name="rms_mix_bwd")

    fold_heads = lambda g: g.reshape(n_heads, HEAD_DIM).sum(axis=0, keepdims=True)
    small = {
        "g_mix": g_g_mix, "q_gain": fold_heads(g_qg), "k_gain": fold_heads(g_kg),
        "rpb": _bias_grad(d_btab, n_heads),
        "ssm_a_re": g_a_re.reshape(2, n_groups, SSM_P), "ssm_a_im": g_a_im.reshape(2, n_groups, SSM_P),
        "ssm_b_re": g_b_re.reshape(2, n_groups, SSM_P, SSM_C), "ssm_b_im": g_b_im.reshape(2, n_groups, SSM_P, SSM_C),
        "ssm_c_re": g_c_re, "ssm_c_im": g_c_im,
        "ssm_log_step": g_ls.reshape(2, n_groups, SSM_P).sum(axis=-1),
        "ssm_d": g_ssm_d, "b_glu": g_b_glu, "g_out_attn": g_goa, "g_out_ssm": g_gos, "g_ffn": g_g_ffn,
    }
    reduced = {}
    for names, slabs, chips in ((ffn_names, ffn_slabs, ffn_chips), (mix_names, mix_slabs, mix_chips)):
        for i, n in enumerate(names):
            reduced[n] = (slabs[i], sib[names][i], chips[i])
    return loss, grad_x, small, g_w_in, reduced


def x_norm(xv, g):
    return xv * _rstd(xv) * g


def s5_mid(y0, y1, uv, d_skip):
    ypre = y0 + y1 + d_skip * uv
    return ypre, _gelu(ypre)


def mix_out_fwd(ya, yg, t, b_glu, g_oa, g_os):
    ys = yg * _sigmoid(t + b_glu)
    return jnp.concatenate([ya * _rstd(ya) * g_oa, ys * _rstd(ys) * g_os], axis=1)


def mix_out_bwd(d_y, ya, yg, t, b_glu, g_oa, g_os, *, a_width):
    sg = _sigmoid(t + b_glu)
    ys = yg * sg
    d_ya, c_goa = _rms_bwd(d_y[:, :a_width], ya, g_oa)
    d_ys, c_gos = _rms_bwd(d_y[:, a_width:], ys, g_os)
    d_t = d_ys * yg * sg * (1.0 - sg)
    return d_ya, d_ys * sg, d_t, c_goa, c_gos, d_t


def gelu_skip_bwd(d_yg, ypre, uv, d_skip):
    d_ypre = d_yg * _gelu_grad(ypre)
    return d_ypre, d_ypre * d_skip, d_ypre * uv


def swiglu_fwd(gv, uv):
    gv, uv = gv.astype(F32), uv.astype(F32)
    return gv * _sigmoid(gv) * uv


def swiglu_bwd(d_act, gv, uv):
    d_act, gv, uv = d_act.astype(F32), gv.astype(F32), uv.astype(F32)
    sg = _sigmoid(gv)
    return d_act * uv * (sg * (1.0 + gv * (1.0 - sg))), d_act * gv * sg


def loss_head(ffn_out, x1, target, *, inv_d):
    diff = ffn_out + x1 - target
    return diff * inv_d, diff * inv_d, diff * diff


def residual_rms_bwd(d_res, d_h, xv, g):
    dx, c_g = _rms_bwd(d_h, xv, g)
    return d_res + dx, c_g


_ANY = pl.BlockSpec(memory_space=pl.ANY)


def _mesh_place():
    return lax.axis_index("x"), lax.axis_index("y"), lax.axis_index("c")


def _chips(x, y):
    return [(x, y), (1 - x, y), (x, 1 - y), (1 - x, 1 - y)]


def _slab(px, py, pc):
    return 4 * px + 2 * py + pc


def _all_gather(arrs, *, name):
    n = len(arrs)

    def body(*refs):
        in_refs, out_refs = refs[:n], refs[n:2 * n]
        send_sems, recv_sems, local_sems = refs[2 * n:]
        x, y, c = _mesh_place()
        me, sibling = (x, y, c), (x, y, 1 - c)
        others = _chips(x, y)[1:]

        def copy(w, k, block, to, src=None):
            dst = out_refs[w].at[_slab(*block)]
            return pltpu.make_async_remote_copy(
                src_ref=dst if src is None else src, dst_ref=dst, send_sem=send_sems.at[7 * w + k],
                recv_sem=recv_sems.at[7 * w + k], device_id=to, device_id_type=MESH)

        mine = [pltpu.make_async_copy(in_refs[w], out_refs[w].at[_slab(*me)], local_sems.at[w]) for w in range(n)]
        first = []
        for w in range(n):
            mine[w].start()
            first.append(copy(w, 0, me, sibling, src=in_refs[w]))
            first += [copy(w, 1 + j, me, (*chip, c), src=in_refs[w]) for j, chip in enumerate(others)]
        for cp in first:
            cp.start()
        passed = []
        for j, chip in enumerate(others):
            for w in range(n):
                copy(w, 1 + j, (*chip, c), me).wait_recv()
                fwd = copy(w, 4 + j, (*chip, c), sibling)
                fwd.start()
                passed.append(fwd)
        for w in range(n):
            copy(w, 0, sibling, me).wait_recv()
        for j, chip in enumerate(others):
            for w in range(n):
                copy(w, 4 + j, (*chip, 1 - c), me).wait_recv()
        for cp in first + passed:
            cp.wait_send()
        for cp in mine:
            cp.wait()

    return pl.pallas_call(
        body, name=name, in_specs=[_ANY] * n, out_specs=[_ANY] * n,
        out_shape=[jax.ShapeDtypeStruct((N_DEV,) + a.shape, a.dtype) for a in arrs],
        scratch_shapes=[pltpu.SemaphoreType.DMA((7 * n,)), pltpu.SemaphoreType.DMA((7 * n,)),
                        pltpu.SemaphoreType.DMA((n,))],
        compiler_params=pltpu.CompilerParams(has_side_effects=True),
    )(*arrs)


def _swap(arrs, n_out, plan, *, name):
    n = len(arrs)

    def body(*refs):
        in_refs, out_refs = refs[:n], refs[n:2 * n]
        send_sems, recv_sems = refs[2 * n:]
        copies = []
        for k in range(n_out):
            for w, (src, dst, to) in enumerate(plan(in_refs, out_refs, k)):
                copies.append(pltpu.make_async_remote_copy(
                    src_ref=src, dst_ref=dst, send_sem=send_sems.at[n_out * w + k],
                    recv_sem=recv_sems.at[n_out * w + k], device_id=to, device_id_type=MESH))
        for cp in copies:
            cp.start()
        for cp in copies:
            cp.wait_recv()
        for cp in copies:
            cp.wait_send()

    return pl.pallas_call(
        body, name=name, in_specs=[_ANY] * n, out_specs=[_ANY] * n,
        out_shape=[jax.ShapeDtypeStruct((n_out,) + a.shape[1:], a.dtype) for a in arrs],
        scratch_shapes=[pltpu.SemaphoreType.DMA((n_out * n,)), pltpu.SemaphoreType.DMA((n_out * n,))],
        compiler_params=pltpu.CompilerParams(has_side_effects=True),
    )(*arrs)


def _sibling_exchange(grads):
    def plan(in_refs, out_refs, k):
        x, y, c = _mesh_place()
        px, py = _chips(x, y)[k]
        return [(g.at[_slab(px, py, 1 - c)], o.at[k], (x, y, 1 - c)) for g, o in zip(in_refs, out_refs)]

    return _swap(grads, 4, plan, name="reduce_sibling")


def _chip_exchange(partials):
    def plan(in_refs, out_refs, k):
        x, y, c = _mesh_place()
        px, py = _chips(x, y)[k + 1]
        return [(g.at[k], o.at[k], (px, py, c)) for g, o in zip(in_refs, out_refs)]

    return _swap(partials, 3, plan, name="reduce_chips")


def _adamw(w, m, v, parts, *, name, tr=256):
    rows, cols = w.shape
    tr = _tile(rows, tr)
    n_p = len(parts)

    def body(*refs):
        w_ref, m_ref, v_ref = refs[:3]
        p_refs = refs[3:3 + n_p]
        g_ref, d_ref, nm_ref, nv_ref = refs[3 + n_p:]
        g = None
        for (_, lead), r in zip(parts, p_refs):
            for piece in ([r[...]] if lead is None else [r[i] for i in range(lead)]):
                g = piece.astype(F32) if g is None else g + piece.astype(F32)
        new_m = ADAM_B1 * m_ref[...] + (1.0 - ADAM_B1) * g
        new_v = ADAM_B2 * v_ref[...] + (1.0 - ADAM_B2) * (g * g)
        m_hat = new_m / (1.0 - ADAM_B1 ** ADAM_STEP)
        v_hat = new_v / (1.0 - ADAM_B2 ** ADAM_STEP)
        g_ref[...] = g
        d_ref[...] = -ADAM_LR * (m_hat / (jnp.sqrt(v_hat) + ADAM_EPS) + ADAM_WD * w_ref[...])
        nm_ref[...] = new_m
        nv_ref[...] = new_v

    tile = pl.BlockSpec((tr, cols), lambda i: (i, 0))
    p_specs = [tile if lead is None else pl.BlockSpec((lead, tr, cols), lambda i: (0, i, 0)) for _, lead in parts]
    return pl.pallas_call(
        body, name=name, grid=(rows // tr,), in_specs=[tile] * 3 + p_specs, out_specs=[tile] * 4,
        out_shape=[jax.ShapeDtypeStruct((rows, cols), F32)] * 4, compiler_params=_params(),
    )(w, m, v, *[a for a, _ in parts])


_PACK_TILE = SUBLANES * 128
_PACK_ROWS = 512


def _pack(arrs):
    flat = []
    for a in arrs:
        f = a.reshape(-1)
        flat.append(jnp.pad(f, (0, (-f.shape[0]) % _PACK_TILE)))
    total = sum(f.shape[0] for f in flat)
    flat.append(jnp.zeros(((-total) % (_PACK_ROWS * 128),), F32))
    return jnp.concatenate(flat).reshape(-1, 128)


def _unpack(buf, shapes):
    out, at = [], 0
    flat = buf.reshape(-1)
    for s in shapes:
        n = math.prod(s)
        out.append(flat[at:at + n].reshape(s))
        at += n + (-n) % _PACK_TILE
    return out


BIG = ("w_in", "w_glu", "w_out", "w_ffn_gate", "w_ffn_up", "w_ffn_down")
WEIGHTS = ("g_mix", "w_in", "q_gain", "k_gain", "rpb", "ssm_a_re", "ssm_a_im", "ssm_b_re", "ssm_b_im", "ssm_c_re",
           "ssm_c_im", "ssm_log_step", "ssm_d", "w_glu", "b_glu", "g_out_attn", "g_out_ssm", "w_out", "g_ffn",
           "w_ffn_gate", "w_ffn_up", "w_ffn_down")
SMALL = tuple(n for n in WEIGHTS if n not in BIG)
VECTORS = ("g_mix", "q_gain", "k_gain", "ssm_d", "b_glu", "g_out_attn", "g_out_ssm", "g_ffn")


def kernel(x, g_mix, w_in, q_gain, k_gain, rpb, ssm_a_re, ssm_a_im, ssm_b_re, ssm_b_im, ssm_c_re, ssm_c_im, ssm_log_step, ssm_d, w_glu, b_glu, g_out_attn, g_out_ssm, w_out, g_ffn, w_ffn_gate, w_ffn_up, w_ffn_down, loss_target, m_g_mix, m_w_in, m_q_gain, m_k_gain, m_rpb, m_ssm_a_re, m_ssm_a_im, m_ssm_b_re, m_ssm_b_im, m_ssm_c_re, m_ssm_c_im, m_ssm_log_step, m_ssm_d, m_w_glu, m_b_glu, m_g_out_attn, m_g_out_ssm, m_w_out, m_g_ffn, m_w_ffn_gate, m_w_ffn_up, m_w_ffn_down, v_g_mix, v_w_in, v_q_gain, v_k_gain, v_rpb, v_ssm_a_re, v_ssm_a_im, v_ssm_b_re, v_ssm_b_im, v_ssm_c_re, v_ssm_c_im, v_ssm_log_step, v_ssm_d, v_w_glu, v_b_glu, v_g_out_attn, v_g_out_ssm, v_w_out, v_g_ffn, v_w_ffn_gate, v_w_ffn_up, v_w_ffn_down):
    wts = dict(g_mix=g_mix, w_in=w_in, q_gain=q_gain, k_gain=k_gain, rpb=rpb, ssm_a_re=ssm_a_re, ssm_a_im=ssm_a_im,
               ssm_b_re=ssm_b_re, ssm_b_im=ssm_b_im, ssm_c_re=ssm_c_re, ssm_c_im=ssm_c_im, ssm_log_step=ssm_log_step,
               ssm_d=ssm_d, w_glu=w_glu, b_glu=b_glu, g_out_attn=g_out_attn, g_out_ssm=g_out_ssm, w_out=w_out,
               g_ffn=g_ffn, w_ffn_gate=w_ffn_gate, w_ffn_up=w_ffn_up, w_ffn_down=w_ffn_down)
    mom = dict(g_mix=m_g_mix, w_in=m_w_in, q_gain=m_q_gain, k_gain=m_k_gain, rpb=m_rpb, ssm_a_re=m_ssm_a_re,
               ssm_a_im=m_ssm_a_im, ssm_b_re=m_ssm_b_re, ssm_b_im=m_ssm_b_im, ssm_c_re=m_ssm_c_re,
               ssm_c_im=m_ssm_c_im, ssm_log_step=m_ssm_log_step, ssm_d=m_ssm_d, w_glu=m_w_glu, b_glu=m_b_glu,
               g_out_attn=m_g_out_attn, g_out_ssm=m_g_out_ssm, w_out=m_w_out, g_ffn=m_g_ffn,
               w_ffn_gate=m_w_ffn_gate, w_ffn_up=m_w_ffn_up, w_ffn_down=m_w_ffn_down)
    var = dict(g_mix=v_g_mix, w_in=v_w_in, q_gain=v_q_gain, k_gain=v_k_gain, rpb=v_rpb, ssm_a_re=v_ssm_a_re,
               ssm_a_im=v_ssm_a_im, ssm_b_re=v_ssm_b_re, ssm_b_im=v_ssm_b_im, ssm_c_re=v_ssm_c_re,
               ssm_c_im=v_ssm_c_im, ssm_log_step=v_ssm_log_step, ssm_d=v_ssm_d, w_glu=v_w_glu, b_glu=v_b_glu,
               g_out_attn=v_g_out_attn, g_out_ssm=v_g_out_ssm, w_out=v_w_out, g_ffn=v_g_ffn,
               w_ffn_gate=v_w_ffn_gate, w_ffn_up=v_w_ffn_up, w_ffn_down=v_w_ffn_down)
    ix, iy, ic = _mesh_place()
    me = _slab(ix, iy, ic)
    d_model = x.shape[-1]

    shard = {n: wts[n][0] for n in BIG}
    shard_b = {n: shard[n].astype(BF16) for n in BIG}
    w_in_full = _all_gather([shard_b["w_in"]], name="gather_w_in")[0]
    p = {n: (wts[n][0].reshape(1, -1) if n in VECTORS else wts[n][0]) for n in SMALL}

    loss, grad_x, g_small, g_w_in, reduced = _local_step(x[0], loss_target[0], p, w_in_full,
                                                         {n: shard_b[n] for n in BIG if n != "w_in"})
    loss = lax.psum(loss, ("x", "y", "c"))

    in_slabs = [g_w_in]
    in_sibling = _sibling_exchange(in_slabs)
    in_chips = _chip_exchange(_partial_sums(in_slabs, in_sibling, ("w_in",)))
    reduced["w_in"] = (in_slabs[0], in_sibling[0], in_chips[0])
    out = {}
    for n in BIG:
        slabs, from_sibling, from_chips = reduced[n]
        rows, cols = slabs.shape[1:]
        own = lax.dynamic_index_in_dim(slabs, me, 0, keepdims=False)
        res = _adamw(shard[n].reshape(rows, cols), mom[n][0].reshape(rows, cols), var[n][0].reshape(rows, cols),
                     [(own, None), (from_sibling, 1), (from_chips, 3)], name=f"adamw_{n}")
        out[n] = [r.reshape(wts[n].shape) for r in res]

    order = list(SMALL)
    shapes = [wts[n].shape for n in order]
    packed = _pack([g_small[n] for n in order])
    gathered = _all_gather([packed], name="gather_small_grads")[0]
    res = _adamw(_pack([wts[n] for n in order]), _pack([mom[n] for n in order]), _pack([var[n] for n in order]),
                 [(gathered, N_DEV)], name="adamw_small")
    for kind, buf in enumerate(res):
        for n, a in zip(order, _unpack(buf, shapes)):
            out.setdefault(n, [None] * 4)[kind] = a

    return (loss, grad_x[None], *[out[n][0] for n in WEIGHTS], *[out[n][1] for n in WEIGHTS],
            *[out[n][2] for n in WEIGHTS], *[out[n][3] for n in WEIGHTS])
```

```python
import functools
import math

import jax
import jax.numpy as jnp
from jax import lax
from jax.experimental import pallas as pl
from jax.experimental.pallas import tpu as pltpu

F32 = jnp.float32
BF16 = jnp.bfloat16

N_DEV = 8
GRID_W = 64
WIN_H = 8
WIN_W = 16
HEAD_DIM = 64
HEADS_PER_GROUP = 4
GROUP_LANES = HEADS_PER_GROUP * HEAD_DIM
SSM_C = 16
SSM_P = 64
GROUPS_PER_TILE = 8
U_TILE = GROUPS_PER_TILE * SSM_C
ST_TILE = GROUPS_PER_TILE * SSM_P
SUBLANES = 8
RMS_EPS = 1e-6
NEG_INF = -1e30
A_RE_MAX = -1e-4
ADAM_LR, ADAM_B1, ADAM_B2, ADAM_EPS, ADAM_WD, ADAM_STEP = 0.001, 0.9, 0.999, 1e-08, 0.01, 10
VMEM_LIMIT_V7X = 56 * 1024 * 1024
MESH = pl.DeviceIdType.MESH

_NN = (((1,), (0,)), ((), ()))
_NT = (((1,), (1,)), ((), ()))
_TN = (((0,), (0,)), ((), ()))
_DIMS = {"nn": _NN, "nt": _NT, "tn": _TN}


def _params(**kw):
    return pltpu.CompilerParams(vmem_limit_bytes=VMEM_LIMIT_V7X, **kw)


def _dot(a, b, dims=_NN):
    return lax.dot_general(a, b, dims, preferred_element_type=F32)


def _mm(a, b, *, name, grid, a_spec, b_spec, o_spec, o_shape, dims, k_axis=None, res=None, out_dtype=F32,
        exact=False, second=None):
    dn = _DIMS[dims]
    nk = 1 if k_axis is None else grid[k_axis]
    acc_shape = tuple(d for d in o_spec.block_shape if d is not None)
    n_in = 2 + (2 if second is not None else 0)

    def body(*refs):
        a_ref, b_ref = refs[:2]
        r_ref = refs[n_in] if res is not None else None
        o_ref, acc = refs[-2:]
        if exact:
            p = lax.dot_general(a_ref[...], b_ref[...], dn, precision=lax.Precision.HIGHEST,
                                preferred_element_type=F32)
        else:
            p = _dot(a_ref[...].astype(BF16), b_ref[...].astype(BF16), dn)
        if second is not None:
            p = p + _dot(refs[2][...].astype(BF16), refs[3][...].astype(BF16), dn)

        def finish(v):
            if r_ref is not None:
                v = v + r_ref[...].astype(F32)
            o_ref[...] = v.astype(out_dtype)

        if nk == 1:
            finish(p)
        else:
            k = pl.program_id(k_axis)

            @pl.when(k == 0)
            def _():
                acc[...] = p

            @pl.when(k > 0)
            def _():
                acc[...] += p

            @pl.when(k == nk - 1)
            def _():
                finish(acc[...])

    ins = [a, b] + (list(second) if second is not None else []) + ([res] if res is not None else [])
    in_specs = [a_spec, b_spec] * (n_in // 2) + ([o_spec] if res is not None else [])
    return pl.pallas_call(
        body, name=name, grid=grid, in_specs=in_specs, out_specs=o_spec,
        out_shape=jax.ShapeDtypeStruct(o_shape, out_dtype),
        scratch_shapes=[pltpu.VMEM(acc_shape if nk > 1 else (SUBLANES, 128), F32)],
        compiler_params=_params(),
    )(*ins)


def _tile(n, want):
    if n <= want:
        return n
    t = want
    while n % t:
        t //= 2
    return t


def _mm_plain(a, b, dims, *, name, res=None, out_dtype=F32, tm=512, tn=512, tk=512, exact=False):
    if dims == "nn":
        (m, k), n = a.shape, b.shape[1]
    elif dims == "nt":
        (m, k), n = a.shape, b.shape[0]
    else:
        (k, m), n = a.shape, b.shape[1]
    tm, tn = _tile(m, tm), _tile(n, tn)
    if dims == "tn":
        tk = _tile(k, tk)
        grid = (m // tm, n // tn, k // tk)
        a_spec = pl.BlockSpec((tk, tm), lambda i, j, kk: (kk, i))
        b_spec = pl.BlockSpec((tk, tn), lambda i, j, kk: (kk, j))
        o_spec = pl.BlockSpec((tm, tn), lambda i, j, kk: (i, j))
        return _mm(a, b, name=name, grid=grid, a_spec=a_spec, b_spec=b_spec, o_spec=o_spec, o_shape=(m, n),
                   dims=dims, k_axis=2, res=res, out_dtype=out_dtype)
    grid = (n // tn, m // tm)
    a_spec = pl.BlockSpec((tm, k), lambda j, i: (i, 0))
    if dims == "nn":
        b_spec = pl.BlockSpec((k, tn), lambda j, i: (0, j))
    else:
        b_spec = pl.BlockSpec((tn, k), lambda j, i: (j, 0))
    o_spec = pl.BlockSpec((tm, tn), lambda j, i: (i, j))
    return _mm(a, b, name=name, grid=grid, a_spec=a_spec, b_spec=b_spec, o_spec=o_spec, o_shape=(m, n), dims=dims,
               res=res, out_dtype=out_dtype, exact=exact)


def _rowwise(fn, tiled, bcast, outs, accs=(), *, name, tm=256):
    m = tiled[0].shape[0]
    tm = _tile(m, tm)
    n_t, n_b, n_o, n_a = len(tiled), len(bcast), len(outs), len(accs)

    def body(*refs):
        ins = [r[...] for r in refs[: n_t + n_b]]
        o_refs = refs[n_t + n_b: n_t + n_b + n_o]
        a_refs = refs[n_t + n_b + n_o:]
        res = fn(*ins)
        if not isinstance(res, (tuple, list)):
            res = (res,)
        for r, v in zip(o_refs, res[:n_o]):
            r[...] = v.astype(r.dtype)
        first = pl.program_id(0) == 0
        for r, v in zip(a_refs, res[n_o:]):
            s = jnp.sum(v, axis=0, keepdims=True)

            @pl.when(first)
            def _():
                r[...] = s

            @pl.when(jnp.logical_not(first))
            def _():
                r[...] += s

    in_specs = [pl.BlockSpec((tm, t.shape[1]), lambda i: (i, 0)) for t in tiled]
    in_specs += [pl.BlockSpec(b.shape, lambda i, nd=b.ndim: (0,) * nd) for b in bcast]
    out_specs = [pl.BlockSpec((tm, n), lambda i: (i, 0)) for n, _ in outs]
    out_specs += [pl.BlockSpec((1, n), lambda i: (0, 0)) for n in accs]
    out_shape = [jax.ShapeDtypeStruct((m, n), dt) for n, dt in outs]
    out_shape += [jax.ShapeDtypeStruct((1, n), F32) for n in accs]
    res = pl.pallas_call(body, name=name, grid=(m // tm,), in_specs=in_specs, out_specs=out_specs,
                         out_shape=out_shape, compiler_params=_params())(*tiled, *bcast)
    return res


def _rstd(x):
    return lax.rsqrt(jnp.mean(x * x, axis=-1, keepdims=True) + RMS_EPS)


def _rms_bwd(dh, x, g):
    xh = x * _rstd(x)
    dxh = dh * g
    dx = _rstd(x) * (dxh - xh * jnp.mean(dxh * xh, axis=-1, keepdims=True))
    return dx, dh * xh


def _sigmoid(x):
    return 1.0 / (1.0 + jnp.exp(-x))


_GELU_K = math.sqrt(2.0 / math.pi)
_GELU_C = 0.044715


def _gelu(x):
    return 0.5 * x * (1.0 + jnp.tanh(_GELU_K * (x + _GELU_C * x * x * x)))


def _gelu_grad(x):
    th = jnp.tanh(_GELU_K * (x + _GELU_C * x * x * x))
    return 0.5 * (1.0 + th) + 0.5 * x * (1.0 - th * th) * _GELU_K * (1.0 + 3.0 * _GELU_C * x * x)


class _Exchange:
    def __init__(self, arrays, outs, n_sems, sends, recvs=None, local=None, aliases=None):
        self.arrays, self.outs, self.n_sems = list(arrays), list(outs), n_sems
        self.sends, self.local, self.aliases = sends, local, aliases or {}
        self.recvs = recvs or (lambda i, o: [(k, dst) for k, _, dst, _ in sends(i, o)])

    def descriptors(self, in_refs, out_refs, send_sems, recv_sems, local_sems):
        me = _mesh_place()
        remote = lambda k, src, dst, to: pltpu.make_async_remote_copy(
            src_ref=src, dst_ref=dst, send_sem=send_sems.at[k], recv_sem=recv_sems.at[k], device_id=to,
            device_id_type=MESH)
        out = [remote(*s) for s in self.sends(in_refs, out_refs)]
        arrive = [remote(k, dst, dst, me) for k, dst in self.recvs(in_refs, out_refs)]
        own = [pltpu.make_async_copy(src, dst, local_sems.at[i])
               for i, (src, dst) in enumerate(self.local(in_refs, out_refs) if self.local else [])]
        return out, arrive, own

    def start(self, *refs):
        out, _, own = self.descriptors(*refs)
        for cp in own + out:
            cp.start()

    def finish(self, *refs):
        out, arrive, own = self.descriptors(*refs)
        for cp in arrive:
            cp.wait_recv()
        for cp in out:
            cp.wait_send()
        for cp in own:
            cp.wait()


def _call(body, *, name, grid, in_specs, out_specs, out_shape, scratch_shapes, args, ride=None, first=None, last=None):
    if ride is None:
        res = pl.pallas_call(body, name=name, grid=grid, in_specs=in_specs, out_specs=out_specs, out_shape=out_shape,
                             scratch_shapes=scratch_shapes, compiler_params=_params())(*args)
        return list(res), []
    n_in, n_out, n_scr = len(in_specs), len(out_specs), len(scratch_shapes)
    r_in, r_out = len(ride.arrays), len(ride.outs)

    def wrapped(*refs):
        ins, refs = refs[:n_in], refs[n_in:]
        x_in, refs = refs[:r_in], refs[r_in:]
        outs, refs = refs[:n_out], refs[n_out:]
        x_out, refs = refs[:r_out], refs[r_out:]
        scr, sems = refs[:n_scr], refs[n_scr:]

        @pl.when(first())
        def _():
            ride.start(x_in, x_out, *sems)

        body(*ins, *outs, *scr)

        @pl.when(last())
        def _():
            ride.finish(x_in, x_out, *sems)

    n_local = max(1, len(ride.arrays))
    res = pl.pallas_call(
        wrapped, name=name, grid=grid, in_specs=list(in_specs) + [_ANY] * r_in,
        out_specs=list(out_specs) + [_ANY] * r_out, out_shape=list(out_shape) + ride.outs,
        scratch_shapes=list(scratch_shapes) + [pltpu.SemaphoreType.DMA((ride.n_sems,)),
                                               pltpu.SemaphoreType.DMA((ride.n_sems,)),
                                               pltpu.SemaphoreType.DMA((n_local,))],
        input_output_aliases={n_in + i: n_out + o for i, o in ride.aliases.items()},
        compiler_params=_params(has_side_effects=True),
    )(*args, *ride.arrays)
    return list(res[:n_out]), list(res[n_out:])


def _gather_first(shards):
    def sends(i, o):
        x, y, c = _mesh_place()
        peers = [(x, y, 1 - c)] + [(px, py, c) for px, py in _chips(x, y)[1:]]
        return [(4 * w + k, i[w], o[w].at[_slab(x, y, c)], to) for w in range(len(i)) for k, to in enumerate(peers)]

    def recvs(i, o):
        x, y, c = _mesh_place()
        peers = [(x, y, 1 - c)] + [(px, py, c) for px, py in _chips(x, y)[1:]]
        return [(4 * w + k, o[w].at[_slab(*peer)]) for w in range(len(i)) for k, peer in enumerate(peers)]

    def local(i, o):
        return [(i[w], o[w].at[_slab(*_mesh_place())]) for w in range(len(i))]

    outs = [jax.ShapeDtypeStruct((N_DEV,) + a.shape, a.dtype) for a in shards]
    return _Exchange(shards, outs, 4 * len(shards), sends, recvs, local)


def _gather_second(gathered):
    def sends(i, o):
        x, y, c = _mesh_place()
        return [(3 * w + j, o[w].at[_slab(px, py, c)], o[w].at[_slab(px, py, c)], (x, y, 1 - c))
                for w in range(len(o)) for j, (px, py) in enumerate(_chips(x, y)[1:])]

    def recvs(i, o):
        x, y, c = _mesh_place()
        return [(3 * w + j, o[w].at[_slab(px, py, 1 - c)])
                for w in range(len(o)) for j, (px, py) in enumerate(_chips(x, y)[1:])]

    outs = [jax.ShapeDtypeStruct(a.shape, a.dtype) for a in gathered]
    return _Exchange(gathered, outs, 3 * len(gathered), sends, recvs, aliases={w: w for w in range(len(gathered))})


def _reduce_sibling(slabs):
    def sends(i, o):
        x, y, c = _mesh_place()
        return [(4 * w + k, i[w].at[_slab(px, py, 1 - c)], o[w].at[k], (x, y, 1 - c))
                for w in range(len(i)) for k, (px, py) in enumerate(_chips(x, y))]

    outs = [jax.ShapeDtypeStruct((4,) + a.shape[1:], a.dtype) for a in slabs]
    return _Exchange(slabs, outs, 4 * len(slabs), sends)


def _reduce_chips(partials):
    def sends(i, o):
        x, y, c = _mesh_place()
        return [(3 * w + k, i[w].at[k], o[w].at[k], (px, py, c))
                for w in range(len(i)) for k, (px, py) in enumerate(_chips(x, y)[1:])]

    outs = [jax.ShapeDtypeStruct(a.shape, a.dtype) for a in partials]
    return _Exchange(partials, outs, 3 * len(partials), sends)


def _head_masks():
    lane_head = lax.broadcasted_iota(jnp.int32, (1, GROUP_LANES), 1) // HEAD_DIM
    return [(lane_head == h).astype(F32) for h in range(HEADS_PER_GROUP)]


def _head_block_diag():
    r = lax.broadcasted_iota(jnp.int32, (GROUP_LANES, GROUP_LANES), 0) // HEAD_DIM
    c = lax.broadcasted_iota(jnp.int32, (GROUP_LANES, GROUP_LANES), 1) // HEAD_DIM
    return (r == c).astype(BF16)


def _head_mean(x, bd):
    hi = x.astype(BF16)
    lo = (x - hi.astype(F32)).astype(BF16)
    return (_dot(hi, bd) + _dot(lo, bd)) * (1.0 / HEAD_DIM)


def _stack_heads(x, masks):
    return jnp.concatenate([x * m for m in masks], axis=0)


def _unstack_heads(xs, masks):
    out = xs[0:GRID_W] * masks[0]
    for h in range(1, HEADS_PER_GROUP):
        out = out + xs[h * GRID_W:(h + 1) * GRID_W] * masks[h]
    return out


def _row_start(r, rows):
    return jnp.clip(r - WIN_H // 2, 0, rows - WIN_H)


def _attn_common_specs(seq, n_hg, rows):
    win_keys = WIN_H * GRID_W
    q_spec = pl.BlockSpec((GRID_W, GROUP_LANES), lambda g, r: (r, g))
    k_spec = pl.BlockSpec((seq, GROUP_LANES), lambda g, r: (0, n_hg + g))
    v_spec = pl.BlockSpec((seq, GROUP_LANES), lambda g, r: (0, 2 * n_hg + g))
    gain_spec = pl.BlockSpec((1, GROUP_LANES), lambda g, r: (0, 0))

    def variant(r):
        return _row_start(r, rows) - r + (WIN_H - 1)

    bias_spec = pl.BlockSpec((None, None, HEADS_PER_GROUP, GRID_W, win_keys), lambda g, r: (g, variant(r), 0, 0, 0))
    return q_spec, k_spec, v_spec, gain_spec, bias_spec, variant


def _attn_prepare_kv(k_ref, v_ref, kg, kn_scr, vb_scr, bd, seq):
    chunk = _tile(seq, 512)

    def step(c, carry):
        rows = pl.ds(pl.multiple_of(c * chunk, chunk), chunk)
        k = k_ref[rows, :]
        kn_scr[rows, :] = (k * lax.rsqrt(_head_mean(k * k, bd) + RMS_EPS) * kg).astype(BF16)
        vb_scr[rows, :] = v_ref[rows, :].astype(BF16)
        return carry

    lax.fori_loop(0, seq // chunk, step, 0)


def _attn_probs(qn, kw, bias, masks):
    qs = _stack_heads(qn, masks).astype(BF16)
    s = _dot(qs, kw, _NT) * (1.0 / math.sqrt(HEAD_DIM)) + bias
    m = jnp.max(s, axis=-1, keepdims=True)
    p = jnp.exp(s - m)
    return qs, p / jnp.sum(p, axis=-1, keepdims=True)


def _grid_ends(grid):
    first = lambda: functools.reduce(jnp.logical_and, [pl.program_id(a) == 0 for a in range(len(grid))])
    last = lambda: functools.reduce(jnp.logical_and, [pl.program_id(a) == n - 1 for a, n in enumerate(grid)])
    return first, last


def _attn_fwd(z, qg4, kg4, btab, ride=None):
    seq = z.shape[0]
    a_width = btab.shape[0] * GROUP_LANES
    n_hg, rows, win_keys = btab.shape[0], seq // GRID_W, WIN_H * GRID_W
    q_spec, k_spec, v_spec, gain_spec, bias_spec, _ = _attn_common_specs(seq, n_hg, rows)

    def body(q_ref, k_ref, v_ref, qg_ref, kg_ref, b_ref, o_ref, kn_scr, vb_scr):
        r = pl.program_id(1)
        bd, masks = _head_block_diag(), _head_masks()

        @pl.when(r == 0)
        def _():
            _attn_prepare_kv(k_ref, v_ref, kg_ref[...], kn_scr, vb_scr, bd, seq)

        win = pl.ds(pl.multiple_of(_row_start(r, rows) * GRID_W, GRID_W), win_keys)
        q = q_ref[...]
        qn = q * lax.rsqrt(_head_mean(q * q, bd) + RMS_EPS) * qg_ref[...]
        bias = b_ref[...].reshape(HEADS_PER_GROUP * GRID_W, win_keys)
        _, p = _attn_probs(qn, kn_scr[win, :], bias, masks)
        o_ref[...] = _unstack_heads(_dot(p.astype(BF16), vb_scr[win, :]), masks)

    first, last = _grid_ends((n_hg, rows))
    (ya,), rode = _call(
        body, name="attn_fwd", grid=(n_hg, rows),
        in_specs=[q_spec, k_spec, v_spec, gain_spec, gain_spec, bias_spec],
        out_specs=[pl.BlockSpec((GRID_W, GROUP_LANES), lambda g, r: (r, g))],
        out_shape=[jax.ShapeDtypeStruct((seq, a_width), F32)],
        scratch_shapes=[pltpu.VMEM((seq, GROUP_LANES), BF16), pltpu.VMEM((seq, GROUP_LANES), BF16)],
        args=(z, z, z, qg4, kg4, btab), ride=ride, first=first, last=last)
    return ya, rode


def _attn_bwd(z, d_out, qg4, kg4, btab, ride=None):
    seq = z.shape[0]
    n_hg, rows, win_keys = btab.shape[0], seq // GRID_W, WIN_H * GRID_W
    a_width = n_hg * GROUP_LANES
    q_spec, k_spec, v_spec, gain_spec, bias_spec, variant = _attn_common_specs(seq, n_hg, rows)
    scale = 1.0 / math.sqrt(HEAD_DIM)

    def body(q_ref, k_ref, v_ref, do_ref, qg_ref, kg_ref, b_ref,
             dq_ref, dk_out, dv_out, db_ref, dqg_ref, dkg_ref, kn_scr, vb_scr, dk_ref, dv_ref):
        r = pl.program_id(1)
        bd, masks = _head_block_diag(), _head_masks()

        @pl.when(r == 0)
        def _():
            _attn_prepare_kv(k_ref, v_ref, kg_ref[...], kn_scr, vb_scr, bd, seq)
            dk_ref[...] = jnp.zeros_like(dk_ref)
            dv_ref[...] = jnp.zeros_like(dv_ref)
            db_ref[...] = jnp.zeros_like(db_ref)
            dqg_ref[...] = jnp.zeros_like(dqg_ref)

        win = pl.ds(pl.multiple_of(_row_start(r, rows) * GRID_W, GRID_W), win_keys)
        q, qg = q_ref[...], qg_ref[...]
        rq = lax.rsqrt(_head_mean(q * q, bd) + RMS_EPS)
        qh = q * rq
        kw, vw = kn_scr[win, :], vb_scr[win, :]
        bias = b_ref[...].reshape(HEADS_PER_GROUP * GRID_W, win_keys)
        qs, p = _attn_probs(qh * qg, kw, bias, masks)
        dos = _stack_heads(do_ref[...], masks).astype(BF16)
        dp = _dot(dos, vw, _NT)
        ds = p * (dp - jnp.sum(p * dp, axis=-1, keepdims=True))
        db_ref[variant(r)] += ds.reshape(HEADS_PER_GROUP, GRID_W, win_keys)
        dsb = ds.astype(BF16)
        dqn = _unstack_heads(_dot(dsb, kw), masks) * scale
        dk_ref[win, :] += _dot(dsb, qs, _TN) * scale
        dv_ref[win, :] += _dot(p.astype(BF16), dos, _TN)
        dqg_ref[...] += jnp.sum(dqn * qh, axis=0, keepdims=True)
        dqh = dqn * qg
        dq_ref[...] = (rq * (dqh - qh * _head_mean(dqh * qh, bd))).astype(BF16)

        @pl.when(r == rows - 1)
        def _():
            chunk = _tile(seq, 512)
            kg = kg_ref[...]

            def step(c, dkg):
                rws = pl.ds(pl.multiple_of(c * chunk, chunk), chunk)
                k = k_ref[rws, :]
                rk = lax.rsqrt(_head_mean(k * k, bd) + RMS_EPS)
                kh = k * rk
                dkn = dk_ref[rws, :]
                dkh = dkn * kg
                dk_out[rws, :] = (rk * (dkh - kh * _head_mean(dkh * kh, bd))).astype(BF16)
                dv_out[rws, :] = dv_ref[rws, :].astype(BF16)
                return dkg + jnp.sum(dkn * kh, axis=0, keepdims=True)

            dkg_ref[...] = lax.fori_loop(0, seq // chunk, step, jnp.zeros((1, GROUP_LANES), F32))

    col_spec = pl.BlockSpec((seq, GROUP_LANES), lambda g, r: (0, g))
    gsum_spec = pl.BlockSpec((None, 1, GROUP_LANES), lambda g, r: (g, 0, 0))
    first, last = _grid_ends((n_hg, rows))
    return _call(
        body, name="attn_bwd", grid=(n_hg, rows),
        in_specs=[q_spec, k_spec, v_spec, pl.BlockSpec((GRID_W, GROUP_LANES), lambda g, r: (r, g)),
                  gain_spec, gain_spec, bias_spec],
        out_specs=[pl.BlockSpec((GRID_W, GROUP_LANES), lambda g, r: (r, g)), col_spec, col_spec,
                   pl.BlockSpec((None, WIN_H, HEADS_PER_GROUP, GRID_W, win_keys), lambda g, r: (g, 0, 0, 0, 0)),
                   gsum_spec, gsum_spec],
        out_shape=[jax.ShapeDtypeStruct((seq, a_width), BF16)] * 3
        + [jax.ShapeDtypeStruct(btab.shape, F32)]
        + [jax.ShapeDtypeStruct((n_hg, 1, GROUP_LANES), F32)] * 2,
        scratch_shapes=[pltpu.VMEM((seq, GROUP_LANES), BF16), pltpu.VMEM((seq, GROUP_LANES), BF16),
                        pltpu.VMEM((seq, GROUP_LANES), F32), pltpu.VMEM((seq, GROUP_LANES), F32)],
        args=(z, z, z, d_out, qg4, kg4, btab), ride=ride, first=first, last=last)


def _bias_index():
    c = jnp.arange(GRID_W)
    col_start = jnp.clip(c - WIN_W // 2, 0, GRID_W - WIN_W)
    col_in = (c[None, :] >= col_start[:, None]) & (c[None, :] < col_start[:, None] + WIN_W)
    dc = jnp.clip(c[None, :] - c[:, None], -(WIN_W - 1), WIN_W - 1) + (WIN_W - 1)
    dr = jnp.arange(WIN_H)[:, None] + jnp.arange(WIN_H)[None, :]
    return col_in, dc, dr


def _bias_table(rpb):
    col_in, dc, _ = _bias_index()
    n_h = rpb.shape[0]
    n_hg = n_h // HEADS_PER_GROUP
    spread = ((jnp.arange(128)[:, None] == dc.reshape(1, -1)) & col_in.reshape(1, -1)).astype(F32)
    rows = jnp.stack([rpb[:, v:v + WIN_H] for v in range(WIN_H)], axis=1)
    rows = jnp.pad(rows, ((0, 0), (0, 0), (0, 0), (0, 128 - rows.shape[-1]))).reshape(n_h * WIN_H * WIN_H, 128)
    tab = _mm_plain(rows, spread, "nn", name="rpb_spread", tm=256, tn=2048, exact=True)
    tab = jnp.where(col_in.reshape(1, -1), tab, NEG_INF)
    tab = tab.reshape(n_hg, HEADS_PER_GROUP, WIN_H, WIN_H, GRID_W, GRID_W).transpose(0, 2, 1, 4, 3, 5)
    return tab.reshape(n_hg, WIN_H, HEADS_PER_GROUP, GRID_W, WIN_H * GRID_W)


def _bias_grad(dtab, n_h):
    col_in, dc, _ = _bias_index()
    onehot = (dc.reshape(-1, 1) == jnp.arange(128)[None, :]) & col_in.reshape(-1, 1)
    n_hg = n_h // HEADS_PER_GROUP
    d = dtab.reshape(n_hg, WIN_H, HEADS_PER_GROUP, GRID_W, WIN_H, GRID_W).transpose(0, 2, 1, 4, 3, 5)
    d = d.reshape(n_h * WIN_H * WIN_H, GRID_W * GRID_W)
    diag = _mm_plain(d, onehot.astype(BF16), "nn", name="rpb_diag_sum", tm=256, tn=128)
    diag = diag.reshape(n_h, WIN_H, WIN_H, 128)[..., : 2 * WIN_W - 1]
    out = jnp.zeros((n_h, 2 * WIN_H - 1, 2 * WIN_W - 1), F32)
    for v in range(WIN_H):
        out = out.at[:, v:v + WIN_H].add(diag[:, v])
    return out


def _cmul(ar, ai, br, bi):
    return ar * br - ai * bi, ar * bi + ai * br


def _s5_discretize(a_re, a_im, dt, b_re, b_im):
    c = b_re.shape[1]

    def fn(are, aim, dt_, bre, bim):
        lr, li = jnp.minimum(are, A_RE_MAX), aim
        mag = jnp.exp(lr * dt_)
        l1r, l1i = mag * jnp.cos(li * dt_), mag * jnp.sin(li * dt_)
        den = lr * lr + li * li
        nr, ni = l1r - 1.0, l1i
        cr, ci = (nr * lr + ni * li) / den, (ni * lr - nr * li) / den
        bbr, bbi = _cmul(cr, ci, bre, bim)
        shape = (are.shape[0], SUBLANES)
        lane = lax.broadcasted_iota(jnp.int32, shape, 1)
        pr, pi = l1r, l1i
        acc_r, acc_i = jnp.zeros(shape, F32), jnp.zeros(shape, F32)
        for k in range(SUBLANES):
            acc_r = jnp.where(lane == k, pr, acc_r)
            acc_i = jnp.where(lane == k, pi, acc_i)
            pr, pi = _cmul(pr, pi, l1r, l1i)
        return acc_r, acc_i, cr, ci, bbr, bbi

    return _rowwise(fn, [a_re, a_im, dt, b_re, b_im], [],
                    [(SUBLANES, F32), (SUBLANES, F32), (1, F32), (1, F32), (c, F32), (c, F32)],
                    name="s5_discretize", tm=1024)


def _s5_param_grads(a_re, a_im, dt, b_re, b_im, l1r, l1i, cr, ci, bbr, bbi, r_re, r_im, gb_re, gb_im):
    c = b_re.shape[1]

    def fn(are, aim, dt_, bre, bim, l1r_, l1i_, cr_, ci_, bbr_, bbi_, rr, ri, gbr, gbi):
        lr, li = jnp.minimum(are, A_RE_MAX), aim
        den = lr * lr + li * li
        dbr, dbi = _cmul(cr_, -ci_, gbr, gbi)
        gcr, gci = _cmul(bre, -bim, gbr, gbi)
        gcr, gci = jnp.sum(gcr, axis=1, keepdims=True), jnp.sum(gci, axis=1, keepdims=True)
        qr, qi = _cmul(bbr_, -bbi_, gbr, gbi)
        qr = rr - jnp.sum(qr, axis=1, keepdims=True)
        qi = ri - jnp.sum(qi, axis=1, keepdims=True)
        tr, ti = _cmul(gcr, gci, lr / den, li / den)
        ur, ui = _cmul(l1r_, -l1i_, tr, ti)
        gwr, gwi = qr + ur, qi + ui
        vr, vi = _cmul(cr_, -ci_, lr / den, li / den)
        vr, vi = _cmul(gcr, gci, vr, vi)
        glr, gli = dt_ * gwr - vr, dt_ * gwi - vi
        return jnp.where(are < A_RE_MAX, glr, 0.0), gli, (gwr * lr + gwi * li) * dt_, dbr, dbi

    return _rowwise(fn, [a_re, a_im, dt, b_re, b_im, l1r, l1i, cr, ci, bbr, bbi, r_re, r_im, gb_re, gb_im], [],
                    [(1, F32), (1, F32), (1, F32), (c, F32), (c, F32)], name="s5_param_grads", tm=1024)


def _s5_scan(v, win_re, win_im, tabs, wo_re, wo_im, *, reverse, name, t_chunk=256, ride=None):
    seq, width = v.shape
    n_tiles, n_state = width // U_TILE, width * (SSM_P // SSM_C)
    t_chunk = _tile(seq, t_chunk)
    n_chunks, n_blk = seq // t_chunk, t_chunk // SUBLANES
    last_row = 0 if reverse else SUBLANES - 1

    def chunk_of(j):
        return (n_chunks - 1 - j) if reverse else j

    def body(v_ref, wir_ref, wii_ref, tab_ref, wor_ref, woi_ref, sr_ref, si_ref, y_ref, carry, wr, wi):
        @pl.when(pl.program_id(0) == 0)
        def _():
            carry[...] = jnp.zeros_like(carry)

        for jt in range(n_tiles):
            ls = slice(jt * ST_TILE, (jt + 1) * ST_TILE)
            us = slice(jt * U_TILE, (jt + 1) * U_TILE)
            vj = v_ref[:, us].astype(BF16)
            wr[...] = _dot(vj, wir_ref[jt])
            wi[...] = _dot(vj, wii_ref[jt])
            consts = [tab_ref[k, :, ls] for k in range(8)]

            def blk(b, c, consts=consts):
                cr, ci = c
                bb = (n_blk - 1 - b) if reverse else b
                rows = pl.ds(pl.multiple_of(bb * SUBLANES, SUBLANES), SUBLANES)
                xr, xi = wr[rows, :], wi[rows, :]
                for s, k in enumerate((1, 2, 4)):
                    sh = (SUBLANES - k) if reverse else k
                    tr, ti = pltpu.roll(xr, sh, 0), pltpu.roll(xi, sh, 0)
                    lr, li = consts[2 * s], consts[2 * s + 1]
                    xr, xi = xr + lr * tr - li * ti, xi + lr * ti + li * tr
                lr, li = consts[6], consts[7]
                xr, xi = xr + lr * cr - li * ci, xi + lr * ci + li * cr
                wr[rows, :], wi[rows, :] = xr, xi
                shape = (SUBLANES, ST_TILE)
                return (jnp.broadcast_to(xr[last_row:last_row + 1], shape),
                        jnp.broadcast_to(xi[last_row:last_row + 1], shape))

            cr, ci = lax.fori_loop(0, n_blk, blk, (carry[0, :, ls], carry[1, :, ls]), unroll=4)
            carry[0, :, ls], carry[1, :, ls] = cr, ci
            xr_b, xi_b = wr[...].astype(BF16), wi[...].astype(BF16)
            sr_ref[:, ls], si_ref[:, ls] = xr_b, xi_b
            y_ref[:, us] = _dot(xr_b, wor_ref[jt]) + _dot(xi_b, woi_ref[jt])

    whole = lambda a: pl.BlockSpec(a.shape, lambda j, nd=a.ndim: (0,) * nd)
    st_spec = pl.BlockSpec((t_chunk, n_state), lambda j: (chunk_of(j), 0))
    v_spec = pl.BlockSpec((t_chunk, width), lambda j: (chunk_of(j), 0))
    first, last = _grid_ends((n_chunks,))
    return _call(
        body, name=name, grid=(n_chunks,),
        in_specs=[v_spec, whole(win_re), whole(win_im), whole(tabs), whole(wo_re), whole(wo_im)],
        out_specs=[st_spec, st_spec, v_spec],
        out_shape=[jax.ShapeDtypeStruct((seq, n_state), BF16)] * 2 + [jax.ShapeDtypeStruct((seq, width), F32)],
        scratch_shapes=[pltpu.VMEM((2, SUBLANES, n_state), F32), pltpu.VMEM((t_chunk, ST_TILE), F32),
                        pltpu.VMEM((t_chunk, ST_TILE), F32)],
        args=(v, win_re, win_im, tabs, wo_re, wo_im), ride=ride, first=first, last=last)


def _s5_reduce(x_re, x_im, a_re, a_im, u, dy, *, name, t_chunk=512, ride=None):
    seq, n_state = x_re.shape
    width = u.shape[1]
    n_tiles = width // U_TILE
    t_chunk = _tile(seq, t_chunk)

    def body(xr_ref, xi_ref, ar_ref, ai_ref, u_ref, dy_ref, rr_ref, ri_ref, gbr_ref, gbi_ref, gcr_ref, gci_ref):
        xrb, xib, arb, aib = xr_ref[...], xi_ref[...], ar_ref[...], ai_ref[...]
        xr, xi, ar, ai = xrb.astype(F32), xib.astype(F32), arb.astype(F32), aib.astype(F32)
        ub, dyb = u_ref[...].astype(BF16), dy_ref[...].astype(BF16)
        parts = (jnp.sum(ar * xr + ai * xi, axis=0, keepdims=True), jnp.sum(ai * xr - ar * xi, axis=0, keepdims=True),
                 _dot(arb, ub, _TN), _dot(aib, ub, _TN), _dot(xrb, dyb, _TN), _dot(xib, dyb, _TN))
        first = pl.program_id(1) == 0
        for ref, val in zip((rr_ref, ri_ref, gbr_ref, gbi_ref, gcr_ref, gci_ref), parts):
            @pl.when(first)
            def _():
                ref[...] = val

            @pl.when(jnp.logical_not(first))
            def _():
                ref[...] += val

    st_spec = pl.BlockSpec((t_chunk, ST_TILE), lambda j, t: (t, j))
    u_spec = pl.BlockSpec((t_chunk, U_TILE), lambda j, t: (t, j))
    r_spec = pl.BlockSpec((1, ST_TILE), lambda j, t: (0, j))
    g_spec = pl.BlockSpec((None, ST_TILE, U_TILE), lambda j, t: (j, 0, 0))
    first, last = _grid_ends((n_tiles, seq // t_chunk))
    return _call(
        body, name=name, grid=(n_tiles, seq // t_chunk),
        in_specs=[st_spec] * 4 + [u_spec] * 2,
        out_specs=[r_spec, r_spec] + [g_spec] * 4,
        out_shape=[jax.ShapeDtypeStruct((1, n_state), F32)] * 2
        + [jax.ShapeDtypeStruct((n_tiles, ST_TILE, U_TILE), F32)] * 4,
        scratch_shapes=[], args=(x_re, x_im, a_re, a_im, u, dy), ride=ride, first=first, last=last)


def _block_diag_in(ms):
    m = jnp.stack(ms)
    n, g, c, p = m.shape
    m5 = m.reshape(n, g // GROUPS_PER_TILE, GROUPS_PER_TILE, c, p)
    eye = jnp.eye(GROUPS_PER_TILE, dtype=m.dtype)
    out = m5[:, :, :, :, None, :] * eye[None, None, :, None, :, None]
    return out.astype(BF16).reshape(n, g // GROUPS_PER_TILE, GROUPS_PER_TILE * c, GROUPS_PER_TILE * p)


def _block_diag_take(m, c, p):
    t = m.shape[0]
    m5 = m.reshape(t, GROUPS_PER_TILE, p, GROUPS_PER_TILE, c)
    idx = jnp.arange(GROUPS_PER_TILE)
    return m5[:, idx, :, idx, :].transpose(1, 0, 2, 3).reshape(t * GROUPS_PER_TILE, p, c)


def _scan_tables(pw_re, pw_im, reverse):
    row = jnp.arange(SUBLANES)[:, None]
    tabs = []
    for k in (1, 2, 4):
        keep = (row <= SUBLANES - 1 - k) if reverse else (row >= k)
        tabs += [jnp.where(keep, pw_re[k - 1][None, :], 0.0), jnp.where(keep, pw_im[k - 1][None, :], 0.0)]
    order = jnp.arange(SUBLANES)[::-1] if reverse else jnp.arange(SUBLANES)
    tabs += [pw_re[order], pw_im[order]]
    return jnp.stack(tabs)


def _partial_sums(slabs, from_sibling, names):
    x, y, c = _mesh_place()
    theirs = jnp.stack([_slab(px, py, c) for px, py in _chips(x, y)[1:]]).astype(jnp.int32)
    out = []
    for s, f, n in zip(slabs, from_sibling, names):
        rows, cols = s.shape[1:]
        tr = _tile(rows, 512)

        def body(idx_ref, a_ref, b_ref, o_ref):
            o_ref[...] = (a_ref[...] + b_ref[...]).astype(BF16)

        out.append(pl.pallas_call(
            body, name=f"reduce_add_{n}",
            grid_spec=pltpu.PrefetchScalarGridSpec(
                num_scalar_prefetch=1, grid=(3, rows // tr),
                in_specs=[pl.BlockSpec((None, tr, cols), lambda k, i, idx: (idx[k], i, 0)),
                          pl.BlockSpec((None, tr, cols), lambda k, i, idx: (k + 1, i, 0))],
                out_specs=pl.BlockSpec((None, tr, cols), lambda k, i, idx: (k, i, 0))),
            out_shape=jax.ShapeDtypeStruct((3, rows, cols), BF16), compiler_params=_params(),
        )(theirs, s, f))
    return out


def _local_step(x, target, p, w_in, shards):
    seq, d_model = x.shape
    a_width = p["g_out_attn"].shape[-1]
    s_width = p["g_out_ssm"].shape[-1]
    n_heads = a_width // HEAD_DIM
    n_hg = n_heads // HEADS_PER_GROUP
    n_groups = s_width // SSM_C
    n_sh, _, in_sh = w_in.shape
    f_sh = shards["w_ffn_gate"].shape[-1]
    w = {"w_in": w_in}
    slab3 = lambda g, n: g.reshape(N_DEV, -1, shards[n].shape[-1])
    t2, t1 = _tile(seq, 2048), _tile(seq, 1024)
    n2, n1 = seq // t2, seq // t1

    h1 = _rowwise(lambda xv, g: x_norm(xv, g), [x], [p["g_mix"]], [(d_model, BF16)], name="rms_mix")[0]
    z = _mm(h1, w["w_in"], name="in_proj", grid=(n2, n_sh),
            a_spec=pl.BlockSpec((t2, d_model), lambda i, j: (i, 0)),
            b_spec=pl.BlockSpec((None, d_model, in_sh), lambda i, j: (j, 0, 0)),
            o_spec=pl.BlockSpec((t2, in_sh), lambda i, j: (i, j)), o_shape=(seq, n_sh * in_sh), dims="nn")
    qg4 = jnp.tile(p["q_gain"], (1, HEADS_PER_GROUP))
    kg4 = jnp.tile(p["k_gain"], (1, HEADS_PER_GROUP))
    btab = _bias_table(p["rpb"])
    ya, got_a = _attn_fwd(z, qg4, kg4, btab, ride=_gather_first([shards["w_ffn_gate"], shards["w_ffn_up"]]))
    u = z[:, 3 * a_width:]

    n_col = 2 * n_groups * SSM_P
    col = lambda a: a.reshape(n_col, 1)
    a_re_c, a_im_c = col(p["ssm_a_re"]), col(p["ssm_a_im"])
    dt_c = col(jnp.broadcast_to(jnp.exp(p["ssm_log_step"])[:, :, None], (2, n_groups, SSM_P)))
    b_re_c, b_im_c = p["ssm_b_re"].reshape(n_col, SSM_C), p["ssm_b_im"].reshape(n_col, SSM_C)
    pw_re, pw_im, cf_re, cf_im, bb_re, bb_im = _s5_discretize(a_re_c, a_im_c, dt_c, b_re_c, b_im_c)
    n_state = n_groups * SSM_P
    pw_re = pw_re.reshape(2, n_state, SUBLANES).transpose(0, 2, 1)
    pw_im = pw_im.reshape(2, n_state, SUBLANES).transpose(0, 2, 1)
    bb_re4, bb_im4 = bb_re.reshape(2, n_groups, SSM_P, SSM_C), bb_im.reshape(2, n_groups, SSM_P, SSM_C)
    c_re, c_im = p["ssm_c_re"], p["ssm_c_im"]
    t21 = lambda a: a.transpose(0, 2, 1)
    maps_in = _block_diag_in([m for d in range(2) for m in (t21(bb_re4[d]), t21(bb_im4[d]), c_re[d], -c_im[d])])
    maps_out = _block_diag_in([m for d in range(2) for m in (t21(c_re[d]), -t21(c_im[d]), bb_re4[d], bb_im4[d])])
    fwd, bwd_in = [], []
    got_b = None
    for d in range(2):
        rev = d == 1
        tabs = _scan_tables(pw_re[d], pw_im[d], rev)
        if d == 0:
            ride = _gather_first([shards["w_glu"], shards["w_out"], shards["w_ffn_down"]])
        else:
            ride = _gather_second(got_a + got_b)
        (xs_re, xs_im, y_d), got = _s5_scan(u, maps_in[4 * d], maps_in[4 * d + 1], tabs, maps_out[4 * d],
                                            maps_out[4 * d + 1], reverse=rev, name=f"s5_fwd_{d}", ride=ride)
        if d == 0:
            got_b = got
        fwd.append((xs_re, xs_im, y_d))
        bwd_in.append((maps_in[4 * d + 2], maps_in[4 * d + 3], _scan_tables(pw_re[d], -pw_im[d], not rev),
                       maps_out[4 * d + 2], maps_out[4 * d + 3]))
    w["w_gate"], w["w_up"], w_glu_full, w_out_full, w["w_down"] = got
    w["w_glu"] = w_glu_full.reshape(-1, s_width)
    w["w_out"] = w_out_full.reshape(-1, d_model)

    ypre, yg = _rowwise(lambda y0, y1, uv, dsk: s5_mid(y0, y1, uv, dsk), [fwd[0][2], fwd[1][2], u], [p["ssm_d"]],
                        [(s_width, F32), (s_width, F32)], name="s5_skip_gelu")
    t_glu = _mm_plain(yg, w["w_glu"], "nn", name="glu_proj", tn=s_width)
    y_cat = _rowwise(mix_out_fwd, [ya, yg, t_glu], [p["b_glu"], p["g_out_attn"], p["g_out_ssm"]],
                     [(a_width + s_width, BF16)], name="mix_out")[0]
    x1 = _mm_plain(y_cat, w["w_out"], "nn", name="out_proj", res=x, tn=2048)

    h2 = _rowwise(lambda xv, g: x_norm(xv, g), [x1], [p["g_ffn"]], [(d_model, BF16)], name="rms_ffn")[0]
    ffn_up = functools.partial(
        _mm, grid=(n2, n_sh), a_spec=pl.BlockSpec((t2, d_model), lambda i, j: (i, 0)),
        b_spec=pl.BlockSpec((None, d_model, f_sh), lambda i, j: (j, 0, 0)),
        o_spec=pl.BlockSpec((None, t2, f_sh), lambda i, j: (j, i, 0)), o_shape=(n_sh, seq, f_sh), dims="nn",
        out_dtype=BF16)
    gate = ffn_up(h2, w["w_gate"], name="ffn_gate")
    up = ffn_up(h2, w["w_up"], name="ffn_up")
    flat = lambda a: a.reshape(n_sh * seq, f_sh)
    act = _rowwise(swiglu_fwd, [flat(gate), flat(up)], [], [(f_sh, BF16)], name="swiglu",
                   tm=1024)[0].reshape(n_sh, seq, f_sh)
    ffn_out = _mm(act, w["w_down"], name="ffn_down", grid=(n1, n_sh),
                  a_spec=pl.BlockSpec((None, t1, f_sh), lambda i, j: (j, i, 0)),
                  b_spec=pl.BlockSpec((None, f_sh, d_model), lambda i, j: (j, 0, 0)),
                  o_spec=pl.BlockSpec((t1, d_model), lambda i, j: (i, 0)), o_shape=(seq, d_model), dims="nn",
                  k_axis=1)

    dx2, dx2_b, sq = _rowwise(functools.partial(loss_head, inv_d=1.0 / d_model), [ffn_out, x1, target], [],
                              [(d_model, F32), (d_model, BF16)], [d_model], name="loss_head")
    loss = 0.5 * jnp.sum(sq) / d_model

    d_act = _mm(dx2_b, w["w_down"], name="ffn_down_dx", grid=(n2, n_sh),
                a_spec=pl.BlockSpec((t2, d_model), lambda i, j: (i, 0)),
                b_spec=pl.BlockSpec((None, f_sh, d_model), lambda i, j: (j, 0, 0)),
                o_spec=pl.BlockSpec((None, t2, f_sh), lambda i, j: (j, i, 0)), o_shape=(n_sh, seq, f_sh), dims="nt",
                out_dtype=BF16)
    g_w_down = _mm(act, dx2_b, name="ffn_down_dw", grid=(n_sh, n1),
                   a_spec=pl.BlockSpec((None, t1, f_sh), lambda j, k: (j, k, 0)),
                   b_spec=pl.BlockSpec((t1, d_model), lambda j, k: (k, 0)),
                   o_spec=pl.BlockSpec((None, f_sh, d_model), lambda j, k: (j, 0, 0)),
                   o_shape=(n_sh, f_sh, d_model), dims="tn", k_axis=1)
    d_gate, d_up = _rowwise(swiglu_bwd, [flat(d_act), flat(gate), flat(up)], [], [(f_sh, BF16), (f_sh, BF16)],
                            name="swiglu_bwd", tm=1024)
    d_gate, d_up = d_gate.reshape(n_sh, seq, f_sh), d_up.reshape(n_sh, seq, f_sh)
    d_h2 = _mm(d_gate, w["w_gate"], second=(d_up, w["w_up"]), name="ffn_up_gate_dx", grid=(n1, n_sh),
               a_spec=pl.BlockSpec((None, t1, f_sh), lambda i, j: (j, i, 0)),
               b_spec=pl.BlockSpec((None, d_model, f_sh), lambda i, j: (j, 0, 0)),
               o_spec=pl.BlockSpec((t1, d_model), lambda i, j: (i, 0)), o_shape=(seq, d_model), dims="nt", k_axis=1)
    ffn_dw = functools.partial(
        _mm, grid=(n_sh, n1), a_spec=pl.BlockSpec((t1, d_model), lambda j, k: (k, 0)),
        b_spec=pl.BlockSpec((None, t1, f_sh), lambda j, k: (j, k, 0)),
        o_spec=pl.BlockSpec((None, d_model, f_sh), lambda j, k: (j, 0, 0)), o_shape=(n_sh, d_model, f_sh), dims="tn",
        k_axis=1)
    g_w_gate = ffn_dw(h2, d_gate, name="ffn_gate_dw")
    g_w_up = ffn_dw(h2, d_up, name="ffn_up_dw")
    dx1, g_g_ffn = _rowwise(residual_rms_bwd, [dx2, d_h2, x1], [p["g_ffn"]], [(d_model, F32)], [d_model],
                            name="rms_ffn_bwd")

    d_ycat = _mm_plain(dx1, w["w_out"], "nt", name="out_proj_dx", tn=2048)
    g_w_out = _mm_plain(y_cat, dx1, "tn", name="out_proj_dw", tm=1024, tn=2048)
    (d_ya, d_yg_direct, d_t, g_goa, g_gos, g_b_glu) = _rowwise(
        functools.partial(mix_out_bwd, a_width=a_width), [d_ycat, ya, yg, t_glu],
        [p["b_glu"], p["g_out_attn"], p["g_out_ssm"]],
        [(a_width, F32), (s_width, F32), (s_width, BF16)], [a_width, s_width, s_width], name="mix_out_bwd")
    d_yg = _mm_plain(d_t, w["w_glu"], "nt", name="glu_proj_dx", res=d_yg_direct, tn=s_width)
    g_w_glu = _mm_plain(yg, d_t, "tn", name="glu_proj_dw", tm=1024, tn=1024)
    d_ypre, du_skip, g_ssm_d = _rowwise(gelu_skip_bwd, [d_yg, ypre, u], [p["ssm_d"]],
                                        [(s_width, F32), (s_width, F32)], [s_width], name="s5_skip_gelu_bwd")

    ffn_names, mix_names = ("w_ffn_gate", "w_ffn_up", "w_ffn_down"), ("w_glu", "w_out")
    ffn_slabs = [slab3(g, n) for g, n in zip((g_w_gate, g_w_up, g_w_down), ffn_names)]
    mix_slabs = [slab3(g, n) for g, n in zip((g_w_glu, g_w_out), mix_names)]
    du_dirs, adj, r_parts, gb_parts, gc_parts = [], [], [], [], []
    sib, part = {}, {}
    for d, (names, slabs) in enumerate(((ffn_names, ffn_slabs), (mix_names, mix_slabs))):
        win_re, win_im, tabs, wo_re, wo_im = bwd_in[d]
        (as_re, as_im, du_d), got = _s5_scan(d_ypre, win_re, win_im, tabs, wo_re, wo_im, reverse=(d == 0),
                                             name=f"s5_bwd_{d}", ride=_reduce_sibling(slabs))
        du_dirs.append(du_d)
        adj.append((as_re, as_im))
        sib[names] = got
        part[names] = _partial_sums(slabs, got, names)
    for d in range(2):
        (r_re, r_im, gbt_re, gbt_im, gct_re, gct_im), got = _s5_reduce(
            fwd[d][0], fwd[d][1], adj[d][0], adj[d][1], u, d_ypre, name=f"s5_reduce_{d}",
            ride=_reduce_chips(part[mix_names]) if d == 0 else None)
        if d == 0:
            mix_chips = got
        r_parts.append((r_re.reshape(n_state, 1), r_im.reshape(n_state, 1)))
        gb_parts.append((_block_diag_take(gbt_re, SSM_C, SSM_P), _block_diag_take(gbt_im, SSM_C, SSM_P)))
        gc_parts.append((_block_diag_take(gct_re, SSM_C, SSM_P), _block_diag_take(gct_im, SSM_C, SSM_P)))
    cat = lambda i, parts: jnp.concatenate([parts[0][i], parts[1][i]], axis=0)
    gbb_re, gbb_im = cat(0, gb_parts).reshape(n_col, SSM_C), cat(1, gb_parts).reshape(n_col, SSM_C)
    g_a_re, g_a_im, g_ls, g_b_re, g_b_im = _s5_param_grads(
        a_re_c, a_im_c, dt_c, b_re_c, b_im_c, pw_re[:, 0].reshape(n_col, 1), pw_im[:, 0].reshape(n_col, 1),
        cf_re, cf_im, bb_re, bb_im, cat(0, r_parts), cat(1, r_parts), gbb_re, gbb_im)
    g_c_re = cat(0, gc_parts).reshape(2, n_groups, SSM_P, SSM_C).transpose(0, 1, 3, 2)
    g_c_im = -cat(1, gc_parts).reshape(2, n_groups, SSM_P, SSM_C).transpose(0, 1, 3, 2)

    (d_q, d_k, d_v, d_btab, g_qg, g_kg), ffn_chips = _attn_bwd(z, d_ya, qg4, kg4, btab,
                                                                ride=_reduce_chips(part[ffn_names]))
    d_u = _rowwise(lambda a, b, c: a + b + c, [du_dirs[0], du_dirs[1], du_skip], [], [(s_width, BF16)],
                   name="s5_du_sum")[0]
    d_z = jnp.concatenate([d_q, d_k, d_v, d_u], axis=1)
    d_h1 = _mm(d_z, w["w_in"], name="in_proj_dx", grid=(n1, n_sh),
               a_spec=pl.BlockSpec((t1, in_sh), lambda i, j: (i, j)),
               b_spec=pl.BlockSpec((None, d_model, in_sh), lambda i, j: (j, 0, 0)),
               o_spec=pl.BlockSpec((t1, d_model), lambda i, j: (i, 0)), o_shape=(seq, d_model), dims="nt", k_axis=1)
    g_w_in = _mm(h1, d_z, name="in_proj_dw", grid=(n_sh, n1),
                 a_spec=pl.BlockSpec((t1, d_model), lambda j, k: (k, 0)),
                 b_spec=pl.BlockSpec((t1, in_sh), lambda j, k: (k, j)),
                 o_spec=pl.BlockSpec((None, d_model, in_sh), lambda j, k: (j, 0, 0)),
                 o_shape=(n_sh, d_model, in_sh), dims="tn", k_axis=1)
    grad_x, g_g_mix = _rowwise(residual_rms_bwd, [dx1, d_h1, x], [p["g_mix"]], [(d_model, F32)], [d_model],
                               name="rms_mix_bwd")

    fold_heads = lambda g: g.reshape(n_heads, HEAD_DIM).sum(axis=0, keepdims=True)
    small = {
        "g_mix": g_g_mix, "q_gain": fold_heads(g_qg), "k_gain": fold_heads(g_kg),
        "rpb": _bias_grad(d_btab, n_heads),
        "ssm_a_re": g_a_re.reshape(2, n_groups, SSM_P), "ssm_a_im": g_a_im.reshape(2, n_groups, SSM_P),
        "ssm_b_re": g_b_re.reshape(2, n_groups, SSM_P, SSM_C), "ssm_b_im": g_b_im.reshape(2, n_groups, SSM_P, SSM_C),
        "ssm_c_re": g_c_re, "ssm_c_im": g_c_im,
        "ssm_log_step": g_ls.reshape(2, n_groups, SSM_P).sum(axis=-1),
        "ssm_d": g_ssm_d, "b_glu": g_b_glu, "g_out_attn": g_goa, "g_out_ssm": g_gos, "g_ffn": g_g_ffn,
    }
    reduced = {}
    for names, slabs, chips in ((ffn_names, ffn_slabs, ffn_chips), (mix_names, mix_slabs, mix_chips)):
        for i, n in enumerate(names):
            reduced[n] = (slabs[i], sib[names][i], chips[i])
    return loss, grad_x, small, g_w_in, reduced


def x_norm(xv, g):
    return xv * _rstd(xv) * g


def s5_mid(y0, y1, uv, d_skip):
    ypre = y0 + y1 + d_skip * uv
    return ypre, _gelu(ypre)


def mix_out_fwd(ya, yg, t, b_glu, g_oa, g_os):
    ys = yg * _sigmoid(t + b_glu)
    return jnp.concatenate([ya * _rstd(ya) * g_oa, ys * _rstd(ys) * g_os], axis=1)


def mix_out_bwd(d_y, ya, yg, t, b_glu, g_oa, g_os, *, a_width):
    sg = _sigmoid(t + b_glu)
    ys = yg * sg
    d_ya, c_goa = _rms_bwd(d_y[:, :a_width], ya, g_oa)
    d_ys, c_gos = _rms_bwd(d_y[:, a_width:], ys, g_os)
    d_t = d_ys * yg * sg * (1.0 - sg)
    return d_ya, d_ys * sg, d_t, c_goa, c_gos, d_t


def gelu_skip_bwd(d_yg, ypre, uv, d_skip):
    d_ypre = d_yg * _gelu_grad(ypre)
    return d_ypre, d_ypre * d_skip, d_ypre * uv


def swiglu_fwd(gv, uv):
    gv, uv = gv.astype(F32), uv.astype(F32)
    return gv * _sigmoid(gv) * uv


def swiglu_bwd(d_act, gv, uv):
    d_act, gv, uv = d_act.astype(F32), gv.astype(F32), uv.astype(F32)
    sg = _sigmoid(gv)
    return d_act * uv * (sg * (1.0 + gv * (1.0 - sg))), d_act * gv * sg


def loss_head(ffn_out, x1, target, *, inv_d):
    diff = ffn_out + x1 - target
    return diff * inv_d, diff * inv_d, diff * diff


def residual_rms_bwd(d_res, d_h, xv, g):
    dx, c_g = _rms_bwd(d_h, xv, g)
    return d_res + dx, c_g


_ANY = pl.BlockSpec(memory_space=pl.ANY)


def _mesh_place():
    return lax.axis_index("x"), lax.axis_index("y"), lax.axis_index("c")


def _chips(x, y):
    return [(x, y), (1 - x, y), (x, 1 - y), (1 - x, 1 - y)]


def _slab(px, py, pc):
    return 4 * px + 2 * py + pc


def _all_gather(arrs, *, name):
    n = len(arrs)

    def body(*refs):
        in_refs, out_refs = refs[:n], refs[n:2 * n]
        send_sems, recv_sems, local_sems = refs[2 * n:]
        x, y, c = _mesh_place()
        me, sibling = (x, y, c), (x, y, 1 - c)
        others = _chips(x, y)[1:]

        def copy(w, k, block, to, src=None):
            dst = out_refs[w].at[_slab(*block)]
            return pltpu.make_async_remote_copy(
                src_ref=dst if src is None else src, dst_ref=dst, send_sem=send_sems.at[7 * w + k],
                recv_sem=recv_sems.at[7 * w + k], device_id=to, device_id_type=MESH)

        mine = [pltpu.make_async_copy(in_refs[w], out_refs[w].at[_slab(*me)], local_sems.at[w]) for w in range(n)]
        first = []
        for w in range(n):
            mine[w].start()
            first.append(copy(w, 0, me, sibling, src=in_refs[w]))
            first += [copy(w, 1 + j, me, (*chip, c), src=in_refs[w]) for j, chip in enumerate(others)]
        for cp in first:
            cp.start()
        passed = []
        for j, chip in enumerate(others):
            for w in range(n):
                copy(w, 1 + j, (*chip, c), me).wait_recv()
                fwd = copy(w, 4 + j, (*chip, c), sibling)
                fwd.start()
                passed.append(fwd)
        for w in range(n):
            copy(w, 0, sibling, me).wait_recv()
        for j, chip in enumerate(others):
            for w in range(n):
                copy(w, 4 + j, (*chip, 1 - c), me).wait_recv()
        for cp in first + passed:
            cp.wait_send()
        for cp in mine:
            cp.wait()

    return pl.pallas_call(
        body, name=name, in_specs=[_ANY] * n, out_specs=[_ANY] * n,
        out_shape=[jax.ShapeDtypeStruct((N_DEV,) + a.shape, a.dtype) for a in arrs],
        scratch_shapes=[pltpu.SemaphoreType.DMA((7 * n,)), pltpu.SemaphoreType.DMA((7 * n,)),
                        pltpu.SemaphoreType.DMA((n,))],
        compiler_params=pltpu.CompilerParams(has_side_effects=True),
    )(*arrs)


def _swap(arrs, n_out, plan, *, name):
    n = len(arrs)

    def body(*refs):
        in_refs, out_refs = refs[:n], refs[n:2 * n]
        send_sems, recv_sems = refs[2 * n:]
        copies = []
        for k in range(n_out):
            for w, (src, dst, to) in enumerate(plan(in_refs, out_refs, k)):
                copies.append(pltpu.make_async_remote_copy(
                    src_ref=src, dst_ref=dst, send_sem=send_sems.at[n_out * w + k],
                    recv_sem=recv_sems.at[n_out * w + k], device_id=to, device_id_type=MESH))
        for cp in copies:
            cp.start()
        for cp in copies:
            cp.wait_recv()
        for cp in copies:
            cp.wait_send()

    return pl.pallas_call(
        body, name=name, in_specs=[_ANY] * n, out_specs=[_ANY] * n,
        out_shape=[jax.ShapeDtypeStruct((n_out,) + a.shape[1:], a.dtype) for a in arrs],
        scratch_shapes=[pltpu.SemaphoreType.DMA((n_out * n,)), pltpu.SemaphoreType.DMA((n_out * n,))],
        compiler_params=pltpu.CompilerParams(has_side_effects=True),
    )(*arrs)


def _sibling_exchange(grads):
    def plan(in_refs, out_refs, k):
        x, y, c = _mesh_place()
        px, py = _chips(x, y)[k]
        return [(g.at[_slab(px, py, 1 - c)], o.at[k], (x, y, 1 - c)) for g, o in zip(in_refs, out_refs)]

    return _swap(grads, 4, plan, name="reduce_sibling")


def _chip_exchange(partials):
    def plan(in_refs, out_refs, k):
        x, y, c = _mesh_place()
        px, py = _chips(x, y)[k + 1]
        return [(g.at[k], o.at[k], (px, py, c)) for g, o in zip(in_refs, out_refs)]

    return _swap(partials, 3, plan, name="reduce_chips")


def _adamw(w, m, v, parts, *, name, slab, tr=256):
    rows, cols = w.shape
    tr = _tile(rows, tr)
    n_p = len(parts)

    def body(slab_ref, *refs):
        w_ref, m_ref, v_ref = refs[:3]
        p_refs = refs[3:3 + n_p]
        g_ref, d_ref, nm_ref, nv_ref = refs[3 + n_p:]
        g = None
        for (_, lead), r in zip(parts, p_refs):
            for piece in ([r[...]] if lead is None else [r[i] for i in range(lead)]):
                g = piece.astype(F32) if g is None else g + piece.astype(F32)
        new_m = ADAM_B1 * m_ref[...] + (1.0 - ADAM_B1) * g
        new_v = ADAM_B2 * v_ref[...] + (1.0 - ADAM_B2) * (g * g)
        m_hat = new_m / (1.0 - ADAM_B1 ** ADAM_STEP)
        v_hat = new_v / (1.0 - ADAM_B2 ** ADAM_STEP)
        g_ref[...] = g
        d_ref[...] = -ADAM_LR * (m_hat / (jnp.sqrt(v_hat) + ADAM_EPS) + ADAM_WD * w_ref[...])
        nm_ref[...] = new_m
        nv_ref[...] = new_v

    tile = pl.BlockSpec((tr, cols), lambda i, s: (i, 0))
    p_specs = [pl.BlockSpec((None, tr, cols), lambda i, s: (s[0], i, 0)) if lead is None
               else pl.BlockSpec((lead, tr, cols), lambda i, s: (0, i, 0)) for _, lead in parts]
    return pl.pallas_call(
        body, name=name,
        grid_spec=pltpu.PrefetchScalarGridSpec(num_scalar_prefetch=1, grid=(rows // tr,),
                                               in_specs=[tile] * 3 + p_specs, out_specs=[tile] * 4),
        out_shape=[jax.ShapeDtypeStruct((rows, cols), F32)] * 4, compiler_params=_params(),
    )(jnp.reshape(slab, (1,)).astype(jnp.int32), w, m, v, *[a for a, _ in parts])


_PACK_TILE = SUBLANES * 128
_PACK_ROWS = 512


def _pack(arrs):
    flat = []
    for a in arrs:
        f = a.reshape(-1)
        flat.append(jnp.pad(f, (0, (-f.shape[0]) % _PACK_TILE)))
    total = sum(f.shape[0] for f in flat)
    flat.append(jnp.zeros(((-total) % (_PACK_ROWS * 128),), F32))
    return jnp.concatenate(flat).reshape(-1, 128)


def _unpack(buf, shapes):
    out, at = [], 0
    flat = buf.reshape(-1)
    for s in shapes:
        n = math.prod(s)
        out.append(flat[at:at + n].reshape(s))
        at += n + (-n) % _PACK_TILE
    return out


BIG = ("w_in", "w_glu", "w_out", "w_ffn_gate", "w_ffn_up", "w_ffn_down")
WEIGHTS = ("g_mix", "w_in", "q_gain", "k_gain", "rpb", "ssm_a_re", "ssm_a_im", "ssm_b_re", "ssm_b_im", "ssm_c_re",
           "ssm_c_im", "ssm_log_step", "ssm_d", "w_glu", "b_glu", "g_out_attn", "g_out_ssm", "w_out", "g_ffn",
           "w_ffn_gate", "w_ffn_up", "w_ffn_down")
SMALL = tuple(n for n in WEIGHTS if n not in BIG)
VECTORS = ("g_mix", "q_gain", "k_gain", "ssm_d", "b_glu", "g_out_attn", "g_out_ssm", "g_ffn")


def kernel(x, g_mix, w_in, q_gain, k_gain, rpb, ssm_a_re, ssm_a_im, ssm_b_re, ssm_b_im, ssm_c_re, ssm_c_im, ssm_log_step, ssm_d, w_glu, b_glu, g_out_attn, g_out_ssm, w_out, g_ffn, w_ffn_gate, w_ffn_up, w_ffn_down, loss_target, m_g_mix, m_w_in, m_q_gain, m_k_gain, m_rpb, m_ssm_a_re, m_ssm_a_im, m_ssm_b_re, m_ssm_b_im, m_ssm_c_re, m_ssm_c_im, m_ssm_log_step, m_ssm_d, m_w_glu, m_b_glu, m_g_out_attn, m_g_out_ssm, m_w_out, m_g_ffn, m_w_ffn_gate, m_w_ffn_up, m_w_ffn_down, v_g_mix, v_w_in, v_q_gain, v_k_gain, v_rpb, v_ssm_a_re, v_ssm_a_im, v_ssm_b_re, v_ssm_b_im, v_ssm_c_re, v_ssm_c_im, v_ssm_log_step, v_ssm_d, v_w_glu, v_b_glu, v_g_out_attn, v_g_out_ssm, v_w_out, v_g_ffn, v_w_ffn_gate, v_w_ffn_up, v_w_ffn_down):
    wts = dict(g_mix=g_mix, w_in=w_in, q_gain=q_gain, k_gain=k_gain, rpb=rpb, ssm_a_re=ssm_a_re, ssm_a_im=ssm_a_im,
               ssm_b_re=ssm_b_re, ssm_b_im=ssm_b_im, ssm_c_re=ssm_c_re, ssm_c_im=ssm_c_im, ssm_log_step=ssm_log_step,
               ssm_d=ssm_d, w_glu=w_glu, b_glu=b_glu, g_out_attn=g_out_attn, g_out_ssm=g_out_ssm, w_out=w_out,
               g_ffn=g_ffn, w_ffn_gate=w_ffn_gate, w_ffn_up=w_ffn_up, w_ffn_down=w_ffn_down)
    mom = dict(g_mix=m_g_mix, w_in=m_w_in, q_gain=m_q_gain, k_gain=m_k_gain, rpb=m_rpb, ssm_a_re=m_ssm_a_re,
               ssm_a_im=m_ssm_a_im, ssm_b_re=m_ssm_b_re, ssm_b_im=m_ssm_b_im, ssm_c_re=m_ssm_c_re,
               ssm_c_im=m_ssm_c_im, ssm_log_step=m_ssm_log_step, ssm_d=m_ssm_d, w_glu=m_w_glu, b_glu=m_b_glu,
               g_out_attn=m_g_out_attn, g_out_ssm=m_g_out_ssm, w_out=m_w_out, g_ffn=m_g_ffn,
               w_ffn_gate=m_w_ffn_gate, w_ffn_up=m_w_ffn_up, w_ffn_down=m_w_ffn_down)
    var = dict(g_mix=v_g_mix, w_in=v_w_in, q_gain=v_q_gain, k_gain=v_k_gain, rpb=v_rpb, ssm_a_re=v_ssm_a_re,
               ssm_a_im=v_ssm_a_im, ssm_b_re=v_ssm_b_re, ssm_b_im=v_ssm_b_im, ssm_c_re=v_ssm_c_re,
               ssm_c_im=v_ssm_c_im, ssm_log_step=v_ssm_log_step, ssm_d=v_ssm_d, w_glu=v_w_glu, b_glu=v_b_glu,
               g_out_attn=v_g_out_attn, g_out_ssm=v_g_out_ssm, w_out=v_w_out, g_ffn=v_g_ffn,
               w_ffn_gate=v_w_ffn_gate, w_ffn_up=v_w_ffn_up, w_ffn_down=v_w_ffn_down)
    ix, iy, ic = _mesh_place()
    me = _slab(ix, iy, ic)
    d_model = x.shape[-1]

    shard = {n: wts[n][0] for n in BIG}
    shard_b = {n: shard[n].astype(BF16) for n in BIG}
    w_in_full = _all_gather([shard_b["w_in"]], name="gather_w_in")[0]
    p = {n: (wts[n][0].reshape(1, -1) if n in VECTORS else wts[n][0]) for n in SMALL}

    loss, grad_x, g_small, g_w_in, reduced = _local_step(x[0], loss_target[0], p, w_in_full,
                                                         {n: shard_b[n] for n in BIG if n != "w_in"})
    loss = lax.psum(loss, ("x", "y", "c"))

    in_slabs = [g_w_in]
    in_sibling = _sibling_exchange(in_slabs)
    in_chips = _chip_exchange(_partial_sums(in_slabs, in_sibling, ("w_in",)))
    reduced["w_in"] = (in_slabs[0], in_sibling[0], in_chips[0])
    out = {}
    for n in BIG:
        slabs, from_sibling, from_chips = reduced[n]
        rows, cols = slabs.shape[1:]
        res = _adamw(shard[n].reshape(rows, cols), mom[n][0].reshape(rows, cols), var[n][0].reshape(rows, cols),
                     [(slabs, None), (from_sibling, 1), (from_chips, 3)], name=f"adamw_{n}", slab=me)
        out[n] = [r.reshape(wts[n].shape) for r in res]

    order = list(SMALL)
    shapes = [wts[n].shape for n in order]
    packed = _pack([g_small[n] for n in order])
    gathered = _all_gather([packed], name="gather_small_grads")[0]
    res = _adamw(_pack([wts[n] for n in order]), _pack([mom[n] for n in order]), _pack([var[n] for n in order]),
                 [(gathered, N_DEV)], name="adamw_small", slab=me)
    for kind, buf in enumerate(res):
        for n, a in zip(order, _unpack(buf, shapes)):
            out.setdefault(n, [None] * 4)[kind] = a

    return (loss, grad_x[None], *[out[n][0] for n in WEIGHTS], *[out[n][1] for n in WEIGHTS],
            *[out[n][2] for n in WEIGHTS], *[out[n][3] for n in WEIGHTS])
```

```python
import functools
import math

import jax
import jax.numpy as jnp
from jax import lax
from jax.experimental import pallas as pl
from jax.experimental.pallas import tpu as pltpu

F32 = jnp.float32
BF16 = jnp.bfloat16

N_DEV = 8
GRID_W = 64
WIN_H = 8
WIN_W = 16
HEAD_DIM = 64
HEADS_PER_GROUP = 4
GROUP_LANES = HEADS_PER_GROUP * HEAD_DIM
SSM_C = 16
SSM_P = 64
GROUPS_PER_TILE = 8
U_TILE = GROUPS_PER_TILE * SSM_C
ST_TILE = GROUPS_PER_TILE * SSM_P
SUBLANES = 8
RMS_EPS = 1e-6
NEG_INF = -1e30
A_RE_MAX = -1e-4
ADAM_LR, ADAM_B1, ADAM_B2, ADAM_EPS, ADAM_WD, ADAM_STEP = 0.001, 0.9, 0.999, 1e-08, 0.01, 10
VMEM_LIMIT_V7X = 56 * 1024 * 1024
MESH = pl.DeviceIdType.MESH

_NN = (((1,), (0,)), ((), ()))
_NT = (((1,), (1,)), ((), ()))
_TN = (((0,), (0,)), ((), ()))
_DIMS = {"nn": _NN, "nt": _NT, "tn": _TN}


def _params(**kw):
    return pltpu.CompilerParams(vmem_limit_bytes=VMEM_LIMIT_V7X, **kw)


def _dot(a, b, dims=_NN):
    return lax.dot_general(a, b, dims, preferred_element_type=F32)


def _mm(a, b, *, name, grid, a_spec, b_spec, o_spec, o_shape, dims, k_axis=None, res=None, out_dtype=F32,
        exact=False, second=None):
    dn = _DIMS[dims]
    nk = 1 if k_axis is None else grid[k_axis]
    acc_shape = tuple(d for d in o_spec.block_shape if d is not None)
    n_in = 2 + (2 if second is not None else 0)

    def body(*refs):
        a_ref, b_ref = refs[:2]
        r_ref = refs[n_in] if res is not None else None
        o_ref, acc = refs[-2:]
        if exact:
            p = lax.dot_general(a_ref[...], b_ref[...], dn, precision=lax.Precision.HIGHEST,
                                preferred_element_type=F32)
        else:
            p = _dot(a_ref[...].astype(BF16), b_ref[...].astype(BF16), dn)
        if second is not None:
            p = p + _dot(refs[2][...].astype(BF16), refs[3][...].astype(BF16), dn)

        def finish(v):
            if r_ref is not None:
                v = v + r_ref[...].astype(F32)
            o_ref[...] = v.astype(out_dtype)

        if nk == 1:
            finish(p)
        else:
            k = pl.program_id(k_axis)

            @pl.when(k == 0)
            def _():
                acc[...] = p

            @pl.when(k > 0)
            def _():
                acc[...] += p

            @pl.when(k == nk - 1)
            def _():
                finish(acc[...])

    ins = [a, b] + (list(second) if second is not None else []) + ([res] if res is not None else [])
    in_specs = [a_spec, b_spec] * (n_in // 2) + ([o_spec] if res is not None else [])
    return pl.pallas_call(
        body, name=name, grid=grid, in_specs=in_specs, out_specs=o_spec,
        out_shape=jax.ShapeDtypeStruct(o_shape, out_dtype),
        scratch_shapes=[pltpu.VMEM(acc_shape if nk > 1 else (SUBLANES, 128), F32)],
        compiler_params=_params(),
    )(*ins)


def _tile(n, want):
    if n <= want:
        return n
    t = want
    while n % t:
        t //= 2
    return t


def _mm_plain(a, b, dims, *, name, res=None, out_dtype=F32, tm=512, tn=512, tk=512, exact=False):
    if dims == "nn":
        (m, k), n = a.shape, b.shape[1]
    elif dims == "nt":
        (m, k), n = a.shape, b.shape[0]
    else:
        (k, m), n = a.shape, b.shape[1]
    tm, tn = _tile(m, tm), _tile(n, tn)
    if dims == "tn":
        tk = _tile(k, tk)
        grid = (m // tm, n // tn, k // tk)
        a_spec = pl.BlockSpec((tk, tm), lambda i, j, kk: (kk, i))
        b_spec = pl.BlockSpec((tk, tn), lambda i, j, kk: (kk, j))
        o_spec = pl.BlockSpec((tm, tn), lambda i, j, kk: (i, j))
        return _mm(a, b, name=name, grid=grid, a_spec=a_spec, b_spec=b_spec, o_spec=o_spec, o_shape=(m, n),
                   dims=dims, k_axis=2, res=res, out_dtype=out_dtype)
    grid = (n // tn, m // tm)
    a_spec = pl.BlockSpec((tm, k), lambda j, i: (i, 0))
    if dims == "nn":
        b_spec = pl.BlockSpec((k, tn), lambda j, i: (0, j))
    else:
        b_spec = pl.BlockSpec((tn, k), lambda j, i: (j, 0))
    o_spec = pl.BlockSpec((tm, tn), lambda j, i: (i, j))
    return _mm(a, b, name=name, grid=grid, a_spec=a_spec, b_spec=b_spec, o_spec=o_spec, o_shape=(m, n), dims=dims,
               res=res, out_dtype=out_dtype, exact=exact)


def _rowwise(fn, tiled, bcast, outs, accs=(), *, name, tm=256):
    m = tiled[0].shape[0]
    tm = _tile(m, tm)
    n_t, n_b, n_o, n_a = len(tiled), len(bcast), len(outs), len(accs)

    def body(*refs):
        ins = [r[...] for r in refs[: n_t + n_b]]
        o_refs = refs[n_t + n_b: n_t + n_b + n_o]
        a_refs = refs[n_t + n_b + n_o:]
        res = fn(*ins)
        if not isinstance(res, (tuple, list)):
            res = (res,)
        for r, v in zip(o_refs, res[:n_o]):
            r[...] = v.astype(r.dtype)
        first = pl.program_id(0) == 0
        for r, v in zip(a_refs, res[n_o:]):
            s = jnp.sum(v, axis=0, keepdims=True)

            @pl.when(first)
            def _():
                r[...] = s

            @pl.when(jnp.logical_not(first))
            def _():
                r[...] += s

    in_specs = [pl.BlockSpec((tm, t.shape[1]), lambda i: (i, 0)) for t in tiled]
    in_specs += [pl.BlockSpec(b.shape, lambda i, nd=b.ndim: (0,) * nd) for b in bcast]
    out_specs = [pl.BlockSpec((tm, n), lambda i: (i, 0)) for n, _ in outs]
    out_specs += [pl.BlockSpec((1, n), lambda i: (0, 0)) for n in accs]
    out_shape = [jax.ShapeDtypeStruct((m, n), dt) for n, dt in outs]
    out_shape += [jax.ShapeDtypeStruct((1, n), F32) for n in accs]
    res = pl.pallas_call(body, name=name, grid=(m // tm,), in_specs=in_specs, out_specs=out_specs,
                         out_shape=out_shape, compiler_params=_params())(*tiled, *bcast)
    return res


def _rstd(x):
    return lax.rsqrt(jnp.mean(x * x, axis=-1, keepdims=True) + RMS_EPS)


def _rms_bwd(dh, x, g):
    xh = x * _rstd(x)
    dxh = dh * g
    dx = _rstd(x) * (dxh - xh * jnp.mean(dxh * xh, axis=-1, keepdims=True))
    return dx, dh * xh


def _sigmoid(x):
    return 1.0 / (1.0 + jnp.exp(-x))


_GELU_K = math.sqrt(2.0 / math.pi)
_GELU_C = 0.044715


def _gelu(x):
    return 0.5 * x * (1.0 + jnp.tanh(_GELU_K * (x + _GELU_C * x * x * x)))


def _gelu_grad(x):
    th = jnp.tanh(_GELU_K * (x + _GELU_C * x * x * x))
    return 0.5 * (1.0 + th) + 0.5 * x * (1.0 - th * th) * _GELU_K * (1.0 + 3.0 * _GELU_C * x * x)


class _Exchange:
    def __init__(self, arrays, outs, n_sems, sends, recvs=None, local=None, aliases=None):
        self.arrays, self.outs, self.n_sems = list(arrays), list(outs), n_sems
        self.sends, self.local, self.aliases = sends, local, aliases or {}
        self.recvs = recvs or (lambda i, o: [(k, dst) for k, _, dst, _ in sends(i, o)])

    def descriptors(self, in_refs, out_refs, send_sems, recv_sems, local_sems):
        me = _mesh_place()
        remote = lambda k, src, dst, to: pltpu.make_async_remote_copy(
            src_ref=src, dst_ref=dst, send_sem=send_sems.at[k], recv_sem=recv_sems.at[k], device_id=to,
            device_id_type=MESH)
        out = [remote(*s) for s in self.sends(in_refs, out_refs)]
        arrive = [remote(k, dst, dst, me) for k, dst in self.recvs(in_refs, out_refs)]
        own = [pltpu.make_async_copy(src, dst, local_sems.at[i])
               for i, (src, dst) in enumerate(self.local(in_refs, out_refs) if self.local else [])]
        return out, arrive, own

    def start(self, *refs):
        out, _, own = self.descriptors(*refs)
        for cp in own + out:
            cp.start()

    def finish(self, *refs):
        out, arrive, own = self.descriptors(*refs)
        for cp in arrive:
            cp.wait_recv()
        for cp in out:
            cp.wait_send()
        for cp in own:
            cp.wait()


def _call(body, *, name, grid, in_specs, out_specs, out_shape, scratch_shapes, args, ride=None, first=None, last=None):
    if ride is None:
        res = pl.pallas_call(body, name=name, grid=grid, in_specs=in_specs, out_specs=out_specs, out_shape=out_shape,
                             scratch_shapes=scratch_shapes, compiler_params=_params())(*args)
        return list(res), []
    n_in, n_out, n_scr = len(in_specs), len(out_specs), len(scratch_shapes)
    r_in, r_out = len(ride.arrays), len(ride.outs)

    def wrapped(*refs):
        ins, refs = refs[:n_in], refs[n_in:]
        x_in, refs = refs[:r_in], refs[r_in:]
        outs, refs = refs[:n_out], refs[n_out:]
        x_out, refs = refs[:r_out], refs[r_out:]
        scr, sems = refs[:n_scr], refs[n_scr:]

        @pl.when(first())
        def _():
            ride.start(x_in, x_out, *sems)

        body(*ins, *outs, *scr)

        @pl.when(last())
        def _():
            ride.finish(x_in, x_out, *sems)

    n_local = max(1, len(ride.arrays))
    res = pl.pallas_call(
        wrapped, name=name, grid=grid, in_specs=list(in_specs) + [_ANY] * r_in,
        out_specs=list(out_specs) + [_ANY] * r_out, out_shape=list(out_shape) + ride.outs,
        scratch_shapes=list(scratch_shapes) + [pltpu.SemaphoreType.DMA((ride.n_sems,)),
                                               pltpu.SemaphoreType.DMA((ride.n_sems,)),
                                               pltpu.SemaphoreType.DMA((n_local,))],
        input_output_aliases={n_in + i: n_out + o for i, o in ride.aliases.items()},
        compiler_params=_params(has_side_effects=True),
    )(*args, *ride.arrays)
    return list(res[:n_out]), list(res[n_out:])


def _gather_first(shards):
    def sends(i, o):
        x, y, c = _mesh_place()
        peers = [(x, y, 1 - c)] + [(px, py, c) for px, py in _chips(x, y)[1:]]
        return [(4 * w + k, i[w], o[w].at[_slab(x, y, c)], to) for w in range(len(i)) for k, to in enumerate(peers)]

    def recvs(i, o):
        x, y, c = _mesh_place()
        peers = [(x, y, 1 - c)] + [(px, py, c) for px, py in _chips(x, y)[1:]]
        return [(4 * w + k, o[w].at[_slab(*peer)]) for w in range(len(i)) for k, peer in enumerate(peers)]

    def local(i, o):
        return [(i[w], o[w].at[_slab(*_mesh_place())]) for w in range(len(i))]

    outs = [jax.ShapeDtypeStruct((N_DEV,) + a.shape, a.dtype) for a in shards]
    return _Exchange(shards, outs, 4 * len(shards), sends, recvs, local)


def _gather_second(gathered):
    def sends(i, o):
        x, y, c = _mesh_place()
        return [(3 * w + j, o[w].at[_slab(px, py, c)], o[w].at[_slab(px, py, c)], (x, y, 1 - c))
                for w in range(len(o)) for j, (px, py) in enumerate(_chips(x, y)[1:])]

    def recvs(i, o):
        x, y, c = _mesh_place()
        return [(3 * w + j, o[w].at[_slab(px, py, 1 - c)])
                for w in range(len(o)) for j, (px, py) in enumerate(_chips(x, y)[1:])]

    outs = [jax.ShapeDtypeStruct(a.shape, a.dtype) for a in gathered]
    return _Exchange(gathered, outs, 3 * len(gathered), sends, recvs, aliases={w: w for w in range(len(gathered))})


def _reduce_sibling(slabs):
    def sends(i, o):
        x, y, c = _mesh_place()
        return [(4 * w + k, i[w].at[_slab(px, py, 1 - c)], o[w].at[k], (x, y, 1 - c))
                for w in range(len(i)) for k, (px, py) in enumerate(_chips(x, y))]

    outs = [jax.ShapeDtypeStruct((4,) + a.shape[1:], a.dtype) for a in slabs]
    return _Exchange(slabs, outs, 4 * len(slabs), sends)


def _reduce_chips(partials):
    def sends(i, o):
        x, y, c = _mesh_place()
        return [(3 * w + k, i[w].at[k], o[w].at[k], (px, py, c))
                for w in range(len(i)) for k, (px, py) in enumerate(_chips(x, y)[1:])]

    outs = [jax.ShapeDtypeStruct(a.shape, a.dtype) for a in partials]
    return _Exchange(partials, outs, 3 * len(partials), sends)


def _head_masks():
    lane_head = lax.broadcasted_iota(jnp.int32, (1, GROUP_LANES), 1) // HEAD_DIM
    return [(lane_head == h).astype(F32) for h in range(HEADS_PER_GROUP)]


def _head_block_diag():
    r = lax.broadcasted_iota(jnp.int32, (GROUP_LANES, GROUP_LANES), 0) // HEAD_DIM
    c = lax.broadcasted_iota(jnp.int32, (GROUP_LANES, GROUP_LANES), 1) // HEAD_DIM
    return (r == c).astype(BF16)


def _head_mean(x, bd):
    hi = x.astype(BF16)
    lo = (x - hi.astype(F32)).astype(BF16)
    return (_dot(hi, bd) + _dot(lo, bd)) * (1.0 / HEAD_DIM)


def _stack_heads(x, masks):
    return jnp.concatenate([x * m for m in masks], axis=0)


def _unstack_heads(xs, masks):
    out = xs[0:GRID_W] * masks[0]
    for h in range(1, HEADS_PER_GROUP):
        out = out + xs[h * GRID_W:(h + 1) * GRID_W] * masks[h]
    return out


def _row_start(r, rows):
    return jnp.clip(r - WIN_H // 2, 0, rows - WIN_H)


ROWS_PER_STEP = 2


def _attn_common_specs(seq, n_hg, rows):
    win_keys = WIN_H * GRID_W
    q_spec = pl.BlockSpec((ROWS_PER_STEP * GRID_W, GROUP_LANES), lambda g, r: (r, g))
    k_spec = pl.BlockSpec((seq, GROUP_LANES), lambda g, r: (0, n_hg + g))
    v_spec = pl.BlockSpec((seq, GROUP_LANES), lambda g, r: (0, 2 * n_hg + g))
    gain_spec = pl.BlockSpec((1, GROUP_LANES), lambda g, r: (0, 0))

    def variant(r):
        return _row_start(r, rows) - r + (WIN_H - 1)

    bias_specs = [pl.BlockSpec((None, None, HEADS_PER_GROUP, GRID_W, win_keys),
                               lambda g, r, h=h: (g, variant(ROWS_PER_STEP * r + h), 0, 0, 0))
                  for h in range(ROWS_PER_STEP)]
    return q_spec, k_spec, v_spec, gain_spec, bias_specs, variant


def _attn_prepare_kv(k_ref, v_ref, kg, kn_scr, vb_scr, bd, seq):
    chunk = _tile(seq, 512)

    def step(c, carry):
        rows = pl.ds(pl.multiple_of(c * chunk, chunk), chunk)
        k = k_ref[rows, :]
        kn_scr[rows, :] = (k * lax.rsqrt(_head_mean(k * k, bd) + RMS_EPS) * kg).astype(BF16)
        vb_scr[rows, :] = v_ref[rows, :].astype(BF16)
        return carry

    lax.fori_loop(0, seq // chunk, step, 0)


def _attn_probs(qn, kw, bias, masks):
    qs = _stack_heads(qn, masks).astype(BF16)
    s = _dot(qs, kw, _NT) * (1.0 / math.sqrt(HEAD_DIM)) + bias
    m = jnp.max(s, axis=-1, keepdims=True)
    p = jnp.exp(s - m)
    return qs, p / jnp.sum(p, axis=-1, keepdims=True)


def _grid_ends(grid):
    first = lambda: functools.reduce(jnp.logical_and, [pl.program_id(a) == 0 for a in range(len(grid))])
    last = lambda: functools.reduce(jnp.logical_and, [pl.program_id(a) == n - 1 for a, n in enumerate(grid)])
    return first, last


def _attn_fwd(z, qg4, kg4, btab, ride=None):
    seq = z.shape[0]
    a_width = btab.shape[0] * GROUP_LANES
    n_hg, rows, win_keys = btab.shape[0], seq // GRID_W, WIN_H * GRID_W
    q_spec, k_spec, v_spec, gain_spec, bias_specs, _ = _attn_common_specs(seq, n_hg, rows)
    grid = (n_hg, rows // ROWS_PER_STEP)

    def body(q_ref, k_ref, v_ref, qg_ref, kg_ref, *rest):
        b_refs, (o_ref, kn_scr, vb_scr) = rest[:ROWS_PER_STEP], rest[ROWS_PER_STEP:]
        bd, masks = _head_block_diag(), _head_masks()

        @pl.when(pl.program_id(1) == 0)
        def _():
            _attn_prepare_kv(k_ref, v_ref, kg_ref[...], kn_scr, vb_scr, bd, seq)

        for h in range(ROWS_PER_STEP):
            r = ROWS_PER_STEP * pl.program_id(1) + h
            mine = slice(h * GRID_W, (h + 1) * GRID_W)
            win = pl.ds(pl.multiple_of(_row_start(r, rows) * GRID_W, GRID_W), win_keys)
            q = q_ref[mine, :]
            qn = q * lax.rsqrt(_head_mean(q * q, bd) + RMS_EPS) * qg_ref[...]
            bias = b_refs[h][...].reshape(HEADS_PER_GROUP * GRID_W, win_keys)
            _, p = _attn_probs(qn, kn_scr[win, :], bias, masks)
            o_ref[mine, :] = _unstack_heads(_dot(p.astype(BF16), vb_scr[win, :]), masks)

    first, last = _grid_ends(grid)
    (ya,), rode = _call(
        body, name="attn_fwd", grid=grid,
        in_specs=[q_spec, k_spec, v_spec, gain_spec, gain_spec] + bias_specs,
        out_specs=[pl.BlockSpec((ROWS_PER_STEP * GRID_W, GROUP_LANES), lambda g, r: (r, g))],
        out_shape=[jax.ShapeDtypeStruct((seq, a_width), F32)],
        scratch_shapes=[pltpu.VMEM((seq, GROUP_LANES), BF16), pltpu.VMEM((seq, GROUP_LANES), BF16)],
        args=(z, z, z, qg4, kg4) + (btab,) * ROWS_PER_STEP, ride=ride, first=first, last=last)
    return ya, rode


def _attn_bwd(z, d_out, qg4, kg4, btab, ride=None):
    seq = z.shape[0]
    n_hg, rows, win_keys = btab.shape[0], seq // GRID_W, WIN_H * GRID_W
    a_width = n_hg * GROUP_LANES
    q_spec, k_spec, v_spec, gain_spec, bias_specs, variant = _attn_common_specs(seq, n_hg, rows)
    scale = 1.0 / math.sqrt(HEAD_DIM)
    grid = (n_hg, rows // ROWS_PER_STEP)

    def body(q_ref, k_ref, v_ref, do_ref, qg_ref, kg_ref, *rest):
        b_refs, rest = rest[:ROWS_PER_STEP], rest[ROWS_PER_STEP:]
        dq_ref, dk_out, dv_out, db_ref, dqg_ref, dkg_ref, kn_scr, vb_scr, dk_ref, dv_ref = rest
        bd, masks = _head_block_diag(), _head_masks()

        @pl.when(pl.program_id(1) == 0)
        def _():
            _attn_prepare_kv(k_ref, v_ref, kg_ref[...], kn_scr, vb_scr, bd, seq)
            dk_ref[...] = jnp.zeros_like(dk_ref)
            dv_ref[...] = jnp.zeros_like(dv_ref)
            db_ref[...] = jnp.zeros_like(db_ref)
            dqg_ref[...] = jnp.zeros_like(dqg_ref)

        qg = qg_ref[...]
        for h in range(ROWS_PER_STEP):
            r = ROWS_PER_STEP * pl.program_id(1) + h
            mine = slice(h * GRID_W, (h + 1) * GRID_W)
            win = pl.ds(pl.multiple_of(_row_start(r, rows) * GRID_W, GRID_W), win_keys)
            q = q_ref[mine, :]
            rq = lax.rsqrt(_head_mean(q * q, bd) + RMS_EPS)
            qh = q * rq
            kw, vw = kn_scr[win, :], vb_scr[win, :]
            bias = b_refs[h][...].reshape(HEADS_PER_GROUP * GRID_W, win_keys)
            qs, p = _attn_probs(qh * qg, kw, bias, masks)
            dos = _stack_heads(do_ref[mine, :], masks).astype(BF16)
            dp = _dot(dos, vw, _NT)
            ds = p * (dp - jnp.sum(p * dp, axis=-1, keepdims=True))
            db_ref[variant(r)] += ds.reshape(HEADS_PER_GROUP, GRID_W, win_keys)
            dsb = ds.astype(BF16)
            dqn = _unstack_heads(_dot(dsb, kw), masks) * scale
            dk_ref[win, :] += _dot(dsb, qs, _TN) * scale
            dv_ref[win, :] += _dot(p.astype(BF16), dos, _TN)
            dqg_ref[...] += jnp.sum(dqn * qh, axis=0, keepdims=True)
            dqh = dqn * qg
            dq_ref[mine, :] = (rq * (dqh - qh * _head_mean(dqh * qh, bd))).astype(BF16)

        @pl.when(pl.program_id(1) == grid[1] - 1)
        def _():
            chunk = _tile(seq, 512)
            kg = kg_ref[...]

            def step(c, dkg):
                rws = pl.ds(pl.multiple_of(c * chunk, chunk), chunk)
                k = k_ref[rws, :]
                rk = lax.rsqrt(_head_mean(k * k, bd) + RMS_EPS)
                kh = k * rk
                dkn = dk_ref[rws, :]
                dkh = dkn * kg
                dk_out[rws, :] = (rk * (dkh - kh * _head_mean(dkh * kh, bd))).astype(BF16)
                dv_out[rws, :] = dv_ref[rws, :].astype(BF16)
                return dkg + jnp.sum(dkn * kh, axis=0, keepdims=True)

            dkg_ref[...] = lax.fori_loop(0, seq // chunk, step, jnp.zeros((1, GROUP_LANES), F32))

    col_spec = pl.BlockSpec((seq, GROUP_LANES), lambda g, r: (0, g))
    gsum_spec = pl.BlockSpec((None, 1, GROUP_LANES), lambda g, r: (g, 0, 0))
    first, last = _grid_ends(grid)
    rows_spec = pl.BlockSpec((ROWS_PER_STEP * GRID_W, GROUP_LANES), lambda g, r: (r, g))
    return _call(
        body, name="attn_bwd", grid=grid,
        in_specs=[q_spec, k_spec, v_spec, rows_spec, gain_spec, gain_spec] + bias_specs,
        out_specs=[rows_spec, col_spec, col_spec,
                   pl.BlockSpec((None, WIN_H, HEADS_PER_GROUP, GRID_W, win_keys), lambda g, r: (g, 0, 0, 0, 0)),
                   gsum_spec, gsum_spec],
        out_shape=[jax.ShapeDtypeStruct((seq, a_width), BF16)] * 3
        + [jax.ShapeDtypeStruct(btab.shape, F32)]
        + [jax.ShapeDtypeStruct((n_hg, 1, GROUP_LANES), F32)] * 2,
        scratch_shapes=[pltpu.VMEM((seq, GROUP_LANES), BF16), pltpu.VMEM((seq, GROUP_LANES), BF16),
                        pltpu.VMEM((seq, GROUP_LANES), F32), pltpu.VMEM((seq, GROUP_LANES), F32)],
        args=(z, z, z, d_out, qg4, kg4) + (btab,) * ROWS_PER_STEP, ride=ride, first=first, last=last)


def _bias_index():
    c = jnp.arange(GRID_W)
    col_start = jnp.clip(c - WIN_W // 2, 0, GRID_W - WIN_W)
    col_in = (c[None, :] >= col_start[:, None]) & (c[None, :] < col_start[:, None] + WIN_W)
    dc = jnp.clip(c[None, :] - c[:, None], -(WIN_W - 1), WIN_W - 1) + (WIN_W - 1)
    dr = jnp.arange(WIN_H)[:, None] + jnp.arange(WIN_H)[None, :]
    return col_in, dc, dr


def _bias_table(rpb):
    col_in, dc, _ = _bias_index()
    n_h = rpb.shape[0]
    n_hg = n_h // HEADS_PER_GROUP
    spread = ((jnp.arange(128)[:, None] == dc.reshape(1, -1)) & col_in.reshape(1, -1)).astype(F32)
    rows = jnp.stack([rpb[:, v:v + WIN_H] for v in range(WIN_H)], axis=1)
    rows = jnp.pad(rows, ((0, 0), (0, 0), (0, 0), (0, 128 - rows.shape[-1]))).reshape(n_h * WIN_H * WIN_H, 128)
    tab = _mm_plain(rows, spread, "nn", name="rpb_spread", tm=256, tn=2048, exact=True)
    tab = jnp.where(col_in.reshape(1, -1), tab, NEG_INF)
    tab = tab.reshape(n_hg, HEADS_PER_GROUP, WIN_H, WIN_H, GRID_W, GRID_W).transpose(0, 2, 1, 4, 3, 5)
    return tab.reshape(n_hg, WIN_H, HEADS_PER_GROUP, GRID_W, WIN_H * GRID_W)


def _bias_grad(dtab, n_h):
    col_in, dc, _ = _bias_index()
    onehot = (dc.reshape(-1, 1) == jnp.arange(128)[None, :]) & col_in.reshape(-1, 1)
    n_hg = n_h // HEADS_PER_GROUP
    d = dtab.reshape(n_hg, WIN_H, HEADS_PER_GROUP, GRID_W, WIN_H, GRID_W).transpose(0, 2, 1, 4, 3, 5)
    d = d.reshape(n_h * WIN_H * WIN_H, GRID_W * GRID_W)
    diag = _mm_plain(d, onehot.astype(BF16), "nn", name="rpb_diag_sum", tm=256, tn=128)
    diag = diag.reshape(n_h, WIN_H, WIN_H, 128)[..., : 2 * WIN_W - 1]
    out = jnp.zeros((n_h, 2 * WIN_H - 1, 2 * WIN_W - 1), F32)
    for v in range(WIN_H):
        out = out.at[:, v:v + WIN_H].add(diag[:, v])
    return out


def _cmul(ar, ai, br, bi):
    return ar * br - ai * bi, ar * bi + ai * br


def _s5_discretize(a_re, a_im, dt, b_re, b_im):
    c = b_re.shape[1]

    def fn(are, aim, dt_, bre, bim):
        lr, li = jnp.minimum(are, A_RE_MAX), aim
        mag = jnp.exp(lr * dt_)
        l1r, l1i = mag * jnp.cos(li * dt_), mag * jnp.sin(li * dt_)
        den = lr * lr + li * li
        nr, ni = l1r - 1.0, l1i
        cr, ci = (nr * lr + ni * li) / den, (ni * lr - nr * li) / den
        bbr, bbi = _cmul(cr, ci, bre, bim)
        shape = (are.shape[0], SUBLANES)
        lane = lax.broadcasted_iota(jnp.int32, shape, 1)
        pr, pi = l1r, l1i
        acc_r, acc_i = jnp.zeros(shape, F32), jnp.zeros(shape, F32)
        for k in range(SUBLANES):
            acc_r = jnp.where(lane == k, pr, acc_r)
            acc_i = jnp.where(lane == k, pi, acc_i)
            pr, pi = _cmul(pr, pi, l1r, l1i)
        return acc_r, acc_i, cr, ci, bbr, bbi

    return _rowwise(fn, [a_re, a_im, dt, b_re, b_im], [],
                    [(SUBLANES, F32), (SUBLANES, F32), (1, F32), (1, F32), (c, F32), (c, F32)],
                    name="s5_discretize", tm=1024)


def _s5_param_grads(a_re, a_im, dt, b_re, b_im, l1r, l1i, cr, ci, bbr, bbi, r_re, r_im, gb_re, gb_im):
    c = b_re.shape[1]

    def fn(are, aim, dt_, bre, bim, l1r_, l1i_, cr_, ci_, bbr_, bbi_, rr, ri, gbr, gbi):
        lr, li = jnp.minimum(are, A_RE_MAX), aim
        den = lr * lr + li * li
        dbr, dbi = _cmul(cr_, -ci_, gbr, gbi)
        gcr, gci = _cmul(bre, -bim, gbr, gbi)
        gcr, gci = jnp.sum(gcr, axis=1, keepdims=True), jnp.sum(gci, axis=1, keepdims=True)
        qr, qi = _cmul(bbr_, -bbi_, gbr, gbi)
        qr = rr - jnp.sum(qr, axis=1, keepdims=True)
        qi = ri - jnp.sum(qi, axis=1, keepdims=True)
        tr, ti = _cmul(gcr, gci, lr / den, li / den)
        ur, ui = _cmul(l1r_, -l1i_, tr, ti)
        gwr, gwi = qr + ur, qi + ui
        vr, vi = _cmul(cr_, -ci_, lr / den, li / den)
        vr, vi = _cmul(gcr, gci, vr, vi)
        glr, gli = dt_ * gwr - vr, dt_ * gwi - vi
        return jnp.where(are < A_RE_MAX, glr, 0.0), gli, (gwr * lr + gwi * li) * dt_, dbr, dbi

    return _rowwise(fn, [a_re, a_im, dt, b_re, b_im, l1r, l1i, cr, ci, bbr, bbi, r_re, r_im, gb_re, gb_im], [],
                    [(1, F32), (1, F32), (1, F32), (c, F32), (c, F32)], name="s5_param_grads", tm=1024)


def _s5_scan(v, win_re, win_im, tabs, wo_re, wo_im, *, reverse, name, t_chunk=256, ride=None):
    seq, width = v.shape
    n_tiles, n_state = width // U_TILE, width * (SSM_P // SSM_C)
    t_chunk = _tile(seq, t_chunk)
    n_chunks, n_blk = seq // t_chunk, t_chunk // SUBLANES
    last_row = 0 if reverse else SUBLANES - 1

    def chunk_of(j):
        return (n_chunks - 1 - j) if reverse else j

    def body(v_ref, wir_ref, wii_ref, tab_ref, wor_ref, woi_ref, sr_ref, si_ref, y_ref, carry, wr, wi):
        @pl.when(pl.program_id(0) == 0)
        def _():
            carry[...] = jnp.zeros_like(carry)

        for jt in range(n_tiles):
            ls = slice(jt * ST_TILE, (jt + 1) * ST_TILE)
            us = slice(jt * U_TILE, (jt + 1) * U_TILE)
            vj = v_ref[:, us].astype(BF16)
            consts = [tab_ref[k, :, ls] for k in range(8)]
            xr = _dot(vj, wir_ref[jt]).reshape(n_blk, SUBLANES, ST_TILE)
            xi = _dot(vj, wii_ref[jt]).reshape(n_blk, SUBLANES, ST_TILE)
            for s, k in enumerate((1, 2, 4)):
                sh = (SUBLANES - k) if reverse else k
                tr, ti = pltpu.roll(xr, sh, 1), pltpu.roll(xi, sh, 1)
                lr, li = consts[2 * s][None], consts[2 * s + 1][None]
                xr, xi = xr + lr * tr - li * ti, xi + lr * ti + li * tr
            wr[...] = xr.reshape(t_chunk, ST_TILE)
            wi[...] = xi.reshape(t_chunk, ST_TILE)

            def blk(b, c, consts=consts):
                cr, ci = c
                bb = (n_blk - 1 - b) if reverse else b
                rows = pl.ds(pl.multiple_of(bb * SUBLANES, SUBLANES), SUBLANES)
                lr, li = consts[6], consts[7]
                xr = wr[rows, :] + lr * cr - li * ci
                xi = wi[rows, :] + lr * ci + li * cr
                wr[rows, :], wi[rows, :] = xr, xi
                shape = (SUBLANES, ST_TILE)
                return (jnp.broadcast_to(xr[last_row:last_row + 1], shape),
                        jnp.broadcast_to(xi[last_row:last_row + 1], shape))

            cr, ci = lax.fori_loop(0, n_blk, blk, (carry[0, :, ls], carry[1, :, ls]), unroll=2)
            carry[0, :, ls], carry[1, :, ls] = cr, ci
            xr_b, xi_b = wr[...].astype(BF16), wi[...].astype(BF16)
            sr_ref[:, ls], si_ref[:, ls] = xr_b, xi_b
            y_ref[:, us] = _dot(xr_b, wor_ref[jt]) + _dot(xi_b, woi_ref[jt])

    whole = lambda a: pl.BlockSpec(a.shape, lambda j, nd=a.ndim: (0,) * nd)
    st_spec = pl.BlockSpec((t_chunk, n_state), lambda j: (chunk_of(j), 0))
    v_spec = pl.BlockSpec((t_chunk, width), lambda j: (chunk_of(j), 0))
    first, last = _grid_ends((n_chunks,))
    return _call(
        body, name=name, grid=(n_chunks,),
        in_specs=[v_spec, whole(win_re), whole(win_im), whole(tabs), whole(wo_re), whole(wo_im)],
        out_specs=[st_spec, st_spec, v_spec],
        out_shape=[jax.ShapeDtypeStruct((seq, n_state), BF16)] * 2 + [jax.ShapeDtypeStruct((seq, width), F32)],
        scratch_shapes=[pltpu.VMEM((2, SUBLANES, n_state), F32), pltpu.VMEM((t_chunk, ST_TILE), F32),
                        pltpu.VMEM((t_chunk, ST_TILE), F32)],
        args=(v, win_re, win_im, tabs, wo_re, wo_im), ride=ride, first=first, last=last)


def _s5_reduce(x_re, x_im, a_re, a_im, u, dy, *, name, t_chunk=512, ride=None):
    seq, n_state = x_re.shape
    width = u.shape[1]
    n_tiles = width // U_TILE
    t_chunk = _tile(seq, t_chunk)

    def body(xr_ref, xi_ref, ar_ref, ai_ref, u_ref, dy_ref, rr_ref, ri_ref, gbr_ref, gbi_ref, gcr_ref, gci_ref):
        xrb, xib, arb, aib = xr_ref[...], xi_ref[...], ar_ref[...], ai_ref[...]
        xr, xi, ar, ai = xrb.astype(F32), xib.astype(F32), arb.astype(F32), aib.astype(F32)
        ub, dyb = u_ref[...].astype(BF16), dy_ref[...].astype(BF16)
        parts = (jnp.sum(ar * xr + ai * xi, axis=0, keepdims=True), jnp.sum(ai * xr - ar * xi, axis=0, keepdims=True),
                 _dot(arb, ub, _TN), _dot(aib, ub, _TN), _dot(xrb, dyb, _TN), _dot(xib, dyb, _TN))
        first = pl.program_id(1) == 0
        for ref, val in zip((rr_ref, ri_ref, gbr_ref, gbi_ref, gcr_ref, gci_ref), parts):
            @pl.when(first)
            def _():
                ref[...] = val

            @pl.when(jnp.logical_not(first))
            def _():
                ref[...] += val

    st_spec = pl.BlockSpec((t_chunk, ST_TILE), lambda j, t: (t, j))
    u_spec = pl.BlockSpec((t_chunk, U_TILE), lambda j, t: (t, j))
    r_spec = pl.BlockSpec((1, ST_TILE), lambda j, t: (0, j))
    g_spec = pl.BlockSpec((None, ST_TILE, U_TILE), lambda j, t: (j, 0, 0))
    first, last = _grid_ends((n_tiles, seq // t_chunk))
    return _call(
        body, name=name, grid=(n_tiles, seq // t_chunk),
        in_specs=[st_spec] * 4 + [u_spec] * 2,
        out_specs=[r_spec, r_spec] + [g_spec] * 4,
        out_shape=[jax.ShapeDtypeStruct((1, n_state), F32)] * 2
        + [jax.ShapeDtypeStruct((n_tiles, ST_TILE, U_TILE), F32)] * 4,
        scratch_shapes=[], args=(x_re, x_im, a_re, a_im, u, dy), ride=ride, first=first, last=last)


def _block_diag_in(ms):
    m = jnp.stack(ms)
    n, g, c, p = m.shape
    m5 = m.reshape(n, g // GROUPS_PER_TILE, GROUPS_PER_TILE, c, p)
    eye = jnp.eye(GROUPS_PER_TILE, dtype=m.dtype)
    out = m5[:, :, :, :, None, :] * eye[None, None, :, None, :, None]
    return out.astype(BF16).reshape(n, g // GROUPS_PER_TILE, GROUPS_PER_TILE * c, GROUPS_PER_TILE * p)


def _block_diag_take(m, c, p):
    t = m.shape[0]
    m5 = m.reshape(t, GROUPS_PER_TILE, p, GROUPS_PER_TILE, c)
    idx = jnp.arange(GROUPS_PER_TILE)
    return m5[:, idx, :, idx, :].transpose(1, 0, 2, 3).reshape(t * GROUPS_PER_TILE, p, c)


def _scan_tables(pw_re, pw_im, reverse):
    row = jnp.arange(SUBLANES)[:, None]
    tabs = []
    for k in (1, 2, 4):
        keep = (row <= SUBLANES - 1 - k) if reverse else (row >= k)
        tabs += [jnp.where(keep, pw_re[k - 1][None, :], 0.0), jnp.where(keep, pw_im[k - 1][None, :], 0.0)]
    order = jnp.arange(SUBLANES)[::-1] if reverse else jnp.arange(SUBLANES)
    tabs += [pw_re[order], pw_im[order]]
    return jnp.stack(tabs)


def _partial_sums(slabs, from_sibling, names):
    x, y, c = _mesh_place()
    theirs = jnp.stack([_slab(px, py, c) for px, py in _chips(x, y)[1:]]).astype(jnp.int32)
    out = []
    for s, f, n in zip(slabs, from_sibling, names):
        rows, cols = s.shape[1:]
        tr = _tile(rows, 512)

        def body(idx_ref, a_ref, b_ref, o_ref):
            o_ref[...] = (a_ref[...] + b_ref[...]).astype(BF16)

        out.append(pl.pallas_call(
            body, name=f"reduce_add_{n}",
            grid_spec=pltpu.PrefetchScalarGridSpec(
                num_scalar_prefetch=1, grid=(3, rows // tr),
                in_specs=[pl.BlockSpec((None, tr, cols), lambda k, i, idx: (idx[k], i, 0)),
                          pl.BlockSpec((None, tr, cols), lambda k, i, idx: (k + 1, i, 0))],
                out_specs=pl.BlockSpec((None, tr, cols), lambda k, i, idx: (k, i, 0))),
            out_shape=jax.ShapeDtypeStruct((3, rows, cols), BF16), compiler_params=_params(),
        )(theirs, s, f))
    return out


def _local_step(x, target, p, w_in, shards):
    seq, d_model = x.shape
    a_width = p["g_out_attn"].shape[-1]
    s_width = p["g_out_ssm"].shape[-1]
    n_heads = a_width // HEAD_DIM
    n_hg = n_heads // HEADS_PER_GROUP
    n_groups = s_width // SSM_C
    n_sh, _, in_sh = w_in.shape
    f_sh = shards["w_ffn_gate"].shape[-1]
    w = {"w_in": w_in}
    slab3 = lambda g, n: g.reshape(N_DEV, -1, shards[n].shape[-1])
    t2, t1 = _tile(seq, 2048), _tile(seq, 1024)
    n2, n1 = seq // t2, seq // t1

    h1 = _rowwise(lambda xv, g: x_norm(xv, g), [x], [p["g_mix"]], [(d_model, BF16)], name="rms_mix")[0]
    z = _mm(h1, w["w_in"], name="in_proj", grid=(n2, n_sh),
            a_spec=pl.BlockSpec((t2, d_model), lambda i, j: (i, 0)),
            b_spec=pl.BlockSpec((None, d_model, in_sh), lambda i, j: (j, 0, 0)),
            o_spec=pl.BlockSpec((t2, in_sh), lambda i, j: (i, j)), o_shape=(seq, n_sh * in_sh), dims="nn")
    qg4 = jnp.tile(p["q_gain"], (1, HEADS_PER_GROUP))
    kg4 = jnp.tile(p["k_gain"], (1, HEADS_PER_GROUP))
    btab = _bias_table(p["rpb"])
    ya, got_a = _attn_fwd(z, qg4, kg4, btab, ride=_gather_first([shards["w_ffn_gate"], shards["w_ffn_up"]]))
    u = z[:, 3 * a_width:]

    n_col = 2 * n_groups * SSM_P
    col = lambda a: a.reshape(n_col, 1)
    a_re_c, a_im_c = col(p["ssm_a_re"]), col(p["ssm_a_im"])
    dt_c = col(jnp.broadcast_to(jnp.exp(p["ssm_log_step"])[:, :, None], (2, n_groups, SSM_P)))
    b_re_c, b_im_c = p["ssm_b_re"].reshape(n_col, SSM_C), p["ssm_b_im"].reshape(n_col, SSM_C)
    pw_re, pw_im, cf_re, cf_im, bb_re, bb_im = _s5_discretize(a_re_c, a_im_c, dt_c, b_re_c, b_im_c)
    n_state = n_groups * SSM_P
    pw_re = pw_re.reshape(2, n_state, SUBLANES).transpose(0, 2, 1)
    pw_im = pw_im.reshape(2, n_state, SUBLANES).transpose(0, 2, 1)
    bb_re4, bb_im4 = bb_re.reshape(2, n_groups, SSM_P, SSM_C), bb_im.reshape(2, n_groups, SSM_P, SSM_C)
    c_re, c_im = p["ssm_c_re"], p["ssm_c_im"]
    t21 = lambda a: a.transpose(0, 2, 1)
    maps_in = _block_diag_in([m for d in range(2) for m in (t21(bb_re4[d]), t21(bb_im4[d]), c_re[d], -c_im[d])])
    maps_out = _block_diag_in([m for d in range(2) for m in (t21(c_re[d]), -t21(c_im[d]), bb_re4[d], bb_im4[d])])
    fwd, bwd_in = [], []
    got_b = None
    for d in range(2):
        rev = d == 1
        tabs = _scan_tables(pw_re[d], pw_im[d], rev)
        if d == 0:
            ride = _gather_first([shards["w_glu"], shards["w_out"], shards["w_ffn_down"]])
        else:
            ride = _gather_second(got_a + got_b)
        (xs_re, xs_im, y_d), got = _s5_scan(u, maps_in[4 * d], maps_in[4 * d + 1], tabs, maps_out[4 * d],
                                            maps_out[4 * d + 1], reverse=rev, name=f"s5_fwd_{d}", ride=ride)
        if d == 0:
            got_b = got
        fwd.append((xs_re, xs_im, y_d))
        bwd_in.append((maps_in[4 * d + 2], maps_in[4 * d + 3], _scan_tables(pw_re[d], -pw_im[d], not rev),
                       maps_out[4 * d + 2], maps_out[4 * d + 3]))
    w["w_gate"], w["w_up"], w_glu_full, w_out_full, w["w_down"] = got
    w["w_glu"] = w_glu_full.reshape(-1, s_width)
    w["w_out"] = w_out_full.reshape(-1, d_model)

    ypre, yg = _rowwise(lambda y0, y1, uv, dsk: s5_mid(y0, y1, uv, dsk), [fwd[0][2], fwd[1][2], u], [p["ssm_d"]],
                        [(s_width, F32), (s_width, F32)], name="s5_skip_gelu")
    t_glu = _mm_plain(yg, w["w_glu"], "nn", name="glu_proj", tn=s_width)
    y_cat = _rowwise(mix_out_fwd, [ya, yg, t_glu], [p["b_glu"], p["g_out_attn"], p["g_out_ssm"]],
                     [(a_width + s_width, BF16)], name="mix_out")[0]
    x1 = _mm_plain(y_cat, w["w_out"], "nn", name="out_proj", res=x, tn=2048)

    h2 = _rowwise(lambda xv, g: x_norm(xv, g), [x1], [p["g_ffn"]], [(d_model, BF16)], name="rms_ffn")[0]
    ffn_up = functools.partial(
        _mm, grid=(n2, n_sh), a_spec=pl.BlockSpec((t2, d_model), lambda i, j: (i, 0)),
        b_spec=pl.BlockSpec((None, d_model, f_sh), lambda i, j: (j, 0, 0)),
        o_spec=pl.BlockSpec((None, t2, f_sh), lambda i, j: (j, i, 0)), o_shape=(n_sh, seq, f_sh), dims="nn",
        out_dtype=BF16)
    gate = ffn_up(h2, w["w_gate"], name="ffn_gate")
    up = ffn_up(h2, w["w_up"], name="ffn_up")
    flat = lambda a: a.reshape(n_sh * seq, f_sh)
    act = _rowwise(swiglu_fwd, [flat(gate), flat(up)], [], [(f_sh, BF16)], name="swiglu",
                   tm=1024)[0].reshape(n_sh, seq, f_sh)
    ffn_out = _mm(act, w["w_down"], name="ffn_down", grid=(n1, n_sh),
                  a_spec=pl.BlockSpec((None, t1, f_sh), lambda i, j: (j, i, 0)),
                  b_spec=pl.BlockSpec((None, f_sh, d_model), lambda i, j: (j, 0, 0)),
                  o_spec=pl.BlockSpec((t1, d_model), lambda i, j: (i, 0)), o_shape=(seq, d_model), dims="nn",
                  k_axis=1)

    dx2, dx2_b, sq = _rowwise(functools.partial(loss_head, inv_d=1.0 / d_model), [ffn_out, x1, target], [],
                              [(d_model, F32), (d_model, BF16)], [d_model], name="loss_head")
    loss = 0.5 * jnp.sum(sq) / d_model

    d_act = _mm(dx2_b, w["w_down"], name="ffn_down_dx", grid=(n2, n_sh),
                a_spec=pl.BlockSpec((t2, d_model), lambda i, j: (i, 0)),
                b_spec=pl.BlockSpec((None, f_sh, d_model), lambda i, j: (j, 0, 0)),
                o_spec=pl.BlockSpec((None, t2, f_sh), lambda i, j: (j, i, 0)), o_shape=(n_sh, seq, f_sh), dims="nt",
                out_dtype=BF16)
    g_w_down = _mm(act, dx2_b, name="ffn_down_dw", grid=(n_sh, n1),
                   a_spec=pl.BlockSpec((None, t1, f_sh), lambda j, k: (j, k, 0)),
                   b_spec=pl.BlockSpec((t1, d_model), lambda j, k: (k, 0)),
                   o_spec=pl.BlockSpec((None, f_sh, d_model), lambda j, k: (j, 0, 0)),
                   o_shape=(n_sh, f_sh, d_model), dims="tn", k_axis=1)
    d_gate, d_up = _rowwise(swiglu_bwd, [flat(d_act), flat(gate), flat(up)], [], [(f_sh, BF16), (f_sh, BF16)],
                            name="swiglu_bwd", tm=1024)
    d_gate, d_up = d_gate.reshape(n_sh, seq, f_sh), d_up.reshape(n_sh, seq, f_sh)
    d_h2 = _mm(d_gate, w["w_gate"], second=(d_up, w["w_up"]), name="ffn_up_gate_dx", grid=(n1, n_sh),
               a_spec=pl.BlockSpec((None, t1, f_sh), lambda i, j: (j, i, 0)),
               b_spec=pl.BlockSpec((None, d_model, f_sh), lambda i, j: (j, 0, 0)),
               o_spec=pl.BlockSpec((t1, d_model), lambda i, j: (i, 0)), o_shape=(seq, d_model), dims="nt", k_axis=1)
    ffn_dw = functools.partial(
        _mm, grid=(n_sh, n1), a_spec=pl.BlockSpec((t1, d_model), lambda j, k: (k, 0)),
        b_spec=pl.BlockSpec((None, t1, f_sh), lambda j, k: (j, k, 0)),
        o_spec=pl.BlockSpec((None, d_model, f_sh), lambda j, k: (j, 0, 0)), o_shape=(n_sh, d_model, f_sh), dims="tn",
        k_axis=1)
    g_w_gate = ffn_dw(h2, d_gate, name="ffn_gate_dw")
    g_w_up = ffn_dw(h2, d_up, name="ffn_up_dw")
    dx1, g_g_ffn = _rowwise(residual_rms_bwd, [dx2, d_h2, x1], [p["g_ffn"]], [(d_model, F32)], [d_model],
                            name="rms_ffn_bwd")

    d_ycat = _mm_plain(dx1, w["w_out"], "nt", name="out_proj_dx", tn=2048)
    g_w_out = _mm_plain(y_cat, dx1, "tn", name="out_proj_dw", tm=1024, tn=2048)
    (d_ya, d_yg_direct, d_t, g_goa, g_gos, g_b_glu) = _rowwise(
        functools.partial(mix_out_bwd, a_width=a_width), [d_ycat, ya, yg, t_glu],
        [p["b_glu"], p["g_out_attn"], p["g_out_ssm"]],
        [(a_width, F32), (s_width, F32), (s_width, BF16)], [a_width, s_width, s_width], name="mix_out_bwd")
    d_yg = _mm_plain(d_t, w["w_glu"], "nt", name="glu_proj_dx", res=d_yg_direct, tn=s_width)
    g_w_glu = _mm_plain(yg, d_t, "tn", name="glu_proj_dw", tm=1024, tn=1024)
    d_ypre, du_skip, g_ssm_d = _rowwise(gelu_skip_bwd, [d_yg, ypre, u], [p["ssm_d"]],
                                        [(s_width, F32), (s_width, F32)], [s_width], name="s5_skip_gelu_bwd")

    ffn_names, mix_names = ("w_ffn_gate", "w_ffn_up", "w_ffn_down"), ("w_glu", "w_out")
    ffn_slabs = [slab3(g, n) for g, n in zip((g_w_gate, g_w_up, g_w_down), ffn_names)]
    mix_slabs = [slab3(g, n) for g, n in zip((g_w_glu, g_w_out), mix_names)]
    du_dirs, adj, r_parts, gb_parts, gc_parts = [], [], [], [], []
    sib, part = {}, {}
    for d, (names, slabs) in enumerate(((ffn_names, ffn_slabs), (mix_names, mix_slabs))):
        win_re, win_im, tabs, wo_re, wo_im = bwd_in[d]
        (as_re, as_im, du_d), got = _s5_scan(d_ypre, win_re, win_im, tabs, wo_re, wo_im, reverse=(d == 0),
                                             name=f"s5_bwd_{d}", ride=_reduce_sibling(slabs))
        du_dirs.append(du_d)
        adj.append((as_re, as_im))
        sib[names] = got
        part[names] = _partial_sums(slabs, got, names)
    for d in range(2):
        (r_re, r_im, gbt_re, gbt_im, gct_re, gct_im), got = _s5_reduce(
            fwd[d][0], fwd[d][1], adj[d][0], adj[d][1], u, d_ypre, name=f"s5_reduce_{d}",
            ride=_reduce_chips(part[mix_names]) if d == 0 else None)
        if d == 0:
            mix_chips = got
        r_parts.append((r_re.reshape(n_state, 1), r_im.reshape(n_state, 1)))
        gb_parts.append((_block_diag_take(gbt_re, SSM_C, SSM_P), _block_diag_take(gbt_im, SSM_C, SSM_P)))
        gc_parts.append((_block_diag_take(gct_re, SSM_C, SSM_P), _block_diag_take(gct_im, SSM_C, SSM_P)))
    cat = lambda i, parts: jnp.concatenate([parts[0][i], parts[1][i]], axis=0)
    gbb_re, gbb_im = cat(0, gb_parts).reshape(n_col, SSM_C), cat(1, gb_parts).reshape(n_col, SSM_C)
    g_a_re, g_a_im, g_ls, g_b_re, g_b_im = _s5_param_grads(
        a_re_c, a_im_c, dt_c, b_re_c, b_im_c, pw_re[:, 0].reshape(n_col, 1), pw_im[:, 0].reshape(n_col, 1),
        cf_re, cf_im, bb_re, bb_im, cat(0, r_parts), cat(1, r_parts), gbb_re, gbb_im)
    g_c_re = cat(0, gc_parts).reshape(2, n_groups, SSM_P, SSM_C).transpose(0, 1, 3, 2)
    g_c_im = -cat(1, gc_parts).reshape(2, n_groups, SSM_P, SSM_C).transpose(0, 1, 3, 2)

    (d_q, d_k, d_v, d_btab, g_qg, g_kg), ffn_chips = _attn_bwd(z, d_ya, qg4, kg4, btab,
                                                                ride=_reduce_chips(part[ffn_names]))
    d_u = _rowwise(lambda a, b, c: a + b + c, [du_dirs[0], du_dirs[1], du_skip], [], [(s_width, BF16)],
                   name="s5_du_sum")[0]
    d_z = jnp.concatenate([d_q, d_k, d_v, d_u], axis=1)
    d_h1 = _mm(d_z, w["w_in"], name="in_proj_dx", grid=(n1, n_sh),
               a_spec=pl.BlockSpec((t1, in_sh), lambda i, j: (i, j)),
               b_spec=pl.BlockSpec((None, d_model, in_sh), lambda i, j: (j, 0, 0)),
               o_spec=pl.BlockSpec((t1, d_model), lambda i, j: (i, 0)), o_shape=(seq, d_model), dims="nt", k_axis=1)
    g_w_in = _mm(h1, d_z, name="in_proj_dw", grid=(n_sh, n1),
                 a_spec=pl.BlockSpec((t1, d_model), lambda j, k: (k, 0)),
                 b_spec=pl.BlockSpec((t1, in_sh), lambda j, k: (k, j)),
                 o_spec=pl.BlockSpec((None, d_model, in_sh), lambda j, k: (j, 0, 0)),
                 o_shape=(n_sh, d_model, in_sh), dims="tn", k_axis=1)
    grad_x, g_g_mix = _rowwise(residual_rms_bwd, [dx1, d_h1, x], [p["g_mix"]], [(d_model, F32)], [d_model],
                               name="rms_mix_bwd")

    fold_heads = lambda g: g.reshape(n_heads, HEAD_DIM).sum(axis=0, keepdims=True)
    small = {
        "g_mix": g_g_mix, "q_gain": fold_heads(g_qg), "k_gain": fold_heads(g_kg),
        "rpb": _bias_grad(d_btab, n_heads),
        "ssm_a_re": g_a_re.reshape(2, n_groups, SSM_P), "ssm_a_im": g_a_im.reshape(2, n_groups, SSM_P),
        "ssm_b_re": g_b_re.reshape(2, n_groups, SSM_P, SSM_C), "ssm_b_im": g_b_im.reshape(2, n_groups, SSM_P, SSM_C),
        "ssm_c_re": g_c_re, "ssm_c_im": g_c_im,
        "ssm_log_step": g_ls.reshape(2, n_groups, SSM_P).sum(axis=-1),
        "ssm_d": g_ssm_d, "b_glu": g_b_glu, "g_out_attn": g_goa, "g_out_ssm": g_gos, "g_ffn": g_g_ffn,
    }
    reduced = {}
    for names, slabs, chips in ((ffn_names, ffn_slabs, ffn_chips), (mix_names, mix_slabs, mix_chips)):
        for i, n in enumerate(names):
            reduced[n] = (slabs[i], sib[names][i], chips[i])
    return loss, grad_x, small, g_w_in, reduced


def x_norm(xv, g):
    return xv * _rstd(xv) * g


def s5_mid(y0, y1, uv, d_skip):
    ypre = y0 + y1 + d_skip * uv
    return ypre, _gelu(ypre)


def mix_out_fwd(ya, yg, t, b_glu, g_oa, g_os):
    ys = yg * _sigmoid(t + b_glu)
    return jnp.concatenate([ya * _rstd(ya) * g_oa, ys * _rstd(ys) * g_os], axis=1)


def mix_out_bwd(d_y, ya, yg, t, b_glu, g_oa, g_os, *, a_width):
    sg = _sigmoid(t + b_glu)
    ys = yg * sg
    d_ya, c_goa = _rms_bwd(d_y[:, :a_width], ya, g_oa)
    d_ys, c_gos = _rms_bwd(d_y[:, a_width:], ys, g_os)
    d_t = d_ys * yg * sg * (1.0 - sg)
    return d_ya, d_ys * sg, d_t, c_goa, c_gos, d_t


def gelu_skip_bwd(d_yg, ypre, uv, d_skip):
    d_ypre = d_yg * _gelu_grad(ypre)
    return d_ypre, d_ypre * d_skip, d_ypre * uv


def swiglu_fwd(gv, uv):
    gv, uv = gv.astype(F32), uv.astype(F32)
    return gv * _sigmoid(gv) * uv


def swiglu_bwd(d_act, gv, uv):
    d_act, gv, uv = d_act.astype(F32), gv.astype(F32), uv.astype(F32)
    sg = _sigmoid(gv)
    return d_act * uv * (sg * (1.0 + gv * (1.0 - sg))), d_act * gv * sg


def loss_head(ffn_out, x1, target, *, inv_d):
    diff = ffn_out + x1 - target
    return diff * inv_d, diff * inv_d, diff * diff


def residual_rms_bwd(d_res, d_h, xv, g):
    dx, c_g = _rms_bwd(d_h, xv, g)
    return d_res + dx, c_g


_ANY = pl.BlockSpec(memory_space=pl.ANY)


def _mesh_place():
    return lax.axis_index("x"), lax.axis_index("y"), lax.axis_index("c")


def _chips(x, y):
    return [(x, y), (1 - x, y), (x, 1 - y), (1 - x, 1 - y)]


def _slab(px, py, pc):
    return 4 * px + 2 * py + pc


def _all_gather(arrs, *, name):
    n = len(arrs)

    def body(*refs):
        in_refs, out_refs = refs[:n], refs[n:2 * n]
        send_sems, recv_sems, local_sems = refs[2 * n:]
        x, y, c = _mesh_place()
        me, sibling = (x, y, c), (x, y, 1 - c)
        others = _chips(x, y)[1:]

        def copy(w, k, block, to, src=None):
            dst = out_refs[w].at[_slab(*block)]
            return pltpu.make_async_remote_copy(
                src_ref=dst if src is None else src, dst_ref=dst, send_sem=send_sems.at[7 * w + k],
                recv_sem=recv_sems.at[7 * w + k], device_id=to, device_id_type=MESH)

        mine = [pltpu.make_async_copy(in_refs[w], out_refs[w].at[_slab(*me)], local_sems.at[w]) for w in range(n)]
        first = []
        for w in range(n):
            mine[w].start()
            first.append(copy(w, 0, me, sibling, src=in_refs[w]))
            first += [copy(w, 1 + j, me, (*chip, c), src=in_refs[w]) for j, chip in enumerate(others)]
        for cp in first:
            cp.start()
        passed = []
        for j, chip in enumerate(others):
            for w in range(n):
                copy(w, 1 + j, (*chip, c), me).wait_recv()
                fwd = copy(w, 4 + j, (*chip, c), sibling)
                fwd.start()
                passed.append(fwd)
        for w in range(n):
            copy(w, 0, sibling, me).wait_recv()
        for j, chip in enumerate(others):
            for w in range(n):
                copy(w, 4 + j, (*chip, 1 - c), me).wait_recv()
        for cp in first + passed:
            cp.wait_send()
        for cp in mine:
            cp.wait()

    return pl.pallas_call(
        body, name=name, in_specs=[_ANY] * n, out_specs=[_ANY] * n,
        out_shape=[jax.ShapeDtypeStruct((N_DEV,) + a.shape, a.dtype) for a in arrs],
        scratch_shapes=[pltpu.SemaphoreType.DMA((7 * n,)), pltpu.SemaphoreType.DMA((7 * n,)),
                        pltpu.SemaphoreType.DMA((n,))],
        compiler_params=pltpu.CompilerParams(has_side_effects=True),
    )(*arrs)


def _swap(arrs, n_out, plan, *, name):
    n = len(arrs)

    def body(*refs):
        in_refs, out_refs = refs[:n], refs[n:2 * n]
        send_sems, recv_sems = refs[2 * n:]
        copies = []
        for k in range(n_out):
            for w, (src, dst, to) in enumerate(plan(in_refs, out_refs, k)):
                copies.append(pltpu.make_async_remote_copy(
                    src_ref=src, dst_ref=dst, send_sem=send_sems.at[n_out * w + k],
                    recv_sem=recv_sems.at[n_out * w + k], device_id=to, device_id_type=MESH))
        for cp in copies:
            cp.start()
        for cp in copies:
            cp.wait_recv()
        for cp in copies:
            cp.wait_send()

    return pl.pallas_call(
        body, name=name, in_specs=[_ANY] * n, out_specs=[_ANY] * n,
        out_shape=[jax.ShapeDtypeStruct((n_out,) + a.shape[1:], a.dtype) for a in arrs],
        scratch_shapes=[pltpu.SemaphoreType.DMA((n_out * n,)), pltpu.SemaphoreType.DMA((n_out * n,))],
        compiler_params=pltpu.CompilerParams(has_side_effects=True),
    )(*arrs)


def _sibling_exchange(grads):
    def plan(in_refs, out_refs, k):
        x, y, c = _mesh_place()
        px, py = _chips(x, y)[k]
        return [(g.at[_slab(px, py, 1 - c)], o.at[k], (x, y, 1 - c)) for g, o in zip(in_refs, out_refs)]

    return _swap(grads, 4, plan, name="reduce_sibling")


def _chip_exchange(partials):
    def plan(in_refs, out_refs, k):
        x, y, c = _mesh_place()
        px, py = _chips(x, y)[k + 1]
        return [(g.at[k], o.at[k], (px, py, c)) for g, o in zip(in_refs, out_refs)]

    return _swap(partials, 3, plan, name="reduce_chips")


def _adamw(w, m, v, parts, *, name, slab, tr=256):
    rows, cols = w.shape
    tr = _tile(rows, tr)
    n_p = len(parts)

    def body(slab_ref, *refs):
        w_ref, m_ref, v_ref = refs[:3]
        p_refs = refs[3:3 + n_p]
        g_ref, d_ref, nm_ref, nv_ref = refs[3 + n_p:]
        g = None
        for (_, lead), r in zip(parts, p_refs):
            for piece in ([r[...]] if lead is None else [r[i] for i in range(lead)]):
                g = piece.astype(F32) if g is None else g + piece.astype(F32)
        new_m = ADAM_B1 * m_ref[...] + (1.0 - ADAM_B1) * g
        new_v = ADAM_B2 * v_ref[...] + (1.0 - ADAM_B2) * (g * g)
        m_hat = new_m / (1.0 - ADAM_B1 ** ADAM_STEP)
        v_hat = new_v / (1.0 - ADAM_B2 ** ADAM_STEP)
        g_ref[...] = g
        d_ref[...] = -ADAM_LR * (m_hat / (jnp.sqrt(v_hat) + ADAM_EPS) + ADAM_WD * w_ref[...])
        nm_ref[...] = new_m
        nv_ref[...] = new_v

    tile = pl.BlockSpec((tr, cols), lambda i, s: (i, 0))
    p_specs = [pl.BlockSpec((None, tr, cols), lambda i, s: (s[0], i, 0)) if lead is None
               else pl.BlockSpec((lead, tr, cols), lambda i, s: (0, i, 0)) for _, lead in parts]
    return pl.pallas_call(
        body, name=name,
        grid_spec=pltpu.PrefetchScalarGridSpec(num_scalar_prefetch=1, grid=(rows // tr,),
                                               in_specs=[tile] * 3 + p_specs, out_specs=[tile] * 4),
        out_shape=[jax.ShapeDtypeStruct((rows, cols), F32)] * 4, compiler_params=_params(),
    )(jnp.reshape(slab, (1,)).astype(jnp.int32), w, m, v, *[a for a, _ in parts])


_PACK_TILE = SUBLANES * 128
_PACK_ROWS = 512


def _pack(arrs):
    flat = []
    for a in arrs:
        f = a.reshape(-1)
        flat.append(jnp.pad(f, (0, (-f.shape[0]) % _PACK_TILE)))
    total = sum(f.shape[0] for f in flat)
    flat.append(jnp.zeros(((-total) % (_PACK_ROWS * 128),), F32))
    return jnp.concatenate(flat).reshape(-1, 128)


def _unpack(buf, shapes):
    out, at = [], 0
    flat = buf.reshape(-1)
    for s in shapes:
        n = math.prod(s)
        out.append(flat[at:at + n].reshape(s))
        at += n + (-n) % _PACK_TILE
    return out


BIG = ("w_in", "w_glu", "w_out", "w_ffn_gate", "w_ffn_up", "w_ffn_down")
WEIGHTS = ("g_mix", "w_in", "q_gain", "k_gain", "rpb", "ssm_a_re", "ssm_a_im", "ssm_b_re", "ssm_b_im", "ssm_c_re",
           "ssm_c_im", "ssm_log_step", "ssm_d", "w_glu", "b_glu", "g_out_attn", "g_out_ssm", "w_out", "g_ffn",
           "w_ffn_gate", "w_ffn_up", "w_ffn_down")
SMALL = tuple(n for n in WEIGHTS if n not in BIG)
VECTORS = ("g_mix", "q_gain", "k_gain", "ssm_d", "b_glu", "g_out_attn", "g_out_ssm", "g_ffn")


def kernel(x, g_mix, w_in, q_gain, k_gain, rpb, ssm_a_re, ssm_a_im, ssm_b_re, ssm_b_im, ssm_c_re, ssm_c_im, ssm_log_step, ssm_d, w_glu, b_glu, g_out_attn, g_out_ssm, w_out, g_ffn, w_ffn_gate, w_ffn_up, w_ffn_down, loss_target, m_g_mix, m_w_in, m_q_gain, m_k_gain, m_rpb, m_ssm_a_re, m_ssm_a_im, m_ssm_b_re, m_ssm_b_im, m_ssm_c_re, m_ssm_c_im, m_ssm_log_step, m_ssm_d, m_w_glu, m_b_glu, m_g_out_attn, m_g_out_ssm, m_w_out, m_g_ffn, m_w_ffn_gate, m_w_ffn_up, m_w_ffn_down, v_g_mix, v_w_in, v_q_gain, v_k_gain, v_rpb, v_ssm_a_re, v_ssm_a_im, v_ssm_b_re, v_ssm_b_im, v_ssm_c_re, v_ssm_c_im, v_ssm_log_step, v_ssm_d, v_w_glu, v_b_glu, v_g_out_attn, v_g_out_ssm, v_w_out, v_g_ffn, v_w_ffn_gate, v_w_ffn_up, v_w_ffn_down):
    wts = dict(g_mix=g_mix, w_in=w_in, q_gain=q_gain, k_gain=k_gain, rpb=rpb, ssm_a_re=ssm_a_re, ssm_a_im=ssm_a_im,
               ssm_b_re=ssm_b_re, ssm_b_im=ssm_b_im, ssm_c_re=ssm_c_re, ssm_c_im=ssm_c_im, ssm_log_step=ssm_log_step,
               ssm_d=ssm_d, w_glu=w_glu, b_glu=b_glu, g_out_attn=g_out_attn, g_out_ssm=g_out_ssm, w_out=w_out,
               g_ffn=g_ffn, w_ffn_gate=w_ffn_gate, w_ffn_up=w_ffn_up, w_ffn_down=w_ffn_down)
    mom = dict(g_mix=m_g_mix, w_in=m_w_in, q_gain=m_q_gain, k_gain=m_k_gain, rpb=m_rpb, ssm_a_re=m_ssm_a_re,
               ssm_a_im=m_ssm_a_im, ssm_b_re=m_ssm_b_re, ssm_b_im=m_ssm_b_im, ssm_c_re=m_ssm_c_re,
               ssm_c_im=m_ssm_c_im, ssm_log_step=m_ssm_log_step, ssm_d=m_ssm_d, w_glu=m_w_glu, b_glu=m_b_glu,
               g_out_attn=m_g_out_attn, g_out_ssm=m_g_out_ssm, w_out=m_w_out, g_ffn=m_g_ffn,
               w_ffn_gate=m_w_ffn_gate, w_ffn_up=m_w_ffn_up, w_ffn_down=m_w_ffn_down)
    var = dict(g_mix=v_g_mix, w_in=v_w_in, q_gain=v_q_gain, k_gain=v_k_gain, rpb=v_rpb, ssm_a_re=v_ssm_a_re,
               ssm_a_im=v_ssm_a_im, ssm_b_re=v_ssm_b_re, ssm_b_im=v_ssm_b_im, ssm_c_re=v_ssm_c_re,
               ssm_c_im=v_ssm_c_im, ssm_log_step=v_ssm_log_step, ssm_d=v_ssm_d, w_glu=v_w_glu, b_glu=v_b_glu,
               g_out_attn=v_g_out_attn, g_out_ssm=v_g_out_ssm, w_out=v_w_out, g_ffn=v_g_ffn,
               w_ffn_gate=v_w_ffn_gate, w_ffn_up=v_w_ffn_up, w_ffn_down=v_w_ffn_down)
    ix, iy, ic = _mesh_place()
    me = _slab(ix, iy, ic)
    d_model = x.shape[-1]

    shard = {n: wts[n][0] for n in BIG}
    shard_b = {n: shard[n].astype(BF16) for n in BIG}
    w_in_full = _all_gather([shard_b["w_in"]], name="gather_w_in")[0]
    p = {n: (wts[n][0].reshape(1, -1) if n in VECTORS else wts[n][0]) for n in SMALL}

    loss, grad_x, g_small, g_w_in, reduced = _local_step(x[0], loss_target[0], p, w_in_full,
                                                         {n: shard_b[n] for n in BIG if n != "w_in"})
    loss = lax.psum(loss, ("x", "y", "c"))

    in_slabs = [g_w_in]
    in_sibling = _sibling_exchange(in_slabs)
    in_chips = _chip_exchange(_partial_sums(in_slabs, in_sibling, ("w_in",)))
    reduced["w_in"] = (in_slabs[0], in_sibling[0], in_chips[0])
    out = {}
    for n in BIG:
        slabs, from_sibling, from_chips = reduced[n]
        rows, cols = slabs.shape[1:]
        res = _adamw(shard[n].reshape(rows, cols), mom[n][0].reshape(rows, cols), var[n][0].reshape(rows, cols),
                     [(slabs, None), (from_sibling, 1), (from_chips, 3)], name=f"adamw_{n}", slab=me)
        out[n] = [r.reshape(wts[n].shape) for r in res]

    order = list(SMALL)
    shapes = [wts[n].shape for n in order]
    packed = _pack([g_small[n] for n in order])
    gathered = _all_gather([packed], name="gather_small_grads")[0]
    res = _adamw(_pack([wts[n] for n in order]), _pack([mom[n] for n in order]), _pack([var[n] for n in order]),
                 [(gathered, N_DEV)], name="adamw_small", slab=me)
    for kind, buf in enumerate(res):
        for n, a in zip(order, _unpack(buf, shapes)):
            out.setdefault(n, [None] * 4)[kind] = a

    return (loss, grad_x[None], *[out[n][0] for n in WEIGHTS], *[out[n][1] for n in WEIGHTS],
            *[out[n][2] for n in WEIGHTS], *[out[n][3] for n in WEIGHTS])
```

```python
import functools
import math

import jax
import jax.numpy as jnp
from jax import lax
from jax.experimental import pallas as pl
from jax.experimental.pallas import tpu as pltpu

F32 = jnp.float32
BF16 = jnp.bfloat16

N_DEV = 8
GRID_W = 64
WIN_H = 8
WIN_W = 16
HEAD_DIM = 64
HEADS_PER_GROUP = 4
GROUP_LANES = HEADS_PER_GROUP * HEAD_DIM
SSM_C = 16
SSM_P = 64
GROUPS_PER_TILE = 8
U_TILE = GROUPS_PER_TILE * SSM_C
ST_TILE = GROUPS_PER_TILE * SSM_P
SUBLANES = 8
RMS_EPS = 1e-6
NEG_INF = -1e30
A_RE_MAX = -1e-4
ADAM_LR, ADAM_B1, ADAM_B2, ADAM_EPS, ADAM_WD, ADAM_STEP = 0.001, 0.9, 0.999, 1e-08, 0.01, 10
VMEM_LIMIT_V7X = 56 * 1024 * 1024
MESH = pl.DeviceIdType.MESH

_NN = (((1,), (0,)), ((), ()))
_NT = (((1,), (1,)), ((), ()))
_TN = (((0,), (0,)), ((), ()))
_DIMS = {"nn": _NN, "nt": _NT, "tn": _TN}


def _params(**kw):
    return pltpu.CompilerParams(vmem_limit_bytes=VMEM_LIMIT_V7X, **kw)


def _dot(a, b, dims=_NN):
    return lax.dot_general(a, b, dims, preferred_element_type=F32)


def _mm(a, b, *, name, grid, a_spec, b_spec, o_spec, o_shape, dims, k_axis=None, res=None, out_dtype=F32,
        exact=False, second=None, ride=None):
    dn = _DIMS[dims]
    nk = 1 if k_axis is None else grid[k_axis]
    acc_shape = tuple(d for d in o_spec.block_shape if d is not None)
    n_in = 2 + (2 if second is not None else 0)

    def body(*refs):
        a_ref, b_ref = refs[:2]
        r_ref = refs[n_in] if res is not None else None
        o_ref, acc = refs[-2:]
        if exact:
            p = lax.dot_general(a_ref[...], b_ref[...], dn, precision=lax.Precision.HIGHEST,
                                preferred_element_type=F32)
        else:
            p = _dot(a_ref[...].astype(BF16), b_ref[...].astype(BF16), dn)
        if second is not None:
            p = p + _dot(refs[2][...].astype(BF16), refs[3][...].astype(BF16), dn)

        def finish(v):
            if r_ref is not None:
                v = v + r_ref[...].astype(F32)
            o_ref[...] = v.astype(out_dtype)

        if nk == 1:
            finish(p)
        else:
            k = pl.program_id(k_axis)

            @pl.when(k == 0)
            def _():
                acc[...] = p

            @pl.when(k > 0)
            def _():
                acc[...] += p

            @pl.when(k == nk - 1)
            def _():
                finish(acc[...])

    ins = [a, b] + (list(second) if second is not None else []) + ([res] if res is not None else [])
    in_specs = [a_spec, b_spec] * (n_in // 2) + ([o_spec] if res is not None else [])
    first, last = _grid_ends(grid)
    (out,), rode = _call(
        body, name=name, grid=grid, in_specs=in_specs, out_specs=[o_spec],
        out_shape=[jax.ShapeDtypeStruct(o_shape, out_dtype)],
        scratch_shapes=[pltpu.VMEM(acc_shape if nk > 1 else (SUBLANES, 128), F32)],
        args=ins, ride=ride, first=first, last=last)
    return out if ride is None else (out, rode)


def _tile(n, want):
    if n <= want:
        return n
    t = want
    while n % t:
        t //= 2
    return t


def _mm_plain(a, b, dims, *, name, res=None, out_dtype=F32, tm=512, tn=512, tk=512, exact=False):
    if dims == "nn":
        (m, k), n = a.shape, b.shape[1]
    elif dims == "nt":
        (m, k), n = a.shape, b.shape[0]
    else:
        (k, m), n = a.shape, b.shape[1]
    tm, tn = _tile(m, tm), _tile(n, tn)
    if dims == "tn":
        tk = _tile(k, tk)
        grid = (m // tm, n // tn, k // tk)
        a_spec = pl.BlockSpec((tk, tm), lambda i, j, kk: (kk, i))
        b_spec = pl.BlockSpec((tk, tn), lambda i, j, kk: (kk, j))
        o_spec = pl.BlockSpec((tm, tn), lambda i, j, kk: (i, j))
        return _mm(a, b, name=name, grid=grid, a_spec=a_spec, b_spec=b_spec, o_spec=o_spec, o_shape=(m, n),
                   dims=dims, k_axis=2, res=res, out_dtype=out_dtype)
    grid = (n // tn, m // tm)
    a_spec = pl.BlockSpec((tm, k), lambda j, i: (i, 0))
    if dims == "nn":
        b_spec = pl.BlockSpec((k, tn), lambda j, i: (0, j))
    else:
        b_spec = pl.BlockSpec((tn, k), lambda j, i: (j, 0))
    o_spec = pl.BlockSpec((tm, tn), lambda j, i: (i, j))
    return _mm(a, b, name=name, grid=grid, a_spec=a_spec, b_spec=b_spec, o_spec=o_spec, o_shape=(m, n), dims=dims,
               res=res, out_dtype=out_dtype, exact=exact)


def _rowwise(fn, tiled, bcast, outs, accs=(), *, name, tm=256, flipped=(), ride=None):
    m = tiled[0].shape[0]
    tm = _tile(m, tm)
    n_t, n_b, n_o, n_f = len(tiled), len(bcast), len(outs), len(flipped)

    def body(*refs):
        ins = [r[...] for r in refs[: n_t + n_b]]
        o_refs = refs[n_t + n_b: n_t + n_b + n_o]
        f_refs = refs[n_t + n_b + n_o: n_t + n_b + n_o + n_f]
        a_refs = refs[n_t + n_b + n_o + n_f:]
        res = fn(*ins)
        if not isinstance(res, (tuple, list)):
            res = (res,)
        for r, v in zip(o_refs, res[:n_o]):
            r[...] = v.astype(r.dtype)
        for r, v in zip(f_refs, res[n_o:n_o + n_f]):
            r[...] = v.astype(F32).T.astype(r.dtype)
        first = pl.program_id(0) == 0
        for r, v in zip(a_refs, res[n_o + n_f:]):
            s = jnp.sum(v, axis=0, keepdims=True)

            @pl.when(first)
            def _():
                r[...] = s

            @pl.when(jnp.logical_not(first))
            def _():
                r[...] += s

    in_specs = [pl.BlockSpec((tm, t.shape[1]), lambda i: (i, 0)) for t in tiled]
    in_specs += [pl.BlockSpec(b.shape, lambda i, nd=b.ndim: (0,) * nd) for b in bcast]
    out_specs = [pl.BlockSpec((tm, n), lambda i: (i, 0)) for n, _ in outs]
    out_specs += [pl.BlockSpec((n, tm), lambda i, per=m // tm // g: (i // per, i % per)) for n, _, g in flipped]
    out_specs += [pl.BlockSpec((1, n), lambda i: (0, 0)) for n in accs]
    out_shape = [jax.ShapeDtypeStruct((m, n), dt) for n, dt in outs]
    out_shape += [jax.ShapeDtypeStruct((g * n, m // g), dt) for n, dt, g in flipped]
    out_shape += [jax.ShapeDtypeStruct((1, n), F32) for n in accs]
    first, last = _grid_ends((m // tm,))
    res, rode = _call(body, name=name, grid=(m // tm,), in_specs=in_specs, out_specs=out_specs, out_shape=out_shape,
                      scratch_shapes=[], args=list(tiled) + list(bcast), ride=ride, first=first, last=last)
    return res if ride is None else (res, rode)


def _rstd(x):
    return lax.rsqrt(jnp.mean(x * x, axis=-1, keepdims=True) + RMS_EPS)


def _rms_bwd(dh, x, g):
    xh = x * _rstd(x)
    dxh = dh * g
    dx = _rstd(x) * (dxh - xh * jnp.mean(dxh * xh, axis=-1, keepdims=True))
    return dx, dh * xh


def _sigmoid(x):
    return 1.0 / (1.0 + jnp.exp(-x))


_GELU_K = math.sqrt(2.0 / math.pi)
_GELU_C = 0.044715


def _gelu(x):
    return 0.5 * x * (1.0 + jnp.tanh(_GELU_K * (x + _GELU_C * x * x * x)))


def _gelu_grad(x):
    th = jnp.tanh(_GELU_K * (x + _GELU_C * x * x * x))
    return 0.5 * (1.0 + th) + 0.5 * x * (1.0 - th * th) * _GELU_K * (1.0 + 3.0 * _GELU_C * x * x)


class _Exchange:
    def __init__(self, arrays, outs, n_sems, sends, recvs=None, local=None, aliases=None):
        self.arrays, self.outs, self.n_sems = list(arrays), list(outs), n_sems
        self.sends, self.local, self.aliases = sends, local, aliases or {}
        self.recvs = recvs or (lambda i, o: [(k, dst) for k, _, dst, _ in sends(i, o)])

    def __add__(self, other):
        na, no, ns = len(self.arrays), len(self.outs), self.n_sems
        mine = lambda f: (lambda i, o: f(i[:na], o[:no]))
        shift = lambda f, at: (lambda i, o: [(k + ns,) + tuple(rest) for k, *rest in f(i[na:], o[no:])]) if at else None
        both = lambda f, g: (lambda i, o: f(i, o) + g(i, o))
        local = None
        if self.local or other.local:
            la = mine(self.local) if self.local else (lambda i, o: [])
            lb = (lambda i, o: other.local(i[na:], o[no:])) if other.local else (lambda i, o: [])
            local = both(la, lb)
        aliases = dict(self.aliases)
        aliases.update({na + i: no + o for i, o in other.aliases.items()})
        return _Exchange(self.arrays + other.arrays, self.outs + other.outs, ns + other.n_sems,
                         both(mine(self.sends), shift(other.sends, True)),
                         both(mine(self.recvs), shift(other.recvs, True)), local, aliases)

    def descriptors(self, in_refs, out_refs, send_sems, recv_sems, local_sems):
        me = _mesh_place()
        remote = lambda k, src, dst, to: pltpu.make_async_remote_copy(
            src_ref=src, dst_ref=dst, send_sem=send_sems.at[k], recv_sem=recv_sems.at[k], device_id=to,
            device_id_type=MESH)
        out = [remote(*s) for s in self.sends(in_refs, out_refs)]
        arrive = [remote(k, dst, dst, me) for k, dst in self.recvs(in_refs, out_refs)]
        own = [pltpu.make_async_copy(src, dst, local_sems.at[i])
               for i, (src, dst) in enumerate(self.local(in_refs, out_refs) if self.local else [])]
        return out, arrive, own

    def start(self, *refs):
        out, _, own = self.descriptors(*refs)
        for cp in own + out:
            cp.start()

    def finish(self, *refs):
        out, arrive, own = self.descriptors(*refs)
        for cp in arrive:
            cp.wait_recv()
        for cp in out:
            cp.wait_send()
        for cp in own:
            cp.wait()


def _call(body, *, name, grid, in_specs, out_specs, out_shape, scratch_shapes, args, ride=None, first=None, last=None):
    if ride is None:
        res = pl.pallas_call(body, name=name, grid=grid, in_specs=in_specs, out_specs=out_specs, out_shape=out_shape,
                             scratch_shapes=scratch_shapes, compiler_params=_params())(*args)
        return list(res), []
    n_in, n_out, n_scr = len(in_specs), len(out_specs), len(scratch_shapes)
    r_in, r_out = len(ride.arrays), len(ride.outs)

    def wrapped(*refs):
        ins, refs = refs[:n_in], refs[n_in:]
        x_in, refs = refs[:r_in], refs[r_in:]
        outs, refs = refs[:n_out], refs[n_out:]
        x_out, refs = refs[:r_out], refs[r_out:]
        scr, sems = refs[:n_scr], refs[n_scr:]

        @pl.when(first())
        def _():
            ride.start(x_in, x_out, *sems)

        body(*ins, *outs, *scr)

        @pl.when(last())
        def _():
            ride.finish(x_in, x_out, *sems)

    n_local = max(1, len(ride.arrays))
    res = pl.pallas_call(
        wrapped, name=name, grid=grid, in_specs=list(in_specs) + [_ANY] * r_in,
        out_specs=list(out_specs) + [_ANY] * r_out, out_shape=list(out_shape) + ride.outs,
        scratch_shapes=list(scratch_shapes) + [pltpu.SemaphoreType.DMA((ride.n_sems,)),
                                               pltpu.SemaphoreType.DMA((ride.n_sems,)),
                                               pltpu.SemaphoreType.DMA((n_local,))],
        input_output_aliases={n_in + i: n_out + o for i, o in ride.aliases.items()},
        compiler_params=_params(has_side_effects=True),
    )(*args, *ride.arrays)
    return list(res[:n_out]), list(res[n_out:])


def _gather_first(shards):
    def sends(i, o):
        x, y, c = _mesh_place()
        peers = [(x, y, 1 - c)] + [(px, py, c) for px, py in _chips(x, y)[1:]]
        return [(4 * w + k, i[w], o[w].at[_slab(x, y, c)], to) for w in range(len(i)) for k, to in enumerate(peers)]

    def recvs(i, o):
        x, y, c = _mesh_place()
        peers = [(x, y, 1 - c)] + [(px, py, c) for px, py in _chips(x, y)[1:]]
        return [(4 * w + k, o[w].at[_slab(*peer)]) for w in range(len(i)) for k, peer in enumerate(peers)]

    def local(i, o):
        return [(i[w], o[w].at[_slab(*_mesh_place())]) for w in range(len(i))]

    outs = [jax.ShapeDtypeStruct((N_DEV,) + a.shape, a.dtype) for a in shards]
    return _Exchange(shards, outs, 4 * len(shards), sends, recvs, local)


def _gather_second(gathered):
    def sends(i, o):
        x, y, c = _mesh_place()
        return [(3 * w + j, o[w].at[_slab(px, py, c)], o[w].at[_slab(px, py, c)], (x, y, 1 - c))
                for w in range(len(o)) for j, (px, py) in enumerate(_chips(x, y)[1:])]

    def recvs(i, o):
        x, y, c = _mesh_place()
        return [(3 * w + j, o[w].at[_slab(px, py, 1 - c)])
                for w in range(len(o)) for j, (px, py) in enumerate(_chips(x, y)[1:])]

    outs = [jax.ShapeDtypeStruct(a.shape, a.dtype) for a in gathered]
    return _Exchange(gathered, outs, 3 * len(gathered), sends, recvs, aliases={w: w for w in range(len(gathered))})


def _reduce_sibling(slabs):
    def sends(i, o):
        x, y, c = _mesh_place()
        return [(4 * w + k, i[w].at[_slab(px, py, 1 - c)], o[w].at[k], (x, y, 1 - c))
                for w in range(len(i)) for k, (px, py) in enumerate(_chips(x, y))]

    outs = [jax.ShapeDtypeStruct((4,) + a.shape[1:], a.dtype) for a in slabs]
    return _Exchange(slabs, outs, 4 * len(slabs), sends)


def _reduce_chips(partials):
    def sends(i, o):
        x, y, c = _mesh_place()
        return [(3 * w + k, i[w].at[k], o[w].at[k], (px, py, c))
                for w in range(len(i)) for k, (px, py) in enumerate(_chips(x, y)[1:])]

    outs = [jax.ShapeDtypeStruct(a.shape, a.dtype) for a in partials]
    return _Exchange(partials, outs, 3 * len(partials), sends)


def _head_masks():
    lane_head = lax.broadcasted_iota(jnp.int32, (1, GROUP_LANES), 1) // HEAD_DIM
    return [(lane_head == h).astype(F32) for h in range(HEADS_PER_GROUP)]


def _head_block_diag():
    r = lax.broadcasted_iota(jnp.int32, (GROUP_LANES, GROUP_LANES), 0) // HEAD_DIM
    c = lax.broadcasted_iota(jnp.int32, (GROUP_LANES, GROUP_LANES), 1) // HEAD_DIM
    return (r == c).astype(BF16)


def _head_mean(x, bd):
    hi = x.astype(BF16)
    lo = (x - hi.astype(F32)).astype(BF16)
    return (_dot(hi, bd) + _dot(lo, bd)) * (1.0 / HEAD_DIM)


def _stack_heads(x, masks):
    return jnp.concatenate([x * m for m in masks], axis=0)


def _unstack_heads(xs, masks):
    out = xs[0:GRID_W] * masks[0]
    for h in range(1, HEADS_PER_GROUP):
        out = out + xs[h * GRID_W:(h + 1) * GRID_W] * masks[h]
    return out


def _row_start(r, rows):
    return jnp.clip(r - WIN_H // 2, 0, rows - WIN_H)


ROWS_PER_STEP = 2


def _attn_common_specs(seq, n_hg, rows):
    win_keys = WIN_H * GRID_W
    q_spec = pl.BlockSpec((ROWS_PER_STEP * GRID_W, GROUP_LANES), lambda g, r: (r, g))
    k_spec = pl.BlockSpec((seq, GROUP_LANES), lambda g, r: (0, n_hg + g))
    v_spec = pl.BlockSpec((seq, GROUP_LANES), lambda g, r: (0, 2 * n_hg + g))
    gain_spec = pl.BlockSpec((1, GROUP_LANES), lambda g, r: (0, 0))

    def variant(r):
        return _row_start(r, rows) - r + (WIN_H - 1)

    bias_specs = [pl.BlockSpec((None, None, HEADS_PER_GROUP, GRID_W, win_keys),
                               lambda g, r, h=h: (g, variant(ROWS_PER_STEP * r + h), 0, 0, 0))
                  for h in range(ROWS_PER_STEP)]
    return q_spec, k_spec, v_spec, gain_spec, bias_specs, variant


def _attn_prepare_kv(k_ref, v_ref, kg, kn_scr, vb_scr, bd, seq):
    chunk = _tile(seq, 512)

    def step(c, carry):
        rows = pl.ds(pl.multiple_of(c * chunk, chunk), chunk)
        k = k_ref[rows, :]
        kn_scr[rows, :] = (k * lax.rsqrt(_head_mean(k * k, bd) + RMS_EPS) * kg).astype(BF16)
        vb_scr[rows, :] = v_ref[rows, :].astype(BF16)
        return carry

    lax.fori_loop(0, seq // chunk, step, 0)


def _attn_probs(qn, kw, bias, masks):
    qs = _stack_heads(qn, masks).astype(BF16)
    s = _dot(qs, kw, _NT) * (1.0 / math.sqrt(HEAD_DIM)) + bias
    m = jnp.max(s, axis=-1, keepdims=True)
    p = jnp.exp(s - m)
    return qs, p / jnp.sum(p, axis=-1, keepdims=True)


def _grid_ends(grid):
    first = lambda: functools.reduce(jnp.logical_and, [pl.program_id(a) == 0 for a in range(len(grid))])
    last = lambda: functools.reduce(jnp.logical_and, [pl.program_id(a) == n - 1 for a, n in enumerate(grid)])
    return first, last


def _attn_fwd(z, qg4, kg4, btab, ride=None):
    seq = z.shape[0]
    a_width = btab.shape[0] * GROUP_LANES
    n_hg, rows, win_keys = btab.shape[0], seq // GRID_W, WIN_H * GRID_W
    q_spec, k_spec, v_spec, gain_spec, bias_specs, _ = _attn_common_specs(seq, n_hg, rows)
    grid = (n_hg, rows // ROWS_PER_STEP)

    def body(q_ref, k_ref, v_ref, qg_ref, kg_ref, *rest):
        b_refs, (o_ref, kn_scr, vb_scr) = rest[:ROWS_PER_STEP], rest[ROWS_PER_STEP:]
        bd, masks = _head_block_diag(), _head_masks()

        @pl.when(pl.program_id(1) == 0)
        def _():
            _attn_prepare_kv(k_ref, v_ref, kg_ref[...], kn_scr, vb_scr, bd, seq)

        for h in range(ROWS_PER_STEP):
            r = ROWS_PER_STEP * pl.program_id(1) + h
            mine = slice(h * GRID_W, (h + 1) * GRID_W)
            win = pl.ds(pl.multiple_of(_row_start(r, rows) * GRID_W, GRID_W), win_keys)
            q = q_ref[mine, :]
            qn = q * lax.rsqrt(_head_mean(q * q, bd) + RMS_EPS) * qg_ref[...]
            bias = b_refs[h][...].reshape(HEADS_PER_GROUP * GRID_W, win_keys)
            _, p = _attn_probs(qn, kn_scr[win, :], bias, masks)
            o_ref[mine, :] = _unstack_heads(_dot(p.astype(BF16), vb_scr[win, :]), masks)

    first, last = _grid_ends(grid)
    (ya,), rode = _call(
        body, name="attn_fwd", grid=grid,
        in_specs=[q_spec, k_spec, v_spec, gain_spec, gain_spec] + bias_specs,
        out_specs=[pl.BlockSpec((ROWS_PER_STEP * GRID_W, GROUP_LANES), lambda g, r: (r, g))],
        out_shape=[jax.ShapeDtypeStruct((seq, a_width), F32)],
        scratch_shapes=[pltpu.VMEM((seq, GROUP_LANES), BF16), pltpu.VMEM((seq, GROUP_LANES), BF16)],
        args=(z, z, z, qg4, kg4) + (btab,) * ROWS_PER_STEP, ride=ride, first=first, last=last)
    return ya, rode


def _attn_bwd(z, d_out, qg4, kg4, btab, ride=None):
    seq = z.shape[0]
    n_hg, rows, win_keys = btab.shape[0], seq // GRID_W, WIN_H * GRID_W
    a_width = n_hg * GROUP_LANES
    q_spec, k_spec, v_spec, gain_spec, bias_specs, variant = _attn_common_specs(seq, n_hg, rows)
    scale = 1.0 / math.sqrt(HEAD_DIM)
    grid = (n_hg, rows // ROWS_PER_STEP)

    def body(q_ref, k_ref, v_ref, do_ref, qg_ref, kg_ref, *rest):
        b_refs, rest = rest[:ROWS_PER_STEP], rest[ROWS_PER_STEP:]
        dq_ref, dk_out, dv_out, db_ref, dqg_ref, dkg_ref, kn_scr, vb_scr, dk_ref, dv_ref = rest
        bd, masks = _head_block_diag(), _head_masks()

        @pl.when(pl.program_id(1) == 0)
        def _():
            _attn_prepare_kv(k_ref, v_ref, kg_ref[...], kn_scr, vb_scr, bd, seq)
            dk_ref[...] = jnp.zeros_like(dk_ref)
            dv_ref[...] = jnp.zeros_like(dv_ref)
            db_ref[...] = jnp.zeros_like(db_ref)
            dqg_ref[...] = jnp.zeros_like(dqg_ref)

        qg = qg_ref[...]
        for h in range(ROWS_PER_STEP):
            r = ROWS_PER_STEP * pl.program_id(1) + h
            mine = slice(h * GRID_W, (h + 1) * GRID_W)
            win = pl.ds(pl.multiple_of(_row_start(r, rows) * GRID_W, GRID_W), win_keys)
            q = q_ref[mine, :]
            rq = lax.rsqrt(_head_mean(q * q, bd) + RMS_EPS)
            qh = q * rq
            kw, vw = kn_scr[win, :], vb_scr[win, :]
            bias = b_refs[h][...].reshape(HEADS_PER_GROUP * GRID_W, win_keys)
            qs, p = _attn_probs(qh * qg, kw, bias, masks)
            dos = _stack_heads(do_ref[mine, :], masks).astype(BF16)
            dp = _dot(dos, vw, _NT)
            ds = p * (dp - jnp.sum(p * dp, axis=-1, keepdims=True))
            db_ref[variant(r)] += ds.reshape(HEADS_PER_GROUP, GRID_W, win_keys)
            dsb = ds.astype(BF16)
            dqn = _unstack_heads(_dot(dsb, kw), masks) * scale
            dk_ref[win, :] += _dot(dsb, qs, _TN) * scale
            dv_ref[win, :] += _dot(p.astype(BF16), dos, _TN)
            dqg_ref[...] += jnp.sum(dqn * qh, axis=0, keepdims=True)
            dqh = dqn * qg
            dq_ref[mine, :] = (rq * (dqh - qh * _head_mean(dqh * qh, bd))).astype(BF16)

        @pl.when(pl.program_id(1) == grid[1] - 1)
        def _():
            chunk = _tile(seq, 512)
            kg = kg_ref[...]

            def step(c, dkg):
                rws = pl.ds(pl.multiple_of(c * chunk, chunk), chunk)
                k = k_ref[rws, :]
                rk = lax.rsqrt(_head_mean(k * k, bd) + RMS_EPS)
                kh = k * rk
                dkn = dk_ref[rws, :]
                dkh = dkn * kg
                dk_out[rws, :] = (rk * (dkh - kh * _head_mean(dkh * kh, bd))).astype(BF16)
                dv_out[rws, :] = dv_ref[rws, :].astype(BF16)
                return dkg + jnp.sum(dkn * kh, axis=0, keepdims=True)

            dkg_ref[...] = lax.fori_loop(0, seq // chunk, step, jnp.zeros((1, GROUP_LANES), F32))

    col_spec = pl.BlockSpec((seq, GROUP_LANES), lambda g, r: (0, g))
    gsum_spec = pl.BlockSpec((None, 1, GROUP_LANES), lambda g, r: (g, 0, 0))
    first, last = _grid_ends(grid)
    rows_spec = pl.BlockSpec((ROWS_PER_STEP * GRID_W, GROUP_LANES), lambda g, r: (r, g))
    return _call(
        body, name="attn_bwd", grid=grid,
        in_specs=[q_spec, k_spec, v_spec, rows_spec, gain_spec, gain_spec] + bias_specs,
        out_specs=[rows_spec, col_spec, col_spec,
                   pl.BlockSpec((None, WIN_H, HEADS_PER_GROUP, GRID_W, win_keys), lambda g, r: (g, 0, 0, 0, 0)),
                   gsum_spec, gsum_spec],
        out_shape=[jax.ShapeDtypeStruct((seq, a_width), BF16)] * 3
        + [jax.ShapeDtypeStruct(btab.shape, F32)]
        + [jax.ShapeDtypeStruct((n_hg, 1, GROUP_LANES), F32)] * 2,
        scratch_shapes=[pltpu.VMEM((seq, GROUP_LANES), BF16), pltpu.VMEM((seq, GROUP_LANES), BF16),
                        pltpu.VMEM((seq, GROUP_LANES), F32), pltpu.VMEM((seq, GROUP_LANES), F32)],
        args=(z, z, z, d_out, qg4, kg4) + (btab,) * ROWS_PER_STEP, ride=ride, first=first, last=last)


def _bias_index():
    c = jnp.arange(GRID_W)
    col_start = jnp.clip(c - WIN_W // 2, 0, GRID_W - WIN_W)
    col_in = (c[None, :] >= col_start[:, None]) & (c[None, :] < col_start[:, None] + WIN_W)
    dc = jnp.clip(c[None, :] - c[:, None], -(WIN_W - 1), WIN_W - 1) + (WIN_W - 1)
    dr = jnp.arange(WIN_H)[:, None] + jnp.arange(WIN_H)[None, :]
    return col_in, dc, dr


def _bias_table(rpb):
    col_in, dc, _ = _bias_index()
    n_h = rpb.shape[0]
    n_hg = n_h // HEADS_PER_GROUP
    spread = ((jnp.arange(128)[:, None] == dc.reshape(1, -1)) & col_in.reshape(1, -1)).astype(F32)
    rows = jnp.stack([rpb[:, v:v + WIN_H] for v in range(WIN_H)], axis=1)
    rows = jnp.pad(rows, ((0, 0), (0, 0), (0, 0), (0, 128 - rows.shape[-1]))).reshape(n_h * WIN_H * WIN_H, 128)
    tab = _mm_plain(rows, spread, "nn", name="rpb_spread", tm=256, tn=2048, exact=True)
    tab = jnp.where(col_in.reshape(1, -1), tab, NEG_INF)
    tab = tab.reshape(n_hg, HEADS_PER_GROUP, WIN_H, WIN_H, GRID_W, GRID_W).transpose(0, 2, 1, 4, 3, 5)
    return tab.reshape(n_hg, WIN_H, HEADS_PER_GROUP, GRID_W, WIN_H * GRID_W)


def _bias_grad(dtab, n_h):
    col_in, dc, _ = _bias_index()
    onehot = (dc.reshape(-1, 1) == jnp.arange(128)[None, :]) & col_in.reshape(-1, 1)
    n_hg = n_h // HEADS_PER_GROUP
    d = dtab.reshape(n_hg, WIN_H, HEADS_PER_GROUP, GRID_W, WIN_H, GRID_W).transpose(0, 2, 1, 4, 3, 5)
    d = d.reshape(n_h * WIN_H * WIN_H, GRID_W * GRID_W)
    diag = _mm_plain(d, onehot.astype(BF16), "nn", name="rpb_diag_sum", tm=256, tn=128)
    diag = diag.reshape(n_h, WIN_H, WIN_H, 128)[..., : 2 * WIN_W - 1]
    out = jnp.zeros((n_h, 2 * WIN_H - 1, 2 * WIN_W - 1), F32)
    for v in range(WIN_H):
        out = out.at[:, v:v + WIN_H].add(diag[:, v])
    return out


def _cmul(ar, ai, br, bi):
    return ar * br - ai * bi, ar * bi + ai * br


def _s5_discretize(a_re, a_im, dt, b_re, b_im):
    c = b_re.shape[1]

    def fn(are, aim, dt_, bre, bim):
        lr, li = jnp.minimum(are, A_RE_MAX), aim
        mag = jnp.exp(lr * dt_)
        l1r, l1i = mag * jnp.cos(li * dt_), mag * jnp.sin(li * dt_)
        den = lr * lr + li * li
        nr, ni = l1r - 1.0, l1i
        cr, ci = (nr * lr + ni * li) / den, (ni * lr - nr * li) / den
        bbr, bbi = _cmul(cr, ci, bre, bim)
        shape = (are.shape[0], SUBLANES)
        lane = lax.broadcasted_iota(jnp.int32, shape, 1)
        pr, pi = l1r, l1i
        acc_r, acc_i = jnp.zeros(shape, F32), jnp.zeros(shape, F32)
        for k in range(SUBLANES):
            acc_r = jnp.where(lane == k, pr, acc_r)
            acc_i = jnp.where(lane == k, pi, acc_i)
            pr, pi = _cmul(pr, pi, l1r, l1i)
        return acc_r, acc_i, cr, ci, bbr, bbi

    return _rowwise(fn, [a_re, a_im, dt, b_re, b_im], [],
                    [(SUBLANES, F32), (SUBLANES, F32), (1, F32), (1, F32), (c, F32), (c, F32)],
                    name="s5_discretize", tm=1024)


def _s5_param_grads(a_re, a_im, dt, b_re, b_im, l1r, l1i, cr, ci, bbr, bbi, r_re, r_im, gb_re, gb_im):
    c = b_re.shape[1]

    def fn(are, aim, dt_, bre, bim, l1r_, l1i_, cr_, ci_, bbr_, bbi_, rr, ri, gbr, gbi):
        lr, li = jnp.minimum(are, A_RE_MAX), aim
        den = lr * lr + li * li
        dbr, dbi = _cmul(cr_, -ci_, gbr, gbi)
        gcr, gci = _cmul(bre, -bim, gbr, gbi)
        gcr, gci = jnp.sum(gcr, axis=1, keepdims=True), jnp.sum(gci, axis=1, keepdims=True)
        qr, qi = _cmul(bbr_, -bbi_, gbr, gbi)
        qr = rr - jnp.sum(qr, axis=1, keepdims=True)
        qi = ri - jnp.sum(qi, axis=1, keepdims=True)
        tr, ti = _cmul(gcr, gci, lr / den, li / den)
        ur, ui = _cmul(l1r_, -l1i_, tr, ti)
        gwr, gwi = qr + ur, qi + ui
        vr, vi = _cmul(cr_, -ci_, lr / den, li / den)
        vr, vi = _cmul(gcr, gci, vr, vi)
        glr, gli = dt_ * gwr - vr, dt_ * gwi - vi
        return jnp.where(are < A_RE_MAX, glr, 0.0), gli, (gwr * lr + gwi * li) * dt_, dbr, dbi

    return _rowwise(fn, [a_re, a_im, dt, b_re, b_im, l1r, l1i, cr, ci, bbr, bbi, r_re, r_im, gb_re, gb_im], [],
                    [(1, F32), (1, F32), (1, F32), (c, F32), (c, F32)], name="s5_param_grads", tm=1024)


def _s5_scan(v, win_re, win_im, tabs, wo_re, wo_im, *, reverse, name, t_chunk=256, ride=None):
    seq, width = v.shape
    n_tiles, n_state = width // U_TILE, width * (SSM_P // SSM_C)
    t_chunk = _tile(seq, t_chunk)
    n_chunks, n_blk = seq // t_chunk, t_chunk // SUBLANES
    last_row = 0 if reverse else SUBLANES - 1

    def chunk_of(j):
        return (n_chunks - 1 - j) if reverse else j

    def body(v_ref, wir_ref, wii_ref, tab_ref, wor_ref, woi_ref, sr_ref, si_ref, y_ref, carry, wr, wi):
        @pl.when(pl.program_id(0) == 0)
        def _():
            carry[...] = jnp.zeros_like(carry)

        for jt in range(n_tiles):
            ls = slice(jt * ST_TILE, (jt + 1) * ST_TILE)
            us = slice(jt * U_TILE, (jt + 1) * U_TILE)
            vj = v_ref[:, us].astype(BF16)
            consts = [tab_ref[k, :, ls] for k in range(8)]
            xr = _dot(vj, wir_ref[jt]).reshape(n_blk, SUBLANES, ST_TILE)
            xi = _dot(vj, wii_ref[jt]).reshape(n_blk, SUBLANES, ST_TILE)
            for s, k in enumerate((1, 2, 4)):
                sh = (SUBLANES - k) if reverse else k
                tr, ti = pltpu.roll(xr, sh, 1), pltpu.roll(xi, sh, 1)
                lr, li = consts[2 * s][None], consts[2 * s + 1][None]
                xr, xi = xr + lr * tr - li * ti, xi + lr * ti + li * tr
            wr[...] = xr.reshape(t_chunk, ST_TILE)
            wi[...] = xi.reshape(t_chunk, ST_TILE)

            def blk(b, c, consts=consts):
                cr, ci = c
                bb = (n_blk - 1 - b) if reverse else b
                rows = pl.ds(pl.multiple_of(bb * SUBLANES, SUBLANES), SUBLANES)
                lr, li = consts[6], consts[7]
                xr = wr[rows, :] + lr * cr - li * ci
                xi = wi[rows, :] + lr * ci + li * cr
                wr[rows, :], wi[rows, :] = xr, xi
                shape = (SUBLANES, ST_TILE)
                return (jnp.broadcast_to(xr[last_row:last_row + 1], shape),
                        jnp.broadcast_to(xi[last_row:last_row + 1], shape))

            cr, ci = lax.fori_loop(0, n_blk, blk, (carry[0, :, ls], carry[1, :, ls]), unroll=2)
            carry[0, :, ls], carry[1, :, ls] = cr, ci
            xr_b, xi_b = wr[...].astype(BF16), wi[...].astype(BF16)
            sr_ref[:, ls], si_ref[:, ls] = xr_b, xi_b
            y_ref[:, us] = _dot(xr_b, wor_ref[jt]) + _dot(xi_b, woi_ref[jt])

    whole = lambda a: pl.BlockSpec(a.shape, lambda j, nd=a.ndim: (0,) * nd)
    st_spec = pl.BlockSpec((t_chunk, n_state), lambda j: (chunk_of(j), 0))
    v_spec = pl.BlockSpec((t_chunk, width), lambda j: (chunk_of(j), 0))
    first, last = _grid_ends((n_chunks,))
    return _call(
        body, name=name, grid=(n_chunks,),
        in_specs=[v_spec, whole(win_re), whole(win_im), whole(tabs), whole(wo_re), whole(wo_im)],
        out_specs=[st_spec, st_spec, v_spec],
        out_shape=[jax.ShapeDtypeStruct((seq, n_state), BF16)] * 2 + [jax.ShapeDtypeStruct((seq, width), F32)],
        scratch_shapes=[pltpu.VMEM((2, SUBLANES, n_state), F32), pltpu.VMEM((t_chunk, ST_TILE), F32),
                        pltpu.VMEM((t_chunk, ST_TILE), F32)],
        args=(v, win_re, win_im, tabs, wo_re, wo_im), ride=ride, first=first, last=last)


def _s5_reduce(x_re, x_im, a_re, a_im, u, dy, *, name, t_chunk=512, ride=None):
    seq, n_state = x_re.shape
    width = u.shape[1]
    n_tiles = width // U_TILE
    t_chunk = _tile(seq, t_chunk)

    def body(xr_ref, xi_ref, ar_ref, ai_ref, u_ref, dy_ref, rr_ref, ri_ref, gbr_ref, gbi_ref, gcr_ref, gci_ref):
        xrb, xib, arb, aib = xr_ref[...], xi_ref[...], ar_ref[...], ai_ref[...]
        xr, xi, ar, ai = xrb.astype(F32), xib.astype(F32), arb.astype(F32), aib.astype(F32)
        ub, dyb = u_ref[...].astype(BF16), dy_ref[...].astype(BF16)
        parts = (jnp.sum(ar * xr + ai * xi, axis=0, keepdims=True), jnp.sum(ai * xr - ar * xi, axis=0, keepdims=True),
                 _dot(arb, ub, _TN), _dot(aib, ub, _TN), _dot(xrb, dyb, _TN), _dot(xib, dyb, _TN))
        first = pl.program_id(1) == 0
        for ref, val in zip((rr_ref, ri_ref, gbr_ref, gbi_ref, gcr_ref, gci_ref), parts):
            @pl.when(first)
            def _():
                ref[...] = val

            @pl.when(jnp.logical_not(first))
            def _():
                ref[...] += val

    st_spec = pl.BlockSpec((t_chunk, ST_TILE), lambda j, t: (t, j))
    u_spec = pl.BlockSpec((t_chunk, U_TILE), lambda j, t: (t, j))
    r_spec = pl.BlockSpec((1, ST_TILE), lambda j, t: (0, j))
    g_spec = pl.BlockSpec((None, ST_TILE, U_TILE), lambda j, t: (j, 0, 0))
    first, last = _grid_ends((n_tiles, seq // t_chunk))
    return _call(
        body, name=name, grid=(n_tiles, seq // t_chunk),
        in_specs=[st_spec] * 4 + [u_spec] * 2,
        out_specs=[r_spec, r_spec] + [g_spec] * 4,
        out_shape=[jax.ShapeDtypeStruct((1, n_state), F32)] * 2
        + [jax.ShapeDtypeStruct((n_tiles, ST_TILE, U_TILE), F32)] * 4,
        scratch_shapes=[], args=(x_re, x_im, a_re, a_im, u, dy), ride=ride, first=first, last=last)


def _block_diag_in(ms):
    m = jnp.stack(ms)
    n, g, c, p = m.shape
    m5 = m.reshape(n, g // GROUPS_PER_TILE, GROUPS_PER_TILE, c, p)
    eye = jnp.eye(GROUPS_PER_TILE, dtype=m.dtype)
    out = m5[:, :, :, :, None, :] * eye[None, None, :, None, :, None]
    return out.astype(BF16).reshape(n, g // GROUPS_PER_TILE, GROUPS_PER_TILE * c, GROUPS_PER_TILE * p)


def _block_diag_take(m, c, p):
    t = m.shape[0]
    m5 = m.reshape(t, GROUPS_PER_TILE, p, GROUPS_PER_TILE, c)
    idx = jnp.arange(GROUPS_PER_TILE)
    return m5[:, idx, :, idx, :].transpose(1, 0, 2, 3).reshape(t * GROUPS_PER_TILE, p, c)


def _scan_tables(pw_re, pw_im, reverse):
    row = jnp.arange(SUBLANES)[:, None]
    tabs = []
    for k in (1, 2, 4):
        keep = (row <= SUBLANES - 1 - k) if reverse else (row >= k)
        tabs += [jnp.where(keep, pw_re[k - 1][None, :], 0.0), jnp.where(keep, pw_im[k - 1][None, :], 0.0)]
    order = jnp.arange(SUBLANES)[::-1] if reverse else jnp.arange(SUBLANES)
    tabs += [pw_re[order], pw_im[order]]
    return jnp.stack(tabs)


def _partial_sums(slabs, from_sibling, names):
    x, y, c = _mesh_place()
    theirs = jnp.stack([_slab(px, py, c) for px, py in _chips(x, y)[1:]]).astype(jnp.int32)
    out = []
    for s, f, n in zip(slabs, from_sibling, names):
        rows, cols = s.shape[1:]
        tr = _tile(rows, 512)

        def body(idx_ref, a_ref, b_ref, o_ref):
            o_ref[...] = (a_ref[...] + b_ref[...]).astype(BF16)

        out.append(pl.pallas_call(
            body, name=f"reduce_add_{n}",
            grid_spec=pltpu.PrefetchScalarGridSpec(
                num_scalar_prefetch=1, grid=(3, rows // tr),
                in_specs=[pl.BlockSpec((None, tr, cols), lambda k, i, idx: (idx[k], i, 0)),
                          pl.BlockSpec((None, tr, cols), lambda k, i, idx: (k + 1, i, 0))],
                out_specs=pl.BlockSpec((None, tr, cols), lambda k, i, idx: (k, i, 0))),
            out_shape=jax.ShapeDtypeStruct((3, rows, cols), BF16), compiler_params=_params(),
        )(theirs, s, f))
    return out


def _local_step(x, target, p, w_in, shards):
    seq, d_model = x.shape
    a_width = p["g_out_attn"].shape[-1]
    s_width = p["g_out_ssm"].shape[-1]
    n_heads = a_width // HEAD_DIM
    n_hg = n_heads // HEADS_PER_GROUP
    n_groups = s_width // SSM_C
    n_sh, _, in_sh = w_in.shape
    f_sh = shards["w_ffn_gate"].shape[-1]
    w = {"w_in": w_in}
    slab3 = lambda g, n: g.reshape(N_DEV, -1, shards[n].shape[-1])
    t2, t1 = _tile(seq, 2048), _tile(seq, 1024)
    n2, n1 = seq // t2, seq // t1

    twice = lambda f: (lambda *a: (f(*a),) * 2)
    h1, h1_t = _rowwise(twice(x_norm), [x], [p["g_mix"]], [(d_model, BF16)], flipped=[(d_model, BF16, 1)],
                        name="rms_mix")
    z = _mm(h1, w["w_in"], name="in_proj", grid=(n2, n_sh),
            a_spec=pl.BlockSpec((t2, d_model), lambda i, j: (i, 0)),
            b_spec=pl.BlockSpec((None, d_model, in_sh), lambda i, j: (j, 0, 0)),
            o_spec=pl.BlockSpec((t2, in_sh), lambda i, j: (i, j)), o_shape=(seq, n_sh * in_sh), dims="nn")
    qg4 = jnp.tile(p["q_gain"], (1, HEADS_PER_GROUP))
    kg4 = jnp.tile(p["k_gain"], (1, HEADS_PER_GROUP))
    btab = _bias_table(p["rpb"])
    ya, got_a = _attn_fwd(z, qg4, kg4, btab, ride=_gather_first([shards["w_ffn_gate"], shards["w_ffn_up"]]))
    u = z[:, 3 * a_width:]

    n_col = 2 * n_groups * SSM_P
    col = lambda a: a.reshape(n_col, 1)
    a_re_c, a_im_c = col(p["ssm_a_re"]), col(p["ssm_a_im"])
    dt_c = col(jnp.broadcast_to(jnp.exp(p["ssm_log_step"])[:, :, None], (2, n_groups, SSM_P)))
    b_re_c, b_im_c = p["ssm_b_re"].reshape(n_col, SSM_C), p["ssm_b_im"].reshape(n_col, SSM_C)
    pw_re, pw_im, cf_re, cf_im, bb_re, bb_im = _s5_discretize(a_re_c, a_im_c, dt_c, b_re_c, b_im_c)
    n_state = n_groups * SSM_P
    pw_re = pw_re.reshape(2, n_state, SUBLANES).transpose(0, 2, 1)
    pw_im = pw_im.reshape(2, n_state, SUBLANES).transpose(0, 2, 1)
    bb_re4, bb_im4 = bb_re.reshape(2, n_groups, SSM_P, SSM_C), bb_im.reshape(2, n_groups, SSM_P, SSM_C)
    c_re, c_im = p["ssm_c_re"], p["ssm_c_im"]
    t21 = lambda a: a.transpose(0, 2, 1)
    maps_in = _block_diag_in([m for d in range(2) for m in (t21(bb_re4[d]), t21(bb_im4[d]), c_re[d], -c_im[d])])
    maps_out = _block_diag_in([m for d in range(2) for m in (t21(c_re[d]), -t21(c_im[d]), bb_re4[d], bb_im4[d])])
    fwd, bwd_in = [], []
    got_b = None
    for d in range(2):
        rev = d == 1
        tabs = _scan_tables(pw_re[d], pw_im[d], rev)
        if d == 0:
            ride = _gather_first([shards["w_glu"], shards["w_out"], shards["w_ffn_down"]])
        else:
            ride = _gather_second(got_a + got_b)
        (xs_re, xs_im, y_d), got = _s5_scan(u, maps_in[4 * d], maps_in[4 * d + 1], tabs, maps_out[4 * d],
                                            maps_out[4 * d + 1], reverse=rev, name=f"s5_fwd_{d}", ride=ride)
        if d == 0:
            got_b = got
        fwd.append((xs_re, xs_im, y_d))
        bwd_in.append((maps_in[4 * d + 2], maps_in[4 * d + 3], _scan_tables(pw_re[d], -pw_im[d], not rev),
                       maps_out[4 * d + 2], maps_out[4 * d + 3]))
    w["w_gate"], w["w_up"], w_glu_full, w_out_full, w["w_down"] = got
    w["w_glu"] = w_glu_full.reshape(-1, s_width)
    w["w_out"] = w_out_full.reshape(-1, d_model)

    ypre, yg, yg_t = _rowwise(s5_mid, [fwd[0][2], fwd[1][2], u], [p["ssm_d"]], [(s_width, F32), (s_width, F32)],
                              flipped=[(s_width, BF16, 1)], name="s5_skip_gelu")
    t_glu = _mm_plain(yg, w["w_glu"], "nn", name="glu_proj", tn=s_width)
    y_cat, y_cat_t = _rowwise(twice(mix_out_fwd), [ya, yg, t_glu], [p["b_glu"], p["g_out_attn"], p["g_out_ssm"]],
                              [(a_width + s_width, BF16)], flipped=[(a_width + s_width, BF16, 1)], name="mix_out")
    x1 = _mm_plain(y_cat, w["w_out"], "nn", name="out_proj", res=x, tn=2048)

    h2, h2_t = _rowwise(twice(x_norm), [x1], [p["g_ffn"]], [(d_model, BF16)], flipped=[(d_model, BF16, 1)],
                        name="rms_ffn")
    ffn_up = functools.partial(
        _mm, grid=(n2, n_sh), a_spec=pl.BlockSpec((t2, d_model), lambda i, j: (i, 0)),
        b_spec=pl.BlockSpec((None, d_model, f_sh), lambda i, j: (j, 0, 0)),
        o_spec=pl.BlockSpec((None, t2, f_sh), lambda i, j: (j, i, 0)), o_shape=(n_sh, seq, f_sh), dims="nn",
        out_dtype=BF16)
    gate = ffn_up(h2, w["w_gate"], name="ffn_gate")
    up = ffn_up(h2, w["w_up"], name="ffn_up")
    flat = lambda a: a.reshape(n_sh * seq, f_sh)
    act, act_t = _rowwise(twice(swiglu_fwd), [flat(gate), flat(up)], [], [(f_sh, BF16)],
                          flipped=[(f_sh, BF16, n_sh)], name="swiglu", tm=1024)
    act, act_t = act.reshape(n_sh, seq, f_sh), act_t.reshape(n_sh, f_sh, seq)
    ffn_out = _mm(act, w["w_down"], name="ffn_down", grid=(n1, n_sh),
                  a_spec=pl.BlockSpec((None, t1, f_sh), lambda i, j: (j, i, 0)),
                  b_spec=pl.BlockSpec((None, f_sh, d_model), lambda i, j: (j, 0, 0)),
                  o_spec=pl.BlockSpec((t1, d_model), lambda i, j: (i, 0)), o_shape=(seq, d_model), dims="nn",
                  k_axis=1)

    dx2, dx2_b, sq = _rowwise(functools.partial(loss_head, inv_d=1.0 / d_model), [ffn_out, x1, target], [],
                              [(d_model, F32), (d_model, BF16)], [d_model], name="loss_head")
    loss = 0.5 * jnp.sum(sq) / d_model

    d_act = _mm(dx2_b, w["w_down"], name="ffn_down_dx", grid=(n2, n_sh),
                a_spec=pl.BlockSpec((t2, d_model), lambda i, j: (i, 0)),
                b_spec=pl.BlockSpec((None, f_sh, d_model), lambda i, j: (j, 0, 0)),
                o_spec=pl.BlockSpec((None, t2, f_sh), lambda i, j: (j, i, 0)), o_shape=(n_sh, seq, f_sh), dims="nt",
                out_dtype=BF16)
    g_w_down = _mm(act_t, dx2_b, name="ffn_down_dw", grid=(n_sh, n1),
                   a_spec=pl.BlockSpec((None, f_sh, t1), lambda j, k: (j, 0, k)),
                   b_spec=pl.BlockSpec((t1, d_model), lambda j, k: (k, 0)),
                   o_spec=pl.BlockSpec((None, f_sh, d_model), lambda j, k: (j, 0, 0)),
                   o_shape=(n_sh, f_sh, d_model), dims="nn", k_axis=1)
    d_gate, d_up = _rowwise(swiglu_bwd, [flat(d_act), flat(gate), flat(up)], [], [(f_sh, BF16), (f_sh, BF16)],
                            name="swiglu_bwd", tm=1024)
    d_gate, d_up = d_gate.reshape(n_sh, seq, f_sh), d_up.reshape(n_sh, seq, f_sh)
    d_h2 = _mm(d_gate, w["w_gate"], second=(d_up, w["w_up"]), name="ffn_up_gate_dx", grid=(n1, n_sh),
               a_spec=pl.BlockSpec((None, t1, f_sh), lambda i, j: (j, i, 0)),
               b_spec=pl.BlockSpec((None, d_model, f_sh), lambda i, j: (j, 0, 0)),
               o_spec=pl.BlockSpec((t1, d_model), lambda i, j: (i, 0)), o_shape=(seq, d_model), dims="nt", k_axis=1)
    ffn_dw = functools.partial(
        _mm, grid=(n_sh, n1), a_spec=pl.BlockSpec((d_model, t1), lambda j, k: (0, k)),
        b_spec=pl.BlockSpec((None, t1, f_sh), lambda j, k: (j, k, 0)),
        o_spec=pl.BlockSpec((None, d_model, f_sh), lambda j, k: (j, 0, 0)), o_shape=(n_sh, d_model, f_sh), dims="nn",
        k_axis=1)
    g_w_gate = ffn_dw(h2_t, d_gate, name="ffn_gate_dw")
    g_w_up = ffn_dw(h2_t, d_up, name="ffn_up_dw")
    dx1, g_g_ffn = _rowwise(residual_rms_bwd, [dx2, d_h2, x1], [p["g_ffn"]], [(d_model, F32)], [d_model],
                            name="rms_ffn_bwd")

    d_ycat = _mm_plain(dx1, w["w_out"], "nt", name="out_proj_dx", tn=2048)
    mix_w = a_width + s_width
    tm_o = _tile(mix_w, 1024)
    g_w_out = _mm(y_cat_t, dx1, name="out_proj_dw", grid=(mix_w // tm_o, n1),
                  a_spec=pl.BlockSpec((tm_o, t1), lambda i, k: (i, k)),
                  b_spec=pl.BlockSpec((t1, d_model), lambda i, k: (k, 0)),
                  o_spec=pl.BlockSpec((tm_o, d_model), lambda i, k: (i, 0)), o_shape=(mix_w, d_model), dims="nn",
                  k_axis=1)
    (d_ya, d_yg_direct, d_t, g_goa, g_gos, g_b_glu) = _rowwise(
        functools.partial(mix_out_bwd, a_width=a_width), [d_ycat, ya, yg, t_glu],
        [p["b_glu"], p["g_out_attn"], p["g_out_ssm"]],
        [(a_width, F32), (s_width, F32), (s_width, BF16)], [a_width, s_width, s_width], name="mix_out_bwd")
    d_yg = _mm_plain(d_t, w["w_glu"], "nt", name="glu_proj_dx", res=d_yg_direct, tn=s_width)
    g_w_glu = _mm(yg_t, d_t, name="glu_proj_dw", grid=(1, n1),
                  a_spec=pl.BlockSpec((s_width, t1), lambda i, k: (0, k)),
                  b_spec=pl.BlockSpec((t1, s_width), lambda i, k: (k, 0)),
                  o_spec=pl.BlockSpec((s_width, s_width), lambda i, k: (0, 0)), o_shape=(s_width, s_width),
                  dims="nn", k_axis=1)
    d_ypre, du_skip, g_ssm_d = _rowwise(gelu_skip_bwd, [d_yg, ypre, u], [p["ssm_d"]],
                                        [(s_width, F32), (s_width, F32)], [s_width], name="s5_skip_gelu_bwd")

    ffn_names, mix_names = ("w_ffn_gate", "w_ffn_up", "w_ffn_down"), ("w_glu", "w_out")
    ffn_slabs = [slab3(g, n) for g, n in zip((g_w_gate, g_w_up, g_w_down), ffn_names)]
    mix_slabs = [slab3(g, n) for g, n in zip((g_w_glu, g_w_out), mix_names)]
    du_dirs, adj, r_parts, gb_parts, gc_parts = [], [], [], [], []
    sib, part = {}, {}
    for d, (names, slabs) in enumerate(((ffn_names, ffn_slabs), (mix_names, mix_slabs))):
        win_re, win_im, tabs, wo_re, wo_im = bwd_in[d]
        (as_re, as_im, du_d), got = _s5_scan(d_ypre, win_re, win_im, tabs, wo_re, wo_im, reverse=(d == 0),
                                             name=f"s5_bwd_{d}", ride=_reduce_sibling(slabs))
        du_dirs.append(du_d)
        adj.append((as_re, as_im))
        sib[names] = got
        part[names] = _partial_sums(slabs, got, names)
    for d in range(2):
        (r_re, r_im, gbt_re, gbt_im, gct_re, gct_im), got = _s5_reduce(
            fwd[d][0], fwd[d][1], adj[d][0], adj[d][1], u, d_ypre, name=f"s5_reduce_{d}",
            ride=_reduce_chips(part[mix_names]) if d == 0 else None)
        if d == 0:
            mix_chips = got
        r_parts.append((r_re.reshape(n_state, 1), r_im.reshape(n_state, 1)))
        gb_parts.append((_block_diag_take(gbt_re, SSM_C, SSM_P), _block_diag_take(gbt_im, SSM_C, SSM_P)))
        gc_parts.append((_block_diag_take(gct_re, SSM_C, SSM_P), _block_diag_take(gct_im, SSM_C, SSM_P)))
    cat = lambda i, parts: jnp.concatenate([parts[0][i], parts[1][i]], axis=0)
    gbb_re, gbb_im = cat(0, gb_parts).reshape(n_col, SSM_C), cat(1, gb_parts).reshape(n_col, SSM_C)
    g_a_re, g_a_im, g_ls, g_b_re, g_b_im = _s5_param_grads(
        a_re_c, a_im_c, dt_c, b_re_c, b_im_c, pw_re[:, 0].reshape(n_col, 1), pw_im[:, 0].reshape(n_col, 1),
        cf_re, cf_im, bb_re, bb_im, cat(0, r_parts), cat(1, r_parts), gbb_re, gbb_im)
    g_c_re = cat(0, gc_parts).reshape(2, n_groups, SSM_P, SSM_C).transpose(0, 1, 3, 2)
    g_c_im = -cat(1, gc_parts).reshape(2, n_groups, SSM_P, SSM_C).transpose(0, 1, 3, 2)

    (d_q, d_k, d_v, d_btab, g_qg, g_kg), ffn_chips = _attn_bwd(z, d_ya, qg4, kg4, btab,
                                                                ride=_reduce_chips(part[ffn_names]))
    d_u = _rowwise(lambda a, b, c: a + b + c, [du_dirs[0], du_dirs[1], du_skip], [], [(s_width, BF16)],
                   name="s5_du_sum")[0]
    d_z = jnp.concatenate([d_q, d_k, d_v, d_u], axis=1)
    fold_heads = lambda g: g.reshape(n_heads, HEAD_DIM).sum(axis=0, keepdims=True)
    small = {
        "q_gain": fold_heads(g_qg), "k_gain": fold_heads(g_kg), "rpb": _bias_grad(d_btab, n_heads),
        "ssm_a_re": g_a_re.reshape(2, n_groups, SSM_P), "ssm_a_im": g_a_im.reshape(2, n_groups, SSM_P),
        "ssm_b_re": g_b_re.reshape(2, n_groups, SSM_P, SSM_C), "ssm_b_im": g_b_im.reshape(2, n_groups, SSM_P, SSM_C),
        "ssm_c_re": g_c_re, "ssm_c_im": g_c_im,
        "ssm_log_step": g_ls.reshape(2, n_groups, SSM_P).sum(axis=-1),
        "ssm_d": g_ssm_d, "b_glu": g_b_glu, "g_out_attn": g_goa, "g_out_ssm": g_gos, "g_ffn": g_g_ffn,
    }
    packed = _pack([small[n] for n in SMALL_PACKED])
    g_w_in, got = _mm(h1_t, d_z, name="in_proj_dw", grid=(n_sh, n1),
                      a_spec=pl.BlockSpec((d_model, t1), lambda j, k: (0, k)),
                      b_spec=pl.BlockSpec((t1, in_sh), lambda j, k: (k, j)),
                      o_spec=pl.BlockSpec((None, d_model, in_sh), lambda j, k: (j, 0, 0)),
                      o_shape=(n_sh, d_model, in_sh), dims="nn", k_axis=1, ride=_gather_first([packed]))
    d_h1, got = _mm(d_z, w["w_in"], name="in_proj_dx", grid=(n1, n_sh),
                    a_spec=pl.BlockSpec((t1, in_sh), lambda i, j: (i, j)),
                    b_spec=pl.BlockSpec((None, d_model, in_sh), lambda i, j: (j, 0, 0)),
                    o_spec=pl.BlockSpec((t1, d_model), lambda i, j: (i, 0)), o_shape=(seq, d_model), dims="nt",
                    k_axis=1, ride=_gather_second(got) + _reduce_sibling([g_w_in]))
    small_gathered, in_sibling = got
    in_part = _partial_sums([g_w_in], [in_sibling], ("w_in",))
    (grad_x, g_g_mix), (in_chips,) = _rowwise(residual_rms_bwd, [dx1, d_h1, x], [p["g_mix"]], [(d_model, F32)],
                                              [d_model], name="rms_mix_bwd", ride=_reduce_chips(in_part))
    reduced = {"w_in": (g_w_in, in_sibling, in_chips)}
    for names, slabs, chips in ((ffn_names, ffn_slabs, ffn_chips), (mix_names, mix_slabs, mix_chips)):
        for i, n in enumerate(names):
            reduced[n] = (slabs[i], sib[names][i], chips[i])
    return loss, grad_x, small_gathered, g_g_mix, reduced


def x_norm(xv, g):
    return xv * _rstd(xv) * g


def s5_mid(y0, y1, uv, d_skip):
    ypre = y0 + y1 + d_skip * uv
    yg = _gelu(ypre)
    return ypre, yg, yg


def mix_out_fwd(ya, yg, t, b_glu, g_oa, g_os):
    ys = yg * _sigmoid(t + b_glu)
    return jnp.concatenate([ya * _rstd(ya) * g_oa, ys * _rstd(ys) * g_os], axis=1)


def mix_out_bwd(d_y, ya, yg, t, b_glu, g_oa, g_os, *, a_width):
    sg = _sigmoid(t + b_glu)
    ys = yg * sg
    d_ya, c_goa = _rms_bwd(d_y[:, :a_width], ya, g_oa)
    d_ys, c_gos = _rms_bwd(d_y[:, a_width:], ys, g_os)
    d_t = d_ys * yg * sg * (1.0 - sg)
    return d_ya, d_ys * sg, d_t, c_goa, c_gos, d_t


def gelu_skip_bwd(d_yg, ypre, uv, d_skip):
    d_ypre = d_yg * _gelu_grad(ypre)
    return d_ypre, d_ypre * d_skip, d_ypre * uv


def swiglu_fwd(gv, uv):
    gv, uv = gv.astype(F32), uv.astype(F32)
    return gv * _sigmoid(gv) * uv


def swiglu_bwd(d_act, gv, uv):
    d_act, gv, uv = d_act.astype(F32), gv.astype(F32), uv.astype(F32)
    sg = _sigmoid(gv)
    return d_act * uv * (sg * (1.0 + gv * (1.0 - sg))), d_act * gv * sg


def loss_head(ffn_out, x1, target, *, inv_d):
    diff = ffn_out + x1 - target
    return diff * inv_d, diff * inv_d, diff * diff


def residual_rms_bwd(d_res, d_h, xv, g):
    dx, c_g = _rms_bwd(d_h, xv, g)
    return d_res + dx, c_g


_ANY = pl.BlockSpec(memory_space=pl.ANY)


def _mesh_place():
    return lax.axis_index("x"), lax.axis_index("y"), lax.axis_index("c")


def _chips(x, y):
    return [(x, y), (1 - x, y), (x, 1 - y), (1 - x, 1 - y)]


def _slab(px, py, pc):
    return 4 * px + 2 * py + pc


def _all_gather(arrs, *, name):
    n = len(arrs)

    def body(*refs):
        in_refs, out_refs = refs[:n], refs[n:2 * n]
        send_sems, recv_sems, local_sems = refs[2 * n:]
        x, y, c = _mesh_place()
        me, sibling = (x, y, c), (x, y, 1 - c)
        others = _chips(x, y)[1:]

        def copy(w, k, block, to, src=None):
            dst = out_refs[w].at[_slab(*block)]
            return pltpu.make_async_remote_copy(
                src_ref=dst if src is None else src, dst_ref=dst, send_sem=send_sems.at[7 * w + k],
                recv_sem=recv_sems.at[7 * w + k], device_id=to, device_id_type=MESH)

        mine = [pltpu.make_async_copy(in_refs[w], out_refs[w].at[_slab(*me)], local_sems.at[w]) for w in range(n)]
        first = []
        for w in range(n):
            mine[w].start()
            first.append(copy(w, 0, me, sibling, src=in_refs[w]))
            first += [copy(w, 1 + j, me, (*chip, c), src=in_refs[w]) for j, chip in enumerate(others)]
        for cp in first:
            cp.start()
        passed = []
        for j, chip in enumerate(others):
            for w in range(n):
                copy(w, 1 + j, (*chip, c), me).wait_recv()
                fwd = copy(w, 4 + j, (*chip, c), sibling)
                fwd.start()
                passed.append(fwd)
        for w in range(n):
            copy(w, 0, sibling, me).wait_recv()
        for j, chip in enumerate(others):
            for w in range(n):
                copy(w, 4 + j, (*chip, 1 - c), me).wait_recv()
        for cp in first + passed:
            cp.wait_send()
        for cp in mine:
            cp.wait()

    return pl.pallas_call(
        body, name=name, in_specs=[_ANY] * n, out_specs=[_ANY] * n,
        out_shape=[jax.ShapeDtypeStruct((N_DEV,) + a.shape, a.dtype) for a in arrs],
        scratch_shapes=[pltpu.SemaphoreType.DMA((7 * n,)), pltpu.SemaphoreType.DMA((7 * n,)),
                        pltpu.SemaphoreType.DMA((n,))],
        compiler_params=pltpu.CompilerParams(has_side_effects=True),
    )(*arrs)


def _adamw(w, m, v, parts, *, name, slab, tr=256):
    rows, cols = w.shape
    tr = _tile(rows, tr)
    n_p = len(parts)

    def body(slab_ref, *refs):
        w_ref, m_ref, v_ref = refs[:3]
        p_refs = refs[3:3 + n_p]
        g_ref, d_ref, nm_ref, nv_ref = refs[3 + n_p:]
        g = None
        for (_, lead), r in zip(parts, p_refs):
            for piece in ([r[...]] if lead is None else [r[i] for i in range(lead)]):
                g = piece.astype(F32) if g is None else g + piece.astype(F32)
        new_m = ADAM_B1 * m_ref[...] + (1.0 - ADAM_B1) * g
        new_v = ADAM_B2 * v_ref[...] + (1.0 - ADAM_B2) * (g * g)
        m_hat = new_m / (1.0 - ADAM_B1 ** ADAM_STEP)
        v_hat = new_v / (1.0 - ADAM_B2 ** ADAM_STEP)
        g_ref[...] = g
        d_ref[...] = -ADAM_LR * (m_hat / (jnp.sqrt(v_hat) + ADAM_EPS) + ADAM_WD * w_ref[...])
        nm_ref[...] = new_m
        nv_ref[...] = new_v

    tile = pl.BlockSpec((tr, cols), lambda i, s: (i, 0))
    p_specs = [pl.BlockSpec((None, tr, cols), lambda i, s: (s[0], i, 0)) if lead is None
               else pl.BlockSpec((lead, tr, cols), lambda i, s: (0, i, 0)) for _, lead in parts]
    return pl.pallas_call(
        body, name=name,
        grid_spec=pltpu.PrefetchScalarGridSpec(num_scalar_prefetch=1, grid=(rows // tr,),
                                               in_specs=[tile] * 3 + p_specs, out_specs=[tile] * 4),
        out_shape=[jax.ShapeDtypeStruct((rows, cols), F32)] * 4, compiler_params=_params(),
    )(jnp.reshape(slab, (1,)).astype(jnp.int32), w, m, v, *[a for a, _ in parts])


_PACK_TILE = SUBLANES * 128
_PACK_ROWS = 512


def _pack(arrs):
    flat = []
    for a in arrs:
        f = a.reshape(-1)
        flat.append(jnp.pad(f, (0, (-f.shape[0]) % _PACK_TILE)))
    total = sum(f.shape[0] for f in flat)
    flat.append(jnp.zeros(((-total) % (_PACK_ROWS * 128),), F32))
    return jnp.concatenate(flat).reshape(-1, 128)


def _unpack(buf, shapes):
    out, at = [], 0
    flat = buf.reshape(-1)
    for s in shapes:
        n = math.prod(s)
        out.append(flat[at:at + n].reshape(s))
        at += n + (-n) % _PACK_TILE
    return out


BIG = ("w_in", "w_glu", "w_out", "w_ffn_gate", "w_ffn_up", "w_ffn_down")
WEIGHTS = ("g_mix", "w_in", "q_gain", "k_gain", "rpb", "ssm_a_re", "ssm_a_im", "ssm_b_re", "ssm_b_im", "ssm_c_re",
           "ssm_c_im", "ssm_log_step", "ssm_d", "w_glu", "b_glu", "g_out_attn", "g_out_ssm", "w_out", "g_ffn",
           "w_ffn_gate", "w_ffn_up", "w_ffn_down")
SMALL = tuple(n for n in WEIGHTS if n not in BIG)
SMALL_PACKED = tuple(n for n in SMALL if n != "g_mix")
VECTORS = ("g_mix", "q_gain", "k_gain", "ssm_d", "b_glu", "g_out_attn", "g_out_ssm", "g_ffn")


def kernel(x, g_mix, w_in, q_gain, k_gain, rpb, ssm_a_re, ssm_a_im, ssm_b_re, ssm_b_im, ssm_c_re, ssm_c_im, ssm_log_step, ssm_d, w_glu, b_glu, g_out_attn, g_out_ssm, w_out, g_ffn, w_ffn_gate, w_ffn_up, w_ffn_down, loss_target, m_g_mix, m_w_in, m_q_gain, m_k_gain, m_rpb, m_ssm_a_re, m_ssm_a_im, m_ssm_b_re, m_ssm_b_im, m_ssm_c_re, m_ssm_c_im, m_ssm_log_step, m_ssm_d, m_w_glu, m_b_glu, m_g_out_attn, m_g_out_ssm, m_w_out, m_g_ffn, m_w_ffn_gate, m_w_ffn_up, m_w_ffn_down, v_g_mix, v_w_in, v_q_gain, v_k_gain, v_rpb, v_ssm_a_re, v_ssm_a_im, v_ssm_b_re, v_ssm_b_im, v_ssm_c_re, v_ssm_c_im, v_ssm_log_step, v_ssm_d, v_w_glu, v_b_glu, v_g_out_attn, v_g_out_ssm, v_w_out, v_g_ffn, v_w_ffn_gate, v_w_ffn_up, v_w_ffn_down):
    wts = dict(g_mix=g_mix, w_in=w_in, q_gain=q_gain, k_gain=k_gain, rpb=rpb, ssm_a_re=ssm_a_re, ssm_a_im=ssm_a_im,
               ssm_b_re=ssm_b_re, ssm_b_im=ssm_b_im, ssm_c_re=ssm_c_re, ssm_c_im=ssm_c_im, ssm_log_step=ssm_log_step,
               ssm_d=ssm_d, w_glu=w_glu, b_glu=b_glu, g_out_attn=g_out_attn, g_out_ssm=g_out_ssm, w_out=w_out,
               g_ffn=g_ffn, w_ffn_gate=w_ffn_gate, w_ffn_up=w_ffn_up, w_ffn_down=w_ffn_down)
    mom = dict(g_mix=m_g_mix, w_in=m_w_in, q_gain=m_q_gain, k_gain=m_k_gain, rpb=m_rpb, ssm_a_re=m_ssm_a_re,
               ssm_a_im=m_ssm_a_im, ssm_b_re=m_ssm_b_re, ssm_b_im=m_ssm_b_im, ssm_c_re=m_ssm_c_re,
               ssm_c_im=m_ssm_c_im, ssm_log_step=m_ssm_log_step, ssm_d=m_ssm_d, w_glu=m_w_glu, b_glu=m_b_glu,
               g_out_attn=m_g_out_attn, g_out_ssm=m_g_out_ssm, w_out=m_w_out, g_ffn=m_g_ffn,
               w_ffn_gate=m_w_ffn_gate, w_ffn_up=m_w_ffn_up, w_ffn_down=m_w_ffn_down)
    var = dict(g_mix=v_g_mix, w_in=v_w_in, q_gain=v_q_gain, k_gain=v_k_gain, rpb=v_rpb, ssm_a_re=v_ssm_a_re,
               ssm_a_im=v_ssm_a_im, ssm_b_re=v_ssm_b_re, ssm_b_im=v_ssm_b_im, ssm_c_re=v_ssm_c_re,
               ssm_c_im=v_ssm_c_im, ssm_log_step=v_ssm_log_step, ssm_d=v_ssm_d, w_glu=v_w_glu, b_glu=v_b_glu,
               g_out_attn=v_g_out_attn, g_out_ssm=v_g_out_ssm, w_out=v_w_out, g_ffn=v_g_ffn,
               w_ffn_gate=v_w_ffn_gate, w_ffn_up=v_w_ffn_up, w_ffn_down=v_w_ffn_down)
    ix, iy, ic = _mesh_place()
    me = _slab(ix, iy, ic)
    d_model = x.shape[-1]

    shard = {n: wts[n][0] for n in BIG}
    shard_b = {n: shard[n].astype(BF16) for n in BIG}
    w_in_full = _all_gather([shard_b["w_in"]], name="gather_w_in")[0]
    p = {n: (wts[n][0].reshape(1, -1) if n in VECTORS else wts[n][0]) for n in SMALL}

    loss, grad_x, small_gathered, g_g_mix, reduced = _local_step(x[0], loss_target[0], p, w_in_full,
                                                                 {n: shard_b[n] for n in BIG if n != "w_in"})
    loss = lax.psum(loss, ("x", "y", "c"))
    out = {}
    for n in BIG:
        slabs, from_sibling, from_chips = reduced[n]
        rows, cols = slabs.shape[1:]
        res = _adamw(shard[n].reshape(rows, cols), mom[n][0].reshape(rows, cols), var[n][0].reshape(rows, cols),
                     [(slabs, None), (from_sibling, 1), (from_chips, 3)], name=f"adamw_{n}", slab=me)
        out[n] = [r.reshape(wts[n].shape) for r in res]

    order = list(SMALL_PACKED)
    shapes = [wts[n].shape for n in order]
    res = _adamw(_pack([wts[n] for n in order]), _pack([mom[n] for n in order]), _pack([var[n] for n in order]),
                 [(small_gathered, N_DEV)], name="adamw_small", slab=me)
    for kind, buf in enumerate(res):
        for n, a in zip(order, _unpack(buf, shapes)):
            out.setdefault(n, [None] * 4)[kind] = a
    as_rows = lambda a: a.reshape(-1, 128)
    g_mix_all = _all_gather([as_rows(g_g_mix)], name="gather_g_mix")[0]
    res = _adamw(as_rows(wts["g_mix"]), as_rows(mom["g_mix"]), as_rows(var["g_mix"]), [(g_mix_all, N_DEV)],
                 name="adamw_g_mix", slab=me)
    out["g_mix"] = [r.reshape(wts["g_mix"].shape) for r in res]

    return (loss, grad_x[None], *[out[n][0] for n in WEIGHTS], *[out[n][1] for n in WEIGHTS],
            *[out[n][2] for n in WEIGHTS], *[out[n][3] for n in WEIGHTS])
```

```python
import functools
import math

import jax
import jax.numpy as jnp
from jax import lax
from jax.experimental import pallas as pl
from jax.experimental.pallas import tpu as pltpu

F32 = jnp.float32
BF16 = jnp.bfloat16

N_DEV = 8
GRID_W = 64
WIN_H = 8
WIN_W = 16
HEAD_DIM = 64
HEADS_PER_GROUP = 4
GROUP_LANES = HEADS_PER_GROUP * HEAD_DIM
SSM_C = 16
SSM_P = 64
GROUPS_PER_TILE = 8
U_TILE = GROUPS_PER_TILE * SSM_C
ST_TILE = GROUPS_PER_TILE * SSM_P
SUBLANES = 8
RMS_EPS = 1e-6
NEG_INF = -1e30
A_RE_MAX = -1e-4
ADAM_LR, ADAM_B1, ADAM_B2, ADAM_EPS, ADAM_WD, ADAM_STEP = 0.001, 0.9, 0.999, 1e-08, 0.01, 10
VMEM_LIMIT_V7X = 56 * 1024 * 1024
MESH = pl.DeviceIdType.MESH

_NN = (((1,), (0,)), ((), ()))
_NT = (((1,), (1,)), ((), ()))
_TN = (((0,), (0,)), ((), ()))
_DIMS = {"nn": _NN, "nt": _NT, "tn": _TN}


def _params(**kw):
    return pltpu.CompilerParams(vmem_limit_bytes=VMEM_LIMIT_V7X, **kw)


def _dot(a, b, dims=_NN):
    return lax.dot_general(a, b, dims, preferred_element_type=F32)


def _mm(a, b, *, name, grid, a_spec, b_spec, o_spec, o_shape, dims, k_axis=None, res=None, out_dtype=F32,
        exact=False, second=None, ride=None, groups=1):
    dn = _DIMS[dims]
    nk = 1 if k_axis is None else grid[k_axis]
    acc_shape = tuple(d for d in o_spec.block_shape if d is not None)
    n_in = 2 + (2 if second is not None else 0)

    def body(*refs):
        a_ref, b_ref = refs[:2]
        r_ref = refs[n_in] if res is not None else None
        o_ref, acc = refs[-2:]
        def product(x_ref, y_ref):
            if groups == 1:
                return _dot(x_ref[...].astype(BF16), y_ref[...].astype(BF16), dn)
            total, width = None, x_ref.shape[-1] // groups
            for s in range(groups):
                x = x_ref[s] if len(x_ref.shape) == 3 else x_ref[:, s * width:(s + 1) * width]
                t = _dot(x.astype(BF16), y_ref[s].astype(BF16), dn)
                total = t if total is None else total + t
            return total

        if exact:
            p = lax.dot_general(a_ref[...], b_ref[...], dn, precision=lax.Precision.HIGHEST,
                                preferred_element_type=F32)
        else:
            p = product(a_ref, b_ref)
        if second is not None:
            p = p + product(refs[2], refs[3])

        def finish(v):
            if r_ref is not None:
                v = v + r_ref[...].astype(F32)
            o_ref[...] = v.astype(out_dtype)

        if nk == 1:
            finish(p)
        else:
            k = pl.program_id(k_axis)

            @pl.when(k == 0)
            def _():
                acc[...] = p

            @pl.when(k > 0)
            def _():
                acc[...] += p

            @pl.when(k == nk - 1)
            def _():
                finish(acc[...])

    ins = [a, b] + (list(second) if second is not None else []) + ([res] if res is not None else [])
    in_specs = [a_spec, b_spec] * (n_in // 2) + ([o_spec] if res is not None else [])
    first, last = _grid_ends(grid)
    (out,), rode = _call(
        body, name=name, grid=grid, in_specs=in_specs, out_specs=[o_spec],
        out_shape=[jax.ShapeDtypeStruct(o_shape, out_dtype)],
        scratch_shapes=[pltpu.VMEM(acc_shape if nk > 1 else (SUBLANES, 128), F32)],
        args=ins, ride=ride, first=first, last=last)
    return out if ride is None else (out, rode)


def _tile(n, want):
    if n <= want:
        return n
    t = want
    while n % t:
        t //= 2
    return t


def _mm_plain(a, b, dims, *, name, res=None, out_dtype=F32, tm=512, tn=512, tk=512, exact=False, ride=None):
    if dims == "nn":
        (m, k), n = a.shape, b.shape[1]
    elif dims == "nt":
        (m, k), n = a.shape, b.shape[0]
    else:
        (k, m), n = a.shape, b.shape[1]
    tm, tn = _tile(m, tm), _tile(n, tn)
    if dims == "tn":
        tk = _tile(k, tk)
        grid = (m // tm, n // tn, k // tk)
        a_spec = pl.BlockSpec((tk, tm), lambda i, j, kk: (kk, i))
        b_spec = pl.BlockSpec((tk, tn), lambda i, j, kk: (kk, j))
        o_spec = pl.BlockSpec((tm, tn), lambda i, j, kk: (i, j))
        return _mm(a, b, name=name, grid=grid, a_spec=a_spec, b_spec=b_spec, o_spec=o_spec, o_shape=(m, n),
                   dims=dims, k_axis=2, res=res, out_dtype=out_dtype)
    grid = (n // tn, m // tm)
    a_spec = pl.BlockSpec((tm, k), lambda j, i: (i, 0))
    if dims == "nn":
        b_spec = pl.BlockSpec((k, tn), lambda j, i: (0, j))
    else:
        b_spec = pl.BlockSpec((tn, k), lambda j, i: (j, 0))
    o_spec = pl.BlockSpec((tm, tn), lambda j, i: (i, j))
    return _mm(a, b, name=name, grid=grid, a_spec=a_spec, b_spec=b_spec, o_spec=o_spec, o_shape=(m, n), dims=dims,
               res=res, out_dtype=out_dtype, exact=exact, ride=ride)


def _rowwise(fn, tiled, bcast, outs, accs=(), *, name, tm=256, flipped=(), ride=None):
    m = tiled[0].shape[0]
    tm = _tile(m, tm)
    n_t, n_b, n_o, n_f = len(tiled), len(bcast), len(outs), len(flipped)

    def body(*refs):
        ins = [r[...] for r in refs[: n_t + n_b]]
        o_refs = refs[n_t + n_b: n_t + n_b + n_o]
        f_refs = refs[n_t + n_b + n_o: n_t + n_b + n_o + n_f]
        a_refs = refs[n_t + n_b + n_o + n_f:]
        res = fn(*ins)
        if not isinstance(res, (tuple, list)):
            res = (res,)
        for r, v in zip(o_refs, res[:n_o]):
            r[...] = v.astype(r.dtype)
        for r, v in zip(f_refs, res[n_o:n_o + n_f]):
            r[...] = v.astype(F32).T.astype(r.dtype)
        first = pl.program_id(0) == 0
        for r, v in zip(a_refs, res[n_o + n_f:]):
            s = jnp.sum(v, axis=0, keepdims=True)

            @pl.when(first)
            def _():
                r[...] = s

            @pl.when(jnp.logical_not(first))
            def _():
                r[...] += s

    in_specs = [pl.BlockSpec((tm, t.shape[1]), lambda i: (i, 0)) for t in tiled]
    in_specs += [pl.BlockSpec(b.shape, lambda i, nd=b.ndim: (0,) * nd) for b in bcast]
    out_specs = [pl.BlockSpec((tm, n), lambda i: (i, 0)) for n, _ in outs]
    out_specs += [pl.BlockSpec((n, tm), lambda i, per=m // tm // g: (i // per, i % per)) for n, _, g in flipped]
    out_specs += [pl.BlockSpec((1, n), lambda i: (0, 0)) for n in accs]
    out_shape = [jax.ShapeDtypeStruct((m, n), dt) for n, dt in outs]
    out_shape += [jax.ShapeDtypeStruct((g * n, m // g), dt) for n, dt, g in flipped]
    out_shape += [jax.ShapeDtypeStruct((1, n), F32) for n in accs]
    first, last = _grid_ends((m // tm,))
    res, rode = _call(body, name=name, grid=(m // tm,), in_specs=in_specs, out_specs=out_specs, out_shape=out_shape,
                      scratch_shapes=[], args=list(tiled) + list(bcast), ride=ride, first=first, last=last)
    return res if ride is None else (res, rode)


def _rstd(x):
    return lax.rsqrt(jnp.mean(x * x, axis=-1, keepdims=True) + RMS_EPS)


def _rms_bwd(dh, x, g):
    xh = x * _rstd(x)
    dxh = dh * g
    dx = _rstd(x) * (dxh - xh * jnp.mean(dxh * xh, axis=-1, keepdims=True))
    return dx, dh * xh


def _sigmoid(x):
    return 1.0 / (1.0 + jnp.exp(-x))


_GELU_K = math.sqrt(2.0 / math.pi)
_GELU_C = 0.044715


def _gelu(x):
    return 0.5 * x * (1.0 + jnp.tanh(_GELU_K * (x + _GELU_C * x * x * x)))


def _gelu_grad(x):
    th = jnp.tanh(_GELU_K * (x + _GELU_C * x * x * x))
    return 0.5 * (1.0 + th) + 0.5 * x * (1.0 - th * th) * _GELU_K * (1.0 + 3.0 * _GELU_C * x * x)


class _Exchange:
    def __init__(self, arrays, outs, n_sems, sends, recvs=None, local=None, aliases=None):
        self.arrays, self.outs, self.n_sems = list(arrays), list(outs), n_sems
        self.sends, self.local, self.aliases = sends, local, aliases or {}
        self.recvs = recvs or (lambda i, o: [(k, dst) for k, _, dst, _ in sends(i, o)])

    def __add__(self, other):
        na, no, ns = len(self.arrays), len(self.outs), self.n_sems
        mine = lambda f: (lambda i, o: f(i[:na], o[:no]))
        shift = lambda f, at: (lambda i, o: [(k + ns,) + tuple(rest) for k, *rest in f(i[na:], o[no:])]) if at else None
        both = lambda f, g: (lambda i, o: f(i, o) + g(i, o))
        local = None
        if self.local or other.local:
            la = mine(self.local) if self.local else (lambda i, o: [])
            lb = (lambda i, o: other.local(i[na:], o[no:])) if other.local else (lambda i, o: [])
            local = both(la, lb)
        aliases = dict(self.aliases)
        aliases.update({na + i: no + o for i, o in other.aliases.items()})
        return _Exchange(self.arrays + other.arrays, self.outs + other.outs, ns + other.n_sems,
                         both(mine(self.sends), shift(other.sends, True)),
                         both(mine(self.recvs), shift(other.recvs, True)), local, aliases)

    def descriptors(self, in_refs, out_refs, send_sems, recv_sems, local_sems):
        me = _mesh_place()
        remote = lambda k, src, dst, to: pltpu.make_async_remote_copy(
            src_ref=src, dst_ref=dst, send_sem=send_sems.at[k], recv_sem=recv_sems.at[k], device_id=to,
            device_id_type=MESH)
        out = [remote(*s) for s in self.sends(in_refs, out_refs)]
        arrive = [remote(k, dst, dst, me) for k, dst in self.recvs(in_refs, out_refs)]
        own = [pltpu.make_async_copy(src, dst, local_sems.at[i])
               for i, (src, dst) in enumerate(self.local(in_refs, out_refs) if self.local else [])]
        return out, arrive, own

    def start(self, *refs):
        out, _, own = self.descriptors(*refs)
        for cp in own + out:
            cp.start()

    def finish(self, *refs):
        out, arrive, own = self.descriptors(*refs)
        for cp in arrive:
            cp.wait_recv()
        for cp in out:
            cp.wait_send()
        for cp in own:
            cp.wait()


def _call(body, *, name, grid, in_specs, out_specs, out_shape, scratch_shapes, args, ride=None, first=None, last=None):
    if ride is None:
        res = pl.pallas_call(body, name=name, grid=grid, in_specs=in_specs, out_specs=out_specs, out_shape=out_shape,
                             scratch_shapes=scratch_shapes, compiler_params=_params())(*args)
        return list(res), []
    n_in, n_out, n_scr = len(in_specs), len(out_specs), len(scratch_shapes)
    r_in, r_out = len(ride.arrays), len(ride.outs)

    def wrapped(*refs):
        ins, refs = refs[:n_in], refs[n_in:]
        x_in, refs = refs[:r_in], refs[r_in:]
        outs, refs = refs[:n_out], refs[n_out:]
        x_out, refs = refs[:r_out], refs[r_out:]
        scr, sems = refs[:n_scr], refs[n_scr:]

        @pl.when(first())
        def _():
            ride.start(x_in, x_out, *sems)

        body(*ins, *outs, *scr)

        @pl.when(last())
        def _():
            ride.finish(x_in, x_out, *sems)

    n_local = max(1, len(ride.arrays))
    res = pl.pallas_call(
        wrapped, name=name, grid=grid, in_specs=list(in_specs) + [_ANY] * r_in,
        out_specs=list(out_specs) + [_ANY] * r_out, out_shape=list(out_shape) + ride.outs,
        scratch_shapes=list(scratch_shapes) + [pltpu.SemaphoreType.DMA((ride.n_sems,)),
                                               pltpu.SemaphoreType.DMA((ride.n_sems,)),
                                               pltpu.SemaphoreType.DMA((n_local,))],
        input_output_aliases={n_in + i: n_out + o for i, o in ride.aliases.items()},
        compiler_params=_params(has_side_effects=True),
    )(*args, *ride.arrays)
    return list(res[:n_out]), list(res[n_out:])


def _gather_first(shards):
    def sends(i, o):
        x, y, c = _mesh_place()
        peers = [(x, y, 1 - c)] + [(px, py, c) for px, py in _chips(x, y)[1:]]
        return [(4 * w + k, i[w], o[w].at[_slab(x, y, c)], to) for w in range(len(i)) for k, to in enumerate(peers)]

    def recvs(i, o):
        x, y, c = _mesh_place()
        peers = [(x, y, 1 - c)] + [(px, py, c) for px, py in _chips(x, y)[1:]]
        return [(4 * w + k, o[w].at[_slab(*peer)]) for w in range(len(i)) for k, peer in enumerate(peers)]

    def local(i, o):
        return [(i[w], o[w].at[_slab(*_mesh_place())]) for w in range(len(i))]

    outs = [jax.ShapeDtypeStruct((N_DEV,) + a.shape, a.dtype) for a in shards]
    return _Exchange(shards, outs, 4 * len(shards), sends, recvs, local)


def _gather_second(gathered):
    def sends(i, o):
        x, y, c = _mesh_place()
        return [(3 * w + j, o[w].at[_slab(px, py, c)], o[w].at[_slab(px, py, c)], (x, y, 1 - c))
                for w in range(len(o)) for j, (px, py) in enumerate(_chips(x, y)[1:])]

    def recvs(i, o):
        x, y, c = _mesh_place()
        return [(3 * w + j, o[w].at[_slab(px, py, 1 - c)])
                for w in range(len(o)) for j, (px, py) in enumerate(_chips(x, y)[1:])]

    outs = [jax.ShapeDtypeStruct(a.shape, a.dtype) for a in gathered]
    return _Exchange(gathered, outs, 3 * len(gathered), sends, recvs, aliases={w: w for w in range(len(gathered))})


def _reduce_sibling(slabs):
    def sends(i, o):
        x, y, c = _mesh_place()
        return [(4 * w + k, i[w].at[_slab(px, py, 1 - c)], o[w].at[k], (x, y, 1 - c))
                for w in range(len(i)) for k, (px, py) in enumerate(_chips(x, y))]

    outs = [jax.ShapeDtypeStruct((4,) + a.shape[1:], a.dtype) for a in slabs]
    return _Exchange(slabs, outs, 4 * len(slabs), sends)


def _reduce_chips(partials):
    def sends(i, o):
        x, y, c = _mesh_place()
        return [(3 * w + k, i[w].at[k], o[w].at[k], (px, py, c))
                for w in range(len(i)) for k, (px, py) in enumerate(_chips(x, y)[1:])]

    outs = [jax.ShapeDtypeStruct(a.shape, a.dtype) for a in partials]
    return _Exchange(partials, outs, 3 * len(partials), sends)


def _head_masks():
    lane_head = lax.broadcasted_iota(jnp.int32, (1, GROUP_LANES), 1) // HEAD_DIM
    return [(lane_head == h).astype(F32) for h in range(HEADS_PER_GROUP)]


def _head_block_diag():
    r = lax.broadcasted_iota(jnp.int32, (GROUP_LANES, GROUP_LANES), 0) // HEAD_DIM
    c = lax.broadcasted_iota(jnp.int32, (GROUP_LANES, GROUP_LANES), 1) // HEAD_DIM
    return (r == c).astype(BF16)


def _head_mean(x, bd):
    hi = x.astype(BF16)
    lo = (x - hi.astype(F32)).astype(BF16)
    return (_dot(hi, bd) + _dot(lo, bd)) * (1.0 / HEAD_DIM)


def _stack_heads(x, masks):
    return jnp.concatenate([x * m for m in masks], axis=0)


def _unstack_heads(xs, masks):
    out = xs[0:GRID_W] * masks[0]
    for h in range(1, HEADS_PER_GROUP):
        out = out + xs[h * GRID_W:(h + 1) * GRID_W] * masks[h]
    return out


def _row_start(r, rows):
    return jnp.clip(r - WIN_H // 2, 0, rows - WIN_H)


ROWS_PER_STEP = 2


def _attn_common_specs(seq, n_hg, rows):
    win_keys = WIN_H * GRID_W
    q_spec = pl.BlockSpec((ROWS_PER_STEP * GRID_W, GROUP_LANES), lambda g, r: (r, g))
    k_spec = pl.BlockSpec((seq, GROUP_LANES), lambda g, r: (0, n_hg + g))
    v_spec = pl.BlockSpec((seq, GROUP_LANES), lambda g, r: (0, 2 * n_hg + g))
    gain_spec = pl.BlockSpec((1, GROUP_LANES), lambda g, r: (0, 0))

    def variant(r):
        return _row_start(r, rows) - r + (WIN_H - 1)

    bias_specs = [pl.BlockSpec((None, None, HEADS_PER_GROUP, GRID_W, win_keys),
                               lambda g, r, h=h: (g, variant(ROWS_PER_STEP * r + h), 0, 0, 0))
                  for h in range(ROWS_PER_STEP)]
    return q_spec, k_spec, v_spec, gain_spec, bias_specs, variant


def _attn_prepare_kv(k_ref, v_ref, kg, kn_scr, vb_scr, bd, seq):
    chunk = _tile(seq, 512)

    def step(c, carry):
        rows = pl.ds(pl.multiple_of(c * chunk, chunk), chunk)
        k = k_ref[rows, :]
        kn_scr[rows, :] = (k * lax.rsqrt(_head_mean(k * k, bd) + RMS_EPS) * kg).astype(BF16)
        vb_scr[rows, :] = v_ref[rows, :].astype(BF16)
        return carry

    lax.fori_loop(0, seq // chunk, step, 0)


def _attn_probs(qn, kw, bias, masks):
    qs = _stack_heads(qn, masks).astype(BF16)
    s = _dot(qs, kw, _NT) * (1.0 / math.sqrt(HEAD_DIM)) + bias
    m = jnp.max(s, axis=-1, keepdims=True)
    p = jnp.exp(s - m)
    return qs, p / jnp.sum(p, axis=-1, keepdims=True)


def _grid_ends(grid):
    first = lambda: functools.reduce(jnp.logical_and, [pl.program_id(a) == 0 for a in range(len(grid))])
    last = lambda: functools.reduce(jnp.logical_and, [pl.program_id(a) == n - 1 for a, n in enumerate(grid)])
    return first, last


def _attn_fwd(z, qg4, kg4, btab, ride=None):
    seq = z.shape[0]
    a_width = btab.shape[0] * GROUP_LANES
    n_hg, rows, win_keys = btab.shape[0], seq // GRID_W, WIN_H * GRID_W
    q_spec, k_spec, v_spec, gain_spec, bias_specs, _ = _attn_common_specs(seq, n_hg, rows)
    grid = (n_hg, rows // ROWS_PER_STEP)

    def body(q_ref, k_ref, v_ref, qg_ref, kg_ref, *rest):
        b_refs, (o_ref, kn_scr, vb_scr) = rest[:ROWS_PER_STEP], rest[ROWS_PER_STEP:]
        bd, masks = _head_block_diag(), _head_masks()

        @pl.when(pl.program_id(1) == 0)
        def _():
            _attn_prepare_kv(k_ref, v_ref, kg_ref[...], kn_scr, vb_scr, bd, seq)

        for h in range(ROWS_PER_STEP):
            r = ROWS_PER_STEP * pl.program_id(1) + h
            mine = slice(h * GRID_W, (h + 1) * GRID_W)
            win = pl.ds(pl.multiple_of(_row_start(r, rows) * GRID_W, GRID_W), win_keys)
            q = q_ref[mine, :]
            qn = q * lax.rsqrt(_head_mean(q * q, bd) + RMS_EPS) * qg_ref[...]
            bias = b_refs[h][...].reshape(HEADS_PER_GROUP * GRID_W, win_keys)
            _, p = _attn_probs(qn, kn_scr[win, :], bias, masks)
            o_ref[mine, :] = _unstack_heads(_dot(p.astype(BF16), vb_scr[win, :]), masks)

    first, last = _grid_ends(grid)
    (ya,), rode = _call(
        body, name="attn_fwd", grid=grid,
        in_specs=[q_spec, k_spec, v_spec, gain_spec, gain_spec] + bias_specs,
        out_specs=[pl.BlockSpec((ROWS_PER_STEP * GRID_W, GROUP_LANES), lambda g, r: (r, g))],
        out_shape=[jax.ShapeDtypeStruct((seq, a_width), F32)],
        scratch_shapes=[pltpu.VMEM((seq, GROUP_LANES), BF16), pltpu.VMEM((seq, GROUP_LANES), BF16)],
        args=(z, z, z, qg4, kg4) + (btab,) * ROWS_PER_STEP, ride=ride, first=first, last=last)
    return ya, rode


def _attn_bwd(z, d_out, qg4, kg4, btab, ride=None):
    seq = z.shape[0]
    n_hg, rows, win_keys = btab.shape[0], seq // GRID_W, WIN_H * GRID_W
    a_width = n_hg * GROUP_LANES
    q_spec, k_spec, v_spec, gain_spec, bias_specs, variant = _attn_common_specs(seq, n_hg, rows)
    scale = 1.0 / math.sqrt(HEAD_DIM)
    grid = (n_hg, rows // ROWS_PER_STEP)

    def body(q_ref, k_ref, v_ref, do_ref, qg_ref, kg_ref, *rest):
        b_refs, rest = rest[:ROWS_PER_STEP], rest[ROWS_PER_STEP:]
        dq_ref, dk_out, dv_out, db_ref, dqg_ref, dkg_ref, kn_scr, vb_scr, dk_ref, dv_ref = rest
        bd, masks = _head_block_diag(), _head_masks()

        @pl.when(pl.program_id(1) == 0)
        def _():
            _attn_prepare_kv(k_ref, v_ref, kg_ref[...], kn_scr, vb_scr, bd, seq)
            dk_ref[...] = jnp.zeros_like(dk_ref)
            dv_ref[...] = jnp.zeros_like(dv_ref)
            db_ref[...] = jnp.zeros_like(db_ref)
            dqg_ref[...] = jnp.zeros_like(dqg_ref)

        qg = qg_ref[...]
        for h in range(ROWS_PER_STEP):
            r = ROWS_PER_STEP * pl.program_id(1) + h
            mine = slice(h * GRID_W, (h + 1) * GRID_W)
            win = pl.ds(pl.multiple_of(_row_start(r, rows) * GRID_W, GRID_W), win_keys)
            q = q_ref[mine, :]
            rq = lax.rsqrt(_head_mean(q * q, bd) + RMS_EPS)
            qh = q * rq
            kw, vw = kn_scr[win, :], vb_scr[win, :]
            bias = b_refs[h][...].reshape(HEADS_PER_GROUP * GRID_W, win_keys)
            qs, p = _attn_probs(qh * qg, kw, bias, masks)
            dos = _stack_heads(do_ref[mine, :], masks).astype(BF16)
            dp = _dot(dos, vw, _NT)
            ds = p * (dp - jnp.sum(p * dp, axis=-1, keepdims=True))
            db_ref[variant(r)] += ds.reshape(HEADS_PER_GROUP, GRID_W, win_keys)
            dsb = ds.astype(BF16)
            dqn = _unstack_heads(_dot(dsb, kw), masks) * scale
            dk_ref[win, :] += _dot(dsb, qs, _TN) * scale
            dv_ref[win, :] += _dot(p.astype(BF16), dos, _TN)
            dqg_ref[...] += jnp.sum(dqn * qh, axis=0, keepdims=True)
            dqh = dqn * qg
            dq_ref[mine, :] = (rq * (dqh - qh * _head_mean(dqh * qh, bd))).astype(BF16)

        @pl.when(pl.program_id(1) == grid[1] - 1)
        def _():
            chunk = _tile(seq, 512)
            kg = kg_ref[...]

            def step(c, dkg):
                rws = pl.ds(pl.multiple_of(c * chunk, chunk), chunk)
                k = k_ref[rws, :]
                rk = lax.rsqrt(_head_mean(k * k, bd) + RMS_EPS)
                kh = k * rk
                dkn = dk_ref[rws, :]
                dkh = dkn * kg
                dk_out[rws, :] = (rk * (dkh - kh * _head_mean(dkh * kh, bd))).astype(BF16)
                dv_out[rws, :] = dv_ref[rws, :].astype(BF16)
                return dkg + jnp.sum(dkn * kh, axis=0, keepdims=True)

            dkg_ref[...] = lax.fori_loop(0, seq // chunk, step, jnp.zeros((1, GROUP_LANES), F32))

    col_spec = pl.BlockSpec((seq, GROUP_LANES), lambda g, r: (0, g))
    gsum_spec = pl.BlockSpec((None, 1, GROUP_LANES), lambda g, r: (g, 0, 0))
    first, last = _grid_ends(grid)
    rows_spec = pl.BlockSpec((ROWS_PER_STEP * GRID_W, GROUP_LANES), lambda g, r: (r, g))
    return _call(
        body, name="attn_bwd", grid=grid,
        in_specs=[q_spec, k_spec, v_spec, rows_spec, gain_spec, gain_spec] + bias_specs,
        out_specs=[rows_spec, col_spec, col_spec,
                   pl.BlockSpec((None, WIN_H, HEADS_PER_GROUP, GRID_W, win_keys), lambda g, r: (g, 0, 0, 0, 0)),
                   gsum_spec, gsum_spec],
        out_shape=[jax.ShapeDtypeStruct((seq, a_width), BF16)] * 3
        + [jax.ShapeDtypeStruct(btab.shape, F32)]
        + [jax.ShapeDtypeStruct((n_hg, 1, GROUP_LANES), F32)] * 2,
        scratch_shapes=[pltpu.VMEM((seq, GROUP_LANES), BF16), pltpu.VMEM((seq, GROUP_LANES), BF16),
                        pltpu.VMEM((seq, GROUP_LANES), F32), pltpu.VMEM((seq, GROUP_LANES), F32)],
        args=(z, z, z, d_out, qg4, kg4) + (btab,) * ROWS_PER_STEP, ride=ride, first=first, last=last)


def _bias_index():
    c = jnp.arange(GRID_W)
    col_start = jnp.clip(c - WIN_W // 2, 0, GRID_W - WIN_W)
    col_in = (c[None, :] >= col_start[:, None]) & (c[None, :] < col_start[:, None] + WIN_W)
    dc = jnp.clip(c[None, :] - c[:, None], -(WIN_W - 1), WIN_W - 1) + (WIN_W - 1)
    dr = jnp.arange(WIN_H)[:, None] + jnp.arange(WIN_H)[None, :]
    return col_in, dc, dr


def _bias_table(rpb):
    col_in, dc, _ = _bias_index()
    n_h = rpb.shape[0]
    n_hg = n_h // HEADS_PER_GROUP
    spread = ((jnp.arange(128)[:, None] == dc.reshape(1, -1)) & col_in.reshape(1, -1)).astype(F32)
    rows = jnp.stack([rpb[:, v:v + WIN_H] for v in range(WIN_H)], axis=1)
    rows = jnp.pad(rows, ((0, 0), (0, 0), (0, 0), (0, 128 - rows.shape[-1]))).reshape(n_h * WIN_H * WIN_H, 128)
    tab = _mm_plain(rows, spread, "nn", name="rpb_spread", tm=256, tn=2048, exact=True)
    tab = jnp.where(col_in.reshape(1, -1), tab, NEG_INF)
    tab = tab.reshape(n_hg, HEADS_PER_GROUP, WIN_H, WIN_H, GRID_W, GRID_W).transpose(0, 2, 1, 4, 3, 5)
    return tab.reshape(n_hg, WIN_H, HEADS_PER_GROUP, GRID_W, WIN_H * GRID_W)


def _bias_grad(dtab, n_h):
    col_in, dc, _ = _bias_index()
    onehot = (dc.reshape(-1, 1) == jnp.arange(128)[None, :]) & col_in.reshape(-1, 1)
    n_hg = n_h // HEADS_PER_GROUP
    d = dtab.reshape(n_hg, WIN_H, HEADS_PER_GROUP, GRID_W, WIN_H, GRID_W).transpose(0, 2, 1, 4, 3, 5)
    d = d.reshape(n_h * WIN_H * WIN_H, GRID_W * GRID_W)
    diag = _mm_plain(d, onehot.astype(BF16), "nn", name="rpb_diag_sum", tm=256, tn=128)
    diag = diag.reshape(n_h, WIN_H, WIN_H, 128)[..., : 2 * WIN_W - 1]
    out = jnp.zeros((n_h, 2 * WIN_H - 1, 2 * WIN_W - 1), F32)
    for v in range(WIN_H):
        out = out.at[:, v:v + WIN_H].add(diag[:, v])
    return out


def _cmul(ar, ai, br, bi):
    return ar * br - ai * bi, ar * bi + ai * br


def _s5_discretize(a_re, a_im, dt, b_re, b_im):
    c = b_re.shape[1]

    def fn(are, aim, dt_, bre, bim):
        lr, li = jnp.minimum(are, A_RE_MAX), aim
        mag = jnp.exp(lr * dt_)
        l1r, l1i = mag * jnp.cos(li * dt_), mag * jnp.sin(li * dt_)
        den = lr * lr + li * li
        nr, ni = l1r - 1.0, l1i
        cr, ci = (nr * lr + ni * li) / den, (ni * lr - nr * li) / den
        bbr, bbi = _cmul(cr, ci, bre, bim)
        shape = (are.shape[0], SUBLANES)
        lane = lax.broadcasted_iota(jnp.int32, shape, 1)
        pr, pi = l1r, l1i
        acc_r, acc_i = jnp.zeros(shape, F32), jnp.zeros(shape, F32)
        for k in range(SUBLANES):
            acc_r = jnp.where(lane == k, pr, acc_r)
            acc_i = jnp.where(lane == k, pi, acc_i)
            pr, pi = _cmul(pr, pi, l1r, l1i)
        return acc_r, acc_i, cr, ci, bbr, bbi

    return _rowwise(fn, [a_re, a_im, dt, b_re, b_im], [],
                    [(SUBLANES, F32), (SUBLANES, F32), (1, F32), (1, F32), (c, F32), (c, F32)],
                    name="s5_discretize", tm=1024)


def _s5_param_grads(a_re, a_im, dt, b_re, b_im, l1r, l1i, cr, ci, bbr, bbi, r_re, r_im, gb_re, gb_im):
    c = b_re.shape[1]

    def fn(are, aim, dt_, bre, bim, l1r_, l1i_, cr_, ci_, bbr_, bbi_, rr, ri, gbr, gbi):
        lr, li = jnp.minimum(are, A_RE_MAX), aim
        den = lr * lr + li * li
        dbr, dbi = _cmul(cr_, -ci_, gbr, gbi)
        gcr, gci = _cmul(bre, -bim, gbr, gbi)
        gcr, gci = jnp.sum(gcr, axis=1, keepdims=True), jnp.sum(gci, axis=1, keepdims=True)
        qr, qi = _cmul(bbr_, -bbi_, gbr, gbi)
        qr = rr - jnp.sum(qr, axis=1, keepdims=True)
        qi = ri - jnp.sum(qi, axis=1, keepdims=True)
        tr, ti = _cmul(gcr, gci, lr / den, li / den)
        ur, ui = _cmul(l1r_, -l1i_, tr, ti)
        gwr, gwi = qr + ur, qi + ui
        vr, vi = _cmul(cr_, -ci_, lr / den, li / den)
        vr, vi = _cmul(gcr, gci, vr, vi)
        glr, gli = dt_ * gwr - vr, dt_ * gwi - vi
        return jnp.where(are < A_RE_MAX, glr, 0.0), gli, (gwr * lr + gwi * li) * dt_, dbr, dbi

    return _rowwise(fn, [a_re, a_im, dt, b_re, b_im, l1r, l1i, cr, ci, bbr, bbi, r_re, r_im, gb_re, gb_im], [],
                    [(1, F32), (1, F32), (1, F32), (c, F32), (c, F32)], name="s5_param_grads", tm=1024)


def _s5_scan(v, win_re, win_im, tabs, wo_re, wo_im, *, reverse, name, t_chunk=256, ride=None):
    seq, width = v.shape
    n_tiles, n_state = width // U_TILE, width * (SSM_P // SSM_C)
    t_chunk = _tile(seq, t_chunk)
    n_chunks, n_blk = seq // t_chunk, t_chunk // SUBLANES
    last_row = 0 if reverse else SUBLANES - 1

    def chunk_of(j):
        return (n_chunks - 1 - j) if reverse else j

    def body(v_ref, wir_ref, wii_ref, tab_ref, wor_ref, woi_ref, sr_ref, si_ref, y_ref, carry, wr, wi):
        @pl.when(pl.program_id(0) == 0)
        def _():
            carry[...] = jnp.zeros_like(carry)

        for jt in range(n_tiles):
            ls = slice(jt * ST_TILE, (jt + 1) * ST_TILE)
            us = slice(jt * U_TILE, (jt + 1) * U_TILE)
            vj = v_ref[:, us].astype(BF16)
            consts = [tab_ref[k, :, ls] for k in range(8)]
            xr = _dot(vj, wir_ref[jt]).reshape(n_blk, SUBLANES, ST_TILE)
            xi = _dot(vj, wii_ref[jt]).reshape(n_blk, SUBLANES, ST_TILE)
            for s, k in enumerate((1, 2, 4)):
                sh = (SUBLANES - k) if reverse else k
                tr, ti = pltpu.roll(xr, sh, 1), pltpu.roll(xi, sh, 1)
                lr, li = consts[2 * s][None], consts[2 * s + 1][None]
                xr, xi = xr + lr * tr - li * ti, xi + lr * ti + li * tr
            wr[...] = xr.reshape(t_chunk, ST_TILE)
            wi[...] = xi.reshape(t_chunk, ST_TILE)

            def blk(b, c, consts=consts):
                cr, ci = c
                bb = (n_blk - 1 - b) if reverse else b
                rows = pl.ds(pl.multiple_of(bb * SUBLANES, SUBLANES), SUBLANES)
                lr, li = consts[6], consts[7]
                xr = wr[rows, :] + lr * cr - li * ci
                xi = wi[rows, :] + lr * ci + li * cr
                wr[rows, :], wi[rows, :] = xr, xi
                shape = (SUBLANES, ST_TILE)
                return (jnp.broadcast_to(xr[last_row:last_row + 1], shape),
                        jnp.broadcast_to(xi[last_row:last_row + 1], shape))

            cr, ci = lax.fori_loop(0, n_blk, blk, (carry[0, :, ls], carry[1, :, ls]), unroll=2)
            carry[0, :, ls], carry[1, :, ls] = cr, ci
            xr_b, xi_b = wr[...].astype(BF16), wi[...].astype(BF16)
            sr_ref[:, ls], si_ref[:, ls] = xr_b, xi_b
            y_ref[:, us] = _dot(xr_b, wor_ref[jt]) + _dot(xi_b, woi_ref[jt])

    whole = lambda a: pl.BlockSpec(a.shape, lambda j, nd=a.ndim: (0,) * nd)
    st_spec = pl.BlockSpec((t_chunk, n_state), lambda j: (chunk_of(j), 0))
    v_spec = pl.BlockSpec((t_chunk, width), lambda j: (chunk_of(j), 0))
    first, last = _grid_ends((n_chunks,))
    return _call(
        body, name=name, grid=(n_chunks,),
        in_specs=[v_spec, whole(win_re), whole(win_im), whole(tabs), whole(wo_re), whole(wo_im)],
        out_specs=[st_spec, st_spec, v_spec],
        out_shape=[jax.ShapeDtypeStruct((seq, n_state), BF16)] * 2 + [jax.ShapeDtypeStruct((seq, width), F32)],
        scratch_shapes=[pltpu.VMEM((2, SUBLANES, n_state), F32), pltpu.VMEM((t_chunk, ST_TILE), F32),
                        pltpu.VMEM((t_chunk, ST_TILE), F32)],
        args=(v, win_re, win_im, tabs, wo_re, wo_im), ride=ride, first=first, last=last)


def _s5_reduce(x_re, x_im, a_re, a_im, u, dy, *, name, t_chunk=512, ride=None):
    seq, n_state = x_re.shape
    width = u.shape[1]
    n_tiles = width // U_TILE
    t_chunk = _tile(seq, t_chunk)

    def body(xr_ref, xi_ref, ar_ref, ai_ref, u_ref, dy_ref, rr_ref, ri_ref, gbr_ref, gbi_ref, gcr_ref, gci_ref):
        xrb, xib, arb, aib = xr_ref[...], xi_ref[...], ar_ref[...], ai_ref[...]
        xr, xi, ar, ai = xrb.astype(F32), xib.astype(F32), arb.astype(F32), aib.astype(F32)
        ub, dyb = u_ref[...].astype(BF16), dy_ref[...].astype(BF16)
        parts = (jnp.sum(ar * xr + ai * xi, axis=0, keepdims=True), jnp.sum(ai * xr - ar * xi, axis=0, keepdims=True),
                 _dot(arb, ub, _TN), _dot(aib, ub, _TN), _dot(xrb, dyb, _TN), _dot(xib, dyb, _TN))
        first = pl.program_id(1) == 0
        for ref, val in zip((rr_ref, ri_ref, gbr_ref, gbi_ref, gcr_ref, gci_ref), parts):
            @pl.when(first)
            def _():
                ref[...] = val

            @pl.when(jnp.logical_not(first))
            def _():
                ref[...] += val

    st_spec = pl.BlockSpec((t_chunk, ST_TILE), lambda j, t: (t, j))
    u_spec = pl.BlockSpec((t_chunk, U_TILE), lambda j, t: (t, j))
    r_spec = pl.BlockSpec((1, ST_TILE), lambda j, t: (0, j))
    g_spec = pl.BlockSpec((None, ST_TILE, U_TILE), lambda j, t: (j, 0, 0))
    first, last = _grid_ends((n_tiles, seq // t_chunk))
    return _call(
        body, name=name, grid=(n_tiles, seq // t_chunk),
        in_specs=[st_spec] * 4 + [u_spec] * 2,
        out_specs=[r_spec, r_spec] + [g_spec] * 4,
        out_shape=[jax.ShapeDtypeStruct((1, n_state), F32)] * 2
        + [jax.ShapeDtypeStruct((n_tiles, ST_TILE, U_TILE), F32)] * 4,
        scratch_shapes=[], args=(x_re, x_im, a_re, a_im, u, dy), ride=ride, first=first, last=last)


def _block_diag_in(ms):
    m = jnp.stack(ms)
    n, g, c, p = m.shape
    m5 = m.reshape(n, g // GROUPS_PER_TILE, GROUPS_PER_TILE, c, p)
    eye = jnp.eye(GROUPS_PER_TILE, dtype=m.dtype)
    out = m5[:, :, :, :, None, :] * eye[None, None, :, None, :, None]
    return out.astype(BF16).reshape(n, g // GROUPS_PER_TILE, GROUPS_PER_TILE * c, GROUPS_PER_TILE * p)


def _block_diag_take(m, c, p):
    t = m.shape[0]
    m5 = m.reshape(t, GROUPS_PER_TILE, p, GROUPS_PER_TILE, c)
    idx = jnp.arange(GROUPS_PER_TILE)
    return m5[:, idx, :, idx, :].transpose(1, 0, 2, 3).reshape(t * GROUPS_PER_TILE, p, c)


def _scan_tables(pw_re, pw_im, reverse):
    row = jnp.arange(SUBLANES)[:, None]
    tabs = []
    for k in (1, 2, 4):
        keep = (row <= SUBLANES - 1 - k) if reverse else (row >= k)
        tabs += [jnp.where(keep, pw_re[k - 1][None, :], 0.0), jnp.where(keep, pw_im[k - 1][None, :], 0.0)]
    order = jnp.arange(SUBLANES)[::-1] if reverse else jnp.arange(SUBLANES)
    tabs += [pw_re[order], pw_im[order]]
    return jnp.stack(tabs)


def _partial_sums(slabs, from_sibling, names):
    x, y, c = _mesh_place()
    theirs = jnp.stack([_slab(px, py, c) for px, py in _chips(x, y)[1:]]).astype(jnp.int32)
    out = []
    for s, f, n in zip(slabs, from_sibling, names):
        rows, cols = s.shape[1:]
        tr = _tile(rows, 512)

        def body(idx_ref, a_ref, b_ref, o_ref):
            o_ref[...] = (a_ref[...] + b_ref[...]).astype(BF16)

        out.append(pl.pallas_call(
            body, name=f"reduce_add_{n}",
            grid_spec=pltpu.PrefetchScalarGridSpec(
                num_scalar_prefetch=1, grid=(3, rows // tr),
                in_specs=[pl.BlockSpec((None, tr, cols), lambda k, i, idx: (idx[k], i, 0)),
                          pl.BlockSpec((None, tr, cols), lambda k, i, idx: (k + 1, i, 0))],
                out_specs=pl.BlockSpec((None, tr, cols), lambda k, i, idx: (k, i, 0))),
            out_shape=jax.ShapeDtypeStruct((3, rows, cols), BF16), compiler_params=_params(),
        )(theirs, s, f))
    return out


def _local_step(x, target, p, w_in, shards):
    seq, d_model = x.shape
    a_width = p["g_out_attn"].shape[-1]
    s_width = p["g_out_ssm"].shape[-1]
    n_heads = a_width // HEAD_DIM
    n_hg = n_heads // HEADS_PER_GROUP
    n_groups = s_width // SSM_C
    n_sh, _, in_sh = w_in.shape
    f_sh = shards["w_ffn_gate"].shape[-1]
    w = {"w_in": w_in}
    slab3 = lambda g, n: g.reshape(N_DEV, -1, shards[n].shape[-1])
    t2, t1 = _tile(seq, 2048), _tile(seq, 1024)
    n2, n1 = seq // t2, seq // t1

    twice = lambda f: (lambda *a: (f(*a),) * 2)
    h1, h1_t = _rowwise(twice(x_norm), [x], [p["g_mix"]], [(d_model, BF16)], flipped=[(d_model, BF16, 1)],
                        name="rms_mix")
    z = _mm(h1, w["w_in"], name="in_proj", grid=(n2, n_sh),
            a_spec=pl.BlockSpec((t2, d_model), lambda i, j: (i, 0)),
            b_spec=pl.BlockSpec((None, d_model, in_sh), lambda i, j: (j, 0, 0)),
            o_spec=pl.BlockSpec((t2, in_sh), lambda i, j: (i, j)), o_shape=(seq, n_sh * in_sh), dims="nn")
    qg4 = jnp.tile(p["q_gain"], (1, HEADS_PER_GROUP))
    kg4 = jnp.tile(p["k_gain"], (1, HEADS_PER_GROUP))
    btab = _bias_table(p["rpb"])
    ya, got_a = _attn_fwd(z, qg4, kg4, btab, ride=_gather_first([shards["w_ffn_gate"], shards["w_ffn_up"]]))
    u = z[:, 3 * a_width:]

    n_col = 2 * n_groups * SSM_P
    col = lambda a: a.reshape(n_col, 1)
    a_re_c, a_im_c = col(p["ssm_a_re"]), col(p["ssm_a_im"])
    dt_c = col(jnp.broadcast_to(jnp.exp(p["ssm_log_step"])[:, :, None], (2, n_groups, SSM_P)))
    b_re_c, b_im_c = p["ssm_b_re"].reshape(n_col, SSM_C), p["ssm_b_im"].reshape(n_col, SSM_C)
    pw_re, pw_im, cf_re, cf_im, bb_re, bb_im = _s5_discretize(a_re_c, a_im_c, dt_c, b_re_c, b_im_c)
    n_state = n_groups * SSM_P
    pw_re = pw_re.reshape(2, n_state, SUBLANES).transpose(0, 2, 1)
    pw_im = pw_im.reshape(2, n_state, SUBLANES).transpose(0, 2, 1)
    bb_re4, bb_im4 = bb_re.reshape(2, n_groups, SSM_P, SSM_C), bb_im.reshape(2, n_groups, SSM_P, SSM_C)
    c_re, c_im = p["ssm_c_re"], p["ssm_c_im"]
    t21 = lambda a: a.transpose(0, 2, 1)
    maps_in = _block_diag_in([m for d in range(2) for m in (t21(bb_re4[d]), t21(bb_im4[d]), c_re[d], -c_im[d])])
    maps_out = _block_diag_in([m for d in range(2) for m in (t21(c_re[d]), -t21(c_im[d]), bb_re4[d], bb_im4[d])])
    fwd, bwd_in = [], []
    got_b = None
    for d in range(2):
        rev = d == 1
        tabs = _scan_tables(pw_re[d], pw_im[d], rev)
        if d == 0:
            ride = _gather_first([shards["w_glu"], shards["w_out"]])
        else:
            ride = _gather_second(got_a + got_b) + _gather_first([shards["w_ffn_down"]])
        (xs_re, xs_im, y_d), got = _s5_scan(u, maps_in[4 * d], maps_in[4 * d + 1], tabs, maps_out[4 * d],
                                            maps_out[4 * d + 1], reverse=rev, name=f"s5_fwd_{d}", ride=ride)
        if d == 0:
            got_b = got
        fwd.append((xs_re, xs_im, y_d))
        bwd_in.append((maps_in[4 * d + 2], maps_in[4 * d + 3], _scan_tables(pw_re[d], -pw_im[d], not rev),
                       maps_out[4 * d + 2], maps_out[4 * d + 3]))
    w["w_gate"], w["w_up"], w_glu_full, w_out_full, w_down_first = got
    w["w_glu"] = w_glu_full.reshape(-1, s_width)
    w["w_out"] = w_out_full.reshape(-1, d_model)

    ypre, yg, yg_t = _rowwise(s5_mid, [fwd[0][2], fwd[1][2], u], [p["ssm_d"]], [(s_width, F32), (s_width, F32)],
                              flipped=[(s_width, BF16, 1)], name="s5_skip_gelu")
    t_glu = _mm_plain(yg, w["w_glu"], "nn", name="glu_proj", tn=s_width)
    y_cat, y_cat_t = _rowwise(twice(mix_out_fwd), [ya, yg, t_glu], [p["b_glu"], p["g_out_attn"], p["g_out_ssm"]],
                              [(a_width + s_width, BF16)], flipped=[(a_width + s_width, BF16, 1)], name="mix_out")
    x1, (w["w_down"],) = _mm_plain(y_cat, w["w_out"], "nn", name="out_proj", res=x, tn=2048,
                                   ride=_gather_second([w_down_first]))

    h2, h2_t = _rowwise(twice(x_norm), [x1], [p["g_ffn"]], [(d_model, BF16)], flipped=[(d_model, BF16, 1)],
                        name="rms_ffn")
    ffn_up = functools.partial(
        _mm, grid=(n2, n_sh), a_spec=pl.BlockSpec((t2, d_model), lambda i, j: (i, 0)),
        b_spec=pl.BlockSpec((None, d_model, f_sh), lambda i, j: (j, 0, 0)),
        o_spec=pl.BlockSpec((None, t2, f_sh), lambda i, j: (j, i, 0)), o_shape=(n_sh, seq, f_sh), dims="nn",
        out_dtype=BF16)
    gate = ffn_up(h2, w["w_gate"], name="ffn_gate")
    up = ffn_up(h2, w["w_up"], name="ffn_up")
    flat = lambda a: a.reshape(n_sh * seq, f_sh)
    act, act_t = _rowwise(twice(swiglu_fwd), [flat(gate), flat(up)], [], [(f_sh, BF16)],
                          flipped=[(f_sh, BF16, n_sh)], name="swiglu", tm=1024)
    act, act_t = act.reshape(n_sh, seq, f_sh), act_t.reshape(n_sh, f_sh, seq)
    ffn_out = _mm(act, w["w_down"], name="ffn_down", grid=(n1, n_sh // 2), groups=2,
                  a_spec=pl.BlockSpec((2, t1, f_sh), lambda i, j: (j, i, 0)),
                  b_spec=pl.BlockSpec((2, f_sh, d_model), lambda i, j: (j, 0, 0)),
                  o_spec=pl.BlockSpec((t1, d_model), lambda i, j: (i, 0)), o_shape=(seq, d_model), dims="nn",
                  k_axis=1)

    dx2, dx2_b, sq = _rowwise(functools.partial(loss_head, inv_d=1.0 / d_model), [ffn_out, x1, target], [],
                              [(d_model, F32), (d_model, BF16)], [d_model], name="loss_head")
    loss = 0.5 * jnp.sum(sq) / d_model

    d_act = _mm(dx2_b, w["w_down"], name="ffn_down_dx", grid=(n2, n_sh),
                a_spec=pl.BlockSpec((t2, d_model), lambda i, j: (i, 0)),
                b_spec=pl.BlockSpec((None, f_sh, d_model), lambda i, j: (j, 0, 0)),
                o_spec=pl.BlockSpec((None, t2, f_sh), lambda i, j: (j, i, 0)), o_shape=(n_sh, seq, f_sh), dims="nt",
                out_dtype=BF16)
    g_w_down = _mm(act_t, dx2_b, name="ffn_down_dw", grid=(n_sh, n2),
                   a_spec=pl.BlockSpec((None, f_sh, t2), lambda j, k: (j, 0, k)),
                   b_spec=pl.BlockSpec((t2, d_model), lambda j, k: (k, 0)),
                   o_spec=pl.BlockSpec((None, f_sh, d_model), lambda j, k: (j, 0, 0)),
                   o_shape=(n_sh, f_sh, d_model), dims="nn", k_axis=1)
    d_gate, d_up = _rowwise(swiglu_bwd, [flat(d_act), flat(gate), flat(up)], [], [(f_sh, BF16), (f_sh, BF16)],
                            name="swiglu_bwd", tm=1024)
    d_gate, d_up = d_gate.reshape(n_sh, seq, f_sh), d_up.reshape(n_sh, seq, f_sh)
    d_h2 = _mm(d_gate, w["w_gate"], second=(d_up, w["w_up"]), name="ffn_up_gate_dx", grid=(n1, n_sh),
               a_spec=pl.BlockSpec((None, t1, f_sh), lambda i, j: (j, i, 0)),
               b_spec=pl.BlockSpec((None, d_model, f_sh), lambda i, j: (j, 0, 0)),
               o_spec=pl.BlockSpec((t1, d_model), lambda i, j: (i, 0)), o_shape=(seq, d_model), dims="nt", k_axis=1)
    ffn_dw = functools.partial(
        _mm, grid=(n_sh, n2), a_spec=pl.BlockSpec((d_model, t2), lambda j, k: (0, k)),
        b_spec=pl.BlockSpec((None, t2, f_sh), lambda j, k: (j, k, 0)),
        o_spec=pl.BlockSpec((None, d_model, f_sh), lambda j, k: (j, 0, 0)), o_shape=(n_sh, d_model, f_sh), dims="nn",
        k_axis=1)
    g_w_gate = ffn_dw(h2_t, d_gate, name="ffn_gate_dw")
    g_w_up = ffn_dw(h2_t, d_up, name="ffn_up_dw")
    dx1, g_g_ffn = _rowwise(residual_rms_bwd, [dx2, d_h2, x1], [p["g_ffn"]], [(d_model, F32)], [d_model],
                            name="rms_ffn_bwd")

    d_ycat = _mm_plain(dx1, w["w_out"], "nt", name="out_proj_dx", tn=2048)
    mix_w = a_width + s_width
    tm_o = _tile(mix_w, 1024)
    g_w_out = _mm(y_cat_t, dx1, name="out_proj_dw", grid=(mix_w // tm_o, n1),
                  a_spec=pl.BlockSpec((tm_o, t1), lambda i, k: (i, k)),
                  b_spec=pl.BlockSpec((t1, d_model), lambda i, k: (k, 0)),
                  o_spec=pl.BlockSpec((tm_o, d_model), lambda i, k: (i, 0)), o_shape=(mix_w, d_model), dims="nn",
                  k_axis=1)
    (d_ya, d_yg_direct, d_t, g_goa, g_gos, g_b_glu) = _rowwise(
        functools.partial(mix_out_bwd, a_width=a_width), [d_ycat, ya, yg, t_glu],
        [p["b_glu"], p["g_out_attn"], p["g_out_ssm"]],
        [(a_width, F32), (s_width, F32), (s_width, BF16)], [a_width, s_width, s_width], name="mix_out_bwd")
    d_yg = _mm_plain(d_t, w["w_glu"], "nt", name="glu_proj_dx", res=d_yg_direct, tn=s_width)
    g_w_glu = _mm(yg_t, d_t, name="glu_proj_dw", grid=(1, n1),
                  a_spec=pl.BlockSpec((s_width, t1), lambda i, k: (0, k)),
                  b_spec=pl.BlockSpec((t1, s_width), lambda i, k: (k, 0)),
                  o_spec=pl.BlockSpec((s_width, s_width), lambda i, k: (0, 0)), o_shape=(s_width, s_width),
                  dims="nn", k_axis=1)
    d_ypre, du_skip, g_ssm_d = _rowwise(gelu_skip_bwd, [d_yg, ypre, u], [p["ssm_d"]],
                                        [(s_width, F32), (s_width, F32)], [s_width], name="s5_skip_gelu_bwd")

    ffn_names, mix_names = ("w_ffn_gate", "w_ffn_up", "w_ffn_down"), ("w_glu", "w_out")
    ffn_slabs = [slab3(g, n) for g, n in zip((g_w_gate, g_w_up, g_w_down), ffn_names)]
    mix_slabs = [slab3(g, n) for g, n in zip((g_w_glu, g_w_out), mix_names)]
    du_dirs, adj, r_parts, gb_parts, gc_parts = [], [], [], [], []
    sib, part = {}, {}
    for d, (names, slabs) in enumerate(((ffn_names, ffn_slabs), (mix_names, mix_slabs))):
        win_re, win_im, tabs, wo_re, wo_im = bwd_in[d]
        (as_re, as_im, du_d), got = _s5_scan(d_ypre, win_re, win_im, tabs, wo_re, wo_im, reverse=(d == 0),
                                             name=f"s5_bwd_{d}", ride=_reduce_sibling(slabs))
        du_dirs.append(du_d)
        adj.append((as_re, as_im))
        sib[names] = got
        part[names] = _partial_sums(slabs, got, names)
    for d in range(2):
        (r_re, r_im, gbt_re, gbt_im, gct_re, gct_im), got = _s5_reduce(
            fwd[d][0], fwd[d][1], adj[d][0], adj[d][1], u, d_ypre, name=f"s5_reduce_{d}",
            ride=_reduce_chips(part[mix_names] if d == 0 else part[ffn_names][2:]))
        if d == 0:
            mix_chips = got
        else:
            down_chips = got
        r_parts.append((r_re.reshape(n_state, 1), r_im.reshape(n_state, 1)))
        gb_parts.append((_block_diag_take(gbt_re, SSM_C, SSM_P), _block_diag_take(gbt_im, SSM_C, SSM_P)))
        gc_parts.append((_block_diag_take(gct_re, SSM_C, SSM_P), _block_diag_take(gct_im, SSM_C, SSM_P)))
    cat = lambda i, parts: jnp.concatenate([parts[0][i], parts[1][i]], axis=0)
    gbb_re, gbb_im = cat(0, gb_parts).reshape(n_col, SSM_C), cat(1, gb_parts).reshape(n_col, SSM_C)
    g_a_re, g_a_im, g_ls, g_b_re, g_b_im = _s5_param_grads(
        a_re_c, a_im_c, dt_c, b_re_c, b_im_c, pw_re[:, 0].reshape(n_col, 1), pw_im[:, 0].reshape(n_col, 1),
        cf_re, cf_im, bb_re, bb_im, cat(0, r_parts), cat(1, r_parts), gbb_re, gbb_im)
    g_c_re = cat(0, gc_parts).reshape(2, n_groups, SSM_P, SSM_C).transpose(0, 1, 3, 2)
    g_c_im = -cat(1, gc_parts).reshape(2, n_groups, SSM_P, SSM_C).transpose(0, 1, 3, 2)

    (d_q, d_k, d_v, d_btab, g_qg, g_kg), ffn_chips = _attn_bwd(z, d_ya, qg4, kg4, btab,
                                                                ride=_reduce_chips(part[ffn_names][:2]))
    ffn_chips = ffn_chips + down_chips
    d_u = _rowwise(lambda a, b, c: a + b + c, [du_dirs[0], du_dirs[1], du_skip], [], [(s_width, BF16)],
                   name="s5_du_sum")[0]
    d_z = jnp.concatenate([d_q, d_k, d_v, d_u], axis=1)
    fold_heads = lambda g: g.reshape(n_heads, HEAD_DIM).sum(axis=0, keepdims=True)
    small = {
        "q_gain": fold_heads(g_qg), "k_gain": fold_heads(g_kg), "rpb": _bias_grad(d_btab, n_heads),
        "ssm_a_re": g_a_re.reshape(2, n_groups, SSM_P), "ssm_a_im": g_a_im.reshape(2, n_groups, SSM_P),
        "ssm_b_re": g_b_re.reshape(2, n_groups, SSM_P, SSM_C), "ssm_b_im": g_b_im.reshape(2, n_groups, SSM_P, SSM_C),
        "ssm_c_re": g_c_re, "ssm_c_im": g_c_im,
        "ssm_log_step": g_ls.reshape(2, n_groups, SSM_P).sum(axis=-1),
        "ssm_d": g_ssm_d, "b_glu": g_b_glu, "g_out_attn": g_goa, "g_out_ssm": g_gos, "g_ffn": g_g_ffn,
    }
    packed = _pack([small[n] for n in SMALL_PACKED])
    g_w_in, got = _mm(h1_t, d_z, name="in_proj_dw", grid=(n_sh, n2),
                      a_spec=pl.BlockSpec((d_model, t2), lambda j, k: (0, k)),
                      b_spec=pl.BlockSpec((t2, in_sh), lambda j, k: (k, j)),
                      o_spec=pl.BlockSpec((None, d_model, in_sh), lambda j, k: (j, 0, 0)),
                      o_shape=(n_sh, d_model, in_sh), dims="nn", k_axis=1, ride=_gather_first([packed]))
    d_h1, got = _mm(d_z, w["w_in"], name="in_proj_dx", grid=(n1, n_sh // 2), groups=2,
                    a_spec=pl.BlockSpec((t1, 2 * in_sh), lambda i, j: (i, j)),
                    b_spec=pl.BlockSpec((2, d_model, in_sh), lambda i, j: (j, 0, 0)),
                    o_spec=pl.BlockSpec((t1, d_model), lambda i, j: (i, 0)), o_shape=(seq, d_model), dims="nt",
                    k_axis=1, ride=_gather_second(got) + _reduce_sibling([g_w_in]))
    small_gathered, in_sibling = got
    in_part = _partial_sums([g_w_in], [in_sibling], ("w_in",))
    (grad_x, g_g_mix), (in_chips,) = _rowwise(residual_rms_bwd, [dx1, d_h1, x], [p["g_mix"]], [(d_model, F32)],
                                              [d_model], name="rms_mix_bwd", ride=_reduce_chips(in_part))
    reduced = {"w_in": (g_w_in, in_sibling, in_chips)}
    for names, slabs, chips in ((ffn_names, ffn_slabs, ffn_chips), (mix_names, mix_slabs, mix_chips)):
        for i, n in enumerate(names):
            reduced[n] = (slabs[i], sib[names][i], chips[i])
    return loss, grad_x, small_gathered, g_g_mix, reduced


def x_norm(xv, g):
    return xv * _rstd(xv) * g


def s5_mid(y0, y1, uv, d_skip):
    ypre = y0 + y1 + d_skip * uv
    yg = _gelu(ypre)
    return ypre, yg, yg


def mix_out_fwd(ya, yg, t, b_glu, g_oa, g_os):
    ys = yg * _sigmoid(t + b_glu)
    return jnp.concatenate([ya * _rstd(ya) * g_oa, ys * _rstd(ys) * g_os], axis=1)


def mix_out_bwd(d_y, ya, yg, t, b_glu, g_oa, g_os, *, a_width):
    sg = _sigmoid(t + b_glu)
    ys = yg * sg
    d_ya, c_goa = _rms_bwd(d_y[:, :a_width], ya, g_oa)
    d_ys, c_gos = _rms_bwd(d_y[:, a_width:], ys, g_os)
    d_t = d_ys * yg * sg * (1.0 - sg)
    return d_ya, d_ys * sg, d_t, c_goa, c_gos, d_t


def gelu_skip_bwd(d_yg, ypre, uv, d_skip):
    d_ypre = d_yg * _gelu_grad(ypre)
    return d_ypre, d_ypre * d_skip, d_ypre * uv


def swiglu_fwd(gv, uv):
    gv, uv = gv.astype(F32), uv.astype(F32)
    return gv * _sigmoid(gv) * uv


def swiglu_bwd(d_act, gv, uv):
    d_act, gv, uv = d_act.astype(F32), gv.astype(F32), uv.astype(F32)
    sg = _sigmoid(gv)
    return d_act * uv * (sg * (1.0 + gv * (1.0 - sg))), d_act * gv * sg


def loss_head(ffn_out, x1, target, *, inv_d):
    diff = ffn_out + x1 - target
    return diff * inv_d, diff * inv_d, diff * diff


def residual_rms_bwd(d_res, d_h, xv, g):
    dx, c_g = _rms_bwd(d_h, xv, g)
    return d_res + dx, c_g


_ANY = pl.BlockSpec(memory_space=pl.ANY)


def _mesh_place():
    return lax.axis_index("x"), lax.axis_index("y"), lax.axis_index("c")


def _chips(x, y):
    return [(x, y), (1 - x, y), (x, 1 - y), (1 - x, 1 - y)]


def _slab(px, py, pc):
    return 4 * px + 2 * py + pc


def _all_gather(arrs, *, name):
    n = len(arrs)

    def body(*refs):
        in_refs, out_refs = refs[:n], refs[n:2 * n]
        send_sems, recv_sems, local_sems = refs[2 * n:]
        x, y, c = _mesh_place()
        me, sibling = (x, y, c), (x, y, 1 - c)
        others = _chips(x, y)[1:]

        def copy(w, k, block, to, src=None):
            dst = out_refs[w].at[_slab(*block)]
            return pltpu.make_async_remote_copy(
                src_ref=dst if src is None else src, dst_ref=dst, send_sem=send_sems.at[7 * w + k],
                recv_sem=recv_sems.at[7 * w + k], device_id=to, device_id_type=MESH)

        mine = [pltpu.make_async_copy(in_refs[w], out_refs[w].at[_slab(*me)], local_sems.at[w]) for w in range(n)]
        first = []
        for w in range(n):
            mine[w].start()
            first.append(copy(w, 0, me, sibling, src=in_refs[w]))
            first += [copy(w, 1 + j, me, (*chip, c), src=in_refs[w]) for j, chip in enumerate(others)]
        for cp in first:
            cp.start()
        passed = []
        for j, chip in enumerate(others):
            for w in range(n):
                copy(w, 1 + j, (*chip, c), me).wait_recv()
                fwd = copy(w, 4 + j, (*chip, c), sibling)
                fwd.start()
                passed.append(fwd)
        for w in range(n):
            copy(w, 0, sibling, me).wait_recv()
        for j, chip in enumerate(others):
            for w in range(n):
                copy(w, 4 + j, (*chip, 1 - c), me).wait_recv()
        for cp in first + passed:
            cp.wait_send()
        for cp in mine:
            cp.wait()

    return pl.pallas_call(
        body, name=name, in_specs=[_ANY] * n, out_specs=[_ANY] * n,
        out_shape=[jax.ShapeDtypeStruct((N_DEV,) + a.shape, a.dtype) for a in arrs],
        scratch_shapes=[pltpu.SemaphoreType.DMA((7 * n,)), pltpu.SemaphoreType.DMA((7 * n,)),
                        pltpu.SemaphoreType.DMA((n,))],
        compiler_params=pltpu.CompilerParams(has_side_effects=True),
    )(*arrs)


def _adamw(w, m, v, parts, *, name, slab, tr=256):
    rows, cols = w.shape
    tr = _tile(rows, tr)
    n_p = len(parts)

    def body(slab_ref, *refs):
        w_ref, m_ref, v_ref = refs[:3]
        p_refs = refs[3:3 + n_p]
        g_ref, d_ref, nm_ref, nv_ref = refs[3 + n_p:]
        g = None
        for (_, lead), r in zip(parts, p_refs):
            for piece in ([r[...]] if lead is None else [r[i] for i in range(lead)]):
                g = piece.astype(F32) if g is None else g + piece.astype(F32)
        new_m = ADAM_B1 * m_ref[...] + (1.0 - ADAM_B1) * g
        new_v = ADAM_B2 * v_ref[...] + (1.0 - ADAM_B2) * (g * g)
        m_hat = new_m / (1.0 - ADAM_B1 ** ADAM_STEP)
        v_hat = new_v / (1.0 - ADAM_B2 ** ADAM_STEP)
        g_ref[...] = g
        d_ref[...] = -ADAM_LR * (m_hat / (jnp.sqrt(v_hat) + ADAM_EPS) + ADAM_WD * w_ref[...])
        nm_ref[...] = new_m
        nv_ref[...] = new_v

    tile = pl.BlockSpec((tr, cols), lambda i, s: (i, 0))
    p_specs = [pl.BlockSpec((None, tr, cols), lambda i, s: (s[0], i, 0)) if lead is None
               else pl.BlockSpec((lead, tr, cols), lambda i, s: (0, i, 0)) for _, lead in parts]
    return pl.pallas_call(
        body, name=name,
        grid_spec=pltpu.PrefetchScalarGridSpec(num_scalar_prefetch=1, grid=(rows // tr,),
                                               in_specs=[tile] * 3 + p_specs, out_specs=[tile] * 4),
        out_shape=[jax.ShapeDtypeStruct((rows, cols), F32)] * 4, compiler_params=_params(),
    )(jnp.reshape(slab, (1,)).astype(jnp.int32), w, m, v, *[a for a, _ in parts])


_PACK_TILE = SUBLANES * 128
_PACK_ROWS = 512


def _pack(arrs):
    flat = []
    for a in arrs:
        f = a.reshape(-1)
        flat.append(jnp.pad(f, (0, (-f.shape[0]) % _PACK_TILE)))
    total = sum(f.shape[0] for f in flat)
    flat.append(jnp.zeros(((-total) % (_PACK_ROWS * 128),), F32))
    return jnp.concatenate(flat).reshape(-1, 128)


def _unpack(buf, shapes):
    out, at = [], 0
    flat = buf.reshape(-1)
    for s in shapes:
        n = math.prod(s)
        out.append(flat[at:at + n].reshape(s))
        at += n + (-n) % _PACK_TILE
    return out


BIG = ("w_in", "w_glu", "w_out", "w_ffn_gate", "w_ffn_up", "w_ffn_down")
WEIGHTS = ("g_mix", "w_in", "q_gain", "k_gain", "rpb", "ssm_a_re", "ssm_a_im", "ssm_b_re", "ssm_b_im", "ssm_c_re",
           "ssm_c_im", "ssm_log_step", "ssm_d", "w_glu", "b_glu", "g_out_attn", "g_out_ssm", "w_out", "g_ffn",
           "w_ffn_gate", "w_ffn_up", "w_ffn_down")
SMALL = tuple(n for n in WEIGHTS if n not in BIG)
SMALL_PACKED = tuple(n for n in SMALL if n != "g_mix")
VECTORS = ("g_mix", "q_gain", "k_gain", "ssm_d", "b_glu", "g_out_attn", "g_out_ssm", "g_ffn")


def kernel(x, g_mix, w_in, q_gain, k_gain, rpb, ssm_a_re, ssm_a_im, ssm_b_re, ssm_b_im, ssm_c_re, ssm_c_im, ssm_log_step, ssm_d, w_glu, b_glu, g_out_attn, g_out_ssm, w_out, g_ffn, w_ffn_gate, w_ffn_up, w_ffn_down, loss_target, m_g_mix, m_w_in, m_q_gain, m_k_gain, m_rpb, m_ssm_a_re, m_ssm_a_im, m_ssm_b_re, m_ssm_b_im, m_ssm_c_re, m_ssm_c_im, m_ssm_log_step, m_ssm_d, m_w_glu, m_b_glu, m_g_out_attn, m_g_out_ssm, m_w_out, m_g_ffn, m_w_ffn_gate, m_w_ffn_up, m_w_ffn_down, v_g_mix, v_w_in, v_q_gain, v_k_gain, v_rpb, v_ssm_a_re, v_ssm_a_im, v_ssm_b_re, v_ssm_b_im, v_ssm_c_re, v_ssm_c_im, v_ssm_log_step, v_ssm_d, v_w_glu, v_b_glu, v_g_out_attn, v_g_out_ssm, v_w_out, v_g_ffn, v_w_ffn_gate, v_w_ffn_up, v_w_ffn_down):
    wts = dict(g_mix=g_mix, w_in=w_in, q_gain=q_gain, k_gain=k_gain, rpb=rpb, ssm_a_re=ssm_a_re, ssm_a_im=ssm_a_im,
               ssm_b_re=ssm_b_re, ssm_b_im=ssm_b_im, ssm_c_re=ssm_c_re, ssm_c_im=ssm_c_im, ssm_log_step=ssm_log_step,
               ssm_d=ssm_d, w_glu=w_glu, b_glu=b_glu, g_out_attn=g_out_attn, g_out_ssm=g_out_ssm, w_out=w_out,
               g_ffn=g_ffn, w_ffn_gate=w_ffn_gate, w_ffn_up=w_ffn_up, w_ffn_down=w_ffn_down)
    mom = dict(g_mix=m_g_mix, w_in=m_w_in, q_gain=m_q_gain, k_gain=m_k_gain, rpb=m_rpb, ssm_a_re=m_ssm_a_re,
               ssm_a_im=m_ssm_a_im, ssm_b_re=m_ssm_b_re, ssm_b_im=m_ssm_b_im, ssm_c_re=m_ssm_c_re,
               ssm_c_im=m_ssm_c_im, ssm_log_step=m_ssm_log_step, ssm_d=m_ssm_d, w_glu=m_w_glu, b_glu=m_b_glu,
               g_out_attn=m_g_out_attn, g_out_ssm=m_g_out_ssm, w_out=m_w_out, g_ffn=m_g_ffn,
               w_ffn_gate=m_w_ffn_gate, w_ffn_up=m_w_ffn_up, w_ffn_down=m_w_ffn_down)
    var = dict(g_mix=v_g_mix, w_in=v_w_in, q_gain=v_q_gain, k_gain=v_k_gain, rpb=v_rpb, ssm_a_re=v_ssm_a_re,
               ssm_a_im=v_ssm_a_im, ssm_b_re=v_ssm_b_re, ssm_b_im=v_ssm_b_im, ssm_c_re=v_ssm_c_re,
               ssm_c_im=v_ssm_c_im, ssm_log_step=v_ssm_log_step, ssm_d=v_ssm_d, w_glu=v_w_glu, b_glu=v_b_glu,
               g_out_attn=v_g_out_attn, g_out_ssm=v_g_out_ssm, w_out=v_w_out, g_ffn=v_g_ffn,
               w_ffn_gate=v_w_ffn_gate, w_ffn_up=v_w_ffn_up, w_ffn_down=v_w_ffn_down)
    ix, iy, ic = _mesh_place()
    me = _slab(ix, iy, ic)
    d_model = x.shape[-1]

    shard = {n: wts[n][0] for n in BIG}
    shard_b = {n: shard[n].astype(BF16) for n in BIG}
    w_in_full = _all_gather([shard_b["w_in"]], name="gather_w_in")[0]
    p = {n: (wts[n][0].reshape(1, -1) if n in VECTORS else wts[n][0]) for n in SMALL}

    loss, grad_x, small_gathered, g_g_mix, reduced = _local_step(x[0], loss_target[0], p, w_in_full,
                                                                 {n: shard_b[n] for n in BIG if n != "w_in"})
    loss = lax.psum(loss, ("x", "y", "c"))
    out = {}
    for n in BIG:
        slabs, from_sibling, from_chips = reduced[n]
        rows, cols = slabs.shape[1:]
        res = _adamw(shard[n].reshape(rows, cols), mom[n][0].reshape(rows, cols), var[n][0].reshape(rows, cols),
                     [(slabs, None), (from_sibling, 1), (from_chips, 3)], name=f"adamw_{n}", slab=me)
        out[n] = [r.reshape(wts[n].shape) for r in res]

    order = list(SMALL_PACKED)
    shapes = [wts[n].shape for n in order]
    res = _adamw(_pack([wts[n] for n in order]), _pack([mom[n] for n in order]), _pack([var[n] for n in order]),
                 [(small_gathered, N_DEV)], name="adamw_small", slab=me)
    for kind, buf in enumerate(res):
        for n, a in zip(order, _unpack(buf, shapes)):
            out.setdefault(n, [None] * 4)[kind] = a
    as_rows = lambda a: a.reshape(-1, 128)
    g_mix_all = _all_gather([as_rows(g_g_mix)], name="gather_g_mix")[0]
    res = _adamw(as_rows(wts["g_mix"]), as_rows(mom["g_mix"]), as_rows(var["g_mix"]), [(g_mix_all, N_DEV)],
                 name="adamw_g_mix", slab=me)
    out["g_mix"] = [r.reshape(wts["g_mix"].shape) for r in res]

    return (loss, grad_x[None], *[out[n][0] for n in WEIGHTS], *[out[n][1] for n in WEIGHTS],
            *[out[n][2] for n in WEIGHTS], *[out[n][3] for n in WEIGHTS])
```

```python
import functools
import math

import jax
import jax.numpy as jnp
from jax import lax
from jax.experimental import pallas as pl
from jax.experimental.pallas import tpu as pltpu

F32 = jnp.float32
BF16 = jnp.bfloat16

N_DEV = 8
GRID_W = 64
WIN_H = 8
WIN_W = 16
HEAD_DIM = 64
HEADS_PER_GROUP = 4
GROUP_LANES = HEADS_PER_GROUP * HEAD_DIM
SSM_C = 16
SSM_P = 64
GROUPS_PER_TILE = 8
U_TILE = GROUPS_PER_TILE * SSM_C
ST_TILE = GROUPS_PER_TILE * SSM_P
SUBLANES = 8
RMS_EPS = 1e-6
NEG_INF = -1e30
A_RE_MAX = -1e-4
ADAM_LR, ADAM_B1, ADAM_B2, ADAM_EPS, ADAM_WD, ADAM_STEP = 0.001, 0.9, 0.999, 1e-08, 0.01, 10
VMEM_LIMIT_V7X = 56 * 1024 * 1024
MESH = pl.DeviceIdType.MESH

_NN = (((1,), (0,)), ((), ()))
_NT = (((1,), (1,)), ((), ()))
_TN = (((0,), (0,)), ((), ()))
_DIMS = {"nn": _NN, "nt": _NT, "tn": _TN}


def _params(**kw):
    return pltpu.CompilerParams(vmem_limit_bytes=VMEM_LIMIT_V7X, **kw)


def _dot(a, b, dims=_NN):
    return lax.dot_general(a, b, dims, preferred_element_type=F32)


def _mm(a, b, *, name, grid, a_spec, b_spec, o_spec, o_shape, dims, k_axis=None, res=None, out_dtype=F32,
        exact=False, second=None, ride=None, groups=1):
    dn = _DIMS[dims]
    nk = 1 if k_axis is None else grid[k_axis]
    acc_shape = tuple(d for d in o_spec.block_shape if d is not None)
    n_in = 2 + (2 if second is not None else 0)

    def body(*refs):
        a_ref, b_ref = refs[:2]
        r_ref = refs[n_in] if res is not None else None
        o_ref, acc = refs[-2:]
        def product(x_ref, y_ref):
            if groups == 1:
                return _dot(x_ref[...].astype(BF16), y_ref[...].astype(BF16), dn)
            total, width = None, x_ref.shape[-1] // groups
            for s in range(groups):
                x = x_ref[s] if len(x_ref.shape) == 3 else x_ref[:, s * width:(s + 1) * width]
                t = _dot(x.astype(BF16), y_ref[s].astype(BF16), dn)
                total = t if total is None else total + t
            return total

        if exact:
            p = lax.dot_general(a_ref[...], b_ref[...], dn, precision=lax.Precision.HIGHEST,
                                preferred_element_type=F32)
        else:
            p = product(a_ref, b_ref)
        if second is not None:
            p = p + product(refs[2], refs[3])

        def finish(v):
            if r_ref is not None:
                v = v + r_ref[...].astype(F32)
            o_ref[...] = v.astype(out_dtype)

        if nk == 1:
            finish(p)
        else:
            k = pl.program_id(k_axis)

            @pl.when(k == 0)
            def _():
                acc[...] = p

            @pl.when(k > 0)
            def _():
                acc[...] += p

            @pl.when(k == nk - 1)
            def _():
                finish(acc[...])

    ins = [a, b] + (list(second) if second is not None else []) + ([res] if res is not None else [])
    in_specs = [a_spec, b_spec] * (n_in // 2) + ([o_spec] if res is not None else [])
    first, last = _grid_ends(grid)
    (out,), rode = _call(
        body, name=name, grid=grid, in_specs=in_specs, out_specs=[o_spec],
        out_shape=[jax.ShapeDtypeStruct(o_shape, out_dtype)],
        scratch_shapes=[pltpu.VMEM(acc_shape if nk > 1 else (SUBLANES, 128), F32)],
        args=ins, ride=ride, first=first, last=last)
    return out if ride is None else (out, rode)


def _tile(n, want):
    if n <= want:
        return n
    t = want
    while n % t:
        t //= 2
    return t


def _mm_plain(a, b, dims, *, name, res=None, out_dtype=F32, tm=512, tn=512, tk=512, exact=False, ride=None):
    if dims == "nn":
        (m, k), n = a.shape, b.shape[1]
    elif dims == "nt":
        (m, k), n = a.shape, b.shape[0]
    else:
        (k, m), n = a.shape, b.shape[1]
    tm, tn = _tile(m, tm), _tile(n, tn)
    if dims == "tn":
        tk = _tile(k, tk)
        grid = (m // tm, n // tn, k // tk)
        a_spec = pl.BlockSpec((tk, tm), lambda i, j, kk: (kk, i))
        b_spec = pl.BlockSpec((tk, tn), lambda i, j, kk: (kk, j))
        o_spec = pl.BlockSpec((tm, tn), lambda i, j, kk: (i, j))
        return _mm(a, b, name=name, grid=grid, a_spec=a_spec, b_spec=b_spec, o_spec=o_spec, o_shape=(m, n),
                   dims=dims, k_axis=2, res=res, out_dtype=out_dtype)
    grid = (n // tn, m // tm)
    a_spec = pl.BlockSpec((tm, k), lambda j, i: (i, 0))
    if dims == "nn":
        b_spec = pl.BlockSpec((k, tn), lambda j, i: (0, j))
    else:
        b_spec = pl.BlockSpec((tn, k), lambda j, i: (j, 0))
    o_spec = pl.BlockSpec((tm, tn), lambda j, i: (i, j))
    return _mm(a, b, name=name, grid=grid, a_spec=a_spec, b_spec=b_spec, o_spec=o_spec, o_shape=(m, n), dims=dims,
               res=res, out_dtype=out_dtype, exact=exact, ride=ride)


def _rowwise(fn, tiled, bcast, outs, accs=(), *, name, tm=256, flipped=(), ride=None):
    m = tiled[0].shape[0]
    tm = _tile(m, tm)
    n_t, n_b, n_o, n_f = len(tiled), len(bcast), len(outs), len(flipped)

    def body(*refs):
        ins = [r[...] for r in refs[: n_t + n_b]]
        o_refs = refs[n_t + n_b: n_t + n_b + n_o]
        f_refs = refs[n_t + n_b + n_o: n_t + n_b + n_o + n_f]
        a_refs = refs[n_t + n_b + n_o + n_f:]
        res = fn(*ins)
        if not isinstance(res, (tuple, list)):
            res = (res,)
        for r, v in zip(o_refs, res[:n_o]):
            r[...] = v.astype(r.dtype)
        for r, v in zip(f_refs, res[n_o:n_o + n_f]):
            r[...] = v.astype(F32).T.astype(r.dtype)
        first = pl.program_id(0) == 0
        for r, v in zip(a_refs, res[n_o + n_f:]):
            s = jnp.sum(v, axis=0, keepdims=True)

            @pl.when(first)
            def _():
                r[...] = s

            @pl.when(jnp.logical_not(first))
            def _():
                r[...] += s

    in_specs = [pl.BlockSpec((tm, t.shape[1]), lambda i: (i, 0)) for t in tiled]
    in_specs += [pl.BlockSpec(b.shape, lambda i, nd=b.ndim: (0,) * nd) for b in bcast]
    out_specs = [pl.BlockSpec((tm, n), lambda i: (i, 0)) for n, _ in outs]
    out_specs += [pl.BlockSpec((n, tm), lambda i, per=m // tm // g: (i // per, i % per)) for n, _, g in flipped]
    out_specs += [pl.BlockSpec((1, n), lambda i: (0, 0)) for n in accs]
    out_shape = [jax.ShapeDtypeStruct((m, n), dt) for n, dt in outs]
    out_shape += [jax.ShapeDtypeStruct((g * n, m // g), dt) for n, dt, g in flipped]
    out_shape += [jax.ShapeDtypeStruct((1, n), F32) for n in accs]
    first, last = _grid_ends((m // tm,))
    res, rode = _call(body, name=name, grid=(m // tm,), in_specs=in_specs, out_specs=out_specs, out_shape=out_shape,
                      scratch_shapes=[], args=list(tiled) + list(bcast), ride=ride, first=first, last=last)
    return res if ride is None else (res, rode)


def _rstd(x):
    return lax.rsqrt(jnp.mean(x * x, axis=-1, keepdims=True) + RMS_EPS)


def _rms_bwd(dh, x, g):
    xh = x * _rstd(x)
    dxh = dh * g
    dx = _rstd(x) * (dxh - xh * jnp.mean(dxh * xh, axis=-1, keepdims=True))
    return dx, dh * xh


def _sigmoid(x):
    return 1.0 / (1.0 + jnp.exp(-x))


_GELU_K = math.sqrt(2.0 / math.pi)
_GELU_C = 0.044715


def _gelu(x):
    return 0.5 * x * (1.0 + jnp.tanh(_GELU_K * (x + _GELU_C * x * x * x)))


def _gelu_grad(x):
    th = jnp.tanh(_GELU_K * (x + _GELU_C * x * x * x))
    return 0.5 * (1.0 + th) + 0.5 * x * (1.0 - th * th) * _GELU_K * (1.0 + 3.0 * _GELU_C * x * x)


class _Exchange:
    def __init__(self, arrays, outs, n_sems, sends, recvs=None, local=None, aliases=None):
        self.arrays, self.outs, self.n_sems = list(arrays), list(outs), n_sems
        self.sends, self.local, self.aliases = sends, local, aliases or {}
        self.recvs = recvs or (lambda i, o: [(k, dst) for k, _, dst, _ in sends(i, o)])

    def __add__(self, other):
        na, no, ns = len(self.arrays), len(self.outs), self.n_sems
        mine = lambda f: (lambda i, o: f(i[:na], o[:no]))
        shift = lambda f, at: (lambda i, o: [(k + ns,) + tuple(rest) for k, *rest in f(i[na:], o[no:])]) if at else None
        both = lambda f, g: (lambda i, o: f(i, o) + g(i, o))
        local = None
        if self.local or other.local:
            la = mine(self.local) if self.local else (lambda i, o: [])
            lb = (lambda i, o: other.local(i[na:], o[no:])) if other.local else (lambda i, o: [])
            local = both(la, lb)
        aliases = dict(self.aliases)
        aliases.update({na + i: no + o for i, o in other.aliases.items()})
        return _Exchange(self.arrays + other.arrays, self.outs + other.outs, ns + other.n_sems,
                         both(mine(self.sends), shift(other.sends, True)),
                         both(mine(self.recvs), shift(other.recvs, True)), local, aliases)

    def descriptors(self, in_refs, out_refs, send_sems, recv_sems, local_sems):
        me = _mesh_place()
        remote = lambda k, src, dst, to: pltpu.make_async_remote_copy(
            src_ref=src, dst_ref=dst, send_sem=send_sems.at[k], recv_sem=recv_sems.at[k], device_id=to,
            device_id_type=MESH)
        out = [remote(*s) for s in self.sends(in_refs, out_refs)]
        arrive = [remote(k, dst, dst, me) for k, dst in self.recvs(in_refs, out_refs)]
        own = [pltpu.make_async_copy(src, dst, local_sems.at[i])
               for i, (src, dst) in enumerate(self.local(in_refs, out_refs) if self.local else [])]
        return out, arrive, own

    def start(self, *refs):
        out, _, own = self.descriptors(*refs)
        for cp in own + out:
            cp.start()

    def finish(self, *refs):
        out, arrive, own = self.descriptors(*refs)
        for cp in arrive:
            cp.wait_recv()
        for cp in out:
            cp.wait_send()
        for cp in own:
            cp.wait()


def _call(body, *, name, grid, in_specs, out_specs, out_shape, scratch_shapes, args, ride=None, first=None, last=None):
    if ride is None:
        res = pl.pallas_call(body, name=name, grid=grid, in_specs=in_specs, out_specs=out_specs, out_shape=out_shape,
                             scratch_shapes=scratch_shapes, compiler_params=_params())(*args)
        return list(res), []
    n_in, n_out, n_scr = len(in_specs), len(out_specs), len(scratch_shapes)
    r_in, r_out = len(ride.arrays), len(ride.outs)

    def wrapped(*refs):
        ins, refs = refs[:n_in], refs[n_in:]
        x_in, refs = refs[:r_in], refs[r_in:]
        outs, refs = refs[:n_out], refs[n_out:]
        x_out, refs = refs[:r_out], refs[r_out:]
        scr, sems = refs[:n_scr], refs[n_scr:]

        @pl.when(first())
        def _():
            ride.start(x_in, x_out, *sems)

        body(*ins, *outs, *scr)

        @pl.when(last())
        def _():
            ride.finish(x_in, x_out, *sems)

    n_local = max(1, len(ride.arrays))
    res = pl.pallas_call(
        wrapped, name=name, grid=grid, in_specs=list(in_specs) + [_ANY] * r_in,
        out_specs=list(out_specs) + [_ANY] * r_out, out_shape=list(out_shape) + ride.outs,
        scratch_shapes=list(scratch_shapes) + [pltpu.SemaphoreType.DMA((ride.n_sems,)),
                                               pltpu.SemaphoreType.DMA((ride.n_sems,)),
                                               pltpu.SemaphoreType.DMA((n_local,))],
        input_output_aliases={n_in + i: n_out + o for i, o in ride.aliases.items()},
        compiler_params=_params(has_side_effects=True),
    )(*args, *ride.arrays)
    return list(res[:n_out]), list(res[n_out:])


def _gather_first(shards):
    def sends(i, o):
        x, y, c = _mesh_place()
        peers = [(x, y, 1 - c)] + [(px, py, c) for px, py in _chips(x, y)[1:]]
        return [(4 * w + k, i[w], o[w].at[_slab(x, y, c)], to) for w in range(len(i)) for k, to in enumerate(peers)]

    def recvs(i, o):
        x, y, c = _mesh_place()
        peers = [(x, y, 1 - c)] + [(px, py, c) for px, py in _chips(x, y)[1:]]
        return [(4 * w + k, o[w].at[_slab(*peer)]) for w in range(len(i)) for k, peer in enumerate(peers)]

    def local(i, o):
        return [(i[w], o[w].at[_slab(*_mesh_place())]) for w in range(len(i))]

    outs = [jax.ShapeDtypeStruct((N_DEV,) + a.shape, a.dtype) for a in shards]
    return _Exchange(shards, outs, 4 * len(shards), sends, recvs, local)


def _gather_second(gathered):
    def sends(i, o):
        x, y, c = _mesh_place()
        return [(3 * w + j, o[w].at[_slab(px, py, c)], o[w].at[_slab(px, py, c)], (x, y, 1 - c))
                for w in range(len(o)) for j, (px, py) in enumerate(_chips(x, y)[1:])]

    def recvs(i, o):
        x, y, c = _mesh_place()
        return [(3 * w + j, o[w].at[_slab(px, py, 1 - c)])
                for w in range(len(o)) for j, (px, py) in enumerate(_chips(x, y)[1:])]

    outs = [jax.ShapeDtypeStruct(a.shape, a.dtype) for a in gathered]
    return _Exchange(gathered, outs, 3 * len(gathered), sends, recvs, aliases={w: w for w in range(len(gathered))})


def _reduce_sibling(slabs):
    def sends(i, o):
        x, y, c = _mesh_place()
        return [(4 * w + k, i[w].at[_slab(px, py, 1 - c)], o[w].at[k], (x, y, 1 - c))
                for w in range(len(i)) for k, (px, py) in enumerate(_chips(x, y))]

    outs = [jax.ShapeDtypeStruct((4,) + a.shape[1:], a.dtype) for a in slabs]
    return _Exchange(slabs, outs, 4 * len(slabs), sends)


def _reduce_chips(partials):
    def sends(i, o):
        x, y, c = _mesh_place()
        return [(3 * w + k, i[w].at[k], o[w].at[k], (px, py, c))
                for w in range(len(i)) for k, (px, py) in enumerate(_chips(x, y)[1:])]

    outs = [jax.ShapeDtypeStruct(a.shape, a.dtype) for a in partials]
    return _Exchange(partials, outs, 3 * len(partials), sends)


def _head_masks():
    lane_head = lax.broadcasted_iota(jnp.int32, (1, GROUP_LANES), 1) // HEAD_DIM
    return [(lane_head == h).astype(F32) for h in range(HEADS_PER_GROUP)]


def _head_block_diag():
    r = lax.broadcasted_iota(jnp.int32, (GROUP_LANES, GROUP_LANES), 0) // HEAD_DIM
    c = lax.broadcasted_iota(jnp.int32, (GROUP_LANES, GROUP_LANES), 1) // HEAD_DIM
    return (r == c).astype(BF16)


def _head_mean(x, bd):
    hi = x.astype(BF16)
    lo = (x - hi.astype(F32)).astype(BF16)
    return (_dot(hi, bd) + _dot(lo, bd)) * (1.0 / HEAD_DIM)


def _stack_heads(x, masks):
    return jnp.concatenate([x * m for m in masks], axis=0)


def _unstack_heads(xs, masks):
    out = xs[0:GRID_W] * masks[0]
    for h in range(1, HEADS_PER_GROUP):
        out = out + xs[h * GRID_W:(h + 1) * GRID_W] * masks[h]
    return out


def _row_start(r, rows):
    return jnp.clip(r - WIN_H // 2, 0, rows - WIN_H)


ROWS_PER_STEP = 2


def _attn_common_specs(seq, n_hg, rows):
    win_keys = WIN_H * GRID_W
    q_spec = pl.BlockSpec((ROWS_PER_STEP * GRID_W, GROUP_LANES), lambda g, r: (r, g))
    k_spec = pl.BlockSpec((seq, GROUP_LANES), lambda g, r: (0, n_hg + g))
    v_spec = pl.BlockSpec((seq, GROUP_LANES), lambda g, r: (0, 2 * n_hg + g))
    gain_spec = pl.BlockSpec((1, GROUP_LANES), lambda g, r: (0, 0))

    def variant(r):
        return _row_start(r, rows) - r + (WIN_H - 1)

    bias_specs = [pl.BlockSpec((None, None, HEADS_PER_GROUP, GRID_W, win_keys),
                               lambda g, r, h=h: (g, variant(ROWS_PER_STEP * r + h), 0, 0, 0))
                  for h in range(ROWS_PER_STEP)]
    return q_spec, k_spec, v_spec, gain_spec, bias_specs, variant


def _attn_prepare_kv(k_ref, v_ref, kg, kn_scr, vb_scr, bd, seq):
    chunk = _tile(seq, 512)

    def step(c, carry):
        rows = pl.ds(pl.multiple_of(c * chunk, chunk), chunk)
        k = k_ref[rows, :]
        kn_scr[rows, :] = (k * lax.rsqrt(_head_mean(k * k, bd) + RMS_EPS) * kg).astype(BF16)
        vb_scr[rows, :] = v_ref[rows, :].astype(BF16)
        return carry

    lax.fori_loop(0, seq // chunk, step, 0)


def _attn_probs(qn, kw, bias, masks):
    qs = _stack_heads(qn, masks).astype(BF16)
    s = _dot(qs, kw, _NT) * (1.0 / math.sqrt(HEAD_DIM)) + bias
    m = jnp.max(s, axis=-1, keepdims=True)
    p = jnp.exp(s - m)
    return qs, p * (1.0 / jnp.sum(p, axis=-1, keepdims=True))


def _grid_ends(grid):
    first = lambda: functools.reduce(jnp.logical_and, [pl.program_id(a) == 0 for a in range(len(grid))])
    last = lambda: functools.reduce(jnp.logical_and, [pl.program_id(a) == n - 1 for a, n in enumerate(grid)])
    return first, last


def _attn_fwd(z, qg4, kg4, btab, ride=None):
    seq = z.shape[0]
    a_width = btab.shape[0] * GROUP_LANES
    n_hg, rows, win_keys = btab.shape[0], seq // GRID_W, WIN_H * GRID_W
    q_spec, k_spec, v_spec, gain_spec, bias_specs, _ = _attn_common_specs(seq, n_hg, rows)
    grid = (n_hg, rows // ROWS_PER_STEP)

    def body(q_ref, k_ref, v_ref, qg_ref, kg_ref, *rest):
        b_refs, (o_ref, kn_scr, vb_scr) = rest[:ROWS_PER_STEP], rest[ROWS_PER_STEP:]
        bd, masks = _head_block_diag(), _head_masks()

        @pl.when(pl.program_id(1) == 0)
        def _():
            _attn_prepare_kv(k_ref, v_ref, kg_ref[...], kn_scr, vb_scr, bd, seq)

        for h in range(ROWS_PER_STEP):
            r = ROWS_PER_STEP * pl.program_id(1) + h
            mine = slice(h * GRID_W, (h + 1) * GRID_W)
            win = pl.ds(pl.multiple_of(_row_start(r, rows) * GRID_W, GRID_W), win_keys)
            q = q_ref[mine, :]
            qn = q * lax.rsqrt(_head_mean(q * q, bd) + RMS_EPS) * qg_ref[...]
            bias = b_refs[h][...].reshape(HEADS_PER_GROUP * GRID_W, win_keys)
            _, p = _attn_probs(qn, kn_scr[win, :], bias, masks)
            o_ref[mine, :] = _unstack_heads(_dot(p.astype(BF16), vb_scr[win, :]), masks)

    first, last = _grid_ends(grid)
    (ya,), rode = _call(
        body, name="attn_fwd", grid=grid,
        in_specs=[q_spec, k_spec, v_spec, gain_spec, gain_spec] + bias_specs,
        out_specs=[pl.BlockSpec((ROWS_PER_STEP * GRID_W, GROUP_LANES), lambda g, r: (r, g))],
        out_shape=[jax.ShapeDtypeStruct((seq, a_width), F32)],
        scratch_shapes=[pltpu.VMEM((seq, GROUP_LANES), BF16), pltpu.VMEM((seq, GROUP_LANES), BF16)],
        args=(z, z, z, qg4, kg4) + (btab,) * ROWS_PER_STEP, ride=ride, first=first, last=last)
    return ya, rode


def _attn_bwd(z, d_out, qg4, kg4, btab, ride=None):
    seq = z.shape[0]
    n_hg, rows, win_keys = btab.shape[0], seq // GRID_W, WIN_H * GRID_W
    a_width = n_hg * GROUP_LANES
    q_spec, k_spec, v_spec, gain_spec, bias_specs, variant = _attn_common_specs(seq, n_hg, rows)
    scale = 1.0 / math.sqrt(HEAD_DIM)
    grid = (n_hg, rows // ROWS_PER_STEP)

    def body(q_ref, k_ref, v_ref, do_ref, qg_ref, kg_ref, *rest):
        b_refs, rest = rest[:ROWS_PER_STEP], rest[ROWS_PER_STEP:]
        dq_ref, dk_out, dv_out, db_ref, dqg_ref, dkg_ref, kn_scr, vb_scr, dk_ref, dv_ref = rest
        bd, masks = _head_block_diag(), _head_masks()

        @pl.when(pl.program_id(1) == 0)
        def _():
            _attn_prepare_kv(k_ref, v_ref, kg_ref[...], kn_scr, vb_scr, bd, seq)
            dk_ref[...] = jnp.zeros_like(dk_ref)
            dv_ref[...] = jnp.zeros_like(dv_ref)
            db_ref[...] = jnp.zeros_like(db_ref)
            dqg_ref[...] = jnp.zeros_like(dqg_ref)

        qg = qg_ref[...]
        for h in range(ROWS_PER_STEP):
            r = ROWS_PER_STEP * pl.program_id(1) + h
            mine = slice(h * GRID_W, (h + 1) * GRID_W)
            win = pl.ds(pl.multiple_of(_row_start(r, rows) * GRID_W, GRID_W), win_keys)
            q = q_ref[mine, :]
            rq = lax.rsqrt(_head_mean(q * q, bd) + RMS_EPS)
            qh = q * rq
            kw, vw = kn_scr[win, :], vb_scr[win, :]
            bias = b_refs[h][...].reshape(HEADS_PER_GROUP * GRID_W, win_keys)
            qs, p = _attn_probs(qh * qg, kw, bias, masks)
            dos = _stack_heads(do_ref[mine, :], masks).astype(BF16)
            dp = _dot(dos, vw, _NT)
            ds = p * (dp - jnp.sum(p * dp, axis=-1, keepdims=True))
            db_ref[variant(r)] += ds.reshape(HEADS_PER_GROUP, GRID_W, win_keys)
            dsb = ds.astype(BF16)
            dqn = _unstack_heads(_dot(dsb, kw), masks) * scale
            dk_ref[win, :] += _dot(dsb, qs, _TN) * scale
            dv_ref[win, :] += _dot(p.astype(BF16), dos, _TN)
            dqg_ref[...] += jnp.sum(dqn * qh, axis=0, keepdims=True)
            dqh = dqn * qg
            dq_ref[mine, :] = (rq * (dqh - qh * _head_mean(dqh * qh, bd))).astype(BF16)

        @pl.when(pl.program_id(1) == grid[1] - 1)
        def _():
            chunk = _tile(seq, 512)
            kg = kg_ref[...]

            def step(c, dkg):
                rws = pl.ds(pl.multiple_of(c * chunk, chunk), chunk)
                k = k_ref[rws, :]
                rk = lax.rsqrt(_head_mean(k * k, bd) + RMS_EPS)
                kh = k * rk
                dkn = dk_ref[rws, :]
                dkh = dkn * kg
                dk_out[rws, :] = (rk * (dkh - kh * _head_mean(dkh * kh, bd))).astype(BF16)
                dv_out[rws, :] = dv_ref[rws, :].astype(BF16)
                return dkg + jnp.sum(dkn * kh, axis=0, keepdims=True)

            dkg_ref[...] = lax.fori_loop(0, seq // chunk, step, jnp.zeros((1, GROUP_LANES), F32))

    col_spec = pl.BlockSpec((seq, GROUP_LANES), lambda g, r: (0, g))
    gsum_spec = pl.BlockSpec((None, 1, GROUP_LANES), lambda g, r: (g, 0, 0))
    first, last = _grid_ends(grid)
    rows_spec = pl.BlockSpec((ROWS_PER_STEP * GRID_W, GROUP_LANES), lambda g, r: (r, g))
    return _call(
        body, name="attn_bwd", grid=grid,
        in_specs=[q_spec, k_spec, v_spec, rows_spec, gain_spec, gain_spec] + bias_specs,
        out_specs=[rows_spec, col_spec, col_spec,
                   pl.BlockSpec((None, WIN_H, HEADS_PER_GROUP, GRID_W, win_keys), lambda g, r: (g, 0, 0, 0, 0)),
                   gsum_spec, gsum_spec],
        out_shape=[jax.ShapeDtypeStruct((seq, a_width), BF16)] * 3
        + [jax.ShapeDtypeStruct(btab.shape, F32)]
        + [jax.ShapeDtypeStruct((n_hg, 1, GROUP_LANES), F32)] * 2,
        scratch_shapes=[pltpu.VMEM((seq, GROUP_LANES), BF16), pltpu.VMEM((seq, GROUP_LANES), BF16),
                        pltpu.VMEM((seq, GROUP_LANES), F32), pltpu.VMEM((seq, GROUP_LANES), F32)],
        args=(z, z, z, d_out, qg4, kg4) + (btab,) * ROWS_PER_STEP, ride=ride, first=first, last=last)


DC_SLOTS = 2 * WIN_W


def _bias_spread():
    c = jnp.arange(GRID_W)[:, None, None]
    kc = jnp.arange(GRID_W)[None, None, :]
    col_start = jnp.clip(c - WIN_W // 2, 0, GRID_W - WIN_W)
    col_in = (kc >= col_start) & (kc < col_start + WIN_W)
    offset = kc - c + (WIN_W - 1)
    row_i = jnp.arange(WIN_H)[:, None, None, None, None]
    row_d = jnp.arange(DC_SLOTS)[None, :, None, None, None]
    same_row = row_i == jnp.arange(WIN_H)[None, None, None, :, None]
    hit = same_row & (row_d == offset[None, None]) & col_in[None, None]
    mask_slot = (row_i == 0) & (row_d == DC_SLOTS - 1) & jnp.logical_not(col_in)[None, None]
    mask_slot = jnp.broadcast_to(mask_slot, hit.shape)
    m = jnp.where(hit, 1.0, jnp.where(mask_slot, NEG_INF, 0.0)).astype(F32)
    return m.reshape(WIN_H * DC_SLOTS, GRID_W * WIN_H * GRID_W)


def _bias_table(rpb, spread):
    n_h = rpb.shape[0]
    n_hg = n_h // HEADS_PER_GROUP
    rows = jnp.stack([rpb[:, v:v + WIN_H] for v in range(WIN_H)], axis=1)
    rows = jnp.pad(rows, ((0, 0), (0, 0), (0, 0), (0, DC_SLOTS - rows.shape[-1])))
    rows = rows.at[:, :, 0, DC_SLOTS - 1].set(1.0)
    rows = rows.reshape(n_hg, HEADS_PER_GROUP, WIN_H, WIN_H * DC_SLOTS).transpose(0, 2, 1, 3)
    n_rows, win_keys = n_h * WIN_H, WIN_H * GRID_W
    tab = _mm(rows.reshape(n_rows, WIN_H * DC_SLOTS), spread, name="rpb_spread", grid=(GRID_W,),
              a_spec=pl.BlockSpec((n_rows, WIN_H * DC_SLOTS), lambda c: (0, 0)),
              b_spec=pl.BlockSpec((WIN_H * DC_SLOTS, win_keys), lambda c: (0, c)),
              o_spec=pl.BlockSpec((None, n_rows, win_keys), lambda c: (c, 0, 0)),
              o_shape=(GRID_W, n_rows, win_keys), dims="nn", exact=True)
    return tab.transpose(1, 0, 2).reshape(n_hg, WIN_H, HEADS_PER_GROUP, GRID_W, win_keys)


def _bias_grad(dtab, spread, n_h):
    n_hg = n_h // HEADS_PER_GROUP
    n_rows, win_keys = n_h * WIN_H, WIN_H * GRID_W
    d_rows = _mm(dtab.reshape(n_rows, GRID_W, win_keys).transpose(1, 0, 2), spread, name="rpb_diag_sum",
                 grid=(GRID_W,), a_spec=pl.BlockSpec((None, n_rows, win_keys), lambda c: (c, 0, 0)),
                 b_spec=pl.BlockSpec((WIN_H * DC_SLOTS, win_keys), lambda c: (0, c)),
                 o_spec=pl.BlockSpec((n_rows, WIN_H * DC_SLOTS), lambda c: (0, 0)),
                 o_shape=(n_rows, WIN_H * DC_SLOTS), dims="nt", k_axis=0)
    d_rows = d_rows.reshape(n_hg, WIN_H, HEADS_PER_GROUP, WIN_H, DC_SLOTS).transpose(0, 2, 1, 3, 4)
    d_rows = d_rows.reshape(n_h, WIN_H, WIN_H, DC_SLOTS)[..., : 2 * WIN_W - 1]
    out = jnp.zeros((n_h, 2 * WIN_H - 1, 2 * WIN_W - 1), F32)
    for v in range(WIN_H):
        out = out.at[:, v:v + WIN_H].add(d_rows[:, v])
    return out


def _cmul(ar, ai, br, bi):
    return ar * br - ai * bi, ar * bi + ai * br


def _s5_discretize(a_re, a_im, dt, b_re, b_im):
    c = b_re.shape[1]

    def fn(are, aim, dt_, bre, bim):
        lr, li = jnp.minimum(are, A_RE_MAX), aim
        mag = jnp.exp(lr * dt_)
        l1r, l1i = mag * jnp.cos(li * dt_), mag * jnp.sin(li * dt_)
        den = lr * lr + li * li
        nr, ni = l1r - 1.0, l1i
        cr, ci = (nr * lr + ni * li) / den, (ni * lr - nr * li) / den
        bbr, bbi = _cmul(cr, ci, bre, bim)
        shape = (are.shape[0], SUBLANES)
        lane = lax.broadcasted_iota(jnp.int32, shape, 1)
        pr, pi = l1r, l1i
        acc_r, acc_i = jnp.zeros(shape, F32), jnp.zeros(shape, F32)
        for k in range(SUBLANES):
            acc_r = jnp.where(lane == k, pr, acc_r)
            acc_i = jnp.where(lane == k, pi, acc_i)
            pr, pi = _cmul(pr, pi, l1r, l1i)
        return acc_r, acc_i, cr, ci, bbr, bbi

    return _rowwise(fn, [a_re, a_im, dt, b_re, b_im], [],
                    [(SUBLANES, F32), (SUBLANES, F32), (1, F32), (1, F32), (c, F32), (c, F32)],
                    name="s5_discretize", tm=1024)


def _s5_param_grads(a_re, a_im, dt, b_re, b_im, l1r, l1i, cr, ci, bbr, bbi, r_re, r_im, gb_re, gb_im):
    c = b_re.shape[1]

    def fn(are, aim, dt_, bre, bim, l1r_, l1i_, cr_, ci_, bbr_, bbi_, rr, ri, gbr, gbi):
        lr, li = jnp.minimum(are, A_RE_MAX), aim
        den = lr * lr + li * li
        dbr, dbi = _cmul(cr_, -ci_, gbr, gbi)
        gcr, gci = _cmul(bre, -bim, gbr, gbi)
        gcr, gci = jnp.sum(gcr, axis=1, keepdims=True), jnp.sum(gci, axis=1, keepdims=True)
        qr, qi = _cmul(bbr_, -bbi_, gbr, gbi)
        qr = rr - jnp.sum(qr, axis=1, keepdims=True)
        qi = ri - jnp.sum(qi, axis=1, keepdims=True)
        tr, ti = _cmul(gcr, gci, lr / den, li / den)
        ur, ui = _cmul(l1r_, -l1i_, tr, ti)
        gwr, gwi = qr + ur, qi + ui
        vr, vi = _cmul(cr_, -ci_, lr / den, li / den)
        vr, vi = _cmul(gcr, gci, vr, vi)
        glr, gli = dt_ * gwr - vr, dt_ * gwi - vi
        return jnp.where(are < A_RE_MAX, glr, 0.0), gli, (gwr * lr + gwi * li) * dt_, dbr, dbi

    return _rowwise(fn, [a_re, a_im, dt, b_re, b_im, l1r, l1i, cr, ci, bbr, bbi, r_re, r_im, gb_re, gb_im], [],
                    [(1, F32), (1, F32), (1, F32), (c, F32), (c, F32)], name="s5_param_grads", tm=1024)


def _s5_scan(v, win_re, win_im, tabs, wo_re, wo_im, *, reverse, name, t_chunk=256, ride=None):
    seq, width = v.shape
    n_tiles, n_state = width // U_TILE, width * (SSM_P // SSM_C)
    t_chunk = _tile(seq, t_chunk)
    n_chunks, n_blk = seq // t_chunk, t_chunk // SUBLANES
    last_row = 0 if reverse else SUBLANES - 1

    def chunk_of(j):
        return (n_chunks - 1 - j) if reverse else j

    def body(v_ref, wir_ref, wii_ref, tab_ref, wor_ref, woi_ref, sr_ref, si_ref, y_ref, carry, wr, wi):
        @pl.when(pl.program_id(0) == 0)
        def _():
            carry[...] = jnp.zeros_like(carry)

        for jt in range(n_tiles):
            ls = slice(jt * ST_TILE, (jt + 1) * ST_TILE)
            us = slice(jt * U_TILE, (jt + 1) * U_TILE)
            vj = v_ref[:, us].astype(BF16)
            consts = [tab_ref[k, :, ls] for k in range(8)]
            xr = _dot(vj, wir_ref[jt]).reshape(n_blk, SUBLANES, ST_TILE)
            xi = _dot(vj, wii_ref[jt]).reshape(n_blk, SUBLANES, ST_TILE)
            for s, k in enumerate((1, 2, 4)):
                sh = (SUBLANES - k) if reverse else k
                tr, ti = pltpu.roll(xr, sh, 1), pltpu.roll(xi, sh, 1)
                lr, li = consts[2 * s][None], consts[2 * s + 1][None]
                xr, xi = xr + lr * tr - li * ti, xi + lr * ti + li * tr
            wr[...] = xr.reshape(t_chunk, ST_TILE)
            wi[...] = xi.reshape(t_chunk, ST_TILE)

            def blk(b, c, consts=consts):
                cr, ci = c
                bb = (n_blk - 1 - b) if reverse else b
                rows = pl.ds(pl.multiple_of(bb * SUBLANES, SUBLANES), SUBLANES)
                lr, li = consts[6], consts[7]
                xr = wr[rows, :] + lr * cr - li * ci
                xi = wi[rows, :] + lr * ci + li * cr
                wr[rows, :], wi[rows, :] = xr, xi
                shape = (SUBLANES, ST_TILE)
                return (jnp.broadcast_to(xr[last_row:last_row + 1], shape),
                        jnp.broadcast_to(xi[last_row:last_row + 1], shape))

            cr, ci = lax.fori_loop(0, n_blk, blk, (carry[0, :, ls], carry[1, :, ls]), unroll=2)
            carry[0, :, ls], carry[1, :, ls] = cr, ci
            xr_b, xi_b = wr[...].astype(BF16), wi[...].astype(BF16)
            sr_ref[:, ls], si_ref[:, ls] = xr_b, xi_b
            y_ref[:, us] = _dot(xr_b, wor_ref[jt]) + _dot(xi_b, woi_ref[jt])

    whole = lambda a: pl.BlockSpec(a.shape, lambda j, nd=a.ndim: (0,) * nd)
    st_spec = pl.BlockSpec((t_chunk, n_state), lambda j: (chunk_of(j), 0))
    v_spec = pl.BlockSpec((t_chunk, width), lambda j: (chunk_of(j), 0))
    first, last = _grid_ends((n_chunks,))
    return _call(
        body, name=name, grid=(n_chunks,),
        in_specs=[v_spec, whole(win_re), whole(win_im), whole(tabs), whole(wo_re), whole(wo_im)],
        out_specs=[st_spec, st_spec, v_spec],
        out_shape=[jax.ShapeDtypeStruct((seq, n_state), BF16)] * 2 + [jax.ShapeDtypeStruct((seq, width), F32)],
        scratch_shapes=[pltpu.VMEM((2, SUBLANES, n_state), F32), pltpu.VMEM((t_chunk, ST_TILE), F32),
                        pltpu.VMEM((t_chunk, ST_TILE), F32)],
        args=(v, win_re, win_im, tabs, wo_re, wo_im), ride=ride, first=first, last=last)


def _s5_reduce(x_re, x_im, a_re, a_im, u, dy, *, name, t_chunk=512, ride=None):
    seq, n_state = x_re.shape
    width = u.shape[1]
    n_tiles = width // U_TILE
    t_chunk = _tile(seq, t_chunk)

    def body(xr_ref, xi_ref, ar_ref, ai_ref, u_ref, dy_ref, rr_ref, ri_ref, gbr_ref, gbi_ref, gcr_ref, gci_ref):
        xrb, xib, arb, aib = xr_ref[...], xi_ref[...], ar_ref[...], ai_ref[...]
        xr, xi, ar, ai = xrb.astype(F32), xib.astype(F32), arb.astype(F32), aib.astype(F32)
        ub, dyb = u_ref[...].astype(BF16), dy_ref[...].astype(BF16)
        parts = (jnp.sum(ar * xr + ai * xi, axis=0, keepdims=True), jnp.sum(ai * xr - ar * xi, axis=0, keepdims=True),
                 _dot(arb, ub, _TN), _dot(aib, ub, _TN), _dot(xrb, dyb, _TN), _dot(xib, dyb, _TN))
        first = pl.program_id(1) == 0
        for ref, val in zip((rr_ref, ri_ref, gbr_ref, gbi_ref, gcr_ref, gci_ref), parts):
            @pl.when(first)
            def _():
                ref[...] = val

            @pl.when(jnp.logical_not(first))
            def _():
                ref[...] += val

    st_spec = pl.BlockSpec((t_chunk, ST_TILE), lambda j, t: (t, j))
    u_spec = pl.BlockSpec((t_chunk, U_TILE), lambda j, t: (t, j))
    r_spec = pl.BlockSpec((1, ST_TILE), lambda j, t: (0, j))
    g_spec = pl.BlockSpec((None, ST_TILE, U_TILE), lambda j, t: (j, 0, 0))
    first, last = _grid_ends((n_tiles, seq // t_chunk))
    return _call(
        body, name=name, grid=(n_tiles, seq // t_chunk),
        in_specs=[st_spec] * 4 + [u_spec] * 2,
        out_specs=[r_spec, r_spec] + [g_spec] * 4,
        out_shape=[jax.ShapeDtypeStruct((1, n_state), F32)] * 2
        + [jax.ShapeDtypeStruct((n_tiles, ST_TILE, U_TILE), F32)] * 4,
        scratch_shapes=[], args=(x_re, x_im, a_re, a_im, u, dy), ride=ride, first=first, last=last)


def _block_diag_in(ms):
    m = jnp.stack(ms)
    n, g, c, p = m.shape
    m5 = m.reshape(n, g // GROUPS_PER_TILE, GROUPS_PER_TILE, c, p)
    eye = jnp.eye(GROUPS_PER_TILE, dtype=m.dtype)
    out = m5[:, :, :, :, None, :] * eye[None, None, :, None, :, None]
    return out.astype(BF16).reshape(n, g // GROUPS_PER_TILE, GROUPS_PER_TILE * c, GROUPS_PER_TILE * p)


def _block_diag_take(m, c, p):
    t = m.shape[0]
    m5 = m.reshape(t, GROUPS_PER_TILE, p, GROUPS_PER_TILE, c)
    idx = jnp.arange(GROUPS_PER_TILE)
    return m5[:, idx, :, idx, :].transpose(1, 0, 2, 3).reshape(t * GROUPS_PER_TILE, p, c)


def _scan_tables(pw_re, pw_im, reverse):
    row = jnp.arange(SUBLANES)[:, None]
    tabs = []
    for k in (1, 2, 4):
        keep = (row <= SUBLANES - 1 - k) if reverse else (row >= k)
        tabs += [jnp.where(keep, pw_re[k - 1][None, :], 0.0), jnp.where(keep, pw_im[k - 1][None, :], 0.0)]
    order = jnp.arange(SUBLANES)[::-1] if reverse else jnp.arange(SUBLANES)
    tabs += [pw_re[order], pw_im[order]]
    return jnp.stack(tabs)


def _partial_sums(slabs, from_sibling, names):
    x, y, c = _mesh_place()
    theirs = jnp.stack([_slab(px, py, c) for px, py in _chips(x, y)[1:]]).astype(jnp.int32)
    out = []
    for s, f, n in zip(slabs, from_sibling, names):
        rows, cols = s.shape[1:]
        tr = _tile(rows, 512)

        def body(idx_ref, a_ref, b_ref, o_ref):
            o_ref[...] = (a_ref[...] + b_ref[...]).astype(BF16)

        out.append(pl.pallas_call(
            body, name=f"reduce_add_{n}",
            grid_spec=pltpu.PrefetchScalarGridSpec(
                num_scalar_prefetch=1, grid=(3, rows // tr),
                in_specs=[pl.BlockSpec((None, tr, cols), lambda k, i, idx: (idx[k], i, 0)),
                          pl.BlockSpec((None, tr, cols), lambda k, i, idx: (k + 1, i, 0))],
                out_specs=pl.BlockSpec((None, tr, cols), lambda k, i, idx: (k, i, 0))),
            out_shape=jax.ShapeDtypeStruct((3, rows, cols), BF16), compiler_params=_params(),
        )(theirs, s, f))
    return out


def _local_step(x, target, p, w_in, shards):
    seq, d_model = x.shape
    a_width = p["g_out_attn"].shape[-1]
    s_width = p["g_out_ssm"].shape[-1]
    n_heads = a_width // HEAD_DIM
    n_hg = n_heads // HEADS_PER_GROUP
    n_groups = s_width // SSM_C
    n_sh, _, in_sh = w_in.shape
    f_sh = shards["w_ffn_gate"].shape[-1]
    w = {"w_in": w_in}
    slab3 = lambda g, n: g.reshape(N_DEV, -1, shards[n].shape[-1])
    t2, t1 = _tile(seq, 2048), _tile(seq, 1024)
    n2, n1 = seq // t2, seq // t1

    twice = lambda f: (lambda *a: (f(*a),) * 2)
    h1, h1_t = _rowwise(twice(x_norm), [x], [p["g_mix"]], [(d_model, BF16)], flipped=[(d_model, BF16, 1)],
                        name="rms_mix")
    z = _mm(h1, w["w_in"], name="in_proj", grid=(n2, n_sh),
            a_spec=pl.BlockSpec((t2, d_model), lambda i, j: (i, 0)),
            b_spec=pl.BlockSpec((None, d_model, in_sh), lambda i, j: (j, 0, 0)),
            o_spec=pl.BlockSpec((t2, in_sh), lambda i, j: (i, j)), o_shape=(seq, n_sh * in_sh), dims="nn")
    qg4 = jnp.tile(p["q_gain"], (1, HEADS_PER_GROUP))
    kg4 = jnp.tile(p["k_gain"], (1, HEADS_PER_GROUP))
    spread = _bias_spread()
    btab = _bias_table(p["rpb"], spread)
    ya, got_a = _attn_fwd(z, qg4, kg4, btab, ride=_gather_first([shards["w_ffn_gate"], shards["w_ffn_up"]]))
    u = z[:, 3 * a_width:]

    n_col = 2 * n_groups * SSM_P
    col = lambda a: a.reshape(n_col, 1)
    a_re_c, a_im_c = col(p["ssm_a_re"]), col(p["ssm_a_im"])
    dt_c = col(jnp.broadcast_to(jnp.exp(p["ssm_log_step"])[:, :, None], (2, n_groups, SSM_P)))
    b_re_c, b_im_c = p["ssm_b_re"].reshape(n_col, SSM_C), p["ssm_b_im"].reshape(n_col, SSM_C)
    pw_re, pw_im, cf_re, cf_im, bb_re, bb_im = _s5_discretize(a_re_c, a_im_c, dt_c, b_re_c, b_im_c)
    n_state = n_groups * SSM_P
    pw_re = pw_re.reshape(2, n_state, SUBLANES).transpose(0, 2, 1)
    pw_im = pw_im.reshape(2, n_state, SUBLANES).transpose(0, 2, 1)
    bb_re4, bb_im4 = bb_re.reshape(2, n_groups, SSM_P, SSM_C), bb_im.reshape(2, n_groups, SSM_P, SSM_C)
    c_re, c_im = p["ssm_c_re"], p["ssm_c_im"]
    t21 = lambda a: a.transpose(0, 2, 1)
    maps_in = _block_diag_in([m for d in range(2) for m in (t21(bb_re4[d]), t21(bb_im4[d]), c_re[d], -c_im[d])])
    maps_out = _block_diag_in([m for d in range(2) for m in (t21(c_re[d]), -t21(c_im[d]), bb_re4[d], bb_im4[d])])
    fwd, bwd_in = [], []
    got_b = None
    for d in range(2):
        rev = d == 1
        tabs = _scan_tables(pw_re[d], pw_im[d], rev)
        if d == 0:
            ride = _gather_first([shards["w_glu"], shards["w_out"]])
        else:
            ride = _gather_second(got_a + got_b) + _gather_first([shards["w_ffn_down"]])
        (xs_re, xs_im, y_d), got = _s5_scan(u, maps_in[4 * d], maps_in[4 * d + 1], tabs, maps_out[4 * d],
                                            maps_out[4 * d + 1], reverse=rev, name=f"s5_fwd_{d}", ride=ride)
        if d == 0:
            got_b = got
        fwd.append((xs_re, xs_im, y_d))
        bwd_in.append((maps_in[4 * d + 2], maps_in[4 * d + 3], _scan_tables(pw_re[d], -pw_im[d], not rev),
                       maps_out[4 * d + 2], maps_out[4 * d + 3]))
    w["w_gate"], w["w_up"], w_glu_full, w_out_full, w_down_first = got
    w["w_glu"] = w_glu_full.reshape(-1, s_width)
    w["w_out"] = w_out_full.reshape(-1, d_model)

    ypre, yg, yg_t = _rowwise(s5_mid, [fwd[0][2], fwd[1][2], u], [p["ssm_d"]], [(s_width, F32), (s_width, F32)],
                              flipped=[(s_width, BF16, 1)], name="s5_skip_gelu")
    t_glu = _mm_plain(yg, w["w_glu"], "nn", name="glu_proj", tn=s_width)
    y_cat, y_cat_t = _rowwise(twice(mix_out_fwd), [ya, yg, t_glu], [p["b_glu"], p["g_out_attn"], p["g_out_ssm"]],
                              [(a_width + s_width, BF16)], flipped=[(a_width + s_width, BF16, 1)], name="mix_out")
    x1, (w["w_down"],) = _mm_plain(y_cat, w["w_out"], "nn", name="out_proj", res=x, tn=2048,
                                   ride=_gather_second([w_down_first]))

    h2, h2_t = _rowwise(twice(x_norm), [x1], [p["g_ffn"]], [(d_model, BF16)], flipped=[(d_model, BF16, 1)],
                        name="rms_ffn")
    ffn_up = functools.partial(
        _mm, grid=(n2, n_sh), a_spec=pl.BlockSpec((t2, d_model), lambda i, j: (i, 0)),
        b_spec=pl.BlockSpec((None, d_model, f_sh), lambda i, j: (j, 0, 0)),
        o_spec=pl.BlockSpec((None, t2, f_sh), lambda i, j: (j, i, 0)), o_shape=(n_sh, seq, f_sh), dims="nn",
        out_dtype=BF16)
    gate = ffn_up(h2, w["w_gate"], name="ffn_gate")
    up = ffn_up(h2, w["w_up"], name="ffn_up")
    flat = lambda a: a.reshape(n_sh * seq, f_sh)
    act, act_t = _rowwise(twice(swiglu_fwd), [flat(gate), flat(up)], [], [(f_sh, BF16)],
                          flipped=[(f_sh, BF16, n_sh)], name="swiglu", tm=1024)
    act, act_t = act.reshape(n_sh, seq, f_sh), act_t.reshape(n_sh, f_sh, seq)
    ffn_out = _mm(act, w["w_down"], name="ffn_down", grid=(n1, n_sh // 2), groups=2,
                  a_spec=pl.BlockSpec((2, t1, f_sh), lambda i, j: (j, i, 0)),
                  b_spec=pl.BlockSpec((2, f_sh, d_model), lambda i, j: (j, 0, 0)),
                  o_spec=pl.BlockSpec((t1, d_model), lambda i, j: (i, 0)), o_shape=(seq, d_model), dims="nn",
                  k_axis=1)

    dx2, dx2_b, sq = _rowwise(functools.partial(loss_head, inv_d=1.0 / d_model), [ffn_out, x1, target], [],
                              [(d_model, F32), (d_model, BF16)], [d_model], name="loss_head")
    loss = 0.5 * jnp.sum(sq) / d_model

    d_act = _mm(dx2_b, w["w_down"], name="ffn_down_dx", grid=(n2, n_sh),
                a_spec=pl.BlockSpec((t2, d_model), lambda i, j: (i, 0)),
                b_spec=pl.BlockSpec((None, f_sh, d_model), lambda i, j: (j, 0, 0)),
                o_spec=pl.BlockSpec((None, t2, f_sh), lambda i, j: (j, i, 0)), o_shape=(n_sh, seq, f_sh), dims="nt",
                out_dtype=BF16)
    g_w_down = _mm(act_t, dx2_b, name="ffn_down_dw", grid=(n_sh, n2),
                   a_spec=pl.BlockSpec((None, f_sh, t2), lambda j, k: (j, 0, k)),
                   b_spec=pl.BlockSpec((t2, d_model), lambda j, k: (k, 0)),
                   o_spec=pl.BlockSpec((None, f_sh, d_model), lambda j, k: (j, 0, 0)),
                   o_shape=(n_sh, f_sh, d_model), dims="nn", k_axis=1)
    d_gate, d_up = _rowwise(swiglu_bwd, [flat(d_act), flat(gate), flat(up)], [], [(f_sh, BF16), (f_sh, BF16)],
                            name="swiglu_bwd", tm=1024)
    d_gate, d_up = d_gate.reshape(n_sh, seq, f_sh), d_up.reshape(n_sh, seq, f_sh)
    d_h2 = _mm(d_gate, w["w_gate"], second=(d_up, w["w_up"]), name="ffn_up_gate_dx", grid=(n1, n_sh),
               a_spec=pl.BlockSpec((None, t1, f_sh), lambda i, j: (j, i, 0)),
               b_spec=pl.BlockSpec((None, d_model, f_sh), lambda i, j: (j, 0, 0)),
               o_spec=pl.BlockSpec((t1, d_model), lambda i, j: (i, 0)), o_shape=(seq, d_model), dims="nt", k_axis=1)
    ffn_dw = functools.partial(
        _mm, grid=(n_sh, n2), a_spec=pl.BlockSpec((d_model, t2), lambda j, k: (0, k)),
        b_spec=pl.BlockSpec((None, t2, f_sh), lambda j, k: (j, k, 0)),
        o_spec=pl.BlockSpec((None, d_model, f_sh), lambda j, k: (j, 0, 0)), o_shape=(n_sh, d_model, f_sh), dims="nn",
        k_axis=1)
    g_w_gate = ffn_dw(h2_t, d_gate, name="ffn_gate_dw")
    g_w_up = ffn_dw(h2_t, d_up, name="ffn_up_dw")
    dx1, g_g_ffn = _rowwise(residual_rms_bwd, [dx2, d_h2, x1], [p["g_ffn"]], [(d_model, F32)], [d_model],
                            name="rms_ffn_bwd")

    d_ycat = _mm_plain(dx1, w["w_out"], "nt", name="out_proj_dx", tn=2048)
    mix_w = a_width + s_width
    tm_o = _tile(mix_w, 1024)
    g_w_out = _mm(y_cat_t, dx1, name="out_proj_dw", grid=(mix_w // tm_o, n1),
                  a_spec=pl.BlockSpec((tm_o, t1), lambda i, k: (i, k)),
                  b_spec=pl.BlockSpec((t1, d_model), lambda i, k: (k, 0)),
                  o_spec=pl.BlockSpec((tm_o, d_model), lambda i, k: (i, 0)), o_shape=(mix_w, d_model), dims="nn",
                  k_axis=1)
    (d_ya, d_yg_direct, d_t, g_goa, g_gos, g_b_glu) = _rowwise(
        functools.partial(mix_out_bwd, a_width=a_width), [d_ycat, ya, yg, t_glu],
        [p["b_glu"], p["g_out_attn"], p["g_out_ssm"]],
        [(a_width, F32), (s_width, F32), (s_width, BF16)], [a_width, s_width, s_width], name="mix_out_bwd")
    d_yg = _mm_plain(d_t, w["w_glu"], "nt", name="glu_proj_dx", res=d_yg_direct, tn=s_width)
    g_w_glu = _mm(yg_t, d_t, name="glu_proj_dw", grid=(1, n1),
                  a_spec=pl.BlockSpec((s_width, t1), lambda i, k: (0, k)),
                  b_spec=pl.BlockSpec((t1, s_width), lambda i, k: (k, 0)),
                  o_spec=pl.BlockSpec((s_width, s_width), lambda i, k: (0, 0)), o_shape=(s_width, s_width),
                  dims="nn", k_axis=1)
    d_ypre, du_skip, g_ssm_d = _rowwise(gelu_skip_bwd, [d_yg, ypre, u], [p["ssm_d"]],
                                        [(s_width, F32), (s_width, F32)], [s_width], name="s5_skip_gelu_bwd")

    ffn_names, mix_names = ("w_ffn_gate", "w_ffn_up", "w_ffn_down"), ("w_glu", "w_out")
    ffn_slabs = [slab3(g, n) for g, n in zip((g_w_gate, g_w_up, g_w_down), ffn_names)]
    mix_slabs = [slab3(g, n) for g, n in zip((g_w_glu, g_w_out), mix_names)]
    du_dirs, adj, r_parts, gb_parts, gc_parts = [], [], [], [], []
    sib, part = {}, {}
    for d, (names, slabs) in enumerate(((ffn_names, ffn_slabs), (mix_names, mix_slabs))):
        win_re, win_im, tabs, wo_re, wo_im = bwd_in[d]
        (as_re, as_im, du_d), got = _s5_scan(d_ypre, win_re, win_im, tabs, wo_re, wo_im, reverse=(d == 0),
                                             name=f"s5_bwd_{d}", ride=_reduce_sibling(slabs))
        du_dirs.append(du_d)
        adj.append((as_re, as_im))
        sib[names] = got
        part[names] = _partial_sums(slabs, got, names)
    for d in range(2):
        (r_re, r_im, gbt_re, gbt_im, gct_re, gct_im), got = _s5_reduce(
            fwd[d][0], fwd[d][1], adj[d][0], adj[d][1], u, d_ypre, name=f"s5_reduce_{d}",
            ride=_reduce_chips(part[mix_names] if d == 0 else part[ffn_names][2:]))
        if d == 0:
            mix_chips = got
        else:
            down_chips = got
        r_parts.append((r_re.reshape(n_state, 1), r_im.reshape(n_state, 1)))
        gb_parts.append((_block_diag_take(gbt_re, SSM_C, SSM_P), _block_diag_take(gbt_im, SSM_C, SSM_P)))
        gc_parts.append((_block_diag_take(gct_re, SSM_C, SSM_P), _block_diag_take(gct_im, SSM_C, SSM_P)))
    cat = lambda i, parts: jnp.concatenate([parts[0][i], parts[1][i]], axis=0)
    gbb_re, gbb_im = cat(0, gb_parts).reshape(n_col, SSM_C), cat(1, gb_parts).reshape(n_col, SSM_C)
    g_a_re, g_a_im, g_ls, g_b_re, g_b_im = _s5_param_grads(
        a_re_c, a_im_c, dt_c, b_re_c, b_im_c, pw_re[:, 0].reshape(n_col, 1), pw_im[:, 0].reshape(n_col, 1),
        cf_re, cf_im, bb_re, bb_im, cat(0, r_parts), cat(1, r_parts), gbb_re, gbb_im)
    g_c_re = cat(0, gc_parts).reshape(2, n_groups, SSM_P, SSM_C).transpose(0, 1, 3, 2)
    g_c_im = -cat(1, gc_parts).reshape(2, n_groups, SSM_P, SSM_C).transpose(0, 1, 3, 2)

    (d_q, d_k, d_v, d_btab, g_qg, g_kg), ffn_chips = _attn_bwd(z, d_ya, qg4, kg4, btab,
                                                                ride=_reduce_chips(part[ffn_names][:2]))
    ffn_chips = ffn_chips + down_chips
    d_u = _rowwise(lambda a, b, c: a + b + c, [du_dirs[0], du_dirs[1], du_skip], [], [(s_width, BF16)],
                   name="s5_du_sum")[0]
    d_z = jnp.concatenate([d_q, d_k, d_v, d_u], axis=1)
    fold_heads = lambda g: g.reshape(n_heads, HEAD_DIM).sum(axis=0, keepdims=True)
    small = {
        "q_gain": fold_heads(g_qg), "k_gain": fold_heads(g_kg), "rpb": _bias_grad(d_btab, spread, n_heads),
        "ssm_a_re": g_a_re.reshape(2, n_groups, SSM_P), "ssm_a_im": g_a_im.reshape(2, n_groups, SSM_P),
        "ssm_b_re": g_b_re.reshape(2, n_groups, SSM_P, SSM_C), "ssm_b_im": g_b_im.reshape(2, n_groups, SSM_P, SSM_C),
        "ssm_c_re": g_c_re, "ssm_c_im": g_c_im,
        "ssm_log_step": g_ls.reshape(2, n_groups, SSM_P).sum(axis=-1),
        "ssm_d": g_ssm_d, "b_glu": g_b_glu, "g_out_attn": g_goa, "g_out_ssm": g_gos, "g_ffn": g_g_ffn,
    }
    packed = _pack([small[n] for n in SMALL_PACKED])
    g_w_in, got = _mm(h1_t, d_z, name="in_proj_dw", grid=(n_sh, n2),
                      a_spec=pl.BlockSpec((d_model, t2), lambda j, k: (0, k)),
                      b_spec=pl.BlockSpec((t2, in_sh), lambda j, k: (k, j)),
                      o_spec=pl.BlockSpec((None, d_model, in_sh), lambda j, k: (j, 0, 0)),
                      o_shape=(n_sh, d_model, in_sh), dims="nn", k_axis=1, ride=_gather_first([packed]))
    d_h1, got = _mm(d_z, w["w_in"], name="in_proj_dx", grid=(n1, n_sh // 2), groups=2,
                    a_spec=pl.BlockSpec((t1, 2 * in_sh), lambda i, j: (i, j)),
                    b_spec=pl.BlockSpec((2, d_model, in_sh), lambda i, j: (j, 0, 0)),
                    o_spec=pl.BlockSpec((t1, d_model), lambda i, j: (i, 0)), o_shape=(seq, d_model), dims="nt",
                    k_axis=1, ride=_gather_second(got) + _reduce_sibling([g_w_in]))
    small_gathered, in_sibling = got
    in_part = _partial_sums([g_w_in], [in_sibling], ("w_in",))
    (grad_x, g_g_mix), (in_chips,) = _rowwise(residual_rms_bwd, [dx1, d_h1, x], [p["g_mix"]], [(d_model, F32)],
                                              [d_model], name="rms_mix_bwd", ride=_reduce_chips(in_part))
    reduced = {"w_in": (g_w_in, in_sibling, in_chips)}
    for names, slabs, chips in ((ffn_names, ffn_slabs, ffn_chips), (mix_names, mix_slabs, mix_chips)):
        for i, n in enumerate(names):
            reduced[n] = (slabs[i], sib[names][i], chips[i])
    return loss, grad_x, small_gathered, g_g_mix, reduced


def x_norm(xv, g):
    return xv * _rstd(xv) * g


def s5_mid(y0, y1, uv, d_skip):
    ypre = y0 + y1 + d_skip * uv
    yg = _gelu(ypre)
    return ypre, yg, yg


def mix_out_fwd(ya, yg, t, b_glu, g_oa, g_os):
    ys = yg * _sigmoid(t + b_glu)
    return jnp.concatenate([ya * _rstd(ya) * g_oa, ys * _rstd(ys) * g_os], axis=1)


def mix_out_bwd(d_y, ya, yg, t, b_glu, g_oa, g_os, *, a_width):
    sg = _sigmoid(t + b_glu)
    ys = yg * sg
    d_ya, c_goa = _rms_bwd(d_y[:, :a_width], ya, g_oa)
    d_ys, c_gos = _rms_bwd(d_y[:, a_width:], ys, g_os)
    d_t = d_ys * yg * sg * (1.0 - sg)
    return d_ya, d_ys * sg, d_t, c_goa, c_gos, d_t


def gelu_skip_bwd(d_yg, ypre, uv, d_skip):
    d_ypre = d_yg * _gelu_grad(ypre)
    return d_ypre, d_ypre * d_skip, d_ypre * uv


def swiglu_fwd(gv, uv):
    gv, uv = gv.astype(F32), uv.astype(F32)
    return gv * _sigmoid(gv) * uv


def swiglu_bwd(d_act, gv, uv):
    d_act, gv, uv = d_act.astype(F32), gv.astype(F32), uv.astype(F32)
    sg = _sigmoid(gv)
    return d_act * uv * (sg * (1.0 + gv * (1.0 - sg))), d_act * gv * sg


def loss_head(ffn_out, x1, target, *, inv_d):
    diff = ffn_out + x1 - target
    return diff * inv_d, diff * inv_d, diff * diff


def residual_rms_bwd(d_res, d_h, xv, g):
    dx, c_g = _rms_bwd(d_h, xv, g)
    return d_res + dx, c_g


_ANY = pl.BlockSpec(memory_space=pl.ANY)


def _mesh_place():
    return lax.axis_index("x"), lax.axis_index("y"), lax.axis_index("c")


def _chips(x, y):
    return [(x, y), (1 - x, y), (x, 1 - y), (1 - x, 1 - y)]


def _slab(px, py, pc):
    return 4 * px + 2 * py + pc


def _all_gather(arrs, *, name):
    n = len(arrs)

    def body(*refs):
        in_refs, out_refs = refs[:n], refs[n:2 * n]
        send_sems, recv_sems, local_sems = refs[2 * n:]
        x, y, c = _mesh_place()
        me, sibling = (x, y, c), (x, y, 1 - c)
        others = _chips(x, y)[1:]

        def copy(w, k, block, to, src=None):
            dst = out_refs[w].at[_slab(*block)]
            return pltpu.make_async_remote_copy(
                src_ref=dst if src is None else src, dst_ref=dst, send_sem=send_sems.at[7 * w + k],
                recv_sem=recv_sems.at[7 * w + k], device_id=to, device_id_type=MESH)

        mine = [pltpu.make_async_copy(in_refs[w], out_refs[w].at[_slab(*me)], local_sems.at[w]) for w in range(n)]
        first = []
        for w in range(n):
            mine[w].start()
            first.append(copy(w, 0, me, sibling, src=in_refs[w]))
            first += [copy(w, 1 + j, me, (*chip, c), src=in_refs[w]) for j, chip in enumerate(others)]
        for cp in first:
            cp.start()
        passed = []
        for j, chip in enumerate(others):
            for w in range(n):
                copy(w, 1 + j, (*chip, c), me).wait_recv()
                fwd = copy(w, 4 + j, (*chip, c), sibling)
                fwd.start()
                passed.append(fwd)
        for w in range(n):
            copy(w, 0, sibling, me).wait_recv()
        for j, chip in enumerate(others):
            for w in range(n):
                copy(w, 4 + j, (*chip, 1 - c), me).wait_recv()
        for cp in first + passed:
            cp.wait_send()
        for cp in mine:
            cp.wait()

    return pl.pallas_call(
        body, name=name, in_specs=[_ANY] * n, out_specs=[_ANY] * n,
        out_shape=[jax.ShapeDtypeStruct((N_DEV,) + a.shape, a.dtype) for a in arrs],
        scratch_shapes=[pltpu.SemaphoreType.DMA((7 * n,)), pltpu.SemaphoreType.DMA((7 * n,)),
                        pltpu.SemaphoreType.DMA((n,))],
        compiler_params=pltpu.CompilerParams(has_side_effects=True),
    )(*arrs)


def _adamw(w, m, v, parts, *, name, slab, tr=256):
    rows, cols = w.shape
    tr = _tile(rows, tr)
    n_p = len(parts)

    def body(slab_ref, *refs):
        w_ref, m_ref, v_ref = refs[:3]
        p_refs = refs[3:3 + n_p]
        g_ref, d_ref, nm_ref, nv_ref = refs[3 + n_p:]
        g = None
        for (_, lead), r in zip(parts, p_refs):
            for piece in ([r[...]] if lead is None else [r[i] for i in range(lead)]):
                g = piece.astype(F32) if g is None else g + piece.astype(F32)
        new_m = ADAM_B1 * m_ref[...] + (1.0 - ADAM_B1) * g
        new_v = ADAM_B2 * v_ref[...] + (1.0 - ADAM_B2) * (g * g)
        m_hat = new_m / (1.0 - ADAM_B1 ** ADAM_STEP)
        v_hat = new_v / (1.0 - ADAM_B2 ** ADAM_STEP)
        g_ref[...] = g
        d_ref[...] = -ADAM_LR * (m_hat / (jnp.sqrt(v_hat) + ADAM_EPS) + ADAM_WD * w_ref[...])
        nm_ref[...] = new_m
        nv_ref[...] = new_v

    tile = pl.BlockSpec((tr, cols), lambda i, s: (i, 0))
    p_specs = [pl.BlockSpec((None, tr, cols), lambda i, s: (s[0], i, 0)) if lead is None
               else pl.BlockSpec((lead, tr, cols), lambda i, s: (0, i, 0)) for _, lead in parts]
    return pl.pallas_call(
        body, name=name,
        grid_spec=pltpu.PrefetchScalarGridSpec(num_scalar_prefetch=1, grid=(rows // tr,),
                                               in_specs=[tile] * 3 + p_specs, out_specs=[tile] * 4),
        out_shape=[jax.ShapeDtypeStruct((rows, cols), F32)] * 4, compiler_params=_params(),
    )(jnp.reshape(slab, (1,)).astype(jnp.int32), w, m, v, *[a for a, _ in parts])


_PACK_TILE = SUBLANES * 128
_PACK_ROWS = 512


def _pack(arrs):
    flat = []
    for a in arrs:
        f = a.reshape(-1)
        flat.append(jnp.pad(f, (0, (-f.shape[0]) % _PACK_TILE)))
    total = sum(f.shape[0] for f in flat)
    flat.append(jnp.zeros(((-total) % (_PACK_ROWS * 128),), F32))
    return jnp.concatenate(flat).reshape(-1, 128)


def _unpack(buf, shapes):
    out, at = [], 0
    flat = buf.reshape(-1)
    for s in shapes:
        n = math.prod(s)
        out.append(flat[at:at + n].reshape(s))
        at += n + (-n) % _PACK_TILE
    return out


BIG = ("w_in", "w_glu", "w_out", "w_ffn_gate", "w_ffn_up", "w_ffn_down")
WEIGHTS = ("g_mix", "w_in", "q_gain", "k_gain", "rpb", "ssm_a_re", "ssm_a_im", "ssm_b_re", "ssm_b_im", "ssm_c_re",
           "ssm_c_im", "ssm_log_step", "ssm_d", "w_glu", "b_glu", "g_out_attn", "g_out_ssm", "w_out", "g_ffn",
           "w_ffn_gate", "w_ffn_up", "w_ffn_down")
SMALL = tuple(n for n in WEIGHTS if n not in BIG)
SMALL_PACKED = tuple(n for n in SMALL if n != "g_mix")
VECTORS = ("g_mix", "q_gain", "k_gain", "ssm_d", "b_glu", "g_out_attn", "g_out_ssm", "g_ffn")


def kernel(x, g_mix, w_in, q_gain, k_gain, rpb, ssm_a_re, ssm_a_im, ssm_b_re, ssm_b_im, ssm_c_re, ssm_c_im, ssm_log_step, ssm_d, w_glu, b_glu, g_out_attn, g_out_ssm, w_out, g_ffn, w_ffn_gate, w_ffn_up, w_ffn_down, loss_target, m_g_mix, m_w_in, m_q_gain, m_k_gain, m_rpb, m_ssm_a_re, m_ssm_a_im, m_ssm_b_re, m_ssm_b_im, m_ssm_c_re, m_ssm_c_im, m_ssm_log_step, m_ssm_d, m_w_glu, m_b_glu, m_g_out_attn, m_g_out_ssm, m_w_out, m_g_ffn, m_w_ffn_gate, m_w_ffn_up, m_w_ffn_down, v_g_mix, v_w_in, v_q_gain, v_k_gain, v_rpb, v_ssm_a_re, v_ssm_a_im, v_ssm_b_re, v_ssm_b_im, v_ssm_c_re, v_ssm_c_im, v_ssm_log_step, v_ssm_d, v_w_glu, v_b_glu, v_g_out_attn, v_g_out_ssm, v_w_out, v_g_ffn, v_w_ffn_gate, v_w_ffn_up, v_w_ffn_down):
    wts = dict(g_mix=g_mix, w_in=w_in, q_gain=q_gain, k_gain=k_gain, rpb=rpb, ssm_a_re=ssm_a_re, ssm_a_im=ssm_a_im,
               ssm_b_re=ssm_b_re, ssm_b_im=ssm_b_im, ssm_c_re=ssm_c_re, ssm_c_im=ssm_c_im, ssm_log_step=ssm_log_step,
               ssm_d=ssm_d, w_glu=w_glu, b_glu=b_glu, g_out_attn=g_out_attn, g_out_ssm=g_out_ssm, w_out=w_out,
               g_ffn=g_ffn, w_ffn_gate=w_ffn_gate, w_ffn_up=w_ffn_up, w_ffn_down=w_ffn_down)
    mom = dict(g_mix=m_g_mix, w_in=m_w_in, q_gain=m_q_gain, k_gain=m_k_gain, rpb=m_rpb, ssm_a_re=m_ssm_a_re,
               ssm_a_im=m_ssm_a_im, ssm_b_re=m_ssm_b_re, ssm_b_im=m_ssm_b_im, ssm_c_re=m_ssm_c_re,
               ssm_c_im=m_ssm_c_im, ssm_log_step=m_ssm_log_step, ssm_d=m_ssm_d, w_glu=m_w_glu, b_glu=m_b_glu,
               g_out_attn=m_g_out_attn, g_out_ssm=m_g_out_ssm, w_out=m_w_out, g_ffn=m_g_ffn,
               w_ffn_gate=m_w_ffn_gate, w_ffn_up=m_w_ffn_up, w_ffn_down=m_w_ffn_down)
    var = dict(g_mix=v_g_mix, w_in=v_w_in, q_gain=v_q_gain, k_gain=v_k_gain, rpb=v_rpb, ssm_a_re=v_ssm_a_re,
               ssm_a_im=v_ssm_a_im, ssm_b_re=v_ssm_b_re, ssm_b_im=v_ssm_b_im, ssm_c_re=v_ssm_c_re,
               ssm_c_im=v_ssm_c_im, ssm_log_step=v_ssm_log_step, ssm_d=v_ssm_d, w_glu=v_w_glu, b_glu=v_b_glu,
               g_out_attn=v_g_out_attn, g_out_ssm=v_g_out_ssm, w_out=v_w_out, g_ffn=v_g_ffn,
               w_ffn_gate=v_w_ffn_gate, w_ffn_up=v_w_ffn_up, w_ffn_down=v_w_ffn_down)
    ix, iy, ic = _mesh_place()
    me = _slab(ix, iy, ic)
    d_model = x.shape[-1]

    shard = {n: wts[n][0] for n in BIG}
    shard_b = {n: shard[n].astype(BF16) for n in BIG}
    w_in_full = _all_gather([shard_b["w_in"]], name="gather_w_in")[0]
    p = {n: (wts[n][0].reshape(1, -1) if n in VECTORS else wts[n][0]) for n in SMALL}

    loss, grad_x, small_gathered, g_g_mix, reduced = _local_step(x[0], loss_target[0], p, w_in_full,
                                                                 {n: shard_b[n] for n in BIG if n != "w_in"})
    loss = lax.psum(loss, ("x", "y", "c"))
    out = {}
    for n in BIG:
        slabs, from_sibling, from_chips = reduced[n]
        rows, cols = slabs.shape[1:]
        res = _adamw(shard[n].reshape(rows, cols), mom[n][0].reshape(rows, cols), var[n][0].reshape(rows, cols),
                     [(slabs, None), (from_sibling, 1), (from_chips, 3)], name=f"adamw_{n}", slab=me)
        out[n] = [r.reshape(wts[n].shape) for r in res]

    order = list(SMALL_PACKED)
    shapes = [wts[n].shape for n in order]
    res = _adamw(_pack([wts[n] for n in order]), _pack([mom[n] for n in order]), _pack([var[n] for n in order]),
                 [(small_gathered, N_DEV)], name="adamw_small", slab=me)
    for kind, buf in enumerate(res):
        for n, a in zip(order, _unpack(buf, shapes)):
            out.setdefault(n, [None] * 4)[kind] = a
    as_rows = lambda a: a.reshape(-1, 128)
    g_mix_all = _all_gather([as_rows(g_g_mix)], name="gather_g_mix")[0]
    res = _adamw(as_rows(wts["g_mix"]), as_rows(mom["g_mix"]), as_rows(var["g_mix"]), [(g_mix_all, N_DEV)],
                 name="adamw_g_mix", slab=me)
    out["g_mix"] = [r.reshape(wts["g_mix"].shape) for r in res]

    return (loss, grad_x[None], *[out[n][0] for n in WEIGHTS], *[out[n][1] for n in WEIGHTS],
            *[out[n][2] for n in WEIGHTS], *[out[n][3] for n in WEIGHTS])
```

```python
import functools
import math

import jax
import jax.numpy as jnp
from jax import lax
from jax.experimental import pallas as pl
from jax.experimental.pallas import tpu as pltpu

F32 = jnp.float32
BF16 = jnp.bfloat16

N_DEV = 8
GRID_W = 64
WIN_H = 8
WIN_W = 16
HEAD_DIM = 64
HEADS_PER_GROUP = 4
GROUP_LANES = HEADS_PER_GROUP * HEAD_DIM
SSM_C = 16
SSM_P = 64
GROUPS_PER_TILE = 8
U_TILE = GROUPS_PER_TILE * SSM_C
ST_TILE = GROUPS_PER_TILE * SSM_P
SUBLANES = 8
RMS_EPS = 1e-6
NEG_INF = -1e30
A_RE_MAX = -1e-4
ADAM_LR, ADAM_B1, ADAM_B2, ADAM_EPS, ADAM_WD, ADAM_STEP = 0.001, 0.9, 0.999, 1e-08, 0.01, 10
VMEM_LIMIT_V7X = 56 * 1024 * 1024
MESH = pl.DeviceIdType.MESH

_NN = (((1,), (0,)), ((), ()))
_NT = (((1,), (1,)), ((), ()))
_TN = (((0,), (0,)), ((), ()))
_DIMS = {"nn": _NN, "nt": _NT, "tn": _TN}


def _params(**kw):
    return pltpu.CompilerParams(vmem_limit_bytes=VMEM_LIMIT_V7X, **kw)


def _dot(a, b, dims=_NN):
    return lax.dot_general(a, b, dims, preferred_element_type=F32)


def _mm(a, b, *, name, grid, a_spec, b_spec, o_spec, o_shape, dims, k_axis=None, res=None, out_dtype=F32,
        exact=False, second=None, ride=None, groups=1):
    dn = _DIMS[dims]
    nk = 1 if k_axis is None else grid[k_axis]
    acc_shape = tuple(d for d in o_spec.block_shape if d is not None)
    n_in = 2 + (2 if second is not None else 0)

    def body(*refs):
        a_ref, b_ref = refs[:2]
        r_ref = refs[n_in] if res is not None else None
        o_ref, acc = refs[-2:]
        def product(x_ref, y_ref):
            if groups == 1:
                return _dot(x_ref[...].astype(BF16), y_ref[...].astype(BF16), dn)
            total, width = None, x_ref.shape[-1] // groups
            for s in range(groups):
                x = x_ref[s] if len(x_ref.shape) == 3 else x_ref[:, s * width:(s + 1) * width]
                t = _dot(x.astype(BF16), y_ref[s].astype(BF16), dn)
                total = t if total is None else total + t
            return total

        if exact:
            p = lax.dot_general(a_ref[...], b_ref[...], dn, precision=lax.Precision.HIGHEST,
                                preferred_element_type=F32)
        else:
            p = product(a_ref, b_ref)
        if second is not None:
            p = p + product(refs[2], refs[3])

        def finish(v):
            if r_ref is not None:
                v = v + r_ref[...].astype(F32)
            o_ref[...] = v.astype(out_dtype)

        if nk == 1:
            finish(p)
        else:
            k = pl.program_id(k_axis)

            @pl.when(k == 0)
            def _():
                acc[...] = p

            @pl.when(k > 0)
            def _():
                acc[...] += p

            @pl.when(k == nk - 1)
            def _():
                finish(acc[...])

    ins = [a, b] + (list(second) if second is not None else []) + ([res] if res is not None else [])
    in_specs = [a_spec, b_spec] * (n_in // 2) + ([o_spec] if res is not None else [])
    first, last = _grid_ends(grid)
    (out,), rode = _call(
        body, name=name, grid=grid, in_specs=in_specs, out_specs=[o_spec],
        out_shape=[jax.ShapeDtypeStruct(o_shape, out_dtype)],
        scratch_shapes=[pltpu.VMEM(acc_shape if nk > 1 else (SUBLANES, 128), F32)],
        args=ins, ride=ride, first=first, last=last)
    return out if ride is None else (out, rode)


def _tile(n, want):
    if n <= want:
        return n
    t = want
    while n % t:
        t //= 2
    return t


def _mm_plain(a, b, dims, *, name, res=None, out_dtype=F32, tm=512, tn=512, tk=512, exact=False, ride=None):
    if dims == "nn":
        (m, k), n = a.shape, b.shape[1]
    elif dims == "nt":
        (m, k), n = a.shape, b.shape[0]
    else:
        (k, m), n = a.shape, b.shape[1]
    tm, tn = _tile(m, tm), _tile(n, tn)
    if dims == "tn":
        tk = _tile(k, tk)
        grid = (m // tm, n // tn, k // tk)
        a_spec = pl.BlockSpec((tk, tm), lambda i, j, kk: (kk, i))
        b_spec = pl.BlockSpec((tk, tn), lambda i, j, kk: (kk, j))
        o_spec = pl.BlockSpec((tm, tn), lambda i, j, kk: (i, j))
        return _mm(a, b, name=name, grid=grid, a_spec=a_spec, b_spec=b_spec, o_spec=o_spec, o_shape=(m, n),
                   dims=dims, k_axis=2, res=res, out_dtype=out_dtype)
    grid = (n // tn, m // tm)
    a_spec = pl.BlockSpec((tm, k), lambda j, i: (i, 0))
    if dims == "nn":
        b_spec = pl.BlockSpec((k, tn), lambda j, i: (0, j))
    else:
        b_spec = pl.BlockSpec((tn, k), lambda j, i: (j, 0))
    o_spec = pl.BlockSpec((tm, tn), lambda j, i: (i, j))
    return _mm(a, b, name=name, grid=grid, a_spec=a_spec, b_spec=b_spec, o_spec=o_spec, o_shape=(m, n), dims=dims,
               res=res, out_dtype=out_dtype, exact=exact, ride=ride)


def _rowwise(fn, tiled, bcast, outs, accs=(), *, name, tm=256, flipped=(), ride=None):
    m = tiled[0].shape[0]
    tm = _tile(m, tm)
    n_t, n_b, n_o, n_f = len(tiled), len(bcast), len(outs), len(flipped)

    def body(*refs):
        ins = [r[...] for r in refs[: n_t + n_b]]
        o_refs = refs[n_t + n_b: n_t + n_b + n_o]
        f_refs = refs[n_t + n_b + n_o: n_t + n_b + n_o + n_f]
        a_refs = refs[n_t + n_b + n_o + n_f:]
        res = fn(*ins)
        if not isinstance(res, (tuple, list)):
            res = (res,)
        for r, v in zip(o_refs, res[:n_o]):
            r[...] = v.astype(r.dtype)
        for r, v in zip(f_refs, res[n_o:n_o + n_f]):
            r[...] = v.astype(F32).T.astype(r.dtype)
        first = pl.program_id(0) == 0
        for r, v in zip(a_refs, res[n_o + n_f:]):
            s = jnp.sum(v, axis=0, keepdims=True)

            @pl.when(first)
            def _():
                r[...] = s

            @pl.when(jnp.logical_not(first))
            def _():
                r[...] += s

    in_specs = [pl.BlockSpec((tm, t.shape[1]), lambda i: (i, 0)) for t in tiled]
    in_specs += [pl.BlockSpec(b.shape, lambda i, nd=b.ndim: (0,) * nd) for b in bcast]
    out_specs = [pl.BlockSpec((tm, n), lambda i: (i, 0)) for n, _ in outs]
    out_specs += [pl.BlockSpec((n, tm), lambda i, per=m // tm // g: (i // per, i % per)) for n, _, g in flipped]
    out_specs += [pl.BlockSpec((1, n), lambda i: (0, 0)) for n in accs]
    out_shape = [jax.ShapeDtypeStruct((m, n), dt) for n, dt in outs]
    out_shape += [jax.ShapeDtypeStruct((g * n, m // g), dt) for n, dt, g in flipped]
    out_shape += [jax.ShapeDtypeStruct((1, n), F32) for n in accs]
    first, last = _grid_ends((m // tm,))
    res, rode = _call(body, name=name, grid=(m // tm,), in_specs=in_specs, out_specs=out_specs, out_shape=out_shape,
                      scratch_shapes=[], args=list(tiled) + list(bcast), ride=ride, first=first, last=last)
    return res if ride is None else (res, rode)


def _rstd(x):
    return lax.rsqrt(jnp.mean(x * x, axis=-1, keepdims=True) + RMS_EPS)


def _rms_bwd(dh, x, g):
    xh = x * _rstd(x)
    dxh = dh * g
    dx = _rstd(x) * (dxh - xh * jnp.mean(dxh * xh, axis=-1, keepdims=True))
    return dx, dh * xh


def _sigmoid(x):
    return 1.0 / (1.0 + jnp.exp(-x))


_GELU_K = math.sqrt(2.0 / math.pi)
_GELU_C = 0.044715


def _gelu(x):
    return 0.5 * x * (1.0 + jnp.tanh(_GELU_K * (x + _GELU_C * x * x * x)))


def _gelu_grad(x):
    th = jnp.tanh(_GELU_K * (x + _GELU_C * x * x * x))
    return 0.5 * (1.0 + th) + 0.5 * x * (1.0 - th * th) * _GELU_K * (1.0 + 3.0 * _GELU_C * x * x)


class _Exchange:
    def __init__(self, arrays, outs, n_sems, sends, recvs=None, local=None, aliases=None):
        self.arrays, self.outs, self.n_sems = list(arrays), list(outs), n_sems
        self.sends, self.local, self.aliases = sends, local, aliases or {}
        self.recvs = recvs or (lambda i, o: [(k, dst) for k, _, dst, _ in sends(i, o)])

    def __add__(self, other):
        na, no, ns = len(self.arrays), len(self.outs), self.n_sems
        mine = lambda f: (lambda i, o: f(i[:na], o[:no]))
        shift = lambda f, at: (lambda i, o: [(k + ns,) + tuple(rest) for k, *rest in f(i[na:], o[no:])]) if at else None
        both = lambda f, g: (lambda i, o: f(i, o) + g(i, o))
        local = None
        if self.local or other.local:
            la = mine(self.local) if self.local else (lambda i, o: [])
            lb = (lambda i, o: other.local(i[na:], o[no:])) if other.local else (lambda i, o: [])
            local = both(la, lb)
        aliases = dict(self.aliases)
        aliases.update({na + i: no + o for i, o in other.aliases.items()})
        return _Exchange(self.arrays + other.arrays, self.outs + other.outs, ns + other.n_sems,
                         both(mine(self.sends), shift(other.sends, True)),
                         both(mine(self.recvs), shift(other.recvs, True)), local, aliases)

    def descriptors(self, in_refs, out_refs, send_sems, recv_sems, local_sems):
        me = _mesh_place()
        remote = lambda k, src, dst, to: pltpu.make_async_remote_copy(
            src_ref=src, dst_ref=dst, send_sem=send_sems.at[k], recv_sem=recv_sems.at[k], device_id=to,
            device_id_type=MESH)
        out = [remote(*s) for s in self.sends(in_refs, out_refs)]
        arrive = [remote(k, dst, dst, me) for k, dst in self.recvs(in_refs, out_refs)]
        own = [pltpu.make_async_copy(src, dst, local_sems.at[i])
               for i, (src, dst) in enumerate(self.local(in_refs, out_refs) if self.local else [])]
        return out, arrive, own

    def start(self, *refs):
        out, _, own = self.descriptors(*refs)
        for cp in own + out:
            cp.start()

    def finish(self, *refs):
        out, arrive, own = self.descriptors(*refs)
        for cp in arrive:
            cp.wait_recv()
        for cp in out:
            cp.wait_send()
        for cp in own:
            cp.wait()


def _call(body, *, name, grid, in_specs, out_specs, out_shape, scratch_shapes, args, ride=None, first=None, last=None):
    if ride is None:
        res = pl.pallas_call(body, name=name, grid=grid, in_specs=in_specs, out_specs=out_specs, out_shape=out_shape,
                             scratch_shapes=scratch_shapes, compiler_params=_params())(*args)
        return list(res), []
    n_in, n_out, n_scr = len(in_specs), len(out_specs), len(scratch_shapes)
    r_in, r_out = len(ride.arrays), len(ride.outs)

    def wrapped(*refs):
        ins, refs = refs[:n_in], refs[n_in:]
        x_in, refs = refs[:r_in], refs[r_in:]
        outs, refs = refs[:n_out], refs[n_out:]
        x_out, refs = refs[:r_out], refs[r_out:]
        scr, sems = refs[:n_scr], refs[n_scr:]

        @pl.when(first())
        def _():
            ride.start(x_in, x_out, *sems)

        body(*ins, *outs, *scr)

        @pl.when(last())
        def _():
            ride.finish(x_in, x_out, *sems)

    n_local = max(1, len(ride.arrays))
    res = pl.pallas_call(
        wrapped, name=name, grid=grid, in_specs=list(in_specs) + [_ANY] * r_in,
        out_specs=list(out_specs) + [_ANY] * r_out, out_shape=list(out_shape) + ride.outs,
        scratch_shapes=list(scratch_shapes) + [pltpu.SemaphoreType.DMA((ride.n_sems,)),
                                               pltpu.SemaphoreType.DMA((ride.n_sems,)),
                                               pltpu.SemaphoreType.DMA((n_local,))],
        input_output_aliases={n_in + i: n_out + o for i, o in ride.aliases.items()},
        compiler_params=_params(has_side_effects=True),
    )(*args, *ride.arrays)
    return list(res[:n_out]), list(res[n_out:])


def _gather_first(shards):
    def sends(i, o):
        x, y, c = _mesh_place()
        peers = [(x, y, 1 - c)] + [(px, py, c) for px, py in _chips(x, y)[1:]]
        return [(4 * w + k, i[w], o[w].at[_slab(x, y, c)], to) for w in range(len(i)) for k, to in enumerate(peers)]

    def recvs(i, o):
        x, y, c = _mesh_place()
        peers = [(x, y, 1 - c)] + [(px, py, c) for px, py in _chips(x, y)[1:]]
        return [(4 * w + k, o[w].at[_slab(*peer)]) for w in range(len(i)) for k, peer in enumerate(peers)]

    def local(i, o):
        return [(i[w], o[w].at[_slab(*_mesh_place())]) for w in range(len(i))]

    outs = [jax.ShapeDtypeStruct((N_DEV,) + a.shape, a.dtype) for a in shards]
    return _Exchange(shards, outs, 4 * len(shards), sends, recvs, local)


def _gather_second(gathered):
    def sends(i, o):
        x, y, c = _mesh_place()
        return [(3 * w + j, o[w].at[_slab(px, py, c)], o[w].at[_slab(px, py, c)], (x, y, 1 - c))
                for w in range(len(o)) for j, (px, py) in enumerate(_chips(x, y)[1:])]

    def recvs(i, o):
        x, y, c = _mesh_place()
        return [(3 * w + j, o[w].at[_slab(px, py, 1 - c)])
                for w in range(len(o)) for j, (px, py) in enumerate(_chips(x, y)[1:])]

    outs = [jax.ShapeDtypeStruct(a.shape, a.dtype) for a in gathered]
    return _Exchange(gathered, outs, 3 * len(gathered), sends, recvs, aliases={w: w for w in range(len(gathered))})


def _reduce_sibling(slabs):
    def sends(i, o):
        x, y, c = _mesh_place()
        return [(4 * w + k, i[w].at[_slab(px, py, 1 - c)], o[w].at[k], (x, y, 1 - c))
                for w in range(len(i)) for k, (px, py) in enumerate(_chips(x, y))]

    outs = [jax.ShapeDtypeStruct((4,) + a.shape[1:], a.dtype) for a in slabs]
    return _Exchange(slabs, outs, 4 * len(slabs), sends)


def _reduce_chips(partials):
    def sends(i, o):
        x, y, c = _mesh_place()
        return [(3 * w + k, i[w].at[k], o[w].at[k], (px, py, c))
                for w in range(len(i)) for k, (px, py) in enumerate(_chips(x, y)[1:])]

    outs = [jax.ShapeDtypeStruct(a.shape, a.dtype) for a in partials]
    return _Exchange(partials, outs, 3 * len(partials), sends)


def _head_masks():
    lane_head = lax.broadcasted_iota(jnp.int32, (1, GROUP_LANES), 1) // HEAD_DIM
    return [(lane_head == h).astype(F32) for h in range(HEADS_PER_GROUP)]


def _head_block_diag():
    r = lax.broadcasted_iota(jnp.int32, (GROUP_LANES, GROUP_LANES), 0) // HEAD_DIM
    c = lax.broadcasted_iota(jnp.int32, (GROUP_LANES, GROUP_LANES), 1) // HEAD_DIM
    return (r == c).astype(BF16)


def _head_mean(x, bd):
    hi = x.astype(BF16)
    lo = (x - hi.astype(F32)).astype(BF16)
    return (_dot(hi, bd) + _dot(lo, bd)) * (1.0 / HEAD_DIM)


def _stack_heads(x, masks):
    return jnp.concatenate([x * m for m in masks], axis=0)


def _unstack_heads(xs, masks):
    out = xs[0:GRID_W] * masks[0]
    for h in range(1, HEADS_PER_GROUP):
        out = out + xs[h * GRID_W:(h + 1) * GRID_W] * masks[h]
    return out


def _row_start(r, rows):
    return jnp.clip(r - WIN_H // 2, 0, rows - WIN_H)


ROWS_PER_STEP = 2


def _attn_common_specs(seq, n_hg, rows):
    win_keys = WIN_H * GRID_W
    q_spec = pl.BlockSpec((ROWS_PER_STEP * GRID_W, GROUP_LANES), lambda g, r: (r, g))
    k_spec = pl.BlockSpec((seq, GROUP_LANES), lambda g, r: (0, n_hg + g))
    v_spec = pl.BlockSpec((seq, GROUP_LANES), lambda g, r: (0, 2 * n_hg + g))
    gain_spec = pl.BlockSpec((1, GROUP_LANES), lambda g, r: (0, 0))

    def variant(r):
        return _row_start(r, rows) - r + (WIN_H - 1)

    bias_specs = [pl.BlockSpec((None, None, HEADS_PER_GROUP, GRID_W, win_keys),
                               lambda g, r, h=h: (g, variant(ROWS_PER_STEP * r + h), 0, 0, 0))
                  for h in range(ROWS_PER_STEP)]
    return q_spec, k_spec, v_spec, gain_spec, bias_specs, variant


def _attn_prepare_kv(k_ref, v_ref, kg, kn_scr, vb_scr, bd, seq):
    chunk = _tile(seq, 512)

    def step(c, carry):
        rows = pl.ds(pl.multiple_of(c * chunk, chunk), chunk)
        k = k_ref[rows, :]
        kn_scr[rows, :] = (k * lax.rsqrt(_head_mean(k * k, bd) + RMS_EPS) * kg).astype(BF16)
        vb_scr[rows, :] = v_ref[rows, :].astype(BF16)
        return carry

    lax.fori_loop(0, seq // chunk, step, 0)


def _attn_probs(qn, kw, bias, masks):
    qs = _stack_heads(qn, masks).astype(BF16)
    s = _dot(qs, kw, _NT) * (1.0 / math.sqrt(HEAD_DIM)) + bias
    m = jnp.max(s, axis=-1, keepdims=True)
    p = jnp.exp(s - m)
    return qs, p * (1.0 / jnp.sum(p, axis=-1, keepdims=True))


def _grid_ends(grid):
    first = lambda: functools.reduce(jnp.logical_and, [pl.program_id(a) == 0 for a in range(len(grid))])
    last = lambda: functools.reduce(jnp.logical_and, [pl.program_id(a) == n - 1 for a, n in enumerate(grid)])
    return first, last


def _attn_fwd(z, qg4, kg4, btab, ride=None):
    seq = z.shape[0]
    a_width = btab.shape[0] * GROUP_LANES
    n_hg, rows, win_keys = btab.shape[0], seq // GRID_W, WIN_H * GRID_W
    q_spec, k_spec, v_spec, gain_spec, bias_specs, _ = _attn_common_specs(seq, n_hg, rows)
    grid = (n_hg, rows // ROWS_PER_STEP)

    def body(q_ref, k_ref, v_ref, qg_ref, kg_ref, *rest):
        b_refs, (o_ref, kn_scr, vb_scr) = rest[:ROWS_PER_STEP], rest[ROWS_PER_STEP:]
        bd, masks = _head_block_diag(), _head_masks()

        @pl.when(pl.program_id(1) == 0)
        def _():
            _attn_prepare_kv(k_ref, v_ref, kg_ref[...], kn_scr, vb_scr, bd, seq)

        for h in range(ROWS_PER_STEP):
            r = ROWS_PER_STEP * pl.program_id(1) + h
            mine = slice(h * GRID_W, (h + 1) * GRID_W)
            win = pl.ds(pl.multiple_of(_row_start(r, rows) * GRID_W, GRID_W), win_keys)
            q = q_ref[mine, :]
            qn = q * lax.rsqrt(_head_mean(q * q, bd) + RMS_EPS) * qg_ref[...]
            bias = b_refs[h][...].reshape(HEADS_PER_GROUP * GRID_W, win_keys)
            _, p = _attn_probs(qn, kn_scr[win, :], bias, masks)
            o_ref[mine, :] = _unstack_heads(_dot(p.astype(BF16), vb_scr[win, :]), masks)

    first, last = _grid_ends(grid)
    (ya,), rode = _call(
        body, name="attn_fwd", grid=grid,
        in_specs=[q_spec, k_spec, v_spec, gain_spec, gain_spec] + bias_specs,
        out_specs=[pl.BlockSpec((ROWS_PER_STEP * GRID_W, GROUP_LANES), lambda g, r: (r, g))],
        out_shape=[jax.ShapeDtypeStruct((seq, a_width), F32)],
        scratch_shapes=[pltpu.VMEM((seq, GROUP_LANES), BF16), pltpu.VMEM((seq, GROUP_LANES), BF16)],
        args=(z, z, z, qg4, kg4) + (btab,) * ROWS_PER_STEP, ride=ride, first=first, last=last)
    return ya, rode


def _attn_bwd(z, d_out, qg4, kg4, btab, ride=None):
    seq = z.shape[0]
    n_hg, rows, win_keys = btab.shape[0], seq // GRID_W, WIN_H * GRID_W
    a_width = n_hg * GROUP_LANES
    q_spec, k_spec, v_spec, gain_spec, bias_specs, variant = _attn_common_specs(seq, n_hg, rows)
    scale = 1.0 / math.sqrt(HEAD_DIM)
    grid = (n_hg, rows // ROWS_PER_STEP)

    def body(q_ref, k_ref, v_ref, do_ref, qg_ref, kg_ref, *rest):
        b_refs, rest = rest[:ROWS_PER_STEP], rest[ROWS_PER_STEP:]
        dq_ref, dk_out, dv_out, db_ref, dqg_ref, dkg_ref, kn_scr, vb_scr, dk_ref, dv_ref = rest
        bd, masks = _head_block_diag(), _head_masks()

        @pl.when(pl.program_id(1) == 0)
        def _():
            _attn_prepare_kv(k_ref, v_ref, kg_ref[...], kn_scr, vb_scr, bd, seq)
            dk_ref[...] = jnp.zeros_like(dk_ref)
            dv_ref[...] = jnp.zeros_like(dv_ref)
            db_ref[...] = jnp.zeros_like(db_ref)
            dqg_ref[...] = jnp.zeros_like(dqg_ref)

        qg = qg_ref[...]
        for h in range(ROWS_PER_STEP):
            r = ROWS_PER_STEP * pl.program_id(1) + h
            mine = slice(h * GRID_W, (h + 1) * GRID_W)
            win = pl.ds(pl.multiple_of(_row_start(r, rows) * GRID_W, GRID_W), win_keys)
            q = q_ref[mine, :]
            rq = lax.rsqrt(_head_mean(q * q, bd) + RMS_EPS)
            qh = q * rq
            kw, vw = kn_scr[win, :], vb_scr[win, :]
            bias = b_refs[h][...].reshape(HEADS_PER_GROUP * GRID_W, win_keys)
            qs, p = _attn_probs(qh * qg, kw, bias, masks)
            dos = _stack_heads(do_ref[mine, :], masks).astype(BF16)
            dp = _dot(dos, vw, _NT)
            ds = p * (dp - jnp.sum(p * dp, axis=-1, keepdims=True))
            db_ref[variant(r)] += ds.reshape(HEADS_PER_GROUP, GRID_W, win_keys)
            dsb = ds.astype(BF16)
            dqn = _unstack_heads(_dot(dsb, kw), masks) * scale
            dk_ref[win, :] += _dot(dsb, qs, _TN) * scale
            dv_ref[win, :] += _dot(p.astype(BF16), dos, _TN)
            dqg_ref[...] += jnp.sum(dqn * qh, axis=0, keepdims=True)
            dqh = dqn * qg
            dq_ref[mine, :] = (rq * (dqh - qh * _head_mean(dqh * qh, bd))).astype(BF16)

        @pl.when(pl.program_id(1) == grid[1] - 1)
        def _():
            chunk = _tile(seq, 512)
            kg = kg_ref[...]

            def step(c, dkg):
                rws = pl.ds(pl.multiple_of(c * chunk, chunk), chunk)
                k = k_ref[rws, :]
                rk = lax.rsqrt(_head_mean(k * k, bd) + RMS_EPS)
                kh = k * rk
                dkn = dk_ref[rws, :]
                dkh = dkn * kg
                dk_out[rws, :] = (rk * (dkh - kh * _head_mean(dkh * kh, bd))).astype(BF16)
                dv_out[rws, :] = dv_ref[rws, :].astype(BF16)
                return dkg + jnp.sum(dkn * kh, axis=0, keepdims=True)

            dkg_ref[...] = lax.fori_loop(0, seq // chunk, step, jnp.zeros((1, GROUP_LANES), F32))

    col_spec = pl.BlockSpec((seq, GROUP_LANES), lambda g, r: (0, g))
    gsum_spec = pl.BlockSpec((None, 1, GROUP_LANES), lambda g, r: (g, 0, 0))
    first, last = _grid_ends(grid)
    rows_spec = pl.BlockSpec((ROWS_PER_STEP * GRID_W, GROUP_LANES), lambda g, r: (r, g))
    return _call(
        body, name="attn_bwd", grid=grid,
        in_specs=[q_spec, k_spec, v_spec, rows_spec, gain_spec, gain_spec] + bias_specs,
        out_specs=[rows_spec, col_spec, col_spec,
                   pl.BlockSpec((None, WIN_H, HEADS_PER_GROUP, GRID_W, win_keys), lambda g, r: (g, 0, 0, 0, 0)),
                   gsum_spec, gsum_spec],
        out_shape=[jax.ShapeDtypeStruct((seq, a_width), BF16)] * 3
        + [jax.ShapeDtypeStruct(btab.shape, F32)]
        + [jax.ShapeDtypeStruct((n_hg, 1, GROUP_LANES), F32)] * 2,
        scratch_shapes=[pltpu.VMEM((seq, GROUP_LANES), BF16), pltpu.VMEM((seq, GROUP_LANES), BF16),
                        pltpu.VMEM((seq, GROUP_LANES), F32), pltpu.VMEM((seq, GROUP_LANES), F32)],
        args=(z, z, z, d_out, qg4, kg4) + (btab,) * ROWS_PER_STEP, ride=ride, first=first, last=last)


DC_SLOTS = 2 * WIN_W


def _bias_spread():
    c = jnp.arange(GRID_W)[:, None, None]
    kc = jnp.arange(GRID_W)[None, None, :]
    col_start = jnp.clip(c - WIN_W // 2, 0, GRID_W - WIN_W)
    col_in = (kc >= col_start) & (kc < col_start + WIN_W)
    offset = kc - c + (WIN_W - 1)
    row_i = jnp.arange(WIN_H)[:, None, None, None, None]
    row_d = jnp.arange(DC_SLOTS)[None, :, None, None, None]
    same_row = row_i == jnp.arange(WIN_H)[None, None, None, :, None]
    hit = same_row & (row_d == offset[None, None]) & col_in[None, None]
    mask_slot = (row_i == 0) & (row_d == DC_SLOTS - 1) & jnp.logical_not(col_in)[None, None]
    mask_slot = jnp.broadcast_to(mask_slot, hit.shape)
    m = jnp.where(hit, 1.0, jnp.where(mask_slot, NEG_INF, 0.0)).astype(F32)
    return m.reshape(WIN_H * DC_SLOTS, GRID_W * WIN_H * GRID_W)


def _bias_table(rpb, spread):
    n_h = rpb.shape[0]
    n_hg = n_h // HEADS_PER_GROUP
    rows = jnp.stack([rpb[:, v:v + WIN_H] for v in range(WIN_H)], axis=1)
    rows = jnp.pad(rows, ((0, 0), (0, 0), (0, 0), (0, DC_SLOTS - rows.shape[-1])))
    rows = rows.at[:, :, 0, DC_SLOTS - 1].set(1.0)
    rows = rows.reshape(n_hg, HEADS_PER_GROUP, WIN_H, WIN_H * DC_SLOTS).transpose(0, 2, 1, 3)
    n_rows, win_keys, depth = n_h * WIN_H, WIN_H * GRID_W, WIN_H * DC_SLOTS

    def body(r_ref, m_ref, o_ref):
        for cc in range(SUBLANES):
            o_ref[cc] = lax.dot_general(r_ref[...], m_ref[:, cc * win_keys:(cc + 1) * win_keys], _NN,
                                        precision=lax.Precision.HIGHEST, preferred_element_type=F32)

    tab = pl.pallas_call(
        body, name="rpb_spread", grid=(GRID_W // SUBLANES,),
        in_specs=[pl.BlockSpec((n_rows, depth), lambda c: (0, 0)),
                  pl.BlockSpec((depth, SUBLANES * win_keys), lambda c: (0, c))],
        out_specs=pl.BlockSpec((SUBLANES, n_rows, win_keys), lambda c: (c, 0, 0)),
        out_shape=jax.ShapeDtypeStruct((GRID_W, n_rows, win_keys), F32), compiler_params=_params(),
    )(rows.reshape(n_rows, depth), spread)
    return tab.transpose(1, 0, 2).reshape(n_hg, WIN_H, HEADS_PER_GROUP, GRID_W, win_keys)


def _bias_grad(dtab, spread, n_h):
    n_hg = n_h // HEADS_PER_GROUP
    n_rows, win_keys, depth = n_h * WIN_H, WIN_H * GRID_W, WIN_H * DC_SLOTS

    def body(d_ref, m_ref, o_ref):
        total = None
        for cc in range(SUBLANES):
            t = _dot(d_ref[cc].astype(BF16), m_ref[:, cc * win_keys:(cc + 1) * win_keys].astype(BF16), _NT)
            total = t if total is None else total + t

        @pl.when(pl.program_id(0) == 0)
        def _():
            o_ref[...] = total

        @pl.when(pl.program_id(0) > 0)
        def _():
            o_ref[...] += total

    d_rows = pl.pallas_call(
        body, name="rpb_diag_sum", grid=(GRID_W // SUBLANES,),
        in_specs=[pl.BlockSpec((SUBLANES, n_rows, win_keys), lambda c: (c, 0, 0)),
                  pl.BlockSpec((depth, SUBLANES * win_keys), lambda c: (0, c))],
        out_specs=pl.BlockSpec((n_rows, depth), lambda c: (0, 0)),
        out_shape=jax.ShapeDtypeStruct((n_rows, depth), F32), compiler_params=_params(),
    )(dtab.reshape(n_rows, GRID_W, win_keys).transpose(1, 0, 2), spread)
    d_rows = d_rows.reshape(n_hg, WIN_H, HEADS_PER_GROUP, WIN_H, DC_SLOTS).transpose(0, 2, 1, 3, 4)
    d_rows = d_rows.reshape(n_h, WIN_H, WIN_H, DC_SLOTS)[..., : 2 * WIN_W - 1]
    out = jnp.zeros((n_h, 2 * WIN_H - 1, 2 * WIN_W - 1), F32)
    for v in range(WIN_H):
        out = out.at[:, v:v + WIN_H].add(d_rows[:, v])
    return out


def _cmul(ar, ai, br, bi):
    return ar * br - ai * bi, ar * bi + ai * br


def _s5_discretize(a_re, a_im, dt, b_re, b_im, ride=None):
    c = b_re.shape[1]

    def fn(are, aim, dt_, bre, bim):
        lr, li = jnp.minimum(are, A_RE_MAX), aim
        mag = jnp.exp(lr * dt_)
        l1r, l1i = mag * jnp.cos(li * dt_), mag * jnp.sin(li * dt_)
        den = lr * lr + li * li
        nr, ni = l1r - 1.0, l1i
        cr, ci = (nr * lr + ni * li) / den, (ni * lr - nr * li) / den
        bbr, bbi = _cmul(cr, ci, bre, bim)
        shape = (are.shape[0], SUBLANES)
        lane = lax.broadcasted_iota(jnp.int32, shape, 1)
        pr, pi = l1r, l1i
        acc_r, acc_i = jnp.zeros(shape, F32), jnp.zeros(shape, F32)
        for k in range(SUBLANES):
            acc_r = jnp.where(lane == k, pr, acc_r)
            acc_i = jnp.where(lane == k, pi, acc_i)
            pr, pi = _cmul(pr, pi, l1r, l1i)
        return acc_r, acc_i, cr, ci, bbr, bbi

    return _rowwise(fn, [a_re, a_im, dt, b_re, b_im], [],
                    [(SUBLANES, F32), (SUBLANES, F32), (1, F32), (1, F32), (c, F32), (c, F32)],
                    name="s5_discretize", tm=1024, ride=ride)


def _s5_param_grads(a_re, a_im, dt, b_re, b_im, l1r, l1i, cr, ci, bbr, bbi, r_re, r_im, gb_re, gb_im):
    c = b_re.shape[1]

    def fn(are, aim, dt_, bre, bim, l1r_, l1i_, cr_, ci_, bbr_, bbi_, rr, ri, gbr, gbi):
        lr, li = jnp.minimum(are, A_RE_MAX), aim
        den = lr * lr + li * li
        dbr, dbi = _cmul(cr_, -ci_, gbr, gbi)
        gcr, gci = _cmul(bre, -bim, gbr, gbi)
        gcr, gci = jnp.sum(gcr, axis=1, keepdims=True), jnp.sum(gci, axis=1, keepdims=True)
        qr, qi = _cmul(bbr_, -bbi_, gbr, gbi)
        qr = rr - jnp.sum(qr, axis=1, keepdims=True)
        qi = ri - jnp.sum(qi, axis=1, keepdims=True)
        tr, ti = _cmul(gcr, gci, lr / den, li / den)
        ur, ui = _cmul(l1r_, -l1i_, tr, ti)
        gwr, gwi = qr + ur, qi + ui
        vr, vi = _cmul(cr_, -ci_, lr / den, li / den)
        vr, vi = _cmul(gcr, gci, vr, vi)
        glr, gli = dt_ * gwr - vr, dt_ * gwi - vi
        return jnp.where(are < A_RE_MAX, glr, 0.0), gli, (gwr * lr + gwi * li) * dt_, dbr, dbi

    return _rowwise(fn, [a_re, a_im, dt, b_re, b_im, l1r, l1i, cr, ci, bbr, bbi, r_re, r_im, gb_re, gb_im], [],
                    [(1, F32), (1, F32), (1, F32), (c, F32), (c, F32)], name="s5_param_grads", tm=1024)


def _s5_scan(v, win_re, win_im, tabs, wo_re, wo_im, *, reverse, name, t_chunk=256, ride=None):
    seq, width = v.shape
    n_tiles, n_state = width // U_TILE, width * (SSM_P // SSM_C)
    t_chunk = _tile(seq, t_chunk)
    n_chunks, n_blk = seq // t_chunk, t_chunk // SUBLANES
    last_row = 0 if reverse else SUBLANES - 1

    def chunk_of(j):
        return (n_chunks - 1 - j) if reverse else j

    def body(v_ref, wir_ref, wii_ref, tab_ref, wor_ref, woi_ref, sr_ref, si_ref, y_ref, carry, wr, wi):
        @pl.when(pl.program_id(0) == 0)
        def _():
            carry[...] = jnp.zeros_like(carry)

        for jt in range(n_tiles):
            ls = slice(jt * ST_TILE, (jt + 1) * ST_TILE)
            us = slice(jt * U_TILE, (jt + 1) * U_TILE)
            vj = v_ref[:, us].astype(BF16)
            consts = [tab_ref[k, :, ls] for k in range(8)]
            xr = _dot(vj, wir_ref[jt]).reshape(n_blk, SUBLANES, ST_TILE)
            xi = _dot(vj, wii_ref[jt]).reshape(n_blk, SUBLANES, ST_TILE)
            for s, k in enumerate((1, 2, 4)):
                sh = (SUBLANES - k) if reverse else k
                tr, ti = pltpu.roll(xr, sh, 1), pltpu.roll(xi, sh, 1)
                lr, li = consts[2 * s][None], consts[2 * s + 1][None]
                xr, xi = xr + lr * tr - li * ti, xi + lr * ti + li * tr
            wr[...] = xr.reshape(t_chunk, ST_TILE)
            wi[...] = xi.reshape(t_chunk, ST_TILE)

            def blk(b, c, consts=consts):
                cr, ci = c
                bb = (n_blk - 1 - b) if reverse else b
                rows = pl.ds(pl.multiple_of(bb * SUBLANES, SUBLANES), SUBLANES)
                lr, li = consts[6], consts[7]
                xr = wr[rows, :] + lr * cr - li * ci
                xi = wi[rows, :] + lr * ci + li * cr
                wr[rows, :], wi[rows, :] = xr, xi
                shape = (SUBLANES, ST_TILE)
                return (jnp.broadcast_to(xr[last_row:last_row + 1], shape),
                        jnp.broadcast_to(xi[last_row:last_row + 1], shape))

            cr, ci = lax.fori_loop(0, n_blk, blk, (carry[0, :, ls], carry[1, :, ls]), unroll=2)
            carry[0, :, ls], carry[1, :, ls] = cr, ci
            xr_b, xi_b = wr[...].astype(BF16), wi[...].astype(BF16)
            sr_ref[:, ls], si_ref[:, ls] = xr_b, xi_b
            y_ref[:, us] = _dot(xr_b, wor_ref[jt]) + _dot(xi_b, woi_ref[jt])

    whole = lambda a: pl.BlockSpec(a.shape, lambda j, nd=a.ndim: (0,) * nd)
    st_spec = pl.BlockSpec((t_chunk, n_state), lambda j: (chunk_of(j), 0))
    v_spec = pl.BlockSpec((t_chunk, width), lambda j: (chunk_of(j), 0))
    first, last = _grid_ends((n_chunks,))
    return _call(
        body, name=name, grid=(n_chunks,),
        in_specs=[v_spec, whole(win_re), whole(win_im), whole(tabs), whole(wo_re), whole(wo_im)],
        out_specs=[st_spec, st_spec, v_spec],
        out_shape=[jax.ShapeDtypeStruct((seq, n_state), BF16)] * 2 + [jax.ShapeDtypeStruct((seq, width), F32)],
        scratch_shapes=[pltpu.VMEM((2, SUBLANES, n_state), F32), pltpu.VMEM((t_chunk, ST_TILE), F32),
                        pltpu.VMEM((t_chunk, ST_TILE), F32)],
        args=(v, win_re, win_im, tabs, wo_re, wo_im), ride=ride, first=first, last=last)


def _s5_reduce(x_re, x_im, a_re, a_im, u, dy, *, name, t_chunk=512, ride=None):
    seq, n_state = x_re.shape
    width = u.shape[1]
    n_tiles = width // U_TILE
    t_chunk = _tile(seq, t_chunk)

    def body(xr_ref, xi_ref, ar_ref, ai_ref, u_ref, dy_ref, rr_ref, ri_ref, gbr_ref, gbi_ref, gcr_ref, gci_ref):
        xrb, xib, arb, aib = xr_ref[...], xi_ref[...], ar_ref[...], ai_ref[...]
        xr, xi, ar, ai = xrb.astype(F32), xib.astype(F32), arb.astype(F32), aib.astype(F32)
        ub, dyb = u_ref[...].astype(BF16), dy_ref[...].astype(BF16)
        parts = (jnp.sum(ar * xr + ai * xi, axis=0, keepdims=True), jnp.sum(ai * xr - ar * xi, axis=0, keepdims=True),
                 _dot(arb, ub, _TN), _dot(aib, ub, _TN), _dot(xrb, dyb, _TN), _dot(xib, dyb, _TN))
        first = pl.program_id(1) == 0
        for ref, val in zip((rr_ref, ri_ref, gbr_ref, gbi_ref, gcr_ref, gci_ref), parts):
            @pl.when(first)
            def _():
                ref[...] = val

            @pl.when(jnp.logical_not(first))
            def _():
                ref[...] += val

    st_spec = pl.BlockSpec((t_chunk, ST_TILE), lambda j, t: (t, j))
    u_spec = pl.BlockSpec((t_chunk, U_TILE), lambda j, t: (t, j))
    r_spec = pl.BlockSpec((1, ST_TILE), lambda j, t: (0, j))
    g_spec = pl.BlockSpec((None, ST_TILE, U_TILE), lambda j, t: (j, 0, 0))
    first, last = _grid_ends((n_tiles, seq // t_chunk))
    return _call(
        body, name=name, grid=(n_tiles, seq // t_chunk),
        in_specs=[st_spec] * 4 + [u_spec] * 2,
        out_specs=[r_spec, r_spec] + [g_spec] * 4,
        out_shape=[jax.ShapeDtypeStruct((1, n_state), F32)] * 2
        + [jax.ShapeDtypeStruct((n_tiles, ST_TILE, U_TILE), F32)] * 4,
        scratch_shapes=[], args=(x_re, x_im, a_re, a_im, u, dy), ride=ride, first=first, last=last)


def _block_diag_in(ms):
    m = jnp.stack(ms)
    n, g, c, p = m.shape
    m5 = m.reshape(n, g // GROUPS_PER_TILE, GROUPS_PER_TILE, c, p)
    eye = jnp.eye(GROUPS_PER_TILE, dtype=m.dtype)
    out = m5[:, :, :, :, None, :] * eye[None, None, :, None, :, None]
    return out.astype(BF16).reshape(n, g // GROUPS_PER_TILE, GROUPS_PER_TILE * c, GROUPS_PER_TILE * p)


def _block_diag_take(m, c, p):
    t = m.shape[0]
    m5 = m.reshape(t, GROUPS_PER_TILE, p, GROUPS_PER_TILE, c)
    idx = jnp.arange(GROUPS_PER_TILE)
    return m5[:, idx, :, idx, :].transpose(1, 0, 2, 3).reshape(t * GROUPS_PER_TILE, p, c)


def _scan_tables(pw_re, pw_im, reverse):
    row = jnp.arange(SUBLANES)[:, None]
    tabs = []
    for k in (1, 2, 4):
        keep = (row <= SUBLANES - 1 - k) if reverse else (row >= k)
        tabs += [jnp.where(keep, pw_re[k - 1][None, :], 0.0), jnp.where(keep, pw_im[k - 1][None, :], 0.0)]
    order = jnp.arange(SUBLANES)[::-1] if reverse else jnp.arange(SUBLANES)
    tabs += [pw_re[order], pw_im[order]]
    return jnp.stack(tabs)


def _partial_sums(slabs, from_sibling, names):
    x, y, c = _mesh_place()
    theirs = jnp.stack([_slab(px, py, c) for px, py in _chips(x, y)[1:]]).astype(jnp.int32)
    out = []
    for s, f, n in zip(slabs, from_sibling, names):
        rows, cols = s.shape[1:]
        tr = _tile(rows, 512)

        def body(idx_ref, a_ref, b_ref, o_ref):
            o_ref[...] = (a_ref[...] + b_ref[...]).astype(BF16)

        out.append(pl.pallas_call(
            body, name=f"reduce_add_{n}",
            grid_spec=pltpu.PrefetchScalarGridSpec(
                num_scalar_prefetch=1, grid=(3, rows // tr),
                in_specs=[pl.BlockSpec((None, tr, cols), lambda k, i, idx: (idx[k], i, 0)),
                          pl.BlockSpec((None, tr, cols), lambda k, i, idx: (k + 1, i, 0))],
                out_specs=pl.BlockSpec((None, tr, cols), lambda k, i, idx: (k, i, 0))),
            out_shape=jax.ShapeDtypeStruct((3, rows, cols), BF16), compiler_params=_params(),
        )(theirs, s, f))
    return out


def _local_step(x, target, p, shards):
    seq, d_model = x.shape
    a_width = p["g_out_attn"].shape[-1]
    s_width = p["g_out_ssm"].shape[-1]
    n_heads = a_width // HEAD_DIM
    n_hg = n_heads // HEADS_PER_GROUP
    n_groups = s_width // SSM_C
    n_sh, in_sh = N_DEV, shards["w_in"].shape[-1]
    f_sh = shards["w_ffn_gate"].shape[-1]
    w = {}
    slab3 = lambda g, n: g.reshape(N_DEV, -1, shards[n].shape[-1])
    t2, t1 = _tile(seq, 2048), _tile(seq, 1024)
    n2, n1 = seq // t2, seq // t1

    n_col = 2 * n_groups * SSM_P
    col = lambda a: a.reshape(n_col, 1)
    a_re_c, a_im_c = col(p["ssm_a_re"]), col(p["ssm_a_im"])
    dt_c = col(jnp.broadcast_to(jnp.exp(p["ssm_log_step"])[:, :, None], (2, n_groups, SSM_P)))
    b_re_c, b_im_c = p["ssm_b_re"].reshape(n_col, SSM_C), p["ssm_b_im"].reshape(n_col, SSM_C)
    (pw_re, pw_im, cf_re, cf_im, bb_re, bb_im), got = _s5_discretize(a_re_c, a_im_c, dt_c, b_re_c, b_im_c,
                                                                     ride=_gather_first([shards["w_in"]]))

    twice = lambda f: (lambda *a: (f(*a),) * 2)
    (h1, h1_t), (w["w_in"],) = _rowwise(twice(x_norm), [x], [p["g_mix"]], [(d_model, BF16)],
                                        flipped=[(d_model, BF16, 1)], name="rms_mix", ride=_gather_second(got))
    z = _mm(h1, w["w_in"], name="in_proj", grid=(n2, n_sh),
            a_spec=pl.BlockSpec((t2, d_model), lambda i, j: (i, 0)),
            b_spec=pl.BlockSpec((None, d_model, in_sh), lambda i, j: (j, 0, 0)),
            o_spec=pl.BlockSpec((t2, in_sh), lambda i, j: (i, j)), o_shape=(seq, n_sh * in_sh), dims="nn")
    qg4 = jnp.tile(p["q_gain"], (1, HEADS_PER_GROUP))
    kg4 = jnp.tile(p["k_gain"], (1, HEADS_PER_GROUP))
    spread = _bias_spread()
    btab = _bias_table(p["rpb"], spread)
    ya, got_a = _attn_fwd(z, qg4, kg4, btab, ride=_gather_first([shards["w_ffn_gate"], shards["w_ffn_up"]]))
    u = z[:, 3 * a_width:]
    n_state = n_groups * SSM_P
    pw_re = pw_re.reshape(2, n_state, SUBLANES).transpose(0, 2, 1)
    pw_im = pw_im.reshape(2, n_state, SUBLANES).transpose(0, 2, 1)
    bb_re4, bb_im4 = bb_re.reshape(2, n_groups, SSM_P, SSM_C), bb_im.reshape(2, n_groups, SSM_P, SSM_C)
    c_re, c_im = p["ssm_c_re"], p["ssm_c_im"]
    t21 = lambda a: a.transpose(0, 2, 1)
    maps_in = _block_diag_in([m for d in range(2) for m in (t21(bb_re4[d]), t21(bb_im4[d]), c_re[d], -c_im[d])])
    maps_out = _block_diag_in([m for d in range(2) for m in (t21(c_re[d]), -t21(c_im[d]), bb_re4[d], bb_im4[d])])
    fwd, bwd_in = [], []
    got_b = None
    for d in range(2):
        rev = d == 1
        tabs = _scan_tables(pw_re[d], pw_im[d], rev)
        if d == 0:
            ride = _gather_first([shards["w_glu"], shards["w_out"]])
        else:
            ride = _gather_second(got_a + got_b) + _gather_first([shards["w_ffn_down"]])
        (xs_re, xs_im, y_d), got = _s5_scan(u, maps_in[4 * d], maps_in[4 * d + 1], tabs, maps_out[4 * d],
                                            maps_out[4 * d + 1], reverse=rev, name=f"s5_fwd_{d}", ride=ride)
        if d == 0:
            got_b = got
        fwd.append((xs_re, xs_im, y_d))
        bwd_in.append((maps_in[4 * d + 2], maps_in[4 * d + 3], _scan_tables(pw_re[d], -pw_im[d], not rev),
                       maps_out[4 * d + 2], maps_out[4 * d + 3]))
    w["w_gate"], w["w_up"], w_glu_full, w_out_full, w_down_first = got
    w["w_glu"] = w_glu_full.reshape(-1, s_width)
    w["w_out"] = w_out_full.reshape(-1, d_model)

    ypre, yg, yg_t = _rowwise(s5_mid, [fwd[0][2], fwd[1][2], u], [p["ssm_d"]], [(s_width, F32), (s_width, F32)],
                              flipped=[(s_width, BF16, 1)], name="s5_skip_gelu")
    t_glu = _mm_plain(yg, w["w_glu"], "nn", name="glu_proj", tn=s_width)
    y_cat, y_cat_t = _rowwise(twice(mix_out_fwd), [ya, yg, t_glu], [p["b_glu"], p["g_out_attn"], p["g_out_ssm"]],
                              [(a_width + s_width, BF16)], flipped=[(a_width + s_width, BF16, 1)], name="mix_out")
    x1, (w["w_down"],) = _mm_plain(y_cat, w["w_out"], "nn", name="out_proj", res=x, tn=2048,
                                   ride=_gather_second([w_down_first]))

    h2, h2_t = _rowwise(twice(x_norm), [x1], [p["g_ffn"]], [(d_model, BF16)], flipped=[(d_model, BF16, 1)],
                        name="rms_ffn")
    ffn_up = functools.partial(
        _mm, grid=(n2, n_sh), a_spec=pl.BlockSpec((t2, d_model), lambda i, j: (i, 0)),
        b_spec=pl.BlockSpec((None, d_model, f_sh), lambda i, j: (j, 0, 0)),
        o_spec=pl.BlockSpec((None, t2, f_sh), lambda i, j: (j, i, 0)), o_shape=(n_sh, seq, f_sh), dims="nn",
        out_dtype=BF16)
    gate = ffn_up(h2, w["w_gate"], name="ffn_gate")
    up = ffn_up(h2, w["w_up"], name="ffn_up")
    flat = lambda a: a.reshape(n_sh * seq, f_sh)
    act, act_t = _rowwise(twice(swiglu_fwd), [flat(gate), flat(up)], [], [(f_sh, BF16)],
                          flipped=[(f_sh, BF16, n_sh)], name="swiglu", tm=1024)
    act, act_t = act.reshape(n_sh, seq, f_sh), act_t.reshape(n_sh, f_sh, seq)
    ffn_out = _mm(act, w["w_down"], name="ffn_down", grid=(n1, n_sh // 2), groups=2,
                  a_spec=pl.BlockSpec((2, t1, f_sh), lambda i, j: (j, i, 0)),
                  b_spec=pl.BlockSpec((2, f_sh, d_model), lambda i, j: (j, 0, 0)),
                  o_spec=pl.BlockSpec((t1, d_model), lambda i, j: (i, 0)), o_shape=(seq, d_model), dims="nn",
                  k_axis=1)

    dx2, dx2_b, sq = _rowwise(functools.partial(loss_head, inv_d=1.0 / d_model), [ffn_out, x1, target], [],
                              [(d_model, F32), (d_model, BF16)], [d_model], name="loss_head")
    loss = 0.5 * jnp.sum(sq) / d_model

    d_act = _mm(dx2_b, w["w_down"], name="ffn_down_dx", grid=(n2, n_sh),
                a_spec=pl.BlockSpec((t2, d_model), lambda i, j: (i, 0)),
                b_spec=pl.BlockSpec((None, f_sh, d_model), lambda i, j: (j, 0, 0)),
                o_spec=pl.BlockSpec((None, t2, f_sh), lambda i, j: (j, i, 0)), o_shape=(n_sh, seq, f_sh), dims="nt",
                out_dtype=BF16)
    g_w_down = _mm(act_t, dx2_b, name="ffn_down_dw", grid=(n_sh, n2),
                   a_spec=pl.BlockSpec((None, f_sh, t2), lambda j, k: (j, 0, k)),
                   b_spec=pl.BlockSpec((t2, d_model), lambda j, k: (k, 0)),
                   o_spec=pl.BlockSpec((None, f_sh, d_model), lambda j, k: (j, 0, 0)),
                   o_shape=(n_sh, f_sh, d_model), dims="nn", k_axis=1)
    d_gate, d_up = _rowwise(swiglu_bwd, [flat(d_act), flat(gate), flat(up)], [], [(f_sh, BF16), (f_sh, BF16)],
                            name="swiglu_bwd", tm=1024)
    d_gate, d_up = d_gate.reshape(n_sh, seq, f_sh), d_up.reshape(n_sh, seq, f_sh)
    d_h2 = _mm(d_gate, w["w_gate"], second=(d_up, w["w_up"]), name="ffn_up_gate_dx", grid=(n1, n_sh),
               a_spec=pl.BlockSpec((None, t1, f_sh), lambda i, j: (j, i, 0)),
               b_spec=pl.BlockSpec((None, d_model, f_sh), lambda i, j: (j, 0, 0)),
               o_spec=pl.BlockSpec((t1, d_model), lambda i, j: (i, 0)), o_shape=(seq, d_model), dims="nt", k_axis=1)
    ffn_dw = functools.partial(
        _mm, grid=(n_sh, n2), a_spec=pl.BlockSpec((d_model, t2), lambda j, k: (0, k)),
        b_spec=pl.BlockSpec((None, t2, f_sh), lambda j, k: (j, k, 0)),
        o_spec=pl.BlockSpec((None, d_model, f_sh), lambda j, k: (j, 0, 0)), o_shape=(n_sh, d_model, f_sh), dims="nn",
        k_axis=1)
    g_w_gate = ffn_dw(h2_t, d_gate, name="ffn_gate_dw")
    g_w_up = ffn_dw(h2_t, d_up, name="ffn_up_dw")
    dx1, g_g_ffn = _rowwise(residual_rms_bwd, [dx2, d_h2, x1], [p["g_ffn"]], [(d_model, F32)], [d_model],
                            name="rms_ffn_bwd")

    d_ycat = _mm_plain(dx1, w["w_out"], "nt", name="out_proj_dx", tn=2048)
    mix_w = a_width + s_width
    tm_o = _tile(mix_w, 1024)
    g_w_out = _mm(y_cat_t, dx1, name="out_proj_dw", grid=(mix_w // tm_o, n1),
                  a_spec=pl.BlockSpec((tm_o, t1), lambda i, k: (i, k)),
                  b_spec=pl.BlockSpec((t1, d_model), lambda i, k: (k, 0)),
                  o_spec=pl.BlockSpec((tm_o, d_model), lambda i, k: (i, 0)), o_shape=(mix_w, d_model), dims="nn",
                  k_axis=1)
    (d_ya, d_yg_direct, d_t, g_goa, g_gos, g_b_glu) = _rowwise(
        functools.partial(mix_out_bwd, a_width=a_width), [d_ycat, ya, yg, t_glu],
        [p["b_glu"], p["g_out_attn"], p["g_out_ssm"]],
        [(a_width, F32), (s_width, F32), (s_width, BF16)], [a_width, s_width, s_width], name="mix_out_bwd")
    d_yg = _mm_plain(d_t, w["w_glu"], "nt", name="glu_proj_dx", res=d_yg_direct, tn=s_width)
    g_w_glu = _mm(yg_t, d_t, name="glu_proj_dw", grid=(1, n1),
                  a_spec=pl.BlockSpec((s_width, t1), lambda i, k: (0, k)),
                  b_spec=pl.BlockSpec((t1, s_width), lambda i, k: (k, 0)),
                  o_spec=pl.BlockSpec((s_width, s_width), lambda i, k: (0, 0)), o_shape=(s_width, s_width),
                  dims="nn", k_axis=1)
    d_ypre, du_skip, g_ssm_d = _rowwise(gelu_skip_bwd, [d_yg, ypre, u], [p["ssm_d"]],
                                        [(s_width, F32), (s_width, F32)], [s_width], name="s5_skip_gelu_bwd")

    ffn_names, mix_names = ("w_ffn_gate", "w_ffn_up", "w_ffn_down"), ("w_glu", "w_out")
    ffn_slabs = [slab3(g, n) for g, n in zip((g_w_gate, g_w_up, g_w_down), ffn_names)]
    mix_slabs = [slab3(g, n) for g, n in zip((g_w_glu, g_w_out), mix_names)]
    du_dirs, adj, r_parts, gb_parts, gc_parts = [], [], [], [], []
    sib, part = {}, {}
    for d, (names, slabs) in enumerate(((ffn_names, ffn_slabs), (mix_names, mix_slabs))):
        win_re, win_im, tabs, wo_re, wo_im = bwd_in[d]
        (as_re, as_im, du_d), got = _s5_scan(d_ypre, win_re, win_im, tabs, wo_re, wo_im, reverse=(d == 0),
                                             name=f"s5_bwd_{d}", ride=_reduce_sibling(slabs))
        du_dirs.append(du_d)
        adj.append((as_re, as_im))
        sib[names] = got
        part[names] = _partial_sums(slabs, got, names)
    for d in range(2):
        (r_re, r_im, gbt_re, gbt_im, gct_re, gct_im), got = _s5_reduce(
            fwd[d][0], fwd[d][1], adj[d][0], adj[d][1], u, d_ypre, name=f"s5_reduce_{d}",
            ride=_reduce_chips(part[mix_names] if d == 0 else part[ffn_names][2:]))
        if d == 0:
            mix_chips = got
        else:
            down_chips = got
        r_parts.append((r_re.reshape(n_state, 1), r_im.reshape(n_state, 1)))
        gb_parts.append((_block_diag_take(gbt_re, SSM_C, SSM_P), _block_diag_take(gbt_im, SSM_C, SSM_P)))
        gc_parts.append((_block_diag_take(gct_re, SSM_C, SSM_P), _block_diag_take(gct_im, SSM_C, SSM_P)))
    cat = lambda i, parts: jnp.concatenate([parts[0][i], parts[1][i]], axis=0)
    gbb_re, gbb_im = cat(0, gb_parts).reshape(n_col, SSM_C), cat(1, gb_parts).reshape(n_col, SSM_C)
    g_a_re, g_a_im, g_ls, g_b_re, g_b_im = _s5_param_grads(
        a_re_c, a_im_c, dt_c, b_re_c, b_im_c, pw_re[:, 0].reshape(n_col, 1), pw_im[:, 0].reshape(n_col, 1),
        cf_re, cf_im, bb_re, bb_im, cat(0, r_parts), cat(1, r_parts), gbb_re, gbb_im)
    g_c_re = cat(0, gc_parts).reshape(2, n_groups, SSM_P, SSM_C).transpose(0, 1, 3, 2)
    g_c_im = -cat(1, gc_parts).reshape(2, n_groups, SSM_P, SSM_C).transpose(0, 1, 3, 2)

    (d_q, d_k, d_v, d_btab, g_qg, g_kg), ffn_chips = _attn_bwd(z, d_ya, qg4, kg4, btab,
                                                                ride=_reduce_chips(part[ffn_names][:2]))
    ffn_chips = ffn_chips + down_chips
    d_u = _rowwise(lambda a, b, c: a + b + c, [du_dirs[0], du_dirs[1], du_skip], [], [(s_width, BF16)],
                   name="s5_du_sum")[0]
    d_z = jnp.concatenate([d_q, d_k, d_v, d_u], axis=1)
    fold_heads = lambda g: g.reshape(n_heads, HEAD_DIM).sum(axis=0, keepdims=True)
    small = {
        "q_gain": fold_heads(g_qg), "k_gain": fold_heads(g_kg), "rpb": _bias_grad(d_btab, spread, n_heads),
        "ssm_a_re": g_a_re.reshape(2, n_groups, SSM_P), "ssm_a_im": g_a_im.reshape(2, n_groups, SSM_P),
        "ssm_b_re": g_b_re.reshape(2, n_groups, SSM_P, SSM_C), "ssm_b_im": g_b_im.reshape(2, n_groups, SSM_P, SSM_C),
        "ssm_c_re": g_c_re, "ssm_c_im": g_c_im,
        "ssm_log_step": g_ls.reshape(2, n_groups, SSM_P).sum(axis=-1),
        "ssm_d": g_ssm_d, "b_glu": g_b_glu, "g_out_attn": g_goa, "g_out_ssm": g_gos, "g_ffn": g_g_ffn,
    }
    packed = _pack([small[n] for n in SMALL_PACKED])
    g_w_in, got = _mm(h1_t, d_z, name="in_proj_dw", grid=(n_sh, n2),
                      a_spec=pl.BlockSpec((d_model, t2), lambda j, k: (0, k)),
                      b_spec=pl.BlockSpec((t2, in_sh), lambda j, k: (k, j)),
                      o_spec=pl.BlockSpec((None, d_model, in_sh), lambda j, k: (j, 0, 0)),
                      o_shape=(n_sh, d_model, in_sh), dims="nn", k_axis=1, ride=_gather_first([packed]))
    d_h1, got = _mm(d_z, w["w_in"], name="in_proj_dx", grid=(n1, n_sh // 2), groups=2,
                    a_spec=pl.BlockSpec((t1, 2 * in_sh), lambda i, j: (i, j)),
                    b_spec=pl.BlockSpec((2, d_model, in_sh), lambda i, j: (j, 0, 0)),
                    o_spec=pl.BlockSpec((t1, d_model), lambda i, j: (i, 0)), o_shape=(seq, d_model), dims="nt",
                    k_axis=1, ride=_gather_second(got) + _reduce_sibling([g_w_in]))
    small_gathered, in_sibling = got
    in_part = _partial_sums([g_w_in], [in_sibling], ("w_in",))
    (grad_x, g_g_mix), (in_chips,) = _rowwise(residual_rms_bwd, [dx1, d_h1, x], [p["g_mix"]], [(d_model, F32)],
                                              [d_model], name="rms_mix_bwd", ride=_reduce_chips(in_part))
    reduced = {"w_in": (g_w_in, in_sibling, in_chips)}
    for names, slabs, chips in ((ffn_names, ffn_slabs, ffn_chips), (mix_names, mix_slabs, mix_chips)):
        for i, n in enumerate(names):
            reduced[n] = (slabs[i], sib[names][i], chips[i])
    return loss, grad_x, small_gathered, g_g_mix, reduced


def x_norm(xv, g):
    return xv * _rstd(xv) * g


def s5_mid(y0, y1, uv, d_skip):
    ypre = y0 + y1 + d_skip * uv
    yg = _gelu(ypre)
    return ypre, yg, yg


def mix_out_fwd(ya, yg, t, b_glu, g_oa, g_os):
    ys = yg * _sigmoid(t + b_glu)
    return jnp.concatenate([ya * _rstd(ya) * g_oa, ys * _rstd(ys) * g_os], axis=1)


def mix_out_bwd(d_y, ya, yg, t, b_glu, g_oa, g_os, *, a_width):
    sg = _sigmoid(t + b_glu)
    ys = yg * sg
    d_ya, c_goa = _rms_bwd(d_y[:, :a_width], ya, g_oa)
    d_ys, c_gos = _rms_bwd(d_y[:, a_width:], ys, g_os)
    d_t = d_ys * yg * sg * (1.0 - sg)
    return d_ya, d_ys * sg, d_t, c_goa, c_gos, d_t


def gelu_skip_bwd(d_yg, ypre, uv, d_skip):
    d_ypre = d_yg * _gelu_grad(ypre)
    return d_ypre, d_ypre * d_skip, d_ypre * uv


def swiglu_fwd(gv, uv):
    gv, uv = gv.astype(F32), uv.astype(F32)
    return gv * _sigmoid(gv) * uv


def swiglu_bwd(d_act, gv, uv):
    d_act, gv, uv = d_act.astype(F32), gv.astype(F32), uv.astype(F32)
    sg = _sigmoid(gv)
    return d_act * uv * (sg * (1.0 + gv * (1.0 - sg))), d_act * gv * sg


def loss_head(ffn_out, x1, target, *, inv_d):
    diff = ffn_out + x1 - target
    return diff * inv_d, diff * inv_d, diff * diff


def residual_rms_bwd(d_res, d_h, xv, g):
    dx, c_g = _rms_bwd(d_h, xv, g)
    return d_res + dx, c_g


_ANY = pl.BlockSpec(memory_space=pl.ANY)


def _mesh_place():
    return lax.axis_index("x"), lax.axis_index("y"), lax.axis_index("c")


def _chips(x, y):
    return [(x, y), (1 - x, y), (x, 1 - y), (1 - x, 1 - y)]


def _slab(px, py, pc):
    return 4 * px + 2 * py + pc


def _all_gather(arrs, *, name):
    n = len(arrs)

    def body(*refs):
        in_refs, out_refs = refs[:n], refs[n:2 * n]
        send_sems, recv_sems, local_sems = refs[2 * n:]
        x, y, c = _mesh_place()
        me, sibling = (x, y, c), (x, y, 1 - c)
        others = _chips(x, y)[1:]

        def copy(w, k, block, to, src=None):
            dst = out_refs[w].at[_slab(*block)]
            return pltpu.make_async_remote_copy(
                src_ref=dst if src is None else src, dst_ref=dst, send_sem=send_sems.at[7 * w + k],
                recv_sem=recv_sems.at[7 * w + k], device_id=to, device_id_type=MESH)

        mine = [pltpu.make_async_copy(in_refs[w], out_refs[w].at[_slab(*me)], local_sems.at[w]) for w in range(n)]
        first = []
        for w in range(n):
            mine[w].start()
            first.append(copy(w, 0, me, sibling, src=in_refs[w]))
            first += [copy(w, 1 + j, me, (*chip, c), src=in_refs[w]) for j, chip in enumerate(others)]
        for cp in first:
            cp.start()
        passed = []
        for j, chip in enumerate(others):
            for w in range(n):
                copy(w, 1 + j, (*chip, c), me).wait_recv()
                fwd = copy(w, 4 + j, (*chip, c), sibling)
                fwd.start()
                passed.append(fwd)
        for w in range(n):
            copy(w, 0, sibling, me).wait_recv()
        for j, chip in enumerate(others):
            for w in range(n):
                copy(w, 4 + j, (*chip, 1 - c), me).wait_recv()
        for cp in first + passed:
            cp.wait_send()
        for cp in mine:
            cp.wait()

    return pl.pallas_call(
        body, name=name, in_specs=[_ANY] * n, out_specs=[_ANY] * n,
        out_shape=[jax.ShapeDtypeStruct((N_DEV,) + a.shape, a.dtype) for a in arrs],
        scratch_shapes=[pltpu.SemaphoreType.DMA((7 * n,)), pltpu.SemaphoreType.DMA((7 * n,)),
                        pltpu.SemaphoreType.DMA((n,))],
        compiler_params=pltpu.CompilerParams(has_side_effects=True),
    )(*arrs)


def _adamw(w, m, v, parts, *, name, slab, tr=256):
    rows, cols = w.shape
    tr = _tile(rows, tr)
    n_p = len(parts)

    def body(slab_ref, *refs):
        w_ref, m_ref, v_ref = refs[:3]
        p_refs = refs[3:3 + n_p]
        g_ref, d_ref, nm_ref, nv_ref = refs[3 + n_p:]
        g = None
        for (_, lead), r in zip(parts, p_refs):
            for piece in ([r[...]] if lead is None else [r[i] for i in range(lead)]):
                g = piece.astype(F32) if g is None else g + piece.astype(F32)
        new_m = ADAM_B1 * m_ref[...] + (1.0 - ADAM_B1) * g
        new_v = ADAM_B2 * v_ref[...] + (1.0 - ADAM_B2) * (g * g)
        m_hat = new_m / (1.0 - ADAM_B1 ** ADAM_STEP)
        v_hat = new_v / (1.0 - ADAM_B2 ** ADAM_STEP)
        g_ref[...] = g
        d_ref[...] = -ADAM_LR * (m_hat / (jnp.sqrt(v_hat) + ADAM_EPS) + ADAM_WD * w_ref[...])
        nm_ref[...] = new_m
        nv_ref[...] = new_v

    tile = pl.BlockSpec((tr, cols), lambda i, s: (i, 0))
    p_specs = [pl.BlockSpec((None, tr, cols), lambda i, s: (s[0], i, 0)) if lead is None
               else pl.BlockSpec((lead, tr, cols), lambda i, s: (0, i, 0)) for _, lead in parts]
    return pl.pallas_call(
        body, name=name,
        grid_spec=pltpu.PrefetchScalarGridSpec(num_scalar_prefetch=1, grid=(rows // tr,),
                                               in_specs=[tile] * 3 + p_specs, out_specs=[tile] * 4),
        out_shape=[jax.ShapeDtypeStruct((rows, cols), F32)] * 4, compiler_params=_params(),
    )(jnp.reshape(slab, (1,)).astype(jnp.int32), w, m, v, *[a for a, _ in parts])


_PACK_TILE = SUBLANES * 128
_PACK_ROWS = 512


def _pack(arrs):
    flat = []
    for a in arrs:
        f = a.reshape(-1)
        flat.append(jnp.pad(f, (0, (-f.shape[0]) % _PACK_TILE)))
    total = sum(f.shape[0] for f in flat)
    flat.append(jnp.zeros(((-total) % (_PACK_ROWS * 128),), F32))
    return jnp.concatenate(flat).reshape(-1, 128)


def _unpack(buf, shapes):
    out, at = [], 0
    flat = buf.reshape(-1)
    for s in shapes:
        n = math.prod(s)
        out.append(flat[at:at + n].reshape(s))
        at += n + (-n) % _PACK_TILE
    return out


BIG = ("w_in", "w_glu", "w_out", "w_ffn_gate", "w_ffn_up", "w_ffn_down")
WEIGHTS = ("g_mix", "w_in", "q_gain", "k_gain", "rpb", "ssm_a_re", "ssm_a_im", "ssm_b_re", "ssm_b_im", "ssm_c_re",
           "ssm_c_im", "ssm_log_step", "ssm_d", "w_glu", "b_glu", "g_out_attn", "g_out_ssm", "w_out", "g_ffn",
           "w_ffn_gate", "w_ffn_up", "w_ffn_down")
SMALL = tuple(n for n in WEIGHTS if n not in BIG)
SMALL_PACKED = tuple(n for n in SMALL if n != "g_mix")
VECTORS = ("g_mix", "q_gain", "k_gain", "ssm_d", "b_glu", "g_out_attn", "g_out_ssm", "g_ffn")


def kernel(x, g_mix, w_in, q_gain, k_gain, rpb, ssm_a_re, ssm_a_im, ssm_b_re, ssm_b_im, ssm_c_re, ssm_c_im, ssm_log_step, ssm_d, w_glu, b_glu, g_out_attn, g_out_ssm, w_out, g_ffn, w_ffn_gate, w_ffn_up, w_ffn_down, loss_target, m_g_mix, m_w_in, m_q_gain, m_k_gain, m_rpb, m_ssm_a_re, m_ssm_a_im, m_ssm_b_re, m_ssm_b_im, m_ssm_c_re, m_ssm_c_im, m_ssm_log_step, m_ssm_d, m_w_glu, m_b_glu, m_g_out_attn, m_g_out_ssm, m_w_out, m_g_ffn, m_w_ffn_gate, m_w_ffn_up, m_w_ffn_down, v_g_mix, v_w_in, v_q_gain, v_k_gain, v_rpb, v_ssm_a_re, v_ssm_a_im, v_ssm_b_re, v_ssm_b_im, v_ssm_c_re, v_ssm_c_im, v_ssm_log_step, v_ssm_d, v_w_glu, v_b_glu, v_g_out_attn, v_g_out_ssm, v_w_out, v_g_ffn, v_w_ffn_gate, v_w_ffn_up, v_w_ffn_down):
    wts = dict(g_mix=g_mix, w_in=w_in, q_gain=q_gain, k_gain=k_gain, rpb=rpb, ssm_a_re=ssm_a_re, ssm_a_im=ssm_a_im,
               ssm_b_re=ssm_b_re, ssm_b_im=ssm_b_im, ssm_c_re=ssm_c_re, ssm_c_im=ssm_c_im, ssm_log_step=ssm_log_step,
               ssm_d=ssm_d, w_glu=w_glu, b_glu=b_glu, g_out_attn=g_out_attn, g_out_ssm=g_out_ssm, w_out=w_out,
               g_ffn=g_ffn, w_ffn_gate=w_ffn_gate, w_ffn_up=w_ffn_up, w_ffn_down=w_ffn_down)
    mom = dict(g_mix=m_g_mix, w_in=m_w_in, q_gain=m_q_gain, k_gain=m_k_gain, rpb=m_rpb, ssm_a_re=m_ssm_a_re,
               ssm_a_im=m_ssm_a_im, ssm_b_re=m_ssm_b_re, ssm_b_im=m_ssm_b_im, ssm_c_re=m_ssm_c_re,
               ssm_c_im=m_ssm_c_im, ssm_log_step=m_ssm_log_step, ssm_d=m_ssm_d, w_glu=m_w_glu, b_glu=m_b_glu,
               g_out_attn=m_g_out_attn, g_out_ssm=m_g_out_ssm, w_out=m_w_out, g_ffn=m_g_ffn,
               w_ffn_gate=m_w_ffn_gate, w_ffn_up=m_w_ffn_up, w_ffn_down=m_w_ffn_down)
    var = dict(g_mix=v_g_mix, w_in=v_w_in, q_gain=v_q_gain, k_gain=v_k_gain, rpb=v_rpb, ssm_a_re=v_ssm_a_re,
               ssm_a_im=v_ssm_a_im, ssm_b_re=v_ssm_b_re, ssm_b_im=v_ssm_b_im, ssm_c_re=v_ssm_c_re,
               ssm_c_im=v_ssm_c_im, ssm_log_step=v_ssm_log_step, ssm_d=v_ssm_d, w_glu=v_w_glu, b_glu=v_b_glu,
               g_out_attn=v_g_out_attn, g_out_ssm=v_g_out_ssm, w_out=v_w_out, g_ffn=v_g_ffn,
               w_ffn_gate=v_w_ffn_gate, w_ffn_up=v_w_ffn_up, w_ffn_down=v_w_ffn_down)
    ix, iy, ic = _mesh_place()
    me = _slab(ix, iy, ic)
    d_model = x.shape[-1]

    shard = {n: wts[n][0] for n in BIG}
    shard_b = {n: shard[n].astype(BF16) for n in BIG}
    p = {n: (wts[n][0].reshape(1, -1) if n in VECTORS else wts[n][0]) for n in SMALL}

    loss, grad_x, small_gathered, g_g_mix, reduced = _local_step(x[0], loss_target[0], p, shard_b)
    loss = lax.psum(loss, ("x", "y", "c"))
    out = {}
    for n in BIG:
        slabs, from_sibling, from_chips = reduced[n]
        rows, cols = slabs.shape[1:]
        res = _adamw(shard[n].reshape(rows, cols), mom[n][0].reshape(rows, cols), var[n][0].reshape(rows, cols),
                     [(slabs, None), (from_sibling, 1), (from_chips, 3)], name=f"adamw_{n}", slab=me)
        out[n] = [r.reshape(wts[n].shape) for r in res]

    order = list(SMALL_PACKED)
    shapes = [wts[n].shape for n in order]
    res = _adamw(_pack([wts[n] for n in order]), _pack([mom[n] for n in order]), _pack([var[n] for n in order]),
                 [(small_gathered, N_DEV)], name="adamw_small", slab=me)
    for kind, buf in enumerate(res):
        for n, a in zip(order, _unpack(buf, shapes)):
            out.setdefault(n, [None] * 4)[kind] = a
    as_rows = lambda a: a.reshape(-1, 128)
    g_mix_all = _all_gather([as_rows(g_g_mix)], name="gather_g_mix")[0]
    res = _adamw(as_rows(wts["g_mix"]), as_rows(mom["g_mix"]), as_rows(var["g_mix"]), [(g_mix_all, N_DEV)],
                 name="adamw_g_mix", slab=me)
    out["g_mix"] = [r.reshape(wts["g_mix"].shape) for r in res]

    return (loss, grad_x[None], *[out[n][0] for n in WEIGHTS], *[out[n][1] for n in WEIGHTS],
            *[out[n][2] for n in WEIGHTS], *[out[n][3] for n in WEIGHTS])
```

```python
import functools
import math

import jax
import jax.numpy as jnp
from jax import lax
from jax.experimental import pallas as pl
from jax.experimental.pallas import tpu as pltpu

F32 = jnp.float32
BF16 = jnp.bfloat16

N_DEV = 8
GRID_W = 64
WIN_H = 8
WIN_W = 16
HEAD_DIM = 64
HEADS_PER_GROUP = 4
GROUP_LANES = HEADS_PER_GROUP * HEAD_DIM
SSM_C = 16
SSM_P = 64
GROUPS_PER_TILE = 8
U_TILE = GROUPS_PER_TILE * SSM_C
ST_TILE = GROUPS_PER_TILE * SSM_P
SUBLANES = 8
RMS_EPS = 1e-6
NEG_INF = -1e30
A_RE_MAX = -1e-4
ADAM_LR, ADAM_B1, ADAM_B2, ADAM_EPS, ADAM_WD, ADAM_STEP = 0.001, 0.9, 0.999, 1e-08, 0.01, 10
VMEM_LIMIT_V7X = 56 * 1024 * 1024
MESH = pl.DeviceIdType.MESH

_NN = (((1,), (0,)), ((), ()))
_NT = (((1,), (1,)), ((), ()))
_TN = (((0,), (0,)), ((), ()))
_DIMS = {"nn": _NN, "nt": _NT, "tn": _TN}


def _params(**kw):
    return pltpu.CompilerParams(vmem_limit_bytes=VMEM_LIMIT_V7X, **kw)


def _dot(a, b, dims=_NN):
    return lax.dot_general(a, b, dims, preferred_element_type=F32)


def _mm(a, b, *, name, grid, a_spec, b_spec, o_spec, o_shape, dims, k_axis=None, res=None, out_dtype=F32,
        exact=False, second=None, ride=None, groups=1):
    dn = _DIMS[dims]
    nk = 1 if k_axis is None else grid[k_axis]
    acc_shape = tuple(d for d in o_spec.block_shape if d is not None)
    n_in = 2 + (2 if second is not None else 0)

    def body(*refs):
        a_ref, b_ref = refs[:2]
        r_ref = refs[n_in] if res is not None else None
        o_ref, acc = refs[-2:]
        def product(x_ref, y_ref):
            if groups == 1:
                return _dot(x_ref[...].astype(BF16), y_ref[...].astype(BF16), dn)
            total, width = None, x_ref.shape[-1] // groups
            for s in range(groups):
                x = x_ref[s] if len(x_ref.shape) == 3 else x_ref[:, s * width:(s + 1) * width]
                t = _dot(x.astype(BF16), y_ref[s].astype(BF16), dn)
                total = t if total is None else total + t
            return total

        if exact:
            p = lax.dot_general(a_ref[...], b_ref[...], dn, precision=lax.Precision.HIGHEST,
                                preferred_element_type=F32)
        else:
            p = product(a_ref, b_ref)
        if second is not None:
            p = p + product(refs[2], refs[3])

        def finish(v):
            if r_ref is not None:
                v = v + r_ref[...].astype(F32)
            o_ref[...] = v.astype(out_dtype)

        if nk == 1:
            finish(p)
        else:
            k = pl.program_id(k_axis)

            @pl.when(k == 0)
            def _():
                acc[...] = p

            @pl.when(k > 0)
            def _():
                acc[...] += p

            @pl.when(k == nk - 1)
            def _():
                finish(acc[...])

    ins = [a, b] + (list(second) if second is not None else []) + ([res] if res is not None else [])
    in_specs = [a_spec, b_spec] * (n_in // 2) + ([o_spec] if res is not None else [])
    first, last = _grid_ends(grid)
    (out,), rode = _call(
        body, name=name, grid=grid, in_specs=in_specs, out_specs=[o_spec],
        out_shape=[jax.ShapeDtypeStruct(o_shape, out_dtype)],
        scratch_shapes=[pltpu.VMEM(acc_shape if nk > 1 else (SUBLANES, 128), F32)],
        args=ins, ride=ride, first=first, last=last)
    return out if ride is None else (out, rode)


def _tile(n, want):
    if n <= want:
        return n
    t = want
    while n % t:
        t //= 2
    return t


def _mm_plain(a, b, dims, *, name, res=None, out_dtype=F32, tm=512, tn=512, tk=512, exact=False, ride=None):
    if dims == "nn":
        (m, k), n = a.shape, b.shape[1]
    elif dims == "nt":
        (m, k), n = a.shape, b.shape[0]
    else:
        (k, m), n = a.shape, b.shape[1]
    tm, tn = _tile(m, tm), _tile(n, tn)
    if dims == "tn":
        tk = _tile(k, tk)
        grid = (m // tm, n // tn, k // tk)
        a_spec = pl.BlockSpec((tk, tm), lambda i, j, kk: (kk, i))
        b_spec = pl.BlockSpec((tk, tn), lambda i, j, kk: (kk, j))
        o_spec = pl.BlockSpec((tm, tn), lambda i, j, kk: (i, j))
        return _mm(a, b, name=name, grid=grid, a_spec=a_spec, b_spec=b_spec, o_spec=o_spec, o_shape=(m, n),
                   dims=dims, k_axis=2, res=res, out_dtype=out_dtype)
    grid = (n // tn, m // tm)
    a_spec = pl.BlockSpec((tm, k), lambda j, i: (i, 0))
    if dims == "nn":
        b_spec = pl.BlockSpec((k, tn), lambda j, i: (0, j))
    else:
        b_spec = pl.BlockSpec((tn, k), lambda j, i: (j, 0))
    o_spec = pl.BlockSpec((tm, tn), lambda j, i: (i, j))
    return _mm(a, b, name=name, grid=grid, a_spec=a_spec, b_spec=b_spec, o_spec=o_spec, o_shape=(m, n), dims=dims,
               res=res, out_dtype=out_dtype, exact=exact, ride=ride)


def _rowwise(fn, tiled, bcast, outs, accs=(), *, name, tm=256, flipped=(), ride=None):
    m = tiled[0].shape[0]
    tm = _tile(m, tm)
    n_t, n_b, n_o, n_f = len(tiled), len(bcast), len(outs), len(flipped)

    def body(*refs):
        ins = [r[...] for r in refs[: n_t + n_b]]
        o_refs = refs[n_t + n_b: n_t + n_b + n_o]
        f_refs = refs[n_t + n_b + n_o: n_t + n_b + n_o + n_f]
        a_refs = refs[n_t + n_b + n_o + n_f:]
        res = fn(*ins)
        if not isinstance(res, (tuple, list)):
            res = (res,)
        for r, v in zip(o_refs, res[:n_o]):
            r[...] = v.astype(r.dtype)
        for r, v in zip(f_refs, res[n_o:n_o + n_f]):
            r[...] = v.astype(F32).T.astype(r.dtype)
        first = pl.program_id(0) == 0
        for r, v in zip(a_refs, res[n_o + n_f:]):
            s = jnp.sum(v, axis=0, keepdims=True)

            @pl.when(first)
            def _():
                r[...] = s

            @pl.when(jnp.logical_not(first))
            def _():
                r[...] += s

    in_specs = [pl.BlockSpec((tm, t.shape[1]), lambda i: (i, 0)) for t in tiled]
    in_specs += [pl.BlockSpec(b.shape, lambda i, nd=b.ndim: (0,) * nd) for b in bcast]
    out_specs = [pl.BlockSpec((tm, n), lambda i: (i, 0)) for n, _ in outs]
    out_specs += [pl.BlockSpec((n, tm), lambda i, per=m // tm // g: (i // per, i % per)) for n, _, g in flipped]
    out_specs += [pl.BlockSpec((1, n), lambda i: (0, 0)) for n in accs]
    out_shape = [jax.ShapeDtypeStruct((m, n), dt) for n, dt in outs]
    out_shape += [jax.ShapeDtypeStruct((g * n, m // g), dt) for n, dt, g in flipped]
    out_shape += [jax.ShapeDtypeStruct((1, n), F32) for n in accs]
    first, last = _grid_ends((m // tm,))
    res, rode = _call(body, name=name, grid=(m // tm,), in_specs=in_specs, out_specs=out_specs, out_shape=out_shape,
                      scratch_shapes=[], args=list(tiled) + list(bcast), ride=ride, first=first, last=last)
    return res if ride is None else (res, rode)


def _rstd(x):
    return lax.rsqrt(jnp.mean(x * x, axis=-1, keepdims=True) + RMS_EPS)


def _rms_bwd(dh, x, g):
    xh = x * _rstd(x)
    dxh = dh * g
    dx = _rstd(x) * (dxh - xh * jnp.mean(dxh * xh, axis=-1, keepdims=True))
    return dx, dh * xh


def _sigmoid(x):
    return 1.0 / (1.0 + jnp.exp(-x))


_GELU_K = math.sqrt(2.0 / math.pi)
_GELU_C = 0.044715


def _gelu(x):
    return 0.5 * x * (1.0 + jnp.tanh(_GELU_K * (x + _GELU_C * x * x * x)))


def _gelu_grad(x):
    th = jnp.tanh(_GELU_K * (x + _GELU_C * x * x * x))
    return 0.5 * (1.0 + th) + 0.5 * x * (1.0 - th * th) * _GELU_K * (1.0 + 3.0 * _GELU_C * x * x)


class _Exchange:
    def __init__(self, arrays, outs, n_sems, sends, recvs=None, local=None, aliases=None):
        self.arrays, self.outs, self.n_sems = list(arrays), list(outs), n_sems
        self.sends, self.local, self.aliases = sends, local, aliases or {}
        self.recvs = recvs or (lambda i, o: [(k, dst) for k, _, dst, _ in sends(i, o)])

    def __add__(self, other):
        na, no, ns = len(self.arrays), len(self.outs), self.n_sems
        mine = lambda f: (lambda i, o: f(i[:na], o[:no]))
        shift = lambda f, at: (lambda i, o: [(k + ns,) + tuple(rest) for k, *rest in f(i[na:], o[no:])]) if at else None
        both = lambda f, g: (lambda i, o: f(i, o) + g(i, o))
        local = None
        if self.local or other.local:
            la = mine(self.local) if self.local else (lambda i, o: [])
            lb = (lambda i, o: other.local(i[na:], o[no:])) if other.local else (lambda i, o: [])
            local = both(la, lb)
        aliases = dict(self.aliases)
        aliases.update({na + i: no + o for i, o in other.aliases.items()})
        return _Exchange(self.arrays + other.arrays, self.outs + other.outs, ns + other.n_sems,
                         both(mine(self.sends), shift(other.sends, True)),
                         both(mine(self.recvs), shift(other.recvs, True)), local, aliases)

    def descriptors(self, in_refs, out_refs, send_sems, recv_sems, local_sems):
        me = _mesh_place()
        remote = lambda k, src, dst, to: pltpu.make_async_remote_copy(
            src_ref=src, dst_ref=dst, send_sem=send_sems.at[k], recv_sem=recv_sems.at[k], device_id=to,
            device_id_type=MESH)
        out = [remote(*s) for s in self.sends(in_refs, out_refs)]
        arrive = [remote(k, dst, dst, me) for k, dst in self.recvs(in_refs, out_refs)]
        own = [pltpu.make_async_copy(src, dst, local_sems.at[i])
               for i, (src, dst) in enumerate(self.local(in_refs, out_refs) if self.local else [])]
        return out, arrive, own

    def start(self, *refs):
        out, _, own = self.descriptors(*refs)
        for cp in own + out:
            cp.start()

    def finish(self, *refs):
        out, arrive, own = self.descriptors(*refs)
        for cp in arrive:
            cp.wait_recv()
        for cp in out:
            cp.wait_send()
        for cp in own:
            cp.wait()


def _call(body, *, name, grid, in_specs, out_specs, out_shape, scratch_shapes, args, ride=None, first=None, last=None):
    if ride is None:
        res = pl.pallas_call(body, name=name, grid=grid, in_specs=in_specs, out_specs=out_specs, out_shape=out_shape,
                             scratch_shapes=scratch_shapes, compiler_params=_params())(*args)
        return list(res), []
    n_in, n_out, n_scr = len(in_specs), len(out_specs), len(scratch_shapes)
    r_in, r_out = len(ride.arrays), len(ride.outs)

    def wrapped(*refs):
        ins, refs = refs[:n_in], refs[n_in:]
        x_in, refs = refs[:r_in], refs[r_in:]
        outs, refs = refs[:n_out], refs[n_out:]
        x_out, refs = refs[:r_out], refs[r_out:]
        scr, sems = refs[:n_scr], refs[n_scr:]

        @pl.when(first())
        def _():
            ride.start(x_in, x_out, *sems)

        body(*ins, *outs, *scr)

        @pl.when(last())
        def _():
            ride.finish(x_in, x_out, *sems)

    n_local = max(1, len(ride.arrays))
    res = pl.pallas_call(
        wrapped, name=name, grid=grid, in_specs=list(in_specs) + [_ANY] * r_in,
        out_specs=list(out_specs) + [_ANY] * r_out, out_shape=list(out_shape) + ride.outs,
        scratch_shapes=list(scratch_shapes) + [pltpu.SemaphoreType.DMA((ride.n_sems,)),
                                               pltpu.SemaphoreType.DMA((ride.n_sems,)),
                                               pltpu.SemaphoreType.DMA((n_local,))],
        input_output_aliases={n_in + i: n_out + o for i, o in ride.aliases.items()},
        compiler_params=_params(has_side_effects=True),
    )(*args, *ride.arrays)
    return list(res[:n_out]), list(res[n_out:])


def _gather_first(shards):
    def sends(i, o):
        x, y, c = _mesh_place()
        peers = [(x, y, 1 - c)] + [(px, py, c) for px, py in _chips(x, y)[1:]]
        return [(4 * w + k, i[w], o[w].at[_slab(x, y, c)], to) for w in range(len(i)) for k, to in enumerate(peers)]

    def recvs(i, o):
        x, y, c = _mesh_place()
        peers = [(x, y, 1 - c)] + [(px, py, c) for px, py in _chips(x, y)[1:]]
        return [(4 * w + k, o[w].at[_slab(*peer)]) for w in range(len(i)) for k, peer in enumerate(peers)]

    def local(i, o):
        return [(i[w], o[w].at[_slab(*_mesh_place())]) for w in range(len(i))]

    outs = [jax.ShapeDtypeStruct((N_DEV,) + a.shape, a.dtype) for a in shards]
    return _Exchange(shards, outs, 4 * len(shards), sends, recvs, local)


def _gather_second(gathered):
    def sends(i, o):
        x, y, c = _mesh_place()
        return [(3 * w + j, o[w].at[_slab(px, py, c)], o[w].at[_slab(px, py, c)], (x, y, 1 - c))
                for w in range(len(o)) for j, (px, py) in enumerate(_chips(x, y)[1:])]

    def recvs(i, o):
        x, y, c = _mesh_place()
        return [(3 * w + j, o[w].at[_slab(px, py, 1 - c)])
                for w in range(len(o)) for j, (px, py) in enumerate(_chips(x, y)[1:])]

    outs = [jax.ShapeDtypeStruct(a.shape, a.dtype) for a in gathered]
    return _Exchange(gathered, outs, 3 * len(gathered), sends, recvs, aliases={w: w for w in range(len(gathered))})


def _reduce_sibling(slabs):
    def sends(i, o):
        x, y, c = _mesh_place()
        return [(4 * w + k, i[w].at[_slab(px, py, 1 - c)], o[w].at[k], (x, y, 1 - c))
                for w in range(len(i)) for k, (px, py) in enumerate(_chips(x, y))]

    outs = [jax.ShapeDtypeStruct((4,) + a.shape[1:], a.dtype) for a in slabs]
    return _Exchange(slabs, outs, 4 * len(slabs), sends)


def _reduce_chips(partials):
    def sends(i, o):
        x, y, c = _mesh_place()
        return [(3 * w + k, i[w].at[k], o[w].at[k], (px, py, c))
                for w in range(len(i)) for k, (px, py) in enumerate(_chips(x, y)[1:])]

    outs = [jax.ShapeDtypeStruct(a.shape, a.dtype) for a in partials]
    return _Exchange(partials, outs, 3 * len(partials), sends)


def _head_masks():
    lane_head = lax.broadcasted_iota(jnp.int32, (1, GROUP_LANES), 1) // HEAD_DIM
    return [(lane_head == h).astype(F32) for h in range(HEADS_PER_GROUP)]


def _head_block_diag():
    r = lax.broadcasted_iota(jnp.int32, (GROUP_LANES, GROUP_LANES), 0) // HEAD_DIM
    c = lax.broadcasted_iota(jnp.int32, (GROUP_LANES, GROUP_LANES), 1) // HEAD_DIM
    return (r == c).astype(BF16)


def _head_mean(x, bd):
    hi = x.astype(BF16)
    lo = (x - hi.astype(F32)).astype(BF16)
    return (_dot(hi, bd) + _dot(lo, bd)) * (1.0 / HEAD_DIM)


def _stack_heads(x, masks):
    return jnp.concatenate([x * m for m in masks], axis=0)


def _unstack_heads(xs, masks):
    out = xs[0:GRID_W] * masks[0]
    for h in range(1, HEADS_PER_GROUP):
        out = out + xs[h * GRID_W:(h + 1) * GRID_W] * masks[h]
    return out


def _row_start(r, rows):
    return jnp.clip(r - WIN_H // 2, 0, rows - WIN_H)


ROWS_PER_STEP = 2


def _attn_common_specs(seq, n_hg, rows):
    win_keys = WIN_H * GRID_W
    q_spec = pl.BlockSpec((ROWS_PER_STEP * GRID_W, GROUP_LANES), lambda g, r: (r, g))
    k_spec = pl.BlockSpec((seq, GROUP_LANES), lambda g, r: (0, n_hg + g))
    v_spec = pl.BlockSpec((seq, GROUP_LANES), lambda g, r: (0, 2 * n_hg + g))
    gain_spec = pl.BlockSpec((1, GROUP_LANES), lambda g, r: (0, 0))

    def variant(r):
        return _row_start(r, rows) - r + (WIN_H - 1)

    bias_specs = [pl.BlockSpec((None, None, HEADS_PER_GROUP, GRID_W, win_keys),
                               lambda g, r, h=h: (g, variant(ROWS_PER_STEP * r + h), 0, 0, 0))
                  for h in range(ROWS_PER_STEP)]
    return q_spec, k_spec, v_spec, gain_spec, bias_specs, variant


def _attn_prepare_kv(k_ref, v_ref, kg, kn_scr, vb_scr, bd, seq):
    chunk = _tile(seq, 512)

    def step(c, carry):
        rows = pl.ds(pl.multiple_of(c * chunk, chunk), chunk)
        k = k_ref[rows, :]
        kn_scr[rows, :] = (k * lax.rsqrt(_head_mean(k * k, bd) + RMS_EPS) * kg).astype(BF16)
        vb_scr[rows, :] = v_ref[rows, :].astype(BF16)
        return carry

    lax.fori_loop(0, seq // chunk, step, 0)


def _attn_probs(qn, kw, bias, masks):
    qs = _stack_heads(qn, masks).astype(BF16)
    s = _dot(qs, kw, _NT) * (1.0 / math.sqrt(HEAD_DIM)) + bias
    m = jnp.max(s, axis=-1, keepdims=True)
    p = jnp.exp(s - m)
    return qs, p * (1.0 / jnp.sum(p, axis=-1, keepdims=True))


def _grid_ends(grid):
    first = lambda: functools.reduce(jnp.logical_and, [pl.program_id(a) == 0 for a in range(len(grid))])
    last = lambda: functools.reduce(jnp.logical_and, [pl.program_id(a) == n - 1 for a, n in enumerate(grid)])
    return first, last


def _attn_fwd(z, qg4, kg4, btab, ride=None):
    seq = z.shape[0]
    a_width = btab.shape[0] * GROUP_LANES
    n_hg, rows, win_keys = btab.shape[0], seq // GRID_W, WIN_H * GRID_W
    q_spec, k_spec, v_spec, gain_spec, bias_specs, _ = _attn_common_specs(seq, n_hg, rows)
    grid = (n_hg, rows // ROWS_PER_STEP)

    def body(q_ref, k_ref, v_ref, qg_ref, kg_ref, *rest):
        b_refs, (o_ref, kn_scr, vb_scr) = rest[:ROWS_PER_STEP], rest[ROWS_PER_STEP:]
        bd, masks = _head_block_diag(), _head_masks()

        @pl.when(pl.program_id(1) == 0)
        def _():
            _attn_prepare_kv(k_ref, v_ref, kg_ref[...], kn_scr, vb_scr, bd, seq)

        for h in range(ROWS_PER_STEP):
            r = ROWS_PER_STEP * pl.program_id(1) + h
            mine = slice(h * GRID_W, (h + 1) * GRID_W)
            win = pl.ds(pl.multiple_of(_row_start(r, rows) * GRID_W, GRID_W), win_keys)
            q = q_ref[mine, :]
            qn = q * lax.rsqrt(_head_mean(q * q, bd) + RMS_EPS) * qg_ref[...]
            bias = b_refs[h][...].reshape(HEADS_PER_GROUP * GRID_W, win_keys)
            _, p = _attn_probs(qn, kn_scr[win, :], bias, masks)
            o_ref[mine, :] = _unstack_heads(_dot(p.astype(BF16), vb_scr[win, :]), masks)

    first, last = _grid_ends(grid)
    (ya,), rode = _call(
        body, name="attn_fwd", grid=grid,
        in_specs=[q_spec, k_spec, v_spec, gain_spec, gain_spec] + bias_specs,
        out_specs=[pl.BlockSpec((ROWS_PER_STEP * GRID_W, GROUP_LANES), lambda g, r: (r, g))],
        out_shape=[jax.ShapeDtypeStruct((seq, a_width), F32)],
        scratch_shapes=[pltpu.VMEM((seq, GROUP_LANES), BF16), pltpu.VMEM((seq, GROUP_LANES), BF16)],
        args=(z, z, z, qg4, kg4) + (btab,) * ROWS_PER_STEP, ride=ride, first=first, last=last)
    return ya, rode


def _attn_bwd(z, d_out, qg4, kg4, btab, ride=None):
    seq = z.shape[0]
    n_hg, rows, win_keys = btab.shape[0], seq // GRID_W, WIN_H * GRID_W
    a_width = n_hg * GROUP_LANES
    q_spec, k_spec, v_spec, gain_spec, bias_specs, variant = _attn_common_specs(seq, n_hg, rows)
    scale = 1.0 / math.sqrt(HEAD_DIM)
    grid = (n_hg, rows // ROWS_PER_STEP)

    def body(q_ref, k_ref, v_ref, do_ref, qg_ref, kg_ref, *rest):
        b_refs, rest = rest[:ROWS_PER_STEP], rest[ROWS_PER_STEP:]
        dq_ref, dk_out, dv_out, db_ref, dqg_ref, dkg_ref, kn_scr, vb_scr, dk_ref, dv_ref = rest
        bd, masks = _head_block_diag(), _head_masks()

        @pl.when(pl.program_id(1) == 0)
        def _():
            _attn_prepare_kv(k_ref, v_ref, kg_ref[...], kn_scr, vb_scr, bd, seq)
            dk_ref[...] = jnp.zeros_like(dk_ref)
            dv_ref[...] = jnp.zeros_like(dv_ref)
            db_ref[...] = jnp.zeros_like(db_ref)
            dqg_ref[...] = jnp.zeros_like(dqg_ref)

        qg = qg_ref[...]
        for h in range(ROWS_PER_STEP):
            r = ROWS_PER_STEP * pl.program_id(1) + h
            mine = slice(h * GRID_W, (h + 1) * GRID_W)
            win = pl.ds(pl.multiple_of(_row_start(r, rows) * GRID_W, GRID_W), win_keys)
            q = q_ref[mine, :]
            rq = lax.rsqrt(_head_mean(q * q, bd) + RMS_EPS)
            qh = q * rq
            kw, vw = kn_scr[win, :], vb_scr[win, :]
            bias = b_refs[h][...].reshape(HEADS_PER_GROUP * GRID_W, win_keys)
            qs, p = _attn_probs(qh * qg, kw, bias, masks)
            dos = _stack_heads(do_ref[mine, :], masks).astype(BF16)
            dp = _dot(dos, vw, _NT)
            ds = p * (dp - jnp.sum(p * dp, axis=-1, keepdims=True))
            db_ref[variant(r)] += ds.reshape(HEADS_PER_GROUP, GRID_W, win_keys)
            dsb = ds.astype(BF16)
            dqn = _unstack_heads(_dot(dsb, kw), masks) * scale
            dk_ref[win, :] += _dot(dsb, qs, _TN) * scale
            dv_ref[win, :] += _dot(p.astype(BF16), dos, _TN)
            dqg_ref[...] += jnp.sum(dqn * qh, axis=0, keepdims=True)
            dqh = dqn * qg
            dq_ref[mine, :] = (rq * (dqh - qh * _head_mean(dqh * qh, bd))).astype(BF16)

        @pl.when(pl.program_id(1) == grid[1] - 1)
        def _():
            chunk = _tile(seq, 512)
            kg = kg_ref[...]

            def step(c, dkg):
                rws = pl.ds(pl.multiple_of(c * chunk, chunk), chunk)
                k = k_ref[rws, :]
                rk = lax.rsqrt(_head_mean(k * k, bd) + RMS_EPS)
                kh = k * rk
                dkn = dk_ref[rws, :]
                dkh = dkn * kg
                dk_out[rws, :] = (rk * (dkh - kh * _head_mean(dkh * kh, bd))).astype(BF16)
                dv_out[rws, :] = dv_ref[rws, :].astype(BF16)
                return dkg + jnp.sum(dkn * kh, axis=0, keepdims=True)

            dkg_ref[...] = lax.fori_loop(0, seq // chunk, step, jnp.zeros((1, GROUP_LANES), F32))

    col_spec = pl.BlockSpec((seq, GROUP_LANES), lambda g, r: (0, g))
    gsum_spec = pl.BlockSpec((None, 1, GROUP_LANES), lambda g, r: (g, 0, 0))
    first, last = _grid_ends(grid)
    rows_spec = pl.BlockSpec((ROWS_PER_STEP * GRID_W, GROUP_LANES), lambda g, r: (r, g))
    return _call(
        body, name="attn_bwd", grid=grid,
        in_specs=[q_spec, k_spec, v_spec, rows_spec, gain_spec, gain_spec] + bias_specs,
        out_specs=[rows_spec, col_spec, col_spec,
                   pl.BlockSpec((None, WIN_H, HEADS_PER_GROUP, GRID_W, win_keys), lambda g, r: (g, 0, 0, 0, 0)),
                   gsum_spec, gsum_spec],
        out_shape=[jax.ShapeDtypeStruct((seq, a_width), BF16)] * 3
        + [jax.ShapeDtypeStruct(btab.shape, F32)]
        + [jax.ShapeDtypeStruct((n_hg, 1, GROUP_LANES), F32)] * 2,
        scratch_shapes=[pltpu.VMEM((seq, GROUP_LANES), BF16), pltpu.VMEM((seq, GROUP_LANES), BF16),
                        pltpu.VMEM((seq, GROUP_LANES), F32), pltpu.VMEM((seq, GROUP_LANES), F32)],
        args=(z, z, z, d_out, qg4, kg4) + (btab,) * ROWS_PER_STEP, ride=ride, first=first, last=last)


DC_SLOTS = 2 * WIN_W


def _bias_spread(c):
    shape = (WIN_H * DC_SLOTS, WIN_H * GRID_W)
    rows = lax.broadcasted_iota(jnp.int32, shape, 0)
    cols = lax.broadcasted_iota(jnp.int32, shape, 1)
    row_i, row_d = rows // DC_SLOTS, rows % DC_SLOTS
    col_i, kc = cols // GRID_W, cols % GRID_W
    col_start = jnp.clip(c - WIN_W // 2, 0, GRID_W - WIN_W)
    col_in = (kc >= col_start) & (kc < col_start + WIN_W)
    hit = (row_i == col_i) & (row_d == kc - c + (WIN_W - 1)) & col_in
    mask_slot = (row_i == 0) & (row_d == DC_SLOTS - 1) & jnp.logical_not(col_in)
    return jnp.where(hit, 1.0, jnp.where(mask_slot, NEG_INF, 0.0)).astype(F32)


def _bias_table(rpb):
    n_h = rpb.shape[0]
    n_hg = n_h // HEADS_PER_GROUP
    rows = jnp.stack([rpb[:, v:v + WIN_H] for v in range(WIN_H)], axis=1)
    rows = jnp.pad(rows, ((0, 0), (0, 0), (0, 0), (0, DC_SLOTS - rows.shape[-1])))
    rows = rows.at[:, :, 0, DC_SLOTS - 1].set(1.0)
    rows = rows.reshape(n_hg, HEADS_PER_GROUP, WIN_H, WIN_H * DC_SLOTS).transpose(0, 2, 1, 3)
    n_rows, win_keys, depth = n_h * WIN_H, WIN_H * GRID_W, WIN_H * DC_SLOTS

    def body(r_ref, o_ref):
        for cc in range(SUBLANES):
            spread = _bias_spread(pl.program_id(0) * SUBLANES + cc)
            o_ref[cc] = lax.dot_general(r_ref[...], spread, _NN, precision=lax.Precision.HIGHEST,
                                        preferred_element_type=F32)

    tab = pl.pallas_call(
        body, name="rpb_spread", grid=(GRID_W // SUBLANES,),
        in_specs=[pl.BlockSpec((n_rows, depth), lambda c: (0, 0))],
        out_specs=pl.BlockSpec((SUBLANES, n_rows, win_keys), lambda c: (c, 0, 0)),
        out_shape=jax.ShapeDtypeStruct((GRID_W, n_rows, win_keys), F32), compiler_params=_params(),
    )(rows.reshape(n_rows, depth))
    return tab.transpose(1, 0, 2).reshape(n_hg, WIN_H, HEADS_PER_GROUP, GRID_W, win_keys)


def _bias_grad(dtab, n_h):
    n_hg = n_h // HEADS_PER_GROUP
    n_rows, win_keys, depth = n_h * WIN_H, WIN_H * GRID_W, WIN_H * DC_SLOTS

    def body(d_ref, o_ref):
        total = None
        for cc in range(SUBLANES):
            spread = _bias_spread(pl.program_id(0) * SUBLANES + cc).astype(BF16)
            t = _dot(d_ref[cc].astype(BF16), spread, _NT)
            total = t if total is None else total + t

        @pl.when(pl.program_id(0) == 0)
        def _():
            o_ref[...] = total

        @pl.when(pl.program_id(0) > 0)
        def _():
            o_ref[...] += total

    d_rows = pl.pallas_call(
        body, name="rpb_diag_sum", grid=(GRID_W // SUBLANES,),
        in_specs=[pl.BlockSpec((SUBLANES, n_rows, win_keys), lambda c: (c, 0, 0))],
        out_specs=pl.BlockSpec((n_rows, depth), lambda c: (0, 0)),
        out_shape=jax.ShapeDtypeStruct((n_rows, depth), F32), compiler_params=_params(),
    )(dtab.reshape(n_rows, GRID_W, win_keys).transpose(1, 0, 2))
    d_rows = d_rows.reshape(n_hg, WIN_H, HEADS_PER_GROUP, WIN_H, DC_SLOTS).transpose(0, 2, 1, 3, 4)
    d_rows = d_rows.reshape(n_h, WIN_H, WIN_H, DC_SLOTS)[..., : 2 * WIN_W - 1]
    out = jnp.zeros((n_h, 2 * WIN_H - 1, 2 * WIN_W - 1), F32)
    for v in range(WIN_H):
        out = out.at[:, v:v + WIN_H].add(d_rows[:, v])
    return out


def _cmul(ar, ai, br, bi):
    return ar * br - ai * bi, ar * bi + ai * br


def _s5_discretize(a_re, a_im, dt, b_re, b_im, ride=None):
    c = b_re.shape[1]

    def fn(are, aim, dt_, bre, bim):
        lr, li = jnp.minimum(are, A_RE_MAX), aim
        mag = jnp.exp(lr * dt_)
        l1r, l1i = mag * jnp.cos(li * dt_), mag * jnp.sin(li * dt_)
        den = lr * lr + li * li
        nr, ni = l1r - 1.0, l1i
        cr, ci = (nr * lr + ni * li) / den, (ni * lr - nr * li) / den
        bbr, bbi = _cmul(cr, ci, bre, bim)
        shape = (are.shape[0], SUBLANES)
        lane = lax.broadcasted_iota(jnp.int32, shape, 1)
        pr, pi = l1r, l1i
        acc_r, acc_i = jnp.zeros(shape, F32), jnp.zeros(shape, F32)
        for k in range(SUBLANES):
            acc_r = jnp.where(lane == k, pr, acc_r)
            acc_i = jnp.where(lane == k, pi, acc_i)
            pr, pi = _cmul(pr, pi, l1r, l1i)
        return acc_r, acc_i, cr, ci, bbr, bbi

    return _rowwise(fn, [a_re, a_im, dt, b_re, b_im], [],
                    [(SUBLANES, F32), (SUBLANES, F32), (1, F32), (1, F32), (c, F32), (c, F32)],
                    name="s5_discretize", tm=1024, ride=ride)


def _s5_param_grads(a_re, a_im, dt, b_re, b_im, l1r, l1i, cr, ci, bbr, bbi, r_re, r_im, gb_re, gb_im):
    c = b_re.shape[1]

    def fn(are, aim, dt_, bre, bim, l1r_, l1i_, cr_, ci_, bbr_, bbi_, rr, ri, gbr, gbi):
        lr, li = jnp.minimum(are, A_RE_MAX), aim
        den = lr * lr + li * li
        dbr, dbi = _cmul(cr_, -ci_, gbr, gbi)
        gcr, gci = _cmul(bre, -bim, gbr, gbi)
        gcr, gci = jnp.sum(gcr, axis=1, keepdims=True), jnp.sum(gci, axis=1, keepdims=True)
        qr, qi = _cmul(bbr_, -bbi_, gbr, gbi)
        qr = rr - jnp.sum(qr, axis=1, keepdims=True)
        qi = ri - jnp.sum(qi, axis=1, keepdims=True)
        tr, ti = _cmul(gcr, gci, lr / den, li / den)
        ur, ui = _cmul(l1r_, -l1i_, tr, ti)
        gwr, gwi = qr + ur, qi + ui
        vr, vi = _cmul(cr_, -ci_, lr / den, li / den)
        vr, vi = _cmul(gcr, gci, vr, vi)
        glr, gli = dt_ * gwr - vr, dt_ * gwi - vi
        return jnp.where(are < A_RE_MAX, glr, 0.0), gli, (gwr * lr + gwi * li) * dt_, dbr, dbi

    return _rowwise(fn, [a_re, a_im, dt, b_re, b_im, l1r, l1i, cr, ci, bbr, bbi, r_re, r_im, gb_re, gb_im], [],
                    [(1, F32), (1, F32), (1, F32), (c, F32), (c, F32)], name="s5_param_grads", tm=1024)


def _s5_scan(v, win_re, win_im, tabs, wo_re, wo_im, *, reverse, name, t_chunk=256, ride=None):
    seq, width = v.shape
    n_tiles, n_state = width // U_TILE, width * (SSM_P // SSM_C)
    t_chunk = _tile(seq, t_chunk)
    n_chunks, n_blk = seq // t_chunk, t_chunk // SUBLANES
    last_row = 0 if reverse else SUBLANES - 1

    def chunk_of(j):
        return (n_chunks - 1 - j) if reverse else j

    def body(v_ref, wir_ref, wii_ref, tab_ref, wor_ref, woi_ref, sr_ref, si_ref, y_ref, carry, wr, wi):
        @pl.when(pl.program_id(0) == 0)
        def _():
            carry[...] = jnp.zeros_like(carry)

        for jt in range(n_tiles):
            ls = slice(jt * ST_TILE, (jt + 1) * ST_TILE)
            us = slice(jt * U_TILE, (jt + 1) * U_TILE)
            vj = v_ref[:, us].astype(BF16)
            consts = [tab_ref[k, :, ls] for k in range(8)]
            xr = _dot(vj, wir_ref[jt]).reshape(n_blk, SUBLANES, ST_TILE)
            xi = _dot(vj, wii_ref[jt]).reshape(n_blk, SUBLANES, ST_TILE)
            for s, k in enumerate((1, 2, 4)):
                sh = (SUBLANES - k) if reverse else k
                tr, ti = pltpu.roll(xr, sh, 1), pltpu.roll(xi, sh, 1)
                lr, li = consts[2 * s][None], consts[2 * s + 1][None]
                xr, xi = xr + lr * tr - li * ti, xi + lr * ti + li * tr
            wr[...] = xr.reshape(t_chunk, ST_TILE)
            wi[...] = xi.reshape(t_chunk, ST_TILE)

            def blk(b, c, consts=consts):
                cr, ci = c
                bb = (n_blk - 1 - b) if reverse else b
                rows = pl.ds(pl.multiple_of(bb * SUBLANES, SUBLANES), SUBLANES)
                lr, li = consts[6], consts[7]
                xr = wr[rows, :] + lr * cr - li * ci
                xi = wi[rows, :] + lr * ci + li * cr
                wr[rows, :], wi[rows, :] = xr, xi
                shape = (SUBLANES, ST_TILE)
                return (jnp.broadcast_to(xr[last_row:last_row + 1], shape),
                        jnp.broadcast_to(xi[last_row:last_row + 1], shape))

            cr, ci = lax.fori_loop(0, n_blk, blk, (carry[0, :, ls], carry[1, :, ls]), unroll=2)
            carry[0, :, ls], carry[1, :, ls] = cr, ci
            xr_b, xi_b = wr[...].astype(BF16), wi[...].astype(BF16)
            sr_ref[:, ls], si_ref[:, ls] = xr_b, xi_b
            y_ref[:, us] = _dot(xr_b, wor_ref[jt]) + _dot(xi_b, woi_ref[jt])

    whole = lambda a: pl.BlockSpec(a.shape, lambda j, nd=a.ndim: (0,) * nd)
    st_spec = pl.BlockSpec((t_chunk, n_state), lambda j: (chunk_of(j), 0))
    v_spec = pl.BlockSpec((t_chunk, width), lambda j: (chunk_of(j), 0))
    first, last = _grid_ends((n_chunks,))
    return _call(
        body, name=name, grid=(n_chunks,),
        in_specs=[v_spec, whole(win_re), whole(win_im), whole(tabs), whole(wo_re), whole(wo_im)],
        out_specs=[st_spec, st_spec, v_spec],
        out_shape=[jax.ShapeDtypeStruct((seq, n_state), BF16)] * 2 + [jax.ShapeDtypeStruct((seq, width), F32)],
        scratch_shapes=[pltpu.VMEM((2, SUBLANES, n_state), F32), pltpu.VMEM((t_chunk, ST_TILE), F32),
                        pltpu.VMEM((t_chunk, ST_TILE), F32)],
        args=(v, win_re, win_im, tabs, wo_re, wo_im), ride=ride, first=first, last=last)


def _s5_reduce(x_re, x_im, a_re, a_im, u, dy, *, name, t_chunk=512, ride=None):
    seq, n_state = x_re.shape
    width = u.shape[1]
    n_tiles = width // U_TILE
    t_chunk = _tile(seq, t_chunk)

    def body(xr_ref, xi_ref, ar_ref, ai_ref, u_ref, dy_ref, rr_ref, ri_ref, gbr_ref, gbi_ref, gcr_ref, gci_ref):
        xrb, xib, arb, aib = xr_ref[...], xi_ref[...], ar_ref[...], ai_ref[...]
        xr, xi, ar, ai = xrb.astype(F32), xib.astype(F32), arb.astype(F32), aib.astype(F32)
        ub, dyb = u_ref[...].astype(BF16), dy_ref[...].astype(BF16)
        parts = (jnp.sum(ar * xr + ai * xi, axis=0, keepdims=True), jnp.sum(ai * xr - ar * xi, axis=0, keepdims=True),
                 _dot(arb, ub, _TN), _dot(aib, ub, _TN), _dot(xrb, dyb, _TN), _dot(xib, dyb, _TN))
        first = pl.program_id(1) == 0
        for ref, val in zip((rr_ref, ri_ref, gbr_ref, gbi_ref, gcr_ref, gci_ref), parts):
            @pl.when(first)
            def _():
                ref[...] = val

            @pl.when(jnp.logical_not(first))
            def _():
                ref[...] += val

    st_spec = pl.BlockSpec((t_chunk, ST_TILE), lambda j, t: (t, j))
    u_spec = pl.BlockSpec((t_chunk, U_TILE), lambda j, t: (t, j))
    r_spec = pl.BlockSpec((1, ST_TILE), lambda j, t: (0, j))
    g_spec = pl.BlockSpec((None, ST_TILE, U_TILE), lambda j, t: (j, 0, 0))
    first, last = _grid_ends((n_tiles, seq // t_chunk))
    return _call(
        body, name=name, grid=(n_tiles, seq // t_chunk),
        in_specs=[st_spec] * 4 + [u_spec] * 2,
        out_specs=[r_spec, r_spec] + [g_spec] * 4,
        out_shape=[jax.ShapeDtypeStruct((1, n_state), F32)] * 2
        + [jax.ShapeDtypeStruct((n_tiles, ST_TILE, U_TILE), F32)] * 4,
        scratch_shapes=[], args=(x_re, x_im, a_re, a_im, u, dy), ride=ride, first=first, last=last)


def _block_diag_in(ms):
    m = jnp.stack(ms)
    n, g, c, p = m.shape
    m5 = m.reshape(n, g // GROUPS_PER_TILE, GROUPS_PER_TILE, c, p)
    eye = jnp.eye(GROUPS_PER_TILE, dtype=m.dtype)
    out = m5[:, :, :, :, None, :] * eye[None, None, :, None, :, None]
    return out.astype(BF16).reshape(n, g // GROUPS_PER_TILE, GROUPS_PER_TILE * c, GROUPS_PER_TILE * p)


def _block_diag_take(m, c, p):
    t = m.shape[0]
    m5 = m.reshape(t, GROUPS_PER_TILE, p, GROUPS_PER_TILE, c)
    idx = jnp.arange(GROUPS_PER_TILE)
    return m5[:, idx, :, idx, :].transpose(1, 0, 2, 3).reshape(t * GROUPS_PER_TILE, p, c)


def _scan_tables(pw_re, pw_im, reverse):
    row = jnp.arange(SUBLANES)[:, None]
    tabs = []
    for k in (1, 2, 4):
        keep = (row <= SUBLANES - 1 - k) if reverse else (row >= k)
        tabs += [jnp.where(keep, pw_re[k - 1][None, :], 0.0), jnp.where(keep, pw_im[k - 1][None, :], 0.0)]
    order = jnp.arange(SUBLANES)[::-1] if reverse else jnp.arange(SUBLANES)
    tabs += [pw_re[order], pw_im[order]]
    return jnp.stack(tabs)


def _partial_sums(slabs, from_sibling, names):
    x, y, c = _mesh_place()
    theirs = jnp.stack([_slab(px, py, c) for px, py in _chips(x, y)[1:]]).astype(jnp.int32)
    out = []
    for s, f, n in zip(slabs, from_sibling, names):
        rows, cols = s.shape[1:]
        tr = _tile(rows, 512)

        def body(idx_ref, a_ref, b_ref, o_ref):
            o_ref[...] = (a_ref[...] + b_ref[...]).astype(BF16)

        out.append(pl.pallas_call(
            body, name=f"reduce_add_{n}",
            grid_spec=pltpu.PrefetchScalarGridSpec(
                num_scalar_prefetch=1, grid=(3, rows // tr),
                in_specs=[pl.BlockSpec((None, tr, cols), lambda k, i, idx: (idx[k], i, 0)),
                          pl.BlockSpec((None, tr, cols), lambda k, i, idx: (k + 1, i, 0))],
                out_specs=pl.BlockSpec((None, tr, cols), lambda k, i, idx: (k, i, 0))),
            out_shape=jax.ShapeDtypeStruct((3, rows, cols), BF16), compiler_params=_params(),
        )(theirs, s, f))
    return out


def _local_step(x, target, p, shards):
    seq, d_model = x.shape
    a_width = p["g_out_attn"].shape[-1]
    s_width = p["g_out_ssm"].shape[-1]
    n_heads = a_width // HEAD_DIM
    n_hg = n_heads // HEADS_PER_GROUP
    n_groups = s_width // SSM_C
    n_sh, in_sh = N_DEV, shards["w_in"].shape[-1]
    f_sh = shards["w_ffn_gate"].shape[-1]
    w = {}
    slab3 = lambda g, n: g.reshape(N_DEV, -1, shards[n].shape[-1])
    t2, t1 = _tile(seq, 2048), _tile(seq, 1024)
    n2, n1 = seq // t2, seq // t1

    n_col = 2 * n_groups * SSM_P
    col = lambda a: a.reshape(n_col, 1)
    a_re_c, a_im_c = col(p["ssm_a_re"]), col(p["ssm_a_im"])
    dt_c = col(jnp.broadcast_to(jnp.exp(p["ssm_log_step"])[:, :, None], (2, n_groups, SSM_P)))
    b_re_c, b_im_c = p["ssm_b_re"].reshape(n_col, SSM_C), p["ssm_b_im"].reshape(n_col, SSM_C)
    (pw_re, pw_im, cf_re, cf_im, bb_re, bb_im), got = _s5_discretize(a_re_c, a_im_c, dt_c, b_re_c, b_im_c,
                                                                     ride=_gather_first([shards["w_in"]]))

    twice = lambda f: (lambda *a: (f(*a),) * 2)
    (h1, h1_t), (w["w_in"],) = _rowwise(twice(x_norm), [x], [p["g_mix"]], [(d_model, BF16)],
                                        flipped=[(d_model, BF16, 1)], name="rms_mix", ride=_gather_second(got))
    z = _mm(h1, w["w_in"], name="in_proj", grid=(n2, n_sh),
            a_spec=pl.BlockSpec((t2, d_model), lambda i, j: (i, 0)),
            b_spec=pl.BlockSpec((None, d_model, in_sh), lambda i, j: (j, 0, 0)),
            o_spec=pl.BlockSpec((t2, in_sh), lambda i, j: (i, j)), o_shape=(seq, n_sh * in_sh), dims="nn")
    qg4 = jnp.tile(p["q_gain"], (1, HEADS_PER_GROUP))
    kg4 = jnp.tile(p["k_gain"], (1, HEADS_PER_GROUP))
    btab = _bias_table(p["rpb"])
    ya, got_a = _attn_fwd(z, qg4, kg4, btab, ride=_gather_first([shards["w_ffn_gate"], shards["w_ffn_up"]]))
    u = z[:, 3 * a_width:]
    n_state = n_groups * SSM_P
    pw_re = pw_re.reshape(2, n_state, SUBLANES).transpose(0, 2, 1)
    pw_im = pw_im.reshape(2, n_state, SUBLANES).transpose(0, 2, 1)
    bb_re4, bb_im4 = bb_re.reshape(2, n_groups, SSM_P, SSM_C), bb_im.reshape(2, n_groups, SSM_P, SSM_C)
    c_re, c_im = p["ssm_c_re"], p["ssm_c_im"]
    t21 = lambda a: a.transpose(0, 2, 1)
    maps_in = _block_diag_in([m for d in range(2) for m in (t21(bb_re4[d]), t21(bb_im4[d]), c_re[d], -c_im[d])])
    maps_out = _block_diag_in([m for d in range(2) for m in (t21(c_re[d]), -t21(c_im[d]), bb_re4[d], bb_im4[d])])
    fwd, bwd_in = [], []
    got_b = None
    for d in range(2):
        rev = d == 1
        tabs = _scan_tables(pw_re[d], pw_im[d], rev)
        if d == 0:
            ride = _gather_first([shards["w_glu"], shards["w_out"]])
        else:
            ride = _gather_second(got_a + got_b) + _gather_first([shards["w_ffn_down"]])
        (xs_re, xs_im, y_d), got = _s5_scan(u, maps_in[4 * d], maps_in[4 * d + 1], tabs, maps_out[4 * d],
                                            maps_out[4 * d + 1], reverse=rev, name=f"s5_fwd_{d}", ride=ride)
        if d == 0:
            got_b = got
        fwd.append((xs_re, xs_im, y_d))
        bwd_in.append((maps_in[4 * d + 2], maps_in[4 * d + 3], _scan_tables(pw_re[d], -pw_im[d], not rev),
                       maps_out[4 * d + 2], maps_out[4 * d + 3]))
    w["w_gate"], w["w_up"], w_glu_full, w_out_full, w_down_first = got
    w["w_glu"] = w_glu_full.reshape(-1, s_width)
    w["w_out"] = w_out_full.reshape(-1, d_model)

    ypre, yg, yg_t = _rowwise(s5_mid, [fwd[0][2], fwd[1][2], u], [p["ssm_d"]], [(s_width, F32), (s_width, F32)],
                              flipped=[(s_width, BF16, 1)], name="s5_skip_gelu")
    t_glu = _mm_plain(yg, w["w_glu"], "nn", name="glu_proj", tn=s_width)
    y_cat, y_cat_t = _rowwise(twice(mix_out_fwd), [ya, yg, t_glu], [p["b_glu"], p["g_out_attn"], p["g_out_ssm"]],
                              [(a_width + s_width, BF16)], flipped=[(a_width + s_width, BF16, 1)], name="mix_out")
    x1, (w["w_down"],) = _mm_plain(y_cat, w["w_out"], "nn", name="out_proj", res=x, tn=2048,
                                   ride=_gather_second([w_down_first]))

    h2, h2_t = _rowwise(twice(x_norm), [x1], [p["g_ffn"]], [(d_model, BF16)], flipped=[(d_model, BF16, 1)],
                        name="rms_ffn")
    ffn_up = functools.partial(
        _mm, grid=(n2, n_sh), a_spec=pl.BlockSpec((t2, d_model), lambda i, j: (i, 0)),
        b_spec=pl.BlockSpec((None, d_model, f_sh), lambda i, j: (j, 0, 0)),
        o_spec=pl.BlockSpec((None, t2, f_sh), lambda i, j: (j, i, 0)), o_shape=(n_sh, seq, f_sh), dims="nn",
        out_dtype=BF16)
    gate = ffn_up(h2, w["w_gate"], name="ffn_gate")
    up = ffn_up(h2, w["w_up"], name="ffn_up")
    flat = lambda a: a.reshape(n_sh * seq, f_sh)
    act, act_t = _rowwise(twice(swiglu_fwd), [flat(gate), flat(up)], [], [(f_sh, BF16)],
                          flipped=[(f_sh, BF16, n_sh)], name="swiglu", tm=1024)
    act, act_t = act.reshape(n_sh, seq, f_sh), act_t.reshape(n_sh, f_sh, seq)
    ffn_out = _mm(act, w["w_down"], name="ffn_down", grid=(n1, n_sh // 2), groups=2,
                  a_spec=pl.BlockSpec((2, t1, f_sh), lambda i, j: (j, i, 0)),
                  b_spec=pl.BlockSpec((2, f_sh, d_model), lambda i, j: (j, 0, 0)),
                  o_spec=pl.BlockSpec((t1, d_model), lambda i, j: (i, 0)), o_shape=(seq, d_model), dims="nn",
                  k_axis=1)

    dx2, dx2_b, sq = _rowwise(functools.partial(loss_head, inv_d=1.0 / d_model), [ffn_out, x1, target], [],
                              [(d_model, F32), (d_model, BF16)], [d_model], name="loss_head")
    loss = 0.5 * jnp.sum(sq) / d_model

    d_act = _mm(dx2_b, w["w_down"], name="ffn_down_dx", grid=(n2, n_sh),
                a_spec=pl.BlockSpec((t2, d_model), lambda i, j: (i, 0)),
                b_spec=pl.BlockSpec((None, f_sh, d_model), lambda i, j: (j, 0, 0)),
                o_spec=pl.BlockSpec((None, t2, f_sh), lambda i, j: (j, i, 0)), o_shape=(n_sh, seq, f_sh), dims="nt",
                out_dtype=BF16)
    g_w_down = _mm(act_t, dx2_b, name="ffn_down_dw", grid=(n_sh, n2),
                   a_spec=pl.BlockSpec((None, f_sh, t2), lambda j, k: (j, 0, k)),
                   b_spec=pl.BlockSpec((t2, d_model), lambda j, k: (k, 0)),
                   o_spec=pl.BlockSpec((None, f_sh, d_model), lambda j, k: (j, 0, 0)),
                   o_shape=(n_sh, f_sh, d_model), dims="nn", k_axis=1)
    d_gate, d_up = _rowwise(swiglu_bwd, [flat(d_act), flat(gate), flat(up)], [], [(f_sh, BF16), (f_sh, BF16)],
                            name="swiglu_bwd", tm=1024)
    d_gate, d_up = d_gate.reshape(n_sh, seq, f_sh), d_up.reshape(n_sh, seq, f_sh)
    d_h2 = _mm(d_gate, w["w_gate"], second=(d_up, w["w_up"]), name="ffn_up_gate_dx", grid=(n1, n_sh),
               a_spec=pl.BlockSpec((None, t1, f_sh), lambda i, j: (j, i, 0)),
               b_spec=pl.BlockSpec((None, d_model, f_sh), lambda i, j: (j, 0, 0)),
               o_spec=pl.BlockSpec((t1, d_model), lambda i, j: (i, 0)), o_shape=(seq, d_model), dims="nt", k_axis=1)
    ffn_dw = functools.partial(
        _mm, grid=(n_sh, n2), a_spec=pl.BlockSpec((d_model, t2), lambda j, k: (0, k)),
        b_spec=pl.BlockSpec((None, t2, f_sh), lambda j, k: (j, k, 0)),
        o_spec=pl.BlockSpec((None, d_model, f_sh), lambda j, k: (j, 0, 0)), o_shape=(n_sh, d_model, f_sh), dims="nn",
        k_axis=1)
    g_w_gate = ffn_dw(h2_t, d_gate, name="ffn_gate_dw")
    g_w_up = ffn_dw(h2_t, d_up, name="ffn_up_dw")
    dx1, g_g_ffn = _rowwise(residual_rms_bwd, [dx2, d_h2, x1], [p["g_ffn"]], [(d_model, F32)], [d_model],
                            name="rms_ffn_bwd")

    d_ycat = _mm_plain(dx1, w["w_out"], "nt", name="out_proj_dx", tn=2048)
    mix_w = a_width + s_width
    tm_o = _tile(mix_w, 1024)
    g_w_out = _mm(y_cat_t, dx1, name="out_proj_dw", grid=(mix_w // tm_o, n1),
                  a_spec=pl.BlockSpec((tm_o, t1), lambda i, k: (i, k)),
                  b_spec=pl.BlockSpec((t1, d_model), lambda i, k: (k, 0)),
                  o_spec=pl.BlockSpec((tm_o, d_model), lambda i, k: (i, 0)), o_shape=(mix_w, d_model), dims="nn",
                  k_axis=1)
    (d_ya, d_yg_direct, d_t, g_goa, g_gos, g_b_glu) = _rowwise(
        functools.partial(mix_out_bwd, a_width=a_width), [d_ycat, ya, yg, t_glu],
        [p["b_glu"], p["g_out_attn"], p["g_out_ssm"]],
        [(a_width, F32), (s_width, F32), (s_width, BF16)], [a_width, s_width, s_width], name="mix_out_bwd")
    d_yg = _mm_plain(d_t, w["w_glu"], "nt", name="glu_proj_dx", res=d_yg_direct, tn=s_width)
    g_w_glu = _mm(yg_t, d_t, name="glu_proj_dw", grid=(1, n1),
                  a_spec=pl.BlockSpec((s_width, t1), lambda i, k: (0, k)),
                  b_spec=pl.BlockSpec((t1, s_width), lambda i, k: (k, 0)),
                  o_spec=pl.BlockSpec((s_width, s_width), lambda i, k: (0, 0)), o_shape=(s_width, s_width),
                  dims="nn", k_axis=1)
    d_ypre, du_skip, g_ssm_d = _rowwise(gelu_skip_bwd, [d_yg, ypre, u], [p["ssm_d"]],
                                        [(s_width, F32), (s_width, F32)], [s_width], name="s5_skip_gelu_bwd")

    ffn_names, mix_names = ("w_ffn_gate", "w_ffn_up", "w_ffn_down"), ("w_glu", "w_out")
    ffn_slabs = [slab3(g, n) for g, n in zip((g_w_gate, g_w_up, g_w_down), ffn_names)]
    mix_slabs = [slab3(g, n) for g, n in zip((g_w_glu, g_w_out), mix_names)]
    du_dirs, adj, r_parts, gb_parts, gc_parts = [], [], [], [], []
    sib, part = {}, {}
    for d, (names, slabs) in enumerate(((ffn_names, ffn_slabs), (mix_names, mix_slabs))):
        win_re, win_im, tabs, wo_re, wo_im = bwd_in[d]
        (as_re, as_im, du_d), got = _s5_scan(d_ypre, win_re, win_im, tabs, wo_re, wo_im, reverse=(d == 0),
                                             name=f"s5_bwd_{d}", ride=_reduce_sibling(slabs))
        du_dirs.append(du_d)
        adj.append((as_re, as_im))
        sib[names] = got
        part[names] = _partial_sums(slabs, got, names)
    for d in range(2):
        (r_re, r_im, gbt_re, gbt_im, gct_re, gct_im), got = _s5_reduce(
            fwd[d][0], fwd[d][1], adj[d][0], adj[d][1], u, d_ypre, name=f"s5_reduce_{d}",
            ride=_reduce_chips(part[mix_names] if d == 0 else part[ffn_names][2:]))
        if d == 0:
            mix_chips = got
        else:
            down_chips = got
        r_parts.append((r_re.reshape(n_state, 1), r_im.reshape(n_state, 1)))
        gb_parts.append((_block_diag_take(gbt_re, SSM_C, SSM_P), _block_diag_take(gbt_im, SSM_C, SSM_P)))
        gc_parts.append((_block_diag_take(gct_re, SSM_C, SSM_P), _block_diag_take(gct_im, SSM_C, SSM_P)))
    cat = lambda i, parts: jnp.concatenate([parts[0][i], parts[1][i]], axis=0)
    gbb_re, gbb_im = cat(0, gb_parts).reshape(n_col, SSM_C), cat(1, gb_parts).reshape(n_col, SSM_C)
    g_a_re, g_a_im, g_ls, g_b_re, g_b_im = _s5_param_grads(
        a_re_c, a_im_c, dt_c, b_re_c, b_im_c, pw_re[:, 0].reshape(n_col, 1), pw_im[:, 0].reshape(n_col, 1),
        cf_re, cf_im, bb_re, bb_im, cat(0, r_parts), cat(1, r_parts), gbb_re, gbb_im)
    g_c_re = cat(0, gc_parts).reshape(2, n_groups, SSM_P, SSM_C).transpose(0, 1, 3, 2)
    g_c_im = -cat(1, gc_parts).reshape(2, n_groups, SSM_P, SSM_C).transpose(0, 1, 3, 2)

    (d_q, d_k, d_v, d_btab, g_qg, g_kg), ffn_chips = _attn_bwd(z, d_ya, qg4, kg4, btab,
                                                                ride=_reduce_chips(part[ffn_names][:2]))
    ffn_chips = ffn_chips + down_chips
    d_u = _rowwise(lambda a, b, c: a + b + c, [du_dirs[0], du_dirs[1], du_skip], [], [(s_width, BF16)],
                   name="s5_du_sum")[0]
    d_z = jnp.concatenate([d_q, d_k, d_v, d_u], axis=1)
    fold_heads = lambda g: g.reshape(n_heads, HEAD_DIM).sum(axis=0, keepdims=True)
    small = {
        "q_gain": fold_heads(g_qg), "k_gain": fold_heads(g_kg), "rpb": _bias_grad(d_btab, n_heads),
        "ssm_a_re": g_a_re.reshape(2, n_groups, SSM_P), "ssm_a_im": g_a_im.reshape(2, n_groups, SSM_P),
        "ssm_b_re": g_b_re.reshape(2, n_groups, SSM_P, SSM_C), "ssm_b_im": g_b_im.reshape(2, n_groups, SSM_P, SSM_C),
        "ssm_c_re": g_c_re, "ssm_c_im": g_c_im,
        "ssm_log_step": g_ls.reshape(2, n_groups, SSM_P).sum(axis=-1),
        "ssm_d": g_ssm_d, "b_glu": g_b_glu, "g_out_attn": g_goa, "g_out_ssm": g_gos, "g_ffn": g_g_ffn,
    }
    packed = _pack([small[n] for n in SMALL_PACKED])
    g_w_in, got = _mm(h1_t, d_z, name="in_proj_dw", grid=(n_sh, n2),
                      a_spec=pl.BlockSpec((d_model, t2), lambda j, k: (0, k)),
                      b_spec=pl.BlockSpec((t2, in_sh), lambda j, k: (k, j)),
                      o_spec=pl.BlockSpec((None, d_model, in_sh), lambda j, k: (j, 0, 0)),
                      o_shape=(n_sh, d_model, in_sh), dims="nn", k_axis=1, ride=_gather_first([packed]))
    d_h1, got = _mm(d_z, w["w_in"], name="in_proj_dx", grid=(n1, n_sh // 2), groups=2,
                    a_spec=pl.BlockSpec((t1, 2 * in_sh), lambda i, j: (i, j)),
                    b_spec=pl.BlockSpec((2, d_model, in_sh), lambda i, j: (j, 0, 0)),
                    o_spec=pl.BlockSpec((t1, d_model), lambda i, j: (i, 0)), o_shape=(seq, d_model), dims="nt",
                    k_axis=1, ride=_gather_second(got) + _reduce_sibling([g_w_in]))
    small_gathered, in_sibling = got
    in_part = _partial_sums([g_w_in], [in_sibling], ("w_in",))
    (grad_x, g_g_mix), (in_chips,) = _rowwise(residual_rms_bwd, [dx1, d_h1, x], [p["g_mix"]], [(d_model, F32)],
                                              [d_model], name="rms_mix_bwd", ride=_reduce_chips(in_part))
    reduced = {"w_in": (g_w_in, in_sibling, in_chips)}
    for names, slabs, chips in ((ffn_names, ffn_slabs, ffn_chips), (mix_names, mix_slabs, mix_chips)):
        for i, n in enumerate(names):
            reduced[n] = (slabs[i], sib[names][i], chips[i])
    return loss, grad_x, small_gathered, g_g_mix, reduced


def x_norm(xv, g):
    return xv * _rstd(xv) * g


def s5_mid(y0, y1, uv, d_skip):
    ypre = y0 + y1 + d_skip * uv
    yg = _gelu(ypre)
    return ypre, yg, yg


def mix_out_fwd(ya, yg, t, b_glu, g_oa, g_os):
    ys = yg * _sigmoid(t + b_glu)
    return jnp.concatenate([ya * _rstd(ya) * g_oa, ys * _rstd(ys) * g_os], axis=1)


def mix_out_bwd(d_y, ya, yg, t, b_glu, g_oa, g_os, *, a_width):
    sg = _sigmoid(t + b_glu)
    ys = yg * sg
    d_ya, c_goa = _rms_bwd(d_y[:, :a_width], ya, g_oa)
    d_ys, c_gos = _rms_bwd(d_y[:, a_width:], ys, g_os)
    d_t = d_ys * yg * sg * (1.0 - sg)
    return d_ya, d_ys * sg, d_t, c_goa, c_gos, d_t


def gelu_skip_bwd(d_yg, ypre, uv, d_skip):
    d_ypre = d_yg * _gelu_grad(ypre)
    return d_ypre, d_ypre * d_skip, d_ypre * uv


def swiglu_fwd(gv, uv):
    gv, uv = gv.astype(F32), uv.astype(F32)
    return gv * _sigmoid(gv) * uv


def swiglu_bwd(d_act, gv, uv):
    d_act, gv, uv = d_act.astype(F32), gv.astype(F32), uv.astype(F32)
    sg = _sigmoid(gv)
    return d_act * uv * (sg * (1.0 + gv * (1.0 - sg))), d_act * gv * sg


def loss_head(ffn_out, x1, target, *, inv_d):
    diff = ffn_out + x1 - target
    return diff * inv_d, diff * inv_d, diff * diff


def residual_rms_bwd(d_res, d_h, xv, g):
    dx, c_g = _rms_bwd(d_h, xv, g)
    return d_res + dx, c_g


_ANY = pl.BlockSpec(memory_space=pl.ANY)


def _mesh_place():
    return lax.axis_index("x"), lax.axis_index("y"), lax.axis_index("c")


def _chips(x, y):
    return [(x, y), (1 - x, y), (x, 1 - y), (1 - x, 1 - y)]


def _slab(px, py, pc):
    return 4 * px + 2 * py + pc


def _all_gather(arrs, *, name):
    n = len(arrs)

    def body(*refs):
        in_refs, out_refs = refs[:n], refs[n:2 * n]
        send_sems, recv_sems, local_sems = refs[2 * n:]
        x, y, c = _mesh_place()
        me, sibling = (x, y, c), (x, y, 1 - c)
        others = _chips(x, y)[1:]

        def copy(w, k, block, to, src=None):
            dst = out_refs[w].at[_slab(*block)]
            return pltpu.make_async_remote_copy(
                src_ref=dst if src is None else src, dst_ref=dst, send_sem=send_sems.at[7 * w + k],
                recv_sem=recv_sems.at[7 * w + k], device_id=to, device_id_type=MESH)

        mine = [pltpu.make_async_copy(in_refs[w], out_refs[w].at[_slab(*me)], local_sems.at[w]) for w in range(n)]
        first = []
        for w in range(n):
            mine[w].start()
            first.append(copy(w, 0, me, sibling, src=in_refs[w]))
            first += [copy(w, 1 + j, me, (*chip, c), src=in_refs[w]) for j, chip in enumerate(others)]
        for cp in first:
            cp.start()
        passed = []
        for j, chip in enumerate(others):
            for w in range(n):
                copy(w, 1 + j, (*chip, c), me).wait_recv()
                fwd = copy(w, 4 + j, (*chip, c), sibling)
                fwd.start()
                passed.append(fwd)
        for w in range(n):
            copy(w, 0, sibling, me).wait_recv()
        for j, chip in enumerate(others):
            for w in range(n):
                copy(w, 4 + j, (*chip, 1 - c), me).wait_recv()
        for cp in first + passed:
            cp.wait_send()
        for cp in mine:
            cp.wait()

    return pl.pallas_call(
        body, name=name, in_specs=[_ANY] * n, out_specs=[_ANY] * n,
        out_shape=[jax.ShapeDtypeStruct((N_DEV,) + a.shape, a.dtype) for a in arrs],
        scratch_shapes=[pltpu.SemaphoreType.DMA((7 * n,)), pltpu.SemaphoreType.DMA((7 * n,)),
                        pltpu.SemaphoreType.DMA((n,))],
        compiler_params=pltpu.CompilerParams(has_side_effects=True),
    )(*arrs)


def _adamw(w, m, v, parts, *, name, slab, tr=256):
    rows, cols = w.shape
    tr = _tile(rows, tr)
    n_p = len(parts)

    def body(slab_ref, *refs):
        w_ref, m_ref, v_ref = refs[:3]
        p_refs = refs[3:3 + n_p]
        g_ref, d_ref, nm_ref, nv_ref = refs[3 + n_p:]
        g = None
        for (_, lead), r in zip(parts, p_refs):
            for piece in ([r[...]] if lead is None else [r[i] for i in range(lead)]):
                g = piece.astype(F32) if g is None else g + piece.astype(F32)
        new_m = ADAM_B1 * m_ref[...] + (1.0 - ADAM_B1) * g
        new_v = ADAM_B2 * v_ref[...] + (1.0 - ADAM_B2) * (g * g)
        m_hat = new_m / (1.0 - ADAM_B1 ** ADAM_STEP)
        v_hat = new_v / (1.0 - ADAM_B2 ** ADAM_STEP)
        g_ref[...] = g
        d_ref[...] = -ADAM_LR * (m_hat / (jnp.sqrt(v_hat) + ADAM_EPS) + ADAM_WD * w_ref[...])
        nm_ref[...] = new_m
        nv_ref[...] = new_v

    tile = pl.BlockSpec((tr, cols), lambda i, s: (i, 0))
    p_specs = [pl.BlockSpec((None, tr, cols), lambda i, s: (s[0], i, 0)) if lead is None
               else pl.BlockSpec((lead, tr, cols), lambda i, s: (0, i, 0)) for _, lead in parts]
    return pl.pallas_call(
        body, name=name,
        grid_spec=pltpu.PrefetchScalarGridSpec(num_scalar_prefetch=1, grid=(rows // tr,),
                                               in_specs=[tile] * 3 + p_specs, out_specs=[tile] * 4),
        out_shape=[jax.ShapeDtypeStruct((rows, cols), F32)] * 4, compiler_params=_params(),
    )(jnp.reshape(slab, (1,)).astype(jnp.int32), w, m, v, *[a for a, _ in parts])


_PACK_TILE = SUBLANES * 128
_PACK_ROWS = 512


def _pack(arrs):
    flat = []
    for a in arrs:
        f = a.reshape(-1)
        flat.append(jnp.pad(f, (0, (-f.shape[0]) % _PACK_TILE)))
    total = sum(f.shape[0] for f in flat)
    flat.append(jnp.zeros(((-total) % (_PACK_ROWS * 128),), F32))
    return jnp.concatenate(flat).reshape(-1, 128)


def _unpack(buf, shapes):
    out, at = [], 0
    flat = buf.reshape(-1)
    for s in shapes:
        n = math.prod(s)
        out.append(flat[at:at + n].reshape(s))
        at += n + (-n) % _PACK_TILE
    return out


BIG = ("w_in", "w_glu", "w_out", "w_ffn_gate", "w_ffn_up", "w_ffn_down")
WEIGHTS = ("g_mix", "w_in", "q_gain", "k_gain", "rpb", "ssm_a_re", "ssm_a_im", "ssm_b_re", "ssm_b_im", "ssm_c_re",
           "ssm_c_im", "ssm_log_step", "ssm_d", "w_glu", "b_glu", "g_out_attn", "g_out_ssm", "w_out", "g_ffn",
           "w_ffn_gate", "w_ffn_up", "w_ffn_down")
SMALL = tuple(n for n in WEIGHTS if n not in BIG)
SMALL_PACKED = tuple(n for n in SMALL if n != "g_mix")
VECTORS = ("g_mix", "q_gain", "k_gain", "ssm_d", "b_glu", "g_out_attn", "g_out_ssm", "g_ffn")


def kernel(x, g_mix, w_in, q_gain, k_gain, rpb, ssm_a_re, ssm_a_im, ssm_b_re, ssm_b_im, ssm_c_re, ssm_c_im, ssm_log_step, ssm_d, w_glu, b_glu, g_out_attn, g_out_ssm, w_out, g_ffn, w_ffn_gate, w_ffn_up, w_ffn_down, loss_target, m_g_mix, m_w_in, m_q_gain, m_k_gain, m_rpb, m_ssm_a_re, m_ssm_a_im, m_ssm_b_re, m_ssm_b_im, m_ssm_c_re, m_ssm_c_im, m_ssm_log_step, m_ssm_d, m_w_glu, m_b_glu, m_g_out_attn, m_g_out_ssm, m_w_out, m_g_ffn, m_w_ffn_gate, m_w_ffn_up, m_w_ffn_down, v_g_mix, v_w_in, v_q_gain, v_k_gain, v_rpb, v_ssm_a_re, v_ssm_a_im, v_ssm_b_re, v_ssm_b_im, v_ssm_c_re, v_ssm_c_im, v_ssm_log_step, v_ssm_d, v_w_glu, v_b_glu, v_g_out_attn, v_g_out_ssm, v_w_out, v_g_ffn, v_w_ffn_gate, v_w_ffn_up, v_w_ffn_down):
    wts = dict(g_mix=g_mix, w_in=w_in, q_gain=q_gain, k_gain=k_gain, rpb=rpb, ssm_a_re=ssm_a_re, ssm_a_im=ssm_a_im,
               ssm_b_re=ssm_b_re, ssm_b_im=ssm_b_im, ssm_c_re=ssm_c_re, ssm_c_im=ssm_c_im, ssm_log_step=ssm_log_step,
               ssm_d=ssm_d, w_glu=w_glu, b_glu=b_glu, g_out_attn=g_out_attn, g_out_ssm=g_out_ssm, w_out=w_out,
               g_ffn=g_ffn, w_ffn_gate=w_ffn_gate, w_ffn_up=w_ffn_up, w_ffn_down=w_ffn_down)
    mom = dict(g_mix=m_g_mix, w_in=m_w_in, q_gain=m_q_gain, k_gain=m_k_gain, rpb=m_rpb, ssm_a_re=m_ssm_a_re,
               ssm_a_im=m_ssm_a_im, ssm_b_re=m_ssm_b_re, ssm_b_im=m_ssm_b_im, ssm_c_re=m_ssm_c_re,
               ssm_c_im=m_ssm_c_im, ssm_log_step=m_ssm_log_step, ssm_d=m_ssm_d, w_glu=m_w_glu, b_glu=m_b_glu,
               g_out_attn=m_g_out_attn, g_out_ssm=m_g_out_ssm, w_out=m_w_out, g_ffn=m_g_ffn,
               w_ffn_gate=m_w_ffn_gate, w_ffn_up=m_w_ffn_up, w_ffn_down=m_w_ffn_down)
    var = dict(g_mix=v_g_mix, w_in=v_w_in, q_gain=v_q_gain, k_gain=v_k_gain, rpb=v_rpb, ssm_a_re=v_ssm_a_re,
               ssm_a_im=v_ssm_a_im, ssm_b_re=v_ssm_b_re, ssm_b_im=v_ssm_b_im, ssm_c_re=v_ssm_c_re,
               ssm_c_im=v_ssm_c_im, ssm_log_step=v_ssm_log_step, ssm_d=v_ssm_d, w_glu=v_w_glu, b_glu=v_b_glu,
               g_out_attn=v_g_out_attn, g_out_ssm=v_g_out_ssm, w_out=v_w_out, g_ffn=v_g_ffn,
               w_ffn_gate=v_w_ffn_gate, w_ffn_up=v_w_ffn_up, w_ffn_down=v_w_ffn_down)
    ix, iy, ic = _mesh_place()
    me = _slab(ix, iy, ic)
    d_model = x.shape[-1]

    shard = {n: wts[n][0] for n in BIG}
    shard_b = {n: shard[n].astype(BF16) for n in BIG}
    p = {n: (wts[n][0].reshape(1, -1) if n in VECTORS else wts[n][0]) for n in SMALL}

    loss, grad_x, small_gathered, g_g_mix, reduced = _local_step(x[0], loss_target[0], p, shard_b)
    loss = lax.psum(loss, ("x", "y", "c"))
    out = {}
    for n in BIG:
        slabs, from_sibling, from_chips = reduced[n]
        rows, cols = slabs.shape[1:]
        res = _adamw(shard[n].reshape(rows, cols), mom[n][0].reshape(rows, cols), var[n][0].reshape(rows, cols),
                     [(slabs, None), (from_sibling, 1), (from_chips, 3)], name=f"adamw_{n}", slab=me)
        out[n] = [r.reshape(wts[n].shape) for r in res]

    order = list(SMALL_PACKED)
    shapes = [wts[n].shape for n in order]
    res = _adamw(_pack([wts[n] for n in order]), _pack([mom[n] for n in order]), _pack([var[n] for n in order]),
                 [(small_gathered, N_DEV)], name="adamw_small", slab=me)
    for kind, buf in enumerate(res):
        for n, a in zip(order, _unpack(buf, shapes)):
            out.setdefault(n, [None] * 4)[kind] = a
    as_rows = lambda a: a.reshape(-1, 128)
    g_mix_all = _all_gather([as_rows(g_g_mix)], name="gather_g_mix")[0]
    res = _adamw(as_rows(wts["g_mix"]), as_rows(mom["g_mix"]), as_rows(var["g_mix"]), [(g_mix_all, N_DEV)],
                 name="adamw_g_mix", slab=me)
    out["g_mix"] = [r.reshape(wts["g_mix"].shape) for r in res]

    return (loss, grad_x[None], *[out[n][0] for n in WEIGHTS], *[out[n][1] for n in WEIGHTS],
            *[out[n][2] for n in WEIGHTS], *[out[n][3] for n in WEIGHTS])
```

```python
import functools
import math

import jax
import jax.numpy as jnp
from jax import lax
from jax.experimental import pallas as pl
from jax.experimental.pallas import tpu as pltpu

F32 = jnp.float32
BF16 = jnp.bfloat16

N_DEV = 8
GRID_W = 64
WIN_H = 8
WIN_W = 16
HEAD_DIM = 64
HEADS_PER_GROUP = 4
GROUP_LANES = HEADS_PER_GROUP * HEAD_DIM
SSM_C = 16
SSM_P = 64
GROUPS_PER_TILE = 8
U_TILE = GROUPS_PER_TILE * SSM_C
ST_TILE = GROUPS_PER_TILE * SSM_P
SUBLANES = 8
RMS_EPS = 1e-6
NEG_INF = -1e30
A_RE_MAX = -1e-4
ADAM_LR, ADAM_B1, ADAM_B2, ADAM_EPS, ADAM_WD, ADAM_STEP = 0.001, 0.9, 0.999, 1e-08, 0.01, 10
VMEM_LIMIT_V7X = 56 * 1024 * 1024
MESH = pl.DeviceIdType.MESH

_NN = (((1,), (0,)), ((), ()))
_NT = (((1,), (1,)), ((), ()))
_TN = (((0,), (0,)), ((), ()))
_DIMS = {"nn": _NN, "nt": _NT, "tn": _TN}


def _params(**kw):
    return pltpu.CompilerParams(vmem_limit_bytes=VMEM_LIMIT_V7X, **kw)


def _dot(a, b, dims=_NN):
    return lax.dot_general(a, b, dims, preferred_element_type=F32)


def _mm(a, b, *, name, grid, a_spec, b_spec, o_spec, o_shape, dims, k_axis=None, res=None, out_dtype=F32,
        exact=False, second=None, ride=None, groups=1):
    dn = _DIMS[dims]
    nk = 1 if k_axis is None else grid[k_axis]
    acc_shape = tuple(d for d in o_spec.block_shape if d is not None)
    n_in = 2 + (2 if second is not None else 0)

    def body(*refs):
        a_ref, b_ref = refs[:2]
        r_ref = refs[n_in] if res is not None else None
        o_ref, acc = refs[-2:]
        def product(x_ref, y_ref):
            if groups == 1:
                return _dot(x_ref[...].astype(BF16), y_ref[...].astype(BF16), dn)
            total, width = None, x_ref.shape[-1] // groups
            for s in range(groups):
                x = x_ref[s] if len(x_ref.shape) == 3 else x_ref[:, s * width:(s + 1) * width]
                t = _dot(x.astype(BF16), y_ref[s].astype(BF16), dn)
                total = t if total is None else total + t
            return total

        if exact:
            p = lax.dot_general(a_ref[...], b_ref[...], dn, precision=lax.Precision.HIGHEST,
                                preferred_element_type=F32)
        else:
            p = product(a_ref, b_ref)
        if second is not None:
            p = p + product(refs[2], refs[3])

        def finish(v):
            if r_ref is not None:
                v = v + r_ref[...].astype(F32)
            o_ref[...] = v.astype(out_dtype)

        if nk == 1:
            finish(p)
        else:
            k = pl.program_id(k_axis)

            @pl.when(k == 0)
            def _():
                acc[...] = p

            @pl.when(k > 0)
            def _():
                acc[...] += p

            @pl.when(k == nk - 1)
            def _():
                finish(acc[...])

    ins = [a, b] + (list(second) if second is not None else []) + ([res] if res is not None else [])
    in_specs = [a_spec, b_spec] * (n_in // 2) + ([o_spec] if res is not None else [])
    first, last = _grid_ends(grid)
    (out,), rode = _call(
        body, name=name, grid=grid, in_specs=in_specs, out_specs=[o_spec],
        out_shape=[jax.ShapeDtypeStruct(o_shape, out_dtype)],
        scratch_shapes=[pltpu.VMEM(acc_shape if nk > 1 else (SUBLANES, 128), F32)],
        args=ins, ride=ride, first=first, last=last)
    return out if ride is None else (out, rode)


def _tile(n, want):
    if n <= want:
        return n
    t = want
    while n % t:
        t //= 2
    return t


def _mm_plain(a, b, dims, *, name, res=None, out_dtype=F32, tm=512, tn=512, tk=512, exact=False, ride=None):
    if dims == "nn":
        (m, k), n = a.shape, b.shape[1]
    elif dims == "nt":
        (m, k), n = a.shape, b.shape[0]
    else:
        (k, m), n = a.shape, b.shape[1]
    tm, tn = _tile(m, tm), _tile(n, tn)
    if dims == "tn":
        tk = _tile(k, tk)
        grid = (m // tm, n // tn, k // tk)
        a_spec = pl.BlockSpec((tk, tm), lambda i, j, kk: (kk, i))
        b_spec = pl.BlockSpec((tk, tn), lambda i, j, kk: (kk, j))
        o_spec = pl.BlockSpec((tm, tn), lambda i, j, kk: (i, j))
        return _mm(a, b, name=name, grid=grid, a_spec=a_spec, b_spec=b_spec, o_spec=o_spec, o_shape=(m, n),
                   dims=dims, k_axis=2, res=res, out_dtype=out_dtype)
    grid = (n // tn, m // tm)
    a_spec = pl.BlockSpec((tm, k), lambda j, i: (i, 0))
    if dims == "nn":
        b_spec = pl.BlockSpec((k, tn), lambda j, i: (0, j))
    else:
        b_spec = pl.BlockSpec((tn, k), lambda j, i: (j, 0))
    o_spec = pl.BlockSpec((tm, tn), lambda j, i: (i, j))
    return _mm(a, b, name=name, grid=grid, a_spec=a_spec, b_spec=b_spec, o_spec=o_spec, o_shape=(m, n), dims=dims,
               res=res, out_dtype=out_dtype, exact=exact, ride=ride)


def _rowwise(fn, tiled, bcast, outs, accs=(), *, name, tm=256, flipped=(), ride=None):
    m = tiled[0].shape[0]
    tm = _tile(m, tm)
    n_t, n_b, n_o, n_f = len(tiled), len(bcast), len(outs), len(flipped)

    def body(*refs):
        ins = [r[...] for r in refs[: n_t + n_b]]
        o_refs = refs[n_t + n_b: n_t + n_b + n_o]
        f_refs = refs[n_t + n_b + n_o: n_t + n_b + n_o + n_f]
        a_refs = refs[n_t + n_b + n_o + n_f:]
        res = fn(*ins)
        if not isinstance(res, (tuple, list)):
            res = (res,)
        for r, v in zip(o_refs, res[:n_o]):
            r[...] = v.astype(r.dtype)
        for r, v in zip(f_refs, res[n_o:n_o + n_f]):
            r[...] = v.astype(F32).T.astype(r.dtype)
        first = pl.program_id(0) == 0
        for r, v in zip(a_refs, res[n_o + n_f:]):
            s = jnp.sum(v, axis=0, keepdims=True)

            @pl.when(first)
            def _():
                r[...] = s

            @pl.when(jnp.logical_not(first))
            def _():
                r[...] += s

    in_specs = [pl.BlockSpec((tm, t.shape[1]), lambda i: (i, 0)) for t in tiled]
    in_specs += [pl.BlockSpec(b.shape, lambda i, nd=b.ndim: (0,) * nd) for b in bcast]
    out_specs = [pl.BlockSpec((tm, n), lambda i: (i, 0)) for n, _ in outs]
    out_specs += [pl.BlockSpec((n, tm), lambda i, per=m // tm // g: (i // per, i % per)) for n, _, g in flipped]
    out_specs += [pl.BlockSpec((1, n), lambda i: (0, 0)) for n in accs]
    out_shape = [jax.ShapeDtypeStruct((m, n), dt) for n, dt in outs]
    out_shape += [jax.ShapeDtypeStruct((g * n, m // g), dt) for n, dt, g in flipped]
    out_shape += [jax.ShapeDtypeStruct((1, n), F32) for n in accs]
    first, last = _grid_ends((m // tm,))
    res, rode = _call(body, name=name, grid=(m // tm,), in_specs=in_specs, out_specs=out_specs, out_shape=out_shape,
                      scratch_shapes=[], args=list(tiled) + list(bcast), ride=ride, first=first, last=last)
    return res if ride is None else (res, rode)


def _rstd(x):
    return lax.rsqrt(jnp.mean(x * x, axis=-1, keepdims=True) + RMS_EPS)


def _rms_bwd(dh, x, g):
    xh = x * _rstd(x)
    dxh = dh * g
    dx = _rstd(x) * (dxh - xh * jnp.mean(dxh * xh, axis=-1, keepdims=True))
    return dx, dh * xh


def _sigmoid(x):
    return 1.0 / (1.0 + jnp.exp(-x))


_GELU_K = math.sqrt(2.0 / math.pi)
_GELU_C = 0.044715


def _gelu(x):
    return 0.5 * x * (1.0 + jnp.tanh(_GELU_K * (x + _GELU_C * x * x * x)))


def _gelu_grad(x):
    th = jnp.tanh(_GELU_K * (x + _GELU_C * x * x * x))
    return 0.5 * (1.0 + th) + 0.5 * x * (1.0 - th * th) * _GELU_K * (1.0 + 3.0 * _GELU_C * x * x)


class _Exchange:
    def __init__(self, arrays, outs, n_sems, sends, recvs=None, local=None, aliases=None):
        self.arrays, self.outs, self.n_sems = list(arrays), list(outs), n_sems
        self.sends, self.local, self.aliases = sends, local, aliases or {}
        self.recvs = recvs or (lambda i, o: [(k, dst) for k, _, dst, _ in sends(i, o)])

    def __add__(self, other):
        na, no, ns = len(self.arrays), len(self.outs), self.n_sems
        mine = lambda f: (lambda i, o: f(i[:na], o[:no]))
        shift = lambda f, at: (lambda i, o: [(k + ns,) + tuple(rest) for k, *rest in f(i[na:], o[no:])]) if at else None
        both = lambda f, g: (lambda i, o: f(i, o) + g(i, o))
        local = None
        if self.local or other.local:
            la = mine(self.local) if self.local else (lambda i, o: [])
            lb = (lambda i, o: other.local(i[na:], o[no:])) if other.local else (lambda i, o: [])
            local = both(la, lb)
        aliases = dict(self.aliases)
        aliases.update({na + i: no + o for i, o in other.aliases.items()})
        return _Exchange(self.arrays + other.arrays, self.outs + other.outs, ns + other.n_sems,
                         both(mine(self.sends), shift(other.sends, True)),
                         both(mine(self.recvs), shift(other.recvs, True)), local, aliases)

    def descriptors(self, in_refs, out_refs, send_sems, recv_sems, local_sems):
        me = _mesh_place()
        remote = lambda k, src, dst, to: pltpu.make_async_remote_copy(
            src_ref=src, dst_ref=dst, send_sem=send_sems.at[k], recv_sem=recv_sems.at[k], device_id=to,
            device_id_type=MESH)
        out = [remote(*s) for s in self.sends(in_refs, out_refs)]
        arrive = [remote(k, dst, dst, me) for k, dst in self.recvs(in_refs, out_refs)]
        own = [pltpu.make_async_copy(src, dst, local_sems.at[i])
               for i, (src, dst) in enumerate(self.local(in_refs, out_refs) if self.local else [])]
        return out, arrive, own

    def start(self, *refs):
        out, _, own = self.descriptors(*refs)
        for cp in own + out:
            cp.start()

    def finish(self, *refs):
        out, arrive, own = self.descriptors(*refs)
        for cp in arrive:
            cp.wait_recv()
        for cp in out:
            cp.wait_send()
        for cp in own:
            cp.wait()


def _call(body, *, name, grid, in_specs, out_specs, out_shape, scratch_shapes, args, ride=None, first=None, last=None):
    if ride is None:
        res = pl.pallas_call(body, name=name, grid=grid, in_specs=in_specs, out_specs=out_specs, out_shape=out_shape,
                             scratch_shapes=scratch_shapes, compiler_params=_params())(*args)
        return list(res), []
    n_in, n_out, n_scr = len(in_specs), len(out_specs), len(scratch_shapes)
    r_in, r_out = len(ride.arrays), len(ride.outs)

    def wrapped(*refs):
        ins, refs = refs[:n_in], refs[n_in:]
        x_in, refs = refs[:r_in], refs[r_in:]
        outs, refs = refs[:n_out], refs[n_out:]
        x_out, refs = refs[:r_out], refs[r_out:]
        scr, sems = refs[:n_scr], refs[n_scr:]

        @pl.when(first())
        def _():
            ride.start(x_in, x_out, *sems)

        body(*ins, *outs, *scr)

        @pl.when(last())
        def _():
            ride.finish(x_in, x_out, *sems)

    n_local = max(1, len(ride.arrays))
    res = pl.pallas_call(
        wrapped, name=name, grid=grid, in_specs=list(in_specs) + [_ANY] * r_in,
        out_specs=list(out_specs) + [_ANY] * r_out, out_shape=list(out_shape) + ride.outs,
        scratch_shapes=list(scratch_shapes) + [pltpu.SemaphoreType.DMA((ride.n_sems,)),
                                               pltpu.SemaphoreType.DMA((ride.n_sems,)),
                                               pltpu.SemaphoreType.DMA((n_local,))],
        input_output_aliases={n_in + i: n_out + o for i, o in ride.aliases.items()},
        compiler_params=_params(has_side_effects=True),
    )(*args, *ride.arrays)
    return list(res[:n_out]), list(res[n_out:])


def _gather_first(shards):
    def sends(i, o):
        x, y, c = _mesh_place()
        peers = [(x, y, 1 - c)] + [(px, py, c) for px, py in _chips(x, y)[1:]]
        return [(4 * w + k, i[w], o[w].at[_slab(x, y, c)], to) for w in range(len(i)) for k, to in enumerate(peers)]

    def recvs(i, o):
        x, y, c = _mesh_place()
        peers = [(x, y, 1 - c)] + [(px, py, c) for px, py in _chips(x, y)[1:]]
        return [(4 * w + k, o[w].at[_slab(*peer)]) for w in range(len(i)) for k, peer in enumerate(peers)]

    def local(i, o):
        return [(i[w], o[w].at[_slab(*_mesh_place())]) for w in range(len(i))]

    outs = [jax.ShapeDtypeStruct((N_DEV,) + a.shape, a.dtype) for a in shards]
    return _Exchange(shards, outs, 4 * len(shards), sends, recvs, local)


def _gather_second(gathered):
    def sends(i, o):
        x, y, c = _mesh_place()
        return [(3 * w + j, o[w].at[_slab(px, py, c)], o[w].at[_slab(px, py, c)], (x, y, 1 - c))
                for w in range(len(o)) for j, (px, py) in enumerate(_chips(x, y)[1:])]

    def recvs(i, o):
        x, y, c = _mesh_place()
        return [(3 * w + j, o[w].at[_slab(px, py, 1 - c)])
                for w in range(len(o)) for j, (px, py) in enumerate(_chips(x, y)[1:])]

    outs = [jax.ShapeDtypeStruct(a.shape, a.dtype) for a in gathered]
    return _Exchange(gathered, outs, 3 * len(gathered), sends, recvs, aliases={w: w for w in range(len(gathered))})


def _reduce_sibling(slabs):
    def sends(i, o):
        x, y, c = _mesh_place()
        return [(4 * w + k, i[w].at[_slab(px, py, 1 - c)], o[w].at[k], (x, y, 1 - c))
                for w in range(len(i)) for k, (px, py) in enumerate(_chips(x, y))]

    outs = [jax.ShapeDtypeStruct((4,) + a.shape[1:], a.dtype) for a in slabs]
    return _Exchange(slabs, outs, 4 * len(slabs), sends)


def _reduce_chips(partials):
    def sends(i, o):
        x, y, c = _mesh_place()
        return [(3 * w + k, i[w].at[k], o[w].at[k], (px, py, c))
                for w in range(len(i)) for k, (px, py) in enumerate(_chips(x, y)[1:])]

    outs = [jax.ShapeDtypeStruct(a.shape, a.dtype) for a in partials]
    return _Exchange(partials, outs, 3 * len(partials), sends)


def _head_masks():
    lane_head = lax.broadcasted_iota(jnp.int32, (1, GROUP_LANES), 1) // HEAD_DIM
    return [(lane_head == h).astype(F32) for h in range(HEADS_PER_GROUP)]


def _head_block_diag():
    r = lax.broadcasted_iota(jnp.int32, (GROUP_LANES, GROUP_LANES), 0) // HEAD_DIM
    c = lax.broadcasted_iota(jnp.int32, (GROUP_LANES, GROUP_LANES), 1) // HEAD_DIM
    return (r == c).astype(BF16)


def _head_mean(x, bd):
    hi = x.astype(BF16)
    lo = (x - hi.astype(F32)).astype(BF16)
    return (_dot(hi, bd) + _dot(lo, bd)) * (1.0 / HEAD_DIM)


def _stack_heads(x, masks):
    return jnp.concatenate([x * m for m in masks], axis=0)


def _unstack_heads(xs, masks):
    out = xs[0:GRID_W] * masks[0]
    for h in range(1, HEADS_PER_GROUP):
        out = out + xs[h * GRID_W:(h + 1) * GRID_W] * masks[h]
    return out


def _row_start(r, rows):
    return jnp.clip(r - WIN_H // 2, 0, rows - WIN_H)


ROWS_PER_STEP = 2


def _attn_common_specs(seq, n_hg, rows):
    win_keys = WIN_H * GRID_W
    q_spec = pl.BlockSpec((ROWS_PER_STEP * GRID_W, GROUP_LANES), lambda g, r: (r, g))
    k_spec = pl.BlockSpec((seq, GROUP_LANES), lambda g, r: (0, n_hg + g))
    v_spec = pl.BlockSpec((seq, GROUP_LANES), lambda g, r: (0, 2 * n_hg + g))
    gain_spec = pl.BlockSpec((1, GROUP_LANES), lambda g, r: (0, 0))

    def variant(r):
        return _row_start(r, rows) - r + (WIN_H - 1)

    bias_specs = [pl.BlockSpec((None, None, HEADS_PER_GROUP, GRID_W, win_keys),
                               lambda g, r, h=h: (g, variant(ROWS_PER_STEP * r + h), 0, 0, 0))
                  for h in range(ROWS_PER_STEP)]
    return q_spec, k_spec, v_spec, gain_spec, bias_specs, variant


def _attn_prepare_kv(k_ref, v_ref, kg, kn_scr, vb_scr, bd, seq):
    chunk = _tile(seq, 512)

    def step(c, carry):
        rows = pl.ds(pl.multiple_of(c * chunk, chunk), chunk)
        k = k_ref[rows, :]
        kn_scr[rows, :] = (k * lax.rsqrt(_head_mean(k * k, bd) + RMS_EPS) * kg).astype(BF16)
        vb_scr[rows, :] = v_ref[rows, :].astype(BF16)
        return carry

    lax.fori_loop(0, seq // chunk, step, 0)


def _attn_probs(qn, kw, bias, masks):
    qs = _stack_heads(qn, masks).astype(BF16)
    s = _dot(qs, kw, _NT) * (1.0 / math.sqrt(HEAD_DIM)) + bias
    m = jnp.max(s, axis=-1, keepdims=True)
    p = jnp.exp(s - m)
    return qs, p * (1.0 / jnp.sum(p, axis=-1, keepdims=True))


def _grid_ends(grid):
    first = lambda: functools.reduce(jnp.logical_and, [pl.program_id(a) == 0 for a in range(len(grid))])
    last = lambda: functools.reduce(jnp.logical_and, [pl.program_id(a) == n - 1 for a, n in enumerate(grid)])
    return first, last


def _attn_fwd(z, qg4, kg4, btab, ride=None):
    seq = z.shape[0]
    a_width = btab.shape[0] * GROUP_LANES
    n_hg, rows, win_keys = btab.shape[0], seq // GRID_W, WIN_H * GRID_W
    q_spec, k_spec, v_spec, gain_spec, bias_specs, _ = _attn_common_specs(seq, n_hg, rows)
    grid = (n_hg, rows // ROWS_PER_STEP)

    def body(q_ref, k_ref, v_ref, qg_ref, kg_ref, *rest):
        b_refs, (o_ref, kn_scr, vb_scr) = rest[:ROWS_PER_STEP], rest[ROWS_PER_STEP:]
        bd, masks = _head_block_diag(), _head_masks()

        @pl.when(pl.program_id(1) == 0)
        def _():
            _attn_prepare_kv(k_ref, v_ref, kg_ref[...], kn_scr, vb_scr, bd, seq)

        for h in range(ROWS_PER_STEP):
            r = ROWS_PER_STEP * pl.program_id(1) + h
            mine = slice(h * GRID_W, (h + 1) * GRID_W)
            win = pl.ds(pl.multiple_of(_row_start(r, rows) * GRID_W, GRID_W), win_keys)
            q = q_ref[mine, :]
            qn = q * lax.rsqrt(_head_mean(q * q, bd) + RMS_EPS) * qg_ref[...]
            bias = b_refs[h][...].reshape(HEADS_PER_GROUP * GRID_W, win_keys)
            _, p = _attn_probs(qn, kn_scr[win, :], bias, masks)
            o_ref[mine, :] = _unstack_heads(_dot(p.astype(BF16), vb_scr[win, :]), masks)

    first, last = _grid_ends(grid)
    (ya,), rode = _call(
        body, name="attn_fwd", grid=grid,
        in_specs=[q_spec, k_spec, v_spec, gain_spec, gain_spec] + bias_specs,
        out_specs=[pl.BlockSpec((ROWS_PER_STEP * GRID_W, GROUP_LANES), lambda g, r: (r, g))],
        out_shape=[jax.ShapeDtypeStruct((seq, a_width), F32)],
        scratch_shapes=[pltpu.VMEM((seq, GROUP_LANES), BF16), pltpu.VMEM((seq, GROUP_LANES), BF16)],
        args=(z, z, z, qg4, kg4) + (btab,) * ROWS_PER_STEP, ride=ride, first=first, last=last)
    return ya, rode


def _attn_bwd(z, d_out, qg4, kg4, btab, ride=None):
    seq = z.shape[0]
    n_hg, rows, win_keys = btab.shape[0], seq // GRID_W, WIN_H * GRID_W
    a_width = n_hg * GROUP_LANES
    q_spec, k_spec, v_spec, gain_spec, bias_specs, variant = _attn_common_specs(seq, n_hg, rows)
    scale = 1.0 / math.sqrt(HEAD_DIM)
    grid = (n_hg, rows // ROWS_PER_STEP)

    def body(q_ref, k_ref, v_ref, do_ref, qg_ref, kg_ref, *rest):
        b_refs, rest = rest[:ROWS_PER_STEP], rest[ROWS_PER_STEP:]
        dq_ref, dk_out, dv_out, db_ref, dqg_ref, dkg_ref, kn_scr, vb_scr, dk_ref, dv_ref = rest
        bd, masks = _head_block_diag(), _head_masks()

        @pl.when(pl.program_id(1) == 0)
        def _():
            _attn_prepare_kv(k_ref, v_ref, kg_ref[...], kn_scr, vb_scr, bd, seq)
            dk_ref[...] = jnp.zeros_like(dk_ref)
            dv_ref[...] = jnp.zeros_like(dv_ref)
            db_ref[...] = jnp.zeros_like(db_ref)
            dqg_ref[...] = jnp.zeros_like(dqg_ref)

        qg = qg_ref[...]
        for h in range(ROWS_PER_STEP):
            r = ROWS_PER_STEP * pl.program_id(1) + h
            mine = slice(h * GRID_W, (h + 1) * GRID_W)
            win = pl.ds(pl.multiple_of(_row_start(r, rows) * GRID_W, GRID_W), win_keys)
            q = q_ref[mine, :]
            rq = lax.rsqrt(_head_mean(q * q, bd) + RMS_EPS)
            qh = q * rq
            kw, vw = kn_scr[win, :], vb_scr[win, :]
            bias = b_refs[h][...].reshape(HEADS_PER_GROUP * GRID_W, win_keys)
            qs, p = _attn_probs(qh * qg, kw, bias, masks)
            dos = _stack_heads(do_ref[mine, :], masks).astype(BF16)
            dp = _dot(dos, vw, _NT)
            ds = p * (dp - jnp.sum(p * dp, axis=-1, keepdims=True))
            db_ref[variant(r)] += ds.reshape(HEADS_PER_GROUP, GRID_W, win_keys)
            dsb = ds.astype(BF16)
            dqn = _unstack_heads(_dot(dsb, kw), masks) * scale
            dk_ref[win, :] += _dot(dsb, qs, _TN) * scale
            dv_ref[win, :] += _dot(p.astype(BF16), dos, _TN)
            dqg_ref[...] += jnp.sum(dqn * qh, axis=0, keepdims=True)
            dqh = dqn * qg
            dq_ref[mine, :] = (rq * (dqh - qh * _head_mean(dqh * qh, bd))).astype(BF16)

        @pl.when(pl.program_id(1) == grid[1] - 1)
        def _():
            chunk = _tile(seq, 512)
            kg = kg_ref[...]

            def step(c, dkg):
                rws = pl.ds(pl.multiple_of(c * chunk, chunk), chunk)
                k = k_ref[rws, :]
                rk = lax.rsqrt(_head_mean(k * k, bd) + RMS_EPS)
                kh = k * rk
                dkn = dk_ref[rws, :]
                dkh = dkn * kg
                dk_out[rws, :] = (rk * (dkh - kh * _head_mean(dkh * kh, bd))).astype(BF16)
                dv_out[rws, :] = dv_ref[rws, :].astype(BF16)
                return dkg + jnp.sum(dkn * kh, axis=0, keepdims=True)

            dkg_ref[...] = lax.fori_loop(0, seq // chunk, step, jnp.zeros((1, GROUP_LANES), F32))

    col_spec = pl.BlockSpec((seq, GROUP_LANES), lambda g, r: (0, g))
    gsum_spec = pl.BlockSpec((None, 1, GROUP_LANES), lambda g, r: (g, 0, 0))
    first, last = _grid_ends(grid)
    rows_spec = pl.BlockSpec((ROWS_PER_STEP * GRID_W, GROUP_LANES), lambda g, r: (r, g))
    return _call(
        body, name="attn_bwd", grid=grid,
        in_specs=[q_spec, k_spec, v_spec, rows_spec, gain_spec, gain_spec] + bias_specs,
        out_specs=[rows_spec, col_spec, col_spec,
                   pl.BlockSpec((None, WIN_H, HEADS_PER_GROUP, GRID_W, win_keys), lambda g, r: (g, 0, 0, 0, 0)),
                   gsum_spec, gsum_spec],
        out_shape=[jax.ShapeDtypeStruct((seq, a_width), BF16)] * 3
        + [jax.ShapeDtypeStruct(btab.shape, F32)]
        + [jax.ShapeDtypeStruct((n_hg, 1, GROUP_LANES), F32)] * 2,
        scratch_shapes=[pltpu.VMEM((seq, GROUP_LANES), BF16), pltpu.VMEM((seq, GROUP_LANES), BF16),
                        pltpu.VMEM((seq, GROUP_LANES), F32), pltpu.VMEM((seq, GROUP_LANES), F32)],
        args=(z, z, z, d_out, qg4, kg4) + (btab,) * ROWS_PER_STEP, ride=ride, first=first, last=last)


DC_SLOTS = 2 * WIN_W


def _bias_spread(c):
    shape = (WIN_H * DC_SLOTS, WIN_H * GRID_W)
    rows = lax.broadcasted_iota(jnp.int32, shape, 0)
    cols = lax.broadcasted_iota(jnp.int32, shape, 1)
    row_i, row_d = rows // DC_SLOTS, rows % DC_SLOTS
    col_i, kc = cols // GRID_W, cols % GRID_W
    col_start = jnp.clip(c - WIN_W // 2, 0, GRID_W - WIN_W)
    col_in = (kc >= col_start) & (kc < col_start + WIN_W)
    hit = (row_i == col_i) & (row_d == kc - c + (WIN_W - 1)) & col_in
    mask_slot = (row_i == 0) & (row_d == DC_SLOTS - 1) & jnp.logical_not(col_in)
    return jnp.where(hit, 1.0, jnp.where(mask_slot, NEG_INF, 0.0)).astype(F32)


def _bias_table(rpb):
    n_h = rpb.shape[0]
    n_hg = n_h // HEADS_PER_GROUP
    rows = jnp.stack([rpb[:, v:v + WIN_H] for v in range(WIN_H)], axis=1)
    rows = jnp.pad(rows, ((0, 0), (0, 0), (0, 0), (0, DC_SLOTS - rows.shape[-1])))
    rows = rows.at[:, :, 0, DC_SLOTS - 1].set(1.0)
    rows = rows.reshape(n_hg, HEADS_PER_GROUP, WIN_H, WIN_H * DC_SLOTS).transpose(0, 2, 1, 3)
    n_rows, win_keys, depth = n_h * WIN_H, WIN_H * GRID_W, WIN_H * DC_SLOTS

    def body(r_ref, o_ref):
        for cc in range(SUBLANES):
            spread = _bias_spread(pl.program_id(0) * SUBLANES + cc)
            o_ref[cc] = lax.dot_general(r_ref[...], spread, _NN, precision=lax.Precision.HIGHEST,
                                        preferred_element_type=F32)

    tab = pl.pallas_call(
        body, name="rpb_spread", grid=(GRID_W // SUBLANES,),
        in_specs=[pl.BlockSpec((n_rows, depth), lambda c: (0, 0))],
        out_specs=pl.BlockSpec((SUBLANES, n_rows, win_keys), lambda c: (c, 0, 0)),
        out_shape=jax.ShapeDtypeStruct((GRID_W, n_rows, win_keys), F32), compiler_params=_params(),
    )(rows.reshape(n_rows, depth))
    return tab.transpose(1, 0, 2).reshape(n_hg, WIN_H, HEADS_PER_GROUP, GRID_W, win_keys)


def _bias_grad(dtab, n_h):
    n_hg = n_h // HEADS_PER_GROUP
    n_rows, win_keys, depth = n_h * WIN_H, WIN_H * GRID_W, WIN_H * DC_SLOTS

    def body(d_ref, o_ref):
        total = None
        for cc in range(SUBLANES):
            spread = _bias_spread(pl.program_id(0) * SUBLANES + cc).astype(BF16)
            t = _dot(d_ref[cc].astype(BF16), spread, _NT)
            total = t if total is None else total + t

        @pl.when(pl.program_id(0) == 0)
        def _():
            o_ref[...] = total

        @pl.when(pl.program_id(0) > 0)
        def _():
            o_ref[...] += total

    d_rows = pl.pallas_call(
        body, name="rpb_diag_sum", grid=(GRID_W // SUBLANES,),
        in_specs=[pl.BlockSpec((SUBLANES, n_rows, win_keys), lambda c: (c, 0, 0))],
        out_specs=pl.BlockSpec((n_rows, depth), lambda c: (0, 0)),
        out_shape=jax.ShapeDtypeStruct((n_rows, depth), F32), compiler_params=_params(),
    )(dtab.reshape(n_rows, GRID_W, win_keys).transpose(1, 0, 2))
    d_rows = d_rows.reshape(n_hg, WIN_H, HEADS_PER_GROUP, WIN_H, DC_SLOTS).transpose(0, 2, 1, 3, 4)
    d_rows = d_rows.reshape(n_h, WIN_H, WIN_H, DC_SLOTS)[..., : 2 * WIN_W - 1]
    out = jnp.zeros((n_h, 2 * WIN_H - 1, 2 * WIN_W - 1), F32)
    for v in range(WIN_H):
        out = out.at[:, v:v + WIN_H].add(d_rows[:, v])
    return out


def _cmul(ar, ai, br, bi):
    return ar * br - ai * bi, ar * bi + ai * br


def _s5_discretize(a_re, a_im, dt, b_re, b_im, ride=None):
    c = b_re.shape[1]

    def fn(are, aim, dt_, bre, bim):
        lr, li = jnp.minimum(are, A_RE_MAX), aim
        mag = jnp.exp(lr * dt_)
        l1r, l1i = mag * jnp.cos(li * dt_), mag * jnp.sin(li * dt_)
        den = lr * lr + li * li
        nr, ni = l1r - 1.0, l1i
        cr, ci = (nr * lr + ni * li) / den, (ni * lr - nr * li) / den
        bbr, bbi = _cmul(cr, ci, bre, bim)
        shape = (are.shape[0], SUBLANES)
        lane = lax.broadcasted_iota(jnp.int32, shape, 1)
        pr, pi = l1r, l1i
        acc_r, acc_i = jnp.zeros(shape, F32), jnp.zeros(shape, F32)
        for k in range(SUBLANES):
            acc_r = jnp.where(lane == k, pr, acc_r)
            acc_i = jnp.where(lane == k, pi, acc_i)
            pr, pi = _cmul(pr, pi, l1r, l1i)
        return acc_r, acc_i, cr, ci, bbr, bbi

    return _rowwise(fn, [a_re, a_im, dt, b_re, b_im], [],
                    [(SUBLANES, F32), (SUBLANES, F32), (1, F32), (1, F32), (c, F32), (c, F32)],
                    name="s5_discretize", tm=1024, ride=ride)


def _s5_param_grads(a_re, a_im, dt, b_re, b_im, l1r, l1i, cr, ci, bbr, bbi, r_re, r_im, gb_re, gb_im):
    c = b_re.shape[1]

    def fn(are, aim, dt_, bre, bim, l1r_, l1i_, cr_, ci_, bbr_, bbi_, rr, ri, gbr, gbi):
        lr, li = jnp.minimum(are, A_RE_MAX), aim
        den = lr * lr + li * li
        dbr, dbi = _cmul(cr_, -ci_, gbr, gbi)
        gcr, gci = _cmul(bre, -bim, gbr, gbi)
        gcr, gci = jnp.sum(gcr, axis=1, keepdims=True), jnp.sum(gci, axis=1, keepdims=True)
        qr, qi = _cmul(bbr_, -bbi_, gbr, gbi)
        qr = rr - jnp.sum(qr, axis=1, keepdims=True)
        qi = ri - jnp.sum(qi, axis=1, keepdims=True)
        tr, ti = _cmul(gcr, gci, lr / den, li / den)
        ur, ui = _cmul(l1r_, -l1i_, tr, ti)
        gwr, gwi = qr + ur, qi + ui
        vr, vi = _cmul(cr_, -ci_, lr / den, li / den)
        vr, vi = _cmul(gcr, gci, vr, vi)
        glr, gli = dt_ * gwr - vr, dt_ * gwi - vi
        return jnp.where(are < A_RE_MAX, glr, 0.0), gli, (gwr * lr + gwi * li) * dt_, dbr, dbi

    return _rowwise(fn, [a_re, a_im, dt, b_re, b_im, l1r, l1i, cr, ci, bbr, bbi, r_re, r_im, gb_re, gb_im], [],
                    [(1, F32), (1, F32), (1, F32), (c, F32), (c, F32)], name="s5_param_grads", tm=1024)


def _s5_scan(v, win_re, win_im, tabs, wo_re, wo_im, *, reverse, name, t_chunk=256, ride=None):
    seq, width = v.shape
    n_tiles, n_state = width // U_TILE, width * (SSM_P // SSM_C)
    t_chunk = _tile(seq, t_chunk)
    n_chunks, n_blk = seq // t_chunk, t_chunk // SUBLANES
    last_row = 0 if reverse else SUBLANES - 1

    def chunk_of(j):
        return (n_chunks - 1 - j) if reverse else j

    def body(v_ref, wir_ref, wii_ref, tab_ref, wor_ref, woi_ref, sr_ref, si_ref, y_ref, carry, wr, wi):
        @pl.when(pl.program_id(0) == 0)
        def _():
            carry[...] = jnp.zeros_like(carry)

        for jt in range(n_tiles):
            ls = slice(jt * ST_TILE, (jt + 1) * ST_TILE)
            us = slice(jt * U_TILE, (jt + 1) * U_TILE)
            vj = v_ref[:, us].astype(BF16)
            consts = [tab_ref[k, :, ls] for k in range(8)]
            xr = _dot(vj, wir_ref[jt]).reshape(n_blk, SUBLANES, ST_TILE)
            xi = _dot(vj, wii_ref[jt]).reshape(n_blk, SUBLANES, ST_TILE)
            for s, k in enumerate((1, 2, 4)):
                sh = (SUBLANES - k) if reverse else k
                tr, ti = pltpu.roll(xr, sh, 1), pltpu.roll(xi, sh, 1)
                lr, li = consts[2 * s][None], consts[2 * s + 1][None]
                xr, xi = xr + lr * tr - li * ti, xi + lr * ti + li * tr
            wr[...] = xr.reshape(t_chunk, ST_TILE)
            wi[...] = xi.reshape(t_chunk, ST_TILE)

            def blk(b, c, consts=consts):
                cr, ci = c
                bb = (n_blk - 1 - b) if reverse else b
                rows = pl.ds(pl.multiple_of(bb * SUBLANES, SUBLANES), SUBLANES)
                lr, li = consts[6], consts[7]
                xr = wr[rows, :] + lr * cr - li * ci
                xi = wi[rows, :] + lr * ci + li * cr
                wr[rows, :], wi[rows, :] = xr, xi
                shape = (SUBLANES, ST_TILE)
                return (jnp.broadcast_to(xr[last_row:last_row + 1], shape),
                        jnp.broadcast_to(xi[last_row:last_row + 1], shape))

            cr, ci = lax.fori_loop(0, n_blk, blk, (carry[0, :, ls], carry[1, :, ls]), unroll=2)
            carry[0, :, ls], carry[1, :, ls] = cr, ci
            xr_b, xi_b = wr[...].astype(BF16), wi[...].astype(BF16)
            sr_ref[:, ls], si_ref[:, ls] = xr_b, xi_b
            y_ref[:, us] = _dot(xr_b, wor_ref[jt]) + _dot(xi_b, woi_ref[jt])

    whole = lambda a: pl.BlockSpec(a.shape, lambda j, nd=a.ndim: (0,) * nd)
    st_spec = pl.BlockSpec((t_chunk, n_state), lambda j: (chunk_of(j), 0))
    v_spec = pl.BlockSpec((t_chunk, width), lambda j: (chunk_of(j), 0))
    first, last = _grid_ends((n_chunks,))
    return _call(
        body, name=name, grid=(n_chunks,),
        in_specs=[v_spec, whole(win_re), whole(win_im), whole(tabs), whole(wo_re), whole(wo_im)],
        out_specs=[st_spec, st_spec, v_spec],
        out_shape=[jax.ShapeDtypeStruct((seq, n_state), BF16)] * 2 + [jax.ShapeDtypeStruct((seq, width), F32)],
        scratch_shapes=[pltpu.VMEM((2, SUBLANES, n_state), F32), pltpu.VMEM((t_chunk, ST_TILE), F32),
                        pltpu.VMEM((t_chunk, ST_TILE), F32)],
        args=(v, win_re, win_im, tabs, wo_re, wo_im), ride=ride, first=first, last=last)


def _s5_reduce(x_re, x_im, a_re, a_im, u, dy, *, name, t_chunk=512, ride=None):
    seq, n_state = x_re.shape
    width = u.shape[1]
    n_tiles = width // U_TILE
    t_chunk = _tile(seq, t_chunk)

    def body(xr_ref, xi_ref, ar_ref, ai_ref, u_ref, dy_ref, rr_ref, ri_ref, gbr_ref, gbi_ref, gcr_ref, gci_ref):
        xrb, xib, arb, aib = xr_ref[...], xi_ref[...], ar_ref[...], ai_ref[...]
        xr, xi, ar, ai = xrb.astype(F32), xib.astype(F32), arb.astype(F32), aib.astype(F32)
        ub, dyb = u_ref[...].astype(BF16), dy_ref[...].astype(BF16)
        parts = (jnp.sum(ar * xr + ai * xi, axis=0, keepdims=True), jnp.sum(ai * xr - ar * xi, axis=0, keepdims=True),
                 _dot(arb, ub, _TN), _dot(aib, ub, _TN), _dot(xrb, dyb, _TN), _dot(xib, dyb, _TN))
        first = pl.program_id(1) == 0
        for ref, val in zip((rr_ref, ri_ref, gbr_ref, gbi_ref, gcr_ref, gci_ref), parts):
            @pl.when(first)
            def _():
                ref[...] = val

            @pl.when(jnp.logical_not(first))
            def _():
                ref[...] += val

    st_spec = pl.BlockSpec((t_chunk, ST_TILE), lambda j, t: (t, j))
    u_spec = pl.BlockSpec((t_chunk, U_TILE), lambda j, t: (t, j))
    r_spec = pl.BlockSpec((1, ST_TILE), lambda j, t: (0, j))
    g_spec = pl.BlockSpec((None, ST_TILE, U_TILE), lambda j, t: (j, 0, 0))
    first, last = _grid_ends((n_tiles, seq // t_chunk))
    return _call(
        body, name=name, grid=(n_tiles, seq // t_chunk),
        in_specs=[st_spec] * 4 + [u_spec] * 2,
        out_specs=[r_spec, r_spec] + [g_spec] * 4,
        out_shape=[jax.ShapeDtypeStruct((1, n_state), F32)] * 2
        + [jax.ShapeDtypeStruct((n_tiles, ST_TILE, U_TILE), F32)] * 4,
        scratch_shapes=[], args=(x_re, x_im, a_re, a_im, u, dy), ride=ride, first=first, last=last)


def _block_diag_in(ms):
    m = jnp.stack(ms)
    n, g, c, p = m.shape
    m5 = m.reshape(n, g // GROUPS_PER_TILE, GROUPS_PER_TILE, c, p)
    eye = jnp.eye(GROUPS_PER_TILE, dtype=m.dtype)
    out = m5[:, :, :, :, None, :] * eye[None, None, :, None, :, None]
    return out.astype(BF16).reshape(n, g // GROUPS_PER_TILE, GROUPS_PER_TILE * c, GROUPS_PER_TILE * p)


def _block_diag_take(m, c, p):
    t = m.shape[0]
    m5 = m.reshape(t, GROUPS_PER_TILE, p, GROUPS_PER_TILE, c)
    idx = jnp.arange(GROUPS_PER_TILE)
    return m5[:, idx, :, idx, :].transpose(1, 0, 2, 3).reshape(t * GROUPS_PER_TILE, p, c)


def _scan_tables(pw_re, pw_im, reverse):
    row = jnp.arange(SUBLANES)[:, None]
    tabs = []
    for k in (1, 2, 4):
        keep = (row <= SUBLANES - 1 - k) if reverse else (row >= k)
        tabs += [jnp.where(keep, pw_re[k - 1][None, :], 0.0), jnp.where(keep, pw_im[k - 1][None, :], 0.0)]
    order = jnp.arange(SUBLANES)[::-1] if reverse else jnp.arange(SUBLANES)
    tabs += [pw_re[order], pw_im[order]]
    return jnp.stack(tabs)


def _partial_sums(slabs, from_sibling, names):
    x, y, c = _mesh_place()
    theirs = jnp.stack([_slab(px, py, c) for px, py in _chips(x, y)[1:]]).astype(jnp.int32)
    out = []
    for s, f, n in zip(slabs, from_sibling, names):
        rows, cols = s.shape[1:]
        tr = _tile(rows, 512)

        def body(idx_ref, a_ref, b_ref, o_ref):
            o_ref[...] = (a_ref[...] + b_ref[...]).astype(BF16)

        out.append(pl.pallas_call(
            body, name=f"reduce_add_{n}",
            grid_spec=pltpu.PrefetchScalarGridSpec(
                num_scalar_prefetch=1, grid=(3, rows // tr),
                in_specs=[pl.BlockSpec((None, tr, cols), lambda k, i, idx: (idx[k], i, 0)),
                          pl.BlockSpec((None, tr, cols), lambda k, i, idx: (k + 1, i, 0))],
                out_specs=pl.BlockSpec((None, tr, cols), lambda k, i, idx: (k, i, 0))),
            out_shape=jax.ShapeDtypeStruct((3, rows, cols), BF16), compiler_params=_params(),
        )(theirs, s, f))
    return out


def _local_step(x, target, p, shards):
    seq, d_model = x.shape
    a_width = p["g_out_attn"].shape[-1]
    s_width = p["g_out_ssm"].shape[-1]
    n_heads = a_width // HEAD_DIM
    n_hg = n_heads // HEADS_PER_GROUP
    n_groups = s_width // SSM_C
    n_sh, in_sh = N_DEV, shards["w_in"].shape[-1]
    f_sh = shards["w_ffn_gate"].shape[-1]
    w = {}
    slab3 = lambda g, n: g.reshape(N_DEV, -1, shards[n].shape[-1])
    t2, t1 = _tile(seq, 2048), _tile(seq, 1024)
    n2, n1 = seq // t2, seq // t1

    n_col = 2 * n_groups * SSM_P
    col = lambda a: a.reshape(n_col, 1)
    a_re_c, a_im_c = col(p["ssm_a_re"]), col(p["ssm_a_im"])
    dt_c = col(jnp.broadcast_to(jnp.exp(p["ssm_log_step"])[:, :, None], (2, n_groups, SSM_P)))
    b_re_c, b_im_c = p["ssm_b_re"].reshape(n_col, SSM_C), p["ssm_b_im"].reshape(n_col, SSM_C)
    (pw_re, pw_im, cf_re, cf_im, bb_re, bb_im), got = _s5_discretize(a_re_c, a_im_c, dt_c, b_re_c, b_im_c,
                                                                     ride=_gather_first([shards["w_in"]]))

    twice = lambda f: (lambda *a: (f(*a),) * 2)
    (h1, h1_t), (w["w_in"],) = _rowwise(twice(x_norm), [x], [p["g_mix"]], [(d_model, BF16)],
                                        flipped=[(d_model, BF16, 1)], name="rms_mix", ride=_gather_second(got))
    z = _mm(h1, w["w_in"], name="in_proj", grid=(n2, n_sh),
            a_spec=pl.BlockSpec((t2, d_model), lambda i, j: (i, 0)),
            b_spec=pl.BlockSpec((None, d_model, in_sh), lambda i, j: (j, 0, 0)),
            o_spec=pl.BlockSpec((t2, in_sh), lambda i, j: (i, j)), o_shape=(seq, n_sh * in_sh), dims="nn")
    qg4 = jnp.tile(p["q_gain"], (1, HEADS_PER_GROUP))
    kg4 = jnp.tile(p["k_gain"], (1, HEADS_PER_GROUP))
    btab = _bias_table(p["rpb"])
    ya, got_a = _attn_fwd(z, qg4, kg4, btab, ride=_gather_first([shards["w_ffn_gate"], shards["w_ffn_up"]]))
    u = z[:, 3 * a_width:]
    n_state = n_groups * SSM_P
    pw_re = pw_re.reshape(2, n_state, SUBLANES).transpose(0, 2, 1)
    pw_im = pw_im.reshape(2, n_state, SUBLANES).transpose(0, 2, 1)
    bb_re4, bb_im4 = bb_re.reshape(2, n_groups, SSM_P, SSM_C), bb_im.reshape(2, n_groups, SSM_P, SSM_C)
    c_re, c_im = p["ssm_c_re"], p["ssm_c_im"]
    t21 = lambda a: a.transpose(0, 2, 1)
    maps_in = _block_diag_in([m for d in range(2) for m in (t21(bb_re4[d]), t21(bb_im4[d]), c_re[d], -c_im[d])])
    maps_out = _block_diag_in([m for d in range(2) for m in (t21(c_re[d]), -t21(c_im[d]), bb_re4[d], bb_im4[d])])
    fwd, bwd_in = [], []
    got_b = None
    for d in range(2):
        rev = d == 1
        tabs = _scan_tables(pw_re[d], pw_im[d], rev)
        if d == 0:
            ride = _gather_first([shards["w_glu"], shards["w_out"]])
        else:
            ride = _gather_second(got_a + got_b) + _gather_first([shards["w_ffn_down"]])
        (xs_re, xs_im, y_d), got = _s5_scan(u, maps_in[4 * d], maps_in[4 * d + 1], tabs, maps_out[4 * d],
                                            maps_out[4 * d + 1], reverse=rev, name=f"s5_fwd_{d}", ride=ride)
        if d == 0:
            got_b = got
        fwd.append((xs_re, xs_im, y_d))
        bwd_in.append((maps_in[4 * d + 2], maps_in[4 * d + 3], _scan_tables(pw_re[d], -pw_im[d], not rev),
                       maps_out[4 * d + 2], maps_out[4 * d + 3]))
    w["w_gate"], w["w_up"], w_glu_full, w_out_full, w_down_first = got
    w["w_glu"] = w_glu_full.reshape(-1, s_width)
    w["w_out"] = w_out_full.reshape(-1, d_model)

    ypre, yg, yg_t = _rowwise(s5_mid, [fwd[0][2], fwd[1][2], u], [p["ssm_d"]], [(s_width, F32), (s_width, F32)],
                              flipped=[(s_width, BF16, 1)], name="s5_skip_gelu")
    t_glu = _mm_plain(yg, w["w_glu"], "nn", name="glu_proj", tn=s_width)
    y_cat, y_cat_t = _rowwise(twice(mix_out_fwd), [ya, yg, t_glu], [p["b_glu"], p["g_out_attn"], p["g_out_ssm"]],
                              [(a_width + s_width, BF16)], flipped=[(a_width + s_width, BF16, 1)], name="mix_out")
    x1, (w["w_down"],) = _mm_plain(y_cat, w["w_out"], "nn", name="out_proj", res=x, tn=2048,
                                   ride=_gather_second([w_down_first]))

    h2, h2_t = _rowwise(twice(x_norm), [x1], [p["g_ffn"]], [(d_model, BF16)], flipped=[(d_model, BF16, 1)],
                        name="rms_ffn")
    ffn_up = functools.partial(
        _mm, grid=(n2, n_sh), a_spec=pl.BlockSpec((t2, d_model), lambda i, j: (i, 0)),
        b_spec=pl.BlockSpec((None, d_model, f_sh), lambda i, j: (j, 0, 0)),
        o_spec=pl.BlockSpec((None, t2, f_sh), lambda i, j: (j, i, 0)), o_shape=(n_sh, seq, f_sh), dims="nn",
        out_dtype=BF16)
    gate = ffn_up(h2, w["w_gate"], name="ffn_gate")
    up = ffn_up(h2, w["w_up"], name="ffn_up")
    flat = lambda a: a.reshape(n_sh * seq, f_sh)
    act, act_t = _rowwise(twice(swiglu_fwd), [flat(gate), flat(up)], [], [(f_sh, BF16)],
                          flipped=[(f_sh, BF16, n_sh)], name="swiglu", tm=1024)
    act, act_t = act.reshape(n_sh, seq, f_sh), act_t.reshape(n_sh, f_sh, seq)
    ffn_out = _mm(act, w["w_down"], name="ffn_down", grid=(n1, n_sh // 2), groups=2,
                  a_spec=pl.BlockSpec((2, t1, f_sh), lambda i, j: (j, i, 0)),
                  b_spec=pl.BlockSpec((2, f_sh, d_model), lambda i, j: (j, 0, 0)),
                  o_spec=pl.BlockSpec((t1, d_model), lambda i, j: (i, 0)), o_shape=(seq, d_model), dims="nn",
                  k_axis=1)

    dx2, dx2_b, sq = _rowwise(functools.partial(loss_head, inv_d=1.0 / d_model), [ffn_out, x1, target], [],
                              [(d_model, F32), (d_model, BF16)], [d_model], name="loss_head")
    loss = 0.5 * jnp.sum(sq) / d_model

    d_act = _mm(dx2_b, w["w_down"], name="ffn_down_dx", grid=(n2, n_sh),
                a_spec=pl.BlockSpec((t2, d_model), lambda i, j: (i, 0)),
                b_spec=pl.BlockSpec((None, f_sh, d_model), lambda i, j: (j, 0, 0)),
                o_spec=pl.BlockSpec((None, t2, f_sh), lambda i, j: (j, i, 0)), o_shape=(n_sh, seq, f_sh), dims="nt",
                out_dtype=BF16)
    g_w_down = _mm(act_t, dx2_b, name="ffn_down_dw", grid=(n_sh, n2),
                   a_spec=pl.BlockSpec((None, f_sh, t2), lambda j, k: (j, 0, k)),
                   b_spec=pl.BlockSpec((t2, d_model), lambda j, k: (k, 0)),
                   o_spec=pl.BlockSpec((None, f_sh, d_model), lambda j, k: (j, 0, 0)),
                   o_shape=(n_sh, f_sh, d_model), dims="nn", k_axis=1)
    d_gate, d_up = _rowwise(swiglu_bwd, [flat(d_act), flat(gate), flat(up)], [], [(f_sh, BF16), (f_sh, BF16)],
                            name="swiglu_bwd", tm=1024)
    d_gate, d_up = d_gate.reshape(n_sh, seq, f_sh), d_up.reshape(n_sh, seq, f_sh)
    d_h2 = _mm(d_gate, w["w_gate"], second=(d_up, w["w_up"]), name="ffn_up_gate_dx", grid=(n1, n_sh),
               a_spec=pl.BlockSpec((None, t1, f_sh), lambda i, j: (j, i, 0)),
               b_spec=pl.BlockSpec((None, d_model, f_sh), lambda i, j: (j, 0, 0)),
               o_spec=pl.BlockSpec((t1, d_model), lambda i, j: (i, 0)), o_shape=(seq, d_model), dims="nt", k_axis=1)
    ffn_dw = functools.partial(
        _mm, grid=(n_sh, n2), a_spec=pl.BlockSpec((d_model, t2), lambda j, k: (0, k)),
        b_spec=pl.BlockSpec((None, t2, f_sh), lambda j, k: (j, k, 0)),
        o_spec=pl.BlockSpec((None, d_model, f_sh), lambda j, k: (j, 0, 0)), o_shape=(n_sh, d_model, f_sh), dims="nn",
        k_axis=1)
    g_w_gate = ffn_dw(h2_t, d_gate, name="ffn_gate_dw")
    g_w_up = ffn_dw(h2_t, d_up, name="ffn_up_dw")
    dx1, g_g_ffn = _rowwise(residual_rms_bwd, [dx2, d_h2, x1], [p["g_ffn"]], [(d_model, F32)], [d_model],
                            name="rms_ffn_bwd")

    d_ycat = _mm_plain(dx1, w["w_out"], "nt", name="out_proj_dx", tn=2048)
    mix_w = a_width + s_width
    tm_o = _tile(mix_w, 1024)
    g_w_out = _mm(y_cat_t, dx1, name="out_proj_dw", grid=(mix_w // tm_o, n1),
                  a_spec=pl.BlockSpec((tm_o, t1), lambda i, k: (i, k)),
                  b_spec=pl.BlockSpec((t1, d_model), lambda i, k: (k, 0)),
                  o_spec=pl.BlockSpec((tm_o, d_model), lambda i, k: (i, 0)), o_shape=(mix_w, d_model), dims="nn",
                  k_axis=1)
    (d_ya, d_yg_direct, d_t, g_goa, g_gos, g_b_glu) = _rowwise(
        functools.partial(mix_out_bwd, a_width=a_width), [d_ycat, ya, yg, t_glu],
        [p["b_glu"], p["g_out_attn"], p["g_out_ssm"]],
        [(a_width, F32), (s_width, F32), (s_width, BF16)], [a_width, s_width, s_width], name="mix_out_bwd")
    d_yg = _mm_plain(d_t, w["w_glu"], "nt", name="glu_proj_dx", res=d_yg_direct, tn=s_width)
    g_w_glu = _mm(yg_t, d_t, name="glu_proj_dw", grid=(1, n1),
                  a_spec=pl.BlockSpec((s_width, t1), lambda i, k: (0, k)),
                  b_spec=pl.BlockSpec((t1, s_width), lambda i, k: (k, 0)),
                  o_spec=pl.BlockSpec((s_width, s_width), lambda i, k: (0, 0)), o_shape=(s_width, s_width),
                  dims="nn", k_axis=1)
    d_ypre, du_skip, g_ssm_d = _rowwise(gelu_skip_bwd, [d_yg, ypre, u], [p["ssm_d"]],
                                        [(s_width, F32), (s_width, F32)], [s_width], name="s5_skip_gelu_bwd")

    ffn_names, mix_names = ("w_ffn_gate", "w_ffn_up", "w_ffn_down"), ("w_glu", "w_out")
    ffn_slabs = [slab3(g, n) for g, n in zip((g_w_gate, g_w_up, g_w_down), ffn_names)]
    mix_slabs = [slab3(g, n) for g, n in zip((g_w_glu, g_w_out), mix_names)]
    du_dirs, adj, r_parts, gb_parts, gc_parts = [], [], [], [], []
    sib, part = {}, {}
    for d, (names, slabs) in enumerate(((ffn_names, ffn_slabs), (mix_names, mix_slabs))):
        win_re, win_im, tabs, wo_re, wo_im = bwd_in[d]
        (as_re, as_im, du_d), got = _s5_scan(d_ypre, win_re, win_im, tabs, wo_re, wo_im, reverse=(d == 0),
                                             name=f"s5_bwd_{d}", ride=_reduce_sibling(slabs))
        du_dirs.append(du_d)
        adj.append((as_re, as_im))
        sib[names] = got
        part[names] = _partial_sums(slabs, got, names)
    for d in range(2):
        (r_re, r_im, gbt_re, gbt_im, gct_re, gct_im), got = _s5_reduce(
            fwd[d][0], fwd[d][1], adj[d][0], adj[d][1], u, d_ypre, name=f"s5_reduce_{d}",
            ride=_reduce_chips(part[mix_names] if d == 0 else part[ffn_names][2:]))
        if d == 0:
            mix_chips = got
        else:
            down_chips = got
        r_parts.append((r_re.reshape(n_state, 1), r_im.reshape(n_state, 1)))
        gb_parts.append((_block_diag_take(gbt_re, SSM_C, SSM_P), _block_diag_take(gbt_im, SSM_C, SSM_P)))
        gc_parts.append((_block_diag_take(gct_re, SSM_C, SSM_P), _block_diag_take(gct_im, SSM_C, SSM_P)))
    cat = lambda i, parts: jnp.concatenate([parts[0][i], parts[1][i]], axis=0)
    gbb_re, gbb_im = cat(0, gb_parts).reshape(n_col, SSM_C), cat(1, gb_parts).reshape(n_col, SSM_C)
    g_a_re, g_a_im, g_ls, g_b_re, g_b_im = _s5_param_grads(
        a_re_c, a_im_c, dt_c, b_re_c, b_im_c, pw_re[:, 0].reshape(n_col, 1), pw_im[:, 0].reshape(n_col, 1),
        cf_re, cf_im, bb_re, bb_im, cat(0, r_parts), cat(1, r_parts), gbb_re, gbb_im)
    g_c_re = cat(0, gc_parts).reshape(2, n_groups, SSM_P, SSM_C).transpose(0, 1, 3, 2)
    g_c_im = -cat(1, gc_parts).reshape(2, n_groups, SSM_P, SSM_C).transpose(0, 1, 3, 2)

    (d_q, d_k, d_v, d_btab, g_qg, g_kg), ffn_chips = _attn_bwd(z, d_ya, qg4, kg4, btab,
                                                                ride=_reduce_chips(part[ffn_names][:2]))
    ffn_chips = ffn_chips + down_chips
    d_u = _rowwise(lambda a, b, c: a + b + c, [du_dirs[0], du_dirs[1], du_skip], [], [(s_width, BF16)],
                   name="s5_du_sum")[0]
    d_z = jnp.concatenate([d_q, d_k, d_v, d_u], axis=1)
    fold_heads = lambda g: g.reshape(n_heads, HEAD_DIM).sum(axis=0, keepdims=True)
    small = {
        "q_gain": fold_heads(g_qg), "k_gain": fold_heads(g_kg), "rpb": _bias_grad(d_btab, n_heads),
        "ssm_a_re": g_a_re.reshape(2, n_groups, SSM_P), "ssm_a_im": g_a_im.reshape(2, n_groups, SSM_P),
        "ssm_b_re": g_b_re.reshape(2, n_groups, SSM_P, SSM_C), "ssm_b_im": g_b_im.reshape(2, n_groups, SSM_P, SSM_C),
        "ssm_c_re": g_c_re, "ssm_c_im": g_c_im,
        "ssm_log_step": g_ls.reshape(2, n_groups, SSM_P).sum(axis=-1),
        "ssm_d": g_ssm_d, "b_glu": g_b_glu, "g_out_attn": g_goa, "g_out_ssm": g_gos, "g_ffn": g_g_ffn,
    }
    g_w_in, got = _mm(h1_t, d_z, name="in_proj_dw", grid=(n_sh, n2),
                      a_spec=pl.BlockSpec((d_model, t2), lambda j, k: (0, k)),
                      b_spec=pl.BlockSpec((t2, in_sh), lambda j, k: (k, j)),
                      o_spec=pl.BlockSpec((None, d_model, in_sh), lambda j, k: (j, 0, 0)),
                      o_shape=(n_sh, d_model, in_sh), dims="nn", k_axis=1,
                      ride=_gather_first([_as_rows(small[n]) for n in SMALL_LATE]))
    d_h1, got = _mm(d_z, w["w_in"], name="in_proj_dx", grid=(n1, n_sh // 2), groups=2,
                    a_spec=pl.BlockSpec((t1, 2 * in_sh), lambda i, j: (i, j)),
                    b_spec=pl.BlockSpec((2, d_model, in_sh), lambda i, j: (j, 0, 0)),
                    o_spec=pl.BlockSpec((t1, d_model), lambda i, j: (i, 0)), o_shape=(seq, d_model), dims="nt",
                    k_axis=1, ride=_gather_second(got) + _reduce_sibling([g_w_in]))
    small_gathered, in_sibling = dict(zip(SMALL_LATE, got[:-1])), got[-1]
    in_part = _partial_sums([g_w_in], [in_sibling], ("w_in",))
    (grad_x, g_g_mix), (in_chips,) = _rowwise(residual_rms_bwd, [dx1, d_h1, x], [p["g_mix"]], [(d_model, F32)],
                                              [d_model], name="rms_mix_bwd", ride=_reduce_chips(in_part))
    reduced = {"w_in": (g_w_in, in_sibling, in_chips)}
    for names, slabs, chips in ((ffn_names, ffn_slabs, ffn_chips), (mix_names, mix_slabs, mix_chips)):
        for i, n in enumerate(names):
            reduced[n] = (slabs[i], sib[names][i], chips[i])
    return loss, grad_x, small_gathered, g_g_mix, reduced


def x_norm(xv, g):
    return xv * _rstd(xv) * g


def s5_mid(y0, y1, uv, d_skip):
    ypre = y0 + y1 + d_skip * uv
    yg = _gelu(ypre)
    return ypre, yg, yg


def mix_out_fwd(ya, yg, t, b_glu, g_oa, g_os):
    ys = yg * _sigmoid(t + b_glu)
    return jnp.concatenate([ya * _rstd(ya) * g_oa, ys * _rstd(ys) * g_os], axis=1)


def mix_out_bwd(d_y, ya, yg, t, b_glu, g_oa, g_os, *, a_width):
    sg = _sigmoid(t + b_glu)
    ys = yg * sg
    d_ya, c_goa = _rms_bwd(d_y[:, :a_width], ya, g_oa)
    d_ys, c_gos = _rms_bwd(d_y[:, a_width:], ys, g_os)
    d_t = d_ys * yg * sg * (1.0 - sg)
    return d_ya, d_ys * sg, d_t, c_goa, c_gos, d_t


def gelu_skip_bwd(d_yg, ypre, uv, d_skip):
    d_ypre = d_yg * _gelu_grad(ypre)
    return d_ypre, d_ypre * d_skip, d_ypre * uv


def swiglu_fwd(gv, uv):
    gv, uv = gv.astype(F32), uv.astype(F32)
    return gv * _sigmoid(gv) * uv


def swiglu_bwd(d_act, gv, uv):
    d_act, gv, uv = d_act.astype(F32), gv.astype(F32), uv.astype(F32)
    sg = _sigmoid(gv)
    return d_act * uv * (sg * (1.0 + gv * (1.0 - sg))), d_act * gv * sg


def loss_head(ffn_out, x1, target, *, inv_d):
    diff = ffn_out + x1 - target
    return diff * inv_d, diff * inv_d, diff * diff


def residual_rms_bwd(d_res, d_h, xv, g):
    dx, c_g = _rms_bwd(d_h, xv, g)
    return d_res + dx, c_g


_ANY = pl.BlockSpec(memory_space=pl.ANY)


def _mesh_place():
    return lax.axis_index("x"), lax.axis_index("y"), lax.axis_index("c")


def _chips(x, y):
    return [(x, y), (1 - x, y), (x, 1 - y), (1 - x, 1 - y)]


def _slab(px, py, pc):
    return 4 * px + 2 * py + pc


def _all_gather(arrs, *, name):
    n = len(arrs)

    def body(*refs):
        in_refs, out_refs = refs[:n], refs[n:2 * n]
        send_sems, recv_sems, local_sems = refs[2 * n:]
        x, y, c = _mesh_place()
        me, sibling = (x, y, c), (x, y, 1 - c)
        others = _chips(x, y)[1:]

        def copy(w, k, block, to, src=None):
            dst = out_refs[w].at[_slab(*block)]
            return pltpu.make_async_remote_copy(
                src_ref=dst if src is None else src, dst_ref=dst, send_sem=send_sems.at[7 * w + k],
                recv_sem=recv_sems.at[7 * w + k], device_id=to, device_id_type=MESH)

        mine = [pltpu.make_async_copy(in_refs[w], out_refs[w].at[_slab(*me)], local_sems.at[w]) for w in range(n)]
        first = []
        for w in range(n):
            mine[w].start()
            first.append(copy(w, 0, me, sibling, src=in_refs[w]))
            first += [copy(w, 1 + j, me, (*chip, c), src=in_refs[w]) for j, chip in enumerate(others)]
        for cp in first:
            cp.start()
        passed = []
        for j, chip in enumerate(others):
            for w in range(n):
                copy(w, 1 + j, (*chip, c), me).wait_recv()
                fwd = copy(w, 4 + j, (*chip, c), sibling)
                fwd.start()
                passed.append(fwd)
        for w in range(n):
            copy(w, 0, sibling, me).wait_recv()
        for j, chip in enumerate(others):
            for w in range(n):
                copy(w, 4 + j, (*chip, 1 - c), me).wait_recv()
        for cp in first + passed:
            cp.wait_send()
        for cp in mine:
            cp.wait()

    return pl.pallas_call(
        body, name=name, in_specs=[_ANY] * n, out_specs=[_ANY] * n,
        out_shape=[jax.ShapeDtypeStruct((N_DEV,) + a.shape, a.dtype) for a in arrs],
        scratch_shapes=[pltpu.SemaphoreType.DMA((7 * n,)), pltpu.SemaphoreType.DMA((7 * n,)),
                        pltpu.SemaphoreType.DMA((n,))],
        compiler_params=pltpu.CompilerParams(has_side_effects=True),
    )(*arrs)


def _adamw(w, m, v, parts, *, name, slab, tr=256):
    rows, cols = w.shape
    tr = _tile(rows, tr)
    n_p = len(parts)

    def body(slab_ref, *refs):
        w_ref, m_ref, v_ref = refs[:3]
        p_refs = refs[3:3 + n_p]
        g_ref, d_ref, nm_ref, nv_ref = refs[3 + n_p:]
        g = None
        for (_, lead), r in zip(parts, p_refs):
            for piece in ([r[...]] if lead is None else [r[i] for i in range(lead)]):
                g = piece.astype(F32) if g is None else g + piece.astype(F32)
        new_m = ADAM_B1 * m_ref[...] + (1.0 - ADAM_B1) * g
        new_v = ADAM_B2 * v_ref[...] + (1.0 - ADAM_B2) * (g * g)
        m_hat = new_m / (1.0 - ADAM_B1 ** ADAM_STEP)
        v_hat = new_v / (1.0 - ADAM_B2 ** ADAM_STEP)
        g_ref[...] = g
        d_ref[...] = -ADAM_LR * (m_hat / (jnp.sqrt(v_hat) + ADAM_EPS) + ADAM_WD * w_ref[...])
        nm_ref[...] = new_m
        nv_ref[...] = new_v

    tile = pl.BlockSpec((tr, cols), lambda i, s: (i, 0))
    p_specs = [pl.BlockSpec((None, tr, cols), lambda i, s: (s[0], i, 0)) if lead is None
               else pl.BlockSpec((lead, tr, cols), lambda i, s: (0, i, 0)) for _, lead in parts]
    return pl.pallas_call(
        body, name=name,
        grid_spec=pltpu.PrefetchScalarGridSpec(num_scalar_prefetch=1, grid=(rows // tr,),
                                               in_specs=[tile] * 3 + p_specs, out_specs=[tile] * 4),
        out_shape=[jax.ShapeDtypeStruct((rows, cols), F32)] * 4, compiler_params=_params(),
    )(jnp.reshape(slab, (1,)).astype(jnp.int32), w, m, v, *[a for a, _ in parts])


def _as_rows(a):
    return a.reshape(-1, a.shape[-1])


BIG = ("w_in", "w_glu", "w_out", "w_ffn_gate", "w_ffn_up", "w_ffn_down")
WEIGHTS = ("g_mix", "w_in", "q_gain", "k_gain", "rpb", "ssm_a_re", "ssm_a_im", "ssm_b_re", "ssm_b_im", "ssm_c_re",
           "ssm_c_im", "ssm_log_step", "ssm_d", "w_glu", "b_glu", "g_out_attn", "g_out_ssm", "w_out", "g_ffn",
           "w_ffn_gate", "w_ffn_up", "w_ffn_down")
SMALL = tuple(n for n in WEIGHTS if n not in BIG)
SMALL_LATE = tuple(n for n in SMALL if n != "g_mix")
VECTORS = ("g_mix", "q_gain", "k_gain", "ssm_d", "b_glu", "g_out_attn", "g_out_ssm", "g_ffn")


def kernel(x, g_mix, w_in, q_gain, k_gain, rpb, ssm_a_re, ssm_a_im, ssm_b_re, ssm_b_im, ssm_c_re, ssm_c_im, ssm_log_step, ssm_d, w_glu, b_glu, g_out_attn, g_out_ssm, w_out, g_ffn, w_ffn_gate, w_ffn_up, w_ffn_down, loss_target, m_g_mix, m_w_in, m_q_gain, m_k_gain, m_rpb, m_ssm_a_re, m_ssm_a_im, m_ssm_b_re, m_ssm_b_im, m_ssm_c_re, m_ssm_c_im, m_ssm_log_step, m_ssm_d, m_w_glu, m_b_glu, m_g_out_attn, m_g_out_ssm, m_w_out, m_g_ffn, m_w_ffn_gate, m_w_ffn_up, m_w_ffn_down, v_g_mix, v_w_in, v_q_gain, v_k_gain, v_rpb, v_ssm_a_re, v_ssm_a_im, v_ssm_b_re, v_ssm_b_im, v_ssm_c_re, v_ssm_c_im, v_ssm_log_step, v_ssm_d, v_w_glu, v_b_glu, v_g_out_attn, v_g_out_ssm, v_w_out, v_g_ffn, v_w_ffn_gate, v_w_ffn_up, v_w_ffn_down):
    wts = dict(g_mix=g_mix, w_in=w_in, q_gain=q_gain, k_gain=k_gain, rpb=rpb, ssm_a_re=ssm_a_re, ssm_a_im=ssm_a_im,
               ssm_b_re=ssm_b_re, ssm_b_im=ssm_b_im, ssm_c_re=ssm_c_re, ssm_c_im=ssm_c_im, ssm_log_step=ssm_log_step,
               ssm_d=ssm_d, w_glu=w_glu, b_glu=b_glu, g_out_attn=g_out_attn, g_out_ssm=g_out_ssm, w_out=w_out,
               g_ffn=g_ffn, w_ffn_gate=w_ffn_gate, w_ffn_up=w_ffn_up, w_ffn_down=w_ffn_down)
    mom = dict(g_mix=m_g_mix, w_in=m_w_in, q_gain=m_q_gain, k_gain=m_k_gain, rpb=m_rpb, ssm_a_re=m_ssm_a_re,
               ssm_a_im=m_ssm_a_im, ssm_b_re=m_ssm_b_re, ssm_b_im=m_ssm_b_im, ssm_c_re=m_ssm_c_re,
               ssm_c_im=m_ssm_c_im, ssm_log_step=m_ssm_log_step, ssm_d=m_ssm_d, w_glu=m_w_glu, b_glu=m_b_glu,
               g_out_attn=m_g_out_attn, g_out_ssm=m_g_out_ssm, w_out=m_w_out, g_ffn=m_g_ffn,
               w_ffn_gate=m_w_ffn_gate, w_ffn_up=m_w_ffn_up, w_ffn_down=m_w_ffn_down)
    var = dict(g_mix=v_g_mix, w_in=v_w_in, q_gain=v_q_gain, k_gain=v_k_gain, rpb=v_rpb, ssm_a_re=v_ssm_a_re,
               ssm_a_im=v_ssm_a_im, ssm_b_re=v_ssm_b_re, ssm_b_im=v_ssm_b_im, ssm_c_re=v_ssm_c_re,
               ssm_c_im=v_ssm_c_im, ssm_log_step=v_ssm_log_step, ssm_d=v_ssm_d, w_glu=v_w_glu, b_glu=v_b_glu,
               g_out_attn=v_g_out_attn, g_out_ssm=v_g_out_ssm, w_out=v_w_out, g_ffn=v_g_ffn,
               w_ffn_gate=v_w_ffn_gate, w_ffn_up=v_w_ffn_up, w_ffn_down=v_w_ffn_down)
    ix, iy, ic = _mesh_place()
    me = _slab(ix, iy, ic)

    shard = {n: wts[n][0] for n in BIG}
    shard_b = {n: shard[n].astype(BF16) for n in BIG}
    p = {n: (wts[n][0].reshape(1, -1) if n in VECTORS else wts[n][0]) for n in SMALL}

    loss, grad_x, small_gathered, g_g_mix, reduced = _local_step(x[0], loss_target[0], p, shard_b)
    loss = lax.psum(loss, ("x", "y", "c"))
    out = {}
    for n in BIG:
        slabs, from_sibling, from_chips = reduced[n]
        rows, cols = slabs.shape[1:]
        res = _adamw(shard[n].reshape(rows, cols), mom[n][0].reshape(rows, cols), var[n][0].reshape(rows, cols),
                     [(slabs, None), (from_sibling, 1), (from_chips, 3)], name=f"adamw_{n}", slab=me)
        out[n] = [r.reshape(wts[n].shape) for r in res]

    small_gathered["g_mix"] = _all_gather([g_g_mix], name="gather_g_mix")[0]
    for n in SMALL:
        res = _adamw(_as_rows(wts[n]), _as_rows(mom[n]), _as_rows(var[n]), [(small_gathered[n], N_DEV)],
                     name=f"adamw_{n}", slab=me, tr=1024)
        out[n] = [r.reshape(wts[n].shape) for r in res]

    return (loss, grad_x[None], *[out[n][0] for n in WEIGHTS], *[out[n][1] for n in WEIGHTS],
            *[out[n][2] for n in WEIGHTS], *[out[n][3] for n in WEIGHTS])
```

```python
import functools
import math

import jax
import jax.numpy as jnp
from jax import lax
from jax.experimental import pallas as pl
from jax.experimental.pallas import tpu as pltpu

F32 = jnp.float32
BF16 = jnp.bfloat16

N_DEV = 8
GRID_W = 64
WIN_H = 8
WIN_W = 16
HEAD_DIM = 64
HEADS_PER_GROUP = 4
GROUP_LANES = HEADS_PER_GROUP * HEAD_DIM
SSM_C = 16
SSM_P = 64
GROUPS_PER_TILE = 8
U_TILE = GROUPS_PER_TILE * SSM_C
ST_TILE = GROUPS_PER_TILE * SSM_P
TILES_TOGETHER = 2
SUBLANES = 8
RMS_EPS = 1e-6
NEG_INF = -1e30
A_RE_MAX = -1e-4
ADAM_LR, ADAM_B1, ADAM_B2, ADAM_EPS, ADAM_WD, ADAM_STEP = 0.001, 0.9, 0.999, 1e-08, 0.01, 10
VMEM_LIMIT_V7X = 56 * 1024 * 1024
MESH = pl.DeviceIdType.MESH

_NN = (((1,), (0,)), ((), ()))
_NT = (((1,), (1,)), ((), ()))
_TN = (((0,), (0,)), ((), ()))
_DIMS = {"nn": _NN, "nt": _NT, "tn": _TN}


def _params(**kw):
    return pltpu.CompilerParams(vmem_limit_bytes=VMEM_LIMIT_V7X, **kw)


def _dot(a, b, dims=_NN):
    return lax.dot_general(a, b, dims, preferred_element_type=F32)


def _mm(a, b, *, name, grid, a_spec, b_spec, o_spec, o_shape, dims, k_axis=None, res=None, out_dtype=F32,
        exact=False, second=None, ride=None, groups=1):
    dn = _DIMS[dims]
    nk = 1 if k_axis is None else grid[k_axis]
    acc_shape = tuple(d for d in o_spec.block_shape if d is not None)
    n_in = 2 + (2 if second is not None else 0)

    def body(*refs):
        a_ref, b_ref = refs[:2]
        r_ref = refs[n_in] if res is not None else None
        o_ref, acc = refs[-2:]
        def product(x_ref, y_ref):
            if groups == 1:
                return _dot(x_ref[...].astype(BF16), y_ref[...].astype(BF16), dn)
            total, width = None, x_ref.shape[-1] // groups
            for s in range(groups):
                x = x_ref[s] if len(x_ref.shape) == 3 else x_ref[:, s * width:(s + 1) * width]
                t = _dot(x.astype(BF16), y_ref[s].astype(BF16), dn)
                total = t if total is None else total + t
            return total

        if exact:
            p = lax.dot_general(a_ref[...], b_ref[...], dn, precision=lax.Precision.HIGHEST,
                                preferred_element_type=F32)
        else:
            p = product(a_ref, b_ref)
        if second is not None:
            p = p + product(refs[2], refs[3])

        def finish(v):
            if r_ref is not None:
                v = v + r_ref[...].astype(F32)
            o_ref[...] = v.astype(out_dtype)

        if nk == 1:
            finish(p)
        else:
            k = pl.program_id(k_axis)

            @pl.when(k == 0)
            def _():
                acc[...] = p

            @pl.when(k > 0)
            def _():
                acc[...] += p

            @pl.when(k == nk - 1)
            def _():
                finish(acc[...])

    ins = [a, b] + (list(second) if second is not None else []) + ([res] if res is not None else [])
    in_specs = [a_spec, b_spec] * (n_in // 2) + ([o_spec] if res is not None else [])
    first, last = _grid_ends(grid)
    (out,), rode = _call(
        body, name=name, grid=grid, in_specs=in_specs, out_specs=[o_spec],
        out_shape=[jax.ShapeDtypeStruct(o_shape, out_dtype)],
        scratch_shapes=[pltpu.VMEM(acc_shape if nk > 1 else (SUBLANES, 128), F32)],
        args=ins, ride=ride, first=first, last=last)
    return out if ride is None else (out, rode)


def _tile(n, want):
    if n <= want:
        return n
    t = want
    while n % t:
        t //= 2
    return t


def _mm_plain(a, b, dims, *, name, res=None, out_dtype=F32, tm=512, tn=512, tk=512, exact=False, ride=None):
    if dims == "nn":
        (m, k), n = a.shape, b.shape[1]
    elif dims == "nt":
        (m, k), n = a.shape, b.shape[0]
    else:
        (k, m), n = a.shape, b.shape[1]
    tm, tn = _tile(m, tm), _tile(n, tn)
    if dims == "tn":
        tk = _tile(k, tk)
        grid = (m // tm, n // tn, k // tk)
        a_spec = pl.BlockSpec((tk, tm), lambda i, j, kk: (kk, i))
        b_spec = pl.BlockSpec((tk, tn), lambda i, j, kk: (kk, j))
        o_spec = pl.BlockSpec((tm, tn), lambda i, j, kk: (i, j))
        return _mm(a, b, name=name, grid=grid, a_spec=a_spec, b_spec=b_spec, o_spec=o_spec, o_shape=(m, n),
                   dims=dims, k_axis=2, res=res, out_dtype=out_dtype)
    grid = (n // tn, m // tm)
    a_spec = pl.BlockSpec((tm, k), lambda j, i: (i, 0))
    if dims == "nn":
        b_spec = pl.BlockSpec((k, tn), lambda j, i: (0, j))
    else:
        b_spec = pl.BlockSpec((tn, k), lambda j, i: (j, 0))
    o_spec = pl.BlockSpec((tm, tn), lambda j, i: (i, j))
    return _mm(a, b, name=name, grid=grid, a_spec=a_spec, b_spec=b_spec, o_spec=o_spec, o_shape=(m, n), dims=dims,
               res=res, out_dtype=out_dtype, exact=exact, ride=ride)


def _rowwise(fn, tiled, bcast, outs, accs=(), *, name, tm=256, flipped=(), ride=None):
    m = tiled[0].shape[0]
    tm = _tile(m, tm)
    n_t, n_b, n_o, n_f = len(tiled), len(bcast), len(outs), len(flipped)

    def body(*refs):
        ins = [r[...] for r in refs[: n_t + n_b]]
        o_refs = refs[n_t + n_b: n_t + n_b + n_o]
        f_refs = refs[n_t + n_b + n_o: n_t + n_b + n_o + n_f]
        a_refs = refs[n_t + n_b + n_o + n_f:]
        res = fn(*ins)
        if not isinstance(res, (tuple, list)):
            res = (res,)
        for r, v in zip(o_refs, res[:n_o]):
            r[...] = v.astype(r.dtype)
        for r, v in zip(f_refs, res[n_o:n_o + n_f]):
            r[...] = v.astype(F32).T.astype(r.dtype)
        first = pl.program_id(0) == 0
        for r, v in zip(a_refs, res[n_o + n_f:]):
            s = jnp.sum(v, axis=0, keepdims=True)

            @pl.when(first)
            def _():
                r[...] = s

            @pl.when(jnp.logical_not(first))
            def _():
                r[...] += s

    in_specs = [pl.BlockSpec((tm, t.shape[1]), lambda i: (i, 0)) for t in tiled]
    in_specs += [pl.BlockSpec(b.shape, lambda i, nd=b.ndim: (0,) * nd) for b in bcast]
    out_specs = [pl.BlockSpec((tm, n), lambda i: (i, 0)) for n, _ in outs]
    out_specs += [pl.BlockSpec((n, tm), lambda i, per=m // tm // g: (i // per, i % per)) for n, _, g in flipped]
    out_specs += [pl.BlockSpec((1, n), lambda i: (0, 0)) for n in accs]
    out_shape = [jax.ShapeDtypeStruct((m, n), dt) for n, dt in outs]
    out_shape += [jax.ShapeDtypeStruct((g * n, m // g), dt) for n, dt, g in flipped]
    out_shape += [jax.ShapeDtypeStruct((1, n), F32) for n in accs]
    first, last = _grid_ends((m // tm,))
    res, rode = _call(body, name=name, grid=(m // tm,), in_specs=in_specs, out_specs=out_specs, out_shape=out_shape,
                      scratch_shapes=[], args=list(tiled) + list(bcast), ride=ride, first=first, last=last)
    return res if ride is None else (res, rode)


def _rstd(x):
    return lax.rsqrt(jnp.mean(x * x, axis=-1, keepdims=True) + RMS_EPS)


def _rms_bwd(dh, x, g):
    xh = x * _rstd(x)
    dxh = dh * g
    dx = _rstd(x) * (dxh - xh * jnp.mean(dxh * xh, axis=-1, keepdims=True))
    return dx, dh * xh


def _sigmoid(x):
    return 1.0 / (1.0 + jnp.exp(-x))


_GELU_K = math.sqrt(2.0 / math.pi)
_GELU_C = 0.044715


def _gelu(x):
    return 0.5 * x * (1.0 + jnp.tanh(_GELU_K * (x + _GELU_C * x * x * x)))


def _gelu_grad(x):
    th = jnp.tanh(_GELU_K * (x + _GELU_C * x * x * x))
    return 0.5 * (1.0 + th) + 0.5 * x * (1.0 - th * th) * _GELU_K * (1.0 + 3.0 * _GELU_C * x * x)


class _Exchange:
    def __init__(self, arrays, outs, n_sems, sends, recvs=None, local=None, aliases=None):
        self.arrays, self.outs, self.n_sems = list(arrays), list(outs), n_sems
        self.sends, self.local, self.aliases = sends, local, aliases or {}
        self.recvs = recvs or (lambda i, o: [(k, dst) for k, _, dst, _ in sends(i, o)])

    def __add__(self, other):
        na, no, ns = len(self.arrays), len(self.outs), self.n_sems
        mine = lambda f: (lambda i, o: f(i[:na], o[:no]))
        shift = lambda f, at: (lambda i, o: [(k + ns,) + tuple(rest) for k, *rest in f(i[na:], o[no:])]) if at else None
        both = lambda f, g: (lambda i, o: f(i, o) + g(i, o))
        local = None
        if self.local or other.local:
            la = mine(self.local) if self.local else (lambda i, o: [])
            lb = (lambda i, o: other.local(i[na:], o[no:])) if other.local else (lambda i, o: [])
            local = both(la, lb)
        aliases = dict(self.aliases)
        aliases.update({na + i: no + o for i, o in other.aliases.items()})
        return _Exchange(self.arrays + other.arrays, self.outs + other.outs, ns + other.n_sems,
                         both(mine(self.sends), shift(other.sends, True)),
                         both(mine(self.recvs), shift(other.recvs, True)), local, aliases)

    def descriptors(self, in_refs, out_refs, send_sems, recv_sems, local_sems):
        me = _mesh_place()
        remote = lambda k, src, dst, to: pltpu.make_async_remote_copy(
            src_ref=src, dst_ref=dst, send_sem=send_sems.at[k], recv_sem=recv_sems.at[k], device_id=to,
            device_id_type=MESH)
        out = [remote(*s) for s in self.sends(in_refs, out_refs)]
        arrive = [remote(k, dst, dst, me) for k, dst in self.recvs(in_refs, out_refs)]
        own = [pltpu.make_async_copy(src, dst, local_sems.at[i])
               for i, (src, dst) in enumerate(self.local(in_refs, out_refs) if self.local else [])]
        return out, arrive, own

    def start(self, *refs):
        out, _, own = self.descriptors(*refs)
        for cp in own + out:
            cp.start()

    def finish(self, *refs):
        out, arrive, own = self.descriptors(*refs)
        for cp in arrive:
            cp.wait_recv()
        for cp in out:
            cp.wait_send()
        for cp in own:
            cp.wait()


def _call(body, *, name, grid, in_specs, out_specs, out_shape, scratch_shapes, args, ride=None, first=None, last=None):
    if ride is None:
        res = pl.pallas_call(body, name=name, grid=grid, in_specs=in_specs, out_specs=out_specs, out_shape=out_shape,
                             scratch_shapes=scratch_shapes, compiler_params=_params())(*args)
        return list(res), []
    n_in, n_out, n_scr = len(in_specs), len(out_specs), len(scratch_shapes)
    r_in, r_out = len(ride.arrays), len(ride.outs)

    def wrapped(*refs):
        ins, refs = refs[:n_in], refs[n_in:]
        x_in, refs = refs[:r_in], refs[r_in:]
        outs, refs = refs[:n_out], refs[n_out:]
        x_out, refs = refs[:r_out], refs[r_out:]
        scr, sems = refs[:n_scr], refs[n_scr:]

        @pl.when(first())
        def _():
            ride.start(x_in, x_out, *sems)

        body(*ins, *outs, *scr)

        @pl.when(last())
        def _():
            ride.finish(x_in, x_out, *sems)

    n_local = max(1, len(ride.arrays))
    res = pl.pallas_call(
        wrapped, name=name, grid=grid, in_specs=list(in_specs) + [_ANY] * r_in,
        out_specs=list(out_specs) + [_ANY] * r_out, out_shape=list(out_shape) + ride.outs,
        scratch_shapes=list(scratch_shapes) + [pltpu.SemaphoreType.DMA((ride.n_sems,)),
                                               pltpu.SemaphoreType.DMA((ride.n_sems,)),
                                               pltpu.SemaphoreType.DMA((n_local,))],
        input_output_aliases={n_in + i: n_out + o for i, o in ride.aliases.items()},
        compiler_params=_params(has_side_effects=True),
    )(*args, *ride.arrays)
    return list(res[:n_out]), list(res[n_out:])


def _gather_first(shards):
    def sends(i, o):
        x, y, c = _mesh_place()
        peers = [(x, y, 1 - c)] + [(px, py, c) for px, py in _chips(x, y)[1:]]
        return [(4 * w + k, i[w], o[w].at[_slab(x, y, c)], to) for w in range(len(i)) for k, to in enumerate(peers)]

    def recvs(i, o):
        x, y, c = _mesh_place()
        peers = [(x, y, 1 - c)] + [(px, py, c) for px, py in _chips(x, y)[1:]]
        return [(4 * w + k, o[w].at[_slab(*peer)]) for w in range(len(i)) for k, peer in enumerate(peers)]

    def local(i, o):
        return [(i[w], o[w].at[_slab(*_mesh_place())]) for w in range(len(i))]

    outs = [jax.ShapeDtypeStruct((N_DEV,) + a.shape, a.dtype) for a in shards]
    return _Exchange(shards, outs, 4 * len(shards), sends, recvs, local)


def _gather_second(gathered):
    def sends(i, o):
        x, y, c = _mesh_place()
        return [(3 * w + j, o[w].at[_slab(px, py, c)], o[w].at[_slab(px, py, c)], (x, y, 1 - c))
                for w in range(len(o)) for j, (px, py) in enumerate(_chips(x, y)[1:])]

    def recvs(i, o):
        x, y, c = _mesh_place()
        return [(3 * w + j, o[w].at[_slab(px, py, 1 - c)])
                for w in range(len(o)) for j, (px, py) in enumerate(_chips(x, y)[1:])]

    outs = [jax.ShapeDtypeStruct(a.shape, a.dtype) for a in gathered]
    return _Exchange(gathered, outs, 3 * len(gathered), sends, recvs, aliases={w: w for w in range(len(gathered))})


def _reduce_sibling(slabs):
    def sends(i, o):
        x, y, c = _mesh_place()
        return [(4 * w + k, i[w].at[_slab(px, py, 1 - c)], o[w].at[k], (x, y, 1 - c))
                for w in range(len(i)) for k, (px, py) in enumerate(_chips(x, y))]

    outs = [jax.ShapeDtypeStruct((4,) + a.shape[1:], a.dtype) for a in slabs]
    return _Exchange(slabs, outs, 4 * len(slabs), sends)


def _reduce_chips(partials):
    def sends(i, o):
        x, y, c = _mesh_place()
        return [(3 * w + k, i[w].at[k], o[w].at[k], (px, py, c))
                for w in range(len(i)) for k, (px, py) in enumerate(_chips(x, y)[1:])]

    outs = [jax.ShapeDtypeStruct(a.shape, a.dtype) for a in partials]
    return _Exchange(partials, outs, 3 * len(partials), sends)


def _head_masks():
    lane_head = lax.broadcasted_iota(jnp.int32, (1, GROUP_LANES), 1) // HEAD_DIM
    return [(lane_head == h).astype(F32) for h in range(HEADS_PER_GROUP)]


def _head_block_diag():
    r = lax.broadcasted_iota(jnp.int32, (GROUP_LANES, GROUP_LANES), 0) // HEAD_DIM
    c = lax.broadcasted_iota(jnp.int32, (GROUP_LANES, GROUP_LANES), 1) // HEAD_DIM
    return (r == c).astype(BF16)


def _head_mean(x, bd):
    hi = x.astype(BF16)
    lo = (x - hi.astype(F32)).astype(BF16)
    return (_dot(hi, bd) + _dot(lo, bd)) * (1.0 / HEAD_DIM)


def _stack_heads(x, masks):
    return jnp.concatenate([x * m for m in masks], axis=0)


def _unstack_heads(xs, masks):
    out = xs[0:GRID_W] * masks[0]
    for h in range(1, HEADS_PER_GROUP):
        out = out + xs[h * GRID_W:(h + 1) * GRID_W] * masks[h]
    return out


def _row_start(r, rows):
    return jnp.clip(r - WIN_H // 2, 0, rows - WIN_H)


ROWS_PER_STEP = 4


def _attn_common_specs(seq, n_hg, rows):
    win_keys = WIN_H * GRID_W
    q_spec = pl.BlockSpec((ROWS_PER_STEP * GRID_W, GROUP_LANES), lambda g, r: (r, g))
    k_spec = pl.BlockSpec((seq, GROUP_LANES), lambda g, r: (0, n_hg + g))
    v_spec = pl.BlockSpec((seq, GROUP_LANES), lambda g, r: (0, 2 * n_hg + g))
    gain_spec = pl.BlockSpec((1, GROUP_LANES), lambda g, r: (0, 0))

    def variant(r):
        return _row_start(r, rows) - r + (WIN_H - 1)

    bias_specs = [pl.BlockSpec((None, None, HEADS_PER_GROUP, GRID_W, win_keys),
                               lambda g, r, h=h: (g, variant(ROWS_PER_STEP * r + h), 0, 0, 0))
                  for h in range(ROWS_PER_STEP)]
    return q_spec, k_spec, v_spec, gain_spec, bias_specs, variant


def _attn_prepare_kv(k_ref, v_ref, kg, kn_scr, vb_scr, bd, seq):
    chunk = _tile(seq, 512)

    def step(c, carry):
        rows = pl.ds(pl.multiple_of(c * chunk, chunk), chunk)
        k = k_ref[rows, :]
        kn_scr[rows, :] = (k * lax.rsqrt(_head_mean(k * k, bd) + RMS_EPS) * kg).astype(BF16)
        vb_scr[rows, :] = v_ref[rows, :].astype(BF16)
        return carry

    lax.fori_loop(0, seq // chunk, step, 0)


def _attn_probs(qn, kw, bias, masks):
    qs = _stack_heads(qn, masks).astype(BF16)
    s = _dot(qs, kw, _NT) * (1.0 / math.sqrt(HEAD_DIM)) + bias
    m = jnp.max(s, axis=-1, keepdims=True)
    p = jnp.exp(s - m)
    return qs, p * (1.0 / jnp.sum(p, axis=-1, keepdims=True))


def _grid_ends(grid):
    first = lambda: functools.reduce(jnp.logical_and, [pl.program_id(a) == 0 for a in range(len(grid))])
    last = lambda: functools.reduce(jnp.logical_and, [pl.program_id(a) == n - 1 for a, n in enumerate(grid)])
    return first, last


def _attn_fwd(z, qg4, kg4, btab, ride=None):
    seq = z.shape[0]
    a_width = btab.shape[0] * GROUP_LANES
    n_hg, rows, win_keys = btab.shape[0], seq // GRID_W, WIN_H * GRID_W
    q_spec, k_spec, v_spec, gain_spec, bias_specs, _ = _attn_common_specs(seq, n_hg, rows)
    grid = (n_hg, rows // ROWS_PER_STEP)

    def body(q_ref, k_ref, v_ref, qg_ref, kg_ref, *rest):
        b_refs, (o_ref, kn_scr, vb_scr) = rest[:ROWS_PER_STEP], rest[ROWS_PER_STEP:]
        bd, masks = _head_block_diag(), _head_masks()

        @pl.when(pl.program_id(1) == 0)
        def _():
            _attn_prepare_kv(k_ref, v_ref, kg_ref[...], kn_scr, vb_scr, bd, seq)

        for h in range(ROWS_PER_STEP):
            r = ROWS_PER_STEP * pl.program_id(1) + h
            mine = slice(h * GRID_W, (h + 1) * GRID_W)
            win = pl.ds(pl.multiple_of(_row_start(r, rows) * GRID_W, GRID_W), win_keys)
            q = q_ref[mine, :]
            qn = q * lax.rsqrt(_head_mean(q * q, bd) + RMS_EPS) * qg_ref[...]
            bias = b_refs[h][...].reshape(HEADS_PER_GROUP * GRID_W, win_keys)
            _, p = _attn_probs(qn, kn_scr[win, :], bias, masks)
            o_ref[mine, :] = _unstack_heads(_dot(p.astype(BF16), vb_scr[win, :]), masks)

    first, last = _grid_ends(grid)
    (ya,), rode = _call(
        body, name="attn_fwd", grid=grid,
        in_specs=[q_spec, k_spec, v_spec, gain_spec, gain_spec] + bias_specs,
        out_specs=[pl.BlockSpec((ROWS_PER_STEP * GRID_W, GROUP_LANES), lambda g, r: (r, g))],
        out_shape=[jax.ShapeDtypeStruct((seq, a_width), F32)],
        scratch_shapes=[pltpu.VMEM((seq, GROUP_LANES), BF16), pltpu.VMEM((seq, GROUP_LANES), BF16)],
        args=(z, z, z, qg4, kg4) + (btab,) * ROWS_PER_STEP, ride=ride, first=first, last=last)
    return ya, rode


def _attn_bwd(z, d_out, qg4, kg4, btab, ride=None):
    seq = z.shape[0]
    n_hg, rows, win_keys = btab.shape[0], seq // GRID_W, WIN_H * GRID_W
    a_width = n_hg * GROUP_LANES
    q_spec, k_spec, v_spec, gain_spec, bias_specs, variant = _attn_common_specs(seq, n_hg, rows)
    scale = 1.0 / math.sqrt(HEAD_DIM)
    grid = (n_hg, rows // ROWS_PER_STEP)

    def body(q_ref, k_ref, v_ref, do_ref, qg_ref, kg_ref, *rest):
        b_refs, rest = rest[:ROWS_PER_STEP], rest[ROWS_PER_STEP:]
        dq_ref, dk_out, dv_out, db_ref, dqg_ref, dkg_ref, kn_scr, vb_scr, dk_ref, dv_ref = rest
        bd, masks = _head_block_diag(), _head_masks()

        @pl.when(pl.program_id(1) == 0)
        def _():
            _attn_prepare_kv(k_ref, v_ref, kg_ref[...], kn_scr, vb_scr, bd, seq)
            dk_ref[...] = jnp.zeros_like(dk_ref)
            dv_ref[...] = jnp.zeros_like(dv_ref)
            db_ref[...] = jnp.zeros_like(db_ref)
            dqg_ref[...] = jnp.zeros_like(dqg_ref)

        qg = qg_ref[...]
        for h in range(ROWS_PER_STEP):
            r = ROWS_PER_STEP * pl.program_id(1) + h
            mine = slice(h * GRID_W, (h + 1) * GRID_W)
            win = pl.ds(pl.multiple_of(_row_start(r, rows) * GRID_W, GRID_W), win_keys)
            q = q_ref[mine, :]
            rq = lax.rsqrt(_head_mean(q * q, bd) + RMS_EPS)
            qh = q * rq
            kw, vw = kn_scr[win, :], vb_scr[win, :]
            bias = b_refs[h][...].reshape(HEADS_PER_GROUP * GRID_W, win_keys)
            qs, p = _attn_probs(qh * qg, kw, bias, masks)
            dos = _stack_heads(do_ref[mine, :], masks).astype(BF16)
            dp = _dot(dos, vw, _NT)
            ds = p * (dp - jnp.sum(p * dp, axis=-1, keepdims=True))
            db_ref[variant(r)] += ds.reshape(HEADS_PER_GROUP, GRID_W, win_keys)
            dsb = ds.astype(BF16)
            dqn = _unstack_heads(_dot(dsb, kw), masks) * scale
            dk_ref[win, :] += _dot(dsb, qs, _TN) * scale
            dv_ref[win, :] += _dot(p.astype(BF16), dos, _TN)
            dqg_ref[...] += jnp.sum(dqn * qh, axis=0, keepdims=True)
            dqh = dqn * qg
            dq_ref[mine, :] = (rq * (dqh - qh * _head_mean(dqh * qh, bd))).astype(BF16)

        @pl.when(pl.program_id(1) == grid[1] - 1)
        def _():
            chunk = _tile(seq, 512)
            kg = kg_ref[...]

            def step(c, dkg):
                rws = pl.ds(pl.multiple_of(c * chunk, chunk), chunk)
                k = k_ref[rws, :]
                rk = lax.rsqrt(_head_mean(k * k, bd) + RMS_EPS)
                kh = k * rk
                dkn = dk_ref[rws, :]
                dkh = dkn * kg
                dk_out[rws, :] = (rk * (dkh - kh * _head_mean(dkh * kh, bd))).astype(BF16)
                dv_out[rws, :] = dv_ref[rws, :].astype(BF16)
                return dkg + jnp.sum(dkn * kh, axis=0, keepdims=True)

            dkg_ref[...] = lax.fori_loop(0, seq // chunk, step, jnp.zeros((1, GROUP_LANES), F32))

    col_spec = pl.BlockSpec((seq, GROUP_LANES), lambda g, r: (0, g))
    gsum_spec = pl.BlockSpec((None, 1, GROUP_LANES), lambda g, r: (g, 0, 0))
    first, last = _grid_ends(grid)
    rows_spec = pl.BlockSpec((ROWS_PER_STEP * GRID_W, GROUP_LANES), lambda g, r: (r, g))
    return _call(
        body, name="attn_bwd", grid=grid,
        in_specs=[q_spec, k_spec, v_spec, rows_spec, gain_spec, gain_spec] + bias_specs,
        out_specs=[rows_spec, col_spec, col_spec,
                   pl.BlockSpec((None, WIN_H, HEADS_PER_GROUP, GRID_W, win_keys), lambda g, r: (g, 0, 0, 0, 0)),
                   gsum_spec, gsum_spec],
        out_shape=[jax.ShapeDtypeStruct((seq, a_width), BF16)] * 3
        + [jax.ShapeDtypeStruct(btab.shape, F32)]
        + [jax.ShapeDtypeStruct((n_hg, 1, GROUP_LANES), F32)] * 2,
        scratch_shapes=[pltpu.VMEM((seq, GROUP_LANES), BF16), pltpu.VMEM((seq, GROUP_LANES), BF16),
                        pltpu.VMEM((seq, GROUP_LANES), F32), pltpu.VMEM((seq, GROUP_LANES), F32)],
        args=(z, z, z, d_out, qg4, kg4) + (btab,) * ROWS_PER_STEP, ride=ride, first=first, last=last)


DC_SLOTS = 2 * WIN_W


def _bias_spread(c):
    shape = (WIN_H * DC_SLOTS, WIN_H * GRID_W)
    rows = lax.broadcasted_iota(jnp.int32, shape, 0)
    cols = lax.broadcasted_iota(jnp.int32, shape, 1)
    row_i, row_d = rows // DC_SLOTS, rows % DC_SLOTS
    col_i, kc = cols // GRID_W, cols % GRID_W
    col_start = jnp.clip(c - WIN_W // 2, 0, GRID_W - WIN_W)
    col_in = (kc >= col_start) & (kc < col_start + WIN_W)
    hit = (row_i == col_i) & (row_d == kc - c + (WIN_W - 1)) & col_in
    mask_slot = (row_i == 0) & (row_d == DC_SLOTS - 1) & jnp.logical_not(col_in)
    return jnp.where(hit, 1.0, jnp.where(mask_slot, NEG_INF, 0.0)).astype(F32)


def _bias_table(rpb):
    n_h = rpb.shape[0]
    n_hg = n_h // HEADS_PER_GROUP
    rows = jnp.stack([rpb[:, v:v + WIN_H] for v in range(WIN_H)], axis=1)
    rows = jnp.pad(rows, ((0, 0), (0, 0), (0, 0), (0, DC_SLOTS - rows.shape[-1])))
    rows = rows.at[:, :, 0, DC_SLOTS - 1].set(1.0)
    rows = rows.reshape(n_hg, HEADS_PER_GROUP, WIN_H, WIN_H * DC_SLOTS).transpose(0, 2, 1, 3)
    n_rows, win_keys, depth = n_h * WIN_H, WIN_H * GRID_W, WIN_H * DC_SLOTS

    def body(r_ref, o_ref):
        for cc in range(SUBLANES):
            spread = _bias_spread(pl.program_id(0) * SUBLANES + cc)
            o_ref[cc] = lax.dot_general(r_ref[...], spread, _NN, precision=lax.Precision.HIGHEST,
                                        preferred_element_type=F32)

    tab = pl.pallas_call(
        body, name="rpb_spread", grid=(GRID_W // SUBLANES,),
        in_specs=[pl.BlockSpec((n_rows, depth), lambda c: (0, 0))],
        out_specs=pl.BlockSpec((SUBLANES, n_rows, win_keys), lambda c: (c, 0, 0)),
        out_shape=jax.ShapeDtypeStruct((GRID_W, n_rows, win_keys), F32), compiler_params=_params(),
    )(rows.reshape(n_rows, depth))
    return tab.transpose(1, 0, 2).reshape(n_hg, WIN_H, HEADS_PER_GROUP, GRID_W, win_keys)


def _bias_grad(dtab, n_h):
    n_hg = n_h // HEADS_PER_GROUP
    n_rows, win_keys, depth = n_h * WIN_H, WIN_H * GRID_W, WIN_H * DC_SLOTS

    def body(d_ref, o_ref):
        total = None
        for cc in range(SUBLANES):
            spread = _bias_spread(pl.program_id(0) * SUBLANES + cc).astype(BF16)
            t = _dot(d_ref[cc].astype(BF16), spread, _NT)
            total = t if total is None else total + t

        @pl.when(pl.program_id(0) == 0)
        def _():
            o_ref[...] = total

        @pl.when(pl.program_id(0) > 0)
        def _():
            o_ref[...] += total

    d_rows = pl.pallas_call(
        body, name="rpb_diag_sum", grid=(GRID_W // SUBLANES,),
        in_specs=[pl.BlockSpec((SUBLANES, n_rows, win_keys), lambda c: (c, 0, 0))],
        out_specs=pl.BlockSpec((n_rows, depth), lambda c: (0, 0)),
        out_shape=jax.ShapeDtypeStruct((n_rows, depth), F32), compiler_params=_params(),
    )(dtab.reshape(n_rows, GRID_W, win_keys).transpose(1, 0, 2))
    d_rows = d_rows.reshape(n_hg, WIN_H, HEADS_PER_GROUP, WIN_H, DC_SLOTS).transpose(0, 2, 1, 3, 4)
    d_rows = d_rows.reshape(n_h, WIN_H, WIN_H, DC_SLOTS)[..., : 2 * WIN_W - 1]
    out = jnp.zeros((n_h, 2 * WIN_H - 1, 2 * WIN_W - 1), F32)
    for v in range(WIN_H):
        out = out.at[:, v:v + WIN_H].add(d_rows[:, v])
    return out


def _cmul(ar, ai, br, bi):
    return ar * br - ai * bi, ar * bi + ai * br


def _s5_discretize(a_re, a_im, dt, b_re, b_im, ride=None):
    c = b_re.shape[1]

    def fn(are, aim, dt_, bre, bim):
        lr, li = jnp.minimum(are, A_RE_MAX), aim
        mag = jnp.exp(lr * dt_)
        l1r, l1i = mag * jnp.cos(li * dt_), mag * jnp.sin(li * dt_)
        den = lr * lr + li * li
        nr, ni = l1r - 1.0, l1i
        cr, ci = (nr * lr + ni * li) / den, (ni * lr - nr * li) / den
        bbr, bbi = _cmul(cr, ci, bre, bim)
        shape = (are.shape[0], SUBLANES)
        lane = lax.broadcasted_iota(jnp.int32, shape, 1)
        pr, pi = l1r, l1i
        acc_r, acc_i = jnp.zeros(shape, F32), jnp.zeros(shape, F32)
        for k in range(SUBLANES):
            acc_r = jnp.where(lane == k, pr, acc_r)
            acc_i = jnp.where(lane == k, pi, acc_i)
            pr, pi = _cmul(pr, pi, l1r, l1i)
        return acc_r, acc_i, cr, ci, bbr, bbi

    return _rowwise(fn, [a_re, a_im, dt, b_re, b_im], [],
                    [(SUBLANES, F32), (SUBLANES, F32), (1, F32), (1, F32), (c, F32), (c, F32)],
                    name="s5_discretize", tm=1024, ride=ride)


def _s5_param_grads(a_re, a_im, dt, b_re, b_im, l1r, l1i, cr, ci, bbr, bbi, r_re, r_im, gb_re, gb_im):
    c = b_re.shape[1]

    def fn(are, aim, dt_, bre, bim, l1r_, l1i_, cr_, ci_, bbr_, bbi_, rr, ri, gbr, gbi):
        lr, li = jnp.minimum(are, A_RE_MAX), aim
        den = lr * lr + li * li
        dbr, dbi = _cmul(cr_, -ci_, gbr, gbi)
        gcr, gci = _cmul(bre, -bim, gbr, gbi)
        gcr, gci = jnp.sum(gcr, axis=1, keepdims=True), jnp.sum(gci, axis=1, keepdims=True)
        qr, qi = _cmul(bbr_, -bbi_, gbr, gbi)
        qr = rr - jnp.sum(qr, axis=1, keepdims=True)
        qi = ri - jnp.sum(qi, axis=1, keepdims=True)
        tr, ti = _cmul(gcr, gci, lr / den, li / den)
        ur, ui = _cmul(l1r_, -l1i_, tr, ti)
        gwr, gwi = qr + ur, qi + ui
        vr, vi = _cmul(cr_, -ci_, lr / den, li / den)
        vr, vi = _cmul(gcr, gci, vr, vi)
        glr, gli = dt_ * gwr - vr, dt_ * gwi - vi
        return jnp.where(are < A_RE_MAX, glr, 0.0), gli, (gwr * lr + gwi * li) * dt_, dbr, dbi

    return _rowwise(fn, [a_re, a_im, dt, b_re, b_im, l1r, l1i, cr, ci, bbr, bbi, r_re, r_im, gb_re, gb_im], [],
                    [(1, F32), (1, F32), (1, F32), (c, F32), (c, F32)], name="s5_param_grads", tm=1024)


def _s5_scan(v, win_re, win_im, tabs, wo_re, wo_im, *, reverse, name, t_chunk=256, ride=None):
    seq, width = v.shape
    n_tiles, n_state = width // U_TILE, width * (SSM_P // SSM_C)
    t_chunk = _tile(seq, t_chunk)
    n_chunks, n_blk = seq // t_chunk, t_chunk // SUBLANES
    last_row = 0 if reverse else SUBLANES - 1

    def chunk_of(j):
        return (n_chunks - 1 - j) if reverse else j

    def body(v_ref, wir_ref, wii_ref, tab_ref, wor_ref, woi_ref, sr_ref, si_ref, y_ref, carry, wr, wi):
        @pl.when(pl.program_id(0) == 0)
        def _():
            carry[...] = jnp.zeros_like(carry)

        for j0 in range(0, n_tiles, TILES_TOGETHER):
            tiles = list(range(j0, min(j0 + TILES_TOGETHER, n_tiles)))
            lanes = [slice(jt * ST_TILE, (jt + 1) * ST_TILE) for jt in tiles]
            for w, (jt, ls) in enumerate(zip(tiles, lanes)):
                vj = v_ref[:, jt * U_TILE:(jt + 1) * U_TILE].astype(BF16)
                xr = _dot(vj, wir_ref[jt]).reshape(n_blk, SUBLANES, ST_TILE)
                xi = _dot(vj, wii_ref[jt]).reshape(n_blk, SUBLANES, ST_TILE)
                for s, k in enumerate((1, 2, 4)):
                    sh = (SUBLANES - k) if reverse else k
                    tr, ti = pltpu.roll(xr, sh, 1), pltpu.roll(xi, sh, 1)
                    lr, li = tab_ref[2 * s, :, ls][None], tab_ref[2 * s + 1, :, ls][None]
                    xr, xi = xr + lr * tr - li * ti, xi + lr * ti + li * tr
                wr[w] = xr.reshape(t_chunk, ST_TILE)
                wi[w] = xi.reshape(t_chunk, ST_TILE)
            powers = [(tab_ref[6, :, ls], tab_ref[7, :, ls]) for ls in lanes]

            def blk(b, c, powers=powers):
                bb = (n_blk - 1 - b) if reverse else b
                rows = pl.ds(pl.multiple_of(bb * SUBLANES, SUBLANES), SUBLANES)
                out = []
                for w, ((cr, ci), (lr, li)) in enumerate(zip(c, powers)):
                    xr = wr[w, rows, :] + lr * cr - li * ci
                    xi = wi[w, rows, :] + lr * ci + li * cr
                    wr[w, rows, :], wi[w, rows, :] = xr, xi
                    shape = (SUBLANES, ST_TILE)
                    out.append((jnp.broadcast_to(xr[last_row:last_row + 1], shape),
                                jnp.broadcast_to(xi[last_row:last_row + 1], shape)))
                return tuple(out)

            ends = lax.fori_loop(0, n_blk, blk, tuple((carry[0, :, ls], carry[1, :, ls]) for ls in lanes), unroll=2)
            for w, (jt, ls) in enumerate(zip(tiles, lanes)):
                carry[0, :, ls], carry[1, :, ls] = ends[w]
                xr_b, xi_b = wr[w].astype(BF16), wi[w].astype(BF16)
                sr_ref[:, ls], si_ref[:, ls] = xr_b, xi_b
                y_ref[:, jt * U_TILE:(jt + 1) * U_TILE] = _dot(xr_b, wor_ref[jt]) + _dot(xi_b, woi_ref[jt])

    whole = lambda a: pl.BlockSpec(a.shape, lambda j, nd=a.ndim: (0,) * nd)
    st_spec = pl.BlockSpec((t_chunk, n_state), lambda j: (chunk_of(j), 0))
    v_spec = pl.BlockSpec((t_chunk, width), lambda j: (chunk_of(j), 0))
    first, last = _grid_ends((n_chunks,))
    return _call(
        body, name=name, grid=(n_chunks,),
        in_specs=[v_spec, whole(win_re), whole(win_im), whole(tabs), whole(wo_re), whole(wo_im)],
        out_specs=[st_spec, st_spec, v_spec],
        out_shape=[jax.ShapeDtypeStruct((seq, n_state), BF16)] * 2 + [jax.ShapeDtypeStruct((seq, width), F32)],
        scratch_shapes=[pltpu.VMEM((2, SUBLANES, n_state), F32), pltpu.VMEM((TILES_TOGETHER, t_chunk, ST_TILE), F32),
                        pltpu.VMEM((TILES_TOGETHER, t_chunk, ST_TILE), F32)],
        args=(v, win_re, win_im, tabs, wo_re, wo_im), ride=ride, first=first, last=last)


def _s5_reduce(x_re, x_im, a_re, a_im, u, dy, *, name, t_chunk=512, ride=None):
    seq, n_state = x_re.shape
    width = u.shape[1]
    n_tiles = width // U_TILE
    t_chunk = _tile(seq, t_chunk)

    def body(xr_ref, xi_ref, ar_ref, ai_ref, u_ref, dy_ref, rr_ref, ri_ref, gbr_ref, gbi_ref, gcr_ref, gci_ref):
        xrb, xib, arb, aib = xr_ref[...], xi_ref[...], ar_ref[...], ai_ref[...]
        xr, xi, ar, ai = xrb.astype(F32), xib.astype(F32), arb.astype(F32), aib.astype(F32)
        ub, dyb = u_ref[...].astype(BF16), dy_ref[...].astype(BF16)
        parts = (jnp.sum(ar * xr + ai * xi, axis=0, keepdims=True), jnp.sum(ai * xr - ar * xi, axis=0, keepdims=True),
                 _dot(arb, ub, _TN), _dot(aib, ub, _TN), _dot(xrb, dyb, _TN), _dot(xib, dyb, _TN))
        first = pl.program_id(1) == 0
        for ref, val in zip((rr_ref, ri_ref, gbr_ref, gbi_ref, gcr_ref, gci_ref), parts):
            @pl.when(first)
            def _():
                ref[...] = val

            @pl.when(jnp.logical_not(first))
            def _():
                ref[...] += val

    st_spec = pl.BlockSpec((t_chunk, ST_TILE), lambda j, t: (t, j))
    u_spec = pl.BlockSpec((t_chunk, U_TILE), lambda j, t: (t, j))
    r_spec = pl.BlockSpec((1, ST_TILE), lambda j, t: (0, j))
    g_spec = pl.BlockSpec((None, ST_TILE, U_TILE), lambda j, t: (j, 0, 0))
    first, last = _grid_ends((n_tiles, seq // t_chunk))
    return _call(
        body, name=name, grid=(n_tiles, seq // t_chunk),
        in_specs=[st_spec] * 4 + [u_spec] * 2,
        out_specs=[r_spec, r_spec] + [g_spec] * 4,
        out_shape=[jax.ShapeDtypeStruct((1, n_state), F32)] * 2
        + [jax.ShapeDtypeStruct((n_tiles, ST_TILE, U_TILE), F32)] * 4,
        scratch_shapes=[], args=(x_re, x_im, a_re, a_im, u, dy), ride=ride, first=first, last=last)


def _block_diag_in(ms):
    m = jnp.stack(ms)
    n, g, c, p = m.shape
    m5 = m.reshape(n, g // GROUPS_PER_TILE, GROUPS_PER_TILE, c, p)
    eye = jnp.eye(GROUPS_PER_TILE, dtype=m.dtype)
    out = m5[:, :, :, :, None, :] * eye[None, None, :, None, :, None]
    return out.astype(BF16).reshape(n, g // GROUPS_PER_TILE, GROUPS_PER_TILE * c, GROUPS_PER_TILE * p)


def _block_diag_take(m, c, p):
    t = m.shape[0]
    m5 = m.reshape(t, GROUPS_PER_TILE, p, GROUPS_PER_TILE, c)
    idx = jnp.arange(GROUPS_PER_TILE)
    return m5[:, idx, :, idx, :].transpose(1, 0, 2, 3).reshape(t * GROUPS_PER_TILE, p, c)


def _scan_tables(pw_re, pw_im, reverse):
    row = jnp.arange(SUBLANES)[:, None]
    tabs = []
    for k in (1, 2, 4):
        keep = (row <= SUBLANES - 1 - k) if reverse else (row >= k)
        tabs += [jnp.where(keep, pw_re[k - 1][None, :], 0.0), jnp.where(keep, pw_im[k - 1][None, :], 0.0)]
    order = jnp.arange(SUBLANES)[::-1] if reverse else jnp.arange(SUBLANES)
    tabs += [pw_re[order], pw_im[order]]
    return jnp.stack(tabs)


def _partial_sums(slabs, from_sibling, names):
    x, y, c = _mesh_place()
    theirs = jnp.stack([_slab(px, py, c) for px, py in _chips(x, y)[1:]]).astype(jnp.int32)
    out = []
    for s, f, n in zip(slabs, from_sibling, names):
        rows, cols = s.shape[1:]
        tr = _tile(rows, 512)

        def body(idx_ref, a_ref, b_ref, o_ref):
            o_ref[...] = (a_ref[...] + b_ref[...]).astype(BF16)

        out.append(pl.pallas_call(
            body, name=f"reduce_add_{n}",
            grid_spec=pltpu.PrefetchScalarGridSpec(
                num_scalar_prefetch=1, grid=(3, rows // tr),
                in_specs=[pl.BlockSpec((None, tr, cols), lambda k, i, idx: (idx[k], i, 0)),
                          pl.BlockSpec((None, tr, cols), lambda k, i, idx: (k + 1, i, 0))],
                out_specs=pl.BlockSpec((None, tr, cols), lambda k, i, idx: (k, i, 0))),
            out_shape=jax.ShapeDtypeStruct((3, rows, cols), BF16), compiler_params=_params(),
        )(theirs, s, f))
    return out


def _local_step(x, target, p, shards):
    seq, d_model = x.shape
    a_width = p["g_out_attn"].shape[-1]
    s_width = p["g_out_ssm"].shape[-1]
    n_heads = a_width // HEAD_DIM
    n_hg = n_heads // HEADS_PER_GROUP
    n_groups = s_width // SSM_C
    n_sh, in_sh = N_DEV, shards["w_in"].shape[-1]
    f_sh = shards["w_ffn_gate"].shape[-1]
    w = {}
    slab3 = lambda g, n: g.reshape(N_DEV, -1, shards[n].shape[-1])
    t2, t1 = _tile(seq, 2048), _tile(seq, 1024)
    n2, n1 = seq // t2, seq // t1

    n_col = 2 * n_groups * SSM_P
    col = lambda a: a.reshape(n_col, 1)
    a_re_c, a_im_c = col(p["ssm_a_re"]), col(p["ssm_a_im"])
    dt_c = col(jnp.broadcast_to(jnp.exp(p["ssm_log_step"])[:, :, None], (2, n_groups, SSM_P)))
    b_re_c, b_im_c = p["ssm_b_re"].reshape(n_col, SSM_C), p["ssm_b_im"].reshape(n_col, SSM_C)
    (pw_re, pw_im, cf_re, cf_im, bb_re, bb_im), got = _s5_discretize(a_re_c, a_im_c, dt_c, b_re_c, b_im_c,
                                                                     ride=_gather_first([shards["w_in"]]))

    twice = lambda f: (lambda *a: (f(*a),) * 2)
    (h1, h1_t), (w["w_in"],) = _rowwise(twice(x_norm), [x], [p["g_mix"]], [(d_model, BF16)],
                                        flipped=[(d_model, BF16, 1)], name="rms_mix", ride=_gather_second(got))
    z = _mm(h1, w["w_in"], name="in_proj", grid=(n2, n_sh),
            a_spec=pl.BlockSpec((t2, d_model), lambda i, j: (i, 0)),
            b_spec=pl.BlockSpec((None, d_model, in_sh), lambda i, j: (j, 0, 0)),
            o_spec=pl.BlockSpec((t2, in_sh), lambda i, j: (i, j)), o_shape=(seq, n_sh * in_sh), dims="nn")
    qg4 = jnp.tile(p["q_gain"], (1, HEADS_PER_GROUP))
    kg4 = jnp.tile(p["k_gain"], (1, HEADS_PER_GROUP))
    btab = _bias_table(p["rpb"])
    ya, got_a = _attn_fwd(z, qg4, kg4, btab, ride=_gather_first([shards["w_ffn_gate"], shards["w_ffn_up"]]))
    u = z[:, 3 * a_width:]
    n_state = n_groups * SSM_P
    pw_re = pw_re.reshape(2, n_state, SUBLANES).transpose(0, 2, 1)
    pw_im = pw_im.reshape(2, n_state, SUBLANES).transpose(0, 2, 1)
    bb_re4, bb_im4 = bb_re.reshape(2, n_groups, SSM_P, SSM_C), bb_im.reshape(2, n_groups, SSM_P, SSM_C)
    c_re, c_im = p["ssm_c_re"], p["ssm_c_im"]
    t21 = lambda a: a.transpose(0, 2, 1)
    maps_in = _block_diag_in([m for d in range(2) for m in (t21(bb_re4[d]), t21(bb_im4[d]), c_re[d], -c_im[d])])
    maps_out = _block_diag_in([m for d in range(2) for m in (t21(c_re[d]), -t21(c_im[d]), bb_re4[d], bb_im4[d])])
    fwd, bwd_in = [], []
    got_b = None
    for d in range(2):
        rev = d == 1
        tabs = _scan_tables(pw_re[d], pw_im[d], rev)
        if d == 0:
            ride = _gather_first([shards["w_glu"], shards["w_out"]])
        else:
            ride = _gather_second(got_a + got_b) + _gather_first([shards["w_ffn_down"]])
        (xs_re, xs_im, y_d), got = _s5_scan(u, maps_in[4 * d], maps_in[4 * d + 1], tabs, maps_out[4 * d],
                                            maps_out[4 * d + 1], reverse=rev, name=f"s5_fwd_{d}", ride=ride)
        if d == 0:
            got_b = got
        fwd.append((xs_re, xs_im, y_d))
        bwd_in.append((maps_in[4 * d + 2], maps_in[4 * d + 3], _scan_tables(pw_re[d], -pw_im[d], not rev),
                       maps_out[4 * d + 2], maps_out[4 * d + 3]))
    w["w_gate"], w["w_up"], w_glu_full, w_out_full, w_down_first = got
    w["w_glu"] = w_glu_full.reshape(-1, s_width)
    w["w_out"] = w_out_full.reshape(-1, d_model)

    ypre, yg, yg_t = _rowwise(s5_mid, [fwd[0][2], fwd[1][2], u], [p["ssm_d"]], [(s_width, F32), (s_width, F32)],
                              flipped=[(s_width, BF16, 1)], name="s5_skip_gelu")
    t_glu = _mm_plain(yg, w["w_glu"], "nn", name="glu_proj", tn=s_width)
    y_cat, y_cat_t = _rowwise(twice(mix_out_fwd), [ya, yg, t_glu], [p["b_glu"], p["g_out_attn"], p["g_out_ssm"]],
                              [(a_width + s_width, BF16)], flipped=[(a_width + s_width, BF16, 1)], name="mix_out")
    x1, (w["w_down"],) = _mm_plain(y_cat, w["w_out"], "nn", name="out_proj", res=x, tn=2048,
                                   ride=_gather_second([w_down_first]))

    h2, h2_t = _rowwise(twice(x_norm), [x1], [p["g_ffn"]], [(d_model, BF16)], flipped=[(d_model, BF16, 1)],
                        name="rms_ffn")
    ffn_up = functools.partial(
        _mm, grid=(n2, n_sh), a_spec=pl.BlockSpec((t2, d_model), lambda i, j: (i, 0)),
        b_spec=pl.BlockSpec((None, d_model, f_sh), lambda i, j: (j, 0, 0)),
        o_spec=pl.BlockSpec((None, t2, f_sh), lambda i, j: (j, i, 0)), o_shape=(n_sh, seq, f_sh), dims="nn",
        out_dtype=BF16)
    gate = ffn_up(h2, w["w_gate"], name="ffn_gate")
    up = ffn_up(h2, w["w_up"], name="ffn_up")
    flat = lambda a: a.reshape(n_sh * seq, f_sh)
    act_t = _rowwise(swiglu_fwd, [flat(gate), flat(up)], [], [], flipped=[(f_sh, BF16, n_sh)], name="swiglu",
                     tm=1024)[0].reshape(n_sh, f_sh, seq)
    ffn_out = _mm(act_t, w["w_down"], name="ffn_down", grid=(n1, n_sh // 2), groups=2,
                  a_spec=pl.BlockSpec((2, f_sh, t1), lambda i, j: (j, 0, i)),
                  b_spec=pl.BlockSpec((2, f_sh, d_model), lambda i, j: (j, 0, 0)),
                  o_spec=pl.BlockSpec((t1, d_model), lambda i, j: (i, 0)), o_shape=(seq, d_model), dims="tn",
                  k_axis=1)

    dx2, dx2_b, sq = _rowwise(functools.partial(loss_head, inv_d=1.0 / d_model), [ffn_out, x1, target], [],
                              [(d_model, F32), (d_model, BF16)], [d_model], name="loss_head")
    loss = 0.5 * jnp.sum(sq) / d_model

    d_act = _mm(dx2_b, w["w_down"], name="ffn_down_dx", grid=(n2, n_sh),
                a_spec=pl.BlockSpec((t2, d_model), lambda i, j: (i, 0)),
                b_spec=pl.BlockSpec((None, f_sh, d_model), lambda i, j: (j, 0, 0)),
                o_spec=pl.BlockSpec((None, t2, f_sh), lambda i, j: (j, i, 0)), o_shape=(n_sh, seq, f_sh), dims="nt",
                out_dtype=BF16)
    g_w_down = _mm(act_t, dx2_b, name="ffn_down_dw", grid=(n_sh, n2),
                   a_spec=pl.BlockSpec((None, f_sh, t2), lambda j, k: (j, 0, k)),
                   b_spec=pl.BlockSpec((t2, d_model), lambda j, k: (k, 0)),
                   o_spec=pl.BlockSpec((None, f_sh, d_model), lambda j, k: (j, 0, 0)),
                   o_shape=(n_sh, f_sh, d_model), dims="nn", k_axis=1)
    d_gate, d_up = _rowwise(swiglu_bwd, [flat(d_act), flat(gate), flat(up)], [], [(f_sh, BF16), (f_sh, BF16)],
                            name="swiglu_bwd", tm=1024)
    d_gate, d_up = d_gate.reshape(n_sh, seq, f_sh), d_up.reshape(n_sh, seq, f_sh)
    d_h2 = _mm(d_gate, w["w_gate"], second=(d_up, w["w_up"]), name="ffn_up_gate_dx", grid=(n1, n_sh),
               a_spec=pl.BlockSpec((None, t1, f_sh), lambda i, j: (j, i, 0)),
               b_spec=pl.BlockSpec((None, d_model, f_sh), lambda i, j: (j, 0, 0)),
               o_spec=pl.BlockSpec((t1, d_model), lambda i, j: (i, 0)), o_shape=(seq, d_model), dims="nt", k_axis=1)
    ffn_dw = functools.partial(
        _mm, grid=(n_sh, n2), a_spec=pl.BlockSpec((d_model, t2), lambda j, k: (0, k)),
        b_spec=pl.BlockSpec((None, t2, f_sh), lambda j, k: (j, k, 0)),
        o_spec=pl.BlockSpec((None, d_model, f_sh), lambda j, k: (j, 0, 0)), o_shape=(n_sh, d_model, f_sh), dims="nn",
        k_axis=1)
    g_w_gate = ffn_dw(h2_t, d_gate, name="ffn_gate_dw")
    g_w_up = ffn_dw(h2_t, d_up, name="ffn_up_dw")
    dx1, g_g_ffn = _rowwise(residual_rms_bwd, [dx2, d_h2, x1], [p["g_ffn"]], [(d_model, F32)], [d_model],
                            name="rms_ffn_bwd")

    d_ycat = _mm_plain(dx1, w["w_out"], "nt", name="out_proj_dx", tn=2048)
    mix_w = a_width + s_width
    tm_o = _tile(mix_w, 1024)
    g_w_out = _mm(y_cat_t, dx1, name="out_proj_dw", grid=(mix_w // tm_o, n1),
                  a_spec=pl.BlockSpec((tm_o, t1), lambda i, k: (i, k)),
                  b_spec=pl.BlockSpec((t1, d_model), lambda i, k: (k, 0)),
                  o_spec=pl.BlockSpec((tm_o, d_model), lambda i, k: (i, 0)), o_shape=(mix_w, d_model), dims="nn",
                  k_axis=1)
    (d_ya, d_yg_direct, d_t, g_goa, g_gos, g_b_glu) = _rowwise(
        functools.partial(mix_out_bwd, a_width=a_width), [d_ycat, ya, yg, t_glu],
        [p["b_glu"], p["g_out_attn"], p["g_out_ssm"]],
        [(a_width, F32), (s_width, F32), (s_width, BF16)], [a_width, s_width, s_width], name="mix_out_bwd")
    d_yg = _mm_plain(d_t, w["w_glu"], "nt", name="glu_proj_dx", res=d_yg_direct, tn=s_width)
    g_w_glu = _mm(yg_t, d_t, name="glu_proj_dw", grid=(1, n1),
                  a_spec=pl.BlockSpec((s_width, t1), lambda i, k: (0, k)),
                  b_spec=pl.BlockSpec((t1, s_width), lambda i, k: (k, 0)),
                  o_spec=pl.BlockSpec((s_width, s_width), lambda i, k: (0, 0)), o_shape=(s_width, s_width),
                  dims="nn", k_axis=1)
    d_ypre, du_skip, g_ssm_d = _rowwise(gelu_skip_bwd, [d_yg, ypre, u], [p["ssm_d"]],
                                        [(s_width, F32), (s_width, F32)], [s_width], name="s5_skip_gelu_bwd")

    ffn_names, mix_names = ("w_ffn_gate", "w_ffn_up", "w_ffn_down"), ("w_glu", "w_out")
    ffn_slabs = [slab3(g, n) for g, n in zip((g_w_gate, g_w_up, g_w_down), ffn_names)]
    mix_slabs = [slab3(g, n) for g, n in zip((g_w_glu, g_w_out), mix_names)]
    du_dirs, adj, r_parts, gb_parts, gc_parts = [], [], [], [], []
    sib, part = {}, {}
    for d, (names, slabs) in enumerate(((ffn_names, ffn_slabs), (mix_names, mix_slabs))):
        win_re, win_im, tabs, wo_re, wo_im = bwd_in[d]
        (as_re, as_im, du_d), got = _s5_scan(d_ypre, win_re, win_im, tabs, wo_re, wo_im, reverse=(d == 0),
                                             name=f"s5_bwd_{d}", ride=_reduce_sibling(slabs))
        du_dirs.append(du_d)
        adj.append((as_re, as_im))
        sib[names] = got
        part[names] = _partial_sums(slabs, got, names)
    for d in range(2):
        (r_re, r_im, gbt_re, gbt_im, gct_re, gct_im), got = _s5_reduce(
            fwd[d][0], fwd[d][1], adj[d][0], adj[d][1], u, d_ypre, name=f"s5_reduce_{d}",
            ride=_reduce_chips(part[mix_names] if d == 0 else part[ffn_names][2:]))
        if d == 0:
            mix_chips = got
        else:
            down_chips = got
        r_parts.append((r_re.reshape(n_state, 1), r_im.reshape(n_state, 1)))
        gb_parts.append((_block_diag_take(gbt_re, SSM_C, SSM_P), _block_diag_take(gbt_im, SSM_C, SSM_P)))
        gc_parts.append((_block_diag_take(gct_re, SSM_C, SSM_P), _block_diag_take(gct_im, SSM_C, SSM_P)))
    cat = lambda i, parts: jnp.concatenate([parts[0][i], parts[1][i]], axis=0)
    gbb_re, gbb_im = cat(0, gb_parts).reshape(n_col, SSM_C), cat(1, gb_parts).reshape(n_col, SSM_C)
    g_a_re, g_a_im, g_ls, g_b_re, g_b_im = _s5_param_grads(
        a_re_c, a_im_c, dt_c, b_re_c, b_im_c, pw_re[:, 0].reshape(n_col, 1), pw_im[:, 0].reshape(n_col, 1),
        cf_re, cf_im, bb_re, bb_im, cat(0, r_parts), cat(1, r_parts), gbb_re, gbb_im)
    g_c_re = cat(0, gc_parts).reshape(2, n_groups, SSM_P, SSM_C).transpose(0, 1, 3, 2)
    g_c_im = -cat(1, gc_parts).reshape(2, n_groups, SSM_P, SSM_C).transpose(0, 1, 3, 2)

    (d_q, d_k, d_v, d_btab, g_qg, g_kg), ffn_chips = _attn_bwd(z, d_ya, qg4, kg4, btab,
                                                                ride=_reduce_chips(part[ffn_names][:2]))
    ffn_chips = ffn_chips + down_chips
    d_u = _rowwise(lambda a, b, c: a + b + c, [du_dirs[0], du_dirs[1], du_skip], [], [(s_width, BF16)],
                   name="s5_du_sum")[0]
    d_z = jnp.concatenate([d_q, d_k, d_v, d_u], axis=1)
    fold_heads = lambda g: g.reshape(n_heads, HEAD_DIM).sum(axis=0, keepdims=True)
    small = {
        "q_gain": fold_heads(g_qg), "k_gain": fold_heads(g_kg), "rpb": _bias_grad(d_btab, n_heads),
        "ssm_a_re": g_a_re.reshape(2, n_groups, SSM_P), "ssm_a_im": g_a_im.reshape(2, n_groups, SSM_P),
        "ssm_b_re": g_b_re.reshape(2, n_groups, SSM_P, SSM_C), "ssm_b_im": g_b_im.reshape(2, n_groups, SSM_P, SSM_C),
        "ssm_c_re": g_c_re, "ssm_c_im": g_c_im,
        "ssm_log_step": g_ls.reshape(2, n_groups, SSM_P).sum(axis=-1),
        "ssm_d": g_ssm_d, "b_glu": g_b_glu, "g_out_attn": g_goa, "g_out_ssm": g_gos, "g_ffn": g_g_ffn,
    }
    g_w_in, got = _mm(h1_t, d_z, name="in_proj_dw", grid=(n_sh, n2),
                      a_spec=pl.BlockSpec((d_model, t2), lambda j, k: (0, k)),
                      b_spec=pl.BlockSpec((t2, in_sh), lambda j, k: (k, j)),
                      o_spec=pl.BlockSpec((None, d_model, in_sh), lambda j, k: (j, 0, 0)),
                      o_shape=(n_sh, d_model, in_sh), dims="nn", k_axis=1,
                      ride=_gather_first([_as_rows(small[n]) for n in SMALL_LATE]))
    d_h1, got = _mm(d_z, w["w_in"], name="in_proj_dx", grid=(n1, n_sh // 2), groups=2,
                    a_spec=pl.BlockSpec((t1, 2 * in_sh), lambda i, j: (i, j)),
                    b_spec=pl.BlockSpec((2, d_model, in_sh), lambda i, j: (j, 0, 0)),
                    o_spec=pl.BlockSpec((t1, d_model), lambda i, j: (i, 0)), o_shape=(seq, d_model), dims="nt",
                    k_axis=1, ride=_gather_second(got) + _reduce_sibling([g_w_in]))
    small_gathered, in_sibling = dict(zip(SMALL_LATE, got[:-1])), got[-1]
    in_part = _partial_sums([g_w_in], [in_sibling], ("w_in",))
    (grad_x, g_g_mix), (in_chips,) = _rowwise(residual_rms_bwd, [dx1, d_h1, x], [p["g_mix"]], [(d_model, F32)],
                                              [d_model], name="rms_mix_bwd", ride=_reduce_chips(in_part))
    reduced = {"w_in": (g_w_in, in_sibling, in_chips)}
    for names, slabs, chips in ((ffn_names, ffn_slabs, ffn_chips), (mix_names, mix_slabs, mix_chips)):
        for i, n in enumerate(names):
            reduced[n] = (slabs[i], sib[names][i], chips[i])
    return loss, grad_x, small_gathered, g_g_mix, reduced


def x_norm(xv, g):
    return xv * _rstd(xv) * g


def s5_mid(y0, y1, uv, d_skip):
    ypre = y0 + y1 + d_skip * uv
    yg = _gelu(ypre)
    return ypre, yg, yg


def mix_out_fwd(ya, yg, t, b_glu, g_oa, g_os):
    ys = yg * _sigmoid(t + b_glu)
    return jnp.concatenate([ya * _rstd(ya) * g_oa, ys * _rstd(ys) * g_os], axis=1)


def mix_out_bwd(d_y, ya, yg, t, b_glu, g_oa, g_os, *, a_width):
    sg = _sigmoid(t + b_glu)
    ys = yg * sg
    d_ya, c_goa = _rms_bwd(d_y[:, :a_width], ya, g_oa)
    d_ys, c_gos = _rms_bwd(d_y[:, a_width:], ys, g_os)
    d_t = d_ys * yg * sg * (1.0 - sg)
    return d_ya, d_ys * sg, d_t, c_goa, c_gos, d_t


def gelu_skip_bwd(d_yg, ypre, uv, d_skip):
    d_ypre = d_yg * _gelu_grad(ypre)
    return d_ypre, d_ypre * d_skip, d_ypre * uv


def swiglu_fwd(gv, uv):
    gv, uv = gv.astype(F32), uv.astype(F32)
    return gv * _sigmoid(gv) * uv


def swiglu_bwd(d_act, gv, uv):
    d_act, gv, uv = d_act.astype(F32), gv.astype(F32), uv.astype(F32)
    sg = _sigmoid(gv)
    return d_act * uv * (sg * (1.0 + gv * (1.0 - sg))), d_act * gv * sg


def loss_head(ffn_out, x1, target, *, inv_d):
    diff = ffn_out + x1 - target
    return diff * inv_d, diff * inv_d, diff * diff


def residual_rms_bwd(d_res, d_h, xv, g):
    dx, c_g = _rms_bwd(d_h, xv, g)
    return d_res + dx, c_g


_ANY = pl.BlockSpec(memory_space=pl.ANY)


def _mesh_place():
    return lax.axis_index("x"), lax.axis_index("y"), lax.axis_index("c")


def _chips(x, y):
    return [(x, y), (1 - x, y), (x, 1 - y), (1 - x, 1 - y)]


def _slab(px, py, pc):
    return 4 * px + 2 * py + pc


def _all_gather(arrs, *, name):
    n = len(arrs)

    def body(*refs):
        in_refs, out_refs = refs[:n], refs[n:2 * n]
        send_sems, recv_sems, local_sems = refs[2 * n:]
        x, y, c = _mesh_place()
        me, sibling = (x, y, c), (x, y, 1 - c)
        others = _chips(x, y)[1:]

        def copy(w, k, block, to, src=None):
            dst = out_refs[w].at[_slab(*block)]
            return pltpu.make_async_remote_copy(
                src_ref=dst if src is None else src, dst_ref=dst, send_sem=send_sems.at[7 * w + k],
                recv_sem=recv_sems.at[7 * w + k], device_id=to, device_id_type=MESH)

        mine = [pltpu.make_async_copy(in_refs[w], out_refs[w].at[_slab(*me)], local_sems.at[w]) for w in range(n)]
        first = []
        for w in range(n):
            mine[w].start()
            first.append(copy(w, 0, me, sibling, src=in_refs[w]))
            first += [copy(w, 1 + j, me, (*chip, c), src=in_refs[w]) for j, chip in enumerate(others)]
        for cp in first:
            cp.start()
        passed = []
        for j, chip in enumerate(others):
            for w in range(n):
                copy(w, 1 + j, (*chip, c), me).wait_recv()
                fwd = copy(w, 4 + j, (*chip, c), sibling)
                fwd.start()
                passed.append(fwd)
        for w in range(n):
            copy(w, 0, sibling, me).wait_recv()
        for j, chip in enumerate(others):
            for w in range(n):
                copy(w, 4 + j, (*chip, 1 - c), me).wait_recv()
        for cp in first + passed:
            cp.wait_send()
        for cp in mine:
            cp.wait()

    return pl.pallas_call(
        body, name=name, in_specs=[_ANY] * n, out_specs=[_ANY] * n,
        out_shape=[jax.ShapeDtypeStruct((N_DEV,) + a.shape, a.dtype) for a in arrs],
        scratch_shapes=[pltpu.SemaphoreType.DMA((7 * n,)), pltpu.SemaphoreType.DMA((7 * n,)),
                        pltpu.SemaphoreType.DMA((n,))],
        compiler_params=pltpu.CompilerParams(has_side_effects=True),
    )(*arrs)


def _adamw(w, m, v, parts, *, name, slab, tr=256):
    rows, cols = w.shape
    tr = _tile(rows, tr)
    n_p = len(parts)

    def body(slab_ref, *refs):
        w_ref, m_ref, v_ref = refs[:3]
        p_refs = refs[3:3 + n_p]
        g_ref, d_ref, nm_ref, nv_ref = refs[3 + n_p:]
        g = None
        for (_, lead), r in zip(parts, p_refs):
            for piece in ([r[...]] if lead is None else [r[i] for i in range(lead)]):
                g = piece.astype(F32) if g is None else g + piece.astype(F32)
        new_m = ADAM_B1 * m_ref[...] + (1.0 - ADAM_B1) * g
        new_v = ADAM_B2 * v_ref[...] + (1.0 - ADAM_B2) * (g * g)
        m_hat = new_m / (1.0 - ADAM_B1 ** ADAM_STEP)
        v_hat = new_v / (1.0 - ADAM_B2 ** ADAM_STEP)
        g_ref[...] = g
        d_ref[...] = -ADAM_LR * (m_hat / (jnp.sqrt(v_hat) + ADAM_EPS) + ADAM_WD * w_ref[...])
        nm_ref[...] = new_m
        nv_ref[...] = new_v

    tile = pl.BlockSpec((tr, cols), lambda i, s: (i, 0))
    p_specs = [pl.BlockSpec((None, tr, cols), lambda i, s: (s[0], i, 0)) if lead is None
               else pl.BlockSpec((lead, tr, cols), lambda i, s: (0, i, 0)) for _, lead in parts]
    return pl.pallas_call(
        body, name=name,
        grid_spec=pltpu.PrefetchScalarGridSpec(num_scalar_prefetch=1, grid=(rows // tr,),
                                               in_specs=[tile] * 3 + p_specs, out_specs=[tile] * 4),
        out_shape=[jax.ShapeDtypeStruct((rows, cols), F32)] * 4, compiler_params=_params(),
    )(jnp.reshape(slab, (1,)).astype(jnp.int32), w, m, v, *[a for a, _ in parts])


_DENSE_MIN = 256 * 128


def _as_rows(a):
    if a.shape[-1] < 128 and a.size >= _DENSE_MIN and a.size % 128 == 0:
        return a.reshape(-1, 128)
    return a.reshape(-1, a.shape[-1])


BIG = ("w_in", "w_glu", "w_out", "w_ffn_gate", "w_ffn_up", "w_ffn_down")
WEIGHTS = ("g_mix", "w_in", "q_gain", "k_gain", "rpb", "ssm_a_re", "ssm_a_im", "ssm_b_re", "ssm_b_im", "ssm_c_re",
           "ssm_c_im", "ssm_log_step", "ssm_d", "w_glu", "b_glu", "g_out_attn", "g_out_ssm", "w_out", "g_ffn",
           "w_ffn_gate", "w_ffn_up", "w_ffn_down")
SMALL = tuple(n for n in WEIGHTS if n not in BIG)
SMALL_LATE = tuple(n for n in SMALL if n != "g_mix")
VECTORS = ("g_mix", "q_gain", "k_gain", "ssm_d", "b_glu", "g_out_attn", "g_out_ssm", "g_ffn")


def kernel(x, g_mix, w_in, q_gain, k_gain, rpb, ssm_a_re, ssm_a_im, ssm_b_re, ssm_b_im, ssm_c_re, ssm_c_im, ssm_log_step, ssm_d, w_glu, b_glu, g_out_attn, g_out_ssm, w_out, g_ffn, w_ffn_gate, w_ffn_up, w_ffn_down, loss_target, m_g_mix, m_w_in, m_q_gain, m_k_gain, m_rpb, m_ssm_a_re, m_ssm_a_im, m_ssm_b_re, m_ssm_b_im, m_ssm_c_re, m_ssm_c_im, m_ssm_log_step, m_ssm_d, m_w_glu, m_b_glu, m_g_out_attn, m_g_out_ssm, m_w_out, m_g_ffn, m_w_ffn_gate, m_w_ffn_up, m_w_ffn_down, v_g_mix, v_w_in, v_q_gain, v_k_gain, v_rpb, v_ssm_a_re, v_ssm_a_im, v_ssm_b_re, v_ssm_b_im, v_ssm_c_re, v_ssm_c_im, v_ssm_log_step, v_ssm_d, v_w_glu, v_b_glu, v_g_out_attn, v_g_out_ssm, v_w_out, v_g_ffn, v_w_ffn_gate, v_w_ffn_up, v_w_ffn_down):
    wts = dict(g_mix=g_mix, w_in=w_in, q_gain=q_gain, k_gain=k_gain, rpb=rpb, ssm_a_re=ssm_a_re, ssm_a_im=ssm_a_im,
               ssm_b_re=ssm_b_re, ssm_b_im=ssm_b_im, ssm_c_re=ssm_c_re, ssm_c_im=ssm_c_im, ssm_log_step=ssm_log_step,
               ssm_d=ssm_d, w_glu=w_glu, b_glu=b_glu, g_out_attn=g_out_attn, g_out_ssm=g_out_ssm, w_out=w_out,
               g_ffn=g_ffn, w_ffn_gate=w_ffn_gate, w_ffn_up=w_ffn_up, w_ffn_down=w_ffn_down)
    mom = dict(g_mix=m_g_mix, w_in=m_w_in, q_gain=m_q_gain, k_gain=m_k_gain, rpb=m_rpb, ssm_a_re=m_ssm_a_re,
               ssm_a_im=m_ssm_a_im, ssm_b_re=m_ssm_b_re, ssm_b_im=m_ssm_b_im, ssm_c_re=m_ssm_c_re,
               ssm_c_im=m_ssm_c_im, ssm_log_step=m_ssm_log_step, ssm_d=m_ssm_d, w_glu=m_w_glu, b_glu=m_b_glu,
               g_out_attn=m_g_out_attn, g_out_ssm=m_g_out_ssm, w_out=m_w_out, g_ffn=m_g_ffn,
               w_ffn_gate=m_w_ffn_gate, w_ffn_up=m_w_ffn_up, w_ffn_down=m_w_ffn_down)
    var = dict(g_mix=v_g_mix, w_in=v_w_in, q_gain=v_q_gain, k_gain=v_k_gain, rpb=v_rpb, ssm_a_re=v_ssm_a_re,
               ssm_a_im=v_ssm_a_im, ssm_b_re=v_ssm_b_re, ssm_b_im=v_ssm_b_im, ssm_c_re=v_ssm_c_re,
               ssm_c_im=v_ssm_c_im, ssm_log_step=v_ssm_log_step, ssm_d=v_ssm_d, w_glu=v_w_glu, b_glu=v_b_glu,
               g_out_attn=v_g_out_attn, g_out_ssm=v_g_out_ssm, w_out=v_w_out, g_ffn=v_g_ffn,
               w_ffn_gate=v_w_ffn_gate, w_ffn_up=v_w_ffn_up, w_ffn_down=v_w_ffn_down)
    ix, iy, ic = _mesh_place()
    me = _slab(ix, iy, ic)

    shard = {n: wts[n][0] for n in BIG}
    shard_b = {n: shard[n].astype(BF16) for n in BIG}
    p = {n: (wts[n][0].reshape(1, -1) if n in VECTORS else wts[n][0]) for n in SMALL}

    loss, grad_x, small_gathered, g_g_mix, reduced = _local_step(x[0], loss_target[0], p, shard_b)
    loss = lax.psum(loss, ("x", "y", "c"))
    out = {}
    for n in BIG:
        slabs, from_sibling, from_chips = reduced[n]
        rows, cols = slabs.shape[1:]
        res = _adamw(shard[n].reshape(rows, cols), mom[n][0].reshape(rows, cols), var[n][0].reshape(rows, cols),
                     [(slabs, None), (from_sibling, 1), (from_chips, 3)], name=f"adamw_{n}", slab=me)
        out[n] = [r.reshape(wts[n].shape) for r in res]

    small_gathered["g_mix"] = _all_gather([g_g_mix], name="gather_g_mix")[0]
    for n in SMALL:
        res = _adamw(_as_rows(wts[n]), _as_rows(mom[n]), _as_rows(var[n]), [(small_gathered[n], N_DEV)],
                     name=f"adamw_{n}", slab=me, tr=1024)
        out[n] = [r.reshape(wts[n].shape) for r in res]

    return (loss, grad_x[None], *[out[n][0] for n in WEIGHTS], *[out[n][1] for n in WEIGHTS],
            *[out[n][2] for n in WEIGHTS], *[out[n][3] for n in WEIGHTS])
```

```python
import functools
import math

import jax
import jax.numpy as jnp
from jax import lax
from jax.experimental import pallas as pl
from jax.experimental.pallas import tpu as pltpu

F32 = jnp.float32
BF16 = jnp.bfloat16

N_DEV = 8
GRID_W = 64
WIN_H = 8
WIN_W = 16
HEAD_DIM = 64
HEADS_PER_GROUP = 4
GROUP_LANES = HEADS_PER_GROUP * HEAD_DIM
SSM_C = 16
SSM_P = 64
GROUPS_PER_TILE = 8
U_TILE = GROUPS_PER_TILE * SSM_C
ST_TILE = GROUPS_PER_TILE * SSM_P
TILES_TOGETHER = 4
SUBLANES = 8
RMS_EPS = 1e-6
NEG_INF = -1e30
A_RE_MAX = -1e-4
ADAM_LR, ADAM_B1, ADAM_B2, ADAM_EPS, ADAM_WD, ADAM_STEP = 0.001, 0.9, 0.999, 1e-08, 0.01, 10
VMEM_LIMIT_V7X = 56 * 1024 * 1024
MESH = pl.DeviceIdType.MESH

_NN = (((1,), (0,)), ((), ()))
_NT = (((1,), (1,)), ((), ()))
_TN = (((0,), (0,)), ((), ()))
_DIMS = {"nn": _NN, "nt": _NT, "tn": _TN}


def _params(**kw):
    return pltpu.CompilerParams(vmem_limit_bytes=VMEM_LIMIT_V7X, **kw)


def _dot(a, b, dims=_NN):
    return lax.dot_general(a, b, dims, preferred_element_type=F32)


def _mm(a, b, *, name, grid, a_spec, b_spec, o_spec, o_shape, dims, k_axis=None, res=None, out_dtype=F32,
        exact=False, second=None, ride=None, groups=1):
    dn = _DIMS[dims]
    nk = 1 if k_axis is None else grid[k_axis]
    acc_shape = tuple(d for d in o_spec.block_shape if d is not None)
    n_in = 2 + (2 if second is not None else 0)

    def body(*refs):
        a_ref, b_ref = refs[:2]
        r_ref = refs[n_in] if res is not None else None
        o_ref, acc = refs[-2:]
        def product(x_ref, y_ref):
            if groups == 1:
                return _dot(x_ref[...].astype(BF16), y_ref[...].astype(BF16), dn)
            total, width = None, x_ref.shape[-1] // groups
            for s in range(groups):
                x = x_ref[s] if len(x_ref.shape) == 3 else x_ref[:, s * width:(s + 1) * width]
                t = _dot(x.astype(BF16), y_ref[s].astype(BF16), dn)
                total = t if total is None else total + t
            return total

        if exact:
            p = lax.dot_general(a_ref[...], b_ref[...], dn, precision=lax.Precision.HIGHEST,
                                preferred_element_type=F32)
        else:
            p = product(a_ref, b_ref)
        if second is not None:
            p = p + product(refs[2], refs[3])

        def finish(v):
            if r_ref is not None:
                v = v + r_ref[...].astype(F32)
            o_ref[...] = v.astype(out_dtype)

        if nk == 1:
            finish(p)
        else:
            k = pl.program_id(k_axis)

            @pl.when(k == 0)
            def _():
                acc[...] = p

            @pl.when(k > 0)
            def _():
                acc[...] += p

            @pl.when(k == nk - 1)
            def _():
                finish(acc[...])

    ins = [a, b] + (list(second) if second is not None else []) + ([res] if res is not None else [])
    in_specs = [a_spec, b_spec] * (n_in // 2) + ([o_spec] if res is not None else [])
    first, last = _grid_ends(grid)
    (out,), rode = _call(
        body, name=name, grid=grid, in_specs=in_specs, out_specs=[o_spec],
        out_shape=[jax.ShapeDtypeStruct(o_shape, out_dtype)],
        scratch_shapes=[pltpu.VMEM(acc_shape if nk > 1 else (SUBLANES, 128), F32)],
        args=ins, ride=ride, first=first, last=last)
    return out if ride is None else (out, rode)


def _tile(n, want):
    if n <= want:
        return n
    t = want
    while n % t:
        t //= 2
    return t


def _mm_plain(a, b, dims, *, name, res=None, out_dtype=F32, tm=512, tn=512, tk=512, exact=False, ride=None):
    if dims == "nn":
        (m, k), n = a.shape, b.shape[1]
    elif dims == "nt":
        (m, k), n = a.shape, b.shape[0]
    else:
        (k, m), n = a.shape, b.shape[1]
    tm, tn = _tile(m, tm), _tile(n, tn)
    if dims == "tn":
        tk = _tile(k, tk)
        grid = (m // tm, n // tn, k // tk)
        a_spec = pl.BlockSpec((tk, tm), lambda i, j, kk: (kk, i))
        b_spec = pl.BlockSpec((tk, tn), lambda i, j, kk: (kk, j))
        o_spec = pl.BlockSpec((tm, tn), lambda i, j, kk: (i, j))
        return _mm(a, b, name=name, grid=grid, a_spec=a_spec, b_spec=b_spec, o_spec=o_spec, o_shape=(m, n),
                   dims=dims, k_axis=2, res=res, out_dtype=out_dtype)
    grid = (n // tn, m // tm)
    a_spec = pl.BlockSpec((tm, k), lambda j, i: (i, 0))
    if dims == "nn":
        b_spec = pl.BlockSpec((k, tn), lambda j, i: (0, j))
    else:
        b_spec = pl.BlockSpec((tn, k), lambda j, i: (j, 0))
    o_spec = pl.BlockSpec((tm, tn), lambda j, i: (i, j))
    return _mm(a, b, name=name, grid=grid, a_spec=a_spec, b_spec=b_spec, o_spec=o_spec, o_shape=(m, n), dims=dims,
               res=res, out_dtype=out_dtype, exact=exact, ride=ride)


def _rowwise(fn, tiled, bcast, outs, accs=(), *, name, tm=256, flipped=(), ride=None):
    m = tiled[0].shape[0]
    tm = _tile(m, tm)
    n_t, n_b, n_o, n_f = len(tiled), len(bcast), len(outs), len(flipped)

    def body(*refs):
        ins = [r[...] for r in refs[: n_t + n_b]]
        o_refs = refs[n_t + n_b: n_t + n_b + n_o]
        f_refs = refs[n_t + n_b + n_o: n_t + n_b + n_o + n_f]
        a_refs = refs[n_t + n_b + n_o + n_f:]
        res = fn(*ins)
        if not isinstance(res, (tuple, list)):
            res = (res,)
        for r, v in zip(o_refs, res[:n_o]):
            r[...] = v.astype(r.dtype)
        for r, v in zip(f_refs, res[n_o:n_o + n_f]):
            r[...] = v.astype(F32).T.astype(r.dtype)
        first = pl.program_id(0) == 0
        for r, v in zip(a_refs, res[n_o + n_f:]):
            s = jnp.sum(v, axis=0, keepdims=True)

            @pl.when(first)
            def _():
                r[...] = s

            @pl.when(jnp.logical_not(first))
            def _():
                r[...] += s

    in_specs = [pl.BlockSpec((tm, t.shape[1]), lambda i: (i, 0)) for t in tiled]
    in_specs += [pl.BlockSpec(b.shape, lambda i, nd=b.ndim: (0,) * nd) for b in bcast]
    out_specs = [pl.BlockSpec((tm, n), lambda i: (i, 0)) for n, _ in outs]
    out_specs += [pl.BlockSpec((n, tm), lambda i, per=m // tm // g: (i // per, i % per)) for n, _, g in flipped]
    out_specs += [pl.BlockSpec((1, n), lambda i: (0, 0)) for n in accs]
    out_shape = [jax.ShapeDtypeStruct((m, n), dt) for n, dt in outs]
    out_shape += [jax.ShapeDtypeStruct((g * n, m // g), dt) for n, dt, g in flipped]
    out_shape += [jax.ShapeDtypeStruct((1, n), F32) for n in accs]
    first, last = _grid_ends((m // tm,))
    res, rode = _call(body, name=name, grid=(m // tm,), in_specs=in_specs, out_specs=out_specs, out_shape=out_shape,
                      scratch_shapes=[], args=list(tiled) + list(bcast), ride=ride, first=first, last=last)
    return res if ride is None else (res, rode)


def _rstd(x):
    return lax.rsqrt(jnp.mean(x * x, axis=-1, keepdims=True) + RMS_EPS)


def _rms_bwd(dh, x, g):
    xh = x * _rstd(x)
    dxh = dh * g
    dx = _rstd(x) * (dxh - xh * jnp.mean(dxh * xh, axis=-1, keepdims=True))
    return dx, dh * xh


def _sigmoid(x):
    return 1.0 / (1.0 + jnp.exp(-x))


_GELU_K = math.sqrt(2.0 / math.pi)
_GELU_C = 0.044715


def _gelu(x):
    return 0.5 * x * (1.0 + jnp.tanh(_GELU_K * (x + _GELU_C * x * x * x)))


def _gelu_grad(x):
    th = jnp.tanh(_GELU_K * (x + _GELU_C * x * x * x))
    return 0.5 * (1.0 + th) + 0.5 * x * (1.0 - th * th) * _GELU_K * (1.0 + 3.0 * _GELU_C * x * x)


class _Exchange:
    def __init__(self, arrays, outs, n_sems, sends, recvs=None, local=None, aliases=None):
        self.arrays, self.outs, self.n_sems = list(arrays), list(outs), n_sems
        self.sends, self.local, self.aliases = sends, local, aliases or {}
        self.recvs = recvs or (lambda i, o: [(k, dst) for k, _, dst, _ in sends(i, o)])

    def __add__(self, other):
        na, no, ns = len(self.arrays), len(self.outs), self.n_sems
        mine = lambda f: (lambda i, o: f(i[:na], o[:no]))
        shift = lambda f, at: (lambda i, o: [(k + ns,) + tuple(rest) for k, *rest in f(i[na:], o[no:])]) if at else None
        both = lambda f, g: (lambda i, o: f(i, o) + g(i, o))
        local = None
        if self.local or other.local:
            la = mine(self.local) if self.local else (lambda i, o: [])
            lb = (lambda i, o: other.local(i[na:], o[no:])) if other.local else (lambda i, o: [])
            local = both(la, lb)
        aliases = dict(self.aliases)
        aliases.update({na + i: no + o for i, o in other.aliases.items()})
        return _Exchange(self.arrays + other.arrays, self.outs + other.outs, ns + other.n_sems,
                         both(mine(self.sends), shift(other.sends, True)),
                         both(mine(self.recvs), shift(other.recvs, True)), local, aliases)

    def descriptors(self, in_refs, out_refs, send_sems, recv_sems, local_sems):
        me = _mesh_place()
        remote = lambda k, src, dst, to: pltpu.make_async_remote_copy(
            src_ref=src, dst_ref=dst, send_sem=send_sems.at[k], recv_sem=recv_sems.at[k], device_id=to,
            device_id_type=MESH)
        out = [remote(*s) for s in self.sends(in_refs, out_refs)]
        arrive = [remote(k, dst, dst, me) for k, dst in self.recvs(in_refs, out_refs)]
        own = [pltpu.make_async_copy(src, dst, local_sems.at[i])
               for i, (src, dst) in enumerate(self.local(in_refs, out_refs) if self.local else [])]
        return out, arrive, own

    def start(self, *refs):
        out, _, own = self.descriptors(*refs)
        for cp in own + out:
            cp.start()

    def finish(self, *refs):
        out, arrive, own = self.descriptors(*refs)
        for cp in arrive:
            cp.wait_recv()
        for cp in out:
            cp.wait_send()
        for cp in own:
            cp.wait()


def _call(body, *, name, grid, in_specs, out_specs, out_shape, scratch_shapes, args, ride=None, first=None, last=None):
    if ride is None:
        res = pl.pallas_call(body, name=name, grid=grid, in_specs=in_specs, out_specs=out_specs, out_shape=out_shape,
                             scratch_shapes=scratch_shapes, compiler_params=_params())(*args)
        return list(res), []
    n_in, n_out, n_scr = len(in_specs), len(out_specs), len(scratch_shapes)
    r_in, r_out = len(ride.arrays), len(ride.outs)

    def wrapped(*refs):
        ins, refs = refs[:n_in], refs[n_in:]
        x_in, refs = refs[:r_in], refs[r_in:]
        outs, refs = refs[:n_out], refs[n_out:]
        x_out, refs = refs[:r_out], refs[r_out:]
        scr, sems = refs[:n_scr], refs[n_scr:]

        @pl.when(first())
        def _():
            ride.start(x_in, x_out, *sems)

        body(*ins, *outs, *scr)

        @pl.when(last())
        def _():
            ride.finish(x_in, x_out, *sems)

    n_local = max(1, len(ride.arrays))
    res = pl.pallas_call(
        wrapped, name=name, grid=grid, in_specs=list(in_specs) + [_ANY] * r_in,
        out_specs=list(out_specs) + [_ANY] * r_out, out_shape=list(out_shape) + ride.outs,
        scratch_shapes=list(scratch_shapes) + [pltpu.SemaphoreType.DMA((ride.n_sems,)),
                                               pltpu.SemaphoreType.DMA((ride.n_sems,)),
                                               pltpu.SemaphoreType.DMA((n_local,))],
        input_output_aliases={n_in + i: n_out + o for i, o in ride.aliases.items()},
        compiler_params=_params(has_side_effects=True),
    )(*args, *ride.arrays)
    return list(res[:n_out]), list(res[n_out:])


def _gather_first(shards):
    def sends(i, o):
        x, y, c = _mesh_place()
        peers = [(x, y, 1 - c)] + [(px, py, c) for px, py in _chips(x, y)[1:]]
        return [(4 * w + k, i[w], o[w].at[_slab(x, y, c)], to) for w in range(len(i)) for k, to in enumerate(peers)]

    def recvs(i, o):
        x, y, c = _mesh_place()
        peers = [(x, y, 1 - c)] + [(px, py, c) for px, py in _chips(x, y)[1:]]
        return [(4 * w + k, o[w].at[_slab(*peer)]) for w in range(len(i)) for k, peer in enumerate(peers)]

    def local(i, o):
        return [(i[w], o[w].at[_slab(*_mesh_place())]) for w in range(len(i))]

    outs = [jax.ShapeDtypeStruct((N_DEV,) + a.shape, a.dtype) for a in shards]
    return _Exchange(shards, outs, 4 * len(shards), sends, recvs, local)


def _gather_second(gathered):
    def sends(i, o):
        x, y, c = _mesh_place()
        return [(3 * w + j, o[w].at[_slab(px, py, c)], o[w].at[_slab(px, py, c)], (x, y, 1 - c))
                for w in range(len(o)) for j, (px, py) in enumerate(_chips(x, y)[1:])]

    def recvs(i, o):
        x, y, c = _mesh_place()
        return [(3 * w + j, o[w].at[_slab(px, py, 1 - c)])
                for w in range(len(o)) for j, (px, py) in enumerate(_chips(x, y)[1:])]

    outs = [jax.ShapeDtypeStruct(a.shape, a.dtype) for a in gathered]
    return _Exchange(gathered, outs, 3 * len(gathered), sends, recvs, aliases={w: w for w in range(len(gathered))})


def _reduce_sibling(slabs):
    def sends(i, o):
        x, y, c = _mesh_place()
        return [(4 * w + k, i[w].at[_slab(px, py, 1 - c)], o[w].at[k], (x, y, 1 - c))
                for w in range(len(i)) for k, (px, py) in enumerate(_chips(x, y))]

    outs = [jax.ShapeDtypeStruct((4,) + a.shape[1:], a.dtype) for a in slabs]
    return _Exchange(slabs, outs, 4 * len(slabs), sends)


def _reduce_chips(partials):
    def sends(i, o):
        x, y, c = _mesh_place()
        return [(3 * w + k, i[w].at[k], o[w].at[k], (px, py, c))
                for w in range(len(i)) for k, (px, py) in enumerate(_chips(x, y)[1:])]

    outs = [jax.ShapeDtypeStruct(a.shape, a.dtype) for a in partials]
    return _Exchange(partials, outs, 3 * len(partials), sends)


def _head_masks():
    lane_head = lax.broadcasted_iota(jnp.int32, (1, GROUP_LANES), 1) // HEAD_DIM
    return [(lane_head == h).astype(F32) for h in range(HEADS_PER_GROUP)]


def _head_block_diag():
    r = lax.broadcasted_iota(jnp.int32, (GROUP_LANES, GROUP_LANES), 0) // HEAD_DIM
    c = lax.broadcasted_iota(jnp.int32, (GROUP_LANES, GROUP_LANES), 1) // HEAD_DIM
    return (r == c).astype(BF16)


def _head_mean(x, bd):
    hi = x.astype(BF16)
    lo = (x - hi.astype(F32)).astype(BF16)
    return (_dot(hi, bd) + _dot(lo, bd)) * (1.0 / HEAD_DIM)


def _stack_heads(x, masks):
    return jnp.concatenate([x * m for m in masks], axis=0)


def _unstack_heads(xs, masks):
    out = xs[0:GRID_W] * masks[0]
    for h in range(1, HEADS_PER_GROUP):
        out = out + xs[h * GRID_W:(h + 1) * GRID_W] * masks[h]
    return out


def _row_start(r, rows):
    return jnp.clip(r - WIN_H // 2, 0, rows - WIN_H)


ROWS_PER_STEP = 4


def _attn_common_specs(seq, n_hg, rows):
    win_keys = WIN_H * GRID_W
    q_spec = pl.BlockSpec((ROWS_PER_STEP * GRID_W, GROUP_LANES), lambda g, r: (r, g))
    k_spec = pl.BlockSpec((seq, GROUP_LANES), lambda g, r: (0, n_hg + g))
    v_spec = pl.BlockSpec((seq, GROUP_LANES), lambda g, r: (0, 2 * n_hg + g))
    gain_spec = pl.BlockSpec((1, GROUP_LANES), lambda g, r: (0, 0))

    def variant(r):
        return _row_start(r, rows) - r + (WIN_H - 1)

    bias_specs = [pl.BlockSpec((None, None, HEADS_PER_GROUP, GRID_W, win_keys),
                               lambda g, r, h=h: (g, variant(ROWS_PER_STEP * r + h), 0, 0, 0))
                  for h in range(ROWS_PER_STEP)]
    return q_spec, k_spec, v_spec, gain_spec, bias_specs, variant


def _attn_prepare_kv(k_ref, v_ref, kg, kn_scr, vb_scr, bd, seq):
    chunk = _tile(seq, 512)

    def step(c, carry):
        rows = pl.ds(pl.multiple_of(c * chunk, chunk), chunk)
        k = k_ref[rows, :]
        kn_scr[rows, :] = (k * lax.rsqrt(_head_mean(k * k, bd) + RMS_EPS) * kg).astype(BF16)
        vb_scr[rows, :] = v_ref[rows, :].astype(BF16)
        return carry

    lax.fori_loop(0, seq // chunk, step, 0)


def _attn_probs(qn, kw, bias, masks):
    qs = _stack_heads(qn, masks).astype(BF16)
    s = _dot(qs, kw, _NT) * (1.0 / math.sqrt(HEAD_DIM)) + bias
    m = jnp.max(s, axis=-1, keepdims=True)
    p = jnp.exp(s - m)
    return qs, p * (1.0 / jnp.sum(p, axis=-1, keepdims=True))


def _grid_ends(grid):
    first = lambda: functools.reduce(jnp.logical_and, [pl.program_id(a) == 0 for a in range(len(grid))])
    last = lambda: functools.reduce(jnp.logical_and, [pl.program_id(a) == n - 1 for a, n in enumerate(grid)])
    return first, last


def _attn_fwd(z, qg4, kg4, btab, ride=None):
    seq = z.shape[0]
    a_width = btab.shape[0] * GROUP_LANES
    n_hg, rows, win_keys = btab.shape[0], seq // GRID_W, WIN_H * GRID_W
    q_spec, k_spec, v_spec, gain_spec, bias_specs, _ = _attn_common_specs(seq, n_hg, rows)
    grid = (n_hg, rows // ROWS_PER_STEP)

    def body(q_ref, k_ref, v_ref, qg_ref, kg_ref, *rest):
        b_refs, (o_ref, kn_scr, vb_scr) = rest[:ROWS_PER_STEP], rest[ROWS_PER_STEP:]
        bd, masks = _head_block_diag(), _head_masks()

        @pl.when(pl.program_id(1) == 0)
        def _():
            _attn_prepare_kv(k_ref, v_ref, kg_ref[...], kn_scr, vb_scr, bd, seq)

        for h in range(ROWS_PER_STEP):
            r = ROWS_PER_STEP * pl.program_id(1) + h
            mine = slice(h * GRID_W, (h + 1) * GRID_W)
            win = pl.ds(pl.multiple_of(_row_start(r, rows) * GRID_W, GRID_W), win_keys)
            q = q_ref[mine, :]
            qn = q * lax.rsqrt(_head_mean(q * q, bd) + RMS_EPS) * qg_ref[...]
            bias = b_refs[h][...].reshape(HEADS_PER_GROUP * GRID_W, win_keys)
            _, p = _attn_probs(qn, kn_scr[win, :], bias, masks)
            o_ref[mine, :] = _unstack_heads(_dot(p.astype(BF16), vb_scr[win, :]), masks)

    first, last = _grid_ends(grid)
    (ya,), rode = _call(
        body, name="attn_fwd", grid=grid,
        in_specs=[q_spec, k_spec, v_spec, gain_spec, gain_spec] + bias_specs,
        out_specs=[pl.BlockSpec((ROWS_PER_STEP * GRID_W, GROUP_LANES), lambda g, r: (r, g))],
        out_shape=[jax.ShapeDtypeStruct((seq, a_width), F32)],
        scratch_shapes=[pltpu.VMEM((seq, GROUP_LANES), BF16), pltpu.VMEM((seq, GROUP_LANES), BF16)],
        args=(z, z, z, qg4, kg4) + (btab,) * ROWS_PER_STEP, ride=ride, first=first, last=last)
    return ya, rode


def _attn_bwd(z, d_out, qg4, kg4, btab, ride=None):
    seq = z.shape[0]
    n_hg, rows, win_keys = btab.shape[0], seq // GRID_W, WIN_H * GRID_W
    a_width = n_hg * GROUP_LANES
    q_spec, k_spec, v_spec, gain_spec, bias_specs, variant = _attn_common_specs(seq, n_hg, rows)
    scale = 1.0 / math.sqrt(HEAD_DIM)
    grid = (n_hg, rows // ROWS_PER_STEP)

    def body(q_ref, k_ref, v_ref, do_ref, qg_ref, kg_ref, *rest):
        b_refs, rest = rest[:ROWS_PER_STEP], rest[ROWS_PER_STEP:]
        dq_ref, dk_out, dv_out, db_ref, dqg_ref, dkg_ref, kn_scr, vb_scr, dk_ref, dv_ref = rest
        bd, masks = _head_block_diag(), _head_masks()

        @pl.when(pl.program_id(1) == 0)
        def _():
            _attn_prepare_kv(k_ref, v_ref, kg_ref[...], kn_scr, vb_scr, bd, seq)
            dk_ref[...] = jnp.zeros_like(dk_ref)
            dv_ref[...] = jnp.zeros_like(dv_ref)
            db_ref[...] = jnp.zeros_like(db_ref)
            dqg_ref[...] = jnp.zeros_like(dqg_ref)

        qg = qg_ref[...]
        for h in range(ROWS_PER_STEP):
            r = ROWS_PER_STEP * pl.program_id(1) + h
            mine = slice(h * GRID_W, (h + 1) * GRID_W)
            win = pl.ds(pl.multiple_of(_row_start(r, rows) * GRID_W, GRID_W), win_keys)
            q = q_ref[mine, :]
            rq = lax.rsqrt(_head_mean(q * q, bd) + RMS_EPS)
            qh = q * rq
            kw, vw = kn_scr[win, :], vb_scr[win, :]
            bias = b_refs[h][...].reshape(HEADS_PER_GROUP * GRID_W, win_keys)
            qs, p = _attn_probs(qh * qg, kw, bias, masks)
            dos = _stack_heads(do_ref[mine, :], masks).astype(BF16)
            dp = _dot(dos, vw, _NT)
            ds = p * (dp - jnp.sum(p * dp, axis=-1, keepdims=True))
            db_ref[variant(r)] += ds.reshape(HEADS_PER_GROUP, GRID_W, win_keys)
            dsb = ds.astype(BF16)
            dqn = _unstack_heads(_dot(dsb, kw), masks) * scale
            dk_ref[win, :] += _dot(dsb, qs, _TN) * scale
            dv_ref[win, :] += _dot(p.astype(BF16), dos, _TN)
            dqg_ref[...] += jnp.sum(dqn * qh, axis=0, keepdims=True)
            dqh = dqn * qg
            dq_ref[mine, :] = (rq * (dqh - qh * _head_mean(dqh * qh, bd))).astype(BF16)

        @pl.when(pl.program_id(1) == grid[1] - 1)
        def _():
            chunk = _tile(seq, 512)
            kg = kg_ref[...]

            def step(c, dkg):
                rws = pl.ds(pl.multiple_of(c * chunk, chunk), chunk)
                k = k_ref[rws, :]
                rk = lax.rsqrt(_head_mean(k * k, bd) + RMS_EPS)
                kh = k * rk
                dkn = dk_ref[rws, :]
                dkh = dkn * kg
                dk_out[rws, :] = (rk * (dkh - kh * _head_mean(dkh * kh, bd))).astype(BF16)
                dv_out[rws, :] = dv_ref[rws, :].astype(BF16)
                return dkg + jnp.sum(dkn * kh, axis=0, keepdims=True)

            dkg_ref[...] = lax.fori_loop(0, seq // chunk, step, jnp.zeros((1, GROUP_LANES), F32))

    col_spec = pl.BlockSpec((seq, GROUP_LANES), lambda g, r: (0, g))
    gsum_spec = pl.BlockSpec((None, 1, GROUP_LANES), lambda g, r: (g, 0, 0))
    first, last = _grid_ends(grid)
    rows_spec = pl.BlockSpec((ROWS_PER_STEP * GRID_W, GROUP_LANES), lambda g, r: (r, g))
    return _call(
        body, name="attn_bwd", grid=grid,
        in_specs=[q_spec, k_spec, v_spec, rows_spec, gain_spec, gain_spec] + bias_specs,
        out_specs=[rows_spec, col_spec, col_spec,
                   pl.BlockSpec((None, WIN_H, HEADS_PER_GROUP, GRID_W, win_keys), lambda g, r: (g, 0, 0, 0, 0)),
                   gsum_spec, gsum_spec],
        out_shape=[jax.ShapeDtypeStruct((seq, a_width), BF16)] * 3
        + [jax.ShapeDtypeStruct(btab.shape, F32)]
        + [jax.ShapeDtypeStruct((n_hg, 1, GROUP_LANES), F32)] * 2,
        scratch_shapes=[pltpu.VMEM((seq, GROUP_LANES), BF16), pltpu.VMEM((seq, GROUP_LANES), BF16),
                        pltpu.VMEM((seq, GROUP_LANES), F32), pltpu.VMEM((seq, GROUP_LANES), F32)],
        args=(z, z, z, d_out, qg4, kg4) + (btab,) * ROWS_PER_STEP, ride=ride, first=first, last=last)


DC_SLOTS = 2 * WIN_W


def _bias_spread(c):
    shape = (WIN_H * DC_SLOTS, WIN_H * GRID_W)
    rows = lax.broadcasted_iota(jnp.int32, shape, 0)
    cols = lax.broadcasted_iota(jnp.int32, shape, 1)
    row_i, row_d = rows // DC_SLOTS, rows % DC_SLOTS
    col_i, kc = cols // GRID_W, cols % GRID_W
    col_start = jnp.clip(c - WIN_W // 2, 0, GRID_W - WIN_W)
    col_in = (kc >= col_start) & (kc < col_start + WIN_W)
    hit = (row_i == col_i) & (row_d == kc - c + (WIN_W - 1)) & col_in
    mask_slot = (row_i == 0) & (row_d == DC_SLOTS - 1) & jnp.logical_not(col_in)
    return jnp.where(hit, 1.0, jnp.where(mask_slot, NEG_INF, 0.0)).astype(F32)


def _bias_table(rpb):
    n_h = rpb.shape[0]
    n_hg = n_h // HEADS_PER_GROUP
    rows = jnp.stack([rpb[:, v:v + WIN_H] for v in range(WIN_H)], axis=1)
    rows = jnp.pad(rows, ((0, 0), (0, 0), (0, 0), (0, DC_SLOTS - rows.shape[-1])))
    rows = rows.at[:, :, 0, DC_SLOTS - 1].set(1.0)
    rows = rows.reshape(n_hg, HEADS_PER_GROUP, WIN_H, WIN_H * DC_SLOTS).transpose(0, 2, 1, 3)
    n_rows, win_keys, depth = n_h * WIN_H, WIN_H * GRID_W, WIN_H * DC_SLOTS

    def body(r_ref, o_ref):
        for cc in range(SUBLANES):
            spread = _bias_spread(pl.program_id(0) * SUBLANES + cc)
            o_ref[cc] = lax.dot_general(r_ref[...], spread, _NN, precision=lax.Precision.HIGHEST,
                                        preferred_element_type=F32)

    tab = pl.pallas_call(
        body, name="rpb_spread", grid=(GRID_W // SUBLANES,),
        in_specs=[pl.BlockSpec((n_rows, depth), lambda c: (0, 0))],
        out_specs=pl.BlockSpec((SUBLANES, n_rows, win_keys), lambda c: (c, 0, 0)),
        out_shape=jax.ShapeDtypeStruct((GRID_W, n_rows, win_keys), F32), compiler_params=_params(),
    )(rows.reshape(n_rows, depth))
    return tab.transpose(1, 0, 2).reshape(n_hg, WIN_H, HEADS_PER_GROUP, GRID_W, win_keys)


def _bias_grad(dtab, n_h):
    n_hg = n_h // HEADS_PER_GROUP
    n_rows, win_keys, depth = n_h * WIN_H, WIN_H * GRID_W, WIN_H * DC_SLOTS

    def body(d_ref, o_ref):
        total = None
        for cc in range(SUBLANES):
            spread = _bias_spread(pl.program_id(0) * SUBLANES + cc).astype(BF16)
            t = _dot(d_ref[cc].astype(BF16), spread, _NT)
            total = t if total is None else total + t

        @pl.when(pl.program_id(0) == 0)
        def _():
            o_ref[...] = total

        @pl.when(pl.program_id(0) > 0)
        def _():
            o_ref[...] += total

    d_rows = pl.pallas_call(
        body, name="rpb_diag_sum", grid=(GRID_W // SUBLANES,),
        in_specs=[pl.BlockSpec((SUBLANES, n_rows, win_keys), lambda c: (c, 0, 0))],
        out_specs=pl.BlockSpec((n_rows, depth), lambda c: (0, 0)),
        out_shape=jax.ShapeDtypeStruct((n_rows, depth), F32), compiler_params=_params(),
    )(dtab.reshape(n_rows, GRID_W, win_keys).transpose(1, 0, 2))
    d_rows = d_rows.reshape(n_hg, WIN_H, HEADS_PER_GROUP, WIN_H, DC_SLOTS).transpose(0, 2, 1, 3, 4)
    d_rows = d_rows.reshape(n_h, WIN_H, WIN_H, DC_SLOTS)[..., : 2 * WIN_W - 1]
    out = jnp.zeros((n_h, 2 * WIN_H - 1, 2 * WIN_W - 1), F32)
    for v in range(WIN_H):
        out = out.at[:, v:v + WIN_H].add(d_rows[:, v])
    return out


def _cmul(ar, ai, br, bi):
    return ar * br - ai * bi, ar * bi + ai * br


def _s5_discretize(a_re, a_im, dt, b_re, b_im, ride=None):
    c = b_re.shape[1]

    def fn(are, aim, dt_, bre, bim):
        lr, li = jnp.minimum(are, A_RE_MAX), aim
        mag = jnp.exp(lr * dt_)
        l1r, l1i = mag * jnp.cos(li * dt_), mag * jnp.sin(li * dt_)
        den = lr * lr + li * li
        nr, ni = l1r - 1.0, l1i
        cr, ci = (nr * lr + ni * li) / den, (ni * lr - nr * li) / den
        bbr, bbi = _cmul(cr, ci, bre, bim)
        shape = (are.shape[0], SUBLANES)
        lane = lax.broadcasted_iota(jnp.int32, shape, 1)
        pr, pi = l1r, l1i
        acc_r, acc_i = jnp.zeros(shape, F32), jnp.zeros(shape, F32)
        for k in range(SUBLANES):
            acc_r = jnp.where(lane == k, pr, acc_r)
            acc_i = jnp.where(lane == k, pi, acc_i)
            pr, pi = _cmul(pr, pi, l1r, l1i)
        return acc_r, acc_i, cr, ci, bbr, bbi

    return _rowwise(fn, [a_re, a_im, dt, b_re, b_im], [],
                    [(SUBLANES, F32), (SUBLANES, F32), (1, F32), (1, F32), (c, F32), (c, F32)],
                    name="s5_discretize", tm=1024, ride=ride)


def _s5_param_grads(a_re, a_im, dt, b_re, b_im, l1r, l1i, cr, ci, bbr, bbi, r_re, r_im, gb_re, gb_im):
    c = b_re.shape[1]

    def fn(are, aim, dt_, bre, bim, l1r_, l1i_, cr_, ci_, bbr_, bbi_, rr, ri, gbr, gbi):
        lr, li = jnp.minimum(are, A_RE_MAX), aim
        den = lr * lr + li * li
        dbr, dbi = _cmul(cr_, -ci_, gbr, gbi)
        gcr, gci = _cmul(bre, -bim, gbr, gbi)
        gcr, gci = jnp.sum(gcr, axis=1, keepdims=True), jnp.sum(gci, axis=1, keepdims=True)
        qr, qi = _cmul(bbr_, -bbi_, gbr, gbi)
        qr = rr - jnp.sum(qr, axis=1, keepdims=True)
        qi = ri - jnp.sum(qi, axis=1, keepdims=True)
        tr, ti = _cmul(gcr, gci, lr / den, li / den)
        ur, ui = _cmul(l1r_, -l1i_, tr, ti)
        gwr, gwi = qr + ur, qi + ui
        vr, vi = _cmul(cr_, -ci_, lr / den, li / den)
        vr, vi = _cmul(gcr, gci, vr, vi)
        glr, gli = dt_ * gwr - vr, dt_ * gwi - vi
        return jnp.where(are < A_RE_MAX, glr, 0.0), gli, (gwr * lr + gwi * li) * dt_, dbr, dbi

    return _rowwise(fn, [a_re, a_im, dt, b_re, b_im, l1r, l1i, cr, ci, bbr, bbi, r_re, r_im, gb_re, gb_im], [],
                    [(1, F32), (1, F32), (1, F32), (c, F32), (c, F32)], name="s5_param_grads", tm=1024)


def _s5_scan(v, win_re, win_im, tabs, wo_re, wo_im, *, reverse, name, t_chunk=256, ride=None):
    seq, width = v.shape
    n_tiles, n_state = width // U_TILE, width * (SSM_P // SSM_C)
    t_chunk = _tile(seq, t_chunk)
    n_chunks, n_blk = seq // t_chunk, t_chunk // SUBLANES
    last_row = 0 if reverse else SUBLANES - 1

    def chunk_of(j):
        return (n_chunks - 1 - j) if reverse else j

    def body(v_ref, wir_ref, wii_ref, tab_ref, wor_ref, woi_ref, sr_ref, si_ref, y_ref, carry, wr, wi):
        @pl.when(pl.program_id(0) == 0)
        def _():
            carry[...] = jnp.zeros_like(carry)

        for j0 in range(0, n_tiles, TILES_TOGETHER):
            tiles = list(range(j0, min(j0 + TILES_TOGETHER, n_tiles)))
            lanes = [slice(jt * ST_TILE, (jt + 1) * ST_TILE) for jt in tiles]
            for w, (jt, ls) in enumerate(zip(tiles, lanes)):
                vj = v_ref[:, jt * U_TILE:(jt + 1) * U_TILE].astype(BF16)
                xr = _dot(vj, wir_ref[jt]).reshape(n_blk, SUBLANES, ST_TILE)
                xi = _dot(vj, wii_ref[jt]).reshape(n_blk, SUBLANES, ST_TILE)
                for s, k in enumerate((1, 2, 4)):
                    sh = (SUBLANES - k) if reverse else k
                    tr, ti = pltpu.roll(xr, sh, 1), pltpu.roll(xi, sh, 1)
                    lr, li = tab_ref[2 * s, :, ls][None], tab_ref[2 * s + 1, :, ls][None]
                    xr, xi = xr + lr * tr - li * ti, xi + lr * ti + li * tr
                wr[w] = xr.reshape(t_chunk, ST_TILE)
                wi[w] = xi.reshape(t_chunk, ST_TILE)
            powers = [(tab_ref[6, :, ls], tab_ref[7, :, ls]) for ls in lanes]

            def blk(b, c, powers=powers):
                bb = (n_blk - 1 - b) if reverse else b
                rows = pl.ds(pl.multiple_of(bb * SUBLANES, SUBLANES), SUBLANES)
                out = []
                for w, ((cr, ci), (lr, li)) in enumerate(zip(c, powers)):
                    xr = wr[w, rows, :] + lr * cr - li * ci
                    xi = wi[w, rows, :] + lr * ci + li * cr
                    wr[w, rows, :], wi[w, rows, :] = xr, xi
                    shape = (SUBLANES, ST_TILE)
                    out.append((jnp.broadcast_to(xr[last_row:last_row + 1], shape),
                                jnp.broadcast_to(xi[last_row:last_row + 1], shape)))
                return tuple(out)

            ends = lax.fori_loop(0, n_blk, blk, tuple((carry[0, :, ls], carry[1, :, ls]) for ls in lanes), unroll=2)
            for w, (jt, ls) in enumerate(zip(tiles, lanes)):
                carry[0, :, ls], carry[1, :, ls] = ends[w]
                xr_b, xi_b = wr[w].astype(BF16), wi[w].astype(BF16)
                sr_ref[:, ls], si_ref[:, ls] = xr_b, xi_b
                y_ref[:, jt * U_TILE:(jt + 1) * U_TILE] = _dot(xr_b, wor_ref[jt]) + _dot(xi_b, woi_ref[jt])

    whole = lambda a: pl.BlockSpec(a.shape, lambda j, nd=a.ndim: (0,) * nd)
    st_spec = pl.BlockSpec((t_chunk, n_state), lambda j: (chunk_of(j), 0))
    v_spec = pl.BlockSpec((t_chunk, width), lambda j: (chunk_of(j), 0))
    first, last = _grid_ends((n_chunks,))
    return _call(
        body, name=name, grid=(n_chunks,),
        in_specs=[v_spec, whole(win_re), whole(win_im), whole(tabs), whole(wo_re), whole(wo_im)],
        out_specs=[st_spec, st_spec, v_spec],
        out_shape=[jax.ShapeDtypeStruct((seq, n_state), BF16)] * 2 + [jax.ShapeDtypeStruct((seq, width), F32)],
        scratch_shapes=[pltpu.VMEM((2, SUBLANES, n_state), F32), pltpu.VMEM((TILES_TOGETHER, t_chunk, ST_TILE), F32),
                        pltpu.VMEM((TILES_TOGETHER, t_chunk, ST_TILE), F32)],
        args=(v, win_re, win_im, tabs, wo_re, wo_im), ride=ride, first=first, last=last)


def _s5_reduce(x_re, x_im, a_re, a_im, u, dy, *, name, t_chunk=512, ride=None):
    seq, n_state = x_re.shape
    width = u.shape[1]
    n_tiles = width // U_TILE
    t_chunk = _tile(seq, t_chunk)

    def body(xr_ref, xi_ref, ar_ref, ai_ref, u_ref, dy_ref, rr_ref, ri_ref, gbr_ref, gbi_ref, gcr_ref, gci_ref):
        xrb, xib, arb, aib = xr_ref[...], xi_ref[...], ar_ref[...], ai_ref[...]
        xr, xi, ar, ai = xrb.astype(F32), xib.astype(F32), arb.astype(F32), aib.astype(F32)
        ub, dyb = u_ref[...].astype(BF16), dy_ref[...].astype(BF16)
        parts = (jnp.sum(ar * xr + ai * xi, axis=0, keepdims=True), jnp.sum(ai * xr - ar * xi, axis=0, keepdims=True),
                 _dot(ub, arb, _TN), _dot(ub, aib, _TN), _dot(dyb, xrb, _TN), _dot(dyb, xib, _TN))
        first = pl.program_id(1) == 0
        for ref, val in zip((rr_ref, ri_ref, gbr_ref, gbi_ref, gcr_ref, gci_ref), parts):
            @pl.when(first)
            def _():
                ref[...] = val

            @pl.when(jnp.logical_not(first))
            def _():
                ref[...] += val

    st_spec = pl.BlockSpec((t_chunk, ST_TILE), lambda j, t: (t, j))
    u_spec = pl.BlockSpec((t_chunk, U_TILE), lambda j, t: (t, j))
    r_spec = pl.BlockSpec((1, ST_TILE), lambda j, t: (0, j))
    g_spec = pl.BlockSpec((None, U_TILE, ST_TILE), lambda j, t: (j, 0, 0))
    first, last = _grid_ends((n_tiles, seq // t_chunk))
    return _call(
        body, name=name, grid=(n_tiles, seq // t_chunk),
        in_specs=[st_spec] * 4 + [u_spec] * 2,
        out_specs=[r_spec, r_spec] + [g_spec] * 4,
        out_shape=[jax.ShapeDtypeStruct((1, n_state), F32)] * 2
        + [jax.ShapeDtypeStruct((n_tiles, U_TILE, ST_TILE), F32)] * 4,
        scratch_shapes=[], args=(x_re, x_im, a_re, a_im, u, dy), ride=ride, first=first, last=last)


def _block_diag_in(ms):
    m = jnp.stack(ms)
    n, g, c, p = m.shape
    m5 = m.reshape(n, g // GROUPS_PER_TILE, GROUPS_PER_TILE, c, p)
    eye = jnp.eye(GROUPS_PER_TILE, dtype=m.dtype)
    out = m5[:, :, :, :, None, :] * eye[None, None, :, None, :, None]
    return out.astype(BF16).reshape(n, g // GROUPS_PER_TILE, GROUPS_PER_TILE * c, GROUPS_PER_TILE * p)


def _block_diag_take(m, c, p):
    t = m.shape[0]
    m5 = m.reshape(t, GROUPS_PER_TILE, c, GROUPS_PER_TILE, p)
    idx = jnp.arange(GROUPS_PER_TILE)
    return m5[:, idx, :, idx, :].transpose(1, 0, 3, 2).reshape(t * GROUPS_PER_TILE, p, c)


def _scan_tables(pw_re, pw_im, reverse):
    row = jnp.arange(SUBLANES)[:, None]
    tabs = []
    for k in (1, 2, 4):
        keep = (row <= SUBLANES - 1 - k) if reverse else (row >= k)
        tabs += [jnp.where(keep, pw_re[k - 1][None, :], 0.0), jnp.where(keep, pw_im[k - 1][None, :], 0.0)]
    order = jnp.arange(SUBLANES)[::-1] if reverse else jnp.arange(SUBLANES)
    tabs += [pw_re[order], pw_im[order]]
    return jnp.stack(tabs)


def _partial_sums(slabs, from_sibling, names):
    x, y, c = _mesh_place()
    theirs = jnp.stack([_slab(px, py, c) for px, py in _chips(x, y)[1:]]).astype(jnp.int32)
    out = []
    for s, f, n in zip(slabs, from_sibling, names):
        rows, cols = s.shape[1:]
        tr = _tile(rows, 512)

        def body(idx_ref, a_ref, b_ref, o_ref):
            o_ref[...] = (a_ref[...] + b_ref[...]).astype(BF16)

        out.append(pl.pallas_call(
            body, name=f"reduce_add_{n}",
            grid_spec=pltpu.PrefetchScalarGridSpec(
                num_scalar_prefetch=1, grid=(3, rows // tr),
                in_specs=[pl.BlockSpec((None, tr, cols), lambda k, i, idx: (idx[k], i, 0)),
                          pl.BlockSpec((None, tr, cols), lambda k, i, idx: (k + 1, i, 0))],
                out_specs=pl.BlockSpec((None, tr, cols), lambda k, i, idx: (k, i, 0))),
            out_shape=jax.ShapeDtypeStruct((3, rows, cols), BF16), compiler_params=_params(),
        )(theirs, s, f))
    return out


def _local_step(x, target, p, shards):
    seq, d_model = x.shape
    a_width = p["g_out_attn"].shape[-1]
    s_width = p["g_out_ssm"].shape[-1]
    n_heads = a_width // HEAD_DIM
    n_hg = n_heads // HEADS_PER_GROUP
    n_groups = s_width // SSM_C
    n_sh, in_sh = N_DEV, shards["w_in"].shape[-1]
    f_sh = shards["w_ffn_gate"].shape[-1]
    w = {}
    slab3 = lambda g, n: g.reshape(N_DEV, -1, shards[n].shape[-1])
    t2, t1 = _tile(seq, 2048), _tile(seq, 1024)
    n2, n1 = seq // t2, seq // t1

    n_col = 2 * n_groups * SSM_P
    col = lambda a: a.reshape(n_col, 1)
    a_re_c, a_im_c = col(p["ssm_a_re"]), col(p["ssm_a_im"])
    dt_c = col(jnp.broadcast_to(jnp.exp(p["ssm_log_step"])[:, :, None], (2, n_groups, SSM_P)))
    b_re_c, b_im_c = p["ssm_b_re"].reshape(n_col, SSM_C), p["ssm_b_im"].reshape(n_col, SSM_C)
    (pw_re, pw_im, cf_re, cf_im, bb_re, bb_im), got = _s5_discretize(a_re_c, a_im_c, dt_c, b_re_c, b_im_c,
                                                                     ride=_gather_first([shards["w_in"]]))

    twice = lambda f: (lambda *a: (f(*a),) * 2)
    (h1, h1_t), (w["w_in"],) = _rowwise(twice(x_norm), [x], [p["g_mix"]], [(d_model, BF16)],
                                        flipped=[(d_model, BF16, 1)], name="rms_mix", ride=_gather_second(got))
    z = _mm(h1, w["w_in"], name="in_proj", grid=(n2, n_sh),
            a_spec=pl.BlockSpec((t2, d_model), lambda i, j: (i, 0)),
            b_spec=pl.BlockSpec((None, d_model, in_sh), lambda i, j: (j, 0, 0)),
            o_spec=pl.BlockSpec((t2, in_sh), lambda i, j: (i, j)), o_shape=(seq, n_sh * in_sh), dims="nn")
    qg4 = jnp.tile(p["q_gain"], (1, HEADS_PER_GROUP))
    kg4 = jnp.tile(p["k_gain"], (1, HEADS_PER_GROUP))
    btab = _bias_table(p["rpb"])
    ya, got_a = _attn_fwd(z, qg4, kg4, btab, ride=_gather_first([shards["w_ffn_gate"], shards["w_ffn_up"]]))
    u = z[:, 3 * a_width:]
    n_state = n_groups * SSM_P
    pw_re = pw_re.reshape(2, n_state, SUBLANES).transpose(0, 2, 1)
    pw_im = pw_im.reshape(2, n_state, SUBLANES).transpose(0, 2, 1)
    bb_re4, bb_im4 = bb_re.reshape(2, n_groups, SSM_P, SSM_C), bb_im.reshape(2, n_groups, SSM_P, SSM_C)
    c_re, c_im = p["ssm_c_re"], p["ssm_c_im"]
    t21 = lambda a: a.transpose(0, 2, 1)
    maps_in = _block_diag_in([m for d in range(2) for m in (t21(bb_re4[d]), t21(bb_im4[d]), c_re[d], -c_im[d])])
    maps_out = _block_diag_in([m for d in range(2) for m in (t21(c_re[d]), -t21(c_im[d]), bb_re4[d], bb_im4[d])])
    fwd, bwd_in = [], []
    got_b = None
    for d in range(2):
        rev = d == 1
        tabs = _scan_tables(pw_re[d], pw_im[d], rev)
        if d == 0:
            ride = _gather_first([shards["w_glu"], shards["w_out"]])
        else:
            ride = _gather_second(got_a + got_b) + _gather_first([shards["w_ffn_down"]])
        (xs_re, xs_im, y_d), got = _s5_scan(u, maps_in[4 * d], maps_in[4 * d + 1], tabs, maps_out[4 * d],
                                            maps_out[4 * d + 1], reverse=rev, name=f"s5_fwd_{d}", ride=ride)
        if d == 0:
            got_b = got
        fwd.append((xs_re, xs_im, y_d))
        bwd_in.append((maps_in[4 * d + 2], maps_in[4 * d + 3], _scan_tables(pw_re[d], -pw_im[d], not rev),
                       maps_out[4 * d + 2], maps_out[4 * d + 3]))
    w["w_gate"], w["w_up"], w_glu_full, w_out_full, w_down_first = got
    w["w_glu"] = w_glu_full.reshape(-1, s_width)
    w["w_out"] = w_out_full.reshape(-1, d_model)

    ypre, yg, yg_t = _rowwise(s5_mid, [fwd[0][2], fwd[1][2], u], [p["ssm_d"]], [(s_width, F32), (s_width, F32)],
                              flipped=[(s_width, BF16, 1)], name="s5_skip_gelu")
    t_glu = _mm_plain(yg, w["w_glu"], "nn", name="glu_proj", tn=s_width)
    y_cat, y_cat_t = _rowwise(twice(mix_out_fwd), [ya, yg, t_glu], [p["b_glu"], p["g_out_attn"], p["g_out_ssm"]],
                              [(a_width + s_width, BF16)], flipped=[(a_width + s_width, BF16, 1)], name="mix_out")
    x1, (w["w_down"],) = _mm_plain(y_cat, w["w_out"], "nn", name="out_proj", res=x, tn=2048,
                                   ride=_gather_second([w_down_first]))

    h2, h2_t = _rowwise(twice(x_norm), [x1], [p["g_ffn"]], [(d_model, BF16)], flipped=[(d_model, BF16, 1)],
                        name="rms_ffn")
    ffn_up = functools.partial(
        _mm, grid=(n2, n_sh), a_spec=pl.BlockSpec((t2, d_model), lambda i, j: (i, 0)),
        b_spec=pl.BlockSpec((None, d_model, f_sh), lambda i, j: (j, 0, 0)),
        o_spec=pl.BlockSpec((None, t2, f_sh), lambda i, j: (j, i, 0)), o_shape=(n_sh, seq, f_sh), dims="nn",
        out_dtype=BF16)
    gate = ffn_up(h2, w["w_gate"], name="ffn_gate")
    up = ffn_up(h2, w["w_up"], name="ffn_up")
    flat = lambda a: a.reshape(n_sh * seq, f_sh)
    act_t = _rowwise(swiglu_fwd, [flat(gate), flat(up)], [], [], flipped=[(f_sh, BF16, n_sh)], name="swiglu",
                     tm=1024)[0].reshape(n_sh, f_sh, seq)
    ffn_out = _mm(act_t, w["w_down"], name="ffn_down", grid=(n1, n_sh // 2), groups=2,
                  a_spec=pl.BlockSpec((2, f_sh, t1), lambda i, j: (j, 0, i)),
                  b_spec=pl.BlockSpec((2, f_sh, d_model), lambda i, j: (j, 0, 0)),
                  o_spec=pl.BlockSpec((t1, d_model), lambda i, j: (i, 0)), o_shape=(seq, d_model), dims="tn",
                  k_axis=1)

    dx2, dx2_b, sq = _rowwise(functools.partial(loss_head, inv_d=1.0 / d_model), [ffn_out, x1, target], [],
                              [(d_model, F32), (d_model, BF16)], [d_model], name="loss_head")
    loss = 0.5 * jnp.sum(sq) / d_model

    d_act = _mm(dx2_b, w["w_down"], name="ffn_down_dx", grid=(n2, n_sh),
                a_spec=pl.BlockSpec((t2, d_model), lambda i, j: (i, 0)),
                b_spec=pl.BlockSpec((None, f_sh, d_model), lambda i, j: (j, 0, 0)),
                o_spec=pl.BlockSpec((None, t2, f_sh), lambda i, j: (j, i, 0)), o_shape=(n_sh, seq, f_sh), dims="nt",
                out_dtype=BF16)
    g_w_down = _mm(act_t, dx2_b, name="ffn_down_dw", grid=(n_sh, n2),
                   a_spec=pl.BlockSpec((None, f_sh, t2), lambda j, k: (j, 0, k)),
                   b_spec=pl.BlockSpec((t2, d_model), lambda j, k: (k, 0)),
                   o_spec=pl.BlockSpec((None, f_sh, d_model), lambda j, k: (j, 0, 0)),
                   o_shape=(n_sh, f_sh, d_model), dims="nn", k_axis=1)
    d_gate, d_up = _rowwise(swiglu_bwd, [flat(d_act), flat(gate), flat(up)], [], [(f_sh, BF16), (f_sh, BF16)],
                            name="swiglu_bwd", tm=1024)
    d_gate, d_up = d_gate.reshape(n_sh, seq, f_sh), d_up.reshape(n_sh, seq, f_sh)
    d_h2 = _mm(d_gate, w["w_gate"], second=(d_up, w["w_up"]), name="ffn_up_gate_dx", grid=(n1, n_sh),
               a_spec=pl.BlockSpec((None, t1, f_sh), lambda i, j: (j, i, 0)),
               b_spec=pl.BlockSpec((None, d_model, f_sh), lambda i, j: (j, 0, 0)),
               o_spec=pl.BlockSpec((t1, d_model), lambda i, j: (i, 0)), o_shape=(seq, d_model), dims="nt", k_axis=1)
    ffn_dw = functools.partial(
        _mm, grid=(n_sh, n2), a_spec=pl.BlockSpec((d_model, t2), lambda j, k: (0, k)),
        b_spec=pl.BlockSpec((None, t2, f_sh), lambda j, k: (j, k, 0)),
        o_spec=pl.BlockSpec((None, d_model, f_sh), lambda j, k: (j, 0, 0)), o_shape=(n_sh, d_model, f_sh), dims="nn",
        k_axis=1)
    g_w_gate = ffn_dw(h2_t, d_gate, name="ffn_gate_dw")
    g_w_up = ffn_dw(h2_t, d_up, name="ffn_up_dw")
    dx1, g_g_ffn = _rowwise(residual_rms_bwd, [dx2, d_h2, x1], [p["g_ffn"]], [(d_model, F32)], [d_model],
                            name="rms_ffn_bwd")

    d_ycat = _mm_plain(dx1, w["w_out"], "nt", name="out_proj_dx", tn=2048)
    mix_w = a_width + s_width
    tm_o = _tile(mix_w, 1024)
    g_w_out = _mm(y_cat_t, dx1, name="out_proj_dw", grid=(mix_w // tm_o, n1),
                  a_spec=pl.BlockSpec((tm_o, t1), lambda i, k: (i, k)),
                  b_spec=pl.BlockSpec((t1, d_model), lambda i, k: (k, 0)),
                  o_spec=pl.BlockSpec((tm_o, d_model), lambda i, k: (i, 0)), o_shape=(mix_w, d_model), dims="nn",
                  k_axis=1)
    (d_ya, d_yg_direct, d_t, g_goa, g_gos, g_b_glu) = _rowwise(
        functools.partial(mix_out_bwd, a_width=a_width), [d_ycat, ya, yg, t_glu],
        [p["b_glu"], p["g_out_attn"], p["g_out_ssm"]],
        [(a_width, F32), (s_width, F32), (s_width, BF16)], [a_width, s_width, s_width], name="mix_out_bwd")
    d_yg = _mm_plain(d_t, w["w_glu"], "nt", name="glu_proj_dx", res=d_yg_direct, tn=s_width)
    g_w_glu = _mm(yg_t, d_t, name="glu_proj_dw", grid=(1, n1),
                  a_spec=pl.BlockSpec((s_width, t1), lambda i, k: (0, k)),
                  b_spec=pl.BlockSpec((t1, s_width), lambda i, k: (k, 0)),
                  o_spec=pl.BlockSpec((s_width, s_width), lambda i, k: (0, 0)), o_shape=(s_width, s_width),
                  dims="nn", k_axis=1)
    d_ypre, du_skip, g_ssm_d = _rowwise(gelu_skip_bwd, [d_yg, ypre, u], [p["ssm_d"]],
                                        [(s_width, F32), (s_width, F32)], [s_width], name="s5_skip_gelu_bwd")

    names = ("w_ffn_gate", "w_ffn_up", "w_ffn_down", "w_glu", "w_out")
    slabs = {n: slab3(g, n) for n, g in zip(names, (g_w_gate, g_w_up, g_w_down, g_w_glu, g_w_out))}
    sib, part, chips = {}, {}, {}
    du_dirs, adj, r_parts, gb_parts, gc_parts = [], [], [], [], []
    for d in range(2):
        win_re, win_im, tabs, wo_re, wo_im = bwd_in[d]
        ride = _reduce_sibling([slabs[n] for n in names]) if d == 0 else _reduce_chips([part["w_ffn_down"]])
        (as_re, as_im, du_d), got = _s5_scan(d_ypre, win_re, win_im, tabs, wo_re, wo_im, reverse=(d == 0),
                                             name=f"s5_bwd_{d}", ride=ride)
        du_dirs.append(du_d)
        adj.append((as_re, as_im))
        if d == 0:
            sib = dict(zip(names, got))
            part = dict(zip(names, _partial_sums([slabs[n] for n in names], got, names)))
        else:
            chips["w_ffn_down"] = got[0]
    for d in range(2):
        (r_re, r_im, gbt_re, gbt_im, gct_re, gct_im), got = _s5_reduce(
            fwd[d][0], fwd[d][1], adj[d][0], adj[d][1], u, d_ypre, name=f"s5_reduce_{d}",
            ride=_reduce_chips([part["w_glu"], part["w_out"]]) if d == 0 else None)
        if d == 0:
            chips["w_glu"], chips["w_out"] = got
        r_parts.append((r_re.reshape(n_state, 1), r_im.reshape(n_state, 1)))
        gb_parts.append((_block_diag_take(gbt_re, SSM_C, SSM_P), _block_diag_take(gbt_im, SSM_C, SSM_P)))
        gc_parts.append((_block_diag_take(gct_re, SSM_C, SSM_P), _block_diag_take(gct_im, SSM_C, SSM_P)))
    cat = lambda i, parts: jnp.concatenate([parts[0][i], parts[1][i]], axis=0)
    gbb_re, gbb_im = cat(0, gb_parts).reshape(n_col, SSM_C), cat(1, gb_parts).reshape(n_col, SSM_C)
    g_a_re, g_a_im, g_ls, g_b_re, g_b_im = _s5_param_grads(
        a_re_c, a_im_c, dt_c, b_re_c, b_im_c, pw_re[:, 0].reshape(n_col, 1), pw_im[:, 0].reshape(n_col, 1),
        cf_re, cf_im, bb_re, bb_im, cat(0, r_parts), cat(1, r_parts), gbb_re, gbb_im)
    g_c_re = cat(0, gc_parts).reshape(2, n_groups, SSM_P, SSM_C).transpose(0, 1, 3, 2)
    g_c_im = -cat(1, gc_parts).reshape(2, n_groups, SSM_P, SSM_C).transpose(0, 1, 3, 2)

    (d_q, d_k, d_v, d_btab, g_qg, g_kg), got = _attn_bwd(
        z, d_ya, qg4, kg4, btab, ride=_reduce_chips([part["w_ffn_gate"], part["w_ffn_up"]]))
    chips["w_ffn_gate"], chips["w_ffn_up"] = got
    d_u = _rowwise(lambda a, b, c: a + b + c, [du_dirs[0], du_dirs[1], du_skip], [], [(s_width, BF16)],
                   name="s5_du_sum")[0]
    d_z = jnp.concatenate([d_q, d_k, d_v, d_u], axis=1)
    fold_heads = lambda g: g.reshape(n_heads, HEAD_DIM).sum(axis=0, keepdims=True)
    small = {
        "q_gain": fold_heads(g_qg), "k_gain": fold_heads(g_kg), "rpb": _bias_grad(d_btab, n_heads),
        "ssm_a_re": g_a_re.reshape(2, n_groups, SSM_P), "ssm_a_im": g_a_im.reshape(2, n_groups, SSM_P),
        "ssm_b_re": g_b_re.reshape(2, n_groups, SSM_P, SSM_C), "ssm_b_im": g_b_im.reshape(2, n_groups, SSM_P, SSM_C),
        "ssm_c_re": g_c_re, "ssm_c_im": g_c_im,
        "ssm_log_step": g_ls.reshape(2, n_groups, SSM_P).sum(axis=-1),
        "ssm_d": g_ssm_d, "b_glu": g_b_glu, "g_out_attn": g_goa, "g_out_ssm": g_gos, "g_ffn": g_g_ffn,
    }
    g_w_in, got = _mm(h1_t, d_z, name="in_proj_dw", grid=(n_sh, n2),
                      a_spec=pl.BlockSpec((d_model, t2), lambda j, k: (0, k)),
                      b_spec=pl.BlockSpec((t2, in_sh), lambda j, k: (k, j)),
                      o_spec=pl.BlockSpec((None, d_model, in_sh), lambda j, k: (j, 0, 0)),
                      o_shape=(n_sh, d_model, in_sh), dims="nn", k_axis=1,
                      ride=_gather_first([_as_rows(small[n]) for n in SMALL_LATE]))
    d_h1, got = _mm(d_z, w["w_in"], name="in_proj_dx", grid=(n1, n_sh // 2), groups=2,
                    a_spec=pl.BlockSpec((t1, 2 * in_sh), lambda i, j: (i, j)),
                    b_spec=pl.BlockSpec((2, d_model, in_sh), lambda i, j: (j, 0, 0)),
                    o_spec=pl.BlockSpec((t1, d_model), lambda i, j: (i, 0)), o_shape=(seq, d_model), dims="nt",
                    k_axis=1, ride=_gather_second(got) + _reduce_sibling([g_w_in]))
    small_gathered, in_sibling = dict(zip(SMALL_LATE, got[:-1])), got[-1]
    in_part = _partial_sums([g_w_in], [in_sibling], ("w_in",))
    (grad_x, g_g_mix), (in_chips,) = _rowwise(residual_rms_bwd, [dx1, d_h1, x], [p["g_mix"]], [(d_model, F32)],
                                              [d_model], name="rms_mix_bwd", ride=_reduce_chips(in_part))
    reduced = {n: (slabs[n], sib[n], chips[n]) for n in slabs}
    reduced["w_in"] = (g_w_in, in_sibling, in_chips)
    return loss, grad_x, small_gathered, g_g_mix, reduced


def x_norm(xv, g):
    return xv * _rstd(xv) * g


def s5_mid(y0, y1, uv, d_skip):
    ypre = y0 + y1 + d_skip * uv
    yg = _gelu(ypre)
    return ypre, yg, yg


def mix_out_fwd(ya, yg, t, b_glu, g_oa, g_os):
    ys = yg * _sigmoid(t + b_glu)
    return jnp.concatenate([ya * _rstd(ya) * g_oa, ys * _rstd(ys) * g_os], axis=1)


def mix_out_bwd(d_y, ya, yg, t, b_glu, g_oa, g_os, *, a_width):
    sg = _sigmoid(t + b_glu)
    ys = yg * sg
    d_ya, c_goa = _rms_bwd(d_y[:, :a_width], ya, g_oa)
    d_ys, c_gos = _rms_bwd(d_y[:, a_width:], ys, g_os)
    d_t = d_ys * yg * sg * (1.0 - sg)
    return d_ya, d_ys * sg, d_t, c_goa, c_gos, d_t


def gelu_skip_bwd(d_yg, ypre, uv, d_skip):
    d_ypre = d_yg * _gelu_grad(ypre)
    return d_ypre, d_ypre * d_skip, d_ypre * uv


def swiglu_fwd(gv, uv):
    gv, uv = gv.astype(F32), uv.astype(F32)
    return gv * _sigmoid(gv) * uv


def swiglu_bwd(d_act, gv, uv):
    d_act, gv, uv = d_act.astype(F32), gv.astype(F32), uv.astype(F32)
    sg = _sigmoid(gv)
    return d_act * uv * (sg * (1.0 + gv * (1.0 - sg))), d_act * gv * sg


def loss_head(ffn_out, x1, target, *, inv_d):
    diff = ffn_out + x1 - target
    return diff * inv_d, diff * inv_d, diff * diff


def residual_rms_bwd(d_res, d_h, xv, g):
    dx, c_g = _rms_bwd(d_h, xv, g)
    return d_res + dx, c_g


_ANY = pl.BlockSpec(memory_space=pl.ANY)


def _mesh_place():
    return lax.axis_index("x"), lax.axis_index("y"), lax.axis_index("c")


def _chips(x, y):
    return [(x, y), (1 - x, y), (x, 1 - y), (1 - x, 1 - y)]


def _slab(px, py, pc):
    return 4 * px + 2 * py + pc


def _all_gather(arrs, *, name):
    n = len(arrs)

    def body(*refs):
        in_refs, out_refs = refs[:n], refs[n:2 * n]
        send_sems, recv_sems, local_sems = refs[2 * n:]
        x, y, c = _mesh_place()
        me, sibling = (x, y, c), (x, y, 1 - c)
        others = _chips(x, y)[1:]

        def copy(w, k, block, to, src=None):
            dst = out_refs[w].at[_slab(*block)]
            return pltpu.make_async_remote_copy(
                src_ref=dst if src is None else src, dst_ref=dst, send_sem=send_sems.at[7 * w + k],
                recv_sem=recv_sems.at[7 * w + k], device_id=to, device_id_type=MESH)

        mine = [pltpu.make_async_copy(in_refs[w], out_refs[w].at[_slab(*me)], local_sems.at[w]) for w in range(n)]
        first = []
        for w in range(n):
            mine[w].start()
            first.append(copy(w, 0, me, sibling, src=in_refs[w]))
            first += [copy(w, 1 + j, me, (*chip, c), src=in_refs[w]) for j, chip in enumerate(others)]
        for cp in first:
            cp.start()
        passed = []
        for j, chip in enumerate(others):
            for w in range(n):
                copy(w, 1 + j, (*chip, c), me).wait_recv()
                fwd = copy(w, 4 + j, (*chip, c), sibling)
                fwd.start()
                passed.append(fwd)
        for w in range(n):
            copy(w, 0, sibling, me).wait_recv()
        for j, chip in enumerate(others):
            for w in range(n):
                copy(w, 4 + j, (*chip, 1 - c), me).wait_recv()
        for cp in first + passed:
            cp.wait_send()
        for cp in mine:
            cp.wait()

    return pl.pallas_call(
        body, name=name, in_specs=[_ANY] * n, out_specs=[_ANY] * n,
        out_shape=[jax.ShapeDtypeStruct((N_DEV,) + a.shape, a.dtype) for a in arrs],
        scratch_shapes=[pltpu.SemaphoreType.DMA((7 * n,)), pltpu.SemaphoreType.DMA((7 * n,)),
                        pltpu.SemaphoreType.DMA((n,))],
        compiler_params=pltpu.CompilerParams(has_side_effects=True),
    )(*arrs)


def _adamw(w, m, v, parts, *, name, slab, tr=256):
    rows, cols = w.shape
    tr = _tile(rows, tr)
    n_p = len(parts)

    def body(slab_ref, *refs):
        w_ref, m_ref, v_ref = refs[:3]
        p_refs = refs[3:3 + n_p]
        g_ref, d_ref, nm_ref, nv_ref = refs[3 + n_p:]
        g = None
        for (_, lead), r in zip(parts, p_refs):
            for piece in ([r[...]] if lead is None else [r[i] for i in range(lead)]):
                g = piece.astype(F32) if g is None else g + piece.astype(F32)
        new_m = ADAM_B1 * m_ref[...] + (1.0 - ADAM_B1) * g
        new_v = ADAM_B2 * v_ref[...] + (1.0 - ADAM_B2) * (g * g)
        m_hat = new_m / (1.0 - ADAM_B1 ** ADAM_STEP)
        v_hat = new_v / (1.0 - ADAM_B2 ** ADAM_STEP)
        g_ref[...] = g
        d_ref[...] = -ADAM_LR * (m_hat / (jnp.sqrt(v_hat) + ADAM_EPS) + ADAM_WD * w_ref[...])
        nm_ref[...] = new_m
        nv_ref[...] = new_v

    tile = pl.BlockSpec((tr, cols), lambda i, s: (i, 0))
    p_specs = [pl.BlockSpec((None, tr, cols), lambda i, s: (s[0], i, 0)) if lead is None
               else pl.BlockSpec((lead, tr, cols), lambda i, s: (0, i, 0)) for _, lead in parts]
    return pl.pallas_call(
        body, name=name,
        grid_spec=pltpu.PrefetchScalarGridSpec(num_scalar_prefetch=1, grid=(rows // tr,),
                                               in_specs=[tile] * 3 + p_specs, out_specs=[tile] * 4),
        out_shape=[jax.ShapeDtypeStruct((rows, cols), F32)] * 4, compiler_params=_params(),
    )(jnp.reshape(slab, (1,)).astype(jnp.int32), w, m, v, *[a for a, _ in parts])


_DENSE_MIN = 256 * 128


def _as_rows(a):
    if a.shape[-1] < 128 and a.size >= _DENSE_MIN and a.size % 128 == 0:
        return a.reshape(-1, 128)
    return a.reshape(-1, a.shape[-1])


BIG = ("w_in", "w_glu", "w_out", "w_ffn_gate", "w_ffn_up", "w_ffn_down")
WEIGHTS = ("g_mix", "w_in", "q_gain", "k_gain", "rpb", "ssm_a_re", "ssm_a_im", "ssm_b_re", "ssm_b_im", "ssm_c_re",
           "ssm_c_im", "ssm_log_step", "ssm_d", "w_glu", "b_glu", "g_out_attn", "g_out_ssm", "w_out", "g_ffn",
           "w_ffn_gate", "w_ffn_up", "w_ffn_down")
SMALL = tuple(n for n in WEIGHTS if n not in BIG)
SMALL_LATE = tuple(n for n in SMALL if n != "g_mix")
VECTORS = ("g_mix", "q_gain", "k_gain", "ssm_d", "b_glu", "g_out_attn", "g_out_ssm", "g_ffn")


def kernel(x, g_mix, w_in, q_gain, k_gain, rpb, ssm_a_re, ssm_a_im, ssm_b_re, ssm_b_im, ssm_c_re, ssm_c_im, ssm_log_step, ssm_d, w_glu, b_glu, g_out_attn, g_out_ssm, w_out, g_ffn, w_ffn_gate, w_ffn_up, w_ffn_down, loss_target, m_g_mix, m_w_in, m_q_gain, m_k_gain, m_rpb, m_ssm_a_re, m_ssm_a_im, m_ssm_b_re, m_ssm_b_im, m_ssm_c_re, m_ssm_c_im, m_ssm_log_step, m_ssm_d, m_w_glu, m_b_glu, m_g_out_attn, m_g_out_ssm, m_w_out, m_g_ffn, m_w_ffn_gate, m_w_ffn_up, m_w_ffn_down, v_g_mix, v_w_in, v_q_gain, v_k_gain, v_rpb, v_ssm_a_re, v_ssm_a_im, v_ssm_b_re, v_ssm_b_im, v_ssm_c_re, v_ssm_c_im, v_ssm_log_step, v_ssm_d, v_w_glu, v_b_glu, v_g_out_attn, v_g_out_ssm, v_w_out, v_g_ffn, v_w_ffn_gate, v_w_ffn_up, v_w_ffn_down):
    wts = dict(g_mix=g_mix, w_in=w_in, q_gain=q_gain, k_gain=k_gain, rpb=rpb, ssm_a_re=ssm_a_re, ssm_a_im=ssm_a_im,
               ssm_b_re=ssm_b_re, ssm_b_im=ssm_b_im, ssm_c_re=ssm_c_re, ssm_c_im=ssm_c_im, ssm_log_step=ssm_log_step,
               ssm_d=ssm_d, w_glu=w_glu, b_glu=b_glu, g_out_attn=g_out_attn, g_out_ssm=g_out_ssm, w_out=w_out,
               g_ffn=g_ffn, w_ffn_gate=w_ffn_gate, w_ffn_up=w_ffn_up, w_ffn_down=w_ffn_down)
    mom = dict(g_mix=m_g_mix, w_in=m_w_in, q_gain=m_q_gain, k_gain=m_k_gain, rpb=m_rpb, ssm_a_re=m_ssm_a_re,
               ssm_a_im=m_ssm_a_im, ssm_b_re=m_ssm_b_re, ssm_b_im=m_ssm_b_im, ssm_c_re=m_ssm_c_re,
               ssm_c_im=m_ssm_c_im, ssm_log_step=m_ssm_log_step, ssm_d=m_ssm_d, w_glu=m_w_glu, b_glu=m_b_glu,
               g_out_attn=m_g_out_attn, g_out_ssm=m_g_out_ssm, w_out=m_w_out, g_ffn=m_g_ffn,
               w_ffn_gate=m_w_ffn_gate, w_ffn_up=m_w_ffn_up, w_ffn_down=m_w_ffn_down)
    var = dict(g_mix=v_g_mix, w_in=v_w_in, q_gain=v_q_gain, k_gain=v_k_gain, rpb=v_rpb, ssm_a_re=v_ssm_a_re,
               ssm_a_im=v_ssm_a_im, ssm_b_re=v_ssm_b_re, ssm_b_im=v_ssm_b_im, ssm_c_re=v_ssm_c_re,
               ssm_c_im=v_ssm_c_im, ssm_log_step=v_ssm_log_step, ssm_d=v_ssm_d, w_glu=v_w_glu, b_glu=v_b_glu,
               g_out_attn=v_g_out_attn, g_out_ssm=v_g_out_ssm, w_out=v_w_out, g_ffn=v_g_ffn,
               w_ffn_gate=v_w_ffn_gate, w_ffn_up=v_w_ffn_up, w_ffn_down=v_w_ffn_down)
    ix, iy, ic = _mesh_place()
    me = _slab(ix, iy, ic)

    shard = {n: wts[n][0] for n in BIG}
    shard_b = {n: shard[n].astype(BF16) for n in BIG}
    p = {n: (wts[n][0].reshape(1, -1) if n in VECTORS else wts[n][0]) for n in SMALL}

    loss, grad_x, small_gathered, g_g_mix, reduced = _local_step(x[0], loss_target[0], p, shard_b)
    loss = lax.psum(loss, ("x", "y", "c"))
    out = {}
    for n in BIG:
        slabs, from_sibling, from_chips = reduced[n]
        rows, cols = slabs.shape[1:]
        res = _adamw(shard[n].reshape(rows, cols), mom[n][0].reshape(rows, cols), var[n][0].reshape(rows, cols),
                     [(slabs, None), (from_sibling, 1), (from_chips, 3)], name=f"adamw_{n}", slab=me)
        out[n] = [r.reshape(wts[n].shape) for r in res]

    small_gathered["g_mix"] = _all_gather([g_g_mix], name="gather_g_mix")[0]
    for n in SMALL:
        res = _adamw(_as_rows(wts[n]), _as_rows(mom[n]), _as_rows(var[n]), [(small_gathered[n], N_DEV)],
                     name=f"adamw_{n}", slab=me, tr=1024)
        out[n] = [r.reshape(wts[n].shape) for r in res]

    return (loss, grad_x[None], *[out[n][0] for n in WEIGHTS], *[out[n][1] for n in WEIGHTS],
            *[out[n][2] for n in WEIGHTS], *[out[n][3] for n in WEIGHTS])
```

```python
import functools
import math

import jax
import jax.numpy as jnp
from jax import lax
from jax.experimental import pallas as pl
from jax.experimental.pallas import tpu as pltpu

F32 = jnp.float32
BF16 = jnp.bfloat16

N_DEV = 8
GRID_W = 64
WIN_H = 8
WIN_W = 16
HEAD_DIM = 64
HEADS_PER_GROUP = 4
GROUP_LANES = HEADS_PER_GROUP * HEAD_DIM
SSM_C = 16
SSM_P = 64
GROUPS_PER_TILE = 8
U_TILE = GROUPS_PER_TILE * SSM_C
ST_TILE = GROUPS_PER_TILE * SSM_P
TILES_TOGETHER = 4
SUBLANES = 8
RMS_EPS = 1e-6
NEG_INF = -1e30
A_RE_MAX = -1e-4
ADAM_LR, ADAM_B1, ADAM_B2, ADAM_EPS, ADAM_WD, ADAM_STEP = 0.001, 0.9, 0.999, 1e-08, 0.01, 10
VMEM_LIMIT_V7X = 56 * 1024 * 1024
MESH = pl.DeviceIdType.MESH

_NN = (((1,), (0,)), ((), ()))
_NT = (((1,), (1,)), ((), ()))
_TN = (((0,), (0,)), ((), ()))
_DIMS = {"nn": _NN, "nt": _NT, "tn": _TN}


def _params(**kw):
    return pltpu.CompilerParams(vmem_limit_bytes=VMEM_LIMIT_V7X, **kw)


def _dot(a, b, dims=_NN):
    return lax.dot_general(a, b, dims, preferred_element_type=F32)


def _mm(a, b, *, name, grid, a_spec, b_spec, o_spec, o_shape, dims, k_axis=None, res=None, out_dtype=F32,
        exact=False, second=None, ride=None, groups=1):
    dn = _DIMS[dims]
    nk = 1 if k_axis is None else grid[k_axis]
    acc_shape = tuple(d for d in o_spec.block_shape if d is not None)
    n_in = 2 + (2 if second is not None else 0)

    def body(*refs):
        a_ref, b_ref = refs[:2]
        r_ref = refs[n_in] if res is not None else None
        o_ref, acc = refs[-2:]
        def product(x_ref, y_ref):
            if groups == 1:
                return _dot(x_ref[...].astype(BF16), y_ref[...].astype(BF16), dn)
            total, width = None, x_ref.shape[-1] // groups
            for s in range(groups):
                x = x_ref[s] if len(x_ref.shape) == 3 else x_ref[:, s * width:(s + 1) * width]
                t = _dot(x.astype(BF16), y_ref[s].astype(BF16), dn)
                total = t if total is None else total + t
            return total

        if exact:
            p = lax.dot_general(a_ref[...], b_ref[...], dn, precision=lax.Precision.HIGHEST,
                                preferred_element_type=F32)
        else:
            p = product(a_ref, b_ref)
        if second is not None:
            p = p + product(refs[2], refs[3])

        def finish(v):
            if r_ref is not None:
                v = v + r_ref[...].astype(F32)
            o_ref[...] = v.astype(out_dtype)

        if nk == 1:
            finish(p)
        else:
            k = pl.program_id(k_axis)

            @pl.when(k == 0)
            def _():
                acc[...] = p

            @pl.when(k > 0)
            def _():
                acc[...] += p

            @pl.when(k == nk - 1)
            def _():
                finish(acc[...])

    ins = [a, b] + (list(second) if second is not None else []) + ([res] if res is not None else [])
    in_specs = [a_spec, b_spec] * (n_in // 2) + ([o_spec] if res is not None else [])
    first, last = _grid_ends(grid)
    (out,), rode = _call(
        body, name=name, grid=grid, in_specs=in_specs, out_specs=[o_spec],
        out_shape=[jax.ShapeDtypeStruct(o_shape, out_dtype)],
        scratch_shapes=[pltpu.VMEM(acc_shape if nk > 1 else (SUBLANES, 128), F32)],
        args=ins, ride=ride, first=first, last=last)
    return out if ride is None else (out, rode)


def _tile(n, want):
    if n <= want:
        return n
    t = want
    while n % t:
        t //= 2
    return t


def _mm_plain(a, b, dims, *, name, res=None, out_dtype=F32, tm=512, tn=512, tk=512, exact=False, ride=None):
    if dims == "nn":
        (m, k), n = a.shape, b.shape[1]
    elif dims == "nt":
        (m, k), n = a.shape, b.shape[0]
    else:
        (k, m), n = a.shape, b.shape[1]
    tm, tn = _tile(m, tm), _tile(n, tn)
    if dims == "tn":
        tk = _tile(k, tk)
        grid = (m // tm, n // tn, k // tk)
        a_spec = pl.BlockSpec((tk, tm), lambda i, j, kk: (kk, i))
        b_spec = pl.BlockSpec((tk, tn), lambda i, j, kk: (kk, j))
        o_spec = pl.BlockSpec((tm, tn), lambda i, j, kk: (i, j))
        return _mm(a, b, name=name, grid=grid, a_spec=a_spec, b_spec=b_spec, o_spec=o_spec, o_shape=(m, n),
                   dims=dims, k_axis=2, res=res, out_dtype=out_dtype)
    grid = (n // tn, m // tm)
    a_spec = pl.BlockSpec((tm, k), lambda j, i: (i, 0))
    if dims == "nn":
        b_spec = pl.BlockSpec((k, tn), lambda j, i: (0, j))
    else:
        b_spec = pl.BlockSpec((tn, k), lambda j, i: (j, 0))
    o_spec = pl.BlockSpec((tm, tn), lambda j, i: (i, j))
    return _mm(a, b, name=name, grid=grid, a_spec=a_spec, b_spec=b_spec, o_spec=o_spec, o_shape=(m, n), dims=dims,
               res=res, out_dtype=out_dtype, exact=exact, ride=ride)


def _rowwise(fn, tiled, bcast, outs, accs=(), *, name, tm=256, flipped=(), ride=None):
    m = tiled[0].shape[0]
    tm = _tile(m, tm)
    n_t, n_b, n_o, n_f = len(tiled), len(bcast), len(outs), len(flipped)

    def body(*refs):
        ins = [r[...] for r in refs[: n_t + n_b]]
        o_refs = refs[n_t + n_b: n_t + n_b + n_o]
        f_refs = refs[n_t + n_b + n_o: n_t + n_b + n_o + n_f]
        a_refs = refs[n_t + n_b + n_o + n_f:]
        res = fn(*ins)
        if not isinstance(res, (tuple, list)):
            res = (res,)
        for r, v in zip(o_refs, res[:n_o]):
            r[...] = v.astype(r.dtype)
        for r, v in zip(f_refs, res[n_o:n_o + n_f]):
            r[...] = v.astype(F32).T.astype(r.dtype)
        first = pl.program_id(0) == 0
        for r, v in zip(a_refs, res[n_o + n_f:]):
            s = jnp.sum(v, axis=0, keepdims=True)

            @pl.when(first)
            def _():
                r[...] = s

            @pl.when(jnp.logical_not(first))
            def _():
                r[...] += s

    in_specs = [pl.BlockSpec((tm, t.shape[1]), lambda i: (i, 0)) for t in tiled]
    in_specs += [pl.BlockSpec(b.shape, lambda i, nd=b.ndim: (0,) * nd) for b in bcast]
    out_specs = [pl.BlockSpec((tm, n), lambda i: (i, 0)) for n, _ in outs]
    out_specs += [pl.BlockSpec((n, tm), lambda i, per=m // tm // g: (i // per, i % per)) for n, _, g in flipped]
    out_specs += [pl.BlockSpec((1, n), lambda i: (0, 0)) for n in accs]
    out_shape = [jax.ShapeDtypeStruct((m, n), dt) for n, dt in outs]
    out_shape += [jax.ShapeDtypeStruct((g * n, m // g), dt) for n, dt, g in flipped]
    out_shape += [jax.ShapeDtypeStruct((1, n), F32) for n in accs]
    first, last = _grid_ends((m // tm,))
    res, rode = _call(body, name=name, grid=(m // tm,), in_specs=in_specs, out_specs=out_specs, out_shape=out_shape,
                      scratch_shapes=[], args=list(tiled) + list(bcast), ride=ride, first=first, last=last)
    return res if ride is None else (res, rode)


def _rstd(x):
    return lax.rsqrt(jnp.mean(x * x, axis=-1, keepdims=True) + RMS_EPS)


def _rms_bwd(dh, x, g):
    xh = x * _rstd(x)
    dxh = dh * g
    dx = _rstd(x) * (dxh - xh * jnp.mean(dxh * xh, axis=-1, keepdims=True))
    return dx, dh * xh


def _sigmoid(x):
    return 1.0 / (1.0 + jnp.exp(-x))


_GELU_K = math.sqrt(2.0 / math.pi)
_GELU_C = 0.044715


def _gelu(x):
    return 0.5 * x * (1.0 + jnp.tanh(_GELU_K * (x + _GELU_C * x * x * x)))


def _gelu_grad(x):
    th = jnp.tanh(_GELU_K * (x + _GELU_C * x * x * x))
    return 0.5 * (1.0 + th) + 0.5 * x * (1.0 - th * th) * _GELU_K * (1.0 + 3.0 * _GELU_C * x * x)


class _Exchange:
    def __init__(self, arrays, outs, n_sems, sends, recvs=None, local=None, aliases=None):
        self.arrays, self.outs, self.n_sems = list(arrays), list(outs), n_sems
        self.sends, self.local, self.aliases = sends, local, aliases or {}
        self.recvs = recvs or (lambda i, o: [(k, dst) for k, _, dst, _ in sends(i, o)])

    def __add__(self, other):
        na, no, ns = len(self.arrays), len(self.outs), self.n_sems
        mine = lambda f: (lambda i, o: f(i[:na], o[:no]))
        shift = lambda f, at: (lambda i, o: [(k + ns,) + tuple(rest) for k, *rest in f(i[na:], o[no:])]) if at else None
        both = lambda f, g: (lambda i, o: f(i, o) + g(i, o))
        local = None
        if self.local or other.local:
            la = mine(self.local) if self.local else (lambda i, o: [])
            lb = (lambda i, o: other.local(i[na:], o[no:])) if other.local else (lambda i, o: [])
            local = both(la, lb)
        aliases = dict(self.aliases)
        aliases.update({na + i: no + o for i, o in other.aliases.items()})
        return _Exchange(self.arrays + other.arrays, self.outs + other.outs, ns + other.n_sems,
                         both(mine(self.sends), shift(other.sends, True)),
                         both(mine(self.recvs), shift(other.recvs, True)), local, aliases)

    def descriptors(self, in_refs, out_refs, send_sems, recv_sems, local_sems):
        me = _mesh_place()
        remote = lambda k, src, dst, to: pltpu.make_async_remote_copy(
            src_ref=src, dst_ref=dst, send_sem=send_sems.at[k], recv_sem=recv_sems.at[k], device_id=to,
            device_id_type=MESH)
        out = [remote(*s) for s in self.sends(in_refs, out_refs)]
        arrive = [remote(k, dst, dst, me) for k, dst in self.recvs(in_refs, out_refs)]
        own = [pltpu.make_async_copy(src, dst, local_sems.at[i])
               for i, (src, dst) in enumerate(self.local(in_refs, out_refs) if self.local else [])]
        return out, arrive, own

    def start(self, *refs):
        out, _, own = self.descriptors(*refs)
        for cp in own + out:
            cp.start()

    def finish(self, *refs):
        out, arrive, own = self.descriptors(*refs)
        for cp in arrive:
            cp.wait_recv()
        for cp in out:
            cp.wait_send()
        for cp in own:
            cp.wait()


def _call(body, *, name, grid, in_specs, out_specs, out_shape, scratch_shapes, args, ride=None, first=None, last=None):
    if ride is None:
        res = pl.pallas_call(body, name=name, grid=grid, in_specs=in_specs, out_specs=out_specs, out_shape=out_shape,
                             scratch_shapes=scratch_shapes, compiler_params=_params())(*args)
        return list(res), []
    n_in, n_out, n_scr = len(in_specs), len(out_specs), len(scratch_shapes)
    r_in, r_out = len(ride.arrays), len(ride.outs)

    def wrapped(*refs):
        ins, refs = refs[:n_in], refs[n_in:]
        x_in, refs = refs[:r_in], refs[r_in:]
        outs, refs = refs[:n_out], refs[n_out:]
        x_out, refs = refs[:r_out], refs[r_out:]
        scr, sems = refs[:n_scr], refs[n_scr:]

        @pl.when(first())
        def _():
            ride.start(x_in, x_out, *sems)

        body(*ins, *outs, *scr)

        @pl.when(last())
        def _():
            ride.finish(x_in, x_out, *sems)

    n_local = max(1, len(ride.arrays))
    res = pl.pallas_call(
        wrapped, name=name, grid=grid, in_specs=list(in_specs) + [_ANY] * r_in,
        out_specs=list(out_specs) + [_ANY] * r_out, out_shape=list(out_shape) + ride.outs,
        scratch_shapes=list(scratch_shapes) + [pltpu.SemaphoreType.DMA((ride.n_sems,)),
                                               pltpu.SemaphoreType.DMA((ride.n_sems,)),
                                               pltpu.SemaphoreType.DMA((n_local,))],
        input_output_aliases={n_in + i: n_out + o for i, o in ride.aliases.items()},
        compiler_params=_params(has_side_effects=True),
    )(*args, *ride.arrays)
    return list(res[:n_out]), list(res[n_out:])


def _gather_first(shards):
    def sends(i, o):
        x, y, c = _mesh_place()
        peers = [(x, y, 1 - c)] + [(px, py, c) for px, py in _chips(x, y)[1:]]
        return [(4 * w + k, i[w], o[w].at[_slab(x, y, c)], to) for w in range(len(i)) for k, to in enumerate(peers)]

    def recvs(i, o):
        x, y, c = _mesh_place()
        peers = [(x, y, 1 - c)] + [(px, py, c) for px, py in _chips(x, y)[1:]]
        return [(4 * w + k, o[w].at[_slab(*peer)]) for w in range(len(i)) for k, peer in enumerate(peers)]

    def local(i, o):
        return [(i[w], o[w].at[_slab(*_mesh_place())]) for w in range(len(i))]

    outs = [jax.ShapeDtypeStruct((N_DEV,) + a.shape, a.dtype) for a in shards]
    return _Exchange(shards, outs, 4 * len(shards), sends, recvs, local)


def _gather_second(gathered):
    def sends(i, o):
        x, y, c = _mesh_place()
        return [(3 * w + j, o[w].at[_slab(px, py, c)], o[w].at[_slab(px, py, c)], (x, y, 1 - c))
                for w in range(len(o)) for j, (px, py) in enumerate(_chips(x, y)[1:])]

    def recvs(i, o):
        x, y, c = _mesh_place()
        return [(3 * w + j, o[w].at[_slab(px, py, 1 - c)])
                for w in range(len(o)) for j, (px, py) in enumerate(_chips(x, y)[1:])]

    outs = [jax.ShapeDtypeStruct(a.shape, a.dtype) for a in gathered]
    return _Exchange(gathered, outs, 3 * len(gathered), sends, recvs, aliases={w: w for w in range(len(gathered))})


def _reduce_sibling(slabs):
    def sends(i, o):
        x, y, c = _mesh_place()
        return [(4 * w + k, i[w].at[_slab(px, py, 1 - c)], o[w].at[k], (x, y, 1 - c))
                for w in range(len(i)) for k, (px, py) in enumerate(_chips(x, y))]

    outs = [jax.ShapeDtypeStruct((4,) + a.shape[1:], a.dtype) for a in slabs]
    return _Exchange(slabs, outs, 4 * len(slabs), sends)


def _reduce_chips(partials):
    def sends(i, o):
        x, y, c = _mesh_place()
        return [(3 * w + k, i[w].at[k], o[w].at[k], (px, py, c))
                for w in range(len(i)) for k, (px, py) in enumerate(_chips(x, y)[1:])]

    outs = [jax.ShapeDtypeStruct(a.shape, a.dtype) for a in partials]
    return _Exchange(partials, outs, 3 * len(partials), sends)


def _head_masks():
    lane_head = lax.broadcasted_iota(jnp.int32, (1, GROUP_LANES), 1) // HEAD_DIM
    return [(lane_head == h).astype(F32) for h in range(HEADS_PER_GROUP)]


def _head_block_diag():
    r = lax.broadcasted_iota(jnp.int32, (GROUP_LANES, GROUP_LANES), 0) // HEAD_DIM
    c = lax.broadcasted_iota(jnp.int32, (GROUP_LANES, GROUP_LANES), 1) // HEAD_DIM
    return (r == c).astype(BF16)


def _head_mean(x, bd):
    hi = x.astype(BF16)
    lo = (x - hi.astype(F32)).astype(BF16)
    return (_dot(hi, bd) + _dot(lo, bd)) * (1.0 / HEAD_DIM)


def _stack_heads(x, masks):
    return jnp.concatenate([x * m for m in masks], axis=0)


def _unstack_heads(xs, masks):
    out = xs[0:GRID_W] * masks[0]
    for h in range(1, HEADS_PER_GROUP):
        out = out + xs[h * GRID_W:(h + 1) * GRID_W] * masks[h]
    return out


def _row_start(r, rows):
    return jnp.clip(r - WIN_H // 2, 0, rows - WIN_H)


ROWS_PER_STEP = 4


def _attn_common_specs(seq, n_hg, rows):
    win_keys = WIN_H * GRID_W
    q_spec = pl.BlockSpec((ROWS_PER_STEP * GRID_W, GROUP_LANES), lambda g, r: (r, g))
    k_spec = pl.BlockSpec((seq, GROUP_LANES), lambda g, r: (0, n_hg + g))
    v_spec = pl.BlockSpec((seq, GROUP_LANES), lambda g, r: (0, 2 * n_hg + g))
    gain_spec = pl.BlockSpec((1, GROUP_LANES), lambda g, r: (0, 0))

    def variant(r):
        return _row_start(r, rows) - r + (WIN_H - 1)

    bias_specs = [pl.BlockSpec((None, None, HEADS_PER_GROUP, GRID_W, win_keys),
                               lambda g, r, h=h: (g, variant(ROWS_PER_STEP * r + h), 0, 0, 0))
                  for h in range(ROWS_PER_STEP)]
    return q_spec, k_spec, v_spec, gain_spec, bias_specs, variant


def _attn_prepare_kv(k_ref, v_ref, kg, kn_scr, vb_scr, bd, seq):
    chunk = _tile(seq, 512)

    def step(c, carry):
        rows = pl.ds(pl.multiple_of(c * chunk, chunk), chunk)
        k = k_ref[rows, :]
        kn_scr[rows, :] = (k * lax.rsqrt(_head_mean(k * k, bd) + RMS_EPS) * kg).astype(BF16)
        vb_scr[rows, :] = v_ref[rows, :].astype(BF16)
        return carry

    lax.fori_loop(0, seq // chunk, step, 0)


def _attn_probs(qn, kw, bias, masks):
    qs = _stack_heads(qn, masks).astype(BF16)
    s = _dot(qs, kw, _NT) * (1.0 / math.sqrt(HEAD_DIM)) + bias
    m = jnp.max(s, axis=-1, keepdims=True)
    p = jnp.exp(s - m)
    return qs, p * (1.0 / jnp.sum(p, axis=-1, keepdims=True))


def _grid_ends(grid):
    first = lambda: functools.reduce(jnp.logical_and, [pl.program_id(a) == 0 for a in range(len(grid))])
    last = lambda: functools.reduce(jnp.logical_and, [pl.program_id(a) == n - 1 for a, n in enumerate(grid)])
    return first, last


def _attn_fwd(z, qg4, kg4, btab, ride=None):
    seq = z.shape[0]
    a_width = btab.shape[0] * GROUP_LANES
    n_hg, rows, win_keys = btab.shape[0], seq // GRID_W, WIN_H * GRID_W
    q_spec, k_spec, v_spec, gain_spec, bias_specs, _ = _attn_common_specs(seq, n_hg, rows)
    grid = (n_hg, rows // ROWS_PER_STEP)

    def body(q_ref, k_ref, v_ref, qg_ref, kg_ref, *rest):
        b_refs, (o_ref, kn_scr, vb_scr) = rest[:ROWS_PER_STEP], rest[ROWS_PER_STEP:]
        bd, masks = _head_block_diag(), _head_masks()

        @pl.when(pl.program_id(1) == 0)
        def _():
            _attn_prepare_kv(k_ref, v_ref, kg_ref[...], kn_scr, vb_scr, bd, seq)

        for h in range(ROWS_PER_STEP):
            r = ROWS_PER_STEP * pl.program_id(1) + h
            mine = slice(h * GRID_W, (h + 1) * GRID_W)
            win = pl.ds(pl.multiple_of(_row_start(r, rows) * GRID_W, GRID_W), win_keys)
            q = q_ref[mine, :]
            qn = q * lax.rsqrt(_head_mean(q * q, bd) + RMS_EPS) * qg_ref[...]
            bias = b_refs[h][...].reshape(HEADS_PER_GROUP * GRID_W, win_keys)
            _, p = _attn_probs(qn, kn_scr[win, :], bias, masks)
            o_ref[mine, :] = _unstack_heads(_dot(p.astype(BF16), vb_scr[win, :]), masks)

    first, last = _grid_ends(grid)
    (ya,), rode = _call(
        body, name="attn_fwd", grid=grid,
        in_specs=[q_spec, k_spec, v_spec, gain_spec, gain_spec] + bias_specs,
        out_specs=[pl.BlockSpec((ROWS_PER_STEP * GRID_W, GROUP_LANES), lambda g, r: (r, g))],
        out_shape=[jax.ShapeDtypeStruct((seq, a_width), F32)],
        scratch_shapes=[pltpu.VMEM((seq, GROUP_LANES), BF16), pltpu.VMEM((seq, GROUP_LANES), BF16)],
        args=(z, z, z, qg4, kg4) + (btab,) * ROWS_PER_STEP, ride=ride, first=first, last=last)
    return ya, rode


def _attn_bwd(z, d_out, qg4, kg4, btab, ride=None):
    seq = z.shape[0]
    n_hg, rows, win_keys = btab.shape[0], seq // GRID_W, WIN_H * GRID_W
    a_width = n_hg * GROUP_LANES
    q_spec, k_spec, v_spec, gain_spec, bias_specs, variant = _attn_common_specs(seq, n_hg, rows)
    scale = 1.0 / math.sqrt(HEAD_DIM)
    grid = (n_hg, rows // ROWS_PER_STEP)

    def body(q_ref, k_ref, v_ref, do_ref, qg_ref, kg_ref, *rest):
        b_refs, rest = rest[:ROWS_PER_STEP], rest[ROWS_PER_STEP:]
        dq_ref, dk_out, dv_out, db_ref, dqg_ref, dkg_ref, kn_scr, vb_scr, dk_ref, dv_ref = rest
        bd, masks = _head_block_diag(), _head_masks()

        @pl.when(pl.program_id(1) == 0)
        def _():
            _attn_prepare_kv(k_ref, v_ref, kg_ref[...], kn_scr, vb_scr, bd, seq)
            dk_ref[...] = jnp.zeros_like(dk_ref)
            dv_ref[...] = jnp.zeros_like(dv_ref)
            db_ref[...] = jnp.zeros_like(db_ref)
            dqg_ref[...] = jnp.zeros_like(dqg_ref)

        qg = qg_ref[...]
        for h in range(ROWS_PER_STEP):
            r = ROWS_PER_STEP * pl.program_id(1) + h
            mine = slice(h * GRID_W, (h + 1) * GRID_W)
            win = pl.ds(pl.multiple_of(_row_start(r, rows) * GRID_W, GRID_W), win_keys)
            q = q_ref[mine, :]
            rq = lax.rsqrt(_head_mean(q * q, bd) + RMS_EPS)
            qh = q * rq
            kw, vw = kn_scr[win, :], vb_scr[win, :]
            bias = b_refs[h][...].reshape(HEADS_PER_GROUP * GRID_W, win_keys)
            qs, p = _attn_probs(qh * qg, kw, bias, masks)
            dos = _stack_heads(do_ref[mine, :], masks).astype(BF16)
            dp = _dot(dos, vw, _NT)
            ds = p * (dp - jnp.sum(p * dp, axis=-1, keepdims=True))
            db_ref[variant(r)] += ds.reshape(HEADS_PER_GROUP, GRID_W, win_keys)
            dsb = ds.astype(BF16)
            dqn = _unstack_heads(_dot(dsb, kw), masks) * scale
            dk_ref[win, :] += _dot(dsb, qs, _TN) * scale
            dv_ref[win, :] += _dot(p.astype(BF16), dos, _TN)
            dqg_ref[...] += jnp.sum(dqn * qh, axis=0, keepdims=True)
            dqh = dqn * qg
            dq_ref[mine, :] = (rq * (dqh - qh * _head_mean(dqh * qh, bd))).astype(BF16)

        @pl.when(pl.program_id(1) == grid[1] - 1)
        def _():
            chunk = _tile(seq, 512)
            kg = kg_ref[...]

            def step(c, dkg):
                rws = pl.ds(pl.multiple_of(c * chunk, chunk), chunk)
                k = k_ref[rws, :]
                rk = lax.rsqrt(_head_mean(k * k, bd) + RMS_EPS)
                kh = k * rk
                dkn = dk_ref[rws, :]
                dkh = dkn * kg
                dk_out[rws, :] = (rk * (dkh - kh * _head_mean(dkh * kh, bd))).astype(BF16)
                dv_out[rws, :] = dv_ref[rws, :].astype(BF16)
                return dkg + jnp.sum(dkn * kh, axis=0, keepdims=True)

            dkg_ref[...] = lax.fori_loop(0, seq // chunk, step, jnp.zeros((1, GROUP_LANES), F32))

    col_spec = pl.BlockSpec((seq, GROUP_LANES), lambda g, r: (0, g))
    gsum_spec = pl.BlockSpec((None, 1, GROUP_LANES), lambda g, r: (g, 0, 0))
    first, last = _grid_ends(grid)
    rows_spec = pl.BlockSpec((ROWS_PER_STEP * GRID_W, GROUP_LANES), lambda g, r: (r, g))
    return _call(
        body, name="attn_bwd", grid=grid,
        in_specs=[q_spec, k_spec, v_spec, rows_spec, gain_spec, gain_spec] + bias_specs,
        out_specs=[rows_spec, col_spec, col_spec,
                   pl.BlockSpec((None, WIN_H, HEADS_PER_GROUP, GRID_W, win_keys), lambda g, r: (g, 0, 0, 0, 0)),
                   gsum_spec, gsum_spec],
        out_shape=[jax.ShapeDtypeStruct((seq, a_width), BF16)] * 3
        + [jax.ShapeDtypeStruct(btab.shape, F32)]
        + [jax.ShapeDtypeStruct((n_hg, 1, GROUP_LANES), F32)] * 2,
        scratch_shapes=[pltpu.VMEM((seq, GROUP_LANES), BF16), pltpu.VMEM((seq, GROUP_LANES), BF16),
                        pltpu.VMEM((seq, GROUP_LANES), F32), pltpu.VMEM((seq, GROUP_LANES), F32)],
        args=(z, z, z, d_out, qg4, kg4) + (btab,) * ROWS_PER_STEP, ride=ride, first=first, last=last)


DC_SLOTS = 2 * WIN_W


def _bias_spread(c):
    shape = (WIN_H * DC_SLOTS, WIN_H * GRID_W)
    rows = lax.broadcasted_iota(jnp.int32, shape, 0)
    cols = lax.broadcasted_iota(jnp.int32, shape, 1)
    row_i, row_d = rows // DC_SLOTS, rows % DC_SLOTS
    col_i, kc = cols // GRID_W, cols % GRID_W
    col_start = jnp.clip(c - WIN_W // 2, 0, GRID_W - WIN_W)
    col_in = (kc >= col_start) & (kc < col_start + WIN_W)
    hit = (row_i == col_i) & (row_d == kc - c + (WIN_W - 1)) & col_in
    mask_slot = (row_i == 0) & (row_d == DC_SLOTS - 1) & jnp.logical_not(col_in)
    return jnp.where(hit, 1.0, jnp.where(mask_slot, NEG_INF, 0.0)).astype(F32)


def _bias_table(rpb):
    n_h = rpb.shape[0]
    n_hg = n_h // HEADS_PER_GROUP
    rows = jnp.stack([rpb[:, v:v + WIN_H] for v in range(WIN_H)], axis=1)
    rows = jnp.pad(rows, ((0, 0), (0, 0), (0, 0), (0, DC_SLOTS - rows.shape[-1])))
    rows = rows.at[:, :, 0, DC_SLOTS - 1].set(1.0)
    rows = rows.reshape(n_hg, HEADS_PER_GROUP, WIN_H, WIN_H * DC_SLOTS).transpose(0, 2, 1, 3)
    n_rows, win_keys, depth = n_h * WIN_H, WIN_H * GRID_W, WIN_H * DC_SLOTS

    def body(r_ref, o_ref):
        for cc in range(SUBLANES):
            spread = _bias_spread(pl.program_id(0) * SUBLANES + cc)
            o_ref[cc] = lax.dot_general(r_ref[...], spread, _NN, precision=lax.Precision.HIGHEST,
                                        preferred_element_type=F32)

    tab = pl.pallas_call(
        body, name="rpb_spread", grid=(GRID_W // SUBLANES,),
        in_specs=[pl.BlockSpec((n_rows, depth), lambda c: (0, 0))],
        out_specs=pl.BlockSpec((SUBLANES, n_rows, win_keys), lambda c: (c, 0, 0)),
        out_shape=jax.ShapeDtypeStruct((GRID_W, n_rows, win_keys), F32), compiler_params=_params(),
    )(rows.reshape(n_rows, depth))
    return tab.transpose(1, 0, 2).reshape(n_hg, WIN_H, HEADS_PER_GROUP, GRID_W, win_keys)


def _bias_grad(dtab, n_h):
    n_hg = n_h // HEADS_PER_GROUP
    n_rows, win_keys, depth = n_h * WIN_H, WIN_H * GRID_W, WIN_H * DC_SLOTS

    def body(d_ref, o_ref):
        total = None
        for cc in range(SUBLANES):
            spread = _bias_spread(pl.program_id(0) * SUBLANES + cc).astype(BF16)
            t = _dot(d_ref[cc].astype(BF16), spread, _NT)
            total = t if total is None else total + t

        @pl.when(pl.program_id(0) == 0)
        def _():
            o_ref[...] = total

        @pl.when(pl.program_id(0) > 0)
        def _():
            o_ref[...] += total

    d_rows = pl.pallas_call(
        body, name="rpb_diag_sum", grid=(GRID_W // SUBLANES,),
        in_specs=[pl.BlockSpec((SUBLANES, n_rows, win_keys), lambda c: (c, 0, 0))],
        out_specs=pl.BlockSpec((n_rows, depth), lambda c: (0, 0)),
        out_shape=jax.ShapeDtypeStruct((n_rows, depth), F32), compiler_params=_params(),
    )(dtab.reshape(n_rows, GRID_W, win_keys).transpose(1, 0, 2))
    d_rows = d_rows.reshape(n_hg, WIN_H, HEADS_PER_GROUP, WIN_H, DC_SLOTS).transpose(0, 2, 1, 3, 4)
    d_rows = d_rows.reshape(n_h, WIN_H, WIN_H, DC_SLOTS)[..., : 2 * WIN_W - 1]
    out = jnp.zeros((n_h, 2 * WIN_H - 1, 2 * WIN_W - 1), F32)
    for v in range(WIN_H):
        out = out.at[:, v:v + WIN_H].add(d_rows[:, v])
    return out


def _cmul(ar, ai, br, bi):
    return ar * br - ai * bi, ar * bi + ai * br


def _s5_discretize(a_re, a_im, dt, b_re, b_im, ride=None):
    c = b_re.shape[1]

    def fn(are, aim, dt_, bre, bim):
        lr, li = jnp.minimum(are, A_RE_MAX), aim
        mag = jnp.exp(lr * dt_)
        l1r, l1i = mag * jnp.cos(li * dt_), mag * jnp.sin(li * dt_)
        den = lr * lr + li * li
        nr, ni = l1r - 1.0, l1i
        cr, ci = (nr * lr + ni * li) / den, (ni * lr - nr * li) / den
        bbr, bbi = _cmul(cr, ci, bre, bim)
        shape = (are.shape[0], SUBLANES)
        lane = lax.broadcasted_iota(jnp.int32, shape, 1)
        pr, pi = l1r, l1i
        acc_r, acc_i = jnp.zeros(shape, F32), jnp.zeros(shape, F32)
        for k in range(SUBLANES):
            acc_r = jnp.where(lane == k, pr, acc_r)
            acc_i = jnp.where(lane == k, pi, acc_i)
            pr, pi = _cmul(pr, pi, l1r, l1i)
        return acc_r, acc_i, cr, ci, bbr, bbi

    return _rowwise(fn, [a_re, a_im, dt, b_re, b_im], [],
                    [(SUBLANES, F32), (SUBLANES, F32), (1, F32), (1, F32), (c, F32), (c, F32)],
                    name="s5_discretize", tm=1024, ride=ride)


def _s5_param_grads(a_re, a_im, dt, b_re, b_im, l1r, l1i, cr, ci, bbr, bbi, r_re, r_im, gb_re, gb_im):
    c = b_re.shape[1]

    def fn(are, aim, dt_, bre, bim, l1r_, l1i_, cr_, ci_, bbr_, bbi_, rr, ri, gbr, gbi):
        lr, li = jnp.minimum(are, A_RE_MAX), aim
        den = lr * lr + li * li
        dbr, dbi = _cmul(cr_, -ci_, gbr, gbi)
        gcr, gci = _cmul(bre, -bim, gbr, gbi)
        gcr, gci = jnp.sum(gcr, axis=1, keepdims=True), jnp.sum(gci, axis=1, keepdims=True)
        qr, qi = _cmul(bbr_, -bbi_, gbr, gbi)
        qr = rr - jnp.sum(qr, axis=1, keepdims=True)
        qi = ri - jnp.sum(qi, axis=1, keepdims=True)
        tr, ti = _cmul(gcr, gci, lr / den, li / den)
        ur, ui = _cmul(l1r_, -l1i_, tr, ti)
        gwr, gwi = qr + ur, qi + ui
        vr, vi = _cmul(cr_, -ci_, lr / den, li / den)
        vr, vi = _cmul(gcr, gci, vr, vi)
        glr, gli = dt_ * gwr - vr, dt_ * gwi - vi
        return jnp.where(are < A_RE_MAX, glr, 0.0), gli, (gwr * lr + gwi * li) * dt_, dbr, dbi

    return _rowwise(fn, [a_re, a_im, dt, b_re, b_im, l1r, l1i, cr, ci, bbr, bbi, r_re, r_im, gb_re, gb_im], [],
                    [(1, F32), (1, F32), (1, F32), (c, F32), (c, F32)], name="s5_param_grads", tm=1024)


def _s5_scan(v, win_re, win_im, tabs, wo_re, wo_im, *, reverse, name, t_chunk=512, ride=None):
    seq, width = v.shape
    n_tiles, n_state = width // U_TILE, width * (SSM_P // SSM_C)
    t_chunk = _tile(seq, t_chunk)
    n_chunks, n_blk = seq // t_chunk, t_chunk // SUBLANES
    last_row = 0 if reverse else SUBLANES - 1

    def chunk_of(j):
        return (n_chunks - 1 - j) if reverse else j

    def body(v_ref, wir_ref, wii_ref, tab_ref, wor_ref, woi_ref, sr_ref, si_ref, y_ref, carry, wr, wi):
        @pl.when(pl.program_id(0) == 0)
        def _():
            carry[...] = jnp.zeros_like(carry)

        for j0 in range(0, n_tiles, TILES_TOGETHER):
            tiles = list(range(j0, min(j0 + TILES_TOGETHER, n_tiles)))
            lanes = [slice(jt * ST_TILE, (jt + 1) * ST_TILE) for jt in tiles]
            for w, (jt, ls) in enumerate(zip(tiles, lanes)):
                vj = v_ref[:, jt * U_TILE:(jt + 1) * U_TILE].astype(BF16)
                xr = _dot(vj, wir_ref[jt]).reshape(n_blk, SUBLANES, ST_TILE)
                xi = _dot(vj, wii_ref[jt]).reshape(n_blk, SUBLANES, ST_TILE)
                for s, k in enumerate((1, 2, 4)):
                    sh = (SUBLANES - k) if reverse else k
                    tr, ti = pltpu.roll(xr, sh, 1), pltpu.roll(xi, sh, 1)
                    lr, li = tab_ref[2 * s, :, ls][None], tab_ref[2 * s + 1, :, ls][None]
                    xr, xi = xr + lr * tr - li * ti, xi + lr * ti + li * tr
                wr[w] = xr.reshape(t_chunk, ST_TILE)
                wi[w] = xi.reshape(t_chunk, ST_TILE)
            powers = [(tab_ref[6, :, ls], tab_ref[7, :, ls]) for ls in lanes]

            def blk(b, c, powers=powers):
                bb = (n_blk - 1 - b) if reverse else b
                rows = pl.ds(pl.multiple_of(bb * SUBLANES, SUBLANES), SUBLANES)
                out = []
                for w, ((cr, ci), (lr, li)) in enumerate(zip(c, powers)):
                    xr = wr[w, rows, :] + lr * cr - li * ci
                    xi = wi[w, rows, :] + lr * ci + li * cr
                    wr[w, rows, :], wi[w, rows, :] = xr, xi
                    shape = (SUBLANES, ST_TILE)
                    out.append((jnp.broadcast_to(xr[last_row:last_row + 1], shape),
                                jnp.broadcast_to(xi[last_row:last_row + 1], shape)))
                return tuple(out)

            ends = lax.fori_loop(0, n_blk, blk, tuple((carry[0, :, ls], carry[1, :, ls]) for ls in lanes), unroll=2)
            for w, (jt, ls) in enumerate(zip(tiles, lanes)):
                carry[0, :, ls], carry[1, :, ls] = ends[w]
                xr_b, xi_b = wr[w].astype(BF16), wi[w].astype(BF16)
                sr_ref[:, ls], si_ref[:, ls] = xr_b, xi_b
                y_ref[:, jt * U_TILE:(jt + 1) * U_TILE] = _dot(xr_b, wor_ref[jt]) + _dot(xi_b, woi_ref[jt])

    whole = lambda a: pl.BlockSpec(a.shape, lambda j, nd=a.ndim: (0,) * nd)
    st_spec = pl.BlockSpec((t_chunk, n_state), lambda j: (chunk_of(j), 0))
    v_spec = pl.BlockSpec((t_chunk, width), lambda j: (chunk_of(j), 0))
    first, last = _grid_ends((n_chunks,))
    return _call(
        body, name=name, grid=(n_chunks,),
        in_specs=[v_spec, whole(win_re), whole(win_im), whole(tabs), whole(wo_re), whole(wo_im)],
        out_specs=[st_spec, st_spec, v_spec],
        out_shape=[jax.ShapeDtypeStruct((seq, n_state), BF16)] * 2 + [jax.ShapeDtypeStruct((seq, width), F32)],
        scratch_shapes=[pltpu.VMEM((2, SUBLANES, n_state), F32), pltpu.VMEM((TILES_TOGETHER, t_chunk, ST_TILE), F32),
                        pltpu.VMEM((TILES_TOGETHER, t_chunk, ST_TILE), F32)],
        args=(v, win_re, win_im, tabs, wo_re, wo_im), ride=ride, first=first, last=last)


def _s5_reduce(x_re, x_im, a_re, a_im, u, dy, *, name, t_chunk=1024, ride=None):
    seq, n_state = x_re.shape
    width = u.shape[1]
    n_tiles = width // U_TILE
    t_chunk = _tile(seq, t_chunk)

    def body(xr_ref, xi_ref, ar_ref, ai_ref, u_ref, dy_ref, rr_ref, ri_ref, gbr_ref, gbi_ref, gcr_ref, gci_ref):
        xrb, xib, arb, aib = xr_ref[...], xi_ref[...], ar_ref[...], ai_ref[...]
        xr, xi, ar, ai = xrb.astype(F32), xib.astype(F32), arb.astype(F32), aib.astype(F32)
        ub, dyb = u_ref[...].astype(BF16), dy_ref[...].astype(BF16)
        parts = (jnp.sum(ar * xr + ai * xi, axis=0, keepdims=True), jnp.sum(ai * xr - ar * xi, axis=0, keepdims=True),
                 _dot(ub, arb, _TN), _dot(ub, aib, _TN), _dot(dyb, xrb, _TN), _dot(dyb, xib, _TN))
        first = pl.program_id(1) == 0
        for ref, val in zip((rr_ref, ri_ref, gbr_ref, gbi_ref, gcr_ref, gci_ref), parts):
            @pl.when(first)
            def _():
                ref[...] = val

            @pl.when(jnp.logical_not(first))
            def _():
                ref[...] += val

    st_spec = pl.BlockSpec((t_chunk, ST_TILE), lambda j, t: (t, j))
    u_spec = pl.BlockSpec((t_chunk, U_TILE), lambda j, t: (t, j))
    r_spec = pl.BlockSpec((1, ST_TILE), lambda j, t: (0, j))
    g_spec = pl.BlockSpec((None, U_TILE, ST_TILE), lambda j, t: (j, 0, 0))
    first, last = _grid_ends((n_tiles, seq // t_chunk))
    return _call(
        body, name=name, grid=(n_tiles, seq // t_chunk),
        in_specs=[st_spec] * 4 + [u_spec] * 2,
        out_specs=[r_spec, r_spec] + [g_spec] * 4,
        out_shape=[jax.ShapeDtypeStruct((1, n_state), F32)] * 2
        + [jax.ShapeDtypeStruct((n_tiles, U_TILE, ST_TILE), F32)] * 4,
        scratch_shapes=[], args=(x_re, x_im, a_re, a_im, u, dy), ride=ride, first=first, last=last)


def _block_diag_in(ms):
    m = jnp.stack(ms)
    n, g, c, p = m.shape
    m5 = m.reshape(n, g // GROUPS_PER_TILE, GROUPS_PER_TILE, c, p)
    eye = jnp.eye(GROUPS_PER_TILE, dtype=m.dtype)
    out = m5[:, :, :, :, None, :] * eye[None, None, :, None, :, None]
    return out.astype(BF16).reshape(n, g // GROUPS_PER_TILE, GROUPS_PER_TILE * c, GROUPS_PER_TILE * p)


def _block_diag_take(m, c, p):
    t = m.shape[0]
    m5 = m.reshape(t, GROUPS_PER_TILE, c, GROUPS_PER_TILE, p)
    idx = jnp.arange(GROUPS_PER_TILE)
    return m5[:, idx, :, idx, :].transpose(1, 0, 3, 2).reshape(t * GROUPS_PER_TILE, p, c)


def _scan_tables(pw_re, pw_im, reverse):
    row = jnp.arange(SUBLANES)[:, None]
    tabs = []
    for k in (1, 2, 4):
        keep = (row <= SUBLANES - 1 - k) if reverse else (row >= k)
        tabs += [jnp.where(keep, pw_re[k - 1][None, :], 0.0), jnp.where(keep, pw_im[k - 1][None, :], 0.0)]
    order = jnp.arange(SUBLANES)[::-1] if reverse else jnp.arange(SUBLANES)
    tabs += [pw_re[order], pw_im[order]]
    return jnp.stack(tabs)


def _partial_sums(slabs, from_sibling, names):
    x, y, c = _mesh_place()
    theirs = jnp.stack([_slab(px, py, c) for px, py in _chips(x, y)[1:]]).astype(jnp.int32)
    out = []
    for s, f, n in zip(slabs, from_sibling, names):
        rows, cols = s.shape[1:]
        tr = _tile(rows, 512)

        def body(idx_ref, a_ref, b_ref, o_ref):
            o_ref[...] = (a_ref[...] + b_ref[...]).astype(BF16)

        out.append(pl.pallas_call(
            body, name=f"reduce_add_{n}",
            grid_spec=pltpu.PrefetchScalarGridSpec(
                num_scalar_prefetch=1, grid=(3, rows // tr),
                in_specs=[pl.BlockSpec((None, tr, cols), lambda k, i, idx: (idx[k], i, 0)),
                          pl.BlockSpec((None, tr, cols), lambda k, i, idx: (k + 1, i, 0))],
                out_specs=pl.BlockSpec((None, tr, cols), lambda k, i, idx: (k, i, 0))),
            out_shape=jax.ShapeDtypeStruct((3, rows, cols), BF16), compiler_params=_params(),
        )(theirs, s, f))
    return out


def _local_step(x, target, p, shards):
    seq, d_model = x.shape
    a_width = p["g_out_attn"].shape[-1]
    s_width = p["g_out_ssm"].shape[-1]
    n_heads = a_width // HEAD_DIM
    n_hg = n_heads // HEADS_PER_GROUP
    n_groups = s_width // SSM_C
    n_sh, in_sh = N_DEV, shards["w_in"].shape[-1]
    f_sh = shards["w_ffn_gate"].shape[-1]
    w = {}
    slab3 = lambda g, n: g.reshape(N_DEV, -1, shards[n].shape[-1])
    t2, t1 = _tile(seq, 2048), _tile(seq, 1024)
    n2, n1 = seq // t2, seq // t1

    n_col = 2 * n_groups * SSM_P
    col = lambda a: a.reshape(n_col, 1)
    a_re_c, a_im_c = col(p["ssm_a_re"]), col(p["ssm_a_im"])
    dt_c = col(jnp.broadcast_to(jnp.exp(p["ssm_log_step"])[:, :, None], (2, n_groups, SSM_P)))
    b_re_c, b_im_c = p["ssm_b_re"].reshape(n_col, SSM_C), p["ssm_b_im"].reshape(n_col, SSM_C)
    (pw_re, pw_im, cf_re, cf_im, bb_re, bb_im), got = _s5_discretize(a_re_c, a_im_c, dt_c, b_re_c, b_im_c,
                                                                     ride=_gather_first([shards["w_in"]]))

    twice = lambda f: (lambda *a: (f(*a),) * 2)
    (h1, h1_t), (w["w_in"],) = _rowwise(twice(x_norm), [x], [p["g_mix"]], [(d_model, BF16)],
                                        flipped=[(d_model, BF16, 1)], name="rms_mix", ride=_gather_second(got))
    z = _mm(h1, w["w_in"], name="in_proj", grid=(n2, n_sh),
            a_spec=pl.BlockSpec((t2, d_model), lambda i, j: (i, 0)),
            b_spec=pl.BlockSpec((None, d_model, in_sh), lambda i, j: (j, 0, 0)),
            o_spec=pl.BlockSpec((t2, in_sh), lambda i, j: (i, j)), o_shape=(seq, n_sh * in_sh), dims="nn")
    qg4 = jnp.tile(p["q_gain"], (1, HEADS_PER_GROUP))
    kg4 = jnp.tile(p["k_gain"], (1, HEADS_PER_GROUP))
    btab = _bias_table(p["rpb"])
    ya, got_a = _attn_fwd(z, qg4, kg4, btab, ride=_gather_first([shards["w_ffn_gate"], shards["w_ffn_up"]]))
    u = z[:, 3 * a_width:]
    n_state = n_groups * SSM_P
    pw_re = pw_re.reshape(2, n_state, SUBLANES).transpose(0, 2, 1)
    pw_im = pw_im.reshape(2, n_state, SUBLANES).transpose(0, 2, 1)
    bb_re4, bb_im4 = bb_re.reshape(2, n_groups, SSM_P, SSM_C), bb_im.reshape(2, n_groups, SSM_P, SSM_C)
    c_re, c_im = p["ssm_c_re"], p["ssm_c_im"]
    t21 = lambda a: a.transpose(0, 2, 1)
    maps_in = _block_diag_in([m for d in range(2) for m in (t21(bb_re4[d]), t21(bb_im4[d]), c_re[d], -c_im[d])])
    maps_out = _block_diag_in([m for d in range(2) for m in (t21(c_re[d]), -t21(c_im[d]), bb_re4[d], bb_im4[d])])
    fwd, bwd_in = [], []
    got_b = None
    for d in range(2):
        rev = d == 1
        tabs = _scan_tables(pw_re[d], pw_im[d], rev)
        if d == 0:
            ride = _gather_first([shards["w_glu"], shards["w_out"]])
        else:
            ride = _gather_second(got_a + got_b) + _gather_first([shards["w_ffn_down"]])
        (xs_re, xs_im, y_d), got = _s5_scan(u, maps_in[4 * d], maps_in[4 * d + 1], tabs, maps_out[4 * d],
                                            maps_out[4 * d + 1], reverse=rev, name=f"s5_fwd_{d}", ride=ride)
        if d == 0:
            got_b = got
        fwd.append((xs_re, xs_im, y_d))
        bwd_in.append((maps_in[4 * d + 2], maps_in[4 * d + 3], _scan_tables(pw_re[d], -pw_im[d], not rev),
                       maps_out[4 * d + 2], maps_out[4 * d + 3]))
    w["w_gate"], w["w_up"], w_glu_full, w_out_full, w_down_first = got
    w["w_glu"] = w_glu_full.reshape(-1, s_width)
    w["w_out"] = w_out_full.reshape(-1, d_model)

    ypre, yg, yg_t = _rowwise(s5_mid, [fwd[0][2], fwd[1][2], u], [p["ssm_d"]], [(s_width, F32), (s_width, F32)],
                              flipped=[(s_width, BF16, 1)], name="s5_skip_gelu")
    t_glu = _mm_plain(yg, w["w_glu"], "nn", name="glu_proj", tn=s_width)
    y_cat, y_cat_t = _rowwise(twice(mix_out_fwd), [ya, yg, t_glu], [p["b_glu"], p["g_out_attn"], p["g_out_ssm"]],
                              [(a_width + s_width, BF16)], flipped=[(a_width + s_width, BF16, 1)], name="mix_out")
    x1, (w["w_down"],) = _mm_plain(y_cat, w["w_out"], "nn", name="out_proj", res=x, tn=2048,
                                   ride=_gather_second([w_down_first]))

    h2, h2_t = _rowwise(twice(x_norm), [x1], [p["g_ffn"]], [(d_model, BF16)], flipped=[(d_model, BF16, 1)],
                        name="rms_ffn")
    ffn_up = functools.partial(
        _mm, grid=(n2, n_sh), a_spec=pl.BlockSpec((t2, d_model), lambda i, j: (i, 0)),
        b_spec=pl.BlockSpec((None, d_model, f_sh), lambda i, j: (j, 0, 0)),
        o_spec=pl.BlockSpec((None, t2, f_sh), lambda i, j: (j, i, 0)), o_shape=(n_sh, seq, f_sh), dims="nn",
        out_dtype=BF16)
    gate = ffn_up(h2, w["w_gate"], name="ffn_gate")
    up = ffn_up(h2, w["w_up"], name="ffn_up")
    flat = lambda a: a.reshape(n_sh * seq, f_sh)
    act_t = _rowwise(swiglu_fwd, [flat(gate), flat(up)], [], [], flipped=[(f_sh, BF16, n_sh)], name="swiglu",
                     tm=1024)[0].reshape(n_sh, f_sh, seq)
    ffn_out = _mm(act_t, w["w_down"], name="ffn_down", grid=(n1, n_sh // 2), groups=2,
                  a_spec=pl.BlockSpec((2, f_sh, t1), lambda i, j: (j, 0, i)),
                  b_spec=pl.BlockSpec((2, f_sh, d_model), lambda i, j: (j, 0, 0)),
                  o_spec=pl.BlockSpec((t1, d_model), lambda i, j: (i, 0)), o_shape=(seq, d_model), dims="tn",
                  k_axis=1)

    dx2, dx2_b, sq = _rowwise(functools.partial(loss_head, inv_d=1.0 / d_model), [ffn_out, x1, target], [],
                              [(d_model, F32), (d_model, BF16)], [d_model], name="loss_head")
    loss = 0.5 * jnp.sum(sq) / d_model

    d_act = _mm(dx2_b, w["w_down"], name="ffn_down_dx", grid=(n2, n_sh),
                a_spec=pl.BlockSpec((t2, d_model), lambda i, j: (i, 0)),
                b_spec=pl.BlockSpec((None, f_sh, d_model), lambda i, j: (j, 0, 0)),
                o_spec=pl.BlockSpec((None, t2, f_sh), lambda i, j: (j, i, 0)), o_shape=(n_sh, seq, f_sh), dims="nt",
                out_dtype=BF16)
    g_w_down = _mm(act_t, dx2_b, name="ffn_down_dw", grid=(n_sh, n2),
                   a_spec=pl.BlockSpec((None, f_sh, t2), lambda j, k: (j, 0, k)),
                   b_spec=pl.BlockSpec((t2, d_model), lambda j, k: (k, 0)),
                   o_spec=pl.BlockSpec((None, f_sh, d_model), lambda j, k: (j, 0, 0)),
                   o_shape=(n_sh, f_sh, d_model), dims="nn", k_axis=1)
    d_gate, d_up = _rowwise(swiglu_bwd, [flat(d_act), flat(gate), flat(up)], [], [(f_sh, BF16), (f_sh, BF16)],
                            name="swiglu_bwd", tm=1024)
    d_gate, d_up = d_gate.reshape(n_sh, seq, f_sh), d_up.reshape(n_sh, seq, f_sh)
    d_h2 = _mm(d_gate, w["w_gate"], second=(d_up, w["w_up"]), name="ffn_up_gate_dx", grid=(n1, n_sh),
               a_spec=pl.BlockSpec((None, t1, f_sh), lambda i, j: (j, i, 0)),
               b_spec=pl.BlockSpec((None, d_model, f_sh), lambda i, j: (j, 0, 0)),
               o_spec=pl.BlockSpec((t1, d_model), lambda i, j: (i, 0)), o_shape=(seq, d_model), dims="nt", k_axis=1)
    ffn_dw = functools.partial(
        _mm, grid=(n_sh, n2), a_spec=pl.BlockSpec((d_model, t2), lambda j, k: (0, k)),
        b_spec=pl.BlockSpec((None, t2, f_sh), lambda j, k: (j, k, 0)),
        o_spec=pl.BlockSpec((None, d_model, f_sh), lambda j, k: (j, 0, 0)), o_shape=(n_sh, d_model, f_sh), dims="nn",
        k_axis=1)
    g_w_gate = ffn_dw(h2_t, d_gate, name="ffn_gate_dw")
    g_w_up = ffn_dw(h2_t, d_up, name="ffn_up_dw")
    dx1, g_g_ffn = _rowwise(residual_rms_bwd, [dx2, d_h2, x1], [p["g_ffn"]], [(d_model, F32)], [d_model],
                            name="rms_ffn_bwd")

    d_ycat = _mm_plain(dx1, w["w_out"], "nt", name="out_proj_dx", tn=2048)
    mix_w = a_width + s_width
    tm_o = _tile(mix_w, 1024)
    g_w_out = _mm(y_cat_t, dx1, name="out_proj_dw", grid=(mix_w // tm_o, n1),
                  a_spec=pl.BlockSpec((tm_o, t1), lambda i, k: (i, k)),
                  b_spec=pl.BlockSpec((t1, d_model), lambda i, k: (k, 0)),
                  o_spec=pl.BlockSpec((tm_o, d_model), lambda i, k: (i, 0)), o_shape=(mix_w, d_model), dims="nn",
                  k_axis=1)
    (d_ya, d_yg_direct, d_t, g_goa, g_gos, g_b_glu) = _rowwise(
        functools.partial(mix_out_bwd, a_width=a_width), [d_ycat, ya, yg, t_glu],
        [p["b_glu"], p["g_out_attn"], p["g_out_ssm"]],
        [(a_width, F32), (s_width, F32), (s_width, BF16)], [a_width, s_width, s_width], name="mix_out_bwd")
    d_yg = _mm_plain(d_t, w["w_glu"], "nt", name="glu_proj_dx", res=d_yg_direct, tn=s_width)
    g_w_glu = _mm(yg_t, d_t, name="glu_proj_dw", grid=(1, n1),
                  a_spec=pl.BlockSpec((s_width, t1), lambda i, k: (0, k)),
                  b_spec=pl.BlockSpec((t1, s_width), lambda i, k: (k, 0)),
                  o_spec=pl.BlockSpec((s_width, s_width), lambda i, k: (0, 0)), o_shape=(s_width, s_width),
                  dims="nn", k_axis=1)
    d_ypre, du_skip, g_ssm_d = _rowwise(gelu_skip_bwd, [d_yg, ypre, u], [p["ssm_d"]],
                                        [(s_width, F32), (s_width, F32)], [s_width], name="s5_skip_gelu_bwd")

    names = ("w_ffn_gate", "w_ffn_up", "w_ffn_down", "w_glu", "w_out")
    slabs = {n: slab3(g, n) for n, g in zip(names, (g_w_gate, g_w_up, g_w_down, g_w_glu, g_w_out))}
    sib, part, chips = {}, {}, {}
    du_dirs, adj, r_parts, gb_parts, gc_parts = [], [], [], [], []
    for d in range(2):
        win_re, win_im, tabs, wo_re, wo_im = bwd_in[d]
        ride = _reduce_sibling([slabs[n] for n in names]) if d == 0 else _reduce_chips([part["w_ffn_down"]])
        (as_re, as_im, du_d), got = _s5_scan(d_ypre, win_re, win_im, tabs, wo_re, wo_im, reverse=(d == 0),
                                             name=f"s5_bwd_{d}", ride=ride)
        du_dirs.append(du_d)
        adj.append((as_re, as_im))
        if d == 0:
            sib = dict(zip(names, got))
            part = dict(zip(names, _partial_sums([slabs[n] for n in names], got, names)))
        else:
            chips["w_ffn_down"] = got[0]
    for d in range(2):
        (r_re, r_im, gbt_re, gbt_im, gct_re, gct_im), got = _s5_reduce(
            fwd[d][0], fwd[d][1], adj[d][0], adj[d][1], u, d_ypre, name=f"s5_reduce_{d}",
            ride=_reduce_chips([part["w_glu"], part["w_out"]]) if d == 0 else None)
        if d == 0:
            chips["w_glu"], chips["w_out"] = got
        r_parts.append((r_re.reshape(n_state, 1), r_im.reshape(n_state, 1)))
        gb_parts.append((_block_diag_take(gbt_re, SSM_C, SSM_P), _block_diag_take(gbt_im, SSM_C, SSM_P)))
        gc_parts.append((_block_diag_take(gct_re, SSM_C, SSM_P), _block_diag_take(gct_im, SSM_C, SSM_P)))
    cat = lambda i, parts: jnp.concatenate([parts[0][i], parts[1][i]], axis=0)
    gbb_re, gbb_im = cat(0, gb_parts).reshape(n_col, SSM_C), cat(1, gb_parts).reshape(n_col, SSM_C)
    g_a_re, g_a_im, g_ls, g_b_re, g_b_im = _s5_param_grads(
        a_re_c, a_im_c, dt_c, b_re_c, b_im_c, pw_re[:, 0].reshape(n_col, 1), pw_im[:, 0].reshape(n_col, 1),
        cf_re, cf_im, bb_re, bb_im, cat(0, r_parts), cat(1, r_parts), gbb_re, gbb_im)
    g_c_re = cat(0, gc_parts).reshape(2, n_groups, SSM_P, SSM_C).transpose(0, 1, 3, 2)
    g_c_im = -cat(1, gc_parts).reshape(2, n_groups, SSM_P, SSM_C).transpose(0, 1, 3, 2)

    (d_q, d_k, d_v, d_btab, g_qg, g_kg), got = _attn_bwd(
        z, d_ya, qg4, kg4, btab, ride=_reduce_chips([part["w_ffn_gate"], part["w_ffn_up"]]))
    chips["w_ffn_gate"], chips["w_ffn_up"] = got
    d_u = _rowwise(lambda a, b, c: a + b + c, [du_dirs[0], du_dirs[1], du_skip], [], [(s_width, BF16)],
                   name="s5_du_sum")[0]
    d_z = jnp.concatenate([d_q, d_k, d_v, d_u], axis=1)
    fold_heads = lambda g: g.reshape(n_heads, HEAD_DIM).sum(axis=0, keepdims=True)
    small = {
        "q_gain": fold_heads(g_qg), "k_gain": fold_heads(g_kg), "rpb": _bias_grad(d_btab, n_heads),
        "ssm_a_re": g_a_re.reshape(2, n_groups, SSM_P), "ssm_a_im": g_a_im.reshape(2, n_groups, SSM_P),
        "ssm_b_re": g_b_re.reshape(2, n_groups, SSM_P, SSM_C), "ssm_b_im": g_b_im.reshape(2, n_groups, SSM_P, SSM_C),
        "ssm_c_re": g_c_re, "ssm_c_im": g_c_im,
        "ssm_log_step": g_ls.reshape(2, n_groups, SSM_P).sum(axis=-1),
        "ssm_d": g_ssm_d, "b_glu": g_b_glu, "g_out_attn": g_goa, "g_out_ssm": g_gos, "g_ffn": g_g_ffn,
    }
    g_w_in, got = _mm(h1_t, d_z, name="in_proj_dw", grid=(n_sh, n2),
                      a_spec=pl.BlockSpec((d_model, t2), lambda j, k: (0, k)),
                      b_spec=pl.BlockSpec((t2, in_sh), lambda j, k: (k, j)),
                      o_spec=pl.BlockSpec((None, d_model, in_sh), lambda j, k: (j, 0, 0)),
                      o_shape=(n_sh, d_model, in_sh), dims="nn", k_axis=1,
                      ride=_gather_first([_as_rows(small[n]) for n in SMALL_LATE]))
    d_h1, got = _mm(d_z, w["w_in"], name="in_proj_dx", grid=(n1, n_sh // 2), groups=2,
                    a_spec=pl.BlockSpec((t1, 2 * in_sh), lambda i, j: (i, j)),
                    b_spec=pl.BlockSpec((2, d_model, in_sh), lambda i, j: (j, 0, 0)),
                    o_spec=pl.BlockSpec((t1, d_model), lambda i, j: (i, 0)), o_shape=(seq, d_model), dims="nt",
                    k_axis=1, ride=_gather_second(got) + _reduce_sibling([g_w_in]))
    small_gathered, in_sibling = dict(zip(SMALL_LATE, got[:-1])), got[-1]
    in_part = _partial_sums([g_w_in], [in_sibling], ("w_in",))
    (grad_x, g_g_mix), (in_chips,) = _rowwise(residual_rms_bwd, [dx1, d_h1, x], [p["g_mix"]], [(d_model, F32)],
                                              [d_model], name="rms_mix_bwd", ride=_reduce_chips(in_part))
    reduced = {n: (slabs[n], sib[n], chips[n]) for n in slabs}
    reduced["w_in"] = (g_w_in, in_sibling, in_chips)
    return loss, grad_x, small_gathered, g_g_mix, reduced


def x_norm(xv, g):
    return xv * _rstd(xv) * g


def s5_mid(y0, y1, uv, d_skip):
    ypre = y0 + y1 + d_skip * uv
    yg = _gelu(ypre)
    return ypre, yg, yg


def mix_out_fwd(ya, yg, t, b_glu, g_oa, g_os):
    ys = yg * _sigmoid(t + b_glu)
    return jnp.concatenate([ya * _rstd(ya) * g_oa, ys * _rstd(ys) * g_os], axis=1)


def mix_out_bwd(d_y, ya, yg, t, b_glu, g_oa, g_os, *, a_width):
    sg = _sigmoid(t + b_glu)
    ys = yg * sg
    d_ya, c_goa = _rms_bwd(d_y[:, :a_width], ya, g_oa)
    d_ys, c_gos = _rms_bwd(d_y[:, a_width:], ys, g_os)
    d_t = d_ys * yg * sg * (1.0 - sg)
    return d_ya, d_ys * sg, d_t, c_goa, c_gos, d_t


def gelu_skip_bwd(d_yg, ypre, uv, d_skip):
    d_ypre = d_yg * _gelu_grad(ypre)
    return d_ypre, d_ypre * d_skip, d_ypre * uv


def swiglu_fwd(gv, uv):
    gv, uv = gv.astype(F32), uv.astype(F32)
    return gv * _sigmoid(gv) * uv


def swiglu_bwd(d_act, gv, uv):
    d_act, gv, uv = d_act.astype(F32), gv.astype(F32), uv.astype(F32)
    sg = _sigmoid(gv)
    return d_act * uv * (sg * (1.0 + gv * (1.0 - sg))), d_act * gv * sg


def loss_head(ffn_out, x1, target, *, inv_d):
    diff = ffn_out + x1 - target
    return diff * inv_d, diff * inv_d, diff * diff


def residual_rms_bwd(d_res, d_h, xv, g):
    dx, c_g = _rms_bwd(d_h, xv, g)
    return d_res + dx, c_g


_ANY = pl.BlockSpec(memory_space=pl.ANY)


def _mesh_place():
    return lax.axis_index("x"), lax.axis_index("y"), lax.axis_index("c")


def _chips(x, y):
    return [(x, y), (1 - x, y), (x, 1 - y), (1 - x, 1 - y)]


def _slab(px, py, pc):
    return 4 * px + 2 * py + pc


def _all_gather(arrs, *, name):
    n = len(arrs)

    def body(*refs):
        in_refs, out_refs = refs[:n], refs[n:2 * n]
        send_sems, recv_sems, local_sems = refs[2 * n:]
        x, y, c = _mesh_place()
        me, sibling = (x, y, c), (x, y, 1 - c)
        others = _chips(x, y)[1:]

        def copy(w, k, block, to, src=None):
            dst = out_refs[w].at[_slab(*block)]
            return pltpu.make_async_remote_copy(
                src_ref=dst if src is None else src, dst_ref=dst, send_sem=send_sems.at[7 * w + k],
                recv_sem=recv_sems.at[7 * w + k], device_id=to, device_id_type=MESH)

        mine = [pltpu.make_async_copy(in_refs[w], out_refs[w].at[_slab(*me)], local_sems.at[w]) for w in range(n)]
        first = []
        for w in range(n):
            mine[w].start()
            first.append(copy(w, 0, me, sibling, src=in_refs[w]))
            first += [copy(w, 1 + j, me, (*chip, c), src=in_refs[w]) for j, chip in enumerate(others)]
        for cp in first:
            cp.start()
        passed = []
        for j, chip in enumerate(others):
            for w in range(n):
                copy(w, 1 + j, (*chip, c), me).wait_recv()
                fwd = copy(w, 4 + j, (*chip, c), sibling)
                fwd.start()
                passed.append(fwd)
        for w in range(n):
            copy(w, 0, sibling, me).wait_recv()
        for j, chip in enumerate(others):
            for w in range(n):
                copy(w, 4 + j, (*chip, 1 - c), me).wait_recv()
        for cp in first + passed:
            cp.wait_send()
        for cp in mine:
            cp.wait()

    return pl.pallas_call(
        body, name=name, in_specs=[_ANY] * n, out_specs=[_ANY] * n,
        out_shape=[jax.ShapeDtypeStruct((N_DEV,) + a.shape, a.dtype) for a in arrs],
        scratch_shapes=[pltpu.SemaphoreType.DMA((7 * n,)), pltpu.SemaphoreType.DMA((7 * n,)),
                        pltpu.SemaphoreType.DMA((n,))],
        compiler_params=pltpu.CompilerParams(has_side_effects=True),
    )(*arrs)


def _adamw(w, m, v, parts, *, name, slab, tr=256):
    rows, cols = w.shape
    tr = _tile(rows, tr)
    n_p = len(parts)

    def body(slab_ref, *refs):
        w_ref, m_ref, v_ref = refs[:3]
        p_refs = refs[3:3 + n_p]
        g_ref, d_ref, nm_ref, nv_ref = refs[3 + n_p:]
        g = None
        for (_, lead), r in zip(parts, p_refs):
            for piece in ([r[...]] if lead is None else [r[i] for i in range(lead)]):
                g = piece.astype(F32) if g is None else g + piece.astype(F32)
        new_m = ADAM_B1 * m_ref[...] + (1.0 - ADAM_B1) * g
        new_v = ADAM_B2 * v_ref[...] + (1.0 - ADAM_B2) * (g * g)
        m_hat = new_m / (1.0 - ADAM_B1 ** ADAM_STEP)
        v_hat = new_v / (1.0 - ADAM_B2 ** ADAM_STEP)
        g_ref[...] = g
        d_ref[...] = -ADAM_LR * (m_hat / (jnp.sqrt(v_hat) + ADAM_EPS) + ADAM_WD * w_ref[...])
        nm_ref[...] = new_m
        nv_ref[...] = new_v

    tile = pl.BlockSpec((tr, cols), lambda i, s: (i, 0))
    p_specs = [pl.BlockSpec((None, tr, cols), lambda i, s: (s[0], i, 0)) if lead is None
               else pl.BlockSpec((lead, tr, cols), lambda i, s: (0, i, 0)) for _, lead in parts]
    return pl.pallas_call(
        body, name=name,
        grid_spec=pltpu.PrefetchScalarGridSpec(num_scalar_prefetch=1, grid=(rows // tr,),
                                               in_specs=[tile] * 3 + p_specs, out_specs=[tile] * 4),
        out_shape=[jax.ShapeDtypeStruct((rows, cols), F32)] * 4, compiler_params=_params(),
    )(jnp.reshape(slab, (1,)).astype(jnp.int32), w, m, v, *[a for a, _ in parts])


_DENSE_MIN = 256 * 128


def _as_rows(a):
    if a.shape[-1] < 128 and a.size >= _DENSE_MIN and a.size % 128 == 0:
        return a.reshape(-1, 128)
    return a.reshape(-1, a.shape[-1])


BIG = ("w_in", "w_glu", "w_out", "w_ffn_gate", "w_ffn_up", "w_ffn_down")
WEIGHTS = ("g_mix", "w_in", "q_gain", "k_gain", "rpb", "ssm_a_re", "ssm_a_im", "ssm_b_re", "ssm_b_im", "ssm_c_re",
           "ssm_c_im", "ssm_log_step", "ssm_d", "w_glu", "b_glu", "g_out_attn", "g_out_ssm", "w_out", "g_ffn",
           "w_ffn_gate", "w_ffn_up", "w_ffn_down")
SMALL = tuple(n for n in WEIGHTS if n not in BIG)
SMALL_LATE = tuple(n for n in SMALL if n != "g_mix")
VECTORS = ("g_mix", "q_gain", "k_gain", "ssm_d", "b_glu", "g_out_attn", "g_out_ssm", "g_ffn")


def kernel(x, g_mix, w_in, q_gain, k_gain, rpb, ssm_a_re, ssm_a_im, ssm_b_re, ssm_b_im, ssm_c_re, ssm_c_im, ssm_log_step, ssm_d, w_glu, b_glu, g_out_attn, g_out_ssm, w_out, g_ffn, w_ffn_gate, w_ffn_up, w_ffn_down, loss_target, m_g_mix, m_w_in, m_q_gain, m_k_gain, m_rpb, m_ssm_a_re, m_ssm_a_im, m_ssm_b_re, m_ssm_b_im, m_ssm_c_re, m_ssm_c_im, m_ssm_log_step, m_ssm_d, m_w_glu, m_b_glu, m_g_out_attn, m_g_out_ssm, m_w_out, m_g_ffn, m_w_ffn_gate, m_w_ffn_up, m_w_ffn_down, v_g_mix, v_w_in, v_q_gain, v_k_gain, v_rpb, v_ssm_a_re, v_ssm_a_im, v_ssm_b_re, v_ssm_b_im, v_ssm_c_re, v_ssm_c_im, v_ssm_log_step, v_ssm_d, v_w_glu, v_b_glu, v_g_out_attn, v_g_out_ssm, v_w_out, v_g_ffn, v_w_ffn_gate, v_w_ffn_up, v_w_ffn_down):
    wts = dict(g_mix=g_mix, w_in=w_in, q_gain=q_gain, k_gain=k_gain, rpb=rpb, ssm_a_re=ssm_a_re, ssm_a_im=ssm_a_im,
               ssm_b_re=ssm_b_re, ssm_b_im=ssm_b_im, ssm_c_re=ssm_c_re, ssm_c_im=ssm_c_im, ssm_log_step=ssm_log_step,
               ssm_d=ssm_d, w_glu=w_glu, b_glu=b_glu, g_out_attn=g_out_attn, g_out_ssm=g_out_ssm, w_out=w_out,
               g_ffn=g_ffn, w_ffn_gate=w_ffn_gate, w_ffn_up=w_ffn_up, w_ffn_down=w_ffn_down)
    mom = dict(g_mix=m_g_mix, w_in=m_w_in, q_gain=m_q_gain, k_gain=m_k_gain, rpb=m_rpb, ssm_a_re=m_ssm_a_re,
               ssm_a_im=m_ssm_a_im, ssm_b_re=m_ssm_b_re, ssm_b_im=m_ssm_b_im, ssm_c_re=m_ssm_c_re,
               ssm_c_im=m_ssm_c_im, ssm_log_step=m_ssm_log_step, ssm_d=m_ssm_d, w_glu=m_w_glu, b_glu=m_b_glu,
               g_out_attn=m_g_out_attn, g_out_ssm=m_g_out_ssm, w_out=m_w_out, g_ffn=m_g_ffn,
               w_ffn_gate=m_w_ffn_gate, w_ffn_up=m_w_ffn_up, w_ffn_down=m_w_ffn_down)
    var = dict(g_mix=v_g_mix, w_in=v_w_in, q_gain=v_q_gain, k_gain=v_k_gain, rpb=v_rpb, ssm_a_re=v_ssm_a_re,
               ssm_a_im=v_ssm_a_im, ssm_b_re=v_ssm_b_re, ssm_b_im=v_ssm_b_im, ssm_c_re=v_ssm_c_re,
               ssm_c_im=v_ssm_c_im, ssm_log_step=v_ssm_log_step, ssm_d=v_ssm_d, w_glu=v_w_glu, b_glu=v_b_glu,
               g_out_attn=v_g_out_attn, g_out_ssm=v_g_out_ssm, w_out=v_w_out, g_ffn=v_g_ffn,
               w_ffn_gate=v_w_ffn_gate, w_ffn_up=v_w_ffn_up, w_ffn_down=v_w_ffn_down)
    ix, iy, ic = _mesh_place()
    me = _slab(ix, iy, ic)

    shard = {n: wts[n][0] for n in BIG}
    shard_b = {n: shard[n].astype(BF16) for n in BIG}
    p = {n: (wts[n][0].reshape(1, -1) if n in VECTORS else wts[n][0]) for n in SMALL}

    loss, grad_x, small_gathered, g_g_mix, reduced = _local_step(x[0], loss_target[0], p, shard_b)
    loss = lax.psum(loss, ("x", "y", "c"))
    out = {}
    for n in BIG:
        slabs, from_sibling, from_chips = reduced[n]
        rows, cols = slabs.shape[1:]
        res = _adamw(shard[n].reshape(rows, cols), mom[n][0].reshape(rows, cols), var[n][0].reshape(rows, cols),
                     [(slabs, None), (from_sibling, 1), (from_chips, 3)], name=f"adamw_{n}", slab=me)
        out[n] = [r.reshape(wts[n].shape) for r in res]

    small_gathered["g_mix"] = _all_gather([g_g_mix], name="gather_g_mix")[0]
    for n in SMALL:
        res = _adamw(_as_rows(wts[n]), _as_rows(mom[n]), _as_rows(var[n]), [(small_gathered[n], N_DEV)],
                     name=f"adamw_{n}", slab=me, tr=1024)
        out[n] = [r.reshape(wts[n].shape) for r in res]

    return (loss, grad_x[None], *[out[n][0] for n in WEIGHTS], *[out[n][1] for n in WEIGHTS],
            *[out[n][2] for n in WEIGHTS], *[out[n][3] for n in WEIGHTS])
```

```python
import functools
import math

import jax
import jax.numpy as jnp
from jax import lax
from jax.experimental import pallas as pl
from jax.experimental.pallas import tpu as pltpu

F32 = jnp.float32
BF16 = jnp.bfloat16

N_DEV = 8
GRID_W = 64
WIN_H = 8
WIN_W = 16
HEAD_DIM = 64
HEADS_PER_GROUP = 4
GROUP_LANES = HEADS_PER_GROUP * HEAD_DIM
SSM_C = 16
SSM_P = 64
GROUPS_PER_TILE = 8
U_TILE = GROUPS_PER_TILE * SSM_C
ST_TILE = GROUPS_PER_TILE * SSM_P
TILES_TOGETHER = 4
SUBLANES = 8
RMS_EPS = 1e-6
NEG_INF = -1e30
A_RE_MAX = -1e-4
ADAM_LR, ADAM_B1, ADAM_B2, ADAM_EPS, ADAM_WD, ADAM_STEP = 0.001, 0.9, 0.999, 1e-08, 0.01, 10
VMEM_LIMIT_V7X = 56 * 1024 * 1024
MESH = pl.DeviceIdType.MESH

_NN = (((1,), (0,)), ((), ()))
_NT = (((1,), (1,)), ((), ()))
_TN = (((0,), (0,)), ((), ()))
_DIMS = {"nn": _NN, "nt": _NT, "tn": _TN}


def _params(**kw):
    return pltpu.CompilerParams(vmem_limit_bytes=VMEM_LIMIT_V7X, **kw)


def _dot(a, b, dims=_NN):
    return lax.dot_general(a, b, dims, preferred_element_type=F32)


def _mm(a, b, *, name, grid, a_spec, b_spec, o_spec, o_shape, dims, k_axis=None, res=None, out_dtype=F32,
        exact=False, second=None, ride=None, groups=1):
    dn = _DIMS[dims]
    nk = 1 if k_axis is None else grid[k_axis]
    acc_shape = tuple(d for d in o_spec.block_shape if d is not None)
    n_in = 2 + (2 if second is not None else 0)

    def body(*refs):
        a_ref, b_ref = refs[:2]
        r_ref = refs[n_in] if res is not None else None
        o_ref, acc = refs[-2:]
        def product(x_ref, y_ref):
            if groups == 1:
                return _dot(x_ref[...].astype(BF16), y_ref[...].astype(BF16), dn)
            total, width = None, x_ref.shape[-1] // groups
            for s in range(groups):
                x = x_ref[s] if len(x_ref.shape) == 3 else x_ref[:, s * width:(s + 1) * width]
                t = _dot(x.astype(BF16), y_ref[s].astype(BF16), dn)
                total = t if total is None else total + t
            return total

        if exact:
            p = lax.dot_general(a_ref[...], b_ref[...], dn, precision=lax.Precision.HIGHEST,
                                preferred_element_type=F32)
        else:
            p = product(a_ref, b_ref)
        if second is not None:
            p = p + product(refs[2], refs[3])

        def finish(v):
            if r_ref is not None:
                v = v + r_ref[...].astype(F32)
            o_ref[...] = v.astype(out_dtype)

        if nk == 1:
            finish(p)
        else:
            k = pl.program_id(k_axis)

            @pl.when(k == 0)
            def _():
                acc[...] = p

            @pl.when(k > 0)
            def _():
                acc[...] += p

            @pl.when(k == nk - 1)
            def _():
                finish(acc[...])

    ins = [a, b] + (list(second) if second is not None else []) + ([res] if res is not None else [])
    in_specs = [a_spec, b_spec] * (n_in // 2) + ([o_spec] if res is not None else [])
    first, last = _grid_ends(grid)
    (out,), rode = _call(
        body, name=name, grid=grid, in_specs=in_specs, out_specs=[o_spec],
        out_shape=[jax.ShapeDtypeStruct(o_shape, out_dtype)],
        scratch_shapes=[pltpu.VMEM(acc_shape if nk > 1 else (SUBLANES, 128), F32)],
        args=ins, ride=ride, first=first, last=last)
    return out if ride is None else (out, rode)


def _tile(n, want):
    if n <= want:
        return n
    t = want
    while n % t:
        t //= 2
    return t


def _mm_plain(a, b, dims, *, name, res=None, out_dtype=F32, tm=512, tn=512, tk=512, exact=False, ride=None):
    if dims == "nn":
        (m, k), n = a.shape, b.shape[1]
    elif dims == "nt":
        (m, k), n = a.shape, b.shape[0]
    else:
        (k, m), n = a.shape, b.shape[1]
    tm, tn = _tile(m, tm), _tile(n, tn)
    if dims == "tn":
        tk = _tile(k, tk)
        grid = (m // tm, n // tn, k // tk)
        a_spec = pl.BlockSpec((tk, tm), lambda i, j, kk: (kk, i))
        b_spec = pl.BlockSpec((tk, tn), lambda i, j, kk: (kk, j))
        o_spec = pl.BlockSpec((tm, tn), lambda i, j, kk: (i, j))
        return _mm(a, b, name=name, grid=grid, a_spec=a_spec, b_spec=b_spec, o_spec=o_spec, o_shape=(m, n),
                   dims=dims, k_axis=2, res=res, out_dtype=out_dtype)
    grid = (n // tn, m // tm)
    a_spec = pl.BlockSpec((tm, k), lambda j, i: (i, 0))
    if dims == "nn":
        b_spec = pl.BlockSpec((k, tn), lambda j, i: (0, j))
    else:
        b_spec = pl.BlockSpec((tn, k), lambda j, i: (j, 0))
    o_spec = pl.BlockSpec((tm, tn), lambda j, i: (i, j))
    return _mm(a, b, name=name, grid=grid, a_spec=a_spec, b_spec=b_spec, o_spec=o_spec, o_shape=(m, n), dims=dims,
               res=res, out_dtype=out_dtype, exact=exact, ride=ride)


def _rowwise(fn, tiled, bcast, outs, accs=(), *, name, tm=256, flipped=(), ride=None):
    m = tiled[0].shape[0]
    tm = _tile(m, tm)
    n_t, n_b, n_o, n_f = len(tiled), len(bcast), len(outs), len(flipped)

    def body(*refs):
        ins = [r[...] for r in refs[: n_t + n_b]]
        o_refs = refs[n_t + n_b: n_t + n_b + n_o]
        f_refs = refs[n_t + n_b + n_o: n_t + n_b + n_o + n_f]
        a_refs = refs[n_t + n_b + n_o + n_f:]
        res = fn(*ins)
        if not isinstance(res, (tuple, list)):
            res = (res,)
        for r, v in zip(o_refs, res[:n_o]):
            r[...] = v.astype(r.dtype)
        for r, v in zip(f_refs, res[n_o:n_o + n_f]):
            r[...] = v.astype(F32).T.astype(r.dtype)
        first = pl.program_id(0) == 0
        for r, v in zip(a_refs, res[n_o + n_f:]):
            s = jnp.sum(v, axis=0, keepdims=True)

            @pl.when(first)
            def _():
                r[...] = s

            @pl.when(jnp.logical_not(first))
            def _():
                r[...] += s

    in_specs = [pl.BlockSpec((tm, t.shape[1]), lambda i: (i, 0)) for t in tiled]
    in_specs += [pl.BlockSpec(b.shape, lambda i, nd=b.ndim: (0,) * nd) for b in bcast]
    out_specs = [pl.BlockSpec((tm, n), lambda i: (i, 0)) for n, _ in outs]
    out_specs += [pl.BlockSpec((n, tm), lambda i, per=m // tm // g: (i // per, i % per)) for n, _, g in flipped]
    out_specs += [pl.BlockSpec((1, n), lambda i: (0, 0)) for n in accs]
    out_shape = [jax.ShapeDtypeStruct((m, n), dt) for n, dt in outs]
    out_shape += [jax.ShapeDtypeStruct((g * n, m // g), dt) for n, dt, g in flipped]
    out_shape += [jax.ShapeDtypeStruct((1, n), F32) for n in accs]
    first, last = _grid_ends((m // tm,))
    res, rode = _call(body, name=name, grid=(m // tm,), in_specs=in_specs, out_specs=out_specs, out_shape=out_shape,
                      scratch_shapes=[], args=list(tiled) + list(bcast), ride=ride, first=first, last=last)
    return res if ride is None else (res, rode)


def _rstd(x):
    return lax.rsqrt(jnp.mean(x * x, axis=-1, keepdims=True) + RMS_EPS)


def _rms_bwd(dh, x, g):
    xh = x * _rstd(x)
    dxh = dh * g
    dx = _rstd(x) * (dxh - xh * jnp.mean(dxh * xh, axis=-1, keepdims=True))
    return dx, dh * xh


def _sigmoid(x):
    return 1.0 / (1.0 + jnp.exp(-x))


_GELU_K = math.sqrt(2.0 / math.pi)
_GELU_C = 0.044715


def _gelu(x):
    return 0.5 * x * (1.0 + jnp.tanh(_GELU_K * (x + _GELU_C * x * x * x)))


def _gelu_grad(x):
    th = jnp.tanh(_GELU_K * (x + _GELU_C * x * x * x))
    return 0.5 * (1.0 + th) + 0.5 * x * (1.0 - th * th) * _GELU_K * (1.0 + 3.0 * _GELU_C * x * x)


class _Exchange:
    def __init__(self, arrays, outs, n_sems, sends, recvs=None, local=None, aliases=None):
        self.arrays, self.outs, self.n_sems = list(arrays), list(outs), n_sems
        self.sends, self.local, self.aliases = sends, local, aliases or {}
        self.recvs = recvs or (lambda i, o: [(k, dst) for k, _, dst, _ in sends(i, o)])

    def __add__(self, other):
        na, no, ns = len(self.arrays), len(self.outs), self.n_sems
        mine = lambda f: (lambda i, o: f(i[:na], o[:no]))
        shift = lambda f, at: (lambda i, o: [(k + ns,) + tuple(rest) for k, *rest in f(i[na:], o[no:])]) if at else None
        both = lambda f, g: (lambda i, o: f(i, o) + g(i, o))
        local = None
        if self.local or other.local:
            la = mine(self.local) if self.local else (lambda i, o: [])
            lb = (lambda i, o: other.local(i[na:], o[no:])) if other.local else (lambda i, o: [])
            local = both(la, lb)
        aliases = dict(self.aliases)
        aliases.update({na + i: no + o for i, o in other.aliases.items()})
        return _Exchange(self.arrays + other.arrays, self.outs + other.outs, ns + other.n_sems,
                         both(mine(self.sends), shift(other.sends, True)),
                         both(mine(self.recvs), shift(other.recvs, True)), local, aliases)

    def descriptors(self, in_refs, out_refs, send_sems, recv_sems, local_sems):
        me = _mesh_place()
        remote = lambda k, src, dst, to: pltpu.make_async_remote_copy(
            src_ref=src, dst_ref=dst, send_sem=send_sems.at[k], recv_sem=recv_sems.at[k], device_id=to,
            device_id_type=MESH)
        out = [remote(*s) for s in self.sends(in_refs, out_refs)]
        arrive = [remote(k, dst, dst, me) for k, dst in self.recvs(in_refs, out_refs)]
        own = [pltpu.make_async_copy(src, dst, local_sems.at[i])
               for i, (src, dst) in enumerate(self.local(in_refs, out_refs) if self.local else [])]
        return out, arrive, own

    def start(self, *refs):
        out, _, own = self.descriptors(*refs)
        for cp in own + out:
            cp.start()

    def finish(self, *refs):
        out, arrive, own = self.descriptors(*refs)
        for cp in arrive:
            cp.wait_recv()
        for cp in out:
            cp.wait_send()
        for cp in own:
            cp.wait()


def _call(body, *, name, grid, in_specs, out_specs, out_shape, scratch_shapes, args, ride=None, first=None, last=None):
    if ride is None:
        res = pl.pallas_call(body, name=name, grid=grid, in_specs=in_specs, out_specs=out_specs, out_shape=out_shape,
                             scratch_shapes=scratch_shapes, compiler_params=_params())(*args)
        return list(res), []
    n_in, n_out, n_scr = len(in_specs), len(out_specs), len(scratch_shapes)
    r_in, r_out = len(ride.arrays), len(ride.outs)

    def wrapped(*refs):
        ins, refs = refs[:n_in], refs[n_in:]
        x_in, refs = refs[:r_in], refs[r_in:]
        outs, refs = refs[:n_out], refs[n_out:]
        x_out, refs = refs[:r_out], refs[r_out:]
        scr, sems = refs[:n_scr], refs[n_scr:]

        @pl.when(first())
        def _():
            ride.start(x_in, x_out, *sems)

        body(*ins, *outs, *scr)

        @pl.when(last())
        def _():
            ride.finish(x_in, x_out, *sems)

    n_local = max(1, len(ride.arrays))
    res = pl.pallas_call(
        wrapped, name=name, grid=grid, in_specs=list(in_specs) + [_ANY] * r_in,
        out_specs=list(out_specs) + [_ANY] * r_out, out_shape=list(out_shape) + ride.outs,
        scratch_shapes=list(scratch_shapes) + [pltpu.SemaphoreType.DMA((ride.n_sems,)),
                                               pltpu.SemaphoreType.DMA((ride.n_sems,)),
                                               pltpu.SemaphoreType.DMA((n_local,))],
        input_output_aliases={n_in + i: n_out + o for i, o in ride.aliases.items()},
        compiler_params=_params(has_side_effects=True),
    )(*args, *ride.arrays)
    return list(res[:n_out]), list(res[n_out:])


def _gather_first(shards):
    def sends(i, o):
        x, y, c = _mesh_place()
        peers = [(x, y, 1 - c)] + [(px, py, c) for px, py in _chips(x, y)[1:]]
        return [(4 * w + k, i[w], o[w].at[_slab(x, y, c)], to) for w in range(len(i)) for k, to in enumerate(peers)]

    def recvs(i, o):
        x, y, c = _mesh_place()
        peers = [(x, y, 1 - c)] + [(px, py, c) for px, py in _chips(x, y)[1:]]
        return [(4 * w + k, o[w].at[_slab(*peer)]) for w in range(len(i)) for k, peer in enumerate(peers)]

    def local(i, o):
        return [(i[w], o[w].at[_slab(*_mesh_place())]) for w in range(len(i))]

    outs = [jax.ShapeDtypeStruct((N_DEV,) + a.shape, a.dtype) for a in shards]
    return _Exchange(shards, outs, 4 * len(shards), sends, recvs, local)


def _gather_second(gathered):
    def sends(i, o):
        x, y, c = _mesh_place()
        return [(3 * w + j, o[w].at[_slab(px, py, c)], o[w].at[_slab(px, py, c)], (x, y, 1 - c))
                for w in range(len(o)) for j, (px, py) in enumerate(_chips(x, y)[1:])]

    def recvs(i, o):
        x, y, c = _mesh_place()
        return [(3 * w + j, o[w].at[_slab(px, py, 1 - c)])
                for w in range(len(o)) for j, (px, py) in enumerate(_chips(x, y)[1:])]

    outs = [jax.ShapeDtypeStruct(a.shape, a.dtype) for a in gathered]
    return _Exchange(gathered, outs, 3 * len(gathered), sends, recvs, aliases={w: w for w in range(len(gathered))})


def _reduce_sibling(slabs):
    def sends(i, o):
        x, y, c = _mesh_place()
        return [(4 * w + k, i[w].at[_slab(px, py, 1 - c)], o[w].at[k], (x, y, 1 - c))
                for w in range(len(i)) for k, (px, py) in enumerate(_chips(x, y))]

    outs = [jax.ShapeDtypeStruct((4,) + a.shape[1:], a.dtype) for a in slabs]
    return _Exchange(slabs, outs, 4 * len(slabs), sends)


def _reduce_chips(partials):
    def sends(i, o):
        x, y, c = _mesh_place()
        return [(3 * w + k, i[w].at[k], o[w].at[k], (px, py, c))
                for w in range(len(i)) for k, (px, py) in enumerate(_chips(x, y)[1:])]

    outs = [jax.ShapeDtypeStruct(a.shape, a.dtype) for a in partials]
    return _Exchange(partials, outs, 3 * len(partials), sends)


def _head_masks():
    lane_head = lax.broadcasted_iota(jnp.int32, (1, GROUP_LANES), 1) // HEAD_DIM
    return [(lane_head == h).astype(F32) for h in range(HEADS_PER_GROUP)]


def _head_block_diag():
    r = lax.broadcasted_iota(jnp.int32, (GROUP_LANES, GROUP_LANES), 0) // HEAD_DIM
    c = lax.broadcasted_iota(jnp.int32, (GROUP_LANES, GROUP_LANES), 1) // HEAD_DIM
    return (r == c).astype(BF16)


def _head_mean(x, bd):
    hi = x.astype(BF16)
    lo = (x - hi.astype(F32)).astype(BF16)
    return (_dot(hi, bd) + _dot(lo, bd)) * (1.0 / HEAD_DIM)


def _stack_heads(x, masks):
    return jnp.concatenate([x * m for m in masks], axis=0)


def _unstack_heads(xs, masks):
    out = xs[0:GRID_W] * masks[0]
    for h in range(1, HEADS_PER_GROUP):
        out = out + xs[h * GRID_W:(h + 1) * GRID_W] * masks[h]
    return out


def _row_start(r, rows):
    return jnp.clip(r - WIN_H // 2, 0, rows - WIN_H)


ROWS_PER_STEP = 4


def _attn_common_specs(seq, n_hg, rows):
    win_keys = WIN_H * GRID_W
    q_spec = pl.BlockSpec((ROWS_PER_STEP * GRID_W, GROUP_LANES), lambda g, r: (r, g))
    k_spec = pl.BlockSpec((seq, GROUP_LANES), lambda g, r: (0, n_hg + g))
    v_spec = pl.BlockSpec((seq, GROUP_LANES), lambda g, r: (0, 2 * n_hg + g))
    gain_spec = pl.BlockSpec((1, GROUP_LANES), lambda g, r: (0, 0))

    def variant(r):
        return _row_start(r, rows) - r + (WIN_H - 1)

    bias_specs = [pl.BlockSpec((None, None, HEADS_PER_GROUP, GRID_W, win_keys),
                               lambda g, r, h=h: (g, variant(ROWS_PER_STEP * r + h), 0, 0, 0))
                  for h in range(ROWS_PER_STEP)]
    return q_spec, k_spec, v_spec, gain_spec, bias_specs, variant


def _attn_prepare_kv(k_ref, v_ref, kg, kn_scr, vb_scr, bd, seq):
    chunk = _tile(seq, 512)

    def step(c, carry):
        rows = pl.ds(pl.multiple_of(c * chunk, chunk), chunk)
        k = k_ref[rows, :]
        kn_scr[rows, :] = (k * lax.rsqrt(_head_mean(k * k, bd) + RMS_EPS) * kg).astype(BF16)
        vb_scr[rows, :] = v_ref[rows, :].astype(BF16)
        return carry

    lax.fori_loop(0, seq // chunk, step, 0)


def _attn_probs(qn, kw, bias, masks):
    qs = _stack_heads(qn, masks).astype(BF16)
    s = _dot(qs, kw, _NT) * (1.0 / math.sqrt(HEAD_DIM)) + bias
    m = jnp.max(s, axis=-1, keepdims=True)
    p = jnp.exp(s - m)
    return qs, p * (1.0 / jnp.sum(p, axis=-1, keepdims=True))


def _grid_ends(grid):
    first = lambda: functools.reduce(jnp.logical_and, [pl.program_id(a) == 0 for a in range(len(grid))])
    last = lambda: functools.reduce(jnp.logical_and, [pl.program_id(a) == n - 1 for a, n in enumerate(grid)])
    return first, last


def _attn_fwd(z, qg4, kg4, btab, ride=None):
    seq = z.shape[0]
    a_width = btab.shape[0] * GROUP_LANES
    n_hg, rows, win_keys = btab.shape[0], seq // GRID_W, WIN_H * GRID_W
    q_spec, k_spec, v_spec, gain_spec, bias_specs, _ = _attn_common_specs(seq, n_hg, rows)
    grid = (n_hg, rows // ROWS_PER_STEP)

    def body(q_ref, k_ref, v_ref, qg_ref, kg_ref, *rest):
        b_refs, (o_ref, kn_scr, vb_scr) = rest[:ROWS_PER_STEP], rest[ROWS_PER_STEP:]
        bd, masks = _head_block_diag(), _head_masks()

        @pl.when(pl.program_id(1) == 0)
        def _():
            _attn_prepare_kv(k_ref, v_ref, kg_ref[...], kn_scr, vb_scr, bd, seq)

        for h in range(ROWS_PER_STEP):
            r = ROWS_PER_STEP * pl.program_id(1) + h
            mine = slice(h * GRID_W, (h + 1) * GRID_W)
            win = pl.ds(pl.multiple_of(_row_start(r, rows) * GRID_W, GRID_W), win_keys)
            q = q_ref[mine, :]
            qn = q * lax.rsqrt(_head_mean(q * q, bd) + RMS_EPS) * qg_ref[...]
            bias = b_refs[h][...].reshape(HEADS_PER_GROUP * GRID_W, win_keys)
            _, p = _attn_probs(qn, kn_scr[win, :], bias, masks)
            o_ref[mine, :] = _unstack_heads(_dot(p.astype(BF16), vb_scr[win, :]), masks)

    first, last = _grid_ends(grid)
    (ya,), rode = _call(
        body, name="attn_fwd", grid=grid,
        in_specs=[q_spec, k_spec, v_spec, gain_spec, gain_spec] + bias_specs,
        out_specs=[pl.BlockSpec((ROWS_PER_STEP * GRID_W, GROUP_LANES), lambda g, r: (r, g))],
        out_shape=[jax.ShapeDtypeStruct((seq, a_width), F32)],
        scratch_shapes=[pltpu.VMEM((seq, GROUP_LANES), BF16), pltpu.VMEM((seq, GROUP_LANES), BF16)],
        args=(z, z, z, qg4, kg4) + (btab,) * ROWS_PER_STEP, ride=ride, first=first, last=last)
    return ya, rode


def _attn_bwd(z, d_out, qg4, kg4, btab, ride=None):
    seq = z.shape[0]
    n_hg, rows, win_keys = btab.shape[0], seq // GRID_W, WIN_H * GRID_W
    a_width = n_hg * GROUP_LANES
    q_spec, k_spec, v_spec, gain_spec, bias_specs, variant = _attn_common_specs(seq, n_hg, rows)
    scale = 1.0 / math.sqrt(HEAD_DIM)
    grid = (n_hg, rows // ROWS_PER_STEP)

    def body(q_ref, k_ref, v_ref, do_ref, qg_ref, kg_ref, *rest):
        b_refs, rest = rest[:ROWS_PER_STEP], rest[ROWS_PER_STEP:]
        dq_ref, dk_out, dv_out, db_ref, dqg_ref, dkg_ref, kn_scr, vb_scr, dk_ref, dv_ref = rest
        bd, masks = _head_block_diag(), _head_masks()

        @pl.when(pl.program_id(1) == 0)
        def _():
            _attn_prepare_kv(k_ref, v_ref, kg_ref[...], kn_scr, vb_scr, bd, seq)
            dk_ref[...] = jnp.zeros_like(dk_ref)
            dv_ref[...] = jnp.zeros_like(dv_ref)
            db_ref[...] = jnp.zeros_like(db_ref)
            dqg_ref[...] = jnp.zeros_like(dqg_ref)

        qg = qg_ref[...]
        for h in range(ROWS_PER_STEP):
            r = ROWS_PER_STEP * pl.program_id(1) + h
            mine = slice(h * GRID_W, (h + 1) * GRID_W)
            win = pl.ds(pl.multiple_of(_row_start(r, rows) * GRID_W, GRID_W), win_keys)
            q = q_ref[mine, :]
            rq = lax.rsqrt(_head_mean(q * q, bd) + RMS_EPS)
            qh = q * rq
            kw, vw = kn_scr[win, :], vb_scr[win, :]
            bias = b_refs[h][...].reshape(HEADS_PER_GROUP * GRID_W, win_keys)
            qs, p = _attn_probs(qh * qg, kw, bias, masks)
            dos = _stack_heads(do_ref[mine, :], masks).astype(BF16)
            dp = _dot(dos, vw, _NT)
            ds = p * (dp - jnp.sum(p * dp, axis=-1, keepdims=True))
            db_ref[variant(r)] += ds.reshape(HEADS_PER_GROUP, GRID_W, win_keys)
            dsb = ds.astype(BF16)
            dqn = _unstack_heads(_dot(dsb, kw), masks) * scale
            dk_ref[win, :] += _dot(dsb, qs, _TN) * scale
            dv_ref[win, :] += _dot(p.astype(BF16), dos, _TN)
            dqg_ref[...] += jnp.sum(dqn * qh, axis=0, keepdims=True)
            dqh = dqn * qg
            dq_ref[mine, :] = (rq * (dqh - qh * _head_mean(dqh * qh, bd))).astype(BF16)

        @pl.when(pl.program_id(1) == grid[1] - 1)
        def _():
            chunk = _tile(seq, 512)
            kg = kg_ref[...]

            def step(c, dkg):
                rws = pl.ds(pl.multiple_of(c * chunk, chunk), chunk)
                k = k_ref[rws, :]
                rk = lax.rsqrt(_head_mean(k * k, bd) + RMS_EPS)
                kh = k * rk
                dkn = dk_ref[rws, :]
                dkh = dkn * kg
                dk_out[rws, :] = (rk * (dkh - kh * _head_mean(dkh * kh, bd))).astype(BF16)
                dv_out[rws, :] = dv_ref[rws, :].astype(BF16)
                return dkg + jnp.sum(dkn * kh, axis=0, keepdims=True)

            dkg_ref[...] = lax.fori_loop(0, seq // chunk, step, jnp.zeros((1, GROUP_LANES), F32))

    col_spec = pl.BlockSpec((seq, GROUP_LANES), lambda g, r: (0, g))
    gsum_spec = pl.BlockSpec((None, 1, GROUP_LANES), lambda g, r: (g, 0, 0))
    first, last = _grid_ends(grid)
    rows_spec = pl.BlockSpec((ROWS_PER_STEP * GRID_W, GROUP_LANES), lambda g, r: (r, g))
    return _call(
        body, name="attn_bwd", grid=grid,
        in_specs=[q_spec, k_spec, v_spec, rows_spec, gain_spec, gain_spec] + bias_specs,
        out_specs=[rows_spec, col_spec, col_spec,
                   pl.BlockSpec((None, WIN_H, HEADS_PER_GROUP, GRID_W, win_keys), lambda g, r: (g, 0, 0, 0, 0)),
                   gsum_spec, gsum_spec],
        out_shape=[jax.ShapeDtypeStruct((seq, a_width), BF16)] * 3
        + [jax.ShapeDtypeStruct(btab.shape, F32)]
        + [jax.ShapeDtypeStruct((n_hg, 1, GROUP_LANES), F32)] * 2,
        scratch_shapes=[pltpu.VMEM((seq, GROUP_LANES), BF16), pltpu.VMEM((seq, GROUP_LANES), BF16),
                        pltpu.VMEM((seq, GROUP_LANES), F32), pltpu.VMEM((seq, GROUP_LANES), F32)],
        args=(z, z, z, d_out, qg4, kg4) + (btab,) * ROWS_PER_STEP, ride=ride, first=first, last=last)


DC_SLOTS = 2 * WIN_W


def _bias_spread(c):
    shape = (WIN_H * DC_SLOTS, WIN_H * GRID_W)
    rows = lax.broadcasted_iota(jnp.int32, shape, 0)
    cols = lax.broadcasted_iota(jnp.int32, shape, 1)
    row_i, row_d = rows // DC_SLOTS, rows % DC_SLOTS
    col_i, kc = cols // GRID_W, cols % GRID_W
    col_start = jnp.clip(c - WIN_W // 2, 0, GRID_W - WIN_W)
    col_in = (kc >= col_start) & (kc < col_start + WIN_W)
    hit = (row_i == col_i) & (row_d == kc - c + (WIN_W - 1)) & col_in
    mask_slot = (row_i == 0) & (row_d == DC_SLOTS - 1) & jnp.logical_not(col_in)
    return jnp.where(hit, 1.0, jnp.where(mask_slot, NEG_INF, 0.0)).astype(F32)


def _bias_table(rpb):
    n_h = rpb.shape[0]
    n_hg = n_h // HEADS_PER_GROUP
    rows = jnp.stack([rpb[:, v:v + WIN_H] for v in range(WIN_H)], axis=1)
    rows = jnp.pad(rows, ((0, 0), (0, 0), (0, 0), (0, DC_SLOTS - rows.shape[-1])))
    rows = rows.at[:, :, 0, DC_SLOTS - 1].set(1.0)
    rows = rows.reshape(n_hg, HEADS_PER_GROUP, WIN_H, WIN_H * DC_SLOTS).transpose(0, 2, 1, 3)
    n_rows, win_keys, depth = n_h * WIN_H, WIN_H * GRID_W, WIN_H * DC_SLOTS

    def body(r_ref, o_ref):
        for cc in range(SUBLANES):
            spread = _bias_spread(pl.program_id(0) * SUBLANES + cc)
            o_ref[cc] = lax.dot_general(r_ref[...], spread, _NN, precision=lax.Precision.HIGHEST,
                                        preferred_element_type=F32)

    tab = pl.pallas_call(
        body, name="rpb_spread", grid=(GRID_W // SUBLANES,),
        in_specs=[pl.BlockSpec((n_rows, depth), lambda c: (0, 0))],
        out_specs=pl.BlockSpec((SUBLANES, n_rows, win_keys), lambda c: (c, 0, 0)),
        out_shape=jax.ShapeDtypeStruct((GRID_W, n_rows, win_keys), F32), compiler_params=_params(),
    )(rows.reshape(n_rows, depth))
    return tab.transpose(1, 0, 2).reshape(n_hg, WIN_H, HEADS_PER_GROUP, GRID_W, win_keys)


def _bias_grad(dtab, n_h):
    n_hg = n_h // HEADS_PER_GROUP
    n_rows, win_keys, depth = n_h * WIN_H, WIN_H * GRID_W, WIN_H * DC_SLOTS

    def body(d_ref, o_ref):
        total = None
        for cc in range(SUBLANES):
            spread = _bias_spread(pl.program_id(0) * SUBLANES + cc).astype(BF16)
            t = _dot(d_ref[cc].astype(BF16), spread, _NT)
            total = t if total is None else total + t

        @pl.when(pl.program_id(0) == 0)
        def _():
            o_ref[...] = total

        @pl.when(pl.program_id(0) > 0)
        def _():
            o_ref[...] += total

    d_rows = pl.pallas_call(
        body, name="rpb_diag_sum", grid=(GRID_W // SUBLANES,),
        in_specs=[pl.BlockSpec((SUBLANES, n_rows, win_keys), lambda c: (c, 0, 0))],
        out_specs=pl.BlockSpec((n_rows, depth), lambda c: (0, 0)),
        out_shape=jax.ShapeDtypeStruct((n_rows, depth), F32), compiler_params=_params(),
    )(dtab.reshape(n_rows, GRID_W, win_keys).transpose(1, 0, 2))
    d_rows = d_rows.reshape(n_hg, WIN_H, HEADS_PER_GROUP, WIN_H, DC_SLOTS).transpose(0, 2, 1, 3, 4)
    d_rows = d_rows.reshape(n_h, WIN_H, WIN_H, DC_SLOTS)[..., : 2 * WIN_W - 1]
    out = jnp.zeros((n_h, 2 * WIN_H - 1, 2 * WIN_W - 1), F32)
    for v in range(WIN_H):
        out = out.at[:, v:v + WIN_H].add(d_rows[:, v])
    return out


def _cmul(ar, ai, br, bi):
    return ar * br - ai * bi, ar * bi + ai * br


def _s5_discretize(a_re, a_im, dt, b_re, b_im, ride=None):
    c = b_re.shape[1]

    def fn(are, aim, dt_, bre, bim):
        lr, li = jnp.minimum(are, A_RE_MAX), aim
        mag = jnp.exp(lr * dt_)
        l1r, l1i = mag * jnp.cos(li * dt_), mag * jnp.sin(li * dt_)
        den = lr * lr + li * li
        nr, ni = l1r - 1.0, l1i
        cr, ci = (nr * lr + ni * li) / den, (ni * lr - nr * li) / den
        bbr, bbi = _cmul(cr, ci, bre, bim)
        shape = (are.shape[0], SUBLANES)
        lane = lax.broadcasted_iota(jnp.int32, shape, 1)
        pr, pi = l1r, l1i
        acc_r, acc_i = jnp.zeros(shape, F32), jnp.zeros(shape, F32)
        for k in range(SUBLANES):
            acc_r = jnp.where(lane == k, pr, acc_r)
            acc_i = jnp.where(lane == k, pi, acc_i)
            pr, pi = _cmul(pr, pi, l1r, l1i)
        return acc_r, acc_i, cr, ci, bbr, bbi

    return _rowwise(fn, [a_re, a_im, dt, b_re, b_im], [],
                    [(SUBLANES, F32), (SUBLANES, F32), (1, F32), (1, F32), (c, F32), (c, F32)],
                    name="s5_discretize", tm=1024, ride=ride)


def _s5_param_grads(a_re, a_im, dt, b_re, b_im, l1r, l1i, cr, ci, bbr, bbi, r_re, r_im, gb_re, gb_im):
    c = b_re.shape[1]

    def fn(are, aim, dt_, bre, bim, l1r_, l1i_, cr_, ci_, bbr_, bbi_, rr, ri, gbr, gbi):
        lr, li = jnp.minimum(are, A_RE_MAX), aim
        den = lr * lr + li * li
        dbr, dbi = _cmul(cr_, -ci_, gbr, gbi)
        gcr, gci = _cmul(bre, -bim, gbr, gbi)
        gcr, gci = jnp.sum(gcr, axis=1, keepdims=True), jnp.sum(gci, axis=1, keepdims=True)
        qr, qi = _cmul(bbr_, -bbi_, gbr, gbi)
        qr = rr - jnp.sum(qr, axis=1, keepdims=True)
        qi = ri - jnp.sum(qi, axis=1, keepdims=True)
        tr, ti = _cmul(gcr, gci, lr / den, li / den)
        ur, ui = _cmul(l1r_, -l1i_, tr, ti)
        gwr, gwi = qr + ur, qi + ui
        vr, vi = _cmul(cr_, -ci_, lr / den, li / den)
        vr, vi = _cmul(gcr, gci, vr, vi)
        glr, gli = dt_ * gwr - vr, dt_ * gwi - vi
        return jnp.where(are < A_RE_MAX, glr, 0.0), gli, (gwr * lr + gwi * li) * dt_, dbr, dbi

    return _rowwise(fn, [a_re, a_im, dt, b_re, b_im, l1r, l1i, cr, ci, bbr, bbi, r_re, r_im, gb_re, gb_im], [],
                    [(1, F32), (1, F32), (1, F32), (c, F32), (c, F32)], name="s5_param_grads", tm=1024)


def _s5_scan(v, win_re, win_im, tabs, wo_re, wo_im, *, reverse, name, t_chunk=512, ride=None):
    seq, width = v.shape
    n_tiles, n_state = width // U_TILE, width * (SSM_P // SSM_C)
    t_chunk = _tile(seq, t_chunk)
    n_chunks, n_blk = seq // t_chunk, t_chunk // SUBLANES
    last_row = 0 if reverse else SUBLANES - 1

    def chunk_of(j):
        return (n_chunks - 1 - j) if reverse else j

    def body(v_ref, wir_ref, wii_ref, tab_ref, wor_ref, woi_ref, sr_ref, si_ref, y_ref, carry, wr, wi):
        @pl.when(pl.program_id(0) == 0)
        def _():
            carry[...] = jnp.zeros_like(carry)

        for j0 in range(0, n_tiles, TILES_TOGETHER):
            tiles = list(range(j0, min(j0 + TILES_TOGETHER, n_tiles)))
            lanes = [slice(jt * ST_TILE, (jt + 1) * ST_TILE) for jt in tiles]
            for w, (jt, ls) in enumerate(zip(tiles, lanes)):
                vj = v_ref[:, jt * U_TILE:(jt + 1) * U_TILE].astype(BF16)
                xr = _dot(vj, wir_ref[jt]).reshape(n_blk, SUBLANES, ST_TILE)
                xi = _dot(vj, wii_ref[jt]).reshape(n_blk, SUBLANES, ST_TILE)
                for s, k in enumerate((1, 2, 4)):
                    sh = (SUBLANES - k) if reverse else k
                    tr, ti = pltpu.roll(xr, sh, 1), pltpu.roll(xi, sh, 1)
                    lr, li = tab_ref[2 * s, :, ls][None], tab_ref[2 * s + 1, :, ls][None]
                    xr, xi = xr + lr * tr - li * ti, xi + lr * ti + li * tr
                wr[w] = xr.reshape(t_chunk, ST_TILE)
                wi[w] = xi.reshape(t_chunk, ST_TILE)
            powers = [(tab_ref[6, :, ls], tab_ref[7, :, ls]) for ls in lanes]

            def blk(b, c, powers=powers):
                bb = (n_blk - 1 - b) if reverse else b
                rows = pl.ds(pl.multiple_of(bb * SUBLANES, SUBLANES), SUBLANES)
                out = []
                for w, ((cr, ci), (lr, li)) in enumerate(zip(c, powers)):
                    xr = wr[w, rows, :] + lr * cr - li * ci
                    xi = wi[w, rows, :] + lr * ci + li * cr
                    wr[w, rows, :], wi[w, rows, :] = xr, xi
                    shape = (SUBLANES, ST_TILE)
                    out.append((jnp.broadcast_to(xr[last_row:last_row + 1], shape),
                                jnp.broadcast_to(xi[last_row:last_row + 1], shape)))
                return tuple(out)

            ends = lax.fori_loop(0, n_blk, blk, tuple((carry[0, :, ls], carry[1, :, ls]) for ls in lanes), unroll=2)
            for w, (jt, ls) in enumerate(zip(tiles, lanes)):
                carry[0, :, ls], carry[1, :, ls] = ends[w]
                xr_b, xi_b = wr[w].astype(BF16), wi[w].astype(BF16)
                sr_ref[:, ls], si_ref[:, ls] = xr_b, xi_b
                y_ref[:, jt * U_TILE:(jt + 1) * U_TILE] = _dot(xr_b, wor_ref[jt]) + _dot(xi_b, woi_ref[jt])

    whole = lambda a: pl.BlockSpec(a.shape, lambda j, nd=a.ndim: (0,) * nd)
    st_spec = pl.BlockSpec((t_chunk, n_state), lambda j: (chunk_of(j), 0))
    v_spec = pl.BlockSpec((t_chunk, width), lambda j: (chunk_of(j), 0))
    first, last = _grid_ends((n_chunks,))
    return _call(
        body, name=name, grid=(n_chunks,),
        in_specs=[v_spec, whole(win_re), whole(win_im), whole(tabs), whole(wo_re), whole(wo_im)],
        out_specs=[st_spec, st_spec, v_spec],
        out_shape=[jax.ShapeDtypeStruct((seq, n_state), BF16)] * 2 + [jax.ShapeDtypeStruct((seq, width), F32)],
        scratch_shapes=[pltpu.VMEM((2, SUBLANES, n_state), F32), pltpu.VMEM((TILES_TOGETHER, t_chunk, ST_TILE), F32),
                        pltpu.VMEM((TILES_TOGETHER, t_chunk, ST_TILE), F32)],
        args=(v, win_re, win_im, tabs, wo_re, wo_im), ride=ride, first=first, last=last)


def _s5_reduce(x_re, x_im, a_re, a_im, u, dy, *, name, t_chunk=1024, ride=None):
    seq, n_state = x_re.shape
    width = u.shape[1]
    n_tiles = width // U_TILE
    t_chunk = _tile(seq, t_chunk)

    def body(xr_ref, xi_ref, ar_ref, ai_ref, u_ref, dy_ref, rr_ref, ri_ref, gbr_ref, gbi_ref, gcr_ref, gci_ref):
        xrb, xib, arb, aib = xr_ref[...], xi_ref[...], ar_ref[...], ai_ref[...]
        xr, xi, ar, ai = xrb.astype(F32), xib.astype(F32), arb.astype(F32), aib.astype(F32)
        ub, dyb = u_ref[...].astype(BF16), dy_ref[...].astype(BF16)
        parts = (jnp.sum(ar * xr + ai * xi, axis=0, keepdims=True), jnp.sum(ai * xr - ar * xi, axis=0, keepdims=True),
                 _dot(ub, arb, _TN), _dot(ub, aib, _TN), _dot(dyb, xrb, _TN), _dot(dyb, xib, _TN))
        first = pl.program_id(1) == 0
        for ref, val in zip((rr_ref, ri_ref, gbr_ref, gbi_ref, gcr_ref, gci_ref), parts):
            @pl.when(first)
            def _():
                ref[...] = val

            @pl.when(jnp.logical_not(first))
            def _():
                ref[...] += val

    st_spec = pl.BlockSpec((t_chunk, ST_TILE), lambda j, t: (t, j))
    u_spec = pl.BlockSpec((t_chunk, U_TILE), lambda j, t: (t, j))
    r_spec = pl.BlockSpec((1, ST_TILE), lambda j, t: (0, j))
    g_spec = pl.BlockSpec((None, U_TILE, ST_TILE), lambda j, t: (j, 0, 0))
    first, last = _grid_ends((n_tiles, seq // t_chunk))
    return _call(
        body, name=name, grid=(n_tiles, seq // t_chunk),
        in_specs=[st_spec] * 4 + [u_spec] * 2,
        out_specs=[r_spec, r_spec] + [g_spec] * 4,
        out_shape=[jax.ShapeDtypeStruct((1, n_state), F32)] * 2
        + [jax.ShapeDtypeStruct((n_tiles, U_TILE, ST_TILE), F32)] * 4,
        scratch_shapes=[], args=(x_re, x_im, a_re, a_im, u, dy), ride=ride, first=first, last=last)


def _block_diag_in(ms):
    m = jnp.stack(ms)
    n, g, c, p = m.shape
    m5 = m.reshape(n, g // GROUPS_PER_TILE, GROUPS_PER_TILE, c, p)
    eye = jnp.eye(GROUPS_PER_TILE, dtype=m.dtype)
    out = m5[:, :, :, :, None, :] * eye[None, None, :, None, :, None]
    return out.astype(BF16).reshape(n, g // GROUPS_PER_TILE, GROUPS_PER_TILE * c, GROUPS_PER_TILE * p)


def _block_diag_take(m, c, p):
    t = m.shape[0]
    m5 = m.reshape(t, GROUPS_PER_TILE, c, GROUPS_PER_TILE, p)
    idx = jnp.arange(GROUPS_PER_TILE)
    return m5[:, idx, :, idx, :].transpose(1, 0, 3, 2).reshape(t * GROUPS_PER_TILE, p, c)


def _scan_tables(pw_re, pw_im, reverse):
    row = jnp.arange(SUBLANES)[:, None]
    tabs = []
    for k in (1, 2, 4):
        keep = (row <= SUBLANES - 1 - k) if reverse else (row >= k)
        tabs += [jnp.where(keep, pw_re[k - 1][None, :], 0.0), jnp.where(keep, pw_im[k - 1][None, :], 0.0)]
    order = jnp.arange(SUBLANES)[::-1] if reverse else jnp.arange(SUBLANES)
    tabs += [pw_re[order], pw_im[order]]
    return jnp.stack(tabs)


def _partial_sums(slabs, from_sibling, names):
    x, y, c = _mesh_place()
    theirs = jnp.stack([_slab(px, py, c) for px, py in _chips(x, y)[1:]]).astype(jnp.int32)
    out = []
    for s, f, n in zip(slabs, from_sibling, names):
        rows, cols = s.shape[1:]
        tr = _tile(rows, 512)

        def body(idx_ref, a_ref, b_ref, o_ref):
            o_ref[...] = (a_ref[...] + b_ref[...]).astype(BF16)

        out.append(pl.pallas_call(
            body, name=f"reduce_add_{n}",
            grid_spec=pltpu.PrefetchScalarGridSpec(
                num_scalar_prefetch=1, grid=(3, rows // tr),
                in_specs=[pl.BlockSpec((None, tr, cols), lambda k, i, idx: (idx[k], i, 0)),
                          pl.BlockSpec((None, tr, cols), lambda k, i, idx: (k + 1, i, 0))],
                out_specs=pl.BlockSpec((None, tr, cols), lambda k, i, idx: (k, i, 0))),
            out_shape=jax.ShapeDtypeStruct((3, rows, cols), BF16), compiler_params=_params(),
        )(theirs, s, f))
    return out


def _local_step(x, target, p, shards):
    seq, d_model = x.shape
    a_width = p["g_out_attn"].shape[-1]
    s_width = p["g_out_ssm"].shape[-1]
    n_heads = a_width // HEAD_DIM
    n_hg = n_heads // HEADS_PER_GROUP
    n_groups = s_width // SSM_C
    n_sh, in_sh = N_DEV, shards["w_in"].shape[-1]
    f_sh = shards["w_ffn_gate"].shape[-1]
    w = {}
    slab3 = lambda g, n: g.reshape(N_DEV, -1, shards[n].shape[-1])
    t2, t1 = _tile(seq, 2048), _tile(seq, 1024)
    n2, n1 = seq // t2, seq // t1

    n_col = 2 * n_groups * SSM_P
    col = lambda a: a.reshape(n_col, 1)
    a_re_c, a_im_c = col(p["ssm_a_re"]), col(p["ssm_a_im"])
    dt_c = col(jnp.broadcast_to(jnp.exp(p["ssm_log_step"])[:, :, None], (2, n_groups, SSM_P)))
    b_re_c, b_im_c = p["ssm_b_re"].reshape(n_col, SSM_C), p["ssm_b_im"].reshape(n_col, SSM_C)
    (pw_re, pw_im, cf_re, cf_im, bb_re, bb_im), got = _s5_discretize(a_re_c, a_im_c, dt_c, b_re_c, b_im_c,
                                                                     ride=_gather_first([shards["w_in"]]))

    twice = lambda f: (lambda *a: (f(*a),) * 2)
    (h1, h1_t), (w["w_in"],) = _rowwise(twice(x_norm), [x], [p["g_mix"]], [(d_model, BF16)],
                                        flipped=[(d_model, BF16, 1)], name="rms_mix", ride=_gather_second(got))
    z = _mm(h1, w["w_in"], name="in_proj", grid=(n2, n_sh),
            a_spec=pl.BlockSpec((t2, d_model), lambda i, j: (i, 0)),
            b_spec=pl.BlockSpec((None, d_model, in_sh), lambda i, j: (j, 0, 0)),
            o_spec=pl.BlockSpec((t2, in_sh), lambda i, j: (i, j)), o_shape=(seq, n_sh * in_sh), dims="nn")
    qg4 = jnp.tile(p["q_gain"], (1, HEADS_PER_GROUP))
    kg4 = jnp.tile(p["k_gain"], (1, HEADS_PER_GROUP))
    btab = _bias_table(p["rpb"])
    ya, got_a = _attn_fwd(z, qg4, kg4, btab, ride=_gather_first([shards["w_ffn_gate"], shards["w_ffn_up"]]))
    u = z[:, 3 * a_width:]
    n_state = n_groups * SSM_P
    pw_re = pw_re.reshape(2, n_state, SUBLANES).transpose(0, 2, 1)
    pw_im = pw_im.reshape(2, n_state, SUBLANES).transpose(0, 2, 1)
    bb_re4, bb_im4 = bb_re.reshape(2, n_groups, SSM_P, SSM_C), bb_im.reshape(2, n_groups, SSM_P, SSM_C)
    c_re, c_im = p["ssm_c_re"], p["ssm_c_im"]
    t21 = lambda a: a.transpose(0, 2, 1)
    maps_in = _block_diag_in([m for d in range(2) for m in (t21(bb_re4[d]), t21(bb_im4[d]), c_re[d], -c_im[d])])
    maps_out = _block_diag_in([m for d in range(2) for m in (t21(c_re[d]), -t21(c_im[d]), bb_re4[d], bb_im4[d])])
    fwd, bwd_in = [], []
    got_b = None
    for d in range(2):
        rev = d == 1
        tabs = _scan_tables(pw_re[d], pw_im[d], rev)
        if d == 0:
            ride = _gather_first([shards["w_glu"], shards["w_out"]])
        else:
            ride = _gather_second(got_a + got_b) + _gather_first([shards["w_ffn_down"]])
        (xs_re, xs_im, y_d), got = _s5_scan(u, maps_in[4 * d], maps_in[4 * d + 1], tabs, maps_out[4 * d],
                                            maps_out[4 * d + 1], reverse=rev, name=f"s5_fwd_{d}", ride=ride)
        if d == 0:
            got_b = got
        fwd.append((xs_re, xs_im, y_d))
        bwd_in.append((maps_in[4 * d + 2], maps_in[4 * d + 3], _scan_tables(pw_re[d], -pw_im[d], not rev),
                       maps_out[4 * d + 2], maps_out[4 * d + 3]))
    w["w_gate"], w["w_up"], w_glu_full, w_out_full, w_down_first = got
    w["w_glu"] = w_glu_full.reshape(-1, s_width)
    w["w_out"] = w_out_full.reshape(-1, d_model)

    ypre, yg, yg_t = _rowwise(s5_mid, [fwd[0][2], fwd[1][2], u], [p["ssm_d"]], [(s_width, F32), (s_width, F32)],
                              flipped=[(s_width, BF16, 1)], name="s5_skip_gelu")
    t_glu = _mm_plain(yg, w["w_glu"], "nn", name="glu_proj", tn=s_width)
    y_cat, y_cat_t = _rowwise(twice(mix_out_fwd), [ya, yg, t_glu], [p["b_glu"], p["g_out_attn"], p["g_out_ssm"]],
                              [(a_width + s_width, BF16)], flipped=[(a_width + s_width, BF16, 1)], name="mix_out")
    x1, (w["w_down"],) = _mm_plain(y_cat, w["w_out"], "nn", name="out_proj", res=x, tn=2048,
                                   ride=_gather_second([w_down_first]))

    h2, h2_t = _rowwise(twice(x_norm), [x1], [p["g_ffn"]], [(d_model, BF16)], flipped=[(d_model, BF16, 1)],
                        name="rms_ffn")
    flat = lambda a: a.reshape(n_sh * seq, f_sh)

    def gate_up_body(a_ref, wg_ref, wu_ref, g_ref, u_ref, act_ref, _):
        a = a_ref[...]
        gv, uv = _dot(a, wg_ref[...]), _dot(a, wu_ref[...])
        g_ref[...], u_ref[...] = gv.astype(BF16), uv.astype(BF16)
        act_ref[...] = swiglu_fwd(gv, uv).T.astype(BF16)

    shard_spec = pl.BlockSpec((None, d_model, f_sh), lambda i, j: (j, 0, 0))
    tile_spec = pl.BlockSpec((None, t1, f_sh), lambda i, j: (j, i, 0))
    (gate, up, act_t), _ = _call(
        gate_up_body, name="ffn_gate_up", grid=(n1, n_sh),
        in_specs=[pl.BlockSpec((t1, d_model), lambda i, j: (i, 0)), shard_spec, shard_spec],
        out_specs=[tile_spec, tile_spec, pl.BlockSpec((None, f_sh, t1), lambda i, j: (j, 0, i))],
        out_shape=[jax.ShapeDtypeStruct((n_sh, seq, f_sh), BF16)] * 2
        + [jax.ShapeDtypeStruct((n_sh, f_sh, seq), BF16)],
        scratch_shapes=[pltpu.VMEM((SUBLANES, 128), F32)], args=(h2, w["w_gate"], w["w_up"]))
    ffn_out = _mm(act_t, w["w_down"], name="ffn_down", grid=(n1, n_sh // 2), groups=2,
                  a_spec=pl.BlockSpec((2, f_sh, t1), lambda i, j: (j, 0, i)),
                  b_spec=pl.BlockSpec((2, f_sh, d_model), lambda i, j: (j, 0, 0)),
                  o_spec=pl.BlockSpec((t1, d_model), lambda i, j: (i, 0)), o_shape=(seq, d_model), dims="tn",
                  k_axis=1)

    dx2, dx2_b, sq = _rowwise(functools.partial(loss_head, inv_d=1.0 / d_model), [ffn_out, x1, target], [],
                              [(d_model, F32), (d_model, BF16)], [d_model], name="loss_head")
    loss = 0.5 * jnp.sum(sq) / d_model

    d_act = _mm(dx2_b, w["w_down"], name="ffn_down_dx", grid=(n2, n_sh),
                a_spec=pl.BlockSpec((t2, d_model), lambda i, j: (i, 0)),
                b_spec=pl.BlockSpec((None, f_sh, d_model), lambda i, j: (j, 0, 0)),
                o_spec=pl.BlockSpec((None, t2, f_sh), lambda i, j: (j, i, 0)), o_shape=(n_sh, seq, f_sh), dims="nt",
                out_dtype=BF16)
    g_w_down = _mm(act_t, dx2_b, name="ffn_down_dw", grid=(n_sh, n2),
                   a_spec=pl.BlockSpec((None, f_sh, t2), lambda j, k: (j, 0, k)),
                   b_spec=pl.BlockSpec((t2, d_model), lambda j, k: (k, 0)),
                   o_spec=pl.BlockSpec((None, f_sh, d_model), lambda j, k: (j, 0, 0)),
                   o_shape=(n_sh, f_sh, d_model), dims="nn", k_axis=1)
    d_gate, d_up = _rowwise(swiglu_bwd, [flat(d_act), flat(gate), flat(up)], [], [(f_sh, BF16), (f_sh, BF16)],
                            name="swiglu_bwd", tm=1024)
    d_gate, d_up = d_gate.reshape(n_sh, seq, f_sh), d_up.reshape(n_sh, seq, f_sh)
    d_h2 = _mm(d_gate, w["w_gate"], second=(d_up, w["w_up"]), name="ffn_up_gate_dx", grid=(n1, n_sh),
               a_spec=pl.BlockSpec((None, t1, f_sh), lambda i, j: (j, i, 0)),
               b_spec=pl.BlockSpec((None, d_model, f_sh), lambda i, j: (j, 0, 0)),
               o_spec=pl.BlockSpec((t1, d_model), lambda i, j: (i, 0)), o_shape=(seq, d_model), dims="nt", k_axis=1)
    ffn_dw = functools.partial(
        _mm, grid=(n_sh, n2), a_spec=pl.BlockSpec((d_model, t2), lambda j, k: (0, k)),
        b_spec=pl.BlockSpec((None, t2, f_sh), lambda j, k: (j, k, 0)),
        o_spec=pl.BlockSpec((None, d_model, f_sh), lambda j, k: (j, 0, 0)), o_shape=(n_sh, d_model, f_sh), dims="nn",
        k_axis=1)
    g_w_gate = ffn_dw(h2_t, d_gate, name="ffn_gate_dw")
    g_w_up = ffn_dw(h2_t, d_up, name="ffn_up_dw")
    dx1, g_g_ffn = _rowwise(residual_rms_bwd, [dx2, d_h2, x1], [p["g_ffn"]], [(d_model, F32)], [d_model],
                            name="rms_ffn_bwd")

    d_ycat = _mm_plain(dx1, w["w_out"], "nt", name="out_proj_dx", tn=2048)
    mix_w = a_width + s_width
    tm_o = _tile(mix_w, 1024)
    g_w_out = _mm(y_cat_t, dx1, name="out_proj_dw", grid=(mix_w // tm_o, n1),
                  a_spec=pl.BlockSpec((tm_o, t1), lambda i, k: (i, k)),
                  b_spec=pl.BlockSpec((t1, d_model), lambda i, k: (k, 0)),
                  o_spec=pl.BlockSpec((tm_o, d_model), lambda i, k: (i, 0)), o_shape=(mix_w, d_model), dims="nn",
                  k_axis=1)
    (d_ya, d_yg_direct, d_t, g_goa, g_gos, g_b_glu) = _rowwise(
        functools.partial(mix_out_bwd, a_width=a_width), [d_ycat, ya, yg, t_glu],
        [p["b_glu"], p["g_out_attn"], p["g_out_ssm"]],
        [(a_width, F32), (s_width, F32), (s_width, BF16)], [a_width, s_width, s_width], name="mix_out_bwd")
    d_yg = _mm_plain(d_t, w["w_glu"], "nt", name="glu_proj_dx", res=d_yg_direct, tn=s_width)
    g_w_glu = _mm(yg_t, d_t, name="glu_proj_dw", grid=(1, n1),
                  a_spec=pl.BlockSpec((s_width, t1), lambda i, k: (0, k)),
                  b_spec=pl.BlockSpec((t1, s_width), lambda i, k: (k, 0)),
                  o_spec=pl.BlockSpec((s_width, s_width), lambda i, k: (0, 0)), o_shape=(s_width, s_width),
                  dims="nn", k_axis=1)
    d_ypre, du_skip, g_ssm_d = _rowwise(gelu_skip_bwd, [d_yg, ypre, u], [p["ssm_d"]],
                                        [(s_width, F32), (s_width, F32)], [s_width], name="s5_skip_gelu_bwd")

    names = ("w_ffn_gate", "w_ffn_up", "w_ffn_down", "w_glu", "w_out")
    slabs = {n: slab3(g, n) for n, g in zip(names, (g_w_gate, g_w_up, g_w_down, g_w_glu, g_w_out))}
    sib, part, chips = {}, {}, {}
    du_dirs, adj, r_parts, gb_parts, gc_parts = [], [], [], [], []
    for d in range(2):
        win_re, win_im, tabs, wo_re, wo_im = bwd_in[d]
        ride = _reduce_sibling([slabs[n] for n in names]) if d == 0 else _reduce_chips([part["w_ffn_down"]])
        (as_re, as_im, du_d), got = _s5_scan(d_ypre, win_re, win_im, tabs, wo_re, wo_im, reverse=(d == 0),
                                             name=f"s5_bwd_{d}", ride=ride)
        du_dirs.append(du_d)
        adj.append((as_re, as_im))
        if d == 0:
            sib = dict(zip(names, got))
            part = dict(zip(names, _partial_sums([slabs[n] for n in names], got, names)))
        else:
            chips["w_ffn_down"] = got[0]
    for d in range(2):
        (r_re, r_im, gbt_re, gbt_im, gct_re, gct_im), got = _s5_reduce(
            fwd[d][0], fwd[d][1], adj[d][0], adj[d][1], u, d_ypre, name=f"s5_reduce_{d}",
            ride=_reduce_chips([part["w_glu"], part["w_out"]]) if d == 0 else None)
        if d == 0:
            chips["w_glu"], chips["w_out"] = got
        r_parts.append((r_re.reshape(n_state, 1), r_im.reshape(n_state, 1)))
        gb_parts.append((_block_diag_take(gbt_re, SSM_C, SSM_P), _block_diag_take(gbt_im, SSM_C, SSM_P)))
        gc_parts.append((_block_diag_take(gct_re, SSM_C, SSM_P), _block_diag_take(gct_im, SSM_C, SSM_P)))
    cat = lambda i, parts: jnp.concatenate([parts[0][i], parts[1][i]], axis=0)
    gbb_re, gbb_im = cat(0, gb_parts).reshape(n_col, SSM_C), cat(1, gb_parts).reshape(n_col, SSM_C)
    g_a_re, g_a_im, g_ls, g_b_re, g_b_im = _s5_param_grads(
        a_re_c, a_im_c, dt_c, b_re_c, b_im_c, pw_re[:, 0].reshape(n_col, 1), pw_im[:, 0].reshape(n_col, 1),
        cf_re, cf_im, bb_re, bb_im, cat(0, r_parts), cat(1, r_parts), gbb_re, gbb_im)
    g_c_re = cat(0, gc_parts).reshape(2, n_groups, SSM_P, SSM_C).transpose(0, 1, 3, 2)
    g_c_im = -cat(1, gc_parts).reshape(2, n_groups, SSM_P, SSM_C).transpose(0, 1, 3, 2)

    (d_q, d_k, d_v, d_btab, g_qg, g_kg), got = _attn_bwd(
        z, d_ya, qg4, kg4, btab, ride=_reduce_chips([part["w_ffn_gate"], part["w_ffn_up"]]))
    chips["w_ffn_gate"], chips["w_ffn_up"] = got
    d_u = _rowwise(lambda a, b, c: a + b + c, [du_dirs[0], du_dirs[1], du_skip], [], [(s_width, BF16)],
                   name="s5_du_sum")[0]
    d_z = jnp.concatenate([d_q, d_k, d_v, d_u], axis=1)
    fold_heads = lambda g: g.reshape(n_heads, HEAD_DIM).sum(axis=0, keepdims=True)
    small = {
        "q_gain": fold_heads(g_qg), "k_gain": fold_heads(g_kg), "rpb": _bias_grad(d_btab, n_heads),
        "ssm_a_re": g_a_re.reshape(2, n_groups, SSM_P), "ssm_a_im": g_a_im.reshape(2, n_groups, SSM_P),
        "ssm_b_re": g_b_re.reshape(2, n_groups, SSM_P, SSM_C), "ssm_b_im": g_b_im.reshape(2, n_groups, SSM_P, SSM_C),
        "ssm_c_re": g_c_re, "ssm_c_im": g_c_im,
        "ssm_log_step": g_ls.reshape(2, n_groups, SSM_P).sum(axis=-1),
        "ssm_d": g_ssm_d, "b_glu": g_b_glu, "g_out_attn": g_goa, "g_out_ssm": g_gos, "g_ffn": g_g_ffn,
    }
    g_w_in, got = _mm(h1_t, d_z, name="in_proj_dw", grid=(n_sh, n2),
                      a_spec=pl.BlockSpec((d_model, t2), lambda j, k: (0, k)),
                      b_spec=pl.BlockSpec((t2, in_sh), lambda j, k: (k, j)),
                      o_spec=pl.BlockSpec((None, d_model, in_sh), lambda j, k: (j, 0, 0)),
                      o_shape=(n_sh, d_model, in_sh), dims="nn", k_axis=1,
                      ride=_gather_first([_as_rows(small[n]) for n in SMALL_LATE]))
    d_h1, got = _mm(d_z, w["w_in"], name="in_proj_dx", grid=(n1, n_sh // 2), groups=2,
                    a_spec=pl.BlockSpec((t1, 2 * in_sh), lambda i, j: (i, j)),
                    b_spec=pl.BlockSpec((2, d_model, in_sh), lambda i, j: (j, 0, 0)),
                    o_spec=pl.BlockSpec((t1, d_model), lambda i, j: (i, 0)), o_shape=(seq, d_model), dims="nt",
                    k_axis=1, ride=_gather_second(got) + _reduce_sibling([g_w_in]))
    small_gathered, in_sibling = dict(zip(SMALL_LATE, got[:-1])), got[-1]
    in_part = _partial_sums([g_w_in], [in_sibling], ("w_in",))
    (grad_x, g_g_mix), (in_chips,) = _rowwise(residual_rms_bwd, [dx1, d_h1, x], [p["g_mix"]], [(d_model, F32)],
                                              [d_model], name="rms_mix_bwd", ride=_reduce_chips(in_part))
    reduced = {n: (slabs[n], sib[n], chips[n]) for n in slabs}
    reduced["w_in"] = (g_w_in, in_sibling, in_chips)
    return loss, grad_x, small_gathered, g_g_mix, reduced


def x_norm(xv, g):
    return xv * _rstd(xv) * g


def s5_mid(y0, y1, uv, d_skip):
    ypre = y0 + y1 + d_skip * uv
    yg = _gelu(ypre)
    return ypre, yg, yg


def mix_out_fwd(ya, yg, t, b_glu, g_oa, g_os):
    ys = yg * _sigmoid(t + b_glu)
    return jnp.concatenate([ya * _rstd(ya) * g_oa, ys * _rstd(ys) * g_os], axis=1)


def mix_out_bwd(d_y, ya, yg, t, b_glu, g_oa, g_os, *, a_width):
    sg = _sigmoid(t + b_glu)
    ys = yg * sg
    d_ya, c_goa = _rms_bwd(d_y[:, :a_width], ya, g_oa)
    d_ys, c_gos = _rms_bwd(d_y[:, a_width:], ys, g_os)
    d_t = d_ys * yg * sg * (1.0 - sg)
    return d_ya, d_ys * sg, d_t, c_goa, c_gos, d_t


def gelu_skip_bwd(d_yg, ypre, uv, d_skip):
    d_ypre = d_yg * _gelu_grad(ypre)
    return d_ypre, d_ypre * d_skip, d_ypre * uv


def swiglu_fwd(gv, uv):
    gv, uv = gv.astype(F32), uv.astype(F32)
    return gv * _sigmoid(gv) * uv


def swiglu_bwd(d_act, gv, uv):
    d_act, gv, uv = d_act.astype(F32), gv.astype(F32), uv.astype(F32)
    sg = _sigmoid(gv)
    return d_act * uv * (sg * (1.0 + gv * (1.0 - sg))), d_act * gv * sg


def loss_head(ffn_out, x1, target, *, inv_d):
    diff = ffn_out + x1 - target
    return diff * inv_d, diff * inv_d, diff * diff


def residual_rms_bwd(d_res, d_h, xv, g):
    dx, c_g = _rms_bwd(d_h, xv, g)
    return d_res + dx, c_g


_ANY = pl.BlockSpec(memory_space=pl.ANY)


def _mesh_place():
    return lax.axis_index("x"), lax.axis_index("y"), lax.axis_index("c")


def _chips(x, y):
    return [(x, y), (1 - x, y), (x, 1 - y), (1 - x, 1 - y)]


def _slab(px, py, pc):
    return 4 * px + 2 * py + pc


def _all_gather(arrs, *, name):
    n = len(arrs)

    def body(*refs):
        in_refs, out_refs = refs[:n], refs[n:2 * n]
        send_sems, recv_sems, local_sems = refs[2 * n:]
        x, y, c = _mesh_place()
        me, sibling = (x, y, c), (x, y, 1 - c)
        others = _chips(x, y)[1:]

        def copy(w, k, block, to, src=None):
            dst = out_refs[w].at[_slab(*block)]
            return pltpu.make_async_remote_copy(
                src_ref=dst if src is None else src, dst_ref=dst, send_sem=send_sems.at[7 * w + k],
                recv_sem=recv_sems.at[7 * w + k], device_id=to, device_id_type=MESH)

        mine = [pltpu.make_async_copy(in_refs[w], out_refs[w].at[_slab(*me)], local_sems.at[w]) for w in range(n)]
        first = []
        for w in range(n):
            mine[w].start()
            first.append(copy(w, 0, me, sibling, src=in_refs[w]))
            first += [copy(w, 1 + j, me, (*chip, c), src=in_refs[w]) for j, chip in enumerate(others)]
        for cp in first:
            cp.start()
        passed = []
        for j, chip in enumerate(others):
            for w in range(n):
                copy(w, 1 + j, (*chip, c), me).wait_recv()
                fwd = copy(w, 4 + j, (*chip, c), sibling)
                fwd.start()
                passed.append(fwd)
        for w in range(n):
            copy(w, 0, sibling, me).wait_recv()
        for j, chip in enumerate(others):
            for w in range(n):
                copy(w, 4 + j, (*chip, 1 - c), me).wait_recv()
        for cp in first + passed:
            cp.wait_send()
        for cp in mine:
            cp.wait()

    return pl.pallas_call(
        body, name=name, in_specs=[_ANY] * n, out_specs=[_ANY] * n,
        out_shape=[jax.ShapeDtypeStruct((N_DEV,) + a.shape, a.dtype) for a in arrs],
        scratch_shapes=[pltpu.SemaphoreType.DMA((7 * n,)), pltpu.SemaphoreType.DMA((7 * n,)),
                        pltpu.SemaphoreType.DMA((n,))],
        compiler_params=pltpu.CompilerParams(has_side_effects=True),
    )(*arrs)


def _adamw(w, m, v, parts, *, name, slab, tr=256):
    rows, cols = w.shape
    tr = _tile(rows, tr)
    n_p = len(parts)

    def body(slab_ref, *refs):
        w_ref, m_ref, v_ref = refs[:3]
        p_refs = refs[3:3 + n_p]
        g_ref, d_ref, nm_ref, nv_ref = refs[3 + n_p:]
        g = None
        for (_, lead), r in zip(parts, p_refs):
            for piece in ([r[...]] if lead is None else [r[i] for i in range(lead)]):
                g = piece.astype(F32) if g is None else g + piece.astype(F32)
        new_m = ADAM_B1 * m_ref[...] + (1.0 - ADAM_B1) * g
        new_v = ADAM_B2 * v_ref[...] + (1.0 - ADAM_B2) * (g * g)
        m_hat = new_m / (1.0 - ADAM_B1 ** ADAM_STEP)
        v_hat = new_v / (1.0 - ADAM_B2 ** ADAM_STEP)
        g_ref[...] = g
        d_ref[...] = -ADAM_LR * (m_hat / (jnp.sqrt(v_hat) + ADAM_EPS) + ADAM_WD * w_ref[...])
        nm_ref[...] = new_m
        nv_ref[...] = new_v

    tile = pl.BlockSpec((tr, cols), lambda i, s: (i, 0))
    p_specs = [pl.BlockSpec((None, tr, cols), lambda i, s: (s[0], i, 0)) if lead is None
               else pl.BlockSpec((lead, tr, cols), lambda i, s: (0, i, 0)) for _, lead in parts]
    return pl.pallas_call(
        body, name=name,
        grid_spec=pltpu.PrefetchScalarGridSpec(num_scalar_prefetch=1, grid=(rows // tr,),
                                               in_specs=[tile] * 3 + p_specs, out_specs=[tile] * 4),
        out_shape=[jax.ShapeDtypeStruct((rows, cols), F32)] * 4, compiler_params=_params(),
    )(jnp.reshape(slab, (1,)).astype(jnp.int32), w, m, v, *[a for a, _ in parts])


_DENSE_MIN = 256 * 128


def _as_rows(a):
    if a.shape[-1] < 128 and a.size >= _DENSE_MIN and a.size % 128 == 0:
        return a.reshape(-1, 128)
    return a.reshape(-1, a.shape[-1])


BIG = ("w_in", "w_glu", "w_out", "w_ffn_gate", "w_ffn_up", "w_ffn_down")
WEIGHTS = ("g_mix", "w_in", "q_gain", "k_gain", "rpb", "ssm_a_re", "ssm_a_im", "ssm_b_re", "ssm_b_im", "ssm_c_re",
           "ssm_c_im", "ssm_log_step", "ssm_d", "w_glu", "b_glu", "g_out_attn", "g_out_ssm", "w_out", "g_ffn",
           "w_ffn_gate", "w_ffn_up", "w_ffn_down")
SMALL = tuple(n for n in WEIGHTS if n not in BIG)
SMALL_LATE = tuple(n for n in SMALL if n != "g_mix")
VECTORS = ("g_mix", "q_gain", "k_gain", "ssm_d", "b_glu", "g_out_attn", "g_out_ssm", "g_ffn")


def kernel(x, g_mix, w_in, q_gain, k_gain, rpb, ssm_a_re, ssm_a_im, ssm_b_re, ssm_b_im, ssm_c_re, ssm_c_im, ssm_log_step, ssm_d, w_glu, b_glu, g_out_attn, g_out_ssm, w_out, g_ffn, w_ffn_gate, w_ffn_up, w_ffn_down, loss_target, m_g_mix, m_w_in, m_q_gain, m_k_gain, m_rpb, m_ssm_a_re, m_ssm_a_im, m_ssm_b_re, m_ssm_b_im, m_ssm_c_re, m_ssm_c_im, m_ssm_log_step, m_ssm_d, m_w_glu, m_b_glu, m_g_out_attn, m_g_out_ssm, m_w_out, m_g_ffn, m_w_ffn_gate, m_w_ffn_up, m_w_ffn_down, v_g_mix, v_w_in, v_q_gain, v_k_gain, v_rpb, v_ssm_a_re, v_ssm_a_im, v_ssm_b_re, v_ssm_b_im, v_ssm_c_re, v_ssm_c_im, v_ssm_log_step, v_ssm_d, v_w_glu, v_b_glu, v_g_out_attn, v_g_out_ssm, v_w_out, v_g_ffn, v_w_ffn_gate, v_w_ffn_up, v_w_ffn_down):
    wts = dict(g_mix=g_mix, w_in=w_in, q_gain=q_gain, k_gain=k_gain, rpb=rpb, ssm_a_re=ssm_a_re, ssm_a_im=ssm_a_im,
               ssm_b_re=ssm_b_re, ssm_b_im=ssm_b_im, ssm_c_re=ssm_c_re, ssm_c_im=ssm_c_im, ssm_log_step=ssm_log_step,
               ssm_d=ssm_d, w_glu=w_glu, b_glu=b_glu, g_out_attn=g_out_attn, g_out_ssm=g_out_ssm, w_out=w_out,
               g_ffn=g_ffn, w_ffn_gate=w_ffn_gate, w_ffn_up=w_ffn_up, w_ffn_down=w_ffn_down)
    mom = dict(g_mix=m_g_mix, w_in=m_w_in, q_gain=m_q_gain, k_gain=m_k_gain, rpb=m_rpb, ssm_a_re=m_ssm_a_re,
               ssm_a_im=m_ssm_a_im, ssm_b_re=m_ssm_b_re, ssm_b_im=m_ssm_b_im, ssm_c_re=m_ssm_c_re,
               ssm_c_im=m_ssm_c_im, ssm_log_step=m_ssm_log_step, ssm_d=m_ssm_d, w_glu=m_w_glu, b_glu=m_b_glu,
               g_out_attn=m_g_out_attn, g_out_ssm=m_g_out_ssm, w_out=m_w_out, g_ffn=m_g_ffn,
               w_ffn_gate=m_w_ffn_gate, w_ffn_up=m_w_ffn_up, w_ffn_down=m_w_ffn_down)
    var = dict(g_mix=v_g_mix, w_in=v_w_in, q_gain=v_q_gain, k_gain=v_k_gain, rpb=v_rpb, ssm_a_re=v_ssm_a_re,
               ssm_a_im=v_ssm_a_im, ssm_b_re=v_ssm_b_re, ssm_b_im=v_ssm_b_im, ssm_c_re=v_ssm_c_re,
               ssm_c_im=v_ssm_c_im, ssm_log_step=v_ssm_log_step, ssm_d=v_ssm_d, w_glu=v_w_glu, b_glu=v_b_glu,
               g_out_attn=v_g_out_attn, g_out_ssm=v_g_out_ssm, w_out=v_w_out, g_ffn=v_g_ffn,
               w_ffn_gate=v_w_ffn_gate, w_ffn_up=v_w_ffn_up, w_ffn_down=v_w_ffn_down)
    ix, iy, ic = _mesh_place()
    me = _slab(ix, iy, ic)

    shard = {n: wts[n][0] for n in BIG}
    shard_b = {n: shard[n].astype(BF16) for n in BIG}
    p = {n: (wts[n][0].reshape(1, -1) if n in VECTORS else wts[n][0]) for n in SMALL}

    loss, grad_x, small_gathered, g_g_mix, reduced = _local_step(x[0], loss_target[0], p, shard_b)
    loss = lax.psum(loss, ("x", "y", "c"))
    out = {}
    for n in BIG:
        slabs, from_sibling, from_chips = reduced[n]
        rows, cols = slabs.shape[1:]
        res = _adamw(shard[n].reshape(rows, cols), mom[n][0].reshape(rows, cols), var[n][0].reshape(rows, cols),
                     [(slabs, None), (from_sibling, 1), (from_chips, 3)], name=f"adamw_{n}", slab=me)
        out[n] = [r.reshape(wts[n].shape) for r in res]

    small_gathered["g_mix"] = _all_gather([g_g_mix], name="gather_g_mix")[0]
    for n in SMALL:
        res = _adamw(_as_rows(wts[n]), _as_rows(mom[n]), _as_rows(var[n]), [(small_gathered[n], N_DEV)],
                     name=f"adamw_{n}", slab=me, tr=1024)
        out[n] = [r.reshape(wts[n].shape) for r in res]

    return (loss, grad_x[None], *[out[n][0] for n in WEIGHTS], *[out[n][1] for n in WEIGHTS],
            *[out[n][2] for n in WEIGHTS], *[out[n][3] for n in WEIGHTS])
```
